```python
import math
import jax, jax.numpy as jnp
from jax import lax
import numpy as np

D_MODEL = 2048
BATCH = 8
SEQ = 4096
DEPTH = 1

D_MIX = D_MODEL
D_GLA = D_MIX // 2
D_SWA = D_MIX - D_GLA
GLA_HEADS = 4
GLA_DK = D_GLA // 2 // GLA_HEADS
GLA_DV = D_GLA // GLA_HEADS
GLA_KW = GLA_HEADS * GLA_DK
GATE_RANK = 16
GATE_TAU = 16.0
GLA_CHUNK = 64
SWA_HEAD_DIM = 64
SWA_Q_HEADS = D_SWA // SWA_HEAD_DIM
SWA_KV_HEADS = 2
SWA_GROUP = SWA_Q_HEADS // SWA_KV_HEADS
SWA_KVW = SWA_KV_HEADS * SWA_HEAD_DIM
WINDOW = 128
D_FF = 4 * D_MODEL
SPLITS = (GLA_KW, GLA_KW, D_GLA, D_GLA, GATE_RANK, D_SWA, SWA_KVW, SWA_KVW)
D_IN = sum(SPLITS)
ALPHA = (2 * DEPTH) ** 0.25
BETA = (8 * DEPTH) ** -0.25
LN_EPS = 1e-5
RMS_EPS = 1e-5

kernel_name = "hybrid_gla_swa_sink_deepnorm"


def split_points():
    pts, acc = [], 0
    for s in SPLITS[:-1]:
        acc += s
        pts.append(acc)
    return pts


def layer_norm(x, g, b):
    xf = x.astype(jnp.float32)
    mu = jnp.mean(xf, axis=-1, keepdims=True)
    var = jnp.mean(jnp.square(xf - mu), axis=-1, keepdims=True)
    y = (xf - mu) * lax.rsqrt(var + LN_EPS) * g.astype(jnp.float32) + b.astype(jnp.float32)
    return y.astype(x.dtype)


def gla_mixer(q, k, v, gk, g_out, norm_w):
    B, S, H, dk = q.shape
    dv = v.shape[-1]
    C = GLA_CHUNK
    nc = S // C

    def to_chunks(t):
        return t.astype(jnp.float32).reshape(B, nc, C, H, t.shape[-1]).transpose(1, 0, 3, 2, 4)

    qc = to_chunks(q) * (dk ** -0.5)
    kc, vc, gc = to_chunks(k), to_chunks(v), to_chunks(gk)
    bc = jnp.cumsum(gc, axis=3)
    causal = jnp.tril(jnp.ones((C, C), dtype=bool))

    def step(state, inp):
        qb, kb, vb, bb = inp
        o_inter = jnp.einsum('bhcd,bhde->bhce', qb * jnp.exp(bb), state)
        diff = bb[:, :, :, None, :] - bb[:, :, None, :, :]
        decay = jnp.exp(jnp.where(causal[:, :, None], diff, -jnp.inf))
        attn = jnp.einsum('bhijd,bhjd->bhij', qb[:, :, :, None, :] * decay, kb)
        o_intra = jnp.einsum('bhij,bhje->bhie', attn, vb)
        b_last = bb[:, :, -1:, :]
        k_dec = kb * jnp.exp(b_last - bb)
        new_state = state * jnp.exp(b_last[:, :, 0, :, None]) + jnp.einsum('bhcd,bhce->bhde', k_dec, vb)
        return new_state, o_inter + o_intra

    state0 = jnp.zeros((B, H, dk, dv), jnp.float32)
    _, oc = lax.scan(step, state0, (qc, kc, vc, bc))
    o = oc.transpose(1, 0, 3, 2, 4).reshape(B, S, H, dv)
    o = o * lax.rsqrt(jnp.mean(jnp.square(o), axis=-1, keepdims=True) + RMS_EPS)
    o = o * norm_w.astype(jnp.float32) * jax.nn.silu(g_out.astype(jnp.float32))
    return o.reshape(B, S, H * dv).astype(q.dtype)


def swa_mixer(q, k, v, sinks):
    B, S, _, dh = q.shape
    W = WINDOW
    nb = S // W
    qb = q.astype(jnp.float32).reshape(B, nb, W, SWA_KV_HEADS, SWA_GROUP, dh)

    def band(t):
        tb = t.astype(jnp.float32).reshape(B, nb, W, SWA_KV_HEADS, dh)
        prev = jnp.concatenate([jnp.zeros_like(tb[:, :1]), tb[:, :-1]], axis=1)
        return jnp.concatenate([prev, tb], axis=2)

    kb, vb = band(k), band(v)
    s = jnp.einsum('bnqhgd,bnkhd->bnhgqk', qb, kb) * (dh ** -0.5)
    qi = jnp.arange(W)[:, None]
    kj = jnp.arange(2 * W)[None, :]
    in_window = (kj > qi) & (kj <= qi + W)
    blk = jnp.arange(nb)[:, None, None]
    valid = in_window[None] & ((blk > 0) | (kj[None] >= W))
    s = jnp.where(valid[None, :, None, None], s, -jnp.inf)
    sink = sinks.astype(jnp.float32).reshape(SWA_KV_HEADS, SWA_GROUP)[None, None, :, :, None, None]
    m = jnp.maximum(jnp.max(s, axis=-1, keepdims=True), sink)
    p = jnp.exp(s - m)
    denom = jnp.sum(p, axis=-1, keepdims=True) + jnp.exp(sink - m)
    o = jnp.einsum('bnhgqk,bnkhd->bnqhgd', p / denom, vb)
    return o.reshape(B, S, SWA_Q_HEADS * dh).astype(q.dtype)


def _fwd_setup_inputs(seed: int = 0) -> dict:
    key = jax.random.key(seed)
    ks = jax.random.split(key, 12)
    f32 = jnp.float32
    x = jax.random.normal(ks[0], (BATCH, SEQ, D_MODEL), f32)
    col_scale = np.concatenate([np.full((s,), BETA if i in (2, 6, 7) else 1.0, np.float32)
                                for i, s in enumerate(SPLITS)])
    w_in = jax.random.normal(ks[1], (DEPTH, D_MODEL, D_IN), f32) * (D_MODEL ** -0.5) * jnp.asarray(col_scale)
    w_gk2 = jax.random.normal(ks[2], (DEPTH, GATE_RANK, GLA_KW), f32) * (GATE_RANK ** -0.5)
    b_gk = 0.1 * jax.random.normal(ks[3], (DEPTH, GLA_KW), f32)
    gla_norm_w = 1.0 + 0.02 * jax.random.normal(ks[4], (DEPTH, GLA_DV), f32)
    swa_sinks = 0.5 * jax.random.normal(ks[5], (DEPTH, SWA_Q_HEADS), f32)
    w_out = jax.random.normal(ks[6], (DEPTH, D_MIX, D_MODEL), f32) * (D_MIX ** -0.5) * BETA
    ln1_g = 1.0 + 0.02 * jax.random.normal(ks[7], (DEPTH, D_MODEL), f32)
    ln1_b = 0.02 * jax.random.normal(ks[8], (DEPTH, D_MODEL), f32)
    w_up = jax.random.normal(ks[9], (DEPTH, D_MODEL, D_FF), f32) * (D_MODEL ** -0.5)
    w_down = jax.random.normal(ks[10], (DEPTH, D_FF, D_MODEL), f32) * (D_FF ** -0.5) * BETA
    k2 = jax.random.split(ks[11], 2)
    ln2_g = 1.0 + 0.02 * jax.random.normal(k2[0], (DEPTH, D_MODEL), f32)
    ln2_b = 0.02 * jax.random.normal(k2[1], (DEPTH, D_MODEL), f32)
    return {"x": x, "w_in": w_in, "w_gk2": w_gk2, "b_gk": b_gk, "gla_norm_w": gla_norm_w,
            "swa_sinks": swa_sinks, "w_out": w_out, "ln1_g": ln1_g, "ln1_b": ln1_b,
            "w_up": w_up, "w_down": w_down, "ln2_g": ln2_g, "ln2_b": ln2_b}


def _fwd_reference(x, w_in, w_gk2, b_gk, gla_norm_w, swa_sinks, w_out, ln1_g, ln1_b,
              w_up, w_down, ln2_g, ln2_b):
    B, S, _ = x.shape
    pts = split_points()
    for l in range(DEPTH):
        proj = jnp.einsum('bsd,de->bse', x, w_in[l])
        q_g, k_g, v_g, g_g, gk_lo, q_s, k_s, v_s = jnp.split(proj, pts, axis=-1)
        gk = jax.nn.log_sigmoid((jnp.einsum('bsr,rk->bsk', gk_lo, w_gk2[l]) + b_gk[l]).astype(jnp.float32)) / GATE_TAU
        gla_out = gla_mixer(q_g.reshape(B, S, GLA_HEADS, GLA_DK),
                            k_g.reshape(B, S, GLA_HEADS, GLA_DK),
                            v_g.reshape(B, S, GLA_HEADS, GLA_DV),
                            gk.reshape(B, S, GLA_HEADS, GLA_DK),
                            g_g.reshape(B, S, GLA_HEADS, GLA_DV),
                            gla_norm_w[l])
        swa_out = swa_mixer(q_s.reshape(B, S, SWA_Q_HEADS, SWA_HEAD_DIM),
                            k_s.reshape(B, S, SWA_KV_HEADS, SWA_HEAD_DIM),
                            v_s.reshape(B, S, SWA_KV_HEADS, SWA_HEAD_DIM),
                            swa_sinks[l])
        mix = jnp.einsum('bse,ed->bsd', jnp.concatenate([gla_out, swa_out], axis=-1), w_out[l])
        x = layer_norm(ALPHA * x + mix, ln1_g[l], ln1_b[l])
        hdn = jnp.square(jax.nn.relu(jnp.einsum('bsd,df->bsf', x, w_up[l])))
        ff = jnp.einsum('bsf,fd->bsd', hdn, w_down[l])
        x = layer_norm(ALPHA * x + ff, ln2_g[l], ln2_b[l])
    return x


import jax as _jax
import jax.numpy as _jnp

TWIN_FORMAT = 'train_step'
FWD_PARAMS = ['x', 'w_in', 'w_gk2', 'b_gk', 'gla_norm_w', 'swa_sinks', 'w_out', 'ln1_g', 'ln1_b', 'w_up', 'w_down', 'ln2_g', 'ln2_b']
TWIN_WEIGHTS = ['w_in', 'w_gk2', 'b_gk', 'gla_norm_w', 'swa_sinks', 'w_out', 'ln1_g', 'ln1_b', 'w_up', 'w_down', 'ln2_g', 'ln2_b']
TWIN_DIFF_INPUT = 'x'
TWIN_INPUTS = ['x', 'w_in', 'w_gk2', 'b_gk', 'gla_norm_w', 'swa_sinks', 'w_out', 'ln1_g', 'ln1_b', 'w_up', 'w_down', 'ln2_g', 'ln2_b', 'loss_target', 'm_w_in', 'm_w_gk2', 'm_b_gk', 'm_gla_norm_w', 'm_swa_sinks', 'm_w_out', 'm_ln1_g', 'm_ln1_b', 'm_w_up', 'm_w_down', 'm_ln2_g', 'm_ln2_b', 'v_w_in', 'v_w_gk2', 'v_b_gk', 'v_gla_norm_w', 'v_swa_sinks', 'v_w_out', 'v_ln1_g', 'v_ln1_b', 'v_w_up', 'v_w_down', 'v_ln2_g', 'v_ln2_b']
TWIN_OUTPUTS = ['loss', 'grad_x', 'grad_w_in', 'grad_w_gk2', 'grad_b_gk', 'grad_gla_norm_w', 'grad_swa_sinks', 'grad_w_out', 'grad_ln1_g', 'grad_ln1_b', 'grad_w_up', 'grad_w_down', 'grad_ln2_g', 'grad_ln2_b', 'delta_w_in', 'delta_w_gk2', 'delta_b_gk', 'delta_gla_norm_w', 'delta_swa_sinks', 'delta_w_out', 'delta_ln1_g', 'delta_ln1_b', 'delta_w_up', 'delta_w_down', 'delta_ln2_g', 'delta_ln2_b', 'new_m_w_in', 'new_m_w_gk2', 'new_m_b_gk', 'new_m_gla_norm_w', 'new_m_swa_sinks', 'new_m_w_out', 'new_m_ln1_g', 'new_m_ln1_b', 'new_m_w_up', 'new_m_w_down', 'new_m_ln2_g', 'new_m_ln2_b', 'new_v_w_in', 'new_v_w_gk2', 'new_v_b_gk', 'new_v_gla_norm_w', 'new_v_swa_sinks', 'new_v_w_out', 'new_v_ln1_g', 'new_v_ln1_b', 'new_v_w_up', 'new_v_w_down', 'new_v_ln2_g', 'new_v_ln2_b']
TWIN_LEAF_KINDS = {'loss': 'loss', 'grad_x': 'grad_x', 'grad_w_in': 'grad_w', 'grad_w_gk2': 'grad_w', 'grad_b_gk': 'grad_w', 'grad_gla_norm_w': 'grad_w', 'grad_swa_sinks': 'grad_w', 'grad_w_out': 'grad_w', 'grad_ln1_g': 'grad_w', 'grad_ln1_b': 'grad_w', 'grad_w_up': 'grad_w', 'grad_w_down': 'grad_w', 'grad_ln2_g': 'grad_w', 'grad_ln2_b': 'grad_w', 'delta_w_in': 'delta_w', 'delta_w_gk2': 'delta_w', 'delta_b_gk': 'delta_w', 'delta_gla_norm_w': 'delta_w', 'delta_swa_sinks': 'delta_w', 'delta_w_out': 'delta_w', 'delta_ln1_g': 'delta_w', 'delta_ln1_b': 'delta_w', 'delta_w_up': 'delta_w', 'delta_w_down': 'delta_w', 'delta_ln2_g': 'delta_w', 'delta_ln2_b': 'delta_w', 'new_m_w_in': 'new_m', 'new_m_w_gk2': 'new_m', 'new_m_b_gk': 'new_m', 'new_m_gla_norm_w': 'new_m', 'new_m_swa_sinks': 'new_m', 'new_m_w_out': 'new_m', 'new_m_ln1_g': 'new_m', 'new_m_ln1_b': 'new_m', 'new_m_w_up': 'new_m', 'new_m_w_down': 'new_m', 'new_m_ln2_g': 'new_m', 'new_m_ln2_b': 'new_m', 'new_v_w_in': 'new_v', 'new_v_w_gk2': 'new_v', 'new_v_b_gk': 'new_v', 'new_v_gla_norm_w': 'new_v', 'new_v_swa_sinks': 'new_v', 'new_v_w_out': 'new_v', 'new_v_ln1_g': 'new_v', 'new_v_ln1_b': 'new_v', 'new_v_w_up': 'new_v', 'new_v_w_down': 'new_v', 'new_v_ln2_g': 'new_v', 'new_v_ln2_b': 'new_v'}


def _forward(args):
    return _fwd_reference(*[args[k] for k in FWD_PARAMS])


def _output_shape():
    def fwd():
        inp = _fwd_setup_inputs(0)
        return _fwd_reference(*[inp[k] for k in FWD_PARAMS])
    out = _jax.eval_shape(fwd)
    return out.shape, out.dtype

N_MICROBATCH = 1
ADAM_LR = 0.001
ADAM_B1 = 0.9
ADAM_B2 = 0.999
ADAM_EPS = 1e-08
ADAM_WD = 0.01
ADAM_STEP = 10
PER_EXAMPLE_BATCH_AXIS = {'x': 0, 'loss_target': 0}
SHARED_INPUTS = []
_WEIGHT_DTYPES = {'w_in': _jnp.float32, 'w_gk2': _jnp.float32, 'b_gk': _jnp.float32, 'gla_norm_w': _jnp.float32, 'swa_sinks': _jnp.float32, 'w_out': _jnp.float32, 'ln1_g': _jnp.float32, 'ln1_b': _jnp.float32, 'w_up': _jnp.float32, 'w_down': _jnp.float32, 'ln2_g': _jnp.float32, 'ln2_b': _jnp.float32}
MOMENT_SCALE = {'w_in': 3.236062e-02, 'w_gk2': 4.291945e-03, 'b_gk': 1.838774e-02, 'gla_norm_w': 5.353809e-02, 'swa_sinks': 5.651944e-03, 'w_out': 3.238622e-02, 'ln1_g': 4.248389e-01, 'ln1_b': 2.827398e-01, 'w_up': 2.673108e-02, 'w_down': 1.200549e-01, 'ln2_g': 1.601977e+01, 'ln2_b': 3.421556e+00}


def _to_microbatches(a, axis):
    t = _jnp.moveaxis(a, axis, 0)
    t = t.reshape((N_MICROBATCH, t.shape[0] // N_MICROBATCH) + t.shape[1:])
    return _jnp.moveaxis(t, 1, axis + 1)


def setup_inputs(seed: int = 0) -> dict:
    inp = _fwd_setup_inputs(seed)
    key = _jax.random.fold_in(_jax.random.key(seed), 7919)
    shape, _ = _output_shape()
    out = dict(inp)
    out["loss_target"] = _jax.random.normal(_jax.random.fold_in(key, 0), shape, _jnp.float32)
    for i, name in enumerate(TWIN_WEIGHTS):
        w = inp[name].astype(_jnp.float32)
        if MOMENT_SCALE is None:
            s = _jnp.sqrt(_jnp.mean(_jnp.square(w)) + 1e-30)
        else:
            s = MOMENT_SCALE[name]
        km, kv = _jax.random.split(_jax.random.fold_in(key, i + 1))
        out[name] = w
        out["m_" + name] = s * _jax.random.normal(km, w.shape, _jnp.float32)
        out["v_" + name] = (s * s) * _jax.random.uniform(kv, w.shape, _jnp.float32, 0.5, 1.5)
    if N_MICROBATCH > 1:
        for name, axis in PER_EXAMPLE_BATCH_AXIS.items():
            out[name] = _to_microbatches(out[name], axis)
    return {'x': out['x'], 'w_in': out['w_in'], 'w_gk2': out['w_gk2'], 'b_gk': out['b_gk'], 'gla_norm_w': out['gla_norm_w'], 'swa_sinks': out['swa_sinks'], 'w_out': out['w_out'], 'ln1_g': out['ln1_g'], 'ln1_b': out['ln1_b'], 'w_up': out['w_up'], 'w_down': out['w_down'], 'ln2_g': out['ln2_g'], 'ln2_b': out['ln2_b'], 'loss_target': out['loss_target'], 'm_w_in': out['m_w_in'], 'm_w_gk2': out['m_w_gk2'], 'm_b_gk': out['m_b_gk'], 'm_gla_norm_w': out['m_gla_norm_w'], 'm_swa_sinks': out['m_swa_sinks'], 'm_w_out': out['m_w_out'], 'm_ln1_g': out['m_ln1_g'], 'm_ln1_b': out['m_ln1_b'], 'm_w_up': out['m_w_up'], 'm_w_down': out['m_w_down'], 'm_ln2_g': out['m_ln2_g'], 'm_ln2_b': out['m_ln2_b'], 'v_w_in': out['v_w_in'], 'v_w_gk2': out['v_w_gk2'], 'v_b_gk': out['v_b_gk'], 'v_gla_norm_w': out['v_gla_norm_w'], 'v_swa_sinks': out['v_swa_sinks'], 'v_w_out': out['v_w_out'], 'v_ln1_g': out['v_ln1_g'], 'v_ln1_b': out['v_ln1_b'], 'v_w_up': out['v_w_up'], 'v_w_down': out['v_w_down'], 'v_ln2_g': out['v_ln2_g'], 'v_ln2_b': out['v_ln2_b']}


def _loss(weights, diff, rest, loss_target):
    with _jax.named_scope("forward"):
        args = {**rest, TWIN_DIFF_INPUT: diff, **{k: w.astype(_WEIGHT_DTYPES[k]) for k, w in weights.items()}}
        y = _forward(args)
    with _jax.named_scope("loss_head"):
        err = _jnp.square(y.astype(_jnp.float32) - loss_target)
        return 0.5 * _jnp.sum(_jnp.mean(err, axis=-1)) if err.ndim else 0.5 * err


def _adamw(w, g, m, v):
    m = ADAM_B1 * m + (1.0 - ADAM_B1) * g
    v = ADAM_B2 * v + (1.0 - ADAM_B2) * _jnp.square(g)
    m_hat = m / (1.0 - ADAM_B1 ** ADAM_STEP)
    v_hat = v / (1.0 - ADAM_B2 ** ADAM_STEP)
    delta = -ADAM_LR * (m_hat / (_jnp.sqrt(v_hat) + ADAM_EPS) + ADAM_WD * w)
    return delta, m, v


def reference(x, w_in, w_gk2, b_gk, gla_norm_w, swa_sinks, w_out, ln1_g, ln1_b, w_up, w_down, ln2_g, ln2_b, loss_target, m_w_in, m_w_gk2, m_b_gk, m_gla_norm_w, m_swa_sinks, m_w_out, m_ln1_g, m_ln1_b, m_w_up, m_w_down, m_ln2_g, m_ln2_b, v_w_in, v_w_gk2, v_b_gk, v_gla_norm_w, v_swa_sinks, v_w_out, v_ln1_g, v_ln1_b, v_w_up, v_w_down, v_ln2_g, v_ln2_b):
    given = dict(x=x, w_in=w_in, w_gk2=w_gk2, b_gk=b_gk, gla_norm_w=gla_norm_w, swa_sinks=swa_sinks, w_out=w_out, ln1_g=ln1_g, ln1_b=ln1_b, w_up=w_up, w_down=w_down, ln2_g=ln2_g, ln2_b=ln2_b, loss_target=loss_target, m_w_in=m_w_in, m_w_gk2=m_w_gk2, m_b_gk=m_b_gk, m_gla_norm_w=m_gla_norm_w, m_swa_sinks=m_swa_sinks, m_w_out=m_w_out, m_ln1_g=m_ln1_g, m_ln1_b=m_ln1_b, m_w_up=m_w_up, m_w_down=m_w_down, m_ln2_g=m_ln2_g, m_ln2_b=m_ln2_b, v_w_in=v_w_in, v_w_gk2=v_w_gk2, v_b_gk=v_b_gk, v_gla_norm_w=v_gla_norm_w, v_swa_sinks=v_swa_sinks, v_w_out=v_w_out, v_ln1_g=v_ln1_g, v_ln1_b=v_ln1_b, v_w_up=v_w_up, v_w_down=v_w_down, v_ln2_g=v_ln2_g, v_ln2_b=v_ln2_b)
    weights = {n: given[n] for n in TWIN_WEIGHTS}
    shared = {n: given[n] for n in SHARED_INPUTS}
    per_example = {n: given[n] for n in ['x']}
    grad_fn = _jax.value_and_grad(_loss, argnums=(0, 1))

    def one_microbatch(ex, loss_target):
        ex = dict(ex)
        diff = ex.pop(TWIN_DIFF_INPUT)
        return grad_fn(weights, diff, {**shared, **ex}, loss_target)

    if N_MICROBATCH == 1:
        loss, (grad_w, grad_x) = one_microbatch(per_example, given["loss_target"])
    else:
        def body(carry, xs):
            loss_sum, grad_sum = carry
            l_k, (gw_k, gx_k) = one_microbatch(xs[0], xs[1])
            with _jax.named_scope("update"):
                return (loss_sum + l_k, _jax.tree.map(_jnp.add, grad_sum, gw_k)), gx_k

        init = (_jnp.zeros((), _jnp.float32), _jax.tree.map(_jnp.zeros_like, weights))
        (loss, grad_w), grad_x = _jax.lax.scan(body, init, (per_example, given["loss_target"]))
    with _jax.named_scope("update"):
        delta_w, new_m, new_v = {}, {}, {}
        for n in TWIN_WEIGHTS:
            delta_w[n], new_m[n], new_v[n] = _adamw(weights[n], grad_w[n], given["m_" + n], given["v_" + n])
    return (loss, grad_x, *[grad_w[n] for n in TWIN_WEIGHTS], *[delta_w[n] for n in TWIN_WEIGHTS],
            *[new_m[n] for n in TWIN_WEIGHTS], *[new_v[n] for n in TWIN_WEIGHTS])
```

```python
import functools

import jax
import jax.numpy as jnp
from jax import lax
from jax.experimental import pallas as pl
from jax.experimental.pallas import tpu as pltpu

F32 = jnp.float32
BF16 = jnp.bfloat16

N_DEV = 8
D_MODEL = 2048
D_FF = 8192
GLA_HEADS = 4
GLA_DK = 128
GLA_DV = 256
GLA_KW = 512
D_GLA = 1024
GATE_RANK = 16
GATE_TAU = 16.0
SWA_HEADS = 16
SWA_DH = 64
WINDOW = 128
D_SWA = 1024
D_IN = 4368
ALPHA = 2.0 ** 0.25
LN_EPS = 1e-5
RMS_EPS = 1e-5
ADAM_LR = 0.001
ADAM_B1 = 0.9
ADAM_B2 = 0.999
ADAM_EPS = 1e-08
ADAM_WD = 0.01
ADAM_STEP = 10

C_QG, C_KG, C_VG, C_GG, C_QS, C_KS, C_VS, C_LO = 0, 512, 1024, 2048, 3072, 4096, 4224, 4352
D_INP = 4480
LANES = 128
GC = 16
NEG = -1e30

NN = ((1,), (0,))
NT = ((1,), (1,))
TN = ((0,), (0,))

VMEM_LIMIT = 52 * 1024 * 1024
MESH = pl.DeviceIdType.MESH


def _dot(a, b, dn, precision=None):
    return lax.dot_general(a, b, (dn, ((), ())), preferred_element_type=F32, precision=precision)


def _bf16_round(v):
    return v.astype(BF16).astype(F32)


def _cparams(dims):
    return pltpu.CompilerParams(dimension_semantics=dims, vmem_limit_bytes=VMEM_LIMIT)


def _to_padded_cols(w):
    pad = jnp.zeros(w.shape[:-1] + (D_INP - D_IN,), w.dtype)
    return jnp.concatenate([w[..., :3072], w[..., 3088:], w[..., 3072:3088], pad], axis=-1)


def _from_padded_cols(g):
    return jnp.concatenate([g[..., :3072], g[..., C_LO:C_LO + GATE_RANK], g[..., 3072:C_LO]], axis=-1)


def _matmul(name, a, b, dn, grid, a_spec, b_spec, out_shape, out_specs, acc_shape, *,
            extra=(), extra_specs=(), epilogue=None, dims=("parallel", "parallel", "arbitrary")):
    nk = grid[2]
    n_extra = len(extra)

    def body(a_ref, b_ref, *rest):
        extra_refs = rest[:n_extra]
        out_refs = rest[n_extra:-1]
        acc_ref = rest[-1]
        part = _dot(a_ref[...].astype(BF16), b_ref[...].astype(BF16), dn)

        def finish():
            if epilogue is None:
                out_refs[0][...] = acc_ref[...].astype(out_refs[0].dtype)
            else:
                epilogue(acc_ref, extra_refs, out_refs)

        if nk == 1:
            acc_ref[...] = part
            finish()
        else:
            k = pl.program_id(2)

            @pl.when(k == 0)
            def _():
                acc_ref[...] = part

            @pl.when(k > 0)
            def _():
                acc_ref[...] += part

            @pl.when(k == nk - 1)
            def _():
                finish()

    return pl.pallas_call(
        body, name=name, grid=grid,
        in_specs=[a_spec, b_spec, *extra_specs],
        out_specs=out_specs, out_shape=out_shape,
        scratch_shapes=[pltpu.VMEM(acc_shape, F32)],
        compiler_params=_cparams(dims),
    )(a, b, *extra)


def _row_chunks(rows, step=128):
    step = min(step, rows)
    return [(r, step) for r in range(0, rows, step)]


def _ln_stats(r):
    mu = jnp.mean(r, axis=-1, keepdims=True)
    xc = r - mu
    var = jnp.mean(xc * xc, axis=-1, keepdims=True)
    rstd = lax.rsqrt(var + LN_EPS)
    return xc * rstd, rstd


def _ln_bwd(dy_g, xhat, rstd):
    m1 = jnp.mean(dy_g, axis=-1, keepdims=True)
    m2 = jnp.mean(dy_g * xhat, axis=-1, keepdims=True)
    return rstd * (dy_g - m1 - xhat * m2)


def _proj(xb, w_in_p):
    t = xb.shape[0]
    tm, tn = min(1024, t), 640
    return _matmul(
        "proj", xb, w_in_p, NN, (t // tm, D_INP // tn, 1),
        pl.BlockSpec((tm, D_MODEL), lambda i, j, k: (i, 0)),
        pl.BlockSpec((D_MODEL, tn), lambda i, j, k: (0, j)),
        jax.ShapeDtypeStruct((t, D_INP), F32),
        pl.BlockSpec((tm, tn), lambda i, j, k: (i, j)),
        (tm, tn))


def _gate_fwd(proj, w2p, b_gk, tri):
    t = proj.shape[0]
    r = tri.shape[0]

    def body(lo_ref, w_ref, b_ref, tri_ref, bloc_ref, dgdz_ref):
        z = _dot(lo_ref[...].astype(BF16), w_ref[...], NN) + b_ref[...]
        e = jnp.exp(-jnp.abs(z))
        gk = (jnp.minimum(z, 0.0) - jnp.log1p(e)) * (1.0 / GATE_TAU)
        inv = 1.0 / (1.0 + e)
        dgdz_ref[...] = jnp.where(z >= 0.0, e * inv, inv) * (1.0 / GATE_TAU)
        bloc_ref[...] = _dot(tri_ref[...], gk, NN, precision=lax.Precision.HIGHEST)

    return pl.pallas_call(
        body, name="gate_fwd", grid=(t // r,),
        in_specs=[pl.BlockSpec((r, LANES), lambda i: (i, C_LO // LANES)),
                  pl.BlockSpec((LANES, GLA_KW), lambda i: (0, 0)),
                  pl.BlockSpec((1, GLA_KW), lambda i: (0, 0)),
                  pl.BlockSpec((r, r), lambda i: (0, 0))],
        out_specs=[pl.BlockSpec((r, GLA_KW), lambda i: (i, 0)),
                   pl.BlockSpec((r, GLA_KW), lambda i: (i, 0))],
        out_shape=[jax.ShapeDtypeStruct((t, GLA_KW), F32), jax.ShapeDtypeStruct((t, GLA_KW), F32)],
        compiler_params=_cparams(("parallel",)),
    )(proj, w2p, b_gk, tri)


def _gla_fwd(proj, bloc):
    t = proj.shape[0]
    r = min(256, t)
    ncb = r // GC
    scale = GLA_DK ** -0.5

    def body(q_ref, k_ref, v_ref, b_ref, o_ref, st_ref, s_scr):
        @pl.when(pl.program_id(1) == 0)
        def _():
            s_scr[...] = jnp.zeros_like(s_scr)

        rows = lax.broadcasted_iota(jnp.int32, (GC, 1), 0)

        def chunk(c, carry):
            r0 = pl.multiple_of(c * GC, GC)
            q = q_ref[pl.ds(r0, GC), :] * scale
            k = k_ref[pl.ds(r0, GC), :]
            v = v_ref[pl.ds(r0, GC), :]
            b = b_ref[pl.ds(r0, GC), :]
            st = s_scr[...]
            st_ref[c] = st
            bl = b[GC - 1:GC, :]
            kr = _bf16_round(k)
            vr = _bf16_round(v)
            o = _dot((q * jnp.exp(b)).astype(BF16), st.astype(BF16), NT)
            for j in range(GC):
                w = jnp.exp(jnp.where(rows >= j, b - b[j:j + 1, :], NEG))
                a = jnp.sum(_bf16_round(q * w) * kr[j:j + 1, :], axis=-1, keepdims=True)
                o = o + _bf16_round(a) * vr[j:j + 1, :]
            o_ref[pl.ds(r0, GC), :] = o
            kd = k * jnp.exp(bl - b)
            s_scr[...] = st * jnp.exp(bl) + _dot(v.astype(BF16), kd.astype(BF16), TN)
            return carry

        lax.fori_loop(0, ncb, chunk, 0)

    return pl.pallas_call(
        body, name="gla_fwd", grid=(GLA_HEADS, t // r),
        in_specs=[pl.BlockSpec((r, GLA_DK), lambda h, i: (i, C_QG // GLA_DK + h)),
                  pl.BlockSpec((r, GLA_DK), lambda h, i: (i, C_KG // GLA_DK + h)),
                  pl.BlockSpec((r, GLA_DV), lambda h, i: (i, C_VG // GLA_DV + h)),
                  pl.BlockSpec((r, GLA_DK), lambda h, i: (i, h))],
        out_specs=[pl.BlockSpec((r, GLA_DV), lambda h, i: (i, h)),
                   pl.BlockSpec((ncb, None, GLA_DV, GLA_DK), lambda h, i: (i, h, 0, 0))],
        out_shape=[jax.ShapeDtypeStruct((t, D_GLA), F32),
                   jax.ShapeDtypeStruct((t // GC, GLA_HEADS, GLA_DV, GLA_DK), F32)],
        scratch_shapes=[pltpu.VMEM((GLA_DV, GLA_DK), F32)],
        compiler_params=_cparams(("parallel", "arbitrary")),
    )(proj, proj, proj, bloc)


def _gla_norm_fwd(o_raw, proj, norm_w):
    t = o_raw.shape[0]
    r = min(512, t)

    def body(o_ref, g_ref, w_ref, out_ref):
        w = w_ref[...]
        for h in range(GLA_HEADS):
            sl = slice(h * GLA_DV, (h + 1) * GLA_DV)
            o = o_ref[:, sl]
            g = g_ref[:, sl]
            on = o * lax.rsqrt(jnp.mean(o * o, axis=-1, keepdims=True) + RMS_EPS)
            out_ref[:, sl] = (on * w * (g * jax.nn.sigmoid(g))).astype(BF16)

    return pl.pallas_call(
        body, name="gla_norm_fwd", grid=(t // r,),
        in_specs=[pl.BlockSpec((r, D_GLA), lambda i: (i, 0)),
                  pl.BlockSpec((r, D_GLA), lambda i: (i, C_GG // D_GLA)),
                  pl.BlockSpec((1, GLA_DV), lambda i: (0, 0))],
        out_specs=pl.BlockSpec((r, D_GLA), lambda i: (i, 0)),
        out_shape=jax.ShapeDtypeStruct((t, D_GLA), BF16),
        compiler_params=_cparams(("parallel",)),
    )(o_raw, proj, norm_w)


def _swa_masks(i):
    qi = lax.broadcasted_iota(jnp.int32, (WINDOW, 1), 0)
    kj = lax.broadcasted_iota(jnp.int32, (1, 2 * WINDOW), 1)
    valid = (kj > qi) & (kj <= qi + WINDOW) & ((i > 0) | (kj >= WINDOW))
    lo = lax.broadcasted_iota(jnp.int32, (1, LANES), 1) < SWA_DH
    return valid, lo


def _dup_half(x, lo, kv):
    xr = pltpu.roll(x, SWA_DH, axis=1)
    return jnp.where(lo, x, xr) if kv == 0 else jnp.where(lo, xr, x)


def _swa_probs(qm, kdup, valid, sink):
    s = _dot(qm, kdup, NT) * (SWA_DH ** -0.5)
    s = jnp.where(valid, s, NEG)
    m = jnp.maximum(jnp.max(s, axis=-1, keepdims=True), sink)
    p = jnp.exp(s - m)
    es = jnp.exp(sink - m)
    inv = 1.0 / (jnp.sum(p, axis=-1, keepdims=True) + es)
    return p * inv, es * inv


def _swa_fwd(proj, sinks):
    t = proj.shape[0]

    def body(sink_ref, q_ref, kp_ref, kc_ref, vp_ref, vc_ref, o_ref):
        i = pl.program_id(0)
        valid, lo = _swa_masks(i)
        kb = jnp.concatenate([kp_ref[...], kc_ref[...]], axis=0)
        vb = jnp.concatenate([vp_ref[...], vc_ref[...]], axis=0)
        for kv in range(2):
            kdup = _dup_half(kb, lo, kv).astype(BF16)
            vdup = _dup_half(vb, lo, kv).astype(BF16)
            for p in range(4):
                sl = slice(LANES * (4 * kv + p), LANES * (4 * kv + p + 1))
                qp = q_ref[:, sl]
                halves = []
                for e in range(2):
                    qm = jnp.where(lo if e == 0 else ~lo, qp, 0.0).astype(BF16)
                    pn, _ = _swa_probs(qm, kdup, valid, sink_ref[8 * kv + 2 * p + e])
                    halves.append(_dot(pn.astype(BF16), vdup, NN))
                o_ref[:, sl] = jnp.where(lo, halves[0], halves[1]).astype(BF16)

    kvspec = lambda col, prev: pl.BlockSpec(
        (WINDOW, LANES), (lambda i: (jnp.maximum(i - 1, 0), col)) if prev else (lambda i: (i, col)))
    return pl.pallas_call(
        body, name="swa_fwd", grid=(t // WINDOW,),
        in_specs=[pl.BlockSpec(memory_space=pltpu.SMEM),
                  pl.BlockSpec((WINDOW, D_SWA), lambda i: (i, C_QS // D_SWA)),
                  kvspec(C_KS // LANES, True), kvspec(C_KS // LANES, False),
                  kvspec(C_VS // LANES, True), kvspec(C_VS // LANES, False)],
        out_specs=pl.BlockSpec((WINDOW, D_SWA), lambda i: (i, 0)),
        out_shape=jax.ShapeDtypeStruct((t, D_SWA), BF16),
        compiler_params=_cparams(("parallel",)),
    )(sinks, proj, proj, proj, proj, proj)


def _mix_ln1(cat, w_out, x, ln_g, ln_b):
    t = cat.shape[0]
    tm, tk = min(512, t), 512

    def epilogue(acc_ref, extra, outs):
        x_ref, g_ref, b_ref = extra
        r1_ref, h1_ref, h1b_ref = outs
        for r0, n in _row_chunks(tm):
            rs = slice(r0, r0 + n)
            r1 = ALPHA * x_ref[rs, :] + acc_ref[rs, :]
            xhat, _ = _ln_stats(r1)
            h = xhat * g_ref[...] + b_ref[...]
            r1_ref[rs, :] = r1
            h1_ref[rs, :] = h
            h1b_ref[rs, :] = h.astype(BF16)

    row = pl.BlockSpec((tm, D_MODEL), lambda i, j, k: (i, 0))
    vec = pl.BlockSpec((1, D_MODEL), lambda i, j, k: (0, 0))
    return _matmul(
        "mix_ln1", cat, w_out, NN, (t // tm, 1, D_MODEL // tk),
        pl.BlockSpec((tm, tk), lambda i, j, k: (i, k)),
        pl.BlockSpec((tk, D_MODEL), lambda i, j, k: (k, 0)),
        [jax.ShapeDtypeStruct((t, D_MODEL), F32), jax.ShapeDtypeStruct((t, D_MODEL), F32),
         jax.ShapeDtypeStruct((t, D_MODEL), BF16)],
        [row, row, row], (tm, D_MODEL),
        extra=(x, ln_g, ln_b), extra_specs=(row, vec, vec), epilogue=epilogue)


def _mlp_up(h1b, w_up):
    t = h1b.shape[0]
    tm, tn = min(1024, t), 1024

    def epilogue(acc_ref, extra, outs):
        a_ref, hdn_ref = outs
        for r0, n in _row_chunks(tm, 256):
            rs = slice(r0, r0 + n)
            a = jnp.maximum(acc_ref[rs, :], 0.0)
            a_ref[rs, :] = a.astype(BF16)
            hdn_ref[rs, :] = (a * a).astype(BF16)

    out = pl.BlockSpec((tm, tn), lambda i, j, k: (i, j))
    return _matmul(
        "mlp_up", h1b, w_up, NN, (t // tm, D_FF // tn, 1),
        pl.BlockSpec((tm, D_MODEL), lambda i, j, k: (i, 0)),
        pl.BlockSpec((None, D_MODEL, tn), lambda i, j, k: (j, 0, 0)),
        [jax.ShapeDtypeStruct((t, D_FF), BF16), jax.ShapeDtypeStruct((t, D_FF), BF16)],
        [out, out], (tm, tn), epilogue=epilogue)


def _mlp_down_loss(hdn, w_down, h1, target, ln_g, ln_b):
    t = hdn.shape[0]
    tm, tk = min(512, t), 512

    def epilogue(acc_ref, extra, outs):
        h1_ref, t_ref, g_ref, b_ref = extra
        dr2_ref, dr2b_ref, gg_ref, gb_ref, loss_ref = outs

        @pl.when(pl.program_id(0) == 0)
        def _():
            gg_ref[...] = jnp.zeros_like(gg_ref)
            gb_ref[...] = jnp.zeros_like(gb_ref)
            loss_ref[...] = jnp.zeros_like(loss_ref)

        for r0, n in _row_chunks(tm):
            rs = slice(r0, r0 + n)
            xhat, rstd = _ln_stats(ALPHA * h1_ref[rs, :] + acc_ref[rs, :])
            err = xhat * g_ref[...] + b_ref[...] - t_ref[rs, :]
            loss_ref[...] += 0.5 * jnp.sum(jnp.mean(err * err, axis=-1, keepdims=True))
            dy = err * (1.0 / D_MODEL)
            gg_ref[...] += jnp.sum(dy * xhat, axis=0, keepdims=True)
            gb_ref[...] += jnp.sum(dy, axis=0, keepdims=True)
            dr2 = _ln_bwd(dy * g_ref[...], xhat, rstd)
            dr2_ref[rs, :] = dr2
            dr2b_ref[rs, :] = dr2.astype(BF16)

    row = pl.BlockSpec((tm, D_MODEL), lambda i, j, k: (i, 0))
    vec = pl.BlockSpec((1, D_MODEL), lambda i, j, k: (0, 0))
    return _matmul(
        "mlp_down_loss", hdn, w_down, NN, (t // tm, 1, D_FF // tk),
        pl.BlockSpec((tm, tk), lambda i, j, k: (i, k)),
        pl.BlockSpec((tk, D_MODEL), lambda i, j, k: (k, 0)),
        [jax.ShapeDtypeStruct((t, D_MODEL), F32), jax.ShapeDtypeStruct((t, D_MODEL), BF16),
         jax.ShapeDtypeStruct((1, D_MODEL), F32), jax.ShapeDtypeStruct((1, D_MODEL), F32),
         jax.ShapeDtypeStruct((1, LANES), F32)],
        [row, row, vec, vec, pl.BlockSpec((1, LANES), lambda i, j, k: (0, 0))],
        (tm, D_MODEL),
        extra=(h1, target, ln_g, ln_b), extra_specs=(row, row, vec, vec), epilogue=epilogue,
        dims=("arbitrary", "arbitrary", "arbitrary"))


def _mlp_down_bwd(dr2b, w_down, a_act):
    t = dr2b.shape[0]
    tm, tn = min(1024, t), 1024

    def epilogue(acc_ref, extra, outs):
        (a_ref,) = extra
        for r0, n in _row_chunks(tm, 256):
            rs = slice(r0, r0 + n)
            outs[0][rs, :] = (acc_ref[rs, :] * (2.0 * a_ref[rs, :].astype(F32))).astype(BF16)

    blk = pl.BlockSpec((tm, tn), lambda i, j, k: (i, j))
    return _matmul(
        "mlp_down_bwd", dr2b, w_down, NT, (t // tm, D_FF // tn, 1),
        pl.BlockSpec((tm, D_MODEL), lambda i, j, k: (i, 0)),
        pl.BlockSpec((tn, D_MODEL), lambda i, j, k: (j, 0)),
        jax.ShapeDtypeStruct((t, D_FF), BF16), blk, (tm, tn),
        extra=(a_act,), extra_specs=(blk,), epilogue=epilogue)


def _grad_w_down(hdn, dr2b):
    t = hdn.shape[0]
    tm, tn, tk = 1024, 2048, min(512, t)
    return _matmul(
        "grad_w_down", hdn, dr2b, TN, (D_FF // tm, D_MODEL // tn, t // tk),
        pl.BlockSpec((tk, tm), lambda i, j, k: (k, i)),
        pl.BlockSpec((tk, tn), lambda i, j, k: (k, j)),
        jax.ShapeDtypeStruct((D_FF, D_MODEL), F32),
        pl.BlockSpec((tm, tn), lambda i, j, k: (i, j)), (tm, tn))


def _grad_w_up(h1b, du):
    t = h1b.shape[0]
    tm, tn, tk = 1024, 1024, min(512, t)
    return _matmul(
        "grad_w_up", h1b, du, TN, (N_DEV, D_MODEL // tm, t // tk),
        pl.BlockSpec((tk, tm), lambda i, j, k: (k, j)),
        pl.BlockSpec((tk, tn), lambda i, j, k: (k, i)),
        jax.ShapeDtypeStruct((N_DEV, D_MODEL, D_FF // N_DEV), F32),
        pl.BlockSpec((None, tm, tn), lambda i, j, k: (i, j, 0)), (tm, tn))


def _mlp_up_bwd_ln1(du, w_up, dr2, r1, ln_g):
    t = du.shape[0]
    tm, tk = min(256, t), D_FF // N_DEV

    def epilogue(acc_ref, extra, outs):
        dr2_ref, r1_ref, g_ref = extra
        dr1_ref, gg_ref, gb_ref = outs

        @pl.when(pl.program_id(0) == 0)
        def _():
            gg_ref[...] = jnp.zeros_like(gg_ref)
            gb_ref[...] = jnp.zeros_like(gb_ref)

        for r0, n in _row_chunks(tm):
            rs = slice(r0, r0 + n)
            dh1 = ALPHA * dr2_ref[rs, :] + acc_ref[rs, :]
            xhat, rstd = _ln_stats(r1_ref[rs, :])
            gg_ref[...] += jnp.sum(dh1 * xhat, axis=0, keepdims=True)
            gb_ref[...] += jnp.sum(dh1, axis=0, keepdims=True)
            dr1_ref[rs, :] = _ln_bwd(dh1 * g_ref[...], xhat, rstd)

    row = pl.BlockSpec((tm, D_MODEL), lambda i, j, k: (i, 0))
    vec = pl.BlockSpec((1, D_MODEL), lambda i, j, k: (0, 0))
    return _matmul(
        "mlp_up_bwd_ln1", du, w_up, NT, (t // tm, 1, N_DEV),
        pl.BlockSpec((tm, tk), lambda i, j, k: (i, k)),
        pl.BlockSpec((None, D_MODEL, tk), lambda i, j, k: (k, 0, 0)),
        [jax.ShapeDtypeStruct((t, D_MODEL), F32), jax.ShapeDtypeStruct((1, D_MODEL), F32),
         jax.ShapeDtypeStruct((1, D_MODEL), F32)],
        [row, vec, vec], (tm, D_MODEL),
        extra=(dr2, r1, ln_g), extra_specs=(row, row, vec), epilogue=epilogue,
        dims=("arbitrary", "arbitrary", "arbitrary"))


def _dcat(dr1, w_out):
    t = dr1.shape[0]
    tm, tn = min(512, t), 1024
    return _matmul(
        "dcat", dr1, w_out, NT, (t // tm, D_MODEL // tn, 1),
        pl.BlockSpec((tm, D_MODEL), lambda i, j, k: (i, 0)),
        pl.BlockSpec((tn, D_MODEL), lambda i, j, k: (j, 0)),
        jax.ShapeDtypeStruct((t, D_MODEL), F32),
        pl.BlockSpec((tm, tn), lambda i, j, k: (i, j)), (tm, tn))


def _grad_w_out(cat, dr1):
    t = cat.shape[0]
    tm, tn, tk = 1024, 1024, min(512, t)
    return _matmul(
        "grad_w_out", cat, dr1, TN, (D_MODEL // tm, D_MODEL // tn, t // tk),
        pl.BlockSpec((tk, tm), lambda i, j, k: (k, i)),
        pl.BlockSpec((tk, tn), lambda i, j, k: (k, j)),
        jax.ShapeDtypeStruct((D_MODEL, D_MODEL), F32),
        pl.BlockSpec((tm, tn), lambda i, j, k: (i, j)), (tm, tn))


def _gla_norm_bwd(dcat, o_raw, proj, norm_w):
    t = o_raw.shape[0]
    r = min(512, t)

    def body(d_ref, o_ref, g_ref, w_ref, do_ref, dg_ref, dw_ref):
        @pl.when(pl.program_id(0) == 0)
        def _():
            dw_ref[...] = jnp.zeros_like(dw_ref)

        w = w_ref[...]
        dw = jnp.zeros((1, GLA_DV), F32)
        for h in range(GLA_HEADS):
            sl = slice(h * GLA_DV, (h + 1) * GLA_DV)
            o = o_ref[:, sl]
            g = g_ref[:, sl]
            d = d_ref[:, sl]
            rr = lax.rsqrt(jnp.mean(o * o, axis=-1, keepdims=True) + RMS_EPS)
            on = o * rr
            sg = jax.nn.sigmoid(g)
            sil = g * sg
            dg_ref[:, sl] = (d * on * w * (sg * (1.0 + g * (1.0 - sg)))).astype(BF16)
            dw = dw + jnp.sum(d * on * sil, axis=0, keepdims=True)
            don = d * w * sil
            do_ref[:, sl] = rr * (don - on * jnp.mean(don * on, axis=-1, keepdims=True))
        dw_ref[...] += dw

    return pl.pallas_call(
        body, name="gla_norm_bwd", grid=(t // r,),
        in_specs=[pl.BlockSpec((r, D_GLA), lambda i: (i, 0)),
                  pl.BlockSpec((r, D_GLA), lambda i: (i, 0)),
                  pl.BlockSpec((r, D_GLA), lambda i: (i, C_GG // D_GLA)),
                  pl.BlockSpec((1, GLA_DV), lambda i: (0, 0))],
        out_specs=[pl.BlockSpec((r, D_GLA), lambda i: (i, 0)),
                   pl.BlockSpec((r, D_GLA), lambda i: (i, 0)),
                   pl.BlockSpec((1, GLA_DV), lambda i: (0, 0))],
        out_shape=[jax.ShapeDtypeStruct((t, D_GLA), F32), jax.ShapeDtypeStruct((t, D_GLA), BF16),
                   jax.ShapeDtypeStruct((1, GLA_DV), F32)],
        compiler_params=_cparams(("arbitrary",)),
    )(dcat, o_raw, proj, norm_w)


def _gla_bwd(proj, bloc, do_raw, states, triu):
    t = proj.shape[0]
    r = min(256, t)
    ncb = r // GC
    nb = t // r
    scale = GLA_DK ** -0.5

    def body(q_ref, k_ref, v_ref, b_ref, do_ref, st_ref, u_ref, dq_ref, dk_ref, dv_ref, dg_ref, ds_scr):
        @pl.when(pl.program_id(1) == 0)
        def _():
            ds_scr[...] = jnp.zeros_like(ds_scr)

        rows = lax.broadcasted_iota(jnp.int32, (GC, 1), 0)

        def chunk(cc, carry):
            c = ncb - 1 - cc
            r0 = pl.multiple_of(c * GC, GC)
            q = q_ref[pl.ds(r0, GC), :] * scale
            k = k_ref[pl.ds(r0, GC), :]
            v = v_ref[pl.ds(r0, GC), :]
            b = b_ref[pl.ds(r0, GC), :]
            do = do_ref[pl.ds(r0, GC), :]
            st = st_ref[c]
            dsn = ds_scr[...]
            bl = b[GC - 1:GC, :]
            eb = jnp.exp(b)
            ekl = jnp.exp(bl - b)
            ebl = jnp.exp(bl)
            qh = q * eb
            kd = k * ekl
            dob = do.astype(BF16)
            dsb = dsn.astype(BF16)
            dqh = _dot(dob, st.astype(BF16), NN)
            dkd = _dot(v.astype(BF16), dsb, NN)
            dv = _dot(kd.astype(BF16), dsb, NT)
            dq_i = jnp.zeros((GC, GLA_DK), F32)
            dk_i = jnp.zeros((GC, GLA_DK), F32)
            dk_x = jnp.zeros((GC, GLA_DK), F32)
            kr = _bf16_round(k)
            vr = _bf16_round(v)
            dor = _bf16_round(do)
            for j in range(GC):
                w = jnp.exp(jnp.where(rows >= j, b - b[j:j + 1, :], NEG))
                qw = q * w
                qwr = _bf16_round(qw)
                a = _bf16_round(jnp.sum(qwr * kr[j:j + 1, :], axis=-1, keepdims=True))
                da = jnp.sum(dor * vr[j:j + 1, :], axis=-1, keepdims=True)
                dq_i = dq_i + da * (w * k[j:j + 1, :])
                dk_x = jnp.where(rows == j, jnp.sum(da * qw, axis=0, keepdims=True), dk_x)
                dk_i = jnp.where(rows == j, jnp.sum(_bf16_round(da) * qwr, axis=0, keepdims=True), dk_i)
                dv = dv + jnp.where(rows == j, jnp.sum(a * dor, axis=0, keepdims=True), 0.0)
            dqs = dqh * eb + dq_i
            dk = dkd * ekl + dk_i
            db_last = jnp.sum(dkd * kd, axis=0, keepdims=True) + ebl * jnp.sum(dsn * st, axis=0, keepdims=True)
            db = q * dqs - k * (dkd * ekl + dk_x) + jnp.where(rows == GC - 1, db_last, 0.0)
            dq_ref[pl.ds(r0, GC), :] = (dqs * scale).astype(BF16)
            dk_ref[pl.ds(r0, GC), :] = dk.astype(BF16)
            dv_ref[pl.ds(r0, GC), :] = dv.astype(BF16)
            dg_ref[pl.ds(r0, GC), :] = _dot(u_ref[...], db, NN, precision=lax.Precision.HIGHEST)
            ds_scr[...] = dsn * ebl + _dot(dob, qh.astype(BF16), TN)
            return carry

        lax.fori_loop(0, ncb, chunk, 0)

    rev = lambda i: nb - 1 - i
    return pl.pallas_call(
        body, name="gla_bwd", grid=(GLA_HEADS, nb),
        in_specs=[pl.BlockSpec((r, GLA_DK), lambda h, i: (rev(i), C_QG // GLA_DK + h)),
                  pl.BlockSpec((r, GLA_DK), lambda h, i: (rev(i), C_KG // GLA_DK + h)),
                  pl.BlockSpec((r, GLA_DV), lambda h, i: (rev(i), C_VG // GLA_DV + h)),
                  pl.BlockSpec((r, GLA_DK), lambda h, i: (rev(i), h)),
                  pl.BlockSpec((r, GLA_DV), lambda h, i: (rev(i), h)),
                  pl.BlockSpec((ncb, None, GLA_DV, GLA_DK), lambda h, i: (rev(i), h, 0, 0)),
                  pl.BlockSpec((GC, GC), lambda h, i: (0, 0))],
        out_specs=[pl.BlockSpec((r, GLA_DK), lambda h, i: (rev(i), h)),
                   pl.BlockSpec((r, GLA_DK), lambda h, i: (rev(i), h)),
                   pl.BlockSpec((r, GLA_DV), lambda h, i: (rev(i), h)),
                   pl.BlockSpec((r, GLA_DK), lambda h, i: (rev(i), h))],
        out_shape=[jax.ShapeDtypeStruct((t, GLA_KW), BF16), jax.ShapeDtypeStruct((t, GLA_KW), BF16),
                   jax.ShapeDtypeStruct((t, D_GLA), BF16), jax.ShapeDtypeStruct((t, GLA_KW), F32)],
        scratch_shapes=[pltpu.VMEM((GLA_DV, GLA_DK), F32)],
        compiler_params=_cparams(("parallel", "arbitrary")),
    )(proj, proj, proj, bloc, do_raw, states, triu)


def _gate_bwd(dg, dgdz, proj, w2p):
    t = dg.shape[0]
    r = min(512, t)

    def body(dg_ref, s_ref, lo_ref, w_ref, dlo_ref, gw_ref, gb_ref):
        @pl.when(pl.program_id(0) == 0)
        def _():
            gw_ref[...] = jnp.zeros_like(gw_ref)
            gb_ref[...] = jnp.zeros_like(gb_ref)

        dz = dg_ref[...] * s_ref[...]
        dzb = dz.astype(BF16)
        gb_ref[...] += jnp.sum(dz, axis=0, keepdims=True)
        gw_ref[...] += _dot(lo_ref[...].astype(BF16), dzb, TN)
        dlo_ref[...] = _dot(dzb, w_ref[...], NT).astype(BF16)

    return pl.pallas_call(
        body, name="gate_bwd", grid=(t // r,),
        in_specs=[pl.BlockSpec((r, GLA_KW), lambda i: (i, 0)),
                  pl.BlockSpec((r, GLA_KW), lambda i: (i, 0)),
                  pl.BlockSpec((r, LANES), lambda i: (i, C_LO // LANES)),
                  pl.BlockSpec((LANES, GLA_KW), lambda i: (0, 0))],
        out_specs=[pl.BlockSpec((r, LANES), lambda i: (i, 0)),
                   pl.BlockSpec((LANES, GLA_KW), lambda i: (0, 0)),
                   pl.BlockSpec((1, GLA_KW), lambda i: (0, 0))],
        out_shape=[jax.ShapeDtypeStruct((t, LANES), BF16), jax.ShapeDtypeStruct((LANES, GLA_KW), F32),
                   jax.ShapeDtypeStruct((1, GLA_KW), F32)],
        compiler_params=_cparams(("arbitrary",)),
    )(dg, dgdz, proj, w2p)


def _swa_bwd(proj, dcat, sinks):
    t = proj.shape[0]

    def body(sink_ref, q_ref, kp_ref, kc_ref, vp_ref, vc_ref, d_ref, dq_ref, dk_ref, dv_ref, dsink_ref):
        i = pl.program_id(0)

        @pl.when(i == 0)
        def _():
            dk_ref[...] = jnp.zeros_like(dk_ref)
            dv_ref[...] = jnp.zeros_like(dv_ref)
            dsink_ref[...] = jnp.zeros_like(dsink_ref)

        valid, lo = _swa_masks(i)
        lane = lax.broadcasted_iota(jnp.int32, (1, LANES), 1)
        kb = jnp.concatenate([kp_ref[...], kc_ref[...]], axis=0)
        vb = jnp.concatenate([vp_ref[...], vc_ref[...]], axis=0)
        dsink = jnp.zeros((1, LANES), F32)
        folded_k, folded_v = [], []
        for kv in range(2):
            kdup = _dup_half(kb, lo, kv).astype(BF16)
            vdup = _dup_half(vb, lo, kv).astype(BF16)
            dkd = jnp.zeros((2 * WINDOW, LANES), F32)
            dvd = jnp.zeros((2 * WINDOW, LANES), F32)
            for p in range(4):
                sl = slice(LANES * (4 * kv + p), LANES * (4 * kv + p + 1))
                qp = q_ref[:, sl]
                dp_ = d_ref[:, sl]
                halves = []
                for e in range(2):
                    h = 8 * kv + 2 * p + e
                    half = lo if e == 0 else ~lo
                    qm = jnp.where(half, qp, 0.0).astype(BF16)
                    dom = jnp.where(half, dp_, 0.0).astype(BF16)
                    pn, psink = _swa_probs(qm, kdup, valid, sink_ref[h])
                    dpr = _dot(dom, vdup, NT)
                    drow = jnp.sum(dpr * pn, axis=-1, keepdims=True)
                    dsink = dsink + jnp.where(lane == h, -jnp.sum(psink * drow), 0.0)
                    dsb = (pn * (dpr - drow) * (SWA_DH ** -0.5)).astype(BF16)
                    halves.append(_dot(dsb, kdup, NN))
                    dkd = dkd + _dot(dsb, qm, TN)
                    dvd = dvd + _dot(pn.astype(BF16), dom, TN)
                dq_ref[:, sl] = jnp.where(lo, halves[0], halves[1]).astype(BF16)
            folded_k.append(dkd + pltpu.roll(dkd, SWA_DH, axis=1))
            folded_v.append(dvd + pltpu.roll(dvd, SWA_DH, axis=1))
        dkb = jnp.where(lo, folded_k[0], folded_k[1])
        dvb = jnp.where(lo, folded_v[0], folded_v[1])
        dsink_ref[...] += dsink
        cur = pl.ds(pl.multiple_of(i * WINDOW, WINDOW), WINDOW)
        dk_ref[cur, :] += dkb[WINDOW:, :]
        dv_ref[cur, :] += dvb[WINDOW:, :]

        @pl.when(i > 0)
        def _():
            prev = pl.ds(pl.multiple_of((i - 1) * WINDOW, WINDOW), WINDOW)
            dk_ref[prev, :] += dkb[:WINDOW, :]
            dv_ref[prev, :] += dvb[:WINDOW, :]

    kvspec = lambda col, prev: pl.BlockSpec(
        (WINDOW, LANES), (lambda i: (jnp.maximum(i - 1, 0), col)) if prev else (lambda i: (i, col)))
    full = pl.BlockSpec((t, LANES), lambda i: (0, 0))
    return pl.pallas_call(
        body, name="swa_bwd", grid=(t // WINDOW,),
        in_specs=[pl.BlockSpec(memory_space=pltpu.SMEM),
                  pl.BlockSpec((WINDOW, D_SWA), lambda i: (i, C_QS // D_SWA)),
                  kvspec(C_KS // LANES, True), kvspec(C_KS // LANES, False),
                  kvspec(C_VS // LANES, True), kvspec(C_VS // LANES, False),
                  pl.BlockSpec((WINDOW, D_SWA), lambda i: (i, 1))],
        out_specs=[pl.BlockSpec((WINDOW, D_SWA), lambda i: (i, 0)), full, full,
                   pl.BlockSpec((1, LANES), lambda i: (0, 0))],
        out_shape=[jax.ShapeDtypeStruct((t, D_SWA), BF16), jax.ShapeDtypeStruct((t, LANES), F32),
                   jax.ShapeDtypeStruct((t, LANES), F32), jax.ShapeDtypeStruct((1, LANES), F32)],
        compiler_params=_cparams(("arbitrary",)),
    )(sinks, proj, proj, proj, proj, proj, dcat)


def _grad_w_in(xb, dproj):
    t = xb.shape[0]
    tm, tn, tk = 1024, 640, min(512, t)
    return _matmul(
        "grad_w_in", xb, dproj, TN, (D_MODEL // tm, D_INP // tn, t // tk),
        pl.BlockSpec((tk, tm), lambda i, j, k: (k, i)),
        pl.BlockSpec((tk, tn), lambda i, j, k: (k, j)),
        jax.ShapeDtypeStruct((D_MODEL, D_INP), F32),
        pl.BlockSpec((tm, tn), lambda i, j, k: (i, j)), (tm, tn))


def _grad_x(dproj, w_in_p, dr1):
    t = dproj.shape[0]
    tm, tk = min(512, t), 640

    def epilogue(acc_ref, extra, outs):
        for r0, n in _row_chunks(tm):
            rs = slice(r0, r0 + n)
            outs[0][rs, :] = ALPHA * extra[0][rs, :] + acc_ref[rs, :]

    row = pl.BlockSpec((tm, D_MODEL), lambda i, j, k: (i, 0))
    return _matmul(
        "grad_x", dproj, w_in_p, NT, (t // tm, 1, D_INP // tk),
        pl.BlockSpec((tm, tk), lambda i, j, k: (i, k)),
        pl.BlockSpec((D_MODEL, tk), lambda i, j, k: (0, k)),
        jax.ShapeDtypeStruct((t, D_MODEL), F32), row, (tm, D_MODEL),
        extra=(dr1,), extra_specs=(row,), epilogue=epilogue)


def _place():
    x, y, c = lax.axis_index("x"), lax.axis_index("y"), lax.axis_index("c")
    chips = [(1 - x, y), (x, 1 - y), (1 - x, 1 - y)]
    return x, y, c, chips


def _all_gather(shards):
    n = len(shards)
    hbm = pl.BlockSpec(memory_space=pl.ANY)

    def body(*refs):
        ins, outs = refs[:n], refs[n:2 * n]
        send, recv, loc = refs[2 * n:]
        x, y, c, chips = _place()
        me, sib = (x, y, c), (x, y, 1 - c)

        def slot(t, px, py, pc):
            return outs[t].at[4 * px + 2 * py + pc]

        def copy(t, k, block, to, src=None):
            return pltpu.make_async_remote_copy(
                src_ref=slot(t, *block) if src is None else src, dst_ref=slot(t, *block),
                send_sem=send.at[7 * t + k], recv_sem=recv.at[7 * t + k], device_id=to, device_id_type=MESH)

        mine = [pltpu.make_async_copy(ins[t], slot(t, *me), loc.at[t]) for t in range(n)]
        for cp in mine:
            cp.start()
        sent = []
        for t in range(n):
            sent.append(copy(t, 0, me, sib, src=ins[t]))
            sent += [copy(t, 1 + j, me, (*chip, c), src=ins[t]) for j, chip in enumerate(chips)]
        for cp in sent:
            cp.start()
        for t in range(n):
            for j, chip in enumerate(chips):
                copy(t, 1 + j, (*chip, c), me).wait_recv()
                fwd = copy(t, 4 + j, (*chip, c), sib)
                fwd.start()
                sent.append(fwd)
        for t in range(n):
            copy(t, 0, sib, me).wait_recv()
            for j, chip in enumerate(chips):
                copy(t, 4 + j, (*chip, 1 - c), me).wait_recv()
        for cp in sent:
            cp.wait_send()
        for cp in mine:
            cp.wait()

    return pl.pallas_call(
        body, name="all_gather_weights",
        in_specs=[hbm] * n, out_specs=[hbm] * n,
        out_shape=[jax.ShapeDtypeStruct((N_DEV,) + s.shape, s.dtype) for s in shards],
        scratch_shapes=[pltpu.SemaphoreType.DMA((7 * n,)), pltpu.SemaphoreType.DMA((7 * n,)),
                        pltpu.SemaphoreType.DMA((n,))],
    )(*shards)


def _exchange_sibling(grads):
    n = len(grads)
    hbm = pl.BlockSpec(memory_space=pl.ANY)

    def body(*refs):
        ins, outs = refs[:n], refs[n:2 * n]
        send, recv = refs[2 * n:]
        x, y, c, _ = _place()
        cps = []
        for t in range(n):
            for q in range(4):
                cps.append(pltpu.make_async_remote_copy(
                    src_ref=ins[t].at[2 * q + (1 - c)], dst_ref=outs[t].at[q],
                    send_sem=send.at[4 * t + q], recv_sem=recv.at[4 * t + q],
                    device_id=(x, y, 1 - c), device_id_type=MESH))
        for cp in cps:
            cp.start()
        for cp in cps:
            cp.wait()

    return pl.pallas_call(
        body, name="exchange_sibling",
        in_specs=[hbm] * n, out_specs=[hbm] * n,
        out_shape=[jax.ShapeDtypeStruct((4,) + g.shape[1:], g.dtype) for g in grads],
        scratch_shapes=[pltpu.SemaphoreType.DMA((4 * n,)), pltpu.SemaphoreType.DMA((4 * n,))],
    )(*grads)


def _exchange_chips(sums):
    n = len(sums)
    hbm = pl.BlockSpec(memory_space=pl.ANY)

    def body(*refs):
        ins, outs = refs[:n], refs[n:2 * n]
        send, recv = refs[2 * n:]
        x, y, c, chips = _place()
        cps = []
        for t in range(n):
            for j, (px, py) in enumerate(chips):
                cps.append(pltpu.make_async_remote_copy(
                    src_ref=ins[t].at[2 * px + py], dst_ref=outs[t].at[j],
                    send_sem=send.at[3 * t + j], recv_sem=recv.at[3 * t + j],
                    device_id=(px, py, c), device_id_type=MESH))
        for cp in cps:
            cp.start()
        for cp in cps:
            cp.wait()

    return pl.pallas_call(
        body, name="exchange_chips",
        in_specs=[hbm] * n, out_specs=[hbm] * n,
        out_shape=[jax.ShapeDtypeStruct((3,) + s.shape[1:], s.dtype) for s in sums],
        scratch_shapes=[pltpu.SemaphoreType.DMA((3 * n,)), pltpu.SemaphoreType.DMA((3 * n,))],
    )(*sums)


def _pair_sum(name, grad, from_sibling, core):
    _, rows, cols = grad.shape
    tr = 256

    def body(core_ref, g_ref, s_ref, o_ref):
        o_ref[...] = g_ref[...] + s_ref[...]

    return pl.pallas_call(
        body, name=name,
        grid_spec=pltpu.PrefetchScalarGridSpec(
            num_scalar_prefetch=1, grid=(4, rows // tr),
            in_specs=[pl.BlockSpec((None, tr, cols), lambda q, r, cr: (2 * q + cr[0], r, 0)),
                      pl.BlockSpec((None, tr, cols), lambda q, r, cr: (q, r, 0))],
            out_specs=pl.BlockSpec((None, tr, cols), lambda q, r, cr: (q, r, 0))),
        out_shape=jax.ShapeDtypeStruct((4, rows, cols), F32),
        compiler_params=_cparams(("parallel", "parallel")),
    )(core, grad, from_sibling)


def _adam_math(g, w, m, v):
    m2 = ADAM_B1 * m + (1.0 - ADAM_B1) * g
    v2 = ADAM_B2 * v + (1.0 - ADAM_B2) * (g * g)
    m_hat = m2 / (1.0 - ADAM_B1 ** ADAM_STEP)
    v_hat = v2 / (1.0 - ADAM_B2 ** ADAM_STEP)
    delta = -ADAM_LR * (m_hat / (jnp.sqrt(v_hat) + ADAM_EPS) + ADAM_WD * w)
    return delta, m2, v2


def _sum_adam(name, sums, from_chips, chip, w, m, v):
    rows, cols = w.shape
    tr = 128

    def body(chip_ref, p_ref, r_ref, w_ref, m_ref, v_ref, g_out, d_out, m_out, v_out):
        g = ((p_ref[...] + r_ref[0]) + r_ref[1]) + r_ref[2]
        d, m2, v2 = _adam_math(g, w_ref[...], m_ref[...], v_ref[...])
        g_out[...] = g
        d_out[...] = d
        m_out[...] = m2
        v_out[...] = v2

    blk = pl.BlockSpec((tr, cols), lambda r, cr: (r, 0))
    shp = jax.ShapeDtypeStruct((rows, cols), F32)
    return pl.pallas_call(
        body, name=name,
        grid_spec=pltpu.PrefetchScalarGridSpec(
            num_scalar_prefetch=1, grid=(rows // tr,),
            in_specs=[pl.BlockSpec((None, tr, cols), lambda r, cr: (cr[0], r, 0)),
                      pl.BlockSpec((3, tr, cols), lambda r, cr: (0, r, 0)),
                      blk, blk, blk],
            out_specs=[blk, blk, blk, blk]),
        out_shape=[shp, shp, shp, shp],
        compiler_params=_cparams(("parallel",)),
    )(chip, sums, from_chips, w, m, v)


def _adam_small(name, g, w, m, v):
    def body(g_ref, w_ref, m_ref, v_ref, d_out, m_out, v_out):
        d, m2, v2 = _adam_math(g_ref[...], w_ref[...], m_ref[...], v_ref[...])
        d_out[...] = d
        m_out[...] = m2
        v_out[...] = v2

    shp = jax.ShapeDtypeStruct(w.shape, F32)
    return pl.pallas_call(body, name=name, out_shape=[shp, shp, shp])(g, w, m, v)


def _all_reduce_small(pack):
    rows = pack.shape[0]

    def body(in_ref, out_ref, slots, send, recv):
        x, y, c, _ = _place()
        me = 4 * x + 2 * y + c
        slots[me] = in_ref[...]
        cps = []
        for k in range(1, N_DEV):
            dx, dy, dc = (k >> 2) & 1, (k >> 1) & 1, k & 1
            to = (jnp.bitwise_xor(x, dx), jnp.bitwise_xor(y, dy), jnp.bitwise_xor(c, dc))
            cps.append(pltpu.make_async_remote_copy(
                src_ref=in_ref, dst_ref=slots.at[me], send_sem=send.at[k - 1], recv_sem=recv.at[k - 1],
                device_id=to, device_id_type=MESH))
        for cp in cps:
            cp.start()
        for cp in cps:
            cp.wait()
        acc = slots[0]
        for d in range(1, N_DEV):
            acc = acc + slots[d]
        out_ref[...] = acc

    return pl.pallas_call(
        body, name="all_reduce_small",
        in_specs=[pl.BlockSpec(memory_space=pltpu.VMEM)],
        out_specs=pl.BlockSpec(memory_space=pltpu.VMEM),
        out_shape=jax.ShapeDtypeStruct((rows, LANES), F32),
        scratch_shapes=[pltpu.VMEM((N_DEV, rows, LANES), F32),
                        pltpu.SemaphoreType.DMA((N_DEV - 1,)), pltpu.SemaphoreType.DMA((N_DEV - 1,))],
    )(pack)


def _rows128(a):
    flat = a.reshape(-1)
    padn = (-flat.shape[0]) % (8 * LANES)
    if padn:
        flat = jnp.concatenate([flat, jnp.zeros((padn,), flat.dtype)])
    return flat.reshape(-1, LANES)


def kernel(x, w_in, w_gk2, b_gk, gla_norm_w, swa_sinks, w_out, ln1_g, ln1_b, w_up, w_down, ln2_g, ln2_b, loss_target, m_w_in, m_w_gk2, m_b_gk, m_gla_norm_w, m_swa_sinks, m_w_out, m_ln1_g, m_ln1_b, m_w_up, m_w_down, m_ln2_g, m_ln2_b, v_w_in, v_w_gk2, v_b_gk, v_gla_norm_w, v_swa_sinks, v_w_out, v_ln1_g, v_ln1_b, v_w_up, v_w_down, v_ln2_g, v_ln2_b):
    xc, yc, cc = lax.axis_index("x"), lax.axis_index("y"), lax.axis_index("c")
    core = jnp.stack([cc]).astype(jnp.int32)
    chip = jnp.stack([2 * xc + yc]).astype(jnp.int32)
    me = 4 * xc + 2 * yc + cc

    x2 = x[0]
    t = x2.shape[0]
    xb = x2.astype(BF16)
    target = loss_target[0]

    gathered = _all_gather([w_in[0].astype(BF16), w_gk2[0].astype(BF16), w_out[0].astype(BF16),
                            w_up[0].astype(BF16), w_down[0].astype(BF16)])
    w_in_p = _to_padded_cols(gathered[0].transpose(1, 0, 2).reshape(D_MODEL, D_IN))
    w2 = gathered[1].transpose(1, 0, 2).reshape(GATE_RANK, GLA_KW)
    w2p = jnp.concatenate([w2, jnp.zeros((LANES - GATE_RANK, GLA_KW), BF16)], axis=0)
    w_out_f = gathered[2].reshape(D_MODEL, D_MODEL)
    w_up_f = gathered[3]
    w_down_f = gathered[4].reshape(D_FF, D_MODEL)

    proj = _proj(xb, w_in_p)
    rt = min(256, t)
    ii = jnp.arange(rt)
    tri = ((ii[:, None] // GC == ii[None, :] // GC) & (ii[None, :] <= ii[:, None])).astype(F32)
    bloc, dgdz = _gate_fwd(proj, w2p, b_gk, tri)
    o_raw, states = _gla_fwd(proj, bloc)
    gla_out = _gla_norm_fwd(o_raw, proj, gla_norm_w)
    swa_out = _swa_fwd(proj, swa_sinks[0])
    cat = jnp.concatenate([gla_out, swa_out], axis=-1)
    r1, h1, h1b = _mix_ln1(cat, w_out_f, x2, ln1_g, ln1_b)
    a_act, hdn = _mlp_up(h1b, w_up_f)
    dr2, dr2b, g_ln2_g, g_ln2_b, loss_part = _mlp_down_loss(hdn, w_down_f, h1, target, ln2_g, ln2_b)

    du = _mlp_down_bwd(dr2b, w_down_f, a_act)
    gw_down = _grad_w_down(hdn, dr2b)
    gw_up = _grad_w_up(h1b, du)
    dr1, g_ln1_g, g_ln1_b = _mlp_up_bwd_ln1(du, w_up_f, dr2, r1, ln1_g)
    dcat = _dcat(dr1, w_out_f)
    gw_out = _grad_w_out(cat, dr1)
    do_raw, dg_g, g_norm_w = _gla_norm_bwd(dcat, o_raw, proj, gla_norm_w)
    jj = jnp.arange(GC)
    triu = (jj[None, :] >= jj[:, None]).astype(F32)
    dq_g, dk_g, dv_g, dgk = _gla_bwd(proj, bloc, do_raw, states, triu)
    dlo, gw2_p, g_b_gk = _gate_bwd(dgk, dgdz, proj, w2p)
    dq_s, dk_s, dv_s, g_sinks = _swa_bwd(proj, dcat, swa_sinks[0])
    dproj = jnp.concatenate([dq_g, dk_g, dv_g, dg_g, dq_s, dk_s.astype(BF16), dv_s.astype(BF16), dlo], axis=-1)
    gw_in_p = _grad_w_in(xb, dproj)
    grad_x = _grad_x(dproj, w_in_p, dr1)

    gw_in_b = _from_padded_cols(gw_in_p).reshape(D_MODEL, N_DEV, D_IN // N_DEV).transpose(1, 0, 2)
    parts = [gw_in_b, gw_out.reshape(N_DEV, D_MODEL // N_DEV, D_MODEL), gw_up,
             gw_down.reshape(N_DEV, D_FF // N_DEV, D_MODEL)]
    from_sib = _exchange_sibling(parts)
    names = ["w_in", "w_out", "w_up", "w_down"]
    sums = [_pair_sum("pair_sum_" + nm, g, s, core) for nm, g, s in zip(names, parts, from_sib)]
    from_chips = _exchange_chips(sums)
    big = {}
    for nm, s, r, w, m, v in zip(names, sums, from_chips, (w_in, w_out, w_up, w_down),
                                 (m_w_in, m_w_out, m_w_up, m_w_down), (v_w_in, v_w_out, v_w_up, v_w_down)):
        big[nm] = [o[None] for o in _sum_adam("sum_adam_" + nm, s, r, chip, w[0], m[0], v[0])]

    pieces = [loss_part, g_b_gk, g_norm_w, g_sinks[:, :SWA_HEADS], g_ln1_g, g_ln1_b, g_ln2_g, g_ln2_b,
              gw2_p[:GATE_RANK]]
    pack = jnp.concatenate([_rows128(p) for p in pieces], axis=0)
    tot = _all_reduce_small(pack)
    sizes = [p.size for p in pieces]
    offs = [0]
    for p in pieces:
        offs.append(offs[-1] + _rows128(p).shape[0])
    unpack = lambda i, shape: tot[offs[i]:offs[i + 1]].reshape(-1)[:sizes[i]].reshape(shape)
    loss = tot[0, 0]
    small_names = ["b_gk", "gla_norm_w", "swa_sinks", "ln1_g", "ln1_b", "ln2_g", "ln2_b"]
    small_w = dict(b_gk=(b_gk, m_b_gk, v_b_gk), gla_norm_w=(gla_norm_w, m_gla_norm_w, v_gla_norm_w),
                   swa_sinks=(swa_sinks, m_swa_sinks, v_swa_sinks), ln1_g=(ln1_g, m_ln1_g, v_ln1_g),
                   ln1_b=(ln1_b, m_ln1_b, v_ln1_b), ln2_g=(ln2_g, m_ln2_g, v_ln2_g),
                   ln2_b=(ln2_b, m_ln2_b, v_ln2_b))
    small_g = {nm: unpack(1 + i, small_w[nm][0].shape) for i, nm in enumerate(small_names)}
    g_pack = jnp.concatenate([_rows128(small_g[nm]) for nm in small_names], axis=0)
    wmv = [jnp.concatenate([_rows128(small_w[nm][k]) for nm in small_names], axis=0) for k in range(3)]
    small_out = _adam_small("adam_replicated", g_pack, *wmv)
    srow = [0]
    for nm in small_names:
        srow.append(srow[-1] + _rows128(small_w[nm][0]).shape[0])
    small = {}
    for i, nm in enumerate(small_names):
        shape = small_w[nm][0].shape
        n = small_w[nm][0].size
        small[nm] = [small_g[nm]] + [o[srow[i]:srow[i + 1]].reshape(-1)[:n].reshape(shape) for o in small_out]

    gw2_full = unpack(8, (GATE_RANK, GLA_KW))
    gw2_loc = lax.dynamic_slice_in_dim(gw2_full, me * (GLA_KW // N_DEV), GLA_KW // N_DEV, axis=1)
    gk2_out = _adam_small("adam_w_gk2", gw2_loc, w_gk2[0], m_w_gk2[0], v_w_gk2[0])
    big["w_gk2"] = [gw2_loc[None]] + [o[None] for o in gk2_out]

    order = ["w_in", "w_gk2", "b_gk", "gla_norm_w", "swa_sinks", "w_out", "ln1_g", "ln1_b", "w_up", "w_down",
             "ln2_g", "ln2_b"]
    res = {**big, **small}
    outs = [loss, grad_x[None]]
    for k in range(4):
        outs += [res[nm][k] for nm in order]
    return tuple(outs)
```

```python
import functools

import jax
import jax.numpy as jnp
from jax import lax
from jax.experimental import pallas as pl
from jax.experimental.pallas import tpu as pltpu

F32 = jnp.float32
BF16 = jnp.bfloat16

N_DEV = 8
D_MODEL = 2048
D_FF = 8192
GLA_HEADS = 4
GLA_DK = 128
GLA_DV = 256
GLA_KW = 512
D_GLA = 1024
GATE_RANK = 16
GATE_TAU = 16.0
SWA_HEADS = 16
SWA_DH = 64
WINDOW = 128
D_SWA = 1024
D_IN = 4368
ALPHA = 2.0 ** 0.25
LN_EPS = 1e-5
RMS_EPS = 1e-5
ADAM_LR = 0.001
ADAM_B1 = 0.9
ADAM_B2 = 0.999
ADAM_EPS = 1e-08
ADAM_WD = 0.01
ADAM_STEP = 10

C_QG, C_KG, C_VG, C_GG, C_QS, C_KS, C_VS, C_LO = 0, 512, 1024, 2048, 3072, 4096, 4224, 4352
D_INP = 4480
LANES = 128
GC = 16
NEG = -1e30

NN = ((1,), (0,))
NT = ((1,), (1,))
TN = ((0,), (0,))

VMEM_LIMIT = 52 * 1024 * 1024
MESH = pl.DeviceIdType.MESH


def _dot(a, b, dn, precision=None):
    return lax.dot_general(a, b, (dn, ((), ())), preferred_element_type=F32, precision=precision)


def _bf16_round(v):
    return v.astype(BF16).astype(F32)


def _cparams(dims):
    return pltpu.CompilerParams(dimension_semantics=dims, vmem_limit_bytes=VMEM_LIMIT)


def _to_padded_cols(w):
    pad = jnp.zeros(w.shape[:-1] + (D_INP - D_IN,), w.dtype)
    return jnp.concatenate([w[..., :3072], w[..., 3088:], w[..., 3072:3088], pad], axis=-1)


def _from_padded_cols(g):
    return jnp.concatenate([g[..., :3072], g[..., C_LO:C_LO + GATE_RANK], g[..., 3072:C_LO]], axis=-1)


def _matmul(name, a, b, dn, grid, a_spec, b_spec, out_shape, out_specs, acc_shape, *,
            extra=(), extra_specs=(), epilogue=None, dims=("parallel", "parallel", "arbitrary")):
    nk = grid[2]
    n_extra = len(extra)

    def body(a_ref, b_ref, *rest):
        extra_refs = rest[:n_extra]
        out_refs = rest[n_extra:-1]
        acc_ref = rest[-1]
        part = _dot(a_ref[...].astype(BF16), b_ref[...].astype(BF16), dn)

        def finish():
            if epilogue is None:
                out_refs[0][...] = acc_ref[...].astype(out_refs[0].dtype)
            else:
                epilogue(acc_ref, extra_refs, out_refs)

        if nk == 1:
            acc_ref[...] = part
            finish()
        else:
            k = pl.program_id(2)

            @pl.when(k == 0)
            def _():
                acc_ref[...] = part

            @pl.when(k > 0)
            def _():
                acc_ref[...] += part

            @pl.when(k == nk - 1)
            def _():
                finish()

    return pl.pallas_call(
        body, name=name, grid=grid,
        in_specs=[a_spec, b_spec, *extra_specs],
        out_specs=out_specs, out_shape=out_shape,
        scratch_shapes=[pltpu.VMEM(acc_shape, F32)],
        compiler_params=_cparams(dims),
    )(a, b, *extra)


def _row_chunks(rows, step=128):
    step = min(step, rows)
    return [(r, step) for r in range(0, rows, step)]


def _ln_stats(r):
    mu = jnp.mean(r, axis=-1, keepdims=True)
    xc = r - mu
    var = jnp.mean(xc * xc, axis=-1, keepdims=True)
    rstd = lax.rsqrt(var + LN_EPS)
    return xc * rstd, rstd


def _ln_bwd(dy_g, xhat, rstd):
    m1 = jnp.mean(dy_g, axis=-1, keepdims=True)
    m2 = jnp.mean(dy_g * xhat, axis=-1, keepdims=True)
    return rstd * (dy_g - m1 - xhat * m2)


def _proj(xb, w_in_p):
    t = xb.shape[0]
    tm, tn = min(1024, t), 640
    return _matmul(
        "proj", xb, w_in_p, NN, (t // tm, D_INP // tn, 1),
        pl.BlockSpec((tm, D_MODEL), lambda i, j, k: (i, 0)),
        pl.BlockSpec((D_MODEL, tn), lambda i, j, k: (0, j)),
        jax.ShapeDtypeStruct((t, D_INP), F32),
        pl.BlockSpec((tm, tn), lambda i, j, k: (i, j)),
        (tm, tn))


def _gate_fwd(proj, w2p, b_gk, tri):
    t = proj.shape[0]
    r = tri.shape[0]

    def body(lo_ref, w_ref, b_ref, tri_ref, bloc_ref, dgdz_ref):
        z = _dot(lo_ref[...].astype(BF16), w_ref[...], NN) + b_ref[...]
        e = jnp.exp(-jnp.abs(z))
        gk = (jnp.minimum(z, 0.0) - jnp.log1p(e)) * (1.0 / GATE_TAU)
        inv = 1.0 / (1.0 + e)
        dgdz_ref[...] = jnp.where(z >= 0.0, e * inv, inv) * (1.0 / GATE_TAU)
        bloc_ref[...] = _dot(tri_ref[...], gk, NN, precision=lax.Precision.HIGHEST)

    return pl.pallas_call(
        body, name="gate_fwd", grid=(t // r,),
        in_specs=[pl.BlockSpec((r, LANES), lambda i: (i, C_LO // LANES)),
                  pl.BlockSpec((LANES, GLA_KW), lambda i: (0, 0)),
                  pl.BlockSpec((1, GLA_KW), lambda i: (0, 0)),
                  pl.BlockSpec((r, r), lambda i: (0, 0))],
        out_specs=[pl.BlockSpec((r, GLA_KW), lambda i: (i, 0)),
                   pl.BlockSpec((r, GLA_KW), lambda i: (i, 0))],
        out_shape=[jax.ShapeDtypeStruct((t, GLA_KW), F32), jax.ShapeDtypeStruct((t, GLA_KW), F32)],
        compiler_params=_cparams(("parallel",)),
    )(proj, w2p, b_gk, tri)


def _gla_fwd(proj, bloc):
    t = proj.shape[0]
    r = min(256, t)
    ncb = r // GC
    scale = GLA_DK ** -0.5

    def body(q_ref, k_ref, v_ref, b_ref, o_ref, st_ref, s_scr):
        @pl.when(pl.program_id(1) == 0)
        def _():
            s_scr[...] = jnp.zeros_like(s_scr)

        rows = lax.broadcasted_iota(jnp.int32, (GC, 1), 0)

        def chunk(c, carry):
            r0 = pl.multiple_of(c * GC, GC)
            q = q_ref[pl.ds(r0, GC), :] * scale
            k = k_ref[pl.ds(r0, GC), :]
            v = v_ref[pl.ds(r0, GC), :]
            b = b_ref[pl.ds(r0, GC), :]
            st = s_scr[...]
            st_ref[c] = st
            bl = b[GC - 1:GC, :]
            kr = _bf16_round(k)
            vr = _bf16_round(v)
            o = _dot((q * jnp.exp(b)).astype(BF16), st.astype(BF16), NT)
            for j in range(GC):
                w = jnp.exp(jnp.where(rows >= j, b - b[j:j + 1, :], NEG))
                a = jnp.sum(_bf16_round(q * w) * kr[j:j + 1, :], axis=-1, keepdims=True)
                o = o + _bf16_round(a) * vr[j:j + 1, :]
            o_ref[pl.ds(r0, GC), :] = o
            kd = k * jnp.exp(bl - b)
            s_scr[...] = st * jnp.exp(bl) + _dot(v.astype(BF16), kd.astype(BF16), TN)
            return carry

        lax.fori_loop(0, ncb, chunk, 0)

    return pl.pallas_call(
        body, name="gla_fwd", grid=(GLA_HEADS, t // r),
        in_specs=[pl.BlockSpec((r, GLA_DK), lambda h, i: (i, C_QG // GLA_DK + h)),
                  pl.BlockSpec((r, GLA_DK), lambda h, i: (i, C_KG // GLA_DK + h)),
                  pl.BlockSpec((r, GLA_DV), lambda h, i: (i, C_VG // GLA_DV + h)),
                  pl.BlockSpec((r, GLA_DK), lambda h, i: (i, h))],
        out_specs=[pl.BlockSpec((r, GLA_DV), lambda h, i: (i, h)),
                   pl.BlockSpec((ncb, None, GLA_DV, GLA_DK), lambda h, i: (i, h, 0, 0))],
        out_shape=[jax.ShapeDtypeStruct((t, D_GLA), F32),
                   jax.ShapeDtypeStruct((t // GC, GLA_HEADS, GLA_DV, GLA_DK), F32)],
        scratch_shapes=[pltpu.VMEM((GLA_DV, GLA_DK), F32)],
        compiler_params=_cparams(("parallel", "arbitrary")),
    )(proj, proj, proj, bloc)


def _gla_norm_fwd(o_raw, proj, norm_w):
    t = o_raw.shape[0]
    r = min(512, t)

    def body(o_ref, g_ref, w_ref, out_ref):
        w = w_ref[...]
        for h in range(GLA_HEADS):
            sl = slice(h * GLA_DV, (h + 1) * GLA_DV)
            o = o_ref[:, sl]
            g = g_ref[:, sl]
            on = o * lax.rsqrt(jnp.mean(o * o, axis=-1, keepdims=True) + RMS_EPS)
            out_ref[:, sl] = (on * w * (g * jax.nn.sigmoid(g))).astype(BF16)

    return pl.pallas_call(
        body, name="gla_norm_fwd", grid=(t // r,),
        in_specs=[pl.BlockSpec((r, D_GLA), lambda i: (i, 0)),
                  pl.BlockSpec((r, D_GLA), lambda i: (i, C_GG // D_GLA)),
                  pl.BlockSpec((1, GLA_DV), lambda i: (0, 0))],
        out_specs=pl.BlockSpec((r, D_GLA), lambda i: (i, 0)),
        out_shape=jax.ShapeDtypeStruct((t, D_GLA), BF16),
        compiler_params=_cparams(("parallel",)),
    )(o_raw, proj, norm_w)


def _swa_masks(i):
    qi = lax.broadcasted_iota(jnp.int32, (WINDOW, 1), 0)
    kj = lax.broadcasted_iota(jnp.int32, (1, 2 * WINDOW), 1)
    valid = (kj > qi) & (kj <= qi + WINDOW) & ((i > 0) | (kj >= WINDOW))
    lo = lax.broadcasted_iota(jnp.int32, (1, LANES), 1) < SWA_DH
    return valid, lo


def _dup_half(x, lo, kv):
    xr = pltpu.roll(x, SWA_DH, axis=1)
    return jnp.where(lo, x, xr) if kv == 0 else jnp.where(lo, xr, x)


def _swa_probs(qm, kdup, valid, sink):
    s = _dot(qm, kdup, NT) * (SWA_DH ** -0.5)
    s = jnp.where(valid, s, NEG)
    m = jnp.maximum(jnp.max(s, axis=-1, keepdims=True), sink)
    p = jnp.exp(s - m)
    es = jnp.exp(sink - m)
    inv = 1.0 / (jnp.sum(p, axis=-1, keepdims=True) + es)
    return p * inv, es * inv


def _swa_fwd(proj, sinks):
    t = proj.shape[0]

    def body(sink_ref, q_ref, kp_ref, kc_ref, vp_ref, vc_ref, o_ref):
        i = pl.program_id(0)
        valid, lo = _swa_masks(i)
        kb = jnp.concatenate([kp_ref[...], kc_ref[...]], axis=0)
        vb = jnp.concatenate([vp_ref[...], vc_ref[...]], axis=0)
        for kv in range(2):
            kdup = _dup_half(kb, lo, kv).astype(BF16)
            vdup = _dup_half(vb, lo, kv).astype(BF16)
            for p in range(4):
                sl = slice(LANES * (4 * kv + p), LANES * (4 * kv + p + 1))
                qp = q_ref[:, sl]
                halves = []
                for e in range(2):
                    qm = jnp.where(lo if e == 0 else ~lo, qp, 0.0).astype(BF16)
                    pn, _ = _swa_probs(qm, kdup, valid, sink_ref[8 * kv + 2 * p + e])
                    halves.append(_dot(pn.astype(BF16), vdup, NN))
                o_ref[:, sl] = jnp.where(lo, halves[0], halves[1]).astype(BF16)

    kvspec = lambda col, prev: pl.BlockSpec(
        (WINDOW, LANES), (lambda i: (jnp.maximum(i - 1, 0), col)) if prev else (lambda i: (i, col)))
    return pl.pallas_call(
        body, name="swa_fwd", grid=(t // WINDOW,),
        in_specs=[pl.BlockSpec(memory_space=pltpu.SMEM),
                  pl.BlockSpec((WINDOW, D_SWA), lambda i: (i, C_QS // D_SWA)),
                  kvspec(C_KS // LANES, True), kvspec(C_KS // LANES, False),
                  kvspec(C_VS // LANES, True), kvspec(C_VS // LANES, False)],
        out_specs=pl.BlockSpec((WINDOW, D_SWA), lambda i: (i, 0)),
        out_shape=jax.ShapeDtypeStruct((t, D_SWA), BF16),
        compiler_params=_cparams(("parallel",)),
    )(sinks, proj, proj, proj, proj, proj)


def _mix_ln1(cat, w_out, x, ln_g, ln_b):
    t = cat.shape[0]
    tm, tk = min(512, t), 512

    def epilogue(acc_ref, extra, outs):
        x_ref, g_ref, b_ref = extra
        r1_ref, h1_ref, h1b_ref = outs
        for r0, n in _row_chunks(tm):
            rs = slice(r0, r0 + n)
            r1 = ALPHA * x_ref[rs, :] + acc_ref[rs, :]
            xhat, _ = _ln_stats(r1)
            h = xhat * g_ref[...] + b_ref[...]
            r1_ref[rs, :] = r1
            h1_ref[rs, :] = h
            h1b_ref[rs, :] = h.astype(BF16)

    row = pl.BlockSpec((tm, D_MODEL), lambda i, j, k: (i, 0))
    vec = pl.BlockSpec((1, D_MODEL), lambda i, j, k: (0, 0))
    return _matmul(
        "mix_ln1", cat, w_out, NN, (t // tm, 1, D_MODEL // tk),
        pl.BlockSpec((tm, tk), lambda i, j, k: (i, k)),
        pl.BlockSpec((tk, D_MODEL), lambda i, j, k: (k, 0)),
        [jax.ShapeDtypeStruct((t, D_MODEL), F32), jax.ShapeDtypeStruct((t, D_MODEL), F32),
         jax.ShapeDtypeStruct((t, D_MODEL), BF16)],
        [row, row, row], (tm, D_MODEL),
        extra=(x, ln_g, ln_b), extra_specs=(row, vec, vec), epilogue=epilogue)


def _mlp_up(h1b, w_up):
    t = h1b.shape[0]
    tm, tn = min(1024, t), 1024

    def epilogue(acc_ref, extra, outs):
        a_ref, hdn_ref = outs
        for r0, n in _row_chunks(tm, 256):
            rs = slice(r0, r0 + n)
            a = jnp.maximum(acc_ref[rs, :], 0.0)
            a_ref[rs, :] = a.astype(BF16)
            hdn_ref[rs, :] = (a * a).astype(BF16)

    out = pl.BlockSpec((tm, tn), lambda i, j, k: (i, j))
    return _matmul(
        "mlp_up", h1b, w_up, NN, (t // tm, D_FF // tn, 1),
        pl.BlockSpec((tm, D_MODEL), lambda i, j, k: (i, 0)),
        pl.BlockSpec((None, D_MODEL, tn), lambda i, j, k: (j, 0, 0)),
        [jax.ShapeDtypeStruct((t, D_FF), BF16), jax.ShapeDtypeStruct((t, D_FF), BF16)],
        [out, out], (tm, tn), epilogue=epilogue)


def _mlp_down_loss(hdn, w_down, h1, target, ln_g, ln_b):
    t = hdn.shape[0]
    tm, tk = min(512, t), 512

    def epilogue(acc_ref, extra, outs):
        h1_ref, t_ref, g_ref, b_ref = extra
        dr2_ref, dr2b_ref, gg_ref, gb_ref, loss_ref = outs

        @pl.when(pl.program_id(0) == 0)
        def _():
            gg_ref[...] = jnp.zeros_like(gg_ref)
            gb_ref[...] = jnp.zeros_like(gb_ref)
            loss_ref[...] = jnp.zeros_like(loss_ref)

        for r0, n in _row_chunks(tm):
            rs = slice(r0, r0 + n)
            xhat, rstd = _ln_stats(ALPHA * h1_ref[rs, :] + acc_ref[rs, :])
            err = xhat * g_ref[...] + b_ref[...] - t_ref[rs, :]
            loss_ref[...] += 0.5 * jnp.sum(jnp.mean(err * err, axis=-1, keepdims=True))
            dy = err * (1.0 / D_MODEL)
            gg_ref[...] += jnp.sum(dy * xhat, axis=0, keepdims=True)
            gb_ref[...] += jnp.sum(dy, axis=0, keepdims=True)
            dr2 = _ln_bwd(dy * g_ref[...], xhat, rstd)
            dr2_ref[rs, :] = dr2
            dr2b_ref[rs, :] = dr2.astype(BF16)

    row = pl.BlockSpec((tm, D_MODEL), lambda i, j, k: (i, 0))
    vec = pl.BlockSpec((1, D_MODEL), lambda i, j, k: (0, 0))
    return _matmul(
        "mlp_down_loss", hdn, w_down, NN, (t // tm, 1, D_FF // tk),
        pl.BlockSpec((tm, tk), lambda i, j, k: (i, k)),
        pl.BlockSpec((tk, D_MODEL), lambda i, j, k: (k, 0)),
        [jax.ShapeDtypeStruct((t, D_MODEL), F32), jax.ShapeDtypeStruct((t, D_MODEL), BF16),
         jax.ShapeDtypeStruct((1, D_MODEL), F32), jax.ShapeDtypeStruct((1, D_MODEL), F32),
         jax.ShapeDtypeStruct((1, LANES), F32)],
        [row, row, vec, vec, pl.BlockSpec((1, LANES), lambda i, j, k: (0, 0))],
        (tm, D_MODEL),
        extra=(h1, target, ln_g, ln_b), extra_specs=(row, row, vec, vec), epilogue=epilogue,
        dims=("arbitrary", "arbitrary", "arbitrary"))


def _mlp_down_bwd(dr2b, w_down, a_act):
    t = dr2b.shape[0]
    tm, tn = min(1024, t), 1024

    def epilogue(acc_ref, extra, outs):
        (a_ref,) = extra
        for r0, n in _row_chunks(tm, 256):
            rs = slice(r0, r0 + n)
            outs[0][rs, :] = (acc_ref[rs, :] * (2.0 * a_ref[rs, :].astype(F32))).astype(BF16)

    blk = pl.BlockSpec((tm, tn), lambda i, j, k: (i, j))
    return _matmul(
        "mlp_down_bwd", dr2b, w_down, NT, (t // tm, D_FF // tn, 1),
        pl.BlockSpec((tm, D_MODEL), lambda i, j, k: (i, 0)),
        pl.BlockSpec((tn, D_MODEL), lambda i, j, k: (j, 0)),
        jax.ShapeDtypeStruct((t, D_FF), BF16), blk, (tm, tn),
        extra=(a_act,), extra_specs=(blk,), epilogue=epilogue)


def _grad_w_down(hdn, dr2b):
    t = hdn.shape[0]
    tm, tn, tk = 1024, 2048, min(512, t)
    return _matmul(
        "grad_w_down", hdn, dr2b, TN, (D_FF // tm, D_MODEL // tn, t // tk),
        pl.BlockSpec((tk, tm), lambda i, j, k: (k, i)),
        pl.BlockSpec((tk, tn), lambda i, j, k: (k, j)),
        jax.ShapeDtypeStruct((D_FF, D_MODEL), F32),
        pl.BlockSpec((tm, tn), lambda i, j, k: (i, j)), (tm, tn))


def _grad_w_up(h1b, du):
    t = h1b.shape[0]
    tm, tn, tk = 1024, 1024, min(512, t)
    return _matmul(
        "grad_w_up", h1b, du, TN, (N_DEV, D_MODEL // tm, t // tk),
        pl.BlockSpec((tk, tm), lambda i, j, k: (k, j)),
        pl.BlockSpec((tk, tn), lambda i, j, k: (k, i)),
        jax.ShapeDtypeStruct((N_DEV, D_MODEL, D_FF // N_DEV), F32),
        pl.BlockSpec((None, tm, tn), lambda i, j, k: (i, j, 0)), (tm, tn))


def _mlp_up_bwd_ln1(du, w_up, dr2, r1, ln_g):
    t = du.shape[0]
    tm, tk = min(256, t), D_FF // N_DEV

    def epilogue(acc_ref, extra, outs):
        dr2_ref, r1_ref, g_ref = extra
        dr1_ref, gg_ref, gb_ref = outs

        @pl.when(pl.program_id(0) == 0)
        def _():
            gg_ref[...] = jnp.zeros_like(gg_ref)
            gb_ref[...] = jnp.zeros_like(gb_ref)

        for r0, n in _row_chunks(tm):
            rs = slice(r0, r0 + n)
            dh1 = ALPHA * dr2_ref[rs, :] + acc_ref[rs, :]
            xhat, rstd = _ln_stats(r1_ref[rs, :])
            gg_ref[...] += jnp.sum(dh1 * xhat, axis=0, keepdims=True)
            gb_ref[...] += jnp.sum(dh1, axis=0, keepdims=True)
            dr1_ref[rs, :] = _ln_bwd(dh1 * g_ref[...], xhat, rstd)

    row = pl.BlockSpec((tm, D_MODEL), lambda i, j, k: (i, 0))
    vec = pl.BlockSpec((1, D_MODEL), lambda i, j, k: (0, 0))
    return _matmul(
        "mlp_up_bwd_ln1", du, w_up, NT, (t // tm, 1, N_DEV),
        pl.BlockSpec((tm, tk), lambda i, j, k: (i, k)),
        pl.BlockSpec((None, D_MODEL, tk), lambda i, j, k: (k, 0, 0)),
        [jax.ShapeDtypeStruct((t, D_MODEL), F32), jax.ShapeDtypeStruct((1, D_MODEL), F32),
         jax.ShapeDtypeStruct((1, D_MODEL), F32)],
        [row, vec, vec], (tm, D_MODEL),
        extra=(dr2, r1, ln_g), extra_specs=(row, row, vec), epilogue=epilogue,
        dims=("arbitrary", "arbitrary", "arbitrary"))


def _dcat(dr1, w_out):
    t = dr1.shape[0]
    tm, tn = min(512, t), 1024
    return _matmul(
        "dcat", dr1, w_out, NT, (t // tm, D_MODEL // tn, 1),
        pl.BlockSpec((tm, D_MODEL), lambda i, j, k: (i, 0)),
        pl.BlockSpec((tn, D_MODEL), lambda i, j, k: (j, 0)),
        jax.ShapeDtypeStruct((t, D_MODEL), F32),
        pl.BlockSpec((tm, tn), lambda i, j, k: (i, j)), (tm, tn))


def _grad_w_out(cat, dr1):
    t = cat.shape[0]
    tm, tn, tk = 1024, 1024, min(512, t)
    return _matmul(
        "grad_w_out", cat, dr1, TN, (D_MODEL // tm, D_MODEL // tn, t // tk),
        pl.BlockSpec((tk, tm), lambda i, j, k: (k, i)),
        pl.BlockSpec((tk, tn), lambda i, j, k: (k, j)),
        jax.ShapeDtypeStruct((D_MODEL, D_MODEL), F32),
        pl.BlockSpec((tm, tn), lambda i, j, k: (i, j)), (tm, tn))


def _gla_norm_bwd(dcat, o_raw, proj, norm_w):
    t = o_raw.shape[0]
    r = min(512, t)

    def body(d_ref, o_ref, g_ref, w_ref, do_ref, dg_ref, dw_ref):
        @pl.when(pl.program_id(0) == 0)
        def _():
            dw_ref[...] = jnp.zeros_like(dw_ref)

        w = w_ref[...]
        dw = jnp.zeros((1, GLA_DV), F32)
        for h in range(GLA_HEADS):
            sl = slice(h * GLA_DV, (h + 1) * GLA_DV)
            o = o_ref[:, sl]
            g = g_ref[:, sl]
            d = d_ref[:, sl]
            rr = lax.rsqrt(jnp.mean(o * o, axis=-1, keepdims=True) + RMS_EPS)
            on = o * rr
            sg = jax.nn.sigmoid(g)
            sil = g * sg
            dg_ref[:, sl] = (d * on * w * (sg * (1.0 + g * (1.0 - sg)))).astype(BF16)
            dw = dw + jnp.sum(d * on * sil, axis=0, keepdims=True)
            don = d * w * sil
            do_ref[:, sl] = rr * (don - on * jnp.mean(don * on, axis=-1, keepdims=True))
        dw_ref[...] += dw

    return pl.pallas_call(
        body, name="gla_norm_bwd", grid=(t // r,),
        in_specs=[pl.BlockSpec((r, D_GLA), lambda i: (i, 0)),
                  pl.BlockSpec((r, D_GLA), lambda i: (i, 0)),
                  pl.BlockSpec((r, D_GLA), lambda i: (i, C_GG // D_GLA)),
                  pl.BlockSpec((1, GLA_DV), lambda i: (0, 0))],
        out_specs=[pl.BlockSpec((r, D_GLA), lambda i: (i, 0)),
                   pl.BlockSpec((r, D_GLA), lambda i: (i, 0)),
                   pl.BlockSpec((1, GLA_DV), lambda i: (0, 0))],
        out_shape=[jax.ShapeDtypeStruct((t, D_GLA), F32), jax.ShapeDtypeStruct((t, D_GLA), BF16),
                   jax.ShapeDtypeStruct((1, GLA_DV), F32)],
        compiler_params=_cparams(("arbitrary",)),
    )(dcat, o_raw, proj, norm_w)


def _gla_bwd(proj, bloc, do_raw, states, triu):
    t = proj.shape[0]
    r = min(256, t)
    ncb = r // GC
    nb = t // r
    scale = GLA_DK ** -0.5

    def body(q_ref, k_ref, v_ref, b_ref, do_ref, st_ref, u_ref, dq_ref, dk_ref, dv_ref, dg_ref, ds_scr):
        @pl.when(pl.program_id(1) == 0)
        def _():
            ds_scr[...] = jnp.zeros_like(ds_scr)

        rows = lax.broadcasted_iota(jnp.int32, (GC, 1), 0)

        def chunk(cc, carry):
            c = ncb - 1 - cc
            r0 = pl.multiple_of(c * GC, GC)
            q = q_ref[pl.ds(r0, GC), :] * scale
            k = k_ref[pl.ds(r0, GC), :]
            v = v_ref[pl.ds(r0, GC), :]
            b = b_ref[pl.ds(r0, GC), :]
            do = do_ref[pl.ds(r0, GC), :]
            st = st_ref[c]
            dsn = ds_scr[...]
            bl = b[GC - 1:GC, :]
            eb = jnp.exp(b)
            ekl = jnp.exp(bl - b)
            ebl = jnp.exp(bl)
            qh = q * eb
            kd = k * ekl
            dob = do.astype(BF16)
            dsb = dsn.astype(BF16)
            dqh = _dot(dob, st.astype(BF16), NN)
            dkd = _dot(v.astype(BF16), dsb, NN)
            dv = _dot(kd.astype(BF16), dsb, NT)
            dq_i = jnp.zeros((GC, GLA_DK), F32)
            dk_i = jnp.zeros((GC, GLA_DK), F32)
            dk_x = jnp.zeros((GC, GLA_DK), F32)
            kr = _bf16_round(k)
            vr = _bf16_round(v)
            dor = _bf16_round(do)
            for j in range(GC):
                w = jnp.exp(jnp.where(rows >= j, b - b[j:j + 1, :], NEG))
                qw = q * w
                qwr = _bf16_round(qw)
                a = _bf16_round(jnp.sum(qwr * kr[j:j + 1, :], axis=-1, keepdims=True))
                da = jnp.sum(dor * vr[j:j + 1, :], axis=-1, keepdims=True)
                dq_i = dq_i + da * (w * k[j:j + 1, :])
                dk_x = jnp.where(rows == j, jnp.sum(da * qw, axis=0, keepdims=True), dk_x)
                dk_i = jnp.where(rows == j, jnp.sum(_bf16_round(da) * qwr, axis=0, keepdims=True), dk_i)
                dv = dv + jnp.where(rows == j, jnp.sum(a * dor, axis=0, keepdims=True), 0.0)
            dqs = dqh * eb + dq_i
            dk = dkd * ekl + dk_i
            db_last = jnp.sum(dkd * kd, axis=0, keepdims=True) + ebl * jnp.sum(dsn * st, axis=0, keepdims=True)
            db = q * dqs - k * (dkd * ekl + dk_x) + jnp.where(rows == GC - 1, db_last, 0.0)
            dq_ref[pl.ds(r0, GC), :] = (dqs * scale).astype(BF16)
            dk_ref[pl.ds(r0, GC), :] = dk.astype(BF16)
            dv_ref[pl.ds(r0, GC), :] = dv.astype(BF16)
            dg_ref[pl.ds(r0, GC), :] = _dot(u_ref[...], db, NN, precision=lax.Precision.HIGHEST)
            ds_scr[...] = dsn * ebl + _dot(dob, qh.astype(BF16), TN)
            return carry

        lax.fori_loop(0, ncb, chunk, 0)

    rev = lambda i: nb - 1 - i
    return pl.pallas_call(
        body, name="gla_bwd", grid=(GLA_HEADS, nb),
        in_specs=[pl.BlockSpec((r, GLA_DK), lambda h, i: (rev(i), C_QG // GLA_DK + h)),
                  pl.BlockSpec((r, GLA_DK), lambda h, i: (rev(i), C_KG // GLA_DK + h)),
                  pl.BlockSpec((r, GLA_DV), lambda h, i: (rev(i), C_VG // GLA_DV + h)),
                  pl.BlockSpec((r, GLA_DK), lambda h, i: (rev(i), h)),
                  pl.BlockSpec((r, GLA_DV), lambda h, i: (rev(i), h)),
                  pl.BlockSpec((ncb, None, GLA_DV, GLA_DK), lambda h, i: (rev(i), h, 0, 0)),
                  pl.BlockSpec((GC, GC), lambda h, i: (0, 0))],
        out_specs=[pl.BlockSpec((r, GLA_DK), lambda h, i: (rev(i), h)),
                   pl.BlockSpec((r, GLA_DK), lambda h, i: (rev(i), h)),
                   pl.BlockSpec((r, GLA_DV), lambda h, i: (rev(i), h)),
                   pl.BlockSpec((r, GLA_DK), lambda h, i: (rev(i), h))],
        out_shape=[jax.ShapeDtypeStruct((t, GLA_KW), BF16), jax.ShapeDtypeStruct((t, GLA_KW), BF16),
                   jax.ShapeDtypeStruct((t, D_GLA), BF16), jax.ShapeDtypeStruct((t, GLA_KW), F32)],
        scratch_shapes=[pltpu.VMEM((GLA_DV, GLA_DK), F32)],
        compiler_params=_cparams(("parallel", "arbitrary")),
    )(proj, proj, proj, bloc, do_raw, states, triu)


def _gate_bwd(dg, dgdz, proj, w2p):
    t = dg.shape[0]
    r = min(512, t)

    def body(dg_ref, s_ref, lo_ref, w_ref, dlo_ref, gw_ref, gb_ref):
        @pl.when(pl.program_id(0) == 0)
        def _():
            gw_ref[...] = jnp.zeros_like(gw_ref)
            gb_ref[...] = jnp.zeros_like(gb_ref)

        dz = dg_ref[...] * s_ref[...]
        dzb = dz.astype(BF16)
        gb_ref[...] += jnp.sum(dz, axis=0, keepdims=True)
        gw_ref[...] += _dot(lo_ref[...].astype(BF16), dzb, TN)
        dlo_ref[...] = _dot(dzb, w_ref[...], NT).astype(BF16)

    return pl.pallas_call(
        body, name="gate_bwd", grid=(t // r,),
        in_specs=[pl.BlockSpec((r, GLA_KW), lambda i: (i, 0)),
                  pl.BlockSpec((r, GLA_KW), lambda i: (i, 0)),
                  pl.BlockSpec((r, LANES), lambda i: (i, C_LO // LANES)),
                  pl.BlockSpec((LANES, GLA_KW), lambda i: (0, 0))],
        out_specs=[pl.BlockSpec((r, LANES), lambda i: (i, 0)),
                   pl.BlockSpec((LANES, GLA_KW), lambda i: (0, 0)),
                   pl.BlockSpec((1, GLA_KW), lambda i: (0, 0))],
        out_shape=[jax.ShapeDtypeStruct((t, LANES), BF16), jax.ShapeDtypeStruct((LANES, GLA_KW), F32),
                   jax.ShapeDtypeStruct((1, GLA_KW), F32)],
        compiler_params=_cparams(("arbitrary",)),
    )(dg, dgdz, proj, w2p)


def _swa_bwd(proj, dcat, sinks):
    t = proj.shape[0]

    def body(sink_ref, q_ref, kp_ref, kc_ref, vp_ref, vc_ref, d_ref, dq_ref, dk_ref, dv_ref, dsink_ref):
        i = pl.program_id(0)

        @pl.when(i == 0)
        def _():
            dk_ref[...] = jnp.zeros_like(dk_ref)
            dv_ref[...] = jnp.zeros_like(dv_ref)
            dsink_ref[...] = jnp.zeros_like(dsink_ref)

        valid, lo = _swa_masks(i)
        lane = lax.broadcasted_iota(jnp.int32, (1, LANES), 1)
        kb = jnp.concatenate([kp_ref[...], kc_ref[...]], axis=0)
        vb = jnp.concatenate([vp_ref[...], vc_ref[...]], axis=0)
        dsink = jnp.zeros((1, LANES), F32)
        folded_k, folded_v = [], []
        for kv in range(2):
            kdup = _dup_half(kb, lo, kv).astype(BF16)
            vdup = _dup_half(vb, lo, kv).astype(BF16)
            dkd = jnp.zeros((2 * WINDOW, LANES), F32)
            dvd = jnp.zeros((2 * WINDOW, LANES), F32)
            for p in range(4):
                sl = slice(LANES * (4 * kv + p), LANES * (4 * kv + p + 1))
                qp = q_ref[:, sl]
                dp_ = d_ref[:, sl]
                halves = []
                for e in range(2):
                    h = 8 * kv + 2 * p + e
                    half = lo if e == 0 else ~lo
                    qm = jnp.where(half, qp, 0.0).astype(BF16)
                    dom = jnp.where(half, dp_, 0.0).astype(BF16)
                    pn, psink = _swa_probs(qm, kdup, valid, sink_ref[h])
                    dpr = _dot(dom, vdup, NT)
                    drow = jnp.sum(dpr * pn, axis=-1, keepdims=True)
                    dsink = dsink + jnp.where(lane == h, -jnp.sum(psink * drow), 0.0)
                    dsb = (pn * (dpr - drow) * (SWA_DH ** -0.5)).astype(BF16)
                    halves.append(_dot(dsb, kdup, NN))
                    dkd = dkd + _dot(dsb, qm, TN)
                    dvd = dvd + _dot(pn.astype(BF16), dom, TN)
                dq_ref[:, sl] = jnp.where(lo, halves[0], halves[1]).astype(BF16)
            folded_k.append(dkd + pltpu.roll(dkd, SWA_DH, axis=1))
            folded_v.append(dvd + pltpu.roll(dvd, SWA_DH, axis=1))
        dkb = jnp.where(lo, folded_k[0], folded_k[1])
        dvb = jnp.where(lo, folded_v[0], folded_v[1])
        dsink_ref[...] += dsink
        cur = pl.ds(pl.multiple_of(i * WINDOW, WINDOW), WINDOW)
        dk_ref[cur, :] += dkb[WINDOW:, :]
        dv_ref[cur, :] += dvb[WINDOW:, :]

        @pl.when(i > 0)
        def _():
            prev = pl.ds(pl.multiple_of((i - 1) * WINDOW, WINDOW), WINDOW)
            dk_ref[prev, :] += dkb[:WINDOW, :]
            dv_ref[prev, :] += dvb[:WINDOW, :]

    kvspec = lambda col, prev: pl.BlockSpec(
        (WINDOW, LANES), (lambda i: (jnp.maximum(i - 1, 0), col)) if prev else (lambda i: (i, col)))
    full = pl.BlockSpec((t, LANES), lambda i: (0, 0))
    return pl.pallas_call(
        body, name="swa_bwd", grid=(t // WINDOW,),
        in_specs=[pl.BlockSpec(memory_space=pltpu.SMEM),
                  pl.BlockSpec((WINDOW, D_SWA), lambda i: (i, C_QS // D_SWA)),
                  kvspec(C_KS // LANES, True), kvspec(C_KS // LANES, False),
                  kvspec(C_VS // LANES, True), kvspec(C_VS // LANES, False),
                  pl.BlockSpec((WINDOW, D_SWA), lambda i: (i, 1))],
        out_specs=[pl.BlockSpec((WINDOW, D_SWA), lambda i: (i, 0)), full, full,
                   pl.BlockSpec((1, LANES), lambda i: (0, 0))],
        out_shape=[jax.ShapeDtypeStruct((t, D_SWA), BF16), jax.ShapeDtypeStruct((t, LANES), F32),
                   jax.ShapeDtypeStruct((t, LANES), F32), jax.ShapeDtypeStruct((1, LANES), F32)],
        compiler_params=_cparams(("arbitrary",)),
    )(sinks, proj, proj, proj, proj, proj, dcat)


def _grad_w_in(xb, dproj):
    t = xb.shape[0]
    tm, tn, tk = 1024, 640, min(512, t)
    return _matmul(
        "grad_w_in", xb, dproj, TN, (D_MODEL // tm, D_INP // tn, t // tk),
        pl.BlockSpec((tk, tm), lambda i, j, k: (k, i)),
        pl.BlockSpec((tk, tn), lambda i, j, k: (k, j)),
        jax.ShapeDtypeStruct((D_MODEL, D_INP), F32),
        pl.BlockSpec((tm, tn), lambda i, j, k: (i, j)), (tm, tn))


def _grad_x(dproj, w_in_p, dr1):
    t = dproj.shape[0]
    tm, tk = min(512, t), 640

    def epilogue(acc_ref, extra, outs):
        for r0, n in _row_chunks(tm):
            rs = slice(r0, r0 + n)
            outs[0][rs, :] = ALPHA * extra[0][rs, :] + acc_ref[rs, :]

    row = pl.BlockSpec((tm, D_MODEL), lambda i, j, k: (i, 0))
    return _matmul(
        "grad_x", dproj, w_in_p, NT, (t // tm, 1, D_INP // tk),
        pl.BlockSpec((tm, tk), lambda i, j, k: (i, k)),
        pl.BlockSpec((D_MODEL, tk), lambda i, j, k: (0, k)),
        jax.ShapeDtypeStruct((t, D_MODEL), F32), row, (tm, D_MODEL),
        extra=(dr1,), extra_specs=(row,), epilogue=epilogue)


def _place():
    x, y, c = lax.axis_index("x"), lax.axis_index("y"), lax.axis_index("c")
    chips = [(1 - x, y), (x, 1 - y), (1 - x, 1 - y)]
    return x, y, c, chips


def _all_gather(shards):
    n = len(shards)
    hbm = pl.BlockSpec(memory_space=pl.ANY)

    def body(*refs):
        ins, outs = refs[:n], refs[n:2 * n]
        send, recv, loc = refs[2 * n:]
        x, y, c, chips = _place()
        me, sib = (x, y, c), (x, y, 1 - c)

        def slot(t, px, py, pc):
            return outs[t].at[4 * px + 2 * py + pc]

        def copy(t, k, block, to, src=None):
            return pltpu.make_async_remote_copy(
                src_ref=slot(t, *block) if src is None else src, dst_ref=slot(t, *block),
                send_sem=send.at[7 * t + k], recv_sem=recv.at[7 * t + k], device_id=to, device_id_type=MESH)

        mine = [pltpu.make_async_copy(ins[t], slot(t, *me), loc.at[t]) for t in range(n)]
        for cp in mine:
            cp.start()
        sent = []
        for t in range(n):
            sent.append(copy(t, 0, me, sib, src=ins[t]))
            sent += [copy(t, 1 + j, me, (*chip, c), src=ins[t]) for j, chip in enumerate(chips)]
        for cp in sent:
            cp.start()
        for t in range(n):
            for j, chip in enumerate(chips):
                copy(t, 1 + j, (*chip, c), me).wait_recv()
                fwd = copy(t, 4 + j, (*chip, c), sib)
                fwd.start()
                sent.append(fwd)
        for t in range(n):
            copy(t, 0, sib, me).wait_recv()
            for j, chip in enumerate(chips):
                copy(t, 4 + j, (*chip, 1 - c), me).wait_recv()
        for cp in sent:
            cp.wait_send()
        for cp in mine:
            cp.wait()

    return pl.pallas_call(
        body, name="all_gather_weights",
        in_specs=[hbm] * n, out_specs=[hbm] * n,
        out_shape=[jax.ShapeDtypeStruct((N_DEV,) + s.shape, s.dtype) for s in shards],
        scratch_shapes=[pltpu.SemaphoreType.DMA((7 * n,)), pltpu.SemaphoreType.DMA((7 * n,)),
                        pltpu.SemaphoreType.DMA((n,))],
    )(*shards)


def _exchange_sibling(grads):
    n = len(grads)
    hbm = pl.BlockSpec(memory_space=pl.ANY)

    def body(*refs):
        ins, outs = refs[:n], refs[n:2 * n]
        send, recv = refs[2 * n:]
        x, y, c, _ = _place()
        cps = []
        for t in range(n):
            for q in range(4):
                cps.append(pltpu.make_async_remote_copy(
                    src_ref=ins[t].at[2 * q + (1 - c)], dst_ref=outs[t].at[q],
                    send_sem=send.at[4 * t + q], recv_sem=recv.at[4 * t + q],
                    device_id=(x, y, 1 - c), device_id_type=MESH))
        for cp in cps:
            cp.start()
        for cp in cps:
            cp.wait()

    return pl.pallas_call(
        body, name="exchange_sibling",
        in_specs=[hbm] * n, out_specs=[hbm] * n,
        out_shape=[jax.ShapeDtypeStruct((4,) + g.shape[1:], g.dtype) for g in grads],
        scratch_shapes=[pltpu.SemaphoreType.DMA((4 * n,)), pltpu.SemaphoreType.DMA((4 * n,))],
    )(*grads)


def _exchange_chips(sums):
    n = len(sums)
    hbm = pl.BlockSpec(memory_space=pl.ANY)

    def body(*refs):
        ins, outs = refs[:n], refs[n:2 * n]
        send, recv = refs[2 * n:]
        x, y, c, chips = _place()
        cps = []
        for t in range(n):
            for j, (px, py) in enumerate(chips):
                cps.append(pltpu.make_async_remote_copy(
                    src_ref=ins[t].at[j], dst_ref=outs[t].at[j],
                    send_sem=send.at[3 * t + j], recv_sem=recv.at[3 * t + j],
                    device_id=(px, py, c), device_id_type=MESH))
        for cp in cps:
            cp.start()
        for cp in cps:
            cp.wait()

    return pl.pallas_call(
        body, name="exchange_chips",
        in_specs=[hbm] * n, out_specs=[hbm] * n,
        out_shape=[jax.ShapeDtypeStruct((3,) + s.shape[1:], s.dtype) for s in sums],
        scratch_shapes=[pltpu.SemaphoreType.DMA((3 * n,)), pltpu.SemaphoreType.DMA((3 * n,))],
    )(*sums)


def _pair_sum(name, grad, from_sibling, blocks):
    _, rows, cols = grad.shape
    tr = 256

    def body(blk_ref, g_ref, s_ref, o_ref):
        o_ref[...] = (g_ref[...] + s_ref[...]).astype(BF16)

    return pl.pallas_call(
        body, name=name,
        grid_spec=pltpu.PrefetchScalarGridSpec(
            num_scalar_prefetch=1, grid=(3, rows // tr),
            in_specs=[pl.BlockSpec((None, tr, cols), lambda j, r, br: (br[j], r, 0)),
                      pl.BlockSpec((None, tr, cols), lambda j, r, br: (br[3 + j], r, 0))],
            out_specs=pl.BlockSpec((None, tr, cols), lambda j, r, br: (j, r, 0))),
        out_shape=jax.ShapeDtypeStruct((3, rows, cols), BF16),
        compiler_params=_cparams(("parallel", "parallel")),
    )(blocks, grad, from_sibling)


def _adam_math(g, w, m, v):
    m2 = ADAM_B1 * m + (1.0 - ADAM_B1) * g
    v2 = ADAM_B2 * v + (1.0 - ADAM_B2) * (g * g)
    m_hat = m2 / (1.0 - ADAM_B1 ** ADAM_STEP)
    v_hat = v2 / (1.0 - ADAM_B2 ** ADAM_STEP)
    delta = -ADAM_LR * (m_hat / (jnp.sqrt(v_hat) + ADAM_EPS) + ADAM_WD * w)
    return delta, m2, v2


def _sum_adam(name, grad, from_sibling, from_chips, own, w, m, v):
    rows, cols = w.shape
    tr = 128

    def body(own_ref, p_ref, s_ref, r_ref, w_ref, m_ref, v_ref, g_out, d_out, m_out, v_out):
        g = p_ref[...] + s_ref[...]
        for j in range(3):
            g = g + r_ref[j].astype(F32)
        d, m2, v2 = _adam_math(g, w_ref[...], m_ref[...], v_ref[...])
        g_out[...] = g
        d_out[...] = d
        m_out[...] = m2
        v_out[...] = v2

    blk = pl.BlockSpec((tr, cols), lambda r, cr: (r, 0))
    shp = jax.ShapeDtypeStruct((rows, cols), F32)
    return pl.pallas_call(
        body, name=name,
        grid_spec=pltpu.PrefetchScalarGridSpec(
            num_scalar_prefetch=1, grid=(rows // tr,),
            in_specs=[pl.BlockSpec((None, tr, cols), lambda r, cr: (cr[0], r, 0)),
                      pl.BlockSpec((None, tr, cols), lambda r, cr: (cr[1], r, 0)),
                      pl.BlockSpec((3, tr, cols), lambda r, cr: (0, r, 0)),
                      blk, blk, blk],
            out_specs=[blk, blk, blk, blk]),
        out_shape=[shp, shp, shp, shp],
        compiler_params=_cparams(("parallel",)),
    )(own, grad, from_sibling, from_chips, w, m, v)


def _adam_small(name, g, w, m, v):
    def body(g_ref, w_ref, m_ref, v_ref, d_out, m_out, v_out):
        d, m2, v2 = _adam_math(g_ref[...], w_ref[...], m_ref[...], v_ref[...])
        d_out[...] = d
        m_out[...] = m2
        v_out[...] = v2

    shp = jax.ShapeDtypeStruct(w.shape, F32)
    return pl.pallas_call(body, name=name, out_shape=[shp, shp, shp])(g, w, m, v)


def _all_reduce_small(pack):
    rows = pack.shape[0]

    def body(in_ref, out_ref, slots, send, recv):
        x, y, c, _ = _place()
        me = 4 * x + 2 * y + c
        slots[me] = in_ref[...]
        cps = []
        for k in range(1, N_DEV):
            dx, dy, dc = (k >> 2) & 1, (k >> 1) & 1, k & 1
            to = (jnp.bitwise_xor(x, dx), jnp.bitwise_xor(y, dy), jnp.bitwise_xor(c, dc))
            cps.append(pltpu.make_async_remote_copy(
                src_ref=in_ref, dst_ref=slots.at[me], send_sem=send.at[k - 1], recv_sem=recv.at[k - 1],
                device_id=to, device_id_type=MESH))
        for cp in cps:
            cp.start()
        for cp in cps:
            cp.wait()
        acc = slots[0]
        for d in range(1, N_DEV):
            acc = acc + slots[d]
        out_ref[...] = acc

    return pl.pallas_call(
        body, name="all_reduce_small",
        in_specs=[pl.BlockSpec(memory_space=pltpu.VMEM)],
        out_specs=pl.BlockSpec(memory_space=pltpu.VMEM),
        out_shape=jax.ShapeDtypeStruct((rows, LANES), F32),
        scratch_shapes=[pltpu.VMEM((N_DEV, rows, LANES), F32),
                        pltpu.SemaphoreType.DMA((N_DEV - 1,)), pltpu.SemaphoreType.DMA((N_DEV - 1,))],
    )(pack)


def _rows128(a):
    flat = a.reshape(-1)
    padn = (-flat.shape[0]) % (8 * LANES)
    if padn:
        flat = jnp.concatenate([flat, jnp.zeros((padn,), flat.dtype)])
    return flat.reshape(-1, LANES)


def kernel(x, w_in, w_gk2, b_gk, gla_norm_w, swa_sinks, w_out, ln1_g, ln1_b, w_up, w_down, ln2_g, ln2_b, loss_target, m_w_in, m_w_gk2, m_b_gk, m_gla_norm_w, m_swa_sinks, m_w_out, m_ln1_g, m_ln1_b, m_w_up, m_w_down, m_ln2_g, m_ln2_b, v_w_in, v_w_gk2, v_b_gk, v_gla_norm_w, v_swa_sinks, v_w_out, v_ln1_g, v_ln1_b, v_w_up, v_w_down, v_ln2_g, v_ln2_b):
    xc, yc, cc = lax.axis_index("x"), lax.axis_index("y"), lax.axis_index("c")
    me = 4 * xc + 2 * yc + cc

    x2 = x[0]
    t = x2.shape[0]
    xb = x2.astype(BF16)
    target = loss_target[0]

    gathered = _all_gather([w_in[0].astype(BF16), w_gk2[0].astype(BF16), w_out[0].astype(BF16),
                            w_up[0].astype(BF16), w_down[0].astype(BF16)])
    w_in_p = _to_padded_cols(gathered[0].transpose(1, 0, 2).reshape(D_MODEL, D_IN))
    w2 = gathered[1].transpose(1, 0, 2).reshape(GATE_RANK, GLA_KW)
    w2p = jnp.concatenate([w2, jnp.zeros((LANES - GATE_RANK, GLA_KW), BF16)], axis=0)
    w_out_f = gathered[2].reshape(D_MODEL, D_MODEL)
    w_up_f = gathered[3]
    w_down_f = gathered[4].reshape(D_FF, D_MODEL)

    proj = _proj(xb, w_in_p)
    rt = min(256, t)
    ii = jnp.arange(rt)
    tri = ((ii[:, None] // GC == ii[None, :] // GC) & (ii[None, :] <= ii[:, None])).astype(F32)
    bloc, dgdz = _gate_fwd(proj, w2p, b_gk, tri)
    o_raw, states = _gla_fwd(proj, bloc)
    gla_out = _gla_norm_fwd(o_raw, proj, gla_norm_w)
    swa_out = _swa_fwd(proj, swa_sinks[0])
    cat = jnp.concatenate([gla_out, swa_out], axis=-1)
    r1, h1, h1b = _mix_ln1(cat, w_out_f, x2, ln1_g, ln1_b)
    a_act, hdn = _mlp_up(h1b, w_up_f)
    dr2, dr2b, g_ln2_g, g_ln2_b, loss_part = _mlp_down_loss(hdn, w_down_f, h1, target, ln2_g, ln2_b)

    du = _mlp_down_bwd(dr2b, w_down_f, a_act)
    gw_down = _grad_w_down(hdn, dr2b)
    gw_up = _grad_w_up(h1b, du)
    dr1, g_ln1_g, g_ln1_b = _mlp_up_bwd_ln1(du, w_up_f, dr2, r1, ln1_g)
    dcat = _dcat(dr1, w_out_f)
    gw_out = _grad_w_out(cat, dr1)
    do_raw, dg_g, g_norm_w = _gla_norm_bwd(dcat, o_raw, proj, gla_norm_w)
    jj = jnp.arange(GC)
    triu = (jj[None, :] >= jj[:, None]).astype(F32)
    dq_g, dk_g, dv_g, dgk = _gla_bwd(proj, bloc, do_raw, states, triu)
    dlo, gw2_p, g_b_gk = _gate_bwd(dgk, dgdz, proj, w2p)
    dq_s, dk_s, dv_s, g_sinks = _swa_bwd(proj, dcat, swa_sinks[0])
    dproj = jnp.concatenate([dq_g, dk_g, dv_g, dg_g, dq_s, dk_s.astype(BF16), dv_s.astype(BF16), dlo], axis=-1)
    gw_in_p = _grad_w_in(xb, dproj)
    grad_x = _grad_x(dproj, w_in_p, dr1)

    gw_in_b = _from_padded_cols(gw_in_p).reshape(D_MODEL, N_DEV, D_IN // N_DEV).transpose(1, 0, 2)
    parts = [gw_in_b, gw_out.reshape(N_DEV, D_MODEL // N_DEV, D_MODEL), gw_up,
             gw_down.reshape(N_DEV, D_FF // N_DEV, D_MODEL)]
    from_sib = _exchange_sibling(parts)
    names = ["w_in", "w_out", "w_up", "w_down"]
    others = [2 * (1 - xc) + yc, 2 * xc + (1 - yc), 2 * (1 - xc) + (1 - yc)]
    blocks = jnp.stack([2 * q + cc for q in others] + others).astype(jnp.int32)
    own = jnp.stack([me, 2 * xc + yc]).astype(jnp.int32)
    sums = [_pair_sum("pair_sum_" + nm, g, s, blocks) for nm, g, s in zip(names, parts, from_sib)]
    from_chips = _exchange_chips(sums)
    big = {}
    for nm, g, s, r, w, m, v in zip(names, parts, from_sib, from_chips, (w_in, w_out, w_up, w_down),
                                    (m_w_in, m_w_out, m_w_up, m_w_down), (v_w_in, v_w_out, v_w_up, v_w_down)):
        big[nm] = [o[None] for o in _sum_adam("sum_adam_" + nm, g, s, r, own, w[0], m[0], v[0])]

    pieces = [loss_part, g_b_gk, g_norm_w, g_sinks[:, :SWA_HEADS], g_ln1_g, g_ln1_b, g_ln2_g, g_ln2_b,
              gw2_p[:GATE_RANK]]
    pack = jnp.concatenate([_rows128(p) for p in pieces], axis=0)
    tot = _all_reduce_small(pack)
    sizes = [p.size for p in pieces]
    offs = [0]
    for p in pieces:
        offs.append(offs[-1] + _rows128(p).shape[0])
    unpack = lambda i, shape: tot[offs[i]:offs[i + 1]].reshape(-1)[:sizes[i]].reshape(shape)
    loss = tot[0, 0]
    small_names = ["b_gk", "gla_norm_w", "swa_sinks", "ln1_g", "ln1_b", "ln2_g", "ln2_b"]
    small_w = dict(b_gk=(b_gk, m_b_gk, v_b_gk), gla_norm_w=(gla_norm_w, m_gla_norm_w, v_gla_norm_w),
                   swa_sinks=(swa_sinks, m_swa_sinks, v_swa_sinks), ln1_g=(ln1_g, m_ln1_g, v_ln1_g),
                   ln1_b=(ln1_b, m_ln1_b, v_ln1_b), ln2_g=(ln2_g, m_ln2_g, v_ln2_g),
                   ln2_b=(ln2_b, m_ln2_b, v_ln2_b))
    small_g = {nm: unpack(1 + i, small_w[nm][0].shape) for i, nm in enumerate(small_names)}
    g_pack = jnp.concatenate([_rows128(small_g[nm]) for nm in small_names], axis=0)
    wmv = [jnp.concatenate([_rows128(small_w[nm][k]) for nm in small_names], axis=0) for k in range(3)]
    small_out = _adam_small("adam_replicated", g_pack, *wmv)
    srow = [0]
    for nm in small_names:
        srow.append(srow[-1] + _rows128(small_w[nm][0]).shape[0])
    small = {}
    for i, nm in enumerate(small_names):
        shape = small_w[nm][0].shape
        n = small_w[nm][0].size
        small[nm] = [small_g[nm]] + [o[srow[i]:srow[i + 1]].reshape(-1)[:n].reshape(shape) for o in small_out]

    gw2_full = unpack(8, (GATE_RANK, GLA_KW))
    gw2_loc = lax.dynamic_slice_in_dim(gw2_full, me * (GLA_KW // N_DEV), GLA_KW // N_DEV, axis=1)
    gk2_out = _adam_small("adam_w_gk2", gw2_loc, w_gk2[0], m_w_gk2[0], v_w_gk2[0])
    big["w_gk2"] = [gw2_loc[None]] + [o[None] for o in gk2_out]

    order = ["w_in", "w_gk2", "b_gk", "gla_norm_w", "swa_sinks", "w_out", "ln1_g", "ln1_b", "w_up", "w_down",
             "ln2_g", "ln2_b"]
    res = {**big, **small}
    outs = [loss, grad_x[None]]
    for k in range(4):
        outs += [res[nm][k] for nm in order]
    return tuple(outs)
```

```python
import functools

import jax
import jax.numpy as jnp
from jax import lax
from jax.experimental import pallas as pl
from jax.experimental.pallas import tpu as pltpu

F32 = jnp.float32
BF16 = jnp.bfloat16

N_DEV = 8
D_MODEL = 2048
D_FF = 8192
GLA_HEADS = 4
GLA_DK = 128
GLA_DV = 256
GLA_KW = 512
D_GLA = 1024
GATE_RANK = 16
GATE_TAU = 16.0
SWA_HEADS = 16
SWA_DH = 64
WINDOW = 128
D_SWA = 1024
D_IN = 4368
ALPHA = 2.0 ** 0.25
LN_EPS = 1e-5
RMS_EPS = 1e-5
ADAM_LR = 0.001
ADAM_B1 = 0.9
ADAM_B2 = 0.999
ADAM_EPS = 1e-08
ADAM_WD = 0.01
ADAM_STEP = 10

C_QG, C_KG, C_VG, C_GG, C_QS, C_KS, C_VS, C_LO = 0, 512, 1024, 2048, 3072, 4096, 4224, 4352
D_INP = 4480
LANES = 128
GC = 16
NEG = -1e30

NN = ((1,), (0,))
NT = ((1,), (1,))
TN = ((0,), (0,))

VMEM_LIMIT = 52 * 1024 * 1024
MESH = pl.DeviceIdType.MESH


def _dot(a, b, dn, precision=None):
    return lax.dot_general(a, b, (dn, ((), ())), preferred_element_type=F32, precision=precision)


def _bf16_round(v):
    return v.astype(BF16).astype(F32)


def _cparams(dims):
    return pltpu.CompilerParams(dimension_semantics=dims, vmem_limit_bytes=VMEM_LIMIT)


def _dep_operand(dep):
    if dep is None:
        return (), ()
    return (dep,), (pl.BlockSpec(dep.shape, lambda *_: (0,) * dep.ndim),)


def _to_padded_cols(w):
    pad = jnp.zeros(w.shape[:-1] + (D_INP - D_IN,), w.dtype)
    return jnp.concatenate([w[..., :3072], w[..., 3088:], w[..., 3072:3088], pad], axis=-1)


def _from_padded_cols(g):
    return jnp.concatenate([g[..., :3072], g[..., C_LO:C_LO + GATE_RANK], g[..., 3072:C_LO]], axis=-1)


def _matmul(name, a, b, dn, grid, a_spec, b_spec, out_shape, out_specs, acc_shape, *,
            extra=(), extra_specs=(), epilogue=None, dims=("parallel", "parallel", "arbitrary"), dep=None):
    nk = grid[2]
    n_extra = len(extra)
    deps, dep_specs = _dep_operand(dep)

    def body(a_ref, b_ref, *rest):
        extra_refs = rest[:n_extra]
        out_refs = rest[n_extra + len(deps):-1]
        acc_ref = rest[-1]
        part = _dot(a_ref[...].astype(BF16), b_ref[...].astype(BF16), dn)

        def finish():
            if epilogue is None:
                out_refs[0][...] = acc_ref[...].astype(out_refs[0].dtype)
            else:
                epilogue(acc_ref, extra_refs, out_refs)

        if nk == 1:
            acc_ref[...] = part
            finish()
        else:
            k = pl.program_id(2)

            @pl.when(k == 0)
            def _():
                acc_ref[...] = part

            @pl.when(k > 0)
            def _():
                acc_ref[...] += part

            @pl.when(k == nk - 1)
            def _():
                finish()

    return pl.pallas_call(
        body, name=name, grid=grid,
        in_specs=[a_spec, b_spec, *extra_specs, *dep_specs],
        out_specs=out_specs, out_shape=out_shape,
        scratch_shapes=[pltpu.VMEM(acc_shape, F32)],
        compiler_params=_cparams(dims),
    )(a, b, *extra, *deps)


def _row_chunks(rows, step=128):
    step = min(step, rows)
    return [(r, step) for r in range(0, rows, step)]


def _ln_stats(r):
    mu = jnp.mean(r, axis=-1, keepdims=True)
    xc = r - mu
    var = jnp.mean(xc * xc, axis=-1, keepdims=True)
    rstd = lax.rsqrt(var + LN_EPS)
    return xc * rstd, rstd


def _ln_bwd(dy_g, xhat, rstd):
    m1 = jnp.mean(dy_g, axis=-1, keepdims=True)
    m2 = jnp.mean(dy_g * xhat, axis=-1, keepdims=True)
    return rstd * (dy_g - m1 - xhat * m2)


def _proj(xb, w_in_p, dep=None):
    t = xb.shape[0]
    tm, tn = min(1024, t), 640
    return _matmul(
        "proj", xb, w_in_p, NN, (t // tm, D_INP // tn, 1),
        pl.BlockSpec((tm, D_MODEL), lambda i, j, k: (i, 0)),
        pl.BlockSpec((D_MODEL, tn), lambda i, j, k: (0, j)),
        jax.ShapeDtypeStruct((t, D_INP), F32),
        pl.BlockSpec((tm, tn), lambda i, j, k: (i, j)),
        (tm, tn), dep=dep)


def _gate_fwd(proj, w2p, b_gk, tri):
    t = proj.shape[0]
    r = tri.shape[0]

    def body(lo_ref, w_ref, b_ref, tri_ref, bloc_ref, dgdz_ref):
        z = _dot(lo_ref[...].astype(BF16), w_ref[...], NN) + b_ref[...]
        e = jnp.exp(-jnp.abs(z))
        gk = (jnp.minimum(z, 0.0) - jnp.log1p(e)) * (1.0 / GATE_TAU)
        inv = 1.0 / (1.0 + e)
        dgdz_ref[...] = jnp.where(z >= 0.0, e * inv, inv) * (1.0 / GATE_TAU)
        bloc_ref[...] = _dot(tri_ref[...], gk, NN, precision=lax.Precision.HIGHEST)

    return pl.pallas_call(
        body, name="gate_fwd", grid=(t // r,),
        in_specs=[pl.BlockSpec((r, LANES), lambda i: (i, C_LO // LANES)),
                  pl.BlockSpec((LANES, GLA_KW), lambda i: (0, 0)),
                  pl.BlockSpec((1, GLA_KW), lambda i: (0, 0)),
                  pl.BlockSpec((r, r), lambda i: (0, 0))],
        out_specs=[pl.BlockSpec((r, GLA_KW), lambda i: (i, 0)),
                   pl.BlockSpec((r, GLA_KW), lambda i: (i, 0))],
        out_shape=[jax.ShapeDtypeStruct((t, GLA_KW), F32), jax.ShapeDtypeStruct((t, GLA_KW), F32)],
        compiler_params=_cparams(("parallel",)),
    )(proj, w2p, b_gk, tri)


def _gla_fwd(proj, bloc):
    t = proj.shape[0]
    r = min(256, t)
    ncb = r // GC
    scale = GLA_DK ** -0.5

    def body(q_ref, k_ref, v_ref, b_ref, o_ref, st_ref, s_scr):
        @pl.when(pl.program_id(1) == 0)
        def _():
            s_scr[...] = jnp.zeros_like(s_scr)

        rows = lax.broadcasted_iota(jnp.int32, (GC, 1), 0)

        def chunk(c, carry):
            r0 = pl.multiple_of(c * GC, GC)
            q = q_ref[pl.ds(r0, GC), :] * scale
            k = k_ref[pl.ds(r0, GC), :]
            v = v_ref[pl.ds(r0, GC), :]
            b = b_ref[pl.ds(r0, GC), :]
            st = s_scr[...]
            st_ref[c] = st
            bl = b[GC - 1:GC, :]
            kr = _bf16_round(k)
            vr = _bf16_round(v)
            o = _dot((q * jnp.exp(b)).astype(BF16), st.astype(BF16), NT)
            for j in range(GC):
                w = jnp.exp(jnp.where(rows >= j, b - b[j:j + 1, :], NEG))
                a = jnp.sum(_bf16_round(q * w) * kr[j:j + 1, :], axis=-1, keepdims=True)
                o = o + _bf16_round(a) * vr[j:j + 1, :]
            o_ref[pl.ds(r0, GC), :] = o
            kd = k * jnp.exp(bl - b)
            s_scr[...] = st * jnp.exp(bl) + _dot(v.astype(BF16), kd.astype(BF16), TN)
            return carry

        lax.fori_loop(0, ncb, chunk, 0)

    return pl.pallas_call(
        body, name="gla_fwd", grid=(GLA_HEADS, t // r),
        in_specs=[pl.BlockSpec((r, GLA_DK), lambda h, i: (i, C_QG // GLA_DK + h)),
                  pl.BlockSpec((r, GLA_DK), lambda h, i: (i, C_KG // GLA_DK + h)),
                  pl.BlockSpec((r, GLA_DV), lambda h, i: (i, C_VG // GLA_DV + h)),
                  pl.BlockSpec((r, GLA_DK), lambda h, i: (i, h))],
        out_specs=[pl.BlockSpec((r, GLA_DV), lambda h, i: (i, h)),
                   pl.BlockSpec((ncb, None, GLA_DV, GLA_DK), lambda h, i: (i, h, 0, 0))],
        out_shape=[jax.ShapeDtypeStruct((t, D_GLA), F32),
                   jax.ShapeDtypeStruct((t // GC, GLA_HEADS, GLA_DV, GLA_DK), F32)],
        scratch_shapes=[pltpu.VMEM((GLA_DV, GLA_DK), F32)],
        compiler_params=_cparams(("parallel", "arbitrary")),
    )(proj, proj, proj, bloc)


def _gla_norm_fwd(o_raw, proj, norm_w):
    t = o_raw.shape[0]
    r = min(512, t)

    def body(o_ref, g_ref, w_ref, out_ref):
        w = w_ref[...]
        for h in range(GLA_HEADS):
            sl = slice(h * GLA_DV, (h + 1) * GLA_DV)
            o = o_ref[:, sl]
            g = g_ref[:, sl]
            on = o * lax.rsqrt(jnp.mean(o * o, axis=-1, keepdims=True) + RMS_EPS)
            out_ref[:, sl] = (on * w * (g * jax.nn.sigmoid(g))).astype(BF16)

    return pl.pallas_call(
        body, name="gla_norm_fwd", grid=(t // r,),
        in_specs=[pl.BlockSpec((r, D_GLA), lambda i: (i, 0)),
                  pl.BlockSpec((r, D_GLA), lambda i: (i, C_GG // D_GLA)),
                  pl.BlockSpec((1, GLA_DV), lambda i: (0, 0))],
        out_specs=pl.BlockSpec((r, D_GLA), lambda i: (i, 0)),
        out_shape=jax.ShapeDtypeStruct((t, D_GLA), BF16),
        compiler_params=_cparams(("parallel",)),
    )(o_raw, proj, norm_w)


def _swa_masks(i):
    qi = lax.broadcasted_iota(jnp.int32, (WINDOW, 1), 0)
    kj = lax.broadcasted_iota(jnp.int32, (1, 2 * WINDOW), 1)
    valid = (kj > qi) & (kj <= qi + WINDOW) & ((i > 0) | (kj >= WINDOW))
    lo = lax.broadcasted_iota(jnp.int32, (1, LANES), 1) < SWA_DH
    return valid, lo


def _dup_half(x, lo, kv):
    xr = pltpu.roll(x, SWA_DH, axis=1)
    return jnp.where(lo, x, xr) if kv == 0 else jnp.where(lo, xr, x)


def _swa_probs(qm, kdup, valid, sink):
    s = _dot(qm, kdup, NT) * (SWA_DH ** -0.5)
    s = jnp.where(valid, s, NEG)
    m = jnp.maximum(jnp.max(s, axis=-1, keepdims=True), sink)
    p = jnp.exp(s - m)
    es = jnp.exp(sink - m)
    inv = 1.0 / (jnp.sum(p, axis=-1, keepdims=True) + es)
    return p * inv, es * inv


def _swa_fwd(proj, sinks, dep=None):
    t = proj.shape[0]

    deps, dep_specs = _dep_operand(dep)

    def body(sink_ref, q_ref, kp_ref, kc_ref, vp_ref, vc_ref, *rest):
        o_ref = rest[-1]
        i = pl.program_id(0)
        valid, lo = _swa_masks(i)
        kb = jnp.concatenate([kp_ref[...], kc_ref[...]], axis=0)
        vb = jnp.concatenate([vp_ref[...], vc_ref[...]], axis=0)
        for kv in range(2):
            kdup = _dup_half(kb, lo, kv).astype(BF16)
            vdup = _dup_half(vb, lo, kv).astype(BF16)
            for p in range(4):
                sl = slice(LANES * (4 * kv + p), LANES * (4 * kv + p + 1))
                qp = q_ref[:, sl]
                halves = []
                for e in range(2):
                    qm = jnp.where(lo if e == 0 else ~lo, qp, 0.0).astype(BF16)
                    pn, _ = _swa_probs(qm, kdup, valid, sink_ref[8 * kv + 2 * p + e])
                    halves.append(_dot(pn.astype(BF16), vdup, NN))
                o_ref[:, sl] = jnp.where(lo, halves[0], halves[1]).astype(BF16)

    kvspec = lambda col, prev: pl.BlockSpec(
        (WINDOW, LANES), (lambda i: (jnp.maximum(i - 1, 0), col)) if prev else (lambda i: (i, col)))
    return pl.pallas_call(
        body, name="swa_fwd", grid=(t // WINDOW,),
        in_specs=[pl.BlockSpec(memory_space=pltpu.SMEM),
                  pl.BlockSpec((WINDOW, D_SWA), lambda i: (i, C_QS // D_SWA)),
                  kvspec(C_KS // LANES, True), kvspec(C_KS // LANES, False),
                  kvspec(C_VS // LANES, True), kvspec(C_VS // LANES, False), *dep_specs],
        out_specs=pl.BlockSpec((WINDOW, D_SWA), lambda i: (i, 0)),
        out_shape=jax.ShapeDtypeStruct((t, D_SWA), BF16),
        compiler_params=_cparams(("parallel",)),
    )(sinks, proj, proj, proj, proj, proj, *deps)


def _mix_ln1(cat, w_out, x, ln_g, ln_b):
    t = cat.shape[0]
    tm, tk = min(512, t), 512

    def epilogue(acc_ref, extra, outs):
        x_ref, g_ref, b_ref = extra
        r1_ref, h1_ref, h1b_ref = outs
        for r0, n in _row_chunks(tm):
            rs = slice(r0, r0 + n)
            r1 = ALPHA * x_ref[rs, :] + acc_ref[rs, :]
            xhat, _ = _ln_stats(r1)
            h = xhat * g_ref[...] + b_ref[...]
            r1_ref[rs, :] = r1
            h1_ref[rs, :] = h
            h1b_ref[rs, :] = h.astype(BF16)

    row = pl.BlockSpec((tm, D_MODEL), lambda i, j, k: (i, 0))
    vec = pl.BlockSpec((1, D_MODEL), lambda i, j, k: (0, 0))
    return _matmul(
        "mix_ln1", cat, w_out, NN, (t // tm, 1, D_MODEL // tk),
        pl.BlockSpec((tm, tk), lambda i, j, k: (i, k)),
        pl.BlockSpec((tk, D_MODEL), lambda i, j, k: (k, 0)),
        [jax.ShapeDtypeStruct((t, D_MODEL), F32), jax.ShapeDtypeStruct((t, D_MODEL), F32),
         jax.ShapeDtypeStruct((t, D_MODEL), BF16)],
        [row, row, row], (tm, D_MODEL),
        extra=(x, ln_g, ln_b), extra_specs=(row, vec, vec), epilogue=epilogue)


def _mlp_up(h1b, w_up):
    t = h1b.shape[0]
    tm, tn = min(1024, t), 1024

    def epilogue(acc_ref, extra, outs):
        a_ref, hdn_ref = outs
        for r0, n in _row_chunks(tm, 256):
            rs = slice(r0, r0 + n)
            a = jnp.maximum(acc_ref[rs, :], 0.0)
            a_ref[rs, :] = a.astype(BF16)
            hdn_ref[rs, :] = (a * a).astype(BF16)

    out = pl.BlockSpec((tm, tn), lambda i, j, k: (i, j))
    return _matmul(
        "mlp_up", h1b, w_up, NN, (t // tm, D_FF // tn, 1),
        pl.BlockSpec((tm, D_MODEL), lambda i, j, k: (i, 0)),
        pl.BlockSpec((None, D_MODEL, tn), lambda i, j, k: (j, 0, 0)),
        [jax.ShapeDtypeStruct((t, D_FF), BF16), jax.ShapeDtypeStruct((t, D_FF), BF16)],
        [out, out], (tm, tn), epilogue=epilogue)


def _mlp_down_loss(hdn, w_down, h1, target, ln_g, ln_b):
    t = hdn.shape[0]
    tm, tk = min(512, t), 512

    def epilogue(acc_ref, extra, outs):
        h1_ref, t_ref, g_ref, b_ref = extra
        dr2_ref, dr2b_ref, gg_ref, gb_ref, loss_ref = outs

        @pl.when(pl.program_id(0) == 0)
        def _():
            gg_ref[...] = jnp.zeros_like(gg_ref)
            gb_ref[...] = jnp.zeros_like(gb_ref)
            loss_ref[...] = jnp.zeros_like(loss_ref)

        for r0, n in _row_chunks(tm):
            rs = slice(r0, r0 + n)
            xhat, rstd = _ln_stats(ALPHA * h1_ref[rs, :] + acc_ref[rs, :])
            err = xhat * g_ref[...] + b_ref[...] - t_ref[rs, :]
            loss_ref[...] += 0.5 * jnp.sum(jnp.mean(err * err, axis=-1, keepdims=True))
            dy = err * (1.0 / D_MODEL)
            gg_ref[...] += jnp.sum(dy * xhat, axis=0, keepdims=True)
            gb_ref[...] += jnp.sum(dy, axis=0, keepdims=True)
            dr2 = _ln_bwd(dy * g_ref[...], xhat, rstd)
            dr2_ref[rs, :] = dr2
            dr2b_ref[rs, :] = dr2.astype(BF16)

    row = pl.BlockSpec((tm, D_MODEL), lambda i, j, k: (i, 0))
    vec = pl.BlockSpec((1, D_MODEL), lambda i, j, k: (0, 0))
    return _matmul(
        "mlp_down_loss", hdn, w_down, NN, (t // tm, 1, D_FF // tk),
        pl.BlockSpec((tm, tk), lambda i, j, k: (i, k)),
        pl.BlockSpec((tk, D_MODEL), lambda i, j, k: (k, 0)),
        [jax.ShapeDtypeStruct((t, D_MODEL), F32), jax.ShapeDtypeStruct((t, D_MODEL), BF16),
         jax.ShapeDtypeStruct((1, D_MODEL), F32), jax.ShapeDtypeStruct((1, D_MODEL), F32),
         jax.ShapeDtypeStruct((1, LANES), F32)],
        [row, row, vec, vec, pl.BlockSpec((1, LANES), lambda i, j, k: (0, 0))],
        (tm, D_MODEL),
        extra=(h1, target, ln_g, ln_b), extra_specs=(row, row, vec, vec), epilogue=epilogue,
        dims=("arbitrary", "arbitrary", "arbitrary"))


def _mlp_down_bwd(dr2b, w_down, a_act):
    t = dr2b.shape[0]
    tm, tn = min(1024, t), 1024

    def epilogue(acc_ref, extra, outs):
        (a_ref,) = extra
        for r0, n in _row_chunks(tm, 256):
            rs = slice(r0, r0 + n)
            outs[0][rs, :] = (acc_ref[rs, :] * (2.0 * a_ref[rs, :].astype(F32))).astype(BF16)

    blk = pl.BlockSpec((tm, tn), lambda i, j, k: (i, j))
    return _matmul(
        "mlp_down_bwd", dr2b, w_down, NT, (t // tm, D_FF // tn, 1),
        pl.BlockSpec((tm, D_MODEL), lambda i, j, k: (i, 0)),
        pl.BlockSpec((tn, D_MODEL), lambda i, j, k: (j, 0)),
        jax.ShapeDtypeStruct((t, D_FF), BF16), blk, (tm, tn),
        extra=(a_act,), extra_specs=(blk,), epilogue=epilogue)


def _grad_w_down(hdn, dr2b):
    t = hdn.shape[0]
    tm, tn, tk = 1024, 2048, min(512, t)
    return _matmul(
        "grad_w_down", hdn, dr2b, TN, (D_FF // tm, D_MODEL // tn, t // tk),
        pl.BlockSpec((tk, tm), lambda i, j, k: (k, i)),
        pl.BlockSpec((tk, tn), lambda i, j, k: (k, j)),
        jax.ShapeDtypeStruct((D_FF, D_MODEL), F32),
        pl.BlockSpec((tm, tn), lambda i, j, k: (i, j)), (tm, tn))


def _grad_w_up(h1b, du, dep=None):
    t = h1b.shape[0]
    tm, tn, tk = 1024, 1024, min(512, t)
    return _matmul(
        "grad_w_up", h1b, du, TN, (N_DEV, D_MODEL // tm, t // tk),
        pl.BlockSpec((tk, tm), lambda i, j, k: (k, j)),
        pl.BlockSpec((tk, tn), lambda i, j, k: (k, i)),
        jax.ShapeDtypeStruct((N_DEV, D_MODEL, D_FF // N_DEV), F32),
        pl.BlockSpec((None, tm, tn), lambda i, j, k: (i, j, 0)), (tm, tn), dep=dep)


def _mlp_up_bwd_ln1(du, w_up, dr2, r1, ln_g, dep=None):
    t = du.shape[0]
    tm, tk = min(256, t), D_FF // N_DEV

    def epilogue(acc_ref, extra, outs):
        dr2_ref, r1_ref, g_ref = extra
        dr1_ref, gg_ref, gb_ref = outs

        @pl.when(pl.program_id(0) == 0)
        def _():
            gg_ref[...] = jnp.zeros_like(gg_ref)
            gb_ref[...] = jnp.zeros_like(gb_ref)

        for r0, n in _row_chunks(tm):
            rs = slice(r0, r0 + n)
            dh1 = ALPHA * dr2_ref[rs, :] + acc_ref[rs, :]
            xhat, rstd = _ln_stats(r1_ref[rs, :])
            gg_ref[...] += jnp.sum(dh1 * xhat, axis=0, keepdims=True)
            gb_ref[...] += jnp.sum(dh1, axis=0, keepdims=True)
            dr1_ref[rs, :] = _ln_bwd(dh1 * g_ref[...], xhat, rstd)

    row = pl.BlockSpec((tm, D_MODEL), lambda i, j, k: (i, 0))
    vec = pl.BlockSpec((1, D_MODEL), lambda i, j, k: (0, 0))
    return _matmul(
        "mlp_up_bwd_ln1", du, w_up, NT, (t // tm, 1, N_DEV),
        pl.BlockSpec((tm, tk), lambda i, j, k: (i, k)),
        pl.BlockSpec((None, D_MODEL, tk), lambda i, j, k: (k, 0, 0)),
        [jax.ShapeDtypeStruct((t, D_MODEL), F32), jax.ShapeDtypeStruct((1, D_MODEL), F32),
         jax.ShapeDtypeStruct((1, D_MODEL), F32)],
        [row, vec, vec], (tm, D_MODEL),
        extra=(dr2, r1, ln_g), extra_specs=(row, row, vec), epilogue=epilogue,
        dims=("arbitrary", "arbitrary", "arbitrary"), dep=dep)


def _dcat(dr1, w_out):
    t = dr1.shape[0]
    tm, tn = min(512, t), 1024
    return _matmul(
        "dcat", dr1, w_out, NT, (t // tm, D_MODEL // tn, 1),
        pl.BlockSpec((tm, D_MODEL), lambda i, j, k: (i, 0)),
        pl.BlockSpec((tn, D_MODEL), lambda i, j, k: (j, 0)),
        jax.ShapeDtypeStruct((t, D_MODEL), F32),
        pl.BlockSpec((tm, tn), lambda i, j, k: (i, j)), (tm, tn))


def _grad_w_out(cat, dr1, dep=None):
    t = cat.shape[0]
    tm, tn, tk = 1024, 1024, min(512, t)
    return _matmul(
        "grad_w_out", cat, dr1, TN, (D_MODEL // tm, D_MODEL // tn, t // tk),
        pl.BlockSpec((tk, tm), lambda i, j, k: (k, i)),
        pl.BlockSpec((tk, tn), lambda i, j, k: (k, j)),
        jax.ShapeDtypeStruct((D_MODEL, D_MODEL), F32),
        pl.BlockSpec((tm, tn), lambda i, j, k: (i, j)), (tm, tn), dep=dep)


def _gla_norm_bwd(dcat, o_raw, proj, norm_w, dep=None):
    t = o_raw.shape[0]
    r = min(512, t)

    deps, dep_specs = _dep_operand(dep)

    def body(d_ref, o_ref, g_ref, w_ref, *rest):
        do_ref, dg_ref, dw_ref = rest[len(deps):]

        @pl.when(pl.program_id(0) == 0)
        def _():
            dw_ref[...] = jnp.zeros_like(dw_ref)

        w = w_ref[...]
        dw = jnp.zeros((1, GLA_DV), F32)
        for h in range(GLA_HEADS):
            sl = slice(h * GLA_DV, (h + 1) * GLA_DV)
            o = o_ref[:, sl]
            g = g_ref[:, sl]
            d = d_ref[:, sl]
            rr = lax.rsqrt(jnp.mean(o * o, axis=-1, keepdims=True) + RMS_EPS)
            on = o * rr
            sg = jax.nn.sigmoid(g)
            sil = g * sg
            dg_ref[:, sl] = (d * on * w * (sg * (1.0 + g * (1.0 - sg)))).astype(BF16)
            dw = dw + jnp.sum(d * on * sil, axis=0, keepdims=True)
            don = d * w * sil
            do_ref[:, sl] = rr * (don - on * jnp.mean(don * on, axis=-1, keepdims=True))
        dw_ref[...] += dw

    return pl.pallas_call(
        body, name="gla_norm_bwd", grid=(t // r,),
        in_specs=[pl.BlockSpec((r, D_GLA), lambda i: (i, 0)),
                  pl.BlockSpec((r, D_GLA), lambda i: (i, 0)),
                  pl.BlockSpec((r, D_GLA), lambda i: (i, C_GG // D_GLA)),
                  pl.BlockSpec((1, GLA_DV), lambda i: (0, 0)), *dep_specs],
        out_specs=[pl.BlockSpec((r, D_GLA), lambda i: (i, 0)),
                   pl.BlockSpec((r, D_GLA), lambda i: (i, 0)),
                   pl.BlockSpec((1, GLA_DV), lambda i: (0, 0))],
        out_shape=[jax.ShapeDtypeStruct((t, D_GLA), F32), jax.ShapeDtypeStruct((t, D_GLA), BF16),
                   jax.ShapeDtypeStruct((1, GLA_DV), F32)],
        compiler_params=_cparams(("arbitrary",)),
    )(dcat, o_raw, proj, norm_w, *deps)


def _gla_bwd(proj, bloc, do_raw, states, triu):
    t = proj.shape[0]
    r = min(256, t)
    ncb = r // GC
    nb = t // r
    scale = GLA_DK ** -0.5

    def body(q_ref, k_ref, v_ref, b_ref, do_ref, st_ref, u_ref, dq_ref, dk_ref, dv_ref, dg_ref, ds_scr):
        @pl.when(pl.program_id(1) == 0)
        def _():
            ds_scr[...] = jnp.zeros_like(ds_scr)

        rows = lax.broadcasted_iota(jnp.int32, (GC, 1), 0)

        def chunk(cc, carry):
            c = ncb - 1 - cc
            r0 = pl.multiple_of(c * GC, GC)
            q = q_ref[pl.ds(r0, GC), :] * scale
            k = k_ref[pl.ds(r0, GC), :]
            v = v_ref[pl.ds(r0, GC), :]
            b = b_ref[pl.ds(r0, GC), :]
            do = do_ref[pl.ds(r0, GC), :]
            st = st_ref[c]
            dsn = ds_scr[...]
            bl = b[GC - 1:GC, :]
            eb = jnp.exp(b)
            ekl = jnp.exp(bl - b)
            ebl = jnp.exp(bl)
            qh = q * eb
            kd = k * ekl
            dob = do.astype(BF16)
            dsb = dsn.astype(BF16)
            dqh = _dot(dob, st.astype(BF16), NN)
            dkd = _dot(v.astype(BF16), dsb, NN)
            dv = _dot(kd.astype(BF16), dsb, NT)
            dq_i = jnp.zeros((GC, GLA_DK), F32)
            dk_i = jnp.zeros((GC, GLA_DK), F32)
            dk_x = jnp.zeros((GC, GLA_DK), F32)
            kr = _bf16_round(k)
            vr = _bf16_round(v)
            dor = _bf16_round(do)
            for j in range(GC):
                w = jnp.exp(jnp.where(rows >= j, b - b[j:j + 1, :], NEG))
                qw = q * w
                qwr = _bf16_round(qw)
                a = _bf16_round(jnp.sum(qwr * kr[j:j + 1, :], axis=-1, keepdims=True))
                da = jnp.sum(dor * vr[j:j + 1, :], axis=-1, keepdims=True)
                dq_i = dq_i + da * (w * k[j:j + 1, :])
                dk_x = jnp.where(rows == j, jnp.sum(da * qw, axis=0, keepdims=True), dk_x)
                dk_i = jnp.where(rows == j, jnp.sum(_bf16_round(da) * qwr, axis=0, keepdims=True), dk_i)
                dv = dv + jnp.where(rows == j, jnp.sum(a * dor, axis=0, keepdims=True), 0.0)
            dqs = dqh * eb + dq_i
            dk = dkd * ekl + dk_i
            db_last = jnp.sum(dkd * kd, axis=0, keepdims=True) + ebl * jnp.sum(dsn * st, axis=0, keepdims=True)
            db = q * dqs - k * (dkd * ekl + dk_x) + jnp.where(rows == GC - 1, db_last, 0.0)
            dq_ref[pl.ds(r0, GC), :] = (dqs * scale).astype(BF16)
            dk_ref[pl.ds(r0, GC), :] = dk.astype(BF16)
            dv_ref[pl.ds(r0, GC), :] = dv.astype(BF16)
            dg_ref[pl.ds(r0, GC), :] = _dot(u_ref[...], db, NN, precision=lax.Precision.HIGHEST)
            ds_scr[...] = dsn * ebl + _dot(dob, qh.astype(BF16), TN)
            return carry

        lax.fori_loop(0, ncb, chunk, 0)

    rev = lambda i: nb - 1 - i
    return pl.pallas_call(
        body, name="gla_bwd", grid=(GLA_HEADS, nb),
        in_specs=[pl.BlockSpec((r, GLA_DK), lambda h, i: (rev(i), C_QG // GLA_DK + h)),
                  pl.BlockSpec((r, GLA_DK), lambda h, i: (rev(i), C_KG // GLA_DK + h)),
                  pl.BlockSpec((r, GLA_DV), lambda h, i: (rev(i), C_VG // GLA_DV + h)),
                  pl.BlockSpec((r, GLA_DK), lambda h, i: (rev(i), h)),
                  pl.BlockSpec((r, GLA_DV), lambda h, i: (rev(i), h)),
                  pl.BlockSpec((ncb, None, GLA_DV, GLA_DK), lambda h, i: (rev(i), h, 0, 0)),
                  pl.BlockSpec((GC, GC), lambda h, i: (0, 0))],
        out_specs=[pl.BlockSpec((r, GLA_DK), lambda h, i: (rev(i), h)),
                   pl.BlockSpec((r, GLA_DK), lambda h, i: (rev(i), h)),
                   pl.BlockSpec((r, GLA_DV), lambda h, i: (rev(i), h)),
                   pl.BlockSpec((r, GLA_DK), lambda h, i: (rev(i), h))],
        out_shape=[jax.ShapeDtypeStruct((t, GLA_KW), BF16), jax.ShapeDtypeStruct((t, GLA_KW), BF16),
                   jax.ShapeDtypeStruct((t, D_GLA), BF16), jax.ShapeDtypeStruct((t, GLA_KW), F32)],
        scratch_shapes=[pltpu.VMEM((GLA_DV, GLA_DK), F32)],
        compiler_params=_cparams(("parallel", "arbitrary")),
    )(proj, proj, proj, bloc, do_raw, states, triu)


def _gate_bwd(dg, dgdz, proj, w2p):
    t = dg.shape[0]
    r = min(512, t)

    def body(dg_ref, s_ref, lo_ref, w_ref, dlo_ref, gw_ref, gb_ref):
        @pl.when(pl.program_id(0) == 0)
        def _():
            gw_ref[...] = jnp.zeros_like(gw_ref)
            gb_ref[...] = jnp.zeros_like(gb_ref)

        dz = dg_ref[...] * s_ref[...]
        dzb = dz.astype(BF16)
        gb_ref[...] += jnp.sum(dz, axis=0, keepdims=True)
        gw_ref[...] += _dot(lo_ref[...].astype(BF16), dzb, TN)
        dlo_ref[...] = _dot(dzb, w_ref[...], NT).astype(BF16)

    return pl.pallas_call(
        body, name="gate_bwd", grid=(t // r,),
        in_specs=[pl.BlockSpec((r, GLA_KW), lambda i: (i, 0)),
                  pl.BlockSpec((r, GLA_KW), lambda i: (i, 0)),
                  pl.BlockSpec((r, LANES), lambda i: (i, C_LO // LANES)),
                  pl.BlockSpec((LANES, GLA_KW), lambda i: (0, 0))],
        out_specs=[pl.BlockSpec((r, LANES), lambda i: (i, 0)),
                   pl.BlockSpec((LANES, GLA_KW), lambda i: (0, 0)),
                   pl.BlockSpec((1, GLA_KW), lambda i: (0, 0))],
        out_shape=[jax.ShapeDtypeStruct((t, LANES), BF16), jax.ShapeDtypeStruct((LANES, GLA_KW), F32),
                   jax.ShapeDtypeStruct((1, GLA_KW), F32)],
        compiler_params=_cparams(("arbitrary",)),
    )(dg, dgdz, proj, w2p)


def _swa_bwd(proj, dcat, sinks, dep=None):
    t = proj.shape[0]

    deps, dep_specs = _dep_operand(dep)

    def body(sink_ref, q_ref, kp_ref, kc_ref, vp_ref, vc_ref, d_ref, *rest):
        dq_ref, dk_ref, dv_ref, dsink_ref = rest[len(deps):]
        i = pl.program_id(0)

        @pl.when(i == 0)
        def _():
            dk_ref[...] = jnp.zeros_like(dk_ref)
            dv_ref[...] = jnp.zeros_like(dv_ref)
            dsink_ref[...] = jnp.zeros_like(dsink_ref)

        valid, lo = _swa_masks(i)
        lane = lax.broadcasted_iota(jnp.int32, (1, LANES), 1)
        kb = jnp.concatenate([kp_ref[...], kc_ref[...]], axis=0)
        vb = jnp.concatenate([vp_ref[...], vc_ref[...]], axis=0)
        dsink = jnp.zeros((1, LANES), F32)
        folded_k, folded_v = [], []
        for kv in range(2):
            kdup = _dup_half(kb, lo, kv).astype(BF16)
            vdup = _dup_half(vb, lo, kv).astype(BF16)
            dkd = jnp.zeros((2 * WINDOW, LANES), F32)
            dvd = jnp.zeros((2 * WINDOW, LANES), F32)
            for p in range(4):
                sl = slice(LANES * (4 * kv + p), LANES * (4 * kv + p + 1))
                qp = q_ref[:, sl]
                dp_ = d_ref[:, sl]
                halves = []
                for e in range(2):
                    h = 8 * kv + 2 * p + e
                    half = lo if e == 0 else ~lo
                    qm = jnp.where(half, qp, 0.0).astype(BF16)
                    dom = jnp.where(half, dp_, 0.0).astype(BF16)
                    pn, psink = _swa_probs(qm, kdup, valid, sink_ref[h])
                    dpr = _dot(dom, vdup, NT)
                    drow = jnp.sum(dpr * pn, axis=-1, keepdims=True)
                    dsink = dsink + jnp.where(lane == h, -jnp.sum(psink * drow), 0.0)
                    dsb = (pn * (dpr - drow) * (SWA_DH ** -0.5)).astype(BF16)
                    halves.append(_dot(dsb, kdup, NN))
                    dkd = dkd + _dot(dsb, qm, TN)
                    dvd = dvd + _dot(pn.astype(BF16), dom, TN)
                dq_ref[:, sl] = jnp.where(lo, halves[0], halves[1]).astype(BF16)
            folded_k.append(dkd + pltpu.roll(dkd, SWA_DH, axis=1))
            folded_v.append(dvd + pltpu.roll(dvd, SWA_DH, axis=1))
        dkb = jnp.where(lo, folded_k[0], folded_k[1])
        dvb = jnp.where(lo, folded_v[0], folded_v[1])
        dsink_ref[...] += dsink
        cur = pl.ds(pl.multiple_of(i * WINDOW, WINDOW), WINDOW)
        dk_ref[cur, :] += dkb[WINDOW:, :]
        dv_ref[cur, :] += dvb[WINDOW:, :]

        @pl.when(i > 0)
        def _():
            prev = pl.ds(pl.multiple_of((i - 1) * WINDOW, WINDOW), WINDOW)
            dk_ref[prev, :] += dkb[:WINDOW, :]
            dv_ref[prev, :] += dvb[:WINDOW, :]

    kvspec = lambda col, prev: pl.BlockSpec(
        (WINDOW, LANES), (lambda i: (jnp.maximum(i - 1, 0), col)) if prev else (lambda i: (i, col)))
    full = pl.BlockSpec((t, LANES), lambda i: (0, 0))
    return pl.pallas_call(
        body, name="swa_bwd", grid=(t // WINDOW,),
        in_specs=[pl.BlockSpec(memory_space=pltpu.SMEM),
                  pl.BlockSpec((WINDOW, D_SWA), lambda i: (i, C_QS // D_SWA)),
                  kvspec(C_KS // LANES, True), kvspec(C_KS // LANES, False),
                  kvspec(C_VS // LANES, True), kvspec(C_VS // LANES, False),
                  pl.BlockSpec((WINDOW, D_SWA), lambda i: (i, 1)), *dep_specs],
        out_specs=[pl.BlockSpec((WINDOW, D_SWA), lambda i: (i, 0)), full, full,
                   pl.BlockSpec((1, LANES), lambda i: (0, 0))],
        out_shape=[jax.ShapeDtypeStruct((t, D_SWA), BF16), jax.ShapeDtypeStruct((t, LANES), F32),
                   jax.ShapeDtypeStruct((t, LANES), F32), jax.ShapeDtypeStruct((1, LANES), F32)],
        compiler_params=_cparams(("arbitrary",)),
    )(sinks, proj, proj, proj, proj, proj, dcat, *deps)


def _grad_w_in(xb, dproj):
    t = xb.shape[0]
    tm, tn, tk = 1024, 640, min(512, t)
    return _matmul(
        "grad_w_in", xb, dproj, TN, (D_MODEL // tm, D_INP // tn, t // tk),
        pl.BlockSpec((tk, tm), lambda i, j, k: (k, i)),
        pl.BlockSpec((tk, tn), lambda i, j, k: (k, j)),
        jax.ShapeDtypeStruct((D_MODEL, D_INP), F32),
        pl.BlockSpec((tm, tn), lambda i, j, k: (i, j)), (tm, tn))


def _grad_x(dproj, w_in_p, dr1, dep=None):
    t = dproj.shape[0]
    tm, tk = min(512, t), 640

    def epilogue(acc_ref, extra, outs):
        for r0, n in _row_chunks(tm):
            rs = slice(r0, r0 + n)
            outs[0][rs, :] = ALPHA * extra[0][rs, :] + acc_ref[rs, :]

    row = pl.BlockSpec((tm, D_MODEL), lambda i, j, k: (i, 0))
    return _matmul(
        "grad_x", dproj, w_in_p, NT, (t // tm, 1, D_INP // tk),
        pl.BlockSpec((tm, tk), lambda i, j, k: (i, k)),
        pl.BlockSpec((D_MODEL, tk), lambda i, j, k: (0, k)),
        jax.ShapeDtypeStruct((t, D_MODEL), F32), row, (tm, D_MODEL),
        extra=(dr1,), extra_specs=(row,), epilogue=epilogue, dep=dep)


def _place():
    x, y, c = lax.axis_index("x"), lax.axis_index("y"), lax.axis_index("c")
    chips = [(1 - x, y), (x, 1 - y), (1 - x, 1 - y)]
    return x, y, c, chips


def _all_gather(shards):
    n = len(shards)
    hbm = pl.BlockSpec(memory_space=pl.ANY)

    def body(*refs):
        ins, outs = refs[:n], refs[n:2 * n]
        send, recv, loc = refs[2 * n:]
        x, y, c, chips = _place()
        me, sib = (x, y, c), (x, y, 1 - c)

        def slot(t, px, py, pc):
            return outs[t].at[4 * px + 2 * py + pc]

        def copy(t, k, block, to, src=None):
            return pltpu.make_async_remote_copy(
                src_ref=slot(t, *block) if src is None else src, dst_ref=slot(t, *block),
                send_sem=send.at[7 * t + k], recv_sem=recv.at[7 * t + k], device_id=to, device_id_type=MESH)

        mine = [pltpu.make_async_copy(ins[t], slot(t, *me), loc.at[t]) for t in range(n)]
        for cp in mine:
            cp.start()
        sent = []
        for t in range(n):
            sent.append(copy(t, 0, me, sib, src=ins[t]))
            sent += [copy(t, 1 + j, me, (*chip, c), src=ins[t]) for j, chip in enumerate(chips)]
        for cp in sent:
            cp.start()
        for t in range(n):
            for j, chip in enumerate(chips):
                copy(t, 1 + j, (*chip, c), me).wait_recv()
                fwd = copy(t, 4 + j, (*chip, c), sib)
                fwd.start()
                sent.append(fwd)
        for t in range(n):
            copy(t, 0, sib, me).wait_recv()
            for j, chip in enumerate(chips):
                copy(t, 4 + j, (*chip, 1 - c), me).wait_recv()
        for cp in sent:
            cp.wait_send()
        for cp in mine:
            cp.wait()

    return pl.pallas_call(
        body, name="all_gather_weights",
        in_specs=[hbm] * n, out_specs=[hbm] * n,
        out_shape=[jax.ShapeDtypeStruct((N_DEV,) + s.shape, s.dtype) for s in shards],
        scratch_shapes=[pltpu.SemaphoreType.DMA((7 * n,)), pltpu.SemaphoreType.DMA((7 * n,)),
                        pltpu.SemaphoreType.DMA((n,))],
    )(*shards)


def _plan_gather_out(src, land, x, y, c, chips):
    me = 4 * x + 2 * y + c
    return [(src, land.at[me], to) for to in [(x, y, 1 - c)] + [(px, py, c) for px, py in chips]]


def _plan_gather_forward(src, land, x, y, c, chips):
    return [(land.at[4 * px + 2 * py + c], land.at[4 * px + 2 * py + c], (x, y, 1 - c)) for px, py in chips]


def _plan_sibling(src, land, x, y, c, chips):
    return [(src.at[2 * q + (1 - c)], land.at[q], (x, y, 1 - c)) for q in range(4)]


def _plan_chips(src, land, x, y, c, chips):
    return [(src.at[j], land.at[j], (px, py, c)) for j, (px, py) in enumerate(chips)]


_PLAN_COPIES = {_plan_gather_out: 4, _plan_gather_forward: 3, _plan_sibling: 4, _plan_chips: 3}
_HBM = pl.BlockSpec(memory_space=pltpu.HBM)
_SEM = pl.BlockSpec(memory_space=pltpu.SEMAPHORE)
_EFFECT = pltpu.SideEffectType.DATAFLOW_SIDE_EFFECTING


def _hbm(a):
    return pltpu.with_memory_space_constraint(a, pltpu.HBM)


def _plan_descriptors(plans, srcs, lands, send, recv):
    x, y, c, chips = _place()
    cps = []
    for plan, src, land in zip(plans, srcs, lands):
        for s_ref, d_ref, to in plan(src, land, x, y, c, chips):
            k = len(cps)
            cps.append(pltpu.make_async_remote_copy(src_ref=s_ref, dst_ref=d_ref, send_sem=send.at[k],
                                                    recv_sem=recv.at[k], device_id=to, device_id_type=MESH))
    return cps


def _copies_start(name, plans, srcs, lands, after):
    has_src = [s is not None for s in srcs]
    arrays = [s for s in srcs if s is not None] + list(lands)
    n_src = sum(has_src)
    n_cp = sum(_PLAN_COPIES[p] for p in plans)

    def body(*refs):
        ins = refs[:len(arrays)]
        send, recv = refs[len(arrays) + 1], refs[len(arrays) + 2]
        token = refs[-1]
        it = iter(ins[:n_src])
        src_refs = [next(it) if h else None for h in has_src]
        for cp in _plan_descriptors(plans, src_refs, ins[n_src:], send, recv):
            cp.start()
        token[...] = jnp.zeros_like(token)

    outs = pl.pallas_call(
        body, name=name,
        in_specs=[_HBM] * len(arrays) + [pl.BlockSpec(memory_space=pl.ANY)],
        out_specs=(_SEM, _SEM, *[_HBM] * len(arrays), pl.BlockSpec(memory_space=pltpu.VMEM)),
        out_shape=(pltpu.SemaphoreType.DMA((n_cp,)), pltpu.SemaphoreType.DMA((n_cp,)),
                   *[pltpu.HBM(a.shape, a.dtype) for a in arrays], jax.ShapeDtypeStruct((8, LANES), F32)),
        input_output_aliases={i: 2 + i for i in range(len(arrays))},
        compiler_params=pltpu.CompilerParams(has_side_effects=_EFFECT),
    )(*[_hbm(a) for a in arrays], after)
    send, recv = outs[0], outs[1]
    thru = list(outs[2:-1])
    it = iter(thru[:n_src])
    return send, recv, [next(it) if h else None for h in has_src], thru[n_src:], outs[-1]


def _copies_wait(name, plans, started, after):
    send, recv, srcs, lands, _ = started
    has_src = [s is not None for s in srcs]
    arrays = [s for s in srcs if s is not None] + list(lands)
    n_src = sum(has_src)

    def body(*refs):
        ins = refs[:len(arrays)]
        send_ref, recv_ref = refs[len(arrays)], refs[len(arrays) + 1]
        it = iter(ins[:n_src])
        src_refs = [next(it) if h else None for h in has_src]
        for cp in _plan_descriptors(plans, src_refs, ins[n_src:], send_ref, recv_ref):
            cp.wait_send()
            cp.wait_recv()

    outs = pl.pallas_call(
        body, name=name,
        in_specs=[_HBM] * len(arrays) + [_SEM, _SEM, pl.BlockSpec(memory_space=pl.ANY)],
        out_specs=tuple([_HBM] * len(arrays)),
        out_shape=tuple(pltpu.HBM(a.shape, a.dtype) for a in arrays),
        input_output_aliases={i: i for i in range(len(arrays))},
        compiler_params=pltpu.CompilerParams(has_side_effects=_EFFECT),
    )(*arrays, send, recv, after)
    return list(outs[n_src:])


def _pair_sum(name, grad, from_sibling, blocks):
    _, rows, cols = grad.shape
    tr = 256

    def body(blk_ref, g_ref, s_ref, o_ref):
        o_ref[...] = (g_ref[...] + s_ref[...]).astype(BF16)

    return pl.pallas_call(
        body, name=name,
        grid_spec=pltpu.PrefetchScalarGridSpec(
            num_scalar_prefetch=1, grid=(3, rows // tr),
            in_specs=[pl.BlockSpec((None, tr, cols), lambda j, r, br: (br[j], r, 0)),
                      pl.BlockSpec((None, tr, cols), lambda j, r, br: (br[3 + j], r, 0))],
            out_specs=pl.BlockSpec((None, tr, cols), lambda j, r, br: (j, r, 0))),
        out_shape=jax.ShapeDtypeStruct((3, rows, cols), BF16),
        compiler_params=_cparams(("parallel", "parallel")),
    )(blocks, grad, from_sibling)


def _adam_math(g, w, m, v):
    m2 = ADAM_B1 * m + (1.0 - ADAM_B1) * g
    v2 = ADAM_B2 * v + (1.0 - ADAM_B2) * (g * g)
    m_hat = m2 / (1.0 - ADAM_B1 ** ADAM_STEP)
    v_hat = v2 / (1.0 - ADAM_B2 ** ADAM_STEP)
    delta = -ADAM_LR * (m_hat / (jnp.sqrt(v_hat) + ADAM_EPS) + ADAM_WD * w)
    return delta, m2, v2


def _sum_adam(name, grad, from_sibling, from_chips, own, w, m, v):
    rows, cols = w.shape
    tr = 128

    def body(own_ref, p_ref, s_ref, r_ref, w_ref, m_ref, v_ref, g_out, d_out, m_out, v_out):
        g = p_ref[...] + s_ref[...]
        for j in range(3):
            g = g + r_ref[j].astype(F32)
        d, m2, v2 = _adam_math(g, w_ref[...], m_ref[...], v_ref[...])
        g_out[...] = g
        d_out[...] = d
        m_out[...] = m2
        v_out[...] = v2

    blk = pl.BlockSpec((tr, cols), lambda r, cr: (r, 0))
    shp = jax.ShapeDtypeStruct((rows, cols), F32)
    return pl.pallas_call(
        body, name=name,
        grid_spec=pltpu.PrefetchScalarGridSpec(
            num_scalar_prefetch=1, grid=(rows // tr,),
            in_specs=[pl.BlockSpec((None, tr, cols), lambda r, cr: (cr[0], r, 0)),
                      pl.BlockSpec((None, tr, cols), lambda r, cr: (cr[1], r, 0)),
                      pl.BlockSpec((3, tr, cols), lambda r, cr: (0, r, 0)),
                      blk, blk, blk],
            out_specs=[blk, blk, blk, blk]),
        out_shape=[shp, shp, shp, shp],
        compiler_params=_cparams(("parallel",)),
    )(own, grad, from_sibling, from_chips, w, m, v)


def _adam_small(name, g, w, m, v):
    def body(g_ref, w_ref, m_ref, v_ref, d_out, m_out, v_out):
        d, m2, v2 = _adam_math(g_ref[...], w_ref[...], m_ref[...], v_ref[...])
        d_out[...] = d
        m_out[...] = m2
        v_out[...] = v2

    shp = jax.ShapeDtypeStruct(w.shape, F32)
    return pl.pallas_call(body, name=name, out_shape=[shp, shp, shp])(g, w, m, v)


def _all_reduce_small(pack):
    rows = pack.shape[0]

    def body(in_ref, out_ref, slots, send, recv):
        x, y, c, _ = _place()
        me = 4 * x + 2 * y + c
        slots[me] = in_ref[...]
        cps = []
        for k in range(1, N_DEV):
            dx, dy, dc = (k >> 2) & 1, (k >> 1) & 1, k & 1
            to = (jnp.bitwise_xor(x, dx), jnp.bitwise_xor(y, dy), jnp.bitwise_xor(c, dc))
            cps.append(pltpu.make_async_remote_copy(
                src_ref=in_ref, dst_ref=slots.at[me], send_sem=send.at[k - 1], recv_sem=recv.at[k - 1],
                device_id=to, device_id_type=MESH))
        for cp in cps:
            cp.start()
        for cp in cps:
            cp.wait()
        acc = slots[0]
        for d in range(1, N_DEV):
            acc = acc + slots[d]
        out_ref[...] = acc

    return pl.pallas_call(
        body, name="all_reduce_small",
        in_specs=[pl.BlockSpec(memory_space=pltpu.VMEM)],
        out_specs=pl.BlockSpec(memory_space=pltpu.VMEM),
        out_shape=jax.ShapeDtypeStruct((rows, LANES), F32),
        scratch_shapes=[pltpu.VMEM((N_DEV, rows, LANES), F32),
                        pltpu.SemaphoreType.DMA((N_DEV - 1,)), pltpu.SemaphoreType.DMA((N_DEV - 1,))],
    )(pack)


def _rows128(a):
    flat = a.reshape(-1)
    padn = (-flat.shape[0]) % (8 * LANES)
    if padn:
        flat = jnp.concatenate([flat, jnp.zeros((padn,), flat.dtype)])
    return flat.reshape(-1, LANES)


def kernel(x, w_in, w_gk2, b_gk, gla_norm_w, swa_sinks, w_out, ln1_g, ln1_b, w_up, w_down, ln2_g, ln2_b, loss_target, m_w_in, m_w_gk2, m_b_gk, m_gla_norm_w, m_swa_sinks, m_w_out, m_ln1_g, m_ln1_b, m_w_up, m_w_down, m_ln2_g, m_ln2_b, v_w_in, v_w_gk2, v_b_gk, v_gla_norm_w, v_swa_sinks, v_w_out, v_ln1_g, v_ln1_b, v_w_up, v_w_down, v_ln2_g, v_ln2_b):
    xc, yc, cc = lax.axis_index("x"), lax.axis_index("y"), lax.axis_index("c")
    me = 4 * xc + 2 * yc + cc

    x2 = x[0]
    t = x2.shape[0]
    xb = x2.astype(BF16)
    target = loss_target[0]

    first = _all_gather([w_in[0].astype(BF16), w_gk2[0].astype(BF16)])
    w_in_p = _to_padded_cols(first[0].transpose(1, 0, 2).reshape(D_MODEL, D_IN))
    w2 = first[1].transpose(1, 0, 2).reshape(GATE_RANK, GLA_KW)
    w2p = jnp.concatenate([w2, jnp.zeros((LANES - GATE_RANK, GLA_KW), BF16)], axis=0)
    shards = [w_out[0].astype(BF16), w_up[0].astype(BF16), w_down[0].astype(BF16)]
    lands = [lax.dynamic_update_index_in_dim(lax.empty((N_DEV,) + sh.shape, BF16), sh, me, 0) for sh in shards]
    out_plans = [_plan_gather_out] * 3
    fwd_plans = [_plan_gather_forward] * 3
    ag1 = _copies_start("gather_out_start", out_plans, shards, lands, first[0])

    proj = _proj(xb, w_in_p, dep=ag1[4])
    rt = min(256, t)
    ii = jnp.arange(rt)
    tri = ((ii[:, None] // GC == ii[None, :] // GC) & (ii[None, :] <= ii[:, None])).astype(F32)
    bloc, dgdz = _gate_fwd(proj, w2p, b_gk, tri)
    o_raw, states = _gla_fwd(proj, bloc)
    lands = _copies_wait("gather_out_wait", out_plans, ag1, o_raw)
    ag2 = _copies_start("gather_forward_start", fwd_plans, [None] * 3, lands, o_raw)
    gla_out = _gla_norm_fwd(o_raw, proj, gla_norm_w)
    swa_out = _swa_fwd(proj, swa_sinks[0], dep=ag2[4])
    cat = jnp.concatenate([gla_out, swa_out], axis=-1)
    gathered = _copies_wait("gather_forward_wait", fwd_plans, ag2, swa_out)
    w_out_f = gathered[0].reshape(D_MODEL, D_MODEL)
    w_up_f = gathered[1]
    w_down_f = gathered[2].reshape(D_FF, D_MODEL)
    r1, h1, h1b = _mix_ln1(cat, w_out_f, x2, ln1_g, ln1_b)
    a_act, hdn = _mlp_up(h1b, w_up_f)
    dr2, dr2b, g_ln2_g, g_ln2_b, loss_part = _mlp_down_loss(hdn, w_down_f, h1, target, ln2_g, ln2_b)

    others = [2 * (1 - xc) + yc, 2 * xc + (1 - yc), 2 * (1 - xc) + (1 - yc)]
    blocks = jnp.stack([2 * q + cc for q in others] + others).astype(jnp.int32)
    own = jnp.stack([me, 2 * xc + yc]).astype(jnp.int32)
    wmv = dict(w_in=(w_in, m_w_in, v_w_in), w_out=(w_out, m_w_out, v_w_out), w_up=(w_up, m_w_up, v_w_up),
               w_down=(w_down, m_w_down, v_w_down))
    big = {}

    def sib_land(g):
        return lax.empty((4,) + g.shape[1:], F32)

    def chip_land(g):
        return lax.empty((3,) + g.shape[1:], BF16)

    def finish(nm, g, from_sib, from_chips):
        w, m, v = wmv[nm]
        big[nm] = [o[None] for o in _sum_adam("sum_adam_" + nm, g, from_sib, from_chips, own, w[0], m[0], v[0])]

    du = _mlp_down_bwd(dr2b, w_down_f, a_act)
    g_down = _grad_w_down(hdn, dr2b).reshape(N_DEV, D_FF // N_DEV, D_MODEL)
    sa_down = _copies_start("sibling_down_start", [_plan_sibling], [g_down], [sib_land(g_down)], g_down)
    g_up = _grad_w_up(h1b, du, dep=sa_down[4])
    (fs_down,) = _copies_wait("sibling_down_wait", [_plan_sibling], sa_down, g_up)
    p_down = _pair_sum("pair_sum_w_down", g_down, fs_down, blocks)
    sb_down = _copies_start("chips_down_sibling_up_start", [_plan_chips, _plan_sibling], [p_down, g_up],
                            [chip_land(g_down), sib_land(g_up)], p_down)
    dr1, g_ln1_g, g_ln1_b = _mlp_up_bwd_ln1(du, w_up_f, dr2, r1, ln1_g, dep=sb_down[4])
    dcat = _dcat(dr1, w_out_f)
    fc_down, fs_up = _copies_wait("chips_down_sibling_up_wait", [_plan_chips, _plan_sibling], sb_down, dcat)
    finish("w_down", g_down, fs_down, fc_down)
    p_up = _pair_sum("pair_sum_w_up", g_up, fs_up, blocks)
    sb_up = _copies_start("chips_up_start", [_plan_chips], [p_up], [chip_land(g_up)], p_up)
    g_out = _grad_w_out(cat, dr1, dep=sb_up[4]).reshape(N_DEV, D_MODEL // N_DEV, D_MODEL)
    sa_out = _copies_start("sibling_out_start", [_plan_sibling], [g_out], [sib_land(g_out)], g_out)
    do_raw, dg_g, g_norm_w = _gla_norm_bwd(dcat, o_raw, proj, gla_norm_w, dep=sa_out[4])
    jj = jnp.arange(GC)
    triu = (jj[None, :] >= jj[:, None]).astype(F32)
    dq_g, dk_g, dv_g, dgk = _gla_bwd(proj, bloc, do_raw, states, triu)
    (fc_up,) = _copies_wait("chips_up_wait", [_plan_chips], sb_up, dgk)
    (fs_out,) = _copies_wait("sibling_out_wait", [_plan_sibling], sa_out, dgk)
    finish("w_up", g_up, fs_up, fc_up)
    p_out = _pair_sum("pair_sum_w_out", g_out, fs_out, blocks)
    sb_out = _copies_start("chips_out_start", [_plan_chips], [p_out], [chip_land(g_out)], p_out)
    dlo, gw2_p, g_b_gk = _gate_bwd(dgk, dgdz, proj, w2p)
    dq_s, dk_s, dv_s, g_sinks = _swa_bwd(proj, dcat, swa_sinks[0], dep=sb_out[4])
    dproj = jnp.concatenate([dq_g, dk_g, dv_g, dg_g, dq_s, dk_s.astype(BF16), dv_s.astype(BF16), dlo], axis=-1)
    gw_in_p = _grad_w_in(xb, dproj)
    (fc_out,) = _copies_wait("chips_out_wait", [_plan_chips], sb_out, gw_in_p)
    finish("w_out", g_out, fs_out, fc_out)
    g_in = _from_padded_cols(gw_in_p).reshape(D_MODEL, N_DEV, D_IN // N_DEV).transpose(1, 0, 2)
    sa_in = _copies_start("sibling_in_start", [_plan_sibling], [g_in], [sib_land(g_in)], g_in)
    grad_x = _grad_x(dproj, w_in_p, dr1, dep=sa_in[4])
    (fs_in,) = _copies_wait("sibling_in_wait", [_plan_sibling], sa_in, grad_x)
    p_in = _pair_sum("pair_sum_w_in", g_in, fs_in, blocks)
    sb_in = _copies_start("chips_in_start", [_plan_chips], [p_in], [chip_land(g_in)], p_in)
    (fc_in,) = _copies_wait("chips_in_wait", [_plan_chips], sb_in, p_in)
    finish("w_in", g_in, fs_in, fc_in)

    pieces = [loss_part, g_b_gk, g_norm_w, g_sinks[:, :SWA_HEADS], g_ln1_g, g_ln1_b, g_ln2_g, g_ln2_b,
              gw2_p[:GATE_RANK]]
    pack = jnp.concatenate([_rows128(p) for p in pieces], axis=0)
    tot = _all_reduce_small(pack)
    sizes = [p.size for p in pieces]
    offs = [0]
    for p in pieces:
        offs.append(offs[-1] + _rows128(p).shape[0])
    unpack = lambda i, shape: tot[offs[i]:offs[i + 1]].reshape(-1)[:sizes[i]].reshape(shape)
    loss = tot[0, 0]
    small_names = ["b_gk", "gla_norm_w", "swa_sinks", "ln1_g", "ln1_b", "ln2_g", "ln2_b"]
    small_w = dict(b_gk=(b_gk, m_b_gk, v_b_gk), gla_norm_w=(gla_norm_w, m_gla_norm_w, v_gla_norm_w),
                   swa_sinks=(swa_sinks, m_swa_sinks, v_swa_sinks), ln1_g=(ln1_g, m_ln1_g, v_ln1_g),
                   ln1_b=(ln1_b, m_ln1_b, v_ln1_b), ln2_g=(ln2_g, m_ln2_g, v_ln2_g),
                   ln2_b=(ln2_b, m_ln2_b, v_ln2_b))
    small_g = {nm: unpack(1 + i, small_w[nm][0].shape) for i, nm in enumerate(small_names)}
    g_pack = jnp.concatenate([_rows128(small_g[nm]) for nm in small_names], axis=0)
    wmv = [jnp.concatenate([_rows128(small_w[nm][k]) for nm in small_names], axis=0) for k in range(3)]
    small_out = _adam_small("adam_replicated", g_pack, *wmv)
    srow = [0]
    for nm in small_names:
        srow.append(srow[-1] + _rows128(small_w[nm][0]).shape[0])
    small = {}
    for i, nm in enumerate(small_names):
        shape = small_w[nm][0].shape
        n = small_w[nm][0].size
        small[nm] = [small_g[nm]] + [o[srow[i]:srow[i + 1]].reshape(-1)[:n].reshape(shape) for o in small_out]

    gw2_full = unpack(8, (GATE_RANK, GLA_KW))
    gw2_loc = lax.dynamic_slice_in_dim(gw2_full, me * (GLA_KW // N_DEV), GLA_KW // N_DEV, axis=1)
    gk2_out = _adam_small("adam_w_gk2", gw2_loc, w_gk2[0], m_w_gk2[0], v_w_gk2[0])
    big["w_gk2"] = [gw2_loc[None]] + [o[None] for o in gk2_out]

    order = ["w_in", "w_gk2", "b_gk", "gla_norm_w", "swa_sinks", "w_out", "ln1_g", "ln1_b", "w_up", "w_down",
             "ln2_g", "ln2_b"]
    res = {**big, **small}
    outs = [loss, grad_x[None]]
    for k in range(4):
        outs += [res[nm][k] for nm in order]
    return tuple(outs)
```

```python
import functools

import jax
import jax.numpy as jnp
from jax import lax
from jax.experimental import pallas as pl
from jax.experimental.pallas import tpu as pltpu

F32 = jnp.float32
BF16 = jnp.bfloat16

N_DEV = 8
D_MODEL = 2048
D_FF = 8192
GLA_HEADS = 4
GLA_DK = 128
GLA_DV = 256
GLA_KW = 512
D_GLA = 1024
GATE_RANK = 16
GATE_TAU = 16.0
SWA_HEADS = 16
SWA_DH = 64
WINDOW = 128
D_SWA = 1024
D_IN = 4368
ALPHA = 2.0 ** 0.25
LN_EPS = 1e-5
RMS_EPS = 1e-5
ADAM_LR = 0.001
ADAM_B1 = 0.9
ADAM_B2 = 0.999
ADAM_EPS = 1e-08
ADAM_WD = 0.01
ADAM_STEP = 10

C_QG, C_KG, C_VG, C_GG, C_QS, C_KS, C_VS, C_LO = 0, 512, 1024, 2048, 3072, 4096, 4224, 4352
D_INP = 4480
LANES = 128
GC = 16
NEG = -1e30

NN = ((1,), (0,))
NT = ((1,), (1,))
TN = ((0,), (0,))

VMEM_LIMIT = 52 * 1024 * 1024
MESH = pl.DeviceIdType.MESH


def _dot(a, b, dn, precision=None):
    return lax.dot_general(a, b, (dn, ((), ())), preferred_element_type=F32, precision=precision)


def _bf16_round(v):
    return v.astype(BF16).astype(F32)


def _cparams(dims):
    return pltpu.CompilerParams(dimension_semantics=dims, vmem_limit_bytes=VMEM_LIMIT)


def _dep_operand(dep):
    if dep is None:
        return (), ()
    return (dep,), (pl.BlockSpec(dep.shape, lambda *_: (0,) * dep.ndim),)


def _to_padded_cols(w):
    pad = jnp.zeros(w.shape[:-1] + (D_INP - D_IN,), w.dtype)
    return jnp.concatenate([w[..., :3072], w[..., 3088:], w[..., 3072:3088], pad], axis=-1)


def _from_padded_cols(g):
    return jnp.concatenate([g[..., :3072], g[..., C_LO:C_LO + GATE_RANK], g[..., 3072:C_LO]], axis=-1)


def _matmul(name, a, b, dn, grid, a_spec, b_spec, out_shape, out_specs, acc_shape, *,
            extra=(), extra_specs=(), epilogue=None, dims=("parallel", "parallel", "arbitrary"), dep=None):
    nk = grid[2]
    n_extra = len(extra)
    deps, dep_specs = _dep_operand(dep)
    direct = nk == 1 and epilogue is None

    def body(a_ref, b_ref, *rest):
        extra_refs = rest[:n_extra]
        out_refs = rest[n_extra + len(deps):] if direct else rest[n_extra + len(deps):-1]
        acc_ref = None if direct else rest[-1]
        part = _dot(a_ref[...].astype(BF16), b_ref[...].astype(BF16), dn)

        def finish():
            if epilogue is None:
                out_refs[0][...] = acc_ref[...].astype(out_refs[0].dtype)
            else:
                epilogue(acc_ref, extra_refs, out_refs)

        if direct:
            out_refs[0][...] = part.astype(out_refs[0].dtype)
        elif nk == 1:
            acc_ref[...] = part
            finish()
        else:
            k = pl.program_id(2)

            @pl.when(k == 0)
            def _():
                acc_ref[...] = part

            @pl.when(k > 0)
            def _():
                acc_ref[...] += part

            @pl.when(k == nk - 1)
            def _():
                finish()

    return pl.pallas_call(
        body, name=name, grid=grid,
        in_specs=[a_spec, b_spec, *extra_specs, *dep_specs],
        out_specs=out_specs, out_shape=out_shape,
        scratch_shapes=[] if direct else [pltpu.VMEM(acc_shape, F32)],
        compiler_params=_cparams(dims),
    )(a, b, *extra, *deps)


def _row_chunks(rows, step=128):
    step = min(step, rows)
    return [(r, step) for r in range(0, rows, step)]


def _ln_stats(r):
    mu = jnp.mean(r, axis=-1, keepdims=True)
    xc = r - mu
    var = jnp.mean(xc * xc, axis=-1, keepdims=True)
    rstd = lax.rsqrt(var + LN_EPS)
    return xc * rstd, rstd


def _ln_bwd(dy_g, xhat, rstd):
    m1 = jnp.mean(dy_g, axis=-1, keepdims=True)
    m2 = jnp.mean(dy_g * xhat, axis=-1, keepdims=True)
    return rstd * (dy_g - m1 - xhat * m2)


def _proj(xb, w_in_p, dep=None):
    t = xb.shape[0]
    tm, tn = min(1024, t), 640
    return _matmul(
        "proj", xb, w_in_p, NN, (t // tm, D_INP // tn, 1),
        pl.BlockSpec((tm, D_MODEL), lambda i, j, k: (i, 0)),
        pl.BlockSpec((D_MODEL, tn), lambda i, j, k: (0, j)),
        jax.ShapeDtypeStruct((t, D_INP), F32),
        pl.BlockSpec((tm, tn), lambda i, j, k: (i, j)),
        (tm, tn), dep=dep)


def _gate_fwd(proj, w2p, b_gk, tri):
    t = proj.shape[0]
    r = tri.shape[0]

    def body(lo_ref, w_ref, b_ref, tri_ref, bloc_ref, dgdz_ref):
        z = _dot(lo_ref[...].astype(BF16), w_ref[...], NN) + b_ref[...]
        e = jnp.exp(-jnp.abs(z))
        gk = (jnp.minimum(z, 0.0) - jnp.log1p(e)) * (1.0 / GATE_TAU)
        inv = 1.0 / (1.0 + e)
        dgdz_ref[...] = jnp.where(z >= 0.0, e * inv, inv) * (1.0 / GATE_TAU)
        bloc_ref[...] = _dot(tri_ref[...], gk, NN, precision=lax.Precision.HIGHEST)

    return pl.pallas_call(
        body, name="gate_fwd", grid=(t // r,),
        in_specs=[pl.BlockSpec((r, LANES), lambda i: (i, C_LO // LANES)),
                  pl.BlockSpec((LANES, GLA_KW), lambda i: (0, 0)),
                  pl.BlockSpec((1, GLA_KW), lambda i: (0, 0)),
                  pl.BlockSpec((r, r), lambda i: (0, 0))],
        out_specs=[pl.BlockSpec((r, GLA_KW), lambda i: (i, 0)),
                   pl.BlockSpec((r, GLA_KW), lambda i: (i, 0))],
        out_shape=[jax.ShapeDtypeStruct((t, GLA_KW), F32), jax.ShapeDtypeStruct((t, GLA_KW), F32)],
        compiler_params=_cparams(("parallel",)),
    )(proj, w2p, b_gk, tri)


def _gla_fwd(proj, bloc):
    t = proj.shape[0]
    r = min(256, t)
    ncb = r // GC
    scale = GLA_DK ** -0.5

    def body(q_ref, k_ref, v_ref, b_ref, o_ref, st_ref, s_scr):
        @pl.when(pl.program_id(1) == 0)
        def _():
            s_scr[...] = jnp.zeros_like(s_scr)

        rows = lax.broadcasted_iota(jnp.int32, (GC, 1), 0)

        def chunk(c, carry):
            r0 = pl.multiple_of(c * GC, GC)
            q = q_ref[pl.ds(r0, GC), :] * scale
            k = k_ref[pl.ds(r0, GC), :]
            v = v_ref[pl.ds(r0, GC), :]
            b = b_ref[pl.ds(r0, GC), :]
            st = s_scr[...]
            st_ref[c] = st
            bl = b[GC - 1:GC, :]
            kr = _bf16_round(k)
            vr = _bf16_round(v)
            o = _dot((q * jnp.exp(b)).astype(BF16), st.astype(BF16), NT)
            for j in range(GC):
                w = jnp.exp(jnp.where(rows >= j, b - b[j:j + 1, :], NEG))
                a = jnp.sum(_bf16_round(q * w) * kr[j:j + 1, :], axis=-1, keepdims=True)
                o = o + _bf16_round(a) * vr[j:j + 1, :]
            o_ref[pl.ds(r0, GC), :] = o
            kd = k * jnp.exp(bl - b)
            s_scr[...] = st * jnp.exp(bl) + _dot(v.astype(BF16), kd.astype(BF16), TN)
            return carry

        lax.fori_loop(0, ncb, chunk, 0)

    return pl.pallas_call(
        body, name="gla_fwd", grid=(GLA_HEADS, t // r),
        in_specs=[pl.BlockSpec((r, GLA_DK), lambda h, i: (i, C_QG // GLA_DK + h)),
                  pl.BlockSpec((r, GLA_DK), lambda h, i: (i, C_KG // GLA_DK + h)),
                  pl.BlockSpec((r, GLA_DV), lambda h, i: (i, C_VG // GLA_DV + h)),
                  pl.BlockSpec((r, GLA_DK), lambda h, i: (i, h))],
        out_specs=[pl.BlockSpec((r, GLA_DV), lambda h, i: (i, h)),
                   pl.BlockSpec((ncb, None, GLA_DV, GLA_DK), lambda h, i: (i, h, 0, 0))],
        out_shape=[jax.ShapeDtypeStruct((t, D_GLA), F32),
                   jax.ShapeDtypeStruct((t // GC, GLA_HEADS, GLA_DV, GLA_DK), F32)],
        scratch_shapes=[pltpu.VMEM((GLA_DV, GLA_DK), F32)],
        compiler_params=_cparams(("parallel", "arbitrary")),
    )(proj, proj, proj, bloc)


def _gla_norm_fwd(o_raw, proj, norm_w):
    t = o_raw.shape[0]
    r = min(512, t)

    def body(o_ref, g_ref, w_ref, out_ref):
        w = w_ref[...]
        for h in range(GLA_HEADS):
            sl = slice(h * GLA_DV, (h + 1) * GLA_DV)
            o = o_ref[:, sl]
            g = g_ref[:, sl]
            on = o * lax.rsqrt(jnp.mean(o * o, axis=-1, keepdims=True) + RMS_EPS)
            out_ref[:, sl] = (on * w * (g * jax.nn.sigmoid(g))).astype(BF16)

    return pl.pallas_call(
        body, name="gla_norm_fwd", grid=(t // r,),
        in_specs=[pl.BlockSpec((r, D_GLA), lambda i: (i, 0)),
                  pl.BlockSpec((r, D_GLA), lambda i: (i, C_GG // D_GLA)),
                  pl.BlockSpec((1, GLA_DV), lambda i: (0, 0))],
        out_specs=pl.BlockSpec((r, D_GLA), lambda i: (i, 0)),
        out_shape=jax.ShapeDtypeStruct((t, D_GLA), BF16),
        compiler_params=_cparams(("parallel",)),
    )(o_raw, proj, norm_w)


def _swa_masks(i):
    qi = lax.broadcasted_iota(jnp.int32, (WINDOW, 1), 0)
    kj = lax.broadcasted_iota(jnp.int32, (1, 2 * WINDOW), 1)
    valid = (kj > qi) & (kj <= qi + WINDOW) & ((i > 0) | (kj >= WINDOW))
    lo = lax.broadcasted_iota(jnp.int32, (1, LANES), 1) < SWA_DH
    return valid, lo


def _dup_half(x, lo, kv):
    xr = pltpu.roll(x, SWA_DH, axis=1)
    return jnp.where(lo, x, xr) if kv == 0 else jnp.where(lo, xr, x)


def _swa_probs(qm, kdup, valid, sink):
    s = _dot(qm, kdup, NT) * (SWA_DH ** -0.5)
    s = jnp.where(valid, s, NEG)
    m = jnp.maximum(jnp.max(s, axis=-1, keepdims=True), sink)
    p = jnp.exp(s - m)
    es = jnp.exp(sink - m)
    inv = 1.0 / (jnp.sum(p, axis=-1, keepdims=True) + es)
    return p * inv, es * inv


def _swa_fwd(proj, sinks, dep=None):
    t = proj.shape[0]

    deps, dep_specs = _dep_operand(dep)

    def body(sink_ref, q_ref, kp_ref, kc_ref, vp_ref, vc_ref, *rest):
        o_ref = rest[-1]
        i = pl.program_id(0)
        valid, lo = _swa_masks(i)
        kb = jnp.concatenate([kp_ref[...], kc_ref[...]], axis=0)
        vb = jnp.concatenate([vp_ref[...], vc_ref[...]], axis=0)
        for kv in range(2):
            kdup = _dup_half(kb, lo, kv).astype(BF16)
            vdup = _dup_half(vb, lo, kv).astype(BF16)
            for p in range(4):
                sl = slice(LANES * (4 * kv + p), LANES * (4 * kv + p + 1))
                qp = q_ref[:, sl]
                halves = []
                for e in range(2):
                    qm = jnp.where(lo if e == 0 else ~lo, qp, 0.0).astype(BF16)
                    pn, _ = _swa_probs(qm, kdup, valid, sink_ref[8 * kv + 2 * p + e])
                    halves.append(_dot(pn.astype(BF16), vdup, NN))
                o_ref[:, sl] = jnp.where(lo, halves[0], halves[1]).astype(BF16)

    kvspec = lambda col, prev: pl.BlockSpec(
        (WINDOW, LANES), (lambda i: (jnp.maximum(i - 1, 0), col)) if prev else (lambda i: (i, col)))
    return pl.pallas_call(
        body, name="swa_fwd", grid=(t // WINDOW,),
        in_specs=[pl.BlockSpec(memory_space=pltpu.SMEM),
                  pl.BlockSpec((WINDOW, D_SWA), lambda i: (i, C_QS // D_SWA)),
                  kvspec(C_KS // LANES, True), kvspec(C_KS // LANES, False),
                  kvspec(C_VS // LANES, True), kvspec(C_VS // LANES, False), *dep_specs],
        out_specs=pl.BlockSpec((WINDOW, D_SWA), lambda i: (i, 0)),
        out_shape=jax.ShapeDtypeStruct((t, D_SWA), BF16),
        compiler_params=_cparams(("parallel",)),
    )(sinks, proj, proj, proj, proj, proj, *deps)


def _mix_ln1(cat, w_out, x, ln_g, ln_b):
    t = cat.shape[0]
    tm, tk = min(512, t), 512

    def epilogue(acc_ref, extra, outs):
        x_ref, g_ref, b_ref = extra
        r1_ref, h1_ref, h1b_ref = outs
        for r0, n in _row_chunks(tm):
            rs = slice(r0, r0 + n)
            r1 = ALPHA * x_ref[rs, :] + acc_ref[rs, :]
            xhat, _ = _ln_stats(r1)
            h = xhat * g_ref[...] + b_ref[...]
            r1_ref[rs, :] = r1
            h1_ref[rs, :] = h
            h1b_ref[rs, :] = h.astype(BF16)

    row = pl.BlockSpec((tm, D_MODEL), lambda i, j, k: (i, 0))
    vec = pl.BlockSpec((1, D_MODEL), lambda i, j, k: (0, 0))
    return _matmul(
        "mix_ln1", cat, w_out, NN, (t // tm, 1, D_MODEL // tk),
        pl.BlockSpec((tm, tk), lambda i, j, k: (i, k)),
        pl.BlockSpec((tk, D_MODEL), lambda i, j, k: (k, 0)),
        [jax.ShapeDtypeStruct((t, D_MODEL), F32), jax.ShapeDtypeStruct((t, D_MODEL), F32),
         jax.ShapeDtypeStruct((t, D_MODEL), BF16)],
        [row, row, row], (tm, D_MODEL),
        extra=(x, ln_g, ln_b), extra_specs=(row, vec, vec), epilogue=epilogue)


def _mlp_up(h1b, w_up):
    t = h1b.shape[0]
    tm, tn = min(1024, t), 1024

    def epilogue(acc_ref, extra, outs):
        a_ref, hdn_ref = outs
        for r0, n in _row_chunks(tm, 256):
            rs = slice(r0, r0 + n)
            a = jnp.maximum(acc_ref[rs, :], 0.0)
            a_ref[rs, :] = a.astype(BF16)
            hdn_ref[rs, :] = (a * a).astype(BF16)

    out = pl.BlockSpec((tm, tn), lambda i, j, k: (i, j))
    return _matmul(
        "mlp_up", h1b, w_up, NN, (t // tm, D_FF // tn, 1),
        pl.BlockSpec((tm, D_MODEL), lambda i, j, k: (i, 0)),
        pl.BlockSpec((None, D_MODEL, tn), lambda i, j, k: (j, 0, 0)),
        [jax.ShapeDtypeStruct((t, D_FF), BF16), jax.ShapeDtypeStruct((t, D_FF), BF16)],
        [out, out], (tm, tn), epilogue=epilogue)


def _mlp_down_loss(hdn, w_down, h1, target, ln_g, ln_b):
    t = hdn.shape[0]
    tm, tk = min(512, t), 512

    def epilogue(acc_ref, extra, outs):
        h1_ref, t_ref, g_ref, b_ref = extra
        dr2_ref, dr2b_ref, gg_ref, gb_ref, loss_ref = outs

        @pl.when(pl.program_id(0) == 0)
        def _():
            gg_ref[...] = jnp.zeros_like(gg_ref)
            gb_ref[...] = jnp.zeros_like(gb_ref)
            loss_ref[...] = jnp.zeros_like(loss_ref)

        for r0, n in _row_chunks(tm):
            rs = slice(r0, r0 + n)
            xhat, rstd = _ln_stats(ALPHA * h1_ref[rs, :] + acc_ref[rs, :])
            err = xhat * g_ref[...] + b_ref[...] - t_ref[rs, :]
            loss_ref[...] += 0.5 * jnp.sum(jnp.mean(err * err, axis=-1, keepdims=True))
            dy = err * (1.0 / D_MODEL)
            gg_ref[...] += jnp.sum(dy * xhat, axis=0, keepdims=True)
            gb_ref[...] += jnp.sum(dy, axis=0, keepdims=True)
            dr2 = _ln_bwd(dy * g_ref[...], xhat, rstd)
            dr2_ref[rs, :] = dr2
            dr2b_ref[rs, :] = dr2.astype(BF16)

    row = pl.BlockSpec((tm, D_MODEL), lambda i, j, k: (i, 0))
    vec = pl.BlockSpec((1, D_MODEL), lambda i, j, k: (0, 0))
    return _matmul(
        "mlp_down_loss", hdn, w_down, NN, (t // tm, 1, D_FF // tk),
        pl.BlockSpec((tm, tk), lambda i, j, k: (i, k)),
        pl.BlockSpec((tk, D_MODEL), lambda i, j, k: (k, 0)),
        [jax.ShapeDtypeStruct((t, D_MODEL), F32), jax.ShapeDtypeStruct((t, D_MODEL), BF16),
         jax.ShapeDtypeStruct((1, D_MODEL), F32), jax.ShapeDtypeStruct((1, D_MODEL), F32),
         jax.ShapeDtypeStruct((1, LANES), F32)],
        [row, row, vec, vec, pl.BlockSpec((1, LANES), lambda i, j, k: (0, 0))],
        (tm, D_MODEL),
        extra=(h1, target, ln_g, ln_b), extra_specs=(row, row, vec, vec), epilogue=epilogue,
        dims=("arbitrary", "arbitrary", "arbitrary"))


def _mlp_down_bwd(dr2b, w_down, a_act):
    t = dr2b.shape[0]
    tm, tn = min(1024, t), 1024

    def epilogue(acc_ref, extra, outs):
        (a_ref,) = extra
        for r0, n in _row_chunks(tm, 256):
            rs = slice(r0, r0 + n)
            outs[0][rs, :] = (acc_ref[rs, :] * (2.0 * a_ref[rs, :].astype(F32))).astype(BF16)

    blk = pl.BlockSpec((tm, tn), lambda i, j, k: (i, j))
    return _matmul(
        "mlp_down_bwd", dr2b, w_down, NT, (t // tm, D_FF // tn, 1),
        pl.BlockSpec((tm, D_MODEL), lambda i, j, k: (i, 0)),
        pl.BlockSpec((tn, D_MODEL), lambda i, j, k: (j, 0)),
        jax.ShapeDtypeStruct((t, D_FF), BF16), blk, (tm, tn),
        extra=(a_act,), extra_specs=(blk,), epilogue=epilogue)


def _grad_w_down(hdn, dr2b):
    t = hdn.shape[0]
    tm, tn = 1024, 1024
    return _matmul(
        "grad_w_down", hdn, dr2b, TN, (D_MODEL // tn, D_FF // tm, 1),
        pl.BlockSpec((t, tm), lambda j, i, k: (0, i)),
        pl.BlockSpec((t, tn), lambda j, i, k: (0, j)),
        jax.ShapeDtypeStruct((D_FF, D_MODEL), F32),
        pl.BlockSpec((tm, tn), lambda j, i, k: (i, j)), (tm, tn))


def _grad_w_up(h1b, du, dep=None):
    t = h1b.shape[0]
    tm, tn = 1024, 1024
    return _matmul(
        "grad_w_up", h1b, du, TN, (N_DEV, D_MODEL // tm, 1),
        pl.BlockSpec((t, tm), lambda i, j, k: (0, j)),
        pl.BlockSpec((t, tn), lambda i, j, k: (0, i)),
        jax.ShapeDtypeStruct((N_DEV, D_MODEL, D_FF // N_DEV), F32),
        pl.BlockSpec((None, tm, tn), lambda i, j, k: (i, j, 0)), (tm, tn), dep=dep)


def _mlp_up_bwd_ln1(du, w_up, dr2, r1, ln_g, dep=None):
    t = du.shape[0]
    tm, tk = min(256, t), D_FF // N_DEV

    def epilogue(acc_ref, extra, outs):
        dr2_ref, r1_ref, g_ref = extra
        dr1_ref, gg_ref, gb_ref = outs

        @pl.when(pl.program_id(0) == 0)
        def _():
            gg_ref[...] = jnp.zeros_like(gg_ref)
            gb_ref[...] = jnp.zeros_like(gb_ref)

        for r0, n in _row_chunks(tm):
            rs = slice(r0, r0 + n)
            dh1 = ALPHA * dr2_ref[rs, :] + acc_ref[rs, :]
            xhat, rstd = _ln_stats(r1_ref[rs, :])
            gg_ref[...] += jnp.sum(dh1 * xhat, axis=0, keepdims=True)
            gb_ref[...] += jnp.sum(dh1, axis=0, keepdims=True)
            dr1_ref[rs, :] = _ln_bwd(dh1 * g_ref[...], xhat, rstd)

    row = pl.BlockSpec((tm, D_MODEL), lambda i, j, k: (i, 0))
    vec = pl.BlockSpec((1, D_MODEL), lambda i, j, k: (0, 0))
    return _matmul(
        "mlp_up_bwd_ln1", du, w_up, NT, (t // tm, 1, N_DEV),
        pl.BlockSpec((tm, tk), lambda i, j, k: (i, k)),
        pl.BlockSpec((None, D_MODEL, tk), lambda i, j, k: (k, 0, 0)),
        [jax.ShapeDtypeStruct((t, D_MODEL), F32), jax.ShapeDtypeStruct((1, D_MODEL), F32),
         jax.ShapeDtypeStruct((1, D_MODEL), F32)],
        [row, vec, vec], (tm, D_MODEL),
        extra=(dr2, r1, ln_g), extra_specs=(row, row, vec), epilogue=epilogue,
        dims=("arbitrary", "arbitrary", "arbitrary"), dep=dep)


def _dcat(dr1, w_out):
    t = dr1.shape[0]
    tm, tn = min(512, t), 1024
    return _matmul(
        "dcat", dr1, w_out, NT, (t // tm, D_MODEL // tn, 1),
        pl.BlockSpec((tm, D_MODEL), lambda i, j, k: (i, 0)),
        pl.BlockSpec((tn, D_MODEL), lambda i, j, k: (j, 0)),
        jax.ShapeDtypeStruct((t, D_MODEL), F32),
        pl.BlockSpec((tm, tn), lambda i, j, k: (i, j)), (tm, tn))


def _grad_w_out(cat, dr1, dep=None):
    t = cat.shape[0]
    tm, tn, tk = 1024, 1024, min(2048, t)
    return _matmul(
        "grad_w_out", cat, dr1, TN, (D_MODEL // tm, D_MODEL // tn, t // tk),
        pl.BlockSpec((tk, tm), lambda i, j, k: (k, i)),
        pl.BlockSpec((tk, tn), lambda i, j, k: (k, j)),
        jax.ShapeDtypeStruct((D_MODEL, D_MODEL), F32),
        pl.BlockSpec((tm, tn), lambda i, j, k: (i, j)), (tm, tn), dep=dep)


def _gla_norm_bwd(dcat, o_raw, proj, norm_w, dep=None):
    t = o_raw.shape[0]
    r = min(512, t)

    deps, dep_specs = _dep_operand(dep)

    def body(d_ref, o_ref, g_ref, w_ref, *rest):
        do_ref, dg_ref, dw_ref = rest[len(deps):]

        @pl.when(pl.program_id(0) == 0)
        def _():
            dw_ref[...] = jnp.zeros_like(dw_ref)

        w = w_ref[...]
        dw = jnp.zeros((1, GLA_DV), F32)
        for h in range(GLA_HEADS):
            sl = slice(h * GLA_DV, (h + 1) * GLA_DV)
            o = o_ref[:, sl]
            g = g_ref[:, sl]
            d = d_ref[:, sl]
            rr = lax.rsqrt(jnp.mean(o * o, axis=-1, keepdims=True) + RMS_EPS)
            on = o * rr
            sg = jax.nn.sigmoid(g)
            sil = g * sg
            dg_ref[:, sl] = (d * on * w * (sg * (1.0 + g * (1.0 - sg)))).astype(BF16)
            dw = dw + jnp.sum(d * on * sil, axis=0, keepdims=True)
            don = d * w * sil
            do_ref[:, sl] = rr * (don - on * jnp.mean(don * on, axis=-1, keepdims=True))
        dw_ref[...] += dw

    return pl.pallas_call(
        body, name="gla_norm_bwd", grid=(t // r,),
        in_specs=[pl.BlockSpec((r, D_GLA), lambda i: (i, 0)),
                  pl.BlockSpec((r, D_GLA), lambda i: (i, 0)),
                  pl.BlockSpec((r, D_GLA), lambda i: (i, C_GG // D_GLA)),
                  pl.BlockSpec((1, GLA_DV), lambda i: (0, 0)), *dep_specs],
        out_specs=[pl.BlockSpec((r, D_GLA), lambda i: (i, 0)),
                   pl.BlockSpec((r, D_GLA), lambda i: (i, 0)),
                   pl.BlockSpec((1, GLA_DV), lambda i: (0, 0))],
        out_shape=[jax.ShapeDtypeStruct((t, D_GLA), F32), jax.ShapeDtypeStruct((t, D_GLA), BF16),
                   jax.ShapeDtypeStruct((1, GLA_DV), F32)],
        compiler_params=_cparams(("arbitrary",)),
    )(dcat, o_raw, proj, norm_w, *deps)


def _gla_bwd(proj, bloc, do_raw, states, triu):
    t = proj.shape[0]
    r = min(256, t)
    ncb = r // GC
    nb = t // r
    scale = GLA_DK ** -0.5

    def body(q_ref, k_ref, v_ref, b_ref, do_ref, st_ref, u_ref, dq_ref, dk_ref, dv_ref, dg_ref, ds_scr):
        @pl.when(pl.program_id(1) == 0)
        def _():
            ds_scr[...] = jnp.zeros_like(ds_scr)

        rows = lax.broadcasted_iota(jnp.int32, (GC, 1), 0)

        def chunk(cc, carry):
            c = ncb - 1 - cc
            r0 = pl.multiple_of(c * GC, GC)
            q = q_ref[pl.ds(r0, GC), :] * scale
            k = k_ref[pl.ds(r0, GC), :]
            v = v_ref[pl.ds(r0, GC), :]
            b = b_ref[pl.ds(r0, GC), :]
            do = do_ref[pl.ds(r0, GC), :]
            st = st_ref[c]
            dsn = ds_scr[...]
            bl = b[GC - 1:GC, :]
            eb = jnp.exp(b)
            ekl = jnp.exp(bl - b)
            ebl = jnp.exp(bl)
            qh = q * eb
            kd = k * ekl
            dob = do.astype(BF16)
            dsb = dsn.astype(BF16)
            dqh = _dot(dob, st.astype(BF16), NN)
            dkd = _dot(v.astype(BF16), dsb, NN)
            dv = _dot(kd.astype(BF16), dsb, NT)
            dq_i = jnp.zeros((GC, GLA_DK), F32)
            dk_i = jnp.zeros((GC, GLA_DK), F32)
            dk_x = jnp.zeros((GC, GLA_DK), F32)
            kr = _bf16_round(k)
            vr = _bf16_round(v)
            dor = _bf16_round(do)
            for j in range(GC):
                w = jnp.exp(jnp.where(rows >= j, b - b[j:j + 1, :], NEG))
                qw = q * w
                qwr = _bf16_round(qw)
                a = _bf16_round(jnp.sum(qwr * kr[j:j + 1, :], axis=-1, keepdims=True))
                da = jnp.sum(dor * vr[j:j + 1, :], axis=-1, keepdims=True)
                dq_i = dq_i + da * (w * k[j:j + 1, :])
                dk_x = jnp.where(rows == j, jnp.sum(da * qw, axis=0, keepdims=True), dk_x)
                dk_i = jnp.where(rows == j, jnp.sum(_bf16_round(da) * qwr, axis=0, keepdims=True), dk_i)
                dv = dv + jnp.where(rows == j, jnp.sum(a * dor, axis=0, keepdims=True), 0.0)
            dqs = dqh * eb + dq_i
            dk = dkd * ekl + dk_i
            db_last = jnp.sum(dkd * kd, axis=0, keepdims=True) + ebl * jnp.sum(dsn * st, axis=0, keepdims=True)
            db = q * dqs - k * (dkd * ekl + dk_x) + jnp.where(rows == GC - 1, db_last, 0.0)
            dq_ref[pl.ds(r0, GC), :] = (dqs * scale).astype(BF16)
            dk_ref[pl.ds(r0, GC), :] = dk.astype(BF16)
            dv_ref[pl.ds(r0, GC), :] = dv.astype(BF16)
            dg_ref[pl.ds(r0, GC), :] = _dot(u_ref[...], db, NN, precision=lax.Precision.HIGHEST)
            ds_scr[...] = dsn * ebl + _dot(dob, qh.astype(BF16), TN)
            return carry

        lax.fori_loop(0, ncb, chunk, 0)

    rev = lambda i: nb - 1 - i
    return pl.pallas_call(
        body, name="gla_bwd", grid=(GLA_HEADS, nb),
        in_specs=[pl.BlockSpec((r, GLA_DK), lambda h, i: (rev(i), C_QG // GLA_DK + h)),
                  pl.BlockSpec((r, GLA_DK), lambda h, i: (rev(i), C_KG // GLA_DK + h)),
                  pl.BlockSpec((r, GLA_DV), lambda h, i: (rev(i), C_VG // GLA_DV + h)),
                  pl.BlockSpec((r, GLA_DK), lambda h, i: (rev(i), h)),
                  pl.BlockSpec((r, GLA_DV), lambda h, i: (rev(i), h)),
                  pl.BlockSpec((ncb, None, GLA_DV, GLA_DK), lambda h, i: (rev(i), h, 0, 0)),
                  pl.BlockSpec((GC, GC), lambda h, i: (0, 0))],
        out_specs=[pl.BlockSpec((r, GLA_DK), lambda h, i: (rev(i), h)),
                   pl.BlockSpec((r, GLA_DK), lambda h, i: (rev(i), h)),
                   pl.BlockSpec((r, GLA_DV), lambda h, i: (rev(i), h)),
                   pl.BlockSpec((r, GLA_DK), lambda h, i: (rev(i), h))],
        out_shape=[jax.ShapeDtypeStruct((t, GLA_KW), BF16), jax.ShapeDtypeStruct((t, GLA_KW), BF16),
                   jax.ShapeDtypeStruct((t, D_GLA), BF16), jax.ShapeDtypeStruct((t, GLA_KW), F32)],
        scratch_shapes=[pltpu.VMEM((GLA_DV, GLA_DK), F32)],
        compiler_params=_cparams(("parallel", "arbitrary")),
    )(proj, proj, proj, bloc, do_raw, states, triu)


def _gate_bwd(dg, dgdz, proj, w2p):
    t = dg.shape[0]
    r = min(512, t)

    def body(dg_ref, s_ref, lo_ref, w_ref, dlo_ref, gw_ref, gb_ref):
        @pl.when(pl.program_id(0) == 0)
        def _():
            gw_ref[...] = jnp.zeros_like(gw_ref)
            gb_ref[...] = jnp.zeros_like(gb_ref)

        dz = dg_ref[...] * s_ref[...]
        dzb = dz.astype(BF16)
        gb_ref[...] += jnp.sum(dz, axis=0, keepdims=True)
        gw_ref[...] += _dot(lo_ref[...].astype(BF16), dzb, TN)
        dlo_ref[...] = _dot(dzb, w_ref[...], NT).astype(BF16)

    return pl.pallas_call(
        body, name="gate_bwd", grid=(t // r,),
        in_specs=[pl.BlockSpec((r, GLA_KW), lambda i: (i, 0)),
                  pl.BlockSpec((r, GLA_KW), lambda i: (i, 0)),
                  pl.BlockSpec((r, LANES), lambda i: (i, C_LO // LANES)),
                  pl.BlockSpec((LANES, GLA_KW), lambda i: (0, 0))],
        out_specs=[pl.BlockSpec((r, LANES), lambda i: (i, 0)),
                   pl.BlockSpec((LANES, GLA_KW), lambda i: (0, 0)),
                   pl.BlockSpec((1, GLA_KW), lambda i: (0, 0))],
        out_shape=[jax.ShapeDtypeStruct((t, LANES), BF16), jax.ShapeDtypeStruct((LANES, GLA_KW), F32),
                   jax.ShapeDtypeStruct((1, GLA_KW), F32)],
        compiler_params=_cparams(("arbitrary",)),
    )(dg, dgdz, proj, w2p)


def _swa_bwd(proj, dcat, sinks, dep=None):
    t = proj.shape[0]

    deps, dep_specs = _dep_operand(dep)

    def body(sink_ref, q_ref, kp_ref, kc_ref, vp_ref, vc_ref, d_ref, *rest):
        dq_ref, dk_ref, dv_ref, dsink_ref = rest[len(deps):]
        i = pl.program_id(0)

        @pl.when(i == 0)
        def _():
            dk_ref[...] = jnp.zeros_like(dk_ref)
            dv_ref[...] = jnp.zeros_like(dv_ref)
            dsink_ref[...] = jnp.zeros_like(dsink_ref)

        valid, lo = _swa_masks(i)
        lane = lax.broadcasted_iota(jnp.int32, (1, LANES), 1)
        kb = jnp.concatenate([kp_ref[...], kc_ref[...]], axis=0)
        vb = jnp.concatenate([vp_ref[...], vc_ref[...]], axis=0)
        dsink = jnp.zeros((1, LANES), F32)
        folded_k, folded_v = [], []
        for kv in range(2):
            kdup = _dup_half(kb, lo, kv).astype(BF16)
            vdup = _dup_half(vb, lo, kv).astype(BF16)
            dkd = jnp.zeros((2 * WINDOW, LANES), F32)
            dvd = jnp.zeros((2 * WINDOW, LANES), F32)
            for p in range(4):
                sl = slice(LANES * (4 * kv + p), LANES * (4 * kv + p + 1))
                qp = q_ref[:, sl]
                dp_ = d_ref[:, sl]
                halves = []
                for e in range(2):
                    h = 8 * kv + 2 * p + e
                    half = lo if e == 0 else ~lo
                    qm = jnp.where(half, qp, 0.0).astype(BF16)
                    dom = jnp.where(half, dp_, 0.0).astype(BF16)
                    pn, psink = _swa_probs(qm, kdup, valid, sink_ref[h])
                    dpr = _dot(dom, vdup, NT)
                    drow = jnp.sum(dpr * pn, axis=-1, keepdims=True)
                    dsink = dsink + jnp.where(lane == h, -jnp.sum(psink * drow), 0.0)
                    dsb = (pn * (dpr - drow) * (SWA_DH ** -0.5)).astype(BF16)
                    halves.append(_dot(dsb, kdup, NN))
                    dkd = dkd + _dot(dsb, qm, TN)
                    dvd = dvd + _dot(pn.astype(BF16), dom, TN)
                dq_ref[:, sl] = jnp.where(lo, halves[0], halves[1]).astype(BF16)
            folded_k.append(dkd + pltpu.roll(dkd, SWA_DH, axis=1))
            folded_v.append(dvd + pltpu.roll(dvd, SWA_DH, axis=1))
        dkb = jnp.where(lo, folded_k[0], folded_k[1])
        dvb = jnp.where(lo, folded_v[0], folded_v[1])
        dsink_ref[...] += dsink
        cur = pl.ds(pl.multiple_of(i * WINDOW, WINDOW), WINDOW)
        dk_ref[cur, :] += dkb[WINDOW:, :]
        dv_ref[cur, :] += dvb[WINDOW:, :]

        @pl.when(i > 0)
        def _():
            prev = pl.ds(pl.multiple_of((i - 1) * WINDOW, WINDOW), WINDOW)
            dk_ref[prev, :] += dkb[:WINDOW, :]
            dv_ref[prev, :] += dvb[:WINDOW, :]

    kvspec = lambda col, prev: pl.BlockSpec(
        (WINDOW, LANES), (lambda i: (jnp.maximum(i - 1, 0), col)) if prev else (lambda i: (i, col)))
    full = pl.BlockSpec((t, LANES), lambda i: (0, 0))
    return pl.pallas_call(
        body, name="swa_bwd", grid=(t // WINDOW,),
        in_specs=[pl.BlockSpec(memory_space=pltpu.SMEM),
                  pl.BlockSpec((WINDOW, D_SWA), lambda i: (i, C_QS // D_SWA)),
                  kvspec(C_KS // LANES, True), kvspec(C_KS // LANES, False),
                  kvspec(C_VS // LANES, True), kvspec(C_VS // LANES, False),
                  pl.BlockSpec((WINDOW, D_SWA), lambda i: (i, 1)), *dep_specs],
        out_specs=[pl.BlockSpec((WINDOW, D_SWA), lambda i: (i, 0)), full, full,
                   pl.BlockSpec((1, LANES), lambda i: (0, 0))],
        out_shape=[jax.ShapeDtypeStruct((t, D_SWA), BF16), jax.ShapeDtypeStruct((t, LANES), F32),
                   jax.ShapeDtypeStruct((t, LANES), F32), jax.ShapeDtypeStruct((1, LANES), F32)],
        compiler_params=_cparams(("arbitrary",)),
    )(sinks, proj, proj, proj, proj, proj, dcat, *deps)


def _grad_w_in(xb, dproj):
    t = xb.shape[0]
    tm, tn = 1024, 640
    return _matmul(
        "grad_w_in", xb, dproj, TN, (D_MODEL // tm, D_INP // tn, 1),
        pl.BlockSpec((t, tm), lambda i, j, k: (0, i)),
        pl.BlockSpec((t, tn), lambda i, j, k: (0, j)),
        jax.ShapeDtypeStruct((D_MODEL, D_INP), F32),
        pl.BlockSpec((tm, tn), lambda i, j, k: (i, j)), (tm, tn))


def _grad_x(dproj, w_in_p, dr1, dep=None):
    t = dproj.shape[0]
    tm, tk = min(512, t), 640

    def epilogue(acc_ref, extra, outs):
        for r0, n in _row_chunks(tm):
            rs = slice(r0, r0 + n)
            outs[0][rs, :] = ALPHA * extra[0][rs, :] + acc_ref[rs, :]

    row = pl.BlockSpec((tm, D_MODEL), lambda i, j, k: (i, 0))
    return _matmul(
        "grad_x", dproj, w_in_p, NT, (t // tm, 1, D_INP // tk),
        pl.BlockSpec((tm, tk), lambda i, j, k: (i, k)),
        pl.BlockSpec((D_MODEL, tk), lambda i, j, k: (0, k)),
        jax.ShapeDtypeStruct((t, D_MODEL), F32), row, (tm, D_MODEL),
        extra=(dr1,), extra_specs=(row,), epilogue=epilogue, dep=dep)


def _place():
    x, y, c = lax.axis_index("x"), lax.axis_index("y"), lax.axis_index("c")
    chips = [(1 - x, y), (x, 1 - y), (1 - x, 1 - y)]
    return x, y, c, chips


def _all_gather(shards):
    n = len(shards)
    hbm = pl.BlockSpec(memory_space=pl.ANY)

    def body(*refs):
        ins, outs = refs[:n], refs[n:2 * n]
        send, recv, loc = refs[2 * n:]
        x, y, c, chips = _place()
        me, sib = (x, y, c), (x, y, 1 - c)

        def slot(t, px, py, pc):
            return outs[t].at[4 * px + 2 * py + pc]

        def copy(t, k, block, to, src=None):
            return pltpu.make_async_remote_copy(
                src_ref=slot(t, *block) if src is None else src, dst_ref=slot(t, *block),
                send_sem=send.at[7 * t + k], recv_sem=recv.at[7 * t + k], device_id=to, device_id_type=MESH)

        mine = [pltpu.make_async_copy(ins[t], slot(t, *me), loc.at[t]) for t in range(n)]
        for cp in mine:
            cp.start()
        sent = []
        for t in range(n):
            sent.append(copy(t, 0, me, sib, src=ins[t]))
            sent += [copy(t, 1 + j, me, (*chip, c), src=ins[t]) for j, chip in enumerate(chips)]
        for cp in sent:
            cp.start()
        for t in range(n):
            for j, chip in enumerate(chips):
                copy(t, 1 + j, (*chip, c), me).wait_recv()
                fwd = copy(t, 4 + j, (*chip, c), sib)
                fwd.start()
                sent.append(fwd)
        for t in range(n):
            copy(t, 0, sib, me).wait_recv()
            for j, chip in enumerate(chips):
                copy(t, 4 + j, (*chip, 1 - c), me).wait_recv()
        for cp in sent:
            cp.wait_send()
        for cp in mine:
            cp.wait()

    return pl.pallas_call(
        body, name="all_gather_weights",
        in_specs=[hbm] * n, out_specs=[hbm] * n,
        out_shape=[jax.ShapeDtypeStruct((N_DEV,) + s.shape, s.dtype) for s in shards],
        scratch_shapes=[pltpu.SemaphoreType.DMA((7 * n,)), pltpu.SemaphoreType.DMA((7 * n,)),
                        pltpu.SemaphoreType.DMA((n,))],
    )(*shards)


def _plan_gather_out(src, land, x, y, c, chips):
    me = 4 * x + 2 * y + c
    return [(src, land.at[me], to) for to in [(x, y, 1 - c)] + [(px, py, c) for px, py in chips]]


def _plan_gather_forward(src, land, x, y, c, chips):
    return [(land.at[4 * px + 2 * py + c], land.at[4 * px + 2 * py + c], (x, y, 1 - c)) for px, py in chips]


def _plan_sibling(src, land, x, y, c, chips):
    return [(src.at[2 * q + (1 - c)], land.at[q], (x, y, 1 - c)) for q in range(4)]


def _plan_chips(src, land, x, y, c, chips):
    return [(src.at[j], land.at[j], (px, py, c)) for j, (px, py) in enumerate(chips)]


_PLAN_COPIES = {_plan_gather_out: 4, _plan_gather_forward: 3, _plan_sibling: 4, _plan_chips: 3}
_HBM = pl.BlockSpec(memory_space=pltpu.HBM)
_SEM = pl.BlockSpec(memory_space=pltpu.SEMAPHORE)
_EFFECT = pltpu.SideEffectType.DATAFLOW_SIDE_EFFECTING


def _hbm(a):
    return pltpu.with_memory_space_constraint(a, pltpu.HBM)


def _plan_descriptors(plans, srcs, lands, send, recv):
    x, y, c, chips = _place()
    cps = []
    for plan, src, land in zip(plans, srcs, lands):
        for s_ref, d_ref, to in plan(src, land, x, y, c, chips):
            k = len(cps)
            cps.append(pltpu.make_async_remote_copy(src_ref=s_ref, dst_ref=d_ref, send_sem=send.at[k],
                                                    recv_sem=recv.at[k], device_id=to, device_id_type=MESH))
    return cps


def _copies_start(name, plans, srcs, lands, after):
    has_src = [s is not None for s in srcs]
    arrays = [s for s in srcs if s is not None] + list(lands)
    n_src = sum(has_src)
    n_cp = sum(_PLAN_COPIES[p] for p in plans)

    def body(*refs):
        ins = refs[:len(arrays)]
        send, recv = refs[len(arrays) + 1], refs[len(arrays) + 2]
        token = refs[-1]
        it = iter(ins[:n_src])
        src_refs = [next(it) if h else None for h in has_src]
        for cp in _plan_descriptors(plans, src_refs, ins[n_src:], send, recv):
            cp.start()
        token[...] = jnp.zeros_like(token)

    outs = pl.pallas_call(
        body, name=name,
        in_specs=[_HBM] * len(arrays) + [pl.BlockSpec(memory_space=pl.ANY)],
        out_specs=(_SEM, _SEM, *[_HBM] * len(arrays), pl.BlockSpec(memory_space=pltpu.VMEM)),
        out_shape=(pltpu.SemaphoreType.DMA((n_cp,)), pltpu.SemaphoreType.DMA((n_cp,)),
                   *[pltpu.HBM(a.shape, a.dtype) for a in arrays], jax.ShapeDtypeStruct((8, LANES), F32)),
        input_output_aliases={i: 2 + i for i in range(len(arrays))},
        compiler_params=pltpu.CompilerParams(has_side_effects=_EFFECT),
    )(*[_hbm(a) for a in arrays], after)
    send, recv = outs[0], outs[1]
    thru = list(outs[2:-1])
    it = iter(thru[:n_src])
    return send, recv, [next(it) if h else None for h in has_src], thru[n_src:], outs[-1]


def _copies_wait(name, plans, started, after):
    send, recv, srcs, lands, _ = started
    has_src = [s is not None for s in srcs]
    arrays = [s for s in srcs if s is not None] + list(lands)
    n_src = sum(has_src)

    def body(*refs):
        ins = refs[:len(arrays)]
        send_ref, recv_ref = refs[len(arrays)], refs[len(arrays) + 1]
        it = iter(ins[:n_src])
        src_refs = [next(it) if h else None for h in has_src]
        for cp in _plan_descriptors(plans, src_refs, ins[n_src:], send_ref, recv_ref):
            cp.wait_send()
            cp.wait_recv()

    outs = pl.pallas_call(
        body, name=name,
        in_specs=[_HBM] * len(arrays) + [_SEM, _SEM, pl.BlockSpec(memory_space=pl.ANY)],
        out_specs=tuple([_HBM] * len(arrays)),
        out_shape=tuple(pltpu.HBM(a.shape, a.dtype) for a in arrays),
        input_output_aliases={i: i for i in range(len(arrays))},
        compiler_params=pltpu.CompilerParams(has_side_effects=_EFFECT),
    )(*arrays, send, recv, after)
    return list(outs[:n_src]), list(outs[n_src:])


def _pair_sum(name, grad, from_sibling, blocks):
    _, rows, cols = grad.shape
    tr = 256

    def body(blk_ref, g_ref, s_ref, o_ref):
        o_ref[...] = (g_ref[...] + s_ref[...]).astype(BF16)

    return pl.pallas_call(
        body, name=name,
        grid_spec=pltpu.PrefetchScalarGridSpec(
            num_scalar_prefetch=1, grid=(3, rows // tr),
            in_specs=[pl.BlockSpec((None, tr, cols), lambda j, r, br: (br[j], r, 0)),
                      pl.BlockSpec((None, tr, cols), lambda j, r, br: (br[3 + j], r, 0))],
            out_specs=pl.BlockSpec((None, tr, cols), lambda j, r, br: (j, r, 0))),
        out_shape=jax.ShapeDtypeStruct((3, rows, cols), BF16),
        compiler_params=_cparams(("parallel", "parallel")),
    )(blocks, grad, from_sibling)


def _adam_math(g, w, m, v):
    m2 = ADAM_B1 * m + (1.0 - ADAM_B1) * g
    v2 = ADAM_B2 * v + (1.0 - ADAM_B2) * (g * g)
    m_hat = m2 / (1.0 - ADAM_B1 ** ADAM_STEP)
    v_hat = v2 / (1.0 - ADAM_B2 ** ADAM_STEP)
    delta = -ADAM_LR * (m_hat / (jnp.sqrt(v_hat) + ADAM_EPS) + ADAM_WD * w)
    return delta, m2, v2


def _sum_adam(name, grad, from_sibling, from_chips, own, w, m, v):
    rows, cols = w.shape
    tr = 128

    def body(own_ref, p_ref, s_ref, r_ref, w_ref, m_ref, v_ref, g_out, d_out, m_out, v_out):
        g = p_ref[...] + s_ref[...]
        for j in range(3):
            g = g + r_ref[j].astype(F32)
        d, m2, v2 = _adam_math(g, w_ref[...], m_ref[...], v_ref[...])
        g_out[...] = g
        d_out[...] = d
        m_out[...] = m2
        v_out[...] = v2

    blk = pl.BlockSpec((tr, cols), lambda r, cr: (r, 0))
    shp = jax.ShapeDtypeStruct((rows, cols), F32)
    return pl.pallas_call(
        body, name=name,
        grid_spec=pltpu.PrefetchScalarGridSpec(
            num_scalar_prefetch=1, grid=(rows // tr,),
            in_specs=[pl.BlockSpec((None, tr, cols), lambda r, cr: (cr[0], r, 0)),
                      pl.BlockSpec((None, tr, cols), lambda r, cr: (cr[1], r, 0)),
                      pl.BlockSpec((3, tr, cols), lambda r, cr: (0, r, 0)),
                      blk, blk, blk],
            out_specs=[blk, blk, blk, blk]),
        out_shape=[shp, shp, shp, shp],
        compiler_params=_cparams(("parallel",)),
    )(own, grad, from_sibling, from_chips, w, m, v)


def _adam_small(name, g, w, m, v):
    def body(g_ref, w_ref, m_ref, v_ref, d_out, m_out, v_out):
        d, m2, v2 = _adam_math(g_ref[...], w_ref[...], m_ref[...], v_ref[...])
        d_out[...] = d
        m_out[...] = m2
        v_out[...] = v2

    shp = jax.ShapeDtypeStruct(w.shape, F32)
    return pl.pallas_call(body, name=name, out_shape=[shp, shp, shp])(g, w, m, v)


def _all_reduce_small(pack):
    rows = pack.shape[0]

    def body(in_ref, out_ref, slots, send, recv):
        x, y, c, _ = _place()
        me = 4 * x + 2 * y + c
        slots[me] = in_ref[...]
        cps = []
        for k in range(1, N_DEV):
            dx, dy, dc = (k >> 2) & 1, (k >> 1) & 1, k & 1
            to = (jnp.bitwise_xor(x, dx), jnp.bitwise_xor(y, dy), jnp.bitwise_xor(c, dc))
            cps.append(pltpu.make_async_remote_copy(
                src_ref=in_ref, dst_ref=slots.at[me], send_sem=send.at[k - 1], recv_sem=recv.at[k - 1],
                device_id=to, device_id_type=MESH))
        for cp in cps:
            cp.start()
        for cp in cps:
            cp.wait()
        acc = slots[0]
        for d in range(1, N_DEV):
            acc = acc + slots[d]
        out_ref[...] = acc

    return pl.pallas_call(
        body, name="all_reduce_small",
        in_specs=[pl.BlockSpec(memory_space=pltpu.VMEM)],
        out_specs=pl.BlockSpec(memory_space=pltpu.VMEM),
        out_shape=jax.ShapeDtypeStruct((rows, LANES), F32),
        scratch_shapes=[pltpu.VMEM((N_DEV, rows, LANES), F32),
                        pltpu.SemaphoreType.DMA((N_DEV - 1,)), pltpu.SemaphoreType.DMA((N_DEV - 1,))],
    )(pack)


def _rows128(a):
    flat = a.reshape(-1)
    padn = (-flat.shape[0]) % (8 * LANES)
    if padn:
        flat = jnp.concatenate([flat, jnp.zeros((padn,), flat.dtype)])
    return flat.reshape(-1, LANES)


def kernel(x, w_in, w_gk2, b_gk, gla_norm_w, swa_sinks, w_out, ln1_g, ln1_b, w_up, w_down, ln2_g, ln2_b, loss_target, m_w_in, m_w_gk2, m_b_gk, m_gla_norm_w, m_swa_sinks, m_w_out, m_ln1_g, m_ln1_b, m_w_up, m_w_down, m_ln2_g, m_ln2_b, v_w_in, v_w_gk2, v_b_gk, v_gla_norm_w, v_swa_sinks, v_w_out, v_ln1_g, v_ln1_b, v_w_up, v_w_down, v_ln2_g, v_ln2_b):
    xc, yc, cc = lax.axis_index("x"), lax.axis_index("y"), lax.axis_index("c")
    me = 4 * xc + 2 * yc + cc

    x2 = x[0]
    t = x2.shape[0]
    xb = x2.astype(BF16)
    target = loss_target[0]

    first = _all_gather([w_in[0].astype(BF16), w_gk2[0].astype(BF16)])
    w_in_p = _to_padded_cols(first[0].transpose(1, 0, 2).reshape(D_MODEL, D_IN))
    w2 = first[1].transpose(1, 0, 2).reshape(GATE_RANK, GLA_KW)
    w2p = jnp.concatenate([w2, jnp.zeros((LANES - GATE_RANK, GLA_KW), BF16)], axis=0)
    shards = [w_out[0].astype(BF16), w_up[0].astype(BF16), w_down[0].astype(BF16)]
    lands = [lax.dynamic_update_index_in_dim(lax.empty((N_DEV,) + sh.shape, BF16), sh, me, 0) for sh in shards]
    out_plans = [_plan_gather_out] * 3
    fwd_plans = [_plan_gather_forward] * 3
    ag1 = _copies_start("gather_out_start", out_plans, shards, lands, first[0])

    proj = _proj(xb, w_in_p, dep=ag1[4])
    rt = min(256, t)
    ii = jnp.arange(rt)
    tri = ((ii[:, None] // GC == ii[None, :] // GC) & (ii[None, :] <= ii[:, None])).astype(F32)
    bloc, dgdz = _gate_fwd(proj, w2p, b_gk, tri)
    o_raw, states = _gla_fwd(proj, bloc)
    _, lands = _copies_wait("gather_out_wait", out_plans, ag1, o_raw)
    ag2 = _copies_start("gather_forward_start", fwd_plans, [None] * 3, lands, o_raw)
    gla_out = _gla_norm_fwd(o_raw, proj, gla_norm_w)
    swa_out = _swa_fwd(proj, swa_sinks[0], dep=ag2[4])
    cat = jnp.concatenate([gla_out, swa_out], axis=-1)
    _, gathered = _copies_wait("gather_forward_wait", fwd_plans, ag2, swa_out)
    w_out_f = gathered[0].reshape(D_MODEL, D_MODEL)
    w_up_f = gathered[1]
    w_down_f = gathered[2].reshape(D_FF, D_MODEL)
    r1, h1, h1b = _mix_ln1(cat, w_out_f, x2, ln1_g, ln1_b)
    a_act, hdn = _mlp_up(h1b, w_up_f)
    dr2, dr2b, g_ln2_g, g_ln2_b, loss_part = _mlp_down_loss(hdn, w_down_f, h1, target, ln2_g, ln2_b)

    others = [2 * (1 - xc) + yc, 2 * xc + (1 - yc), 2 * (1 - xc) + (1 - yc)]
    blocks = jnp.stack([2 * q + cc for q in others] + others).astype(jnp.int32)
    own = jnp.stack([me, 2 * xc + yc]).astype(jnp.int32)
    wmv = dict(w_in=(w_in, m_w_in, v_w_in), w_out=(w_out, m_w_out, v_w_out), w_up=(w_up, m_w_up, v_w_up),
               w_down=(w_down, m_w_down, v_w_down))
    big = {}

    def sib_land(g):
        return lax.empty((4,) + g.shape[1:], F32)

    def chip_land(g):
        return lax.empty((3,) + g.shape[1:], BF16)

    def finish(nm, g, from_sib, from_chips):
        w, m, v = wmv[nm]
        big[nm] = [o[None] for o in _sum_adam("sum_adam_" + nm, g, from_sib, from_chips, own, w[0], m[0], v[0])]

    du = _mlp_down_bwd(dr2b, w_down_f, a_act)
    g_down = _grad_w_down(hdn, dr2b).reshape(N_DEV, D_FF // N_DEV, D_MODEL)
    sa_down = _copies_start("sibling_down_start", [_plan_sibling], [g_down], [sib_land(g_down)], g_down)
    g_up = _grad_w_up(h1b, du, dep=sa_down[4])
    (g_down,), (fs_down,) = _copies_wait("sibling_down_wait", [_plan_sibling], sa_down, g_up)
    p_down = _pair_sum("pair_sum_w_down", g_down, fs_down, blocks)
    sb_down = _copies_start("chips_down_sibling_up_start", [_plan_chips, _plan_sibling], [p_down, g_up],
                            [chip_land(g_down), sib_land(g_up)], p_down)
    dr1, g_ln1_g, g_ln1_b = _mlp_up_bwd_ln1(du, w_up_f, dr2, r1, ln1_g, dep=sb_down[4])
    dcat = _dcat(dr1, w_out_f)
    (_, g_up), (fc_down, fs_up) = _copies_wait("chips_down_sibling_up_wait", [_plan_chips, _plan_sibling], sb_down,
                                               dcat)
    finish("w_down", g_down, fs_down, fc_down)
    p_up = _pair_sum("pair_sum_w_up", g_up, fs_up, blocks)
    sb_up = _copies_start("chips_up_start", [_plan_chips], [p_up], [chip_land(g_up)], p_up)
    g_out = _grad_w_out(cat, dr1, dep=sb_up[4]).reshape(N_DEV, D_MODEL // N_DEV, D_MODEL)
    sa_out = _copies_start("sibling_out_start", [_plan_sibling], [g_out], [sib_land(g_out)], g_out)
    do_raw, dg_g, g_norm_w = _gla_norm_bwd(dcat, o_raw, proj, gla_norm_w, dep=sa_out[4])
    jj = jnp.arange(GC)
    triu = (jj[None, :] >= jj[:, None]).astype(F32)
    dq_g, dk_g, dv_g, dgk = _gla_bwd(proj, bloc, do_raw, states, triu)
    _, (fc_up,) = _copies_wait("chips_up_wait", [_plan_chips], sb_up, dgk)
    (g_out,), (fs_out,) = _copies_wait("sibling_out_wait", [_plan_sibling], sa_out, dgk)
    finish("w_up", g_up, fs_up, fc_up)
    p_out = _pair_sum("pair_sum_w_out", g_out, fs_out, blocks)
    sb_out = _copies_start("chips_out_start", [_plan_chips], [p_out], [chip_land(g_out)], p_out)
    dlo, gw2_p, g_b_gk = _gate_bwd(dgk, dgdz, proj, w2p)
    dq_s, dk_s, dv_s, g_sinks = _swa_bwd(proj, dcat, swa_sinks[0], dep=sb_out[4])
    dproj = jnp.concatenate([dq_g, dk_g, dv_g, dg_g, dq_s, dk_s.astype(BF16), dv_s.astype(BF16), dlo], axis=-1)
    gw_in_p = _grad_w_in(xb, dproj)
    _, (fc_out,) = _copies_wait("chips_out_wait", [_plan_chips], sb_out, gw_in_p)
    finish("w_out", g_out, fs_out, fc_out)
    g_in = _from_padded_cols(gw_in_p).reshape(D_MODEL, N_DEV, D_IN // N_DEV).transpose(1, 0, 2)
    sa_in = _copies_start("sibling_in_start", [_plan_sibling], [g_in], [sib_land(g_in)], g_in)
    grad_x = _grad_x(dproj, w_in_p, dr1, dep=sa_in[4])
    (g_in,), (fs_in,) = _copies_wait("sibling_in_wait", [_plan_sibling], sa_in, grad_x)
    p_in = _pair_sum("pair_sum_w_in", g_in, fs_in, blocks)
    sb_in = _copies_start("chips_in_start", [_plan_chips], [p_in], [chip_land(g_in)], p_in)
    _, (fc_in,) = _copies_wait("chips_in_wait", [_plan_chips], sb_in, p_in)
    finish("w_in", g_in, fs_in, fc_in)

    pieces = [loss_part, g_b_gk, g_norm_w, g_sinks[:, :SWA_HEADS], g_ln1_g, g_ln1_b, g_ln2_g, g_ln2_b,
              gw2_p[:GATE_RANK]]
    pack = jnp.concatenate([_rows128(p) for p in pieces], axis=0)
    tot = _all_reduce_small(pack)
    sizes = [p.size for p in pieces]
    offs = [0]
    for p in pieces:
        offs.append(offs[-1] + _rows128(p).shape[0])
    unpack = lambda i, shape: tot[offs[i]:offs[i + 1]].reshape(-1)[:sizes[i]].reshape(shape)
    loss = tot[0, 0]
    small_names = ["b_gk", "gla_norm_w", "swa_sinks", "ln1_g", "ln1_b", "ln2_g", "ln2_b"]
    small_w = dict(b_gk=(b_gk, m_b_gk, v_b_gk), gla_norm_w=(gla_norm_w, m_gla_norm_w, v_gla_norm_w),
                   swa_sinks=(swa_sinks, m_swa_sinks, v_swa_sinks), ln1_g=(ln1_g, m_ln1_g, v_ln1_g),
                   ln1_b=(ln1_b, m_ln1_b, v_ln1_b), ln2_g=(ln2_g, m_ln2_g, v_ln2_g),
                   ln2_b=(ln2_b, m_ln2_b, v_ln2_b))
    small_g = {nm: unpack(1 + i, small_w[nm][0].shape) for i, nm in enumerate(small_names)}
    g_pack = jnp.concatenate([_rows128(small_g[nm]) for nm in small_names], axis=0)
    wmv = [jnp.concatenate([_rows128(small_w[nm][k]) for nm in small_names], axis=0) for k in range(3)]
    small_out = _adam_small("adam_replicated", g_pack, *wmv)
    srow = [0]
    for nm in small_names:
        srow.append(srow[-1] + _rows128(small_w[nm][0]).shape[0])
    small = {}
    for i, nm in enumerate(small_names):
        shape = small_w[nm][0].shape
        n = small_w[nm][0].size
        small[nm] = [small_g[nm]] + [o[srow[i]:srow[i + 1]].reshape(-1)[:n].reshape(shape) for o in small_out]

    gw2_full = unpack(8, (GATE_RANK, GLA_KW))
    gw2_loc = lax.dynamic_slice_in_dim(gw2_full, me * (GLA_KW // N_DEV), GLA_KW // N_DEV, axis=1)
    gk2_out = _adam_small("adam_w_gk2", gw2_loc, w_gk2[0], m_w_gk2[0], v_w_gk2[0])
    big["w_gk2"] = [gw2_loc[None]] + [o[None] for o in gk2_out]

    order = ["w_in", "w_gk2", "b_gk", "gla_norm_w", "swa_sinks", "w_out", "ln1_g", "ln1_b", "w_up", "w_down",
             "ln2_g", "ln2_b"]
    res = {**big, **small}
    outs = [loss, grad_x[None]]
    for k in range(4):
        outs += [res[nm][k] for nm in order]
    return tuple(outs)
```

```python
import functools

import jax
import jax.numpy as jnp
from jax import lax
from jax.experimental import pallas as pl
from jax.experimental.pallas import tpu as pltpu

F32 = jnp.float32
BF16 = jnp.bfloat16

N_DEV = 8
D_MODEL = 2048
D_FF = 8192
GLA_HEADS = 4
GLA_DK = 128
GLA_DV = 256
GLA_KW = 512
D_GLA = 1024
GATE_RANK = 16
GATE_TAU = 16.0
SWA_HEADS = 16
SWA_DH = 64
WINDOW = 128
D_SWA = 1024
D_IN = 4368
ALPHA = 2.0 ** 0.25
LN_EPS = 1e-5
RMS_EPS = 1e-5
ADAM_LR = 0.001
ADAM_B1 = 0.9
ADAM_B2 = 0.999
ADAM_EPS = 1e-08
ADAM_WD = 0.01
ADAM_STEP = 10

C_QG, C_KG, C_VG, C_GG, C_QS, C_KS, C_VS, C_LO = 0, 512, 1024, 2048, 3072, 4096, 4224, 4352
D_INP = 4480
LANES = 128
GC = 16
NEG = -1e30

NN = ((1,), (0,))
NT = ((1,), (1,))
TN = ((0,), (0,))

VMEM_LIMIT = 52 * 1024 * 1024
MESH = pl.DeviceIdType.MESH


def _dot(a, b, dn, precision=None):
    return lax.dot_general(a, b, (dn, ((), ())), preferred_element_type=F32, precision=precision)


def _bf16_round(v):
    return v.astype(BF16).astype(F32)


def _cparams(dims):
    return pltpu.CompilerParams(dimension_semantics=dims, vmem_limit_bytes=VMEM_LIMIT)


def _dep_operand(dep):
    if dep is None:
        return (), ()
    return (dep,), (pl.BlockSpec(dep.shape, lambda *_: (0,) * dep.ndim),)


def _to_padded_cols(w):
    pad = jnp.zeros(w.shape[:-1] + (D_INP - D_IN,), w.dtype)
    return jnp.concatenate([w[..., :3072], w[..., 3088:], w[..., 3072:3088], pad], axis=-1)


def _from_padded_cols(g):
    return jnp.concatenate([g[..., :3072], g[..., C_LO:C_LO + GATE_RANK], g[..., 3072:C_LO]], axis=-1)


def _matmul(name, a, b, dn, grid, a_spec, b_spec, out_shape, out_specs, acc_shape, *,
            extra=(), extra_specs=(), epilogue=None, dims=("parallel", "parallel", "arbitrary"), dep=None):
    nk = grid[2]
    n_extra = len(extra)
    deps, dep_specs = _dep_operand(dep)
    direct = epilogue is None and (nk == 1 or (not isinstance(out_shape, (list, tuple)) and out_shape.dtype == F32))

    def body(a_ref, b_ref, *rest):
        extra_refs = rest[:n_extra]
        out_refs = rest[n_extra + len(deps):] if direct else rest[n_extra + len(deps):-1]
        acc_ref = out_refs[0] if direct else rest[-1]
        part = _dot(a_ref[...].astype(BF16), b_ref[...].astype(BF16), dn)

        def finish():
            if epilogue is None:
                out_refs[0][...] = acc_ref[...].astype(out_refs[0].dtype)
            else:
                epilogue(acc_ref, extra_refs, out_refs)

        if direct and nk == 1:
            out_refs[0][...] = part.astype(out_refs[0].dtype)
        elif nk == 1:
            acc_ref[...] = part
            finish()
        else:
            k = pl.program_id(2)

            @pl.when(k == 0)
            def _():
                acc_ref[...] = part

            @pl.when(k > 0)
            def _():
                acc_ref[...] += part

            if not direct:
                @pl.when(k == nk - 1)
                def _():
                    finish()

    return pl.pallas_call(
        body, name=name, grid=grid,
        in_specs=[a_spec, b_spec, *extra_specs, *dep_specs],
        out_specs=out_specs, out_shape=out_shape,
        scratch_shapes=[] if direct else [pltpu.VMEM(acc_shape, F32)],
        compiler_params=_cparams(dims),
    )(a, b, *extra, *deps)


def _row_chunks(rows, step=128):
    step = min(step, rows)
    return [(r, step) for r in range(0, rows, step)]


def _ln_stats(r):
    mu = jnp.mean(r, axis=-1, keepdims=True)
    xc = r - mu
    var = jnp.mean(xc * xc, axis=-1, keepdims=True)
    rstd = lax.rsqrt(var + LN_EPS)
    return xc * rstd, rstd


def _ln_bwd(dy_g, xhat, rstd):
    m1 = jnp.mean(dy_g, axis=-1, keepdims=True)
    m2 = jnp.mean(dy_g * xhat, axis=-1, keepdims=True)
    return rstd * (dy_g - m1 - xhat * m2)


def _proj(xb, w_in_p, dep=None):
    t = xb.shape[0]
    tm, tn = min(1024, t), 640
    return _matmul(
        "proj", xb, w_in_p, NN, (t // tm, D_INP // tn, 1),
        pl.BlockSpec((tm, D_MODEL), lambda i, j, k: (i, 0)),
        pl.BlockSpec((D_MODEL, tn), lambda i, j, k: (0, j)),
        jax.ShapeDtypeStruct((t, D_INP), F32),
        pl.BlockSpec((tm, tn), lambda i, j, k: (i, j)),
        (tm, tn), dep=dep)


def _gate_fwd(proj, w2p, b_gk, tri):
    t = proj.shape[0]
    r = tri.shape[0]

    def body(lo_ref, w_ref, b_ref, tri_ref, bloc_ref, dgdz_ref):
        z = _dot(lo_ref[...].astype(BF16), w_ref[...], NN) + b_ref[...]
        e = jnp.exp(-jnp.abs(z))
        gk = (jnp.minimum(z, 0.0) - jnp.log1p(e)) * (1.0 / GATE_TAU)
        inv = 1.0 / (1.0 + e)
        dgdz_ref[...] = jnp.where(z >= 0.0, e * inv, inv) * (1.0 / GATE_TAU)
        bloc_ref[...] = _dot(tri_ref[...], gk, NN, precision=lax.Precision.HIGHEST)

    return pl.pallas_call(
        body, name="gate_fwd", grid=(t // r,),
        in_specs=[pl.BlockSpec((r, LANES), lambda i: (i, C_LO // LANES)),
                  pl.BlockSpec((LANES, GLA_KW), lambda i: (0, 0)),
                  pl.BlockSpec((1, GLA_KW), lambda i: (0, 0)),
                  pl.BlockSpec((r, r), lambda i: (0, 0))],
        out_specs=[pl.BlockSpec((r, GLA_KW), lambda i: (i, 0)),
                   pl.BlockSpec((r, GLA_KW), lambda i: (i, 0))],
        out_shape=[jax.ShapeDtypeStruct((t, GLA_KW), F32), jax.ShapeDtypeStruct((t, GLA_KW), F32)],
        compiler_params=_cparams(("parallel",)),
    )(proj, w2p, b_gk, tri)


def _gla_fwd(proj, bloc):
    t = proj.shape[0]
    r = min(256, t)
    ncb = r // GC
    scale = GLA_DK ** -0.5

    def body(q_ref, k_ref, v_ref, b_ref, o_ref, st_ref, s_scr):
        @pl.when(pl.program_id(1) == 0)
        def _():
            s_scr[...] = jnp.zeros_like(s_scr)

        rows = lax.broadcasted_iota(jnp.int32, (GC, 1), 0)

        def chunk(c, carry):
            r0 = pl.multiple_of(c * GC, GC)
            q = q_ref[pl.ds(r0, GC), :] * scale
            k = k_ref[pl.ds(r0, GC), :]
            v = v_ref[pl.ds(r0, GC), :]
            b = b_ref[pl.ds(r0, GC), :]
            st = s_scr[...]
            st_ref[c] = st
            bl = b[GC - 1:GC, :]
            kr = _bf16_round(k)
            vr = _bf16_round(v)
            o = _dot((q * jnp.exp(b)).astype(BF16), st.astype(BF16), NT)
            for j in range(GC):
                w = jnp.exp(jnp.where(rows >= j, b - b[j:j + 1, :], NEG))
                a = jnp.sum(_bf16_round(q * w) * kr[j:j + 1, :], axis=-1, keepdims=True)
                o = o + _bf16_round(a) * vr[j:j + 1, :]
            o_ref[pl.ds(r0, GC), :] = o
            kd = k * jnp.exp(bl - b)
            s_scr[...] = st * jnp.exp(bl) + _dot(v.astype(BF16), kd.astype(BF16), TN)
            return carry

        lax.fori_loop(0, ncb, chunk, 0, unroll=2)

    return pl.pallas_call(
        body, name="gla_fwd", grid=(GLA_HEADS, t // r),
        in_specs=[pl.BlockSpec((r, GLA_DK), lambda h, i: (i, C_QG // GLA_DK + h)),
                  pl.BlockSpec((r, GLA_DK), lambda h, i: (i, C_KG // GLA_DK + h)),
                  pl.BlockSpec((r, GLA_DV), lambda h, i: (i, C_VG // GLA_DV + h)),
                  pl.BlockSpec((r, GLA_DK), lambda h, i: (i, h))],
        out_specs=[pl.BlockSpec((r, GLA_DV), lambda h, i: (i, h)),
                   pl.BlockSpec((ncb, None, GLA_DV, GLA_DK), lambda h, i: (i, h, 0, 0))],
        out_shape=[jax.ShapeDtypeStruct((t, D_GLA), F32),
                   jax.ShapeDtypeStruct((t // GC, GLA_HEADS, GLA_DV, GLA_DK), F32)],
        scratch_shapes=[pltpu.VMEM((GLA_DV, GLA_DK), F32)],
        compiler_params=_cparams(("parallel", "arbitrary")),
    )(proj, proj, proj, bloc)


def _gla_norm_fwd(o_raw, proj, norm_w):
    t = o_raw.shape[0]
    r = min(512, t)

    def body(o_ref, g_ref, w_ref, out_ref):
        w = w_ref[...]
        for h in range(GLA_HEADS):
            sl = slice(h * GLA_DV, (h + 1) * GLA_DV)
            o = o_ref[:, sl]
            g = g_ref[:, sl]
            on = o * lax.rsqrt(jnp.mean(o * o, axis=-1, keepdims=True) + RMS_EPS)
            out_ref[:, sl] = (on * w * (g * jax.nn.sigmoid(g))).astype(BF16)

    return pl.pallas_call(
        body, name="gla_norm_fwd", grid=(t // r,),
        in_specs=[pl.BlockSpec((r, D_GLA), lambda i: (i, 0)),
                  pl.BlockSpec((r, D_GLA), lambda i: (i, C_GG // D_GLA)),
                  pl.BlockSpec((1, GLA_DV), lambda i: (0, 0))],
        out_specs=pl.BlockSpec((r, D_GLA), lambda i: (i, 0)),
        out_shape=jax.ShapeDtypeStruct((t, D_GLA), BF16),
        compiler_params=_cparams(("parallel",)),
    )(o_raw, proj, norm_w)


def _swa_masks(i):
    qi = lax.broadcasted_iota(jnp.int32, (WINDOW, 1), 0)
    kj = lax.broadcasted_iota(jnp.int32, (1, 2 * WINDOW), 1)
    valid = (kj > qi) & (kj <= qi + WINDOW) & ((i > 0) | (kj >= WINDOW))
    lo = lax.broadcasted_iota(jnp.int32, (1, LANES), 1) < SWA_DH
    return valid, lo


def _dup_half(x, lo, kv):
    xr = pltpu.roll(x, SWA_DH, axis=1)
    return jnp.where(lo, x, xr) if kv == 0 else jnp.where(lo, xr, x)


def _swa_probs(qm, kdup, valid, sink):
    s = _dot(qm, kdup, NT) * (SWA_DH ** -0.5)
    s = jnp.where(valid, s, NEG)
    m = jnp.maximum(jnp.max(s, axis=-1, keepdims=True), sink)
    p = jnp.exp(s - m)
    es = jnp.exp(sink - m)
    inv = 1.0 / (jnp.sum(p, axis=-1, keepdims=True) + es)
    return p * inv, es * inv


def _swa_fwd(proj, sinks, dep=None):
    t = proj.shape[0]

    deps, dep_specs = _dep_operand(dep)

    def body(sink_ref, q_ref, kp_ref, kc_ref, vp_ref, vc_ref, *rest):
        o_ref = rest[-1]
        i = pl.program_id(0)
        valid, lo = _swa_masks(i)
        kb = jnp.concatenate([kp_ref[...], kc_ref[...]], axis=0)
        vb = jnp.concatenate([vp_ref[...], vc_ref[...]], axis=0)
        for kv in range(2):
            kdup = _dup_half(kb, lo, kv).astype(BF16)
            vdup = _dup_half(vb, lo, kv).astype(BF16)
            for p in range(4):
                sl = slice(LANES * (4 * kv + p), LANES * (4 * kv + p + 1))
                qp = q_ref[:, sl]
                halves = []
                for e in range(2):
                    qm = jnp.where(lo if e == 0 else ~lo, qp, 0.0).astype(BF16)
                    pn, _ = _swa_probs(qm, kdup, valid, sink_ref[8 * kv + 2 * p + e])
                    halves.append(_dot(pn.astype(BF16), vdup, NN))
                o_ref[:, sl] = jnp.where(lo, halves[0], halves[1]).astype(BF16)

    kvspec = lambda col, prev: pl.BlockSpec(
        (WINDOW, LANES), (lambda i: (jnp.maximum(i - 1, 0), col)) if prev else (lambda i: (i, col)))
    return pl.pallas_call(
        body, name="swa_fwd", grid=(t // WINDOW,),
        in_specs=[pl.BlockSpec(memory_space=pltpu.SMEM),
                  pl.BlockSpec((WINDOW, D_SWA), lambda i: (i, C_QS // D_SWA)),
                  kvspec(C_KS // LANES, True), kvspec(C_KS // LANES, False),
                  kvspec(C_VS // LANES, True), kvspec(C_VS // LANES, False), *dep_specs],
        out_specs=pl.BlockSpec((WINDOW, D_SWA), lambda i: (i, 0)),
        out_shape=jax.ShapeDtypeStruct((t, D_SWA), BF16),
        compiler_params=_cparams(("parallel",)),
    )(sinks, proj, proj, proj, proj, proj, *deps)


def _mix_ln1(cat, w_out, x, ln_g, ln_b):
    t = cat.shape[0]
    tm, tk = min(512, t), 512

    def epilogue(acc_ref, extra, outs):
        x_ref, g_ref, b_ref = extra
        r1_ref, h1_ref, h1b_ref = outs
        for r0, n in _row_chunks(tm):
            rs = slice(r0, r0 + n)
            r1 = ALPHA * x_ref[rs, :] + acc_ref[rs, :]
            xhat, _ = _ln_stats(r1)
            h = xhat * g_ref[...] + b_ref[...]
            r1_ref[rs, :] = r1
            h1_ref[rs, :] = h
            h1b_ref[rs, :] = h.astype(BF16)

    row = pl.BlockSpec((tm, D_MODEL), lambda i, j, k: (i, 0))
    vec = pl.BlockSpec((1, D_MODEL), lambda i, j, k: (0, 0))
    return _matmul(
        "mix_ln1", cat, w_out, NN, (t // tm, 1, D_MODEL // tk),
        pl.BlockSpec((tm, tk), lambda i, j, k: (i, k)),
        pl.BlockSpec((tk, D_MODEL), lambda i, j, k: (k, 0)),
        [jax.ShapeDtypeStruct((t, D_MODEL), F32), jax.ShapeDtypeStruct((t, D_MODEL), F32),
         jax.ShapeDtypeStruct((t, D_MODEL), BF16)],
        [row, row, row], (tm, D_MODEL),
        extra=(x, ln_g, ln_b), extra_specs=(row, vec, vec), epilogue=epilogue)


def _mlp_up(h1b, w_up):
    t = h1b.shape[0]
    tm, tn = min(1024, t), 1024

    def epilogue(acc_ref, extra, outs):
        a_ref, hdn_ref = outs
        for r0, n in _row_chunks(tm, 256):
            rs = slice(r0, r0 + n)
            a = jnp.maximum(acc_ref[rs, :], 0.0)
            a_ref[rs, :] = a.astype(BF16)
            hdn_ref[rs, :] = (a * a).astype(BF16)

    out = pl.BlockSpec((tm, tn), lambda i, j, k: (i, j))
    return _matmul(
        "mlp_up", h1b, w_up, NN, (t // tm, D_FF // tn, 1),
        pl.BlockSpec((tm, D_MODEL), lambda i, j, k: (i, 0)),
        pl.BlockSpec((None, D_MODEL, tn), lambda i, j, k: (j, 0, 0)),
        [jax.ShapeDtypeStruct((t, D_FF), BF16), jax.ShapeDtypeStruct((t, D_FF), BF16)],
        [out, out], (tm, tn), epilogue=epilogue)


def _mlp_down_loss(hdn, w_down, h1, target, ln_g, ln_b):
    t = hdn.shape[0]
    tm, tn, tk = min(1024, t), 1024, 2048
    ff = _matmul(
        "mlp_down", hdn, w_down, NN, (t // tm, D_MODEL // tn, D_FF // tk),
        pl.BlockSpec((tm, tk), lambda i, j, k: (i, k)),
        pl.BlockSpec((tk, tn), lambda i, j, k: (k, j)),
        jax.ShapeDtypeStruct((t, D_MODEL), F32),
        pl.BlockSpec((tm, tn), lambda i, j, k: (i, j)), (tm, tn))
    r = min(256, t)

    def body(ff_ref, h1_ref, t_ref, g_ref, b_ref, dr2_ref, dr2b_ref, gg_ref, gb_ref, loss_ref):
        @pl.when(pl.program_id(0) == 0)
        def _():
            gg_ref[...] = jnp.zeros_like(gg_ref)
            gb_ref[...] = jnp.zeros_like(gb_ref)
            loss_ref[...] = jnp.zeros_like(loss_ref)

        for r0, n in _row_chunks(r, 64):
            rs = slice(r0, r0 + n)
            xhat, rstd = _ln_stats(ALPHA * h1_ref[rs, :] + ff_ref[rs, :])
            err = xhat * g_ref[...] + b_ref[...] - t_ref[rs, :]
            loss_ref[...] += 0.5 * jnp.sum(jnp.mean(err * err, axis=-1, keepdims=True))
            dy = err * (1.0 / D_MODEL)
            gg_ref[...] += jnp.sum(dy * xhat, axis=0, keepdims=True)
            gb_ref[...] += jnp.sum(dy, axis=0, keepdims=True)
            dr2 = _ln_bwd(dy * g_ref[...], xhat, rstd)
            dr2_ref[rs, :] = dr2
            dr2b_ref[rs, :] = dr2.astype(BF16)

    row = pl.BlockSpec((r, D_MODEL), lambda i: (i, 0))
    vec = pl.BlockSpec((1, D_MODEL), lambda i: (0, 0))
    return pl.pallas_call(
        body, name="ln2_loss", grid=(t // r,),
        in_specs=[row, row, row, vec, vec],
        out_specs=[row, row, vec, vec, pl.BlockSpec((1, LANES), lambda i: (0, 0))],
        out_shape=[jax.ShapeDtypeStruct((t, D_MODEL), F32), jax.ShapeDtypeStruct((t, D_MODEL), BF16),
                   jax.ShapeDtypeStruct((1, D_MODEL), F32), jax.ShapeDtypeStruct((1, D_MODEL), F32),
                   jax.ShapeDtypeStruct((1, LANES), F32)],
        compiler_params=_cparams(("arbitrary",)),
    )(ff, h1, target, ln_g, ln_b)


def _mlp_down_bwd(dr2b, w_down, a_act):
    t = dr2b.shape[0]
    tm, tn = min(1024, t), 1024

    def epilogue(acc_ref, extra, outs):
        (a_ref,) = extra
        for r0, n in _row_chunks(tm, 256):
            rs = slice(r0, r0 + n)
            outs[0][rs, :] = (acc_ref[rs, :] * (2.0 * a_ref[rs, :].astype(F32))).astype(BF16)

    blk = pl.BlockSpec((tm, tn), lambda i, j, k: (i, j))
    return _matmul(
        "mlp_down_bwd", dr2b, w_down, NT, (t // tm, D_FF // tn, 1),
        pl.BlockSpec((tm, D_MODEL), lambda i, j, k: (i, 0)),
        pl.BlockSpec((tn, D_MODEL), lambda i, j, k: (j, 0)),
        jax.ShapeDtypeStruct((t, D_FF), BF16), blk, (tm, tn),
        extra=(a_act,), extra_specs=(blk,), epilogue=epilogue)


def _grad_w_down(hdn, dr2b):
    t = hdn.shape[0]
    tm, tn = 1024, 1024
    return _matmul(
        "grad_w_down", hdn, dr2b, TN, (D_MODEL // tn, D_FF // tm, 1),
        pl.BlockSpec((t, tm), lambda j, i, k: (0, i)),
        pl.BlockSpec((t, tn), lambda j, i, k: (0, j)),
        jax.ShapeDtypeStruct((D_FF, D_MODEL), F32),
        pl.BlockSpec((tm, tn), lambda j, i, k: (i, j)), (tm, tn))


def _grad_w_up(h1b, du, dep=None):
    t = h1b.shape[0]
    tm, tn = 1024, 1024
    return _matmul(
        "grad_w_up", h1b, du, TN, (N_DEV, D_MODEL // tm, 1),
        pl.BlockSpec((t, tm), lambda i, j, k: (0, j)),
        pl.BlockSpec((t, tn), lambda i, j, k: (0, i)),
        jax.ShapeDtypeStruct((N_DEV, D_MODEL, D_FF // N_DEV), F32),
        pl.BlockSpec((None, tm, tn), lambda i, j, k: (i, j, 0)), (tm, tn), dep=dep)


def _mlp_up_bwd_ln1(du, w_up, dr2, r1, ln_g, dep=None):
    t = du.shape[0]
    tm, tn, tk = min(1024, t), 1024, D_FF // N_DEV
    dff = _matmul(
        "mlp_up_bwd", du, w_up, NT, (t // tm, D_MODEL // tn, N_DEV),
        pl.BlockSpec((tm, tk), lambda i, j, k: (i, k)),
        pl.BlockSpec((None, tn, tk), lambda i, j, k: (k, j, 0)),
        jax.ShapeDtypeStruct((t, D_MODEL), F32),
        pl.BlockSpec((tm, tn), lambda i, j, k: (i, j)), (tm, tn), dep=dep)
    r = min(256, t)

    def body(acc_ref, dr2_ref, r1_ref, g_ref, dr1_ref, dr1b_ref, gg_ref, gb_ref):
        @pl.when(pl.program_id(0) == 0)
        def _():
            gg_ref[...] = jnp.zeros_like(gg_ref)
            gb_ref[...] = jnp.zeros_like(gb_ref)

        for r0, n in _row_chunks(r, 64):
            rs = slice(r0, r0 + n)
            dh1 = ALPHA * dr2_ref[rs, :] + acc_ref[rs, :]
            xhat, rstd = _ln_stats(r1_ref[rs, :])
            gg_ref[...] += jnp.sum(dh1 * xhat, axis=0, keepdims=True)
            gb_ref[...] += jnp.sum(dh1, axis=0, keepdims=True)
            dr1 = _ln_bwd(dh1 * g_ref[...], xhat, rstd)
            dr1_ref[rs, :] = dr1
            dr1b_ref[rs, :] = dr1.astype(BF16)

    row = pl.BlockSpec((r, D_MODEL), lambda i: (i, 0))
    vec = pl.BlockSpec((1, D_MODEL), lambda i: (0, 0))
    return pl.pallas_call(
        body, name="ln1_bwd", grid=(t // r,),
        in_specs=[row, row, row, vec],
        out_specs=[row, row, vec, vec],
        out_shape=[jax.ShapeDtypeStruct((t, D_MODEL), F32), jax.ShapeDtypeStruct((t, D_MODEL), BF16),
                   jax.ShapeDtypeStruct((1, D_MODEL), F32), jax.ShapeDtypeStruct((1, D_MODEL), F32)],
        compiler_params=_cparams(("arbitrary",)),
    )(dff, dr2, r1, ln_g)


def _dcat(dr1b, w_out):
    t = dr1b.shape[0]
    tm, tn = min(1024, t), 1024
    return _matmul(
        "dcat", dr1b, w_out, NT, (t // tm, D_MODEL // tn, 1),
        pl.BlockSpec((tm, D_MODEL), lambda i, j, k: (i, 0)),
        pl.BlockSpec((tn, D_MODEL), lambda i, j, k: (j, 0)),
        jax.ShapeDtypeStruct((t, D_MODEL), F32),
        pl.BlockSpec((tm, tn), lambda i, j, k: (i, j)), (tm, tn))


def _grad_w_out(cat, dr1b, dep=None):
    t = cat.shape[0]
    tm, tn = 1024, 1024
    return _matmul(
        "grad_w_out", cat, dr1b, TN, (D_MODEL // tm, D_MODEL // tn, 1),
        pl.BlockSpec((t, tm), lambda i, j, k: (0, i)),
        pl.BlockSpec((t, tn), lambda i, j, k: (0, j)),
        jax.ShapeDtypeStruct((D_MODEL, D_MODEL), F32),
        pl.BlockSpec((tm, tn), lambda i, j, k: (i, j)), (tm, tn), dep=dep)


def _gla_norm_bwd(dcat, o_raw, proj, norm_w, dep=None):
    t = o_raw.shape[0]
    r = min(512, t)

    deps, dep_specs = _dep_operand(dep)

    def body(d_ref, o_ref, g_ref, w_ref, *rest):
        do_ref, dg_ref, dw_ref = rest[len(deps):]

        @pl.when(pl.program_id(0) == 0)
        def _():
            dw_ref[...] = jnp.zeros_like(dw_ref)

        w = w_ref[...]
        dw = jnp.zeros((1, GLA_DV), F32)
        for h in range(GLA_HEADS):
            sl = slice(h * GLA_DV, (h + 1) * GLA_DV)
            o = o_ref[:, sl]
            g = g_ref[:, sl]
            d = d_ref[:, sl]
            rr = lax.rsqrt(jnp.mean(o * o, axis=-1, keepdims=True) + RMS_EPS)
            on = o * rr
            sg = jax.nn.sigmoid(g)
            sil = g * sg
            dg_ref[:, sl] = (d * on * w * (sg * (1.0 + g * (1.0 - sg)))).astype(BF16)
            dw = dw + jnp.sum(d * on * sil, axis=0, keepdims=True)
            don = d * w * sil
            do_ref[:, sl] = rr * (don - on * jnp.mean(don * on, axis=-1, keepdims=True))
        dw_ref[...] += dw

    return pl.pallas_call(
        body, name="gla_norm_bwd", grid=(t // r,),
        in_specs=[pl.BlockSpec((r, D_GLA), lambda i: (i, 0)),
                  pl.BlockSpec((r, D_GLA), lambda i: (i, 0)),
                  pl.BlockSpec((r, D_GLA), lambda i: (i, C_GG // D_GLA)),
                  pl.BlockSpec((1, GLA_DV), lambda i: (0, 0)), *dep_specs],
        out_specs=[pl.BlockSpec((r, D_GLA), lambda i: (i, 0)),
                   pl.BlockSpec((r, D_GLA), lambda i: (i, 0)),
                   pl.BlockSpec((1, GLA_DV), lambda i: (0, 0))],
        out_shape=[jax.ShapeDtypeStruct((t, D_GLA), F32), jax.ShapeDtypeStruct((t, D_GLA), BF16),
                   jax.ShapeDtypeStruct((1, GLA_DV), F32)],
        compiler_params=_cparams(("arbitrary",)),
    )(dcat, o_raw, proj, norm_w, *deps)


def _gla_bwd(proj, bloc, do_raw, states, triu):
    t = proj.shape[0]
    r = min(256, t)
    ncb = r // GC
    nb = t // r
    scale = GLA_DK ** -0.5

    def body(q_ref, k_ref, v_ref, b_ref, do_ref, st_ref, u_ref, dq_ref, dk_ref, dv_ref, dg_ref, ds_scr):
        @pl.when(pl.program_id(1) == 0)
        def _():
            ds_scr[...] = jnp.zeros_like(ds_scr)

        rows = lax.broadcasted_iota(jnp.int32, (GC, 1), 0)

        def chunk(cc, carry):
            c = ncb - 1 - cc
            r0 = pl.multiple_of(c * GC, GC)
            q = q_ref[pl.ds(r0, GC), :] * scale
            k = k_ref[pl.ds(r0, GC), :]
            v = v_ref[pl.ds(r0, GC), :]
            b = b_ref[pl.ds(r0, GC), :]
            do = do_ref[pl.ds(r0, GC), :]
            st = st_ref[c]
            dsn = ds_scr[...]
            bl = b[GC - 1:GC, :]
            eb = jnp.exp(b)
            ekl = jnp.exp(bl - b)
            ebl = jnp.exp(bl)
            qh = q * eb
            kd = k * ekl
            dob = do.astype(BF16)
            dsb = dsn.astype(BF16)
            dqh = _dot(dob, st.astype(BF16), NN)
            dkd = _dot(v.astype(BF16), dsb, NN)
            dv = _dot(kd.astype(BF16), dsb, NT)
            dq_i = jnp.zeros((GC, GLA_DK), F32)
            dk_i = jnp.zeros((GC, GLA_DK), F32)
            dk_x = jnp.zeros((GC, GLA_DK), F32)
            kr = _bf16_round(k)
            vr = _bf16_round(v)
            dor = _bf16_round(do)
            for j in range(GC):
                w = jnp.exp(jnp.where(rows >= j, b - b[j:j + 1, :], NEG))
                qw = q * w
                qwr = _bf16_round(qw)
                a = _bf16_round(jnp.sum(qwr * kr[j:j + 1, :], axis=-1, keepdims=True))
                da = jnp.sum(dor * vr[j:j + 1, :], axis=-1, keepdims=True)
                dq_i = dq_i + da * (w * k[j:j + 1, :])
                dk_x = jnp.where(rows == j, jnp.sum(da * qw, axis=0, keepdims=True), dk_x)
                dk_i = jnp.where(rows == j, jnp.sum(_bf16_round(da) * qwr, axis=0, keepdims=True), dk_i)
                dv = dv + jnp.where(rows == j, jnp.sum(a * dor, axis=0, keepdims=True), 0.0)
            dqs = dqh * eb + dq_i
            dk = dkd * ekl + dk_i
            db_last = jnp.sum(dkd * kd, axis=0, keepdims=True) + ebl * jnp.sum(dsn * st, axis=0, keepdims=True)
            db = q * dqs - k * (dkd * ekl + dk_x) + jnp.where(rows == GC - 1, db_last, 0.0)
            dq_ref[pl.ds(r0, GC), :] = (dqs * scale).astype(BF16)
            dk_ref[pl.ds(r0, GC), :] = dk.astype(BF16)
            dv_ref[pl.ds(r0, GC), :] = dv.astype(BF16)
            dg_ref[pl.ds(r0, GC), :] = _dot(u_ref[...], db, NN, precision=lax.Precision.HIGHEST)
            ds_scr[...] = dsn * ebl + _dot(dob, qh.astype(BF16), TN)
            return carry

        lax.fori_loop(0, ncb, chunk, 0)

    rev = lambda i: nb - 1 - i
    return pl.pallas_call(
        body, name="gla_bwd", grid=(GLA_HEADS, nb),
        in_specs=[pl.BlockSpec((r, GLA_DK), lambda h, i: (rev(i), C_QG // GLA_DK + h)),
                  pl.BlockSpec((r, GLA_DK), lambda h, i: (rev(i), C_KG // GLA_DK + h)),
                  pl.BlockSpec((r, GLA_DV), lambda h, i: (rev(i), C_VG // GLA_DV + h)),
                  pl.BlockSpec((r, GLA_DK), lambda h, i: (rev(i), h)),
                  pl.BlockSpec((r, GLA_DV), lambda h, i: (rev(i), h)),
                  pl.BlockSpec((ncb, None, GLA_DV, GLA_DK), lambda h, i: (rev(i), h, 0, 0)),
                  pl.BlockSpec((GC, GC), lambda h, i: (0, 0))],
        out_specs=[pl.BlockSpec((r, GLA_DK), lambda h, i: (rev(i), h)),
                   pl.BlockSpec((r, GLA_DK), lambda h, i: (rev(i), h)),
                   pl.BlockSpec((r, GLA_DV), lambda h, i: (rev(i), h)),
                   pl.BlockSpec((r, GLA_DK), lambda h, i: (rev(i), h))],
        out_shape=[jax.ShapeDtypeStruct((t, GLA_KW), BF16), jax.ShapeDtypeStruct((t, GLA_KW), BF16),
                   jax.ShapeDtypeStruct((t, D_GLA), BF16), jax.ShapeDtypeStruct((t, GLA_KW), F32)],
        scratch_shapes=[pltpu.VMEM((GLA_DV, GLA_DK), F32)],
        compiler_params=_cparams(("parallel", "arbitrary")),
    )(proj, proj, proj, bloc, do_raw, states, triu)


def _gate_bwd(dg, dgdz, proj, w2p):
    t = dg.shape[0]
    r = min(512, t)

    def body(dg_ref, s_ref, lo_ref, w_ref, dlo_ref, gw_ref, gb_ref):
        @pl.when(pl.program_id(0) == 0)
        def _():
            gw_ref[...] = jnp.zeros_like(gw_ref)
            gb_ref[...] = jnp.zeros_like(gb_ref)

        dz = dg_ref[...] * s_ref[...]
        dzb = dz.astype(BF16)
        gb_ref[...] += jnp.sum(dz, axis=0, keepdims=True)
        gw_ref[...] += _dot(lo_ref[...].astype(BF16), dzb, TN)
        dlo_ref[...] = _dot(dzb, w_ref[...], NT).astype(BF16)

    return pl.pallas_call(
        body, name="gate_bwd", grid=(t // r,),
        in_specs=[pl.BlockSpec((r, GLA_KW), lambda i: (i, 0)),
                  pl.BlockSpec((r, GLA_KW), lambda i: (i, 0)),
                  pl.BlockSpec((r, LANES), lambda i: (i, C_LO // LANES)),
                  pl.BlockSpec((LANES, GLA_KW), lambda i: (0, 0))],
        out_specs=[pl.BlockSpec((r, LANES), lambda i: (i, 0)),
                   pl.BlockSpec((LANES, GLA_KW), lambda i: (0, 0)),
                   pl.BlockSpec((1, GLA_KW), lambda i: (0, 0))],
        out_shape=[jax.ShapeDtypeStruct((t, LANES), BF16), jax.ShapeDtypeStruct((LANES, GLA_KW), F32),
                   jax.ShapeDtypeStruct((1, GLA_KW), F32)],
        compiler_params=_cparams(("arbitrary",)),
    )(dg, dgdz, proj, w2p)


def _swa_bwd(proj, dcat, sinks, dep=None):
    t = proj.shape[0]

    deps, dep_specs = _dep_operand(dep)

    def body(sink_ref, q_ref, kp_ref, kc_ref, vp_ref, vc_ref, d_ref, *rest):
        dq_ref, dk_ref, dv_ref, dsink_ref = rest[len(deps):]
        i = pl.program_id(0)

        @pl.when(i == 0)
        def _():
            dk_ref[...] = jnp.zeros_like(dk_ref)
            dv_ref[...] = jnp.zeros_like(dv_ref)
            dsink_ref[...] = jnp.zeros_like(dsink_ref)

        valid, lo = _swa_masks(i)
        lane = lax.broadcasted_iota(jnp.int32, (1, LANES), 1)
        kb = jnp.concatenate([kp_ref[...], kc_ref[...]], axis=0)
        vb = jnp.concatenate([vp_ref[...], vc_ref[...]], axis=0)
        dsink = jnp.zeros((1, LANES), F32)
        folded_k, folded_v = [], []
        for kv in range(2):
            kdup = _dup_half(kb, lo, kv).astype(BF16)
            vdup = _dup_half(vb, lo, kv).astype(BF16)
            dkd = jnp.zeros((2 * WINDOW, LANES), F32)
            dvd = jnp.zeros((2 * WINDOW, LANES), F32)
            for p in range(4):
                sl = slice(LANES * (4 * kv + p), LANES * (4 * kv + p + 1))
                qp = q_ref[:, sl]
                dp_ = d_ref[:, sl]
                halves = []
                for e in range(2):
                    h = 8 * kv + 2 * p + e
                    half = lo if e == 0 else ~lo
                    qm = jnp.where(half, qp, 0.0).astype(BF16)
                    dom = jnp.where(half, dp_, 0.0).astype(BF16)
                    pn, psink = _swa_probs(qm, kdup, valid, sink_ref[h])
                    dpr = _dot(dom, vdup, NT)
                    drow = jnp.sum(dpr * pn, axis=-1, keepdims=True)
                    dsink = dsink + jnp.where(lane == h, -jnp.sum(psink * drow), 0.0)
                    dsb = (pn * (dpr - drow) * (SWA_DH ** -0.5)).astype(BF16)
                    halves.append(_dot(dsb, kdup, NN))
                    dkd = dkd + _dot(dsb, qm, TN)
                    dvd = dvd + _dot(pn.astype(BF16), dom, TN)
                dq_ref[:, sl] = jnp.where(lo, halves[0], halves[1]).astype(BF16)
            folded_k.append(dkd + pltpu.roll(dkd, SWA_DH, axis=1))
            folded_v.append(dvd + pltpu.roll(dvd, SWA_DH, axis=1))
        dkb = jnp.where(lo, folded_k[0], folded_k[1])
        dvb = jnp.where(lo, folded_v[0], folded_v[1])
        dsink_ref[...] += dsink
        cur = pl.ds(pl.multiple_of(i * WINDOW, WINDOW), WINDOW)
        dk_ref[cur, :] += dkb[WINDOW:, :]
        dv_ref[cur, :] += dvb[WINDOW:, :]

        @pl.when(i > 0)
        def _():
            prev = pl.ds(pl.multiple_of((i - 1) * WINDOW, WINDOW), WINDOW)
            dk_ref[prev, :] += dkb[:WINDOW, :]
            dv_ref[prev, :] += dvb[:WINDOW, :]

    kvspec = lambda col, prev: pl.BlockSpec(
        (WINDOW, LANES), (lambda i: (jnp.maximum(i - 1, 0), col)) if prev else (lambda i: (i, col)))
    full = pl.BlockSpec((t, LANES), lambda i: (0, 0))
    return pl.pallas_call(
        body, name="swa_bwd", grid=(t // WINDOW,),
        in_specs=[pl.BlockSpec(memory_space=pltpu.SMEM),
                  pl.BlockSpec((WINDOW, D_SWA), lambda i: (i, C_QS // D_SWA)),
                  kvspec(C_KS // LANES, True), kvspec(C_KS // LANES, False),
                  kvspec(C_VS // LANES, True), kvspec(C_VS // LANES, False),
                  pl.BlockSpec((WINDOW, D_SWA), lambda i: (i, 1)), *dep_specs],
        out_specs=[pl.BlockSpec((WINDOW, D_SWA), lambda i: (i, 0)), full, full,
                   pl.BlockSpec((1, LANES), lambda i: (0, 0))],
        out_shape=[jax.ShapeDtypeStruct((t, D_SWA), BF16), jax.ShapeDtypeStruct((t, LANES), F32),
                   jax.ShapeDtypeStruct((t, LANES), F32), jax.ShapeDtypeStruct((1, LANES), F32)],
        compiler_params=_cparams(("arbitrary",)),
    )(sinks, proj, proj, proj, proj, proj, dcat, *deps)


def _grad_w_in(xb, dproj):
    t = xb.shape[0]
    tm, tn = 1024, 640
    return _matmul(
        "grad_w_in", xb, dproj, TN, (D_MODEL // tm, D_INP // tn, 1),
        pl.BlockSpec((t, tm), lambda i, j, k: (0, i)),
        pl.BlockSpec((t, tn), lambda i, j, k: (0, j)),
        jax.ShapeDtypeStruct((D_MODEL, D_INP), F32),
        pl.BlockSpec((tm, tn), lambda i, j, k: (i, j)), (tm, tn))


def _grad_x(dproj, w_in_p, dr1, dep=None):
    t = dproj.shape[0]
    tm, tn = min(512, t), 1024

    def epilogue(acc_ref, extra, outs):
        for r0, n in _row_chunks(tm):
            rs = slice(r0, r0 + n)
            outs[0][rs, :] = ALPHA * extra[0][rs, :] + acc_ref[rs, :]

    blk = pl.BlockSpec((tm, tn), lambda j, i, k: (i, j))
    return _matmul(
        "grad_x", dproj, w_in_p, NT, (D_MODEL // tn, t // tm, 1),
        pl.BlockSpec((tm, D_INP), lambda j, i, k: (i, 0)),
        pl.BlockSpec((tn, D_INP), lambda j, i, k: (j, 0)),
        jax.ShapeDtypeStruct((t, D_MODEL), F32), blk, (tm, tn),
        extra=(dr1,), extra_specs=(blk,), epilogue=epilogue, dep=dep)


def _place():
    x, y, c = lax.axis_index("x"), lax.axis_index("y"), lax.axis_index("c")
    chips = [(1 - x, y), (x, 1 - y), (1 - x, 1 - y)]
    return x, y, c, chips


def _all_gather(shards):
    n = len(shards)
    hbm = pl.BlockSpec(memory_space=pl.ANY)

    def body(*refs):
        ins, outs = refs[:n], refs[n:2 * n]
        send, recv, loc = refs[2 * n:]
        x, y, c, chips = _place()
        me, sib = (x, y, c), (x, y, 1 - c)

        def slot(t, px, py, pc):
            return outs[t].at[4 * px + 2 * py + pc]

        def copy(t, k, block, to, src=None):
            return pltpu.make_async_remote_copy(
                src_ref=slot(t, *block) if src is None else src, dst_ref=slot(t, *block),
                send_sem=send.at[7 * t + k], recv_sem=recv.at[7 * t + k], device_id=to, device_id_type=MESH)

        mine = [pltpu.make_async_copy(ins[t], slot(t, *me), loc.at[t]) for t in range(n)]
        for cp in mine:
            cp.start()
        sent = []
        for t in range(n):
            sent.append(copy(t, 0, me, sib, src=ins[t]))
            sent += [copy(t, 1 + j, me, (*chip, c), src=ins[t]) for j, chip in enumerate(chips)]
        for cp in sent:
            cp.start()
        for t in range(n):
            for j, chip in enumerate(chips):
                copy(t, 1 + j, (*chip, c), me).wait_recv()
                fwd = copy(t, 4 + j, (*chip, c), sib)
                fwd.start()
                sent.append(fwd)
        for t in range(n):
            copy(t, 0, sib, me).wait_recv()
            for j, chip in enumerate(chips):
                copy(t, 4 + j, (*chip, 1 - c), me).wait_recv()
        for cp in sent:
            cp.wait_send()
        for cp in mine:
            cp.wait()

    return pl.pallas_call(
        body, name="all_gather_weights",
        in_specs=[hbm] * n, out_specs=[hbm] * n,
        out_shape=[jax.ShapeDtypeStruct((N_DEV,) + s.shape, s.dtype) for s in shards],
        scratch_shapes=[pltpu.SemaphoreType.DMA((7 * n,)), pltpu.SemaphoreType.DMA((7 * n,)),
                        pltpu.SemaphoreType.DMA((n,))],
    )(*shards)


def _plan_gather_out(src, land, x, y, c, chips):
    me = 4 * x + 2 * y + c
    return [(src, land.at[me], to) for to in [(x, y, 1 - c)] + [(px, py, c) for px, py in chips]]


def _plan_gather_forward(src, land, x, y, c, chips):
    return [(land.at[4 * px + 2 * py + c], land.at[4 * px + 2 * py + c], (x, y, 1 - c)) for px, py in chips]


def _plan_sibling(src, land, x, y, c, chips):
    return [(src.at[2 * q + (1 - c)], land.at[q], (x, y, 1 - c)) for q in range(4)]


def _plan_chips(src, land, x, y, c, chips):
    return [(src.at[j], land.at[j], (px, py, c)) for j, (px, py) in enumerate(chips)]


_PLAN_COPIES = {_plan_gather_out: 4, _plan_gather_forward: 3, _plan_sibling: 4, _plan_chips: 3}
_HBM = pl.BlockSpec(memory_space=pltpu.HBM)
_SEM = pl.BlockSpec(memory_space=pltpu.SEMAPHORE)
_EFFECT = pltpu.SideEffectType.DATAFLOW_SIDE_EFFECTING


def _hbm(a):
    return pltpu.with_memory_space_constraint(a, pltpu.HBM)


def _plan_descriptors(plans, srcs, lands, send, recv):
    x, y, c, chips = _place()
    cps = []
    for plan, src, land in zip(plans, srcs, lands):
        for s_ref, d_ref, to in plan(src, land, x, y, c, chips):
            k = len(cps)
            cps.append(pltpu.make_async_remote_copy(src_ref=s_ref, dst_ref=d_ref, send_sem=send.at[k],
                                                    recv_sem=recv.at[k], device_id=to, device_id_type=MESH))
    return cps


def _copies_start(name, plans, srcs, lands, after):
    has_src = [s is not None for s in srcs]
    arrays = [s for s in srcs if s is not None] + list(lands)
    n_src = sum(has_src)
    n_cp = sum(_PLAN_COPIES[p] for p in plans)

    def body(*refs):
        ins = refs[:len(arrays)]
        send, recv = refs[len(arrays) + 1], refs[len(arrays) + 2]
        token = refs[-1]
        it = iter(ins[:n_src])
        src_refs = [next(it) if h else None for h in has_src]
        for cp in _plan_descriptors(plans, src_refs, ins[n_src:], send, recv):
            cp.start()
        token[...] = jnp.zeros_like(token)

    outs = pl.pallas_call(
        body, name=name,
        in_specs=[_HBM] * len(arrays) + [pl.BlockSpec(memory_space=pl.ANY)],
        out_specs=(_SEM, _SEM, *[_HBM] * len(arrays), pl.BlockSpec(memory_space=pltpu.VMEM)),
        out_shape=(pltpu.SemaphoreType.DMA((n_cp,)), pltpu.SemaphoreType.DMA((n_cp,)),
                   *[pltpu.HBM(a.shape, a.dtype) for a in arrays], jax.ShapeDtypeStruct((8, LANES), F32)),
        input_output_aliases={i: 2 + i for i in range(len(arrays))},
        compiler_params=pltpu.CompilerParams(has_side_effects=_EFFECT),
    )(*[_hbm(a) for a in arrays], after)
    send, recv = outs[0], outs[1]
    thru = list(outs[2:-1])
    it = iter(thru[:n_src])
    return send, recv, [next(it) if h else None for h in has_src], thru[n_src:], outs[-1]


def _copies_wait(name, plans, started, after):
    send, recv, srcs, lands, _ = started
    has_src = [s is not None for s in srcs]
    arrays = [s for s in srcs if s is not None] + list(lands)
    n_src = sum(has_src)

    def body(*refs):
        ins = refs[:len(arrays)]
        send_ref, recv_ref = refs[len(arrays)], refs[len(arrays) + 1]
        it = iter(ins[:n_src])
        src_refs = [next(it) if h else None for h in has_src]
        for cp in _plan_descriptors(plans, src_refs, ins[n_src:], send_ref, recv_ref):
            cp.wait_send()
            cp.wait_recv()

    outs = pl.pallas_call(
        body, name=name,
        in_specs=[_HBM] * len(arrays) + [_SEM, _SEM, pl.BlockSpec(memory_space=pl.ANY)],
        out_specs=tuple([_HBM] * len(arrays)),
        out_shape=tuple(pltpu.HBM(a.shape, a.dtype) for a in arrays),
        input_output_aliases={i: i for i in range(len(arrays))},
        compiler_params=pltpu.CompilerParams(has_side_effects=_EFFECT),
    )(*arrays, send, recv, after)
    return list(outs[:n_src]), list(outs[n_src:])


def _pair_sum(name, grad, from_sibling, blocks):
    _, rows, cols = grad.shape
    tr = 256

    def body(blk_ref, g_ref, s_ref, o_ref):
        o_ref[...] = (g_ref[...] + s_ref[...]).astype(BF16)

    return pl.pallas_call(
        body, name=name,
        grid_spec=pltpu.PrefetchScalarGridSpec(
            num_scalar_prefetch=1, grid=(3, rows // tr),
            in_specs=[pl.BlockSpec((None, tr, cols), lambda j, r, br: (br[j], r, 0)),
                      pl.BlockSpec((None, tr, cols), lambda j, r, br: (br[3 + j], r, 0))],
            out_specs=pl.BlockSpec((None, tr, cols), lambda j, r, br: (j, r, 0))),
        out_shape=jax.ShapeDtypeStruct((3, rows, cols), BF16),
        compiler_params=_cparams(("parallel", "parallel")),
    )(blocks, grad, from_sibling)


def _adam_math(g, w, m, v):
    m2 = ADAM_B1 * m + (1.0 - ADAM_B1) * g
    v2 = ADAM_B2 * v + (1.0 - ADAM_B2) * (g * g)
    m_hat = m2 / (1.0 - ADAM_B1 ** ADAM_STEP)
    v_hat = v2 / (1.0 - ADAM_B2 ** ADAM_STEP)
    delta = -ADAM_LR * (m_hat / (jnp.sqrt(v_hat) + ADAM_EPS) + ADAM_WD * w)
    return delta, m2, v2


def _sum_adam(name, grad, from_sibling, from_chips, own, w, m, v):
    rows, cols = w.shape
    tr = 128

    def body(own_ref, p_ref, s_ref, r_ref, w_ref, m_ref, v_ref, g_out, d_out, m_out, v_out):
        g = p_ref[...] + s_ref[...]
        for j in range(3):
            g = g + r_ref[j].astype(F32)
        d, m2, v2 = _adam_math(g, w_ref[...], m_ref[...], v_ref[...])
        g_out[...] = g
        d_out[...] = d
        m_out[...] = m2
        v_out[...] = v2

    blk = pl.BlockSpec((tr, cols), lambda r, cr: (r, 0))
    shp = jax.ShapeDtypeStruct((rows, cols), F32)
    return pl.pallas_call(
        body, name=name,
        grid_spec=pltpu.PrefetchScalarGridSpec(
            num_scalar_prefetch=1, grid=(rows // tr,),
            in_specs=[pl.BlockSpec((None, tr, cols), lambda r, cr: (cr[0], r, 0)),
                      pl.BlockSpec((None, tr, cols), lambda r, cr: (cr[1], r, 0)),
                      pl.BlockSpec((3, tr, cols), lambda r, cr: (0, r, 0)),
                      blk, blk, blk],
            out_specs=[blk, blk, blk, blk]),
        out_shape=[shp, shp, shp, shp],
        compiler_params=_cparams(("parallel",)),
    )(own, grad, from_sibling, from_chips, w, m, v)


def _adam_small(name, g, w, m, v):
    def body(g_ref, w_ref, m_ref, v_ref, d_out, m_out, v_out):
        d, m2, v2 = _adam_math(g_ref[...], w_ref[...], m_ref[...], v_ref[...])
        d_out[...] = d
        m_out[...] = m2
        v_out[...] = v2

    shp = jax.ShapeDtypeStruct(w.shape, F32)
    return pl.pallas_call(body, name=name, out_shape=[shp, shp, shp])(g, w, m, v)


def _all_reduce_small(pack):
    rows = pack.shape[0]

    def body(in_ref, out_ref, slots, send, recv):
        x, y, c, _ = _place()
        me = 4 * x + 2 * y + c
        slots[me] = in_ref[...]
        cps = []
        for k in range(1, N_DEV):
            dx, dy, dc = (k >> 2) & 1, (k >> 1) & 1, k & 1
            to = (jnp.bitwise_xor(x, dx), jnp.bitwise_xor(y, dy), jnp.bitwise_xor(c, dc))
            cps.append(pltpu.make_async_remote_copy(
                src_ref=in_ref, dst_ref=slots.at[me], send_sem=send.at[k - 1], recv_sem=recv.at[k - 1],
                device_id=to, device_id_type=MESH))
        for cp in cps:
            cp.start()
        for cp in cps:
            cp.wait()
        acc = slots[0]
        for d in range(1, N_DEV):
            acc = acc + slots[d]
        out_ref[...] = acc

    return pl.pallas_call(
        body, name="all_reduce_small",
        in_specs=[pl.BlockSpec(memory_space=pltpu.VMEM)],
        out_specs=pl.BlockSpec(memory_space=pltpu.VMEM),
        out_shape=jax.ShapeDtypeStruct((rows, LANES), F32),
        scratch_shapes=[pltpu.VMEM((N_DEV, rows, LANES), F32),
                        pltpu.SemaphoreType.DMA((N_DEV - 1,)), pltpu.SemaphoreType.DMA((N_DEV - 1,))],
    )(pack)


def _rows128(a):
    flat = a.reshape(-1)
    padn = (-flat.shape[0]) % (8 * LANES)
    if padn:
        flat = jnp.concatenate([flat, jnp.zeros((padn,), flat.dtype)])
    return flat.reshape(-1, LANES)


def kernel(x, w_in, w_gk2, b_gk, gla_norm_w, swa_sinks, w_out, ln1_g, ln1_b, w_up, w_down, ln2_g, ln2_b, loss_target, m_w_in, m_w_gk2, m_b_gk, m_gla_norm_w, m_swa_sinks, m_w_out, m_ln1_g, m_ln1_b, m_w_up, m_w_down, m_ln2_g, m_ln2_b, v_w_in, v_w_gk2, v_b_gk, v_gla_norm_w, v_swa_sinks, v_w_out, v_ln1_g, v_ln1_b, v_w_up, v_w_down, v_ln2_g, v_ln2_b):
    xc, yc, cc = lax.axis_index("x"), lax.axis_index("y"), lax.axis_index("c")
    me = 4 * xc + 2 * yc + cc

    x2 = x[0]
    t = x2.shape[0]
    xb = x2.astype(BF16)
    target = loss_target[0]

    first = _all_gather([w_in[0].astype(BF16), w_gk2[0].astype(BF16)])
    w_in_p = _to_padded_cols(first[0].transpose(1, 0, 2).reshape(D_MODEL, D_IN))
    w2 = first[1].transpose(1, 0, 2).reshape(GATE_RANK, GLA_KW)
    w2p = jnp.concatenate([w2, jnp.zeros((LANES - GATE_RANK, GLA_KW), BF16)], axis=0)
    shards = [w_out[0].astype(BF16), w_up[0].astype(BF16), w_down[0].astype(BF16)]
    lands = [lax.dynamic_update_index_in_dim(lax.empty((N_DEV,) + sh.shape, BF16), sh, me, 0) for sh in shards]
    out_plans = [_plan_gather_out] * 3
    fwd_plans = [_plan_gather_forward] * 3
    ag1 = _copies_start("gather_out_start", out_plans, shards, lands, first[0])

    proj = _proj(xb, w_in_p, dep=ag1[4])
    rt = min(256, t)
    ii = jnp.arange(rt)
    tri = ((ii[:, None] // GC == ii[None, :] // GC) & (ii[None, :] <= ii[:, None])).astype(F32)
    bloc, dgdz = _gate_fwd(proj, w2p, b_gk, tri)
    o_raw, states = _gla_fwd(proj, bloc)
    _, lands = _copies_wait("gather_out_wait", out_plans, ag1, o_raw)
    ag2 = _copies_start("gather_forward_start", fwd_plans, [None] * 3, lands, o_raw)
    gla_out = _gla_norm_fwd(o_raw, proj, gla_norm_w)
    swa_out = _swa_fwd(proj, swa_sinks[0], dep=ag2[4])
    cat = jnp.concatenate([gla_out, swa_out], axis=-1)
    _, gathered = _copies_wait("gather_forward_wait", fwd_plans, ag2, swa_out)
    w_out_f = gathered[0].reshape(D_MODEL, D_MODEL)
    w_up_f = gathered[1]
    w_down_f = gathered[2].reshape(D_FF, D_MODEL)
    r1, h1, h1b = _mix_ln1(cat, w_out_f, x2, ln1_g, ln1_b)
    a_act, hdn = _mlp_up(h1b, w_up_f)
    dr2, dr2b, g_ln2_g, g_ln2_b, loss_part = _mlp_down_loss(hdn, w_down_f, h1, target, ln2_g, ln2_b)

    others = [2 * (1 - xc) + yc, 2 * xc + (1 - yc), 2 * (1 - xc) + (1 - yc)]
    blocks = jnp.stack([2 * q + cc for q in others] + others).astype(jnp.int32)
    own = jnp.stack([me, 2 * xc + yc]).astype(jnp.int32)
    wmv = dict(w_in=(w_in, m_w_in, v_w_in), w_out=(w_out, m_w_out, v_w_out), w_up=(w_up, m_w_up, v_w_up),
               w_down=(w_down, m_w_down, v_w_down))
    big = {}

    def sib_land(g):
        return lax.empty((4,) + g.shape[1:], F32)

    def chip_land(g):
        return lax.empty((3,) + g.shape[1:], BF16)

    def finish(nm, g, from_sib, from_chips):
        w, m, v = wmv[nm]
        big[nm] = [o[None] for o in _sum_adam("sum_adam_" + nm, g, from_sib, from_chips, own, w[0], m[0], v[0])]

    du = _mlp_down_bwd(dr2b, w_down_f, a_act)
    g_down = _grad_w_down(hdn, dr2b).reshape(N_DEV, D_FF // N_DEV, D_MODEL)
    sa_down = _copies_start("sibling_down_start", [_plan_sibling], [g_down], [sib_land(g_down)], g_down)
    g_up = _grad_w_up(h1b, du, dep=sa_down[4])
    (g_down,), (fs_down,) = _copies_wait("sibling_down_wait", [_plan_sibling], sa_down, g_up)
    p_down = _pair_sum("pair_sum_w_down", g_down, fs_down, blocks)
    sb_down = _copies_start("chips_down_sibling_up_start", [_plan_chips, _plan_sibling], [p_down, g_up],
                            [chip_land(g_down), sib_land(g_up)], p_down)
    dr1, dr1b, g_ln1_g, g_ln1_b = _mlp_up_bwd_ln1(du, w_up_f, dr2, r1, ln1_g, dep=sb_down[4])
    dcat = _dcat(dr1b, w_out_f)
    (_, g_up), (fc_down, fs_up) = _copies_wait("chips_down_sibling_up_wait", [_plan_chips, _plan_sibling], sb_down,
                                               dcat)
    finish("w_down", g_down, fs_down, fc_down)
    p_up = _pair_sum("pair_sum_w_up", g_up, fs_up, blocks)
    sb_up = _copies_start("chips_up_start", [_plan_chips], [p_up], [chip_land(g_up)], p_up)
    do_raw, dg_g, g_norm_w = _gla_norm_bwd(dcat, o_raw, proj, gla_norm_w, dep=sb_up[4])
    jj = jnp.arange(GC)
    triu = (jj[None, :] >= jj[:, None]).astype(F32)
    dq_g, dk_g, dv_g, dgk = _gla_bwd(proj, bloc, do_raw, states, triu)
    _, (fc_up,) = _copies_wait("chips_up_wait", [_plan_chips], sb_up, dgk)
    finish("w_up", g_up, fs_up, fc_up)
    dlo, gw2_p, g_b_gk = _gate_bwd(dgk, dgdz, proj, w2p)
    dq_s, dk_s, dv_s, g_sinks = _swa_bwd(proj, dcat, swa_sinks[0])
    dproj = jnp.concatenate([dq_g, dk_g, dv_g, dg_g, dq_s, dk_s.astype(BF16), dv_s.astype(BF16), dlo], axis=-1)
    gw_in_p = _grad_w_in(xb, dproj)
    g_in = _from_padded_cols(gw_in_p).reshape(D_MODEL, N_DEV, D_IN // N_DEV).transpose(1, 0, 2)
    sa_in = _copies_start("sibling_in_start", [_plan_sibling], [g_in], [sib_land(g_in)], g_in)
    g_out = _grad_w_out(cat, dr1b, dep=sa_in[4]).reshape(N_DEV, D_MODEL // N_DEV, D_MODEL)
    (g_in,), (fs_in,) = _copies_wait("sibling_in_wait", [_plan_sibling], sa_in, g_out)
    p_in = _pair_sum("pair_sum_w_in", g_in, fs_in, blocks)
    sb_in = _copies_start("chips_in_sibling_out_start", [_plan_chips, _plan_sibling], [p_in, g_out],
                          [chip_land(g_in), sib_land(g_out)], p_in)
    grad_x = _grad_x(dproj, w_in_p, dr1, dep=sb_in[4])
    (_, g_out), (fc_in, fs_out) = _copies_wait("chips_in_sibling_out_wait", [_plan_chips, _plan_sibling], sb_in,
                                               grad_x)
    finish("w_in", g_in, fs_in, fc_in)
    p_out = _pair_sum("pair_sum_w_out", g_out, fs_out, blocks)
    sb_out = _copies_start("chips_out_start", [_plan_chips], [p_out], [chip_land(g_out)], p_out)
    _, (fc_out,) = _copies_wait("chips_out_wait", [_plan_chips], sb_out, p_out)
    finish("w_out", g_out, fs_out, fc_out)

    pieces = [loss_part, g_b_gk, g_norm_w, g_sinks[:, :SWA_HEADS], g_ln1_g, g_ln1_b, g_ln2_g, g_ln2_b,
              gw2_p[:GATE_RANK]]
    pack = jnp.concatenate([_rows128(p) for p in pieces], axis=0)
    tot = _all_reduce_small(pack)
    sizes = [p.size for p in pieces]
    offs = [0]
    for p in pieces:
        offs.append(offs[-1] + _rows128(p).shape[0])
    unpack = lambda i, shape: tot[offs[i]:offs[i + 1]].reshape(-1)[:sizes[i]].reshape(shape)
    loss = tot[0, 0]
    small_names = ["b_gk", "gla_norm_w", "swa_sinks", "ln1_g", "ln1_b", "ln2_g", "ln2_b"]
    small_w = dict(b_gk=(b_gk, m_b_gk, v_b_gk), gla_norm_w=(gla_norm_w, m_gla_norm_w, v_gla_norm_w),
                   swa_sinks=(swa_sinks, m_swa_sinks, v_swa_sinks), ln1_g=(ln1_g, m_ln1_g, v_ln1_g),
                   ln1_b=(ln1_b, m_ln1_b, v_ln1_b), ln2_g=(ln2_g, m_ln2_g, v_ln2_g),
                   ln2_b=(ln2_b, m_ln2_b, v_ln2_b))
    small_g = {nm: unpack(1 + i, small_w[nm][0].shape) for i, nm in enumerate(small_names)}
    g_pack = jnp.concatenate([_rows128(small_g[nm]) for nm in small_names], axis=0)
    wmv = [jnp.concatenate([_rows128(small_w[nm][k]) for nm in small_names], axis=0) for k in range(3)]
    small_out = _adam_small("adam_replicated", g_pack, *wmv)
    srow = [0]
    for nm in small_names:
        srow.append(srow[-1] + _rows128(small_w[nm][0]).shape[0])
    small = {}
    for i, nm in enumerate(small_names):
        shape = small_w[nm][0].shape
        n = small_w[nm][0].size
        small[nm] = [small_g[nm]] + [o[srow[i]:srow[i + 1]].reshape(-1)[:n].reshape(shape) for o in small_out]

    gw2_full = unpack(8, (GATE_RANK, GLA_KW))
    gw2_loc = lax.dynamic_slice_in_dim(gw2_full, me * (GLA_KW // N_DEV), GLA_KW // N_DEV, axis=1)
    gk2_out = _adam_small("adam_w_gk2", gw2_loc, w_gk2[0], m_w_gk2[0], v_w_gk2[0])
    big["w_gk2"] = [gw2_loc[None]] + [o[None] for o in gk2_out]

    order = ["w_in", "w_gk2", "b_gk", "gla_norm_w", "swa_sinks", "w_out", "ln1_g", "ln1_b", "w_up", "w_down",
             "ln2_g", "ln2_b"]
    res = {**big, **small}
    outs = [loss, grad_x[None]]
    for k in range(4):
        outs += [res[nm][k] for nm in order]
    return tuple(outs)
```

```python
import functools

import jax
import jax.numpy as jnp
from jax import lax
from jax.experimental import pallas as pl
from jax.experimental.pallas import tpu as pltpu

F32 = jnp.float32
BF16 = jnp.bfloat16

N_DEV = 8
D_MODEL = 2048
D_FF = 8192
GLA_HEADS = 4
GLA_DK = 128
GLA_DV = 256
GLA_KW = 512
D_GLA = 1024
GATE_RANK = 16
GATE_TAU = 16.0
SWA_HEADS = 16
SWA_DH = 64
SWA_GROUP = 8
WINDOW = 128
D_SWA = 1024
D_IN = 4368
ALPHA = 2.0 ** 0.25
LN_EPS = 1e-5
RMS_EPS = 1e-5
ADAM_LR = 0.001
ADAM_B1 = 0.9
ADAM_B2 = 0.999
ADAM_EPS = 1e-08
ADAM_WD = 0.01
ADAM_STEP = 10

C_QG, C_KG, C_VG, C_GG, C_QS, C_KS, C_VS, C_LO = 0, 512, 1024, 2048, 3072, 4096, 4224, 4352
D_INP = 4480
LANES = 128
GC = 16
NEG = -1e30

NN = ((1,), (0,))
NT = ((1,), (1,))
TN = ((0,), (0,))

VMEM_LIMIT = 52 * 1024 * 1024
MESH = pl.DeviceIdType.MESH


def _dot(a, b, dn, precision=None):
    return lax.dot_general(a, b, (dn, ((), ())), preferred_element_type=F32, precision=precision)


def _bf16_round(v):
    return v.astype(BF16).astype(F32)


def _cparams(dims):
    return pltpu.CompilerParams(dimension_semantics=dims, vmem_limit_bytes=VMEM_LIMIT)


def _dep_operand(dep):
    if dep is None:
        return (), ()
    return (dep,), (pl.BlockSpec(dep.shape, lambda *_: (0,) * dep.ndim),)


def _to_padded_cols(w):
    pad = jnp.zeros(w.shape[:-1] + (D_INP - D_IN,), w.dtype)
    return jnp.concatenate([w[..., :3072], w[..., 3088:], w[..., 3072:3088], pad], axis=-1)


def _from_padded_cols(g):
    return jnp.concatenate([g[..., :3072], g[..., C_LO:C_LO + GATE_RANK], g[..., 3072:C_LO]], axis=-1)


def _matmul(name, a, b, dn, grid, a_spec, b_spec, out_shape, out_specs, acc_shape, *,
            extra=(), extra_specs=(), epilogue=None, dims=("parallel", "parallel", "arbitrary"), dep=None):
    nk = grid[2]
    n_extra = len(extra)
    deps, dep_specs = _dep_operand(dep)
    direct = epilogue is None and (nk == 1 or (not isinstance(out_shape, (list, tuple)) and out_shape.dtype == F32))

    def body(a_ref, b_ref, *rest):
        extra_refs = rest[:n_extra]
        out_refs = rest[n_extra + len(deps):] if direct else rest[n_extra + len(deps):-1]
        acc_ref = out_refs[0] if direct else rest[-1]
        part = _dot(a_ref[...].astype(BF16), b_ref[...].astype(BF16), dn)

        def finish():
            if epilogue is None:
                out_refs[0][...] = acc_ref[...].astype(out_refs[0].dtype)
            else:
                epilogue(acc_ref, extra_refs, out_refs)

        if direct and nk == 1:
            out_refs[0][...] = part.astype(out_refs[0].dtype)
        elif nk == 1:
            acc_ref[...] = part
            finish()
        else:
            k = pl.program_id(2)

            @pl.when(k == 0)
            def _():
                acc_ref[...] = part

            @pl.when(k > 0)
            def _():
                acc_ref[...] += part

            if not direct:
                @pl.when(k == nk - 1)
                def _():
                    finish()

    return pl.pallas_call(
        body, name=name, grid=grid,
        in_specs=[a_spec, b_spec, *extra_specs, *dep_specs],
        out_specs=out_specs, out_shape=out_shape,
        scratch_shapes=[] if direct else [pltpu.VMEM(acc_shape, F32)],
        compiler_params=_cparams(dims),
    )(a, b, *extra, *deps)


def _row_chunks(rows, step=128):
    step = min(step, rows)
    return [(r, step) for r in range(0, rows, step)]


def _ln_stats(r):
    mu = jnp.mean(r, axis=-1, keepdims=True)
    xc = r - mu
    var = jnp.mean(xc * xc, axis=-1, keepdims=True)
    rstd = lax.rsqrt(var + LN_EPS)
    return xc * rstd, rstd


def _ln_bwd(dy_g, xhat, rstd):
    m1 = jnp.mean(dy_g, axis=-1, keepdims=True)
    m2 = jnp.mean(dy_g * xhat, axis=-1, keepdims=True)
    return rstd * (dy_g - m1 - xhat * m2)


def _proj(xb, w_in_p, dep=None):
    t = xb.shape[0]
    tm, tn = min(1024, t), 640
    return _matmul(
        "proj", xb, w_in_p, NN, (t // tm, D_INP // tn, 1),
        pl.BlockSpec((tm, D_MODEL), lambda i, j, k: (i, 0)),
        pl.BlockSpec((D_MODEL, tn), lambda i, j, k: (0, j)),
        jax.ShapeDtypeStruct((t, D_INP), F32),
        pl.BlockSpec((tm, tn), lambda i, j, k: (i, j)),
        (tm, tn), dep=dep)


def _gate_fwd(proj, w2p, b_gk, tri):
    t = proj.shape[0]
    r = tri.shape[0]

    def body(lo_ref, w_ref, b_ref, tri_ref, bloc_ref, dgdz_ref):
        z = _dot(lo_ref[...].astype(BF16), w_ref[...], NN) + b_ref[...]
        e = jnp.exp(-jnp.abs(z))
        gk = (jnp.minimum(z, 0.0) - jnp.log1p(e)) * (1.0 / GATE_TAU)
        inv = 1.0 / (1.0 + e)
        dgdz_ref[...] = jnp.where(z >= 0.0, e * inv, inv) * (1.0 / GATE_TAU)
        bloc_ref[...] = _dot(tri_ref[...], gk, NN, precision=lax.Precision.HIGHEST)

    return pl.pallas_call(
        body, name="gate_fwd", grid=(t // r,),
        in_specs=[pl.BlockSpec((r, LANES), lambda i: (i, C_LO // LANES)),
                  pl.BlockSpec((LANES, GLA_KW), lambda i: (0, 0)),
                  pl.BlockSpec((1, GLA_KW), lambda i: (0, 0)),
                  pl.BlockSpec((r, r), lambda i: (0, 0))],
        out_specs=[pl.BlockSpec((r, GLA_KW), lambda i: (i, 0)),
                   pl.BlockSpec((r, GLA_KW), lambda i: (i, 0))],
        out_shape=[jax.ShapeDtypeStruct((t, GLA_KW), F32), jax.ShapeDtypeStruct((t, GLA_KW), F32)],
        compiler_params=_cparams(("parallel",)),
    )(proj, w2p, b_gk, tri)


def _gla_fwd(proj, bloc):
    t = proj.shape[0]
    r = min(256, t)
    ncb = r // GC
    scale = GLA_DK ** -0.5

    def body(q_ref, k_ref, v_ref, b_ref, o_ref, st_ref, s_scr):
        @pl.when(pl.program_id(1) == 0)
        def _():
            s_scr[...] = jnp.zeros_like(s_scr)

        rows = lax.broadcasted_iota(jnp.int32, (GC, 1), 0)

        def chunk(c, carry):
            r0 = pl.multiple_of(c * GC, GC)
            q = q_ref[pl.ds(r0, GC), :] * scale
            k = k_ref[pl.ds(r0, GC), :]
            v = v_ref[pl.ds(r0, GC), :]
            b = b_ref[pl.ds(r0, GC), :]
            st = s_scr[...]
            st_ref[c] = st
            bl = b[GC - 1:GC, :]
            kr = _bf16_round(k)
            vr = _bf16_round(v)
            o = _dot((q * jnp.exp(b)).astype(BF16), st.astype(BF16), NT)
            for j in range(GC):
                w = jnp.exp(jnp.where(rows >= j, b - b[j:j + 1, :], NEG))
                a = jnp.sum(_bf16_round(q * w) * kr[j:j + 1, :], axis=-1, keepdims=True)
                o = o + _bf16_round(a) * vr[j:j + 1, :]
            o_ref[pl.ds(r0, GC), :] = o
            kd = k * jnp.exp(bl - b)
            s_scr[...] = st * jnp.exp(bl) + _dot(v.astype(BF16), kd.astype(BF16), TN)
            return carry

        lax.fori_loop(0, ncb, chunk, 0, unroll=2)

    return pl.pallas_call(
        body, name="gla_fwd", grid=(GLA_HEADS, t // r),
        in_specs=[pl.BlockSpec((r, GLA_DK), lambda h, i: (i, C_QG // GLA_DK + h)),
                  pl.BlockSpec((r, GLA_DK), lambda h, i: (i, C_KG // GLA_DK + h)),
                  pl.BlockSpec((r, GLA_DV), lambda h, i: (i, C_VG // GLA_DV + h)),
                  pl.BlockSpec((r, GLA_DK), lambda h, i: (i, h))],
        out_specs=[pl.BlockSpec((r, GLA_DV), lambda h, i: (i, h)),
                   pl.BlockSpec((ncb, None, GLA_DV, GLA_DK), lambda h, i: (i, h, 0, 0))],
        out_shape=[jax.ShapeDtypeStruct((t, D_GLA), F32),
                   jax.ShapeDtypeStruct((t // GC, GLA_HEADS, GLA_DV, GLA_DK), F32)],
        scratch_shapes=[pltpu.VMEM((GLA_DV, GLA_DK), F32)],
        compiler_params=_cparams(("parallel", "arbitrary")),
    )(proj, proj, proj, bloc)


def _gla_norm_fwd(o_raw, proj, norm_w):
    t = o_raw.shape[0]
    r = min(512, t)

    def body(o_ref, g_ref, w_ref, out_ref):
        w = w_ref[...]
        for h in range(GLA_HEADS):
            sl = slice(h * GLA_DV, (h + 1) * GLA_DV)
            o = o_ref[:, sl]
            g = g_ref[:, sl]
            on = o * lax.rsqrt(jnp.mean(o * o, axis=-1, keepdims=True) + RMS_EPS)
            out_ref[:, sl] = (on * w * (g * jax.nn.sigmoid(g))).astype(BF16)

    return pl.pallas_call(
        body, name="gla_norm_fwd", grid=(t // r,),
        in_specs=[pl.BlockSpec((r, D_GLA), lambda i: (i, 0)),
                  pl.BlockSpec((r, D_GLA), lambda i: (i, C_GG // D_GLA)),
                  pl.BlockSpec((1, GLA_DV), lambda i: (0, 0))],
        out_specs=pl.BlockSpec((r, D_GLA), lambda i: (i, 0)),
        out_shape=jax.ShapeDtypeStruct((t, D_GLA), BF16),
        compiler_params=_cparams(("parallel",)),
    )(o_raw, proj, norm_w)


def _swa_masks(i):
    qi = lax.broadcasted_iota(jnp.int32, (SWA_GROUP * WINDOW, 1), 0) & (WINDOW - 1)
    kj = lax.broadcasted_iota(jnp.int32, (1, 2 * WINDOW), 1)
    valid = (kj > qi) & (kj <= qi + WINDOW) & ((i > 0) | (kj >= WINDOW))
    lo = lax.broadcasted_iota(jnp.int32, (1, LANES), 1) < SWA_DH
    return valid, lo


def _dup_half(x, lo, kv):
    xr = pltpu.roll(x, SWA_DH, axis=1)
    return jnp.where(lo, x, xr) if kv == 0 else jnp.where(lo, xr, x)


def _swa_stack(ref, col0, lo, kv):
    parts = []
    for p in range(4):
        c0 = col0 + LANES * (4 * kv + p)
        xp = ref[:, c0:c0 + LANES]
        parts += [jnp.where(lo, xp, 0.0).astype(BF16), jnp.where(lo, 0.0, xp).astype(BF16)]
    return jnp.concatenate(parts, axis=0)


def _swa_unstack(x_all, lo, p):
    r0 = 2 * p * WINDOW
    return jnp.where(lo, x_all[r0:r0 + WINDOW, :], x_all[r0 + WINDOW:r0 + 2 * WINDOW, :])


def _swa_sinks(sink_ref, kv):
    return jnp.concatenate([jnp.full((WINDOW, 1), sink_ref[SWA_GROUP * kv + h], F32) for h in range(SWA_GROUP)],
                           axis=0)


def _swa_probs(qm, kdup, valid, sink):
    s = _dot(qm, kdup, NT) * (SWA_DH ** -0.5)
    s = jnp.where(valid, s, NEG)
    m = jnp.maximum(jnp.max(s, axis=-1, keepdims=True), sink)
    p = jnp.exp(s - m)
    es = jnp.exp(sink - m)
    inv = 1.0 / (jnp.sum(p, axis=-1, keepdims=True) + es)
    return p * inv, es * inv


def _swa_fwd(proj, sinks, dep=None):
    t = proj.shape[0]

    deps, dep_specs = _dep_operand(dep)

    def body(sink_ref, q_ref, kp_ref, kc_ref, vp_ref, vc_ref, *rest):
        o_ref = rest[-1]
        i = pl.program_id(0)
        valid, lo = _swa_masks(i)
        kb = jnp.concatenate([kp_ref[...], kc_ref[...]], axis=0)
        vb = jnp.concatenate([vp_ref[...], vc_ref[...]], axis=0)
        for kv in range(2):
            kdup = _dup_half(kb, lo, kv).astype(BF16)
            vdup = _dup_half(vb, lo, kv).astype(BF16)
            pn, _ = _swa_probs(_swa_stack(q_ref, 0, lo, kv), kdup, valid, _swa_sinks(sink_ref, kv))
            o_all = _dot(pn.astype(BF16), vdup, NN)
            for p in range(4):
                c0 = LANES * (4 * kv + p)
                o_ref[:, c0:c0 + LANES] = _swa_unstack(o_all, lo, p).astype(BF16)

    kvspec = lambda col, prev: pl.BlockSpec(
        (WINDOW, LANES), (lambda i: (jnp.maximum(i - 1, 0), col)) if prev else (lambda i: (i, col)))
    return pl.pallas_call(
        body, name="swa_fwd", grid=(t // WINDOW,),
        in_specs=[pl.BlockSpec(memory_space=pltpu.SMEM),
                  pl.BlockSpec((WINDOW, D_SWA), lambda i: (i, C_QS // D_SWA)),
                  kvspec(C_KS // LANES, True), kvspec(C_KS // LANES, False),
                  kvspec(C_VS // LANES, True), kvspec(C_VS // LANES, False), *dep_specs],
        out_specs=pl.BlockSpec((WINDOW, D_SWA), lambda i: (i, 0)),
        out_shape=jax.ShapeDtypeStruct((t, D_SWA), BF16),
        compiler_params=_cparams(("parallel",)),
    )(sinks, proj, proj, proj, proj, proj, *deps)


def _mix_ln1(cat, w_out, x, ln_g, ln_b):
    t = cat.shape[0]
    tm, tk = min(512, t), 512

    def epilogue(acc_ref, extra, outs):
        x_ref, g_ref, b_ref = extra
        r1_ref, h1_ref, h1b_ref = outs
        for r0, n in _row_chunks(tm):
            rs = slice(r0, r0 + n)
            r1 = ALPHA * x_ref[rs, :] + acc_ref[rs, :]
            xhat, _ = _ln_stats(r1)
            h = xhat * g_ref[...] + b_ref[...]
            r1_ref[rs, :] = r1
            h1_ref[rs, :] = h
            h1b_ref[rs, :] = h.astype(BF16)

    row = pl.BlockSpec((tm, D_MODEL), lambda i, j, k: (i, 0))
    vec = pl.BlockSpec((1, D_MODEL), lambda i, j, k: (0, 0))
    return _matmul(
        "mix_ln1", cat, w_out, NN, (t // tm, 1, D_MODEL // tk),
        pl.BlockSpec((tm, tk), lambda i, j, k: (i, k)),
        pl.BlockSpec((tk, D_MODEL), lambda i, j, k: (k, 0)),
        [jax.ShapeDtypeStruct((t, D_MODEL), F32), jax.ShapeDtypeStruct((t, D_MODEL), F32),
         jax.ShapeDtypeStruct((t, D_MODEL), BF16)],
        [row, row, row], (tm, D_MODEL),
        extra=(x, ln_g, ln_b), extra_specs=(row, vec, vec), epilogue=epilogue)


def _mlp_up(h1b, w_up):
    t = h1b.shape[0]
    tm, tn = min(1024, t), 1024

    def epilogue(acc_ref, extra, outs):
        a_ref, hdn_ref = outs
        for r0, n in _row_chunks(tm, 256):
            rs = slice(r0, r0 + n)
            a = jnp.maximum(acc_ref[rs, :], 0.0)
            a_ref[rs, :] = a.astype(BF16)
            hdn_ref[rs, :] = (a * a).astype(BF16)

    out = pl.BlockSpec((tm, tn), lambda i, j, k: (i, j))
    return _matmul(
        "mlp_up", h1b, w_up, NN, (t // tm, D_FF // tn, 1),
        pl.BlockSpec((tm, D_MODEL), lambda i, j, k: (i, 0)),
        pl.BlockSpec((None, D_MODEL, tn), lambda i, j, k: (j, 0, 0)),
        [jax.ShapeDtypeStruct((t, D_FF), BF16), jax.ShapeDtypeStruct((t, D_FF), BF16)],
        [out, out], (tm, tn), epilogue=epilogue)


def _mlp_down_loss(hdn, w_down, h1, target, ln_g, ln_b):
    t = hdn.shape[0]
    tm, tn, tk = min(1024, t), 1024, 2048
    ff = _matmul(
        "mlp_down", hdn, w_down, NN, (t // tm, D_MODEL // tn, D_FF // tk),
        pl.BlockSpec((tm, tk), lambda i, j, k: (i, k)),
        pl.BlockSpec((tk, tn), lambda i, j, k: (k, j)),
        jax.ShapeDtypeStruct((t, D_MODEL), F32),
        pl.BlockSpec((tm, tn), lambda i, j, k: (i, j)), (tm, tn))
    r = min(256, t)

    def body(ff_ref, h1_ref, t_ref, g_ref, b_ref, dr2_ref, dr2b_ref, gg_ref, gb_ref, loss_ref):
        @pl.when(pl.program_id(0) == 0)
        def _():
            gg_ref[...] = jnp.zeros_like(gg_ref)
            gb_ref[...] = jnp.zeros_like(gb_ref)
            loss_ref[...] = jnp.zeros_like(loss_ref)

        for r0, n in _row_chunks(r, 64):
            rs = slice(r0, r0 + n)
            xhat, rstd = _ln_stats(ALPHA * h1_ref[rs, :] + ff_ref[rs, :])
            err = xhat * g_ref[...] + b_ref[...] - t_ref[rs, :]
            loss_ref[...] += 0.5 * jnp.sum(jnp.mean(err * err, axis=-1, keepdims=True))
            dy = err * (1.0 / D_MODEL)
            gg_ref[...] += jnp.sum(dy * xhat, axis=0, keepdims=True)
            gb_ref[...] += jnp.sum(dy, axis=0, keepdims=True)
            dr2 = _ln_bwd(dy * g_ref[...], xhat, rstd)
            dr2_ref[rs, :] = dr2
            dr2b_ref[rs, :] = dr2.astype(BF16)

    row = pl.BlockSpec((r, D_MODEL), lambda i: (i, 0))
    vec = pl.BlockSpec((1, D_MODEL), lambda i: (0, 0))
    return pl.pallas_call(
        body, name="ln2_loss", grid=(t // r,),
        in_specs=[row, row, row, vec, vec],
        out_specs=[row, row, vec, vec, pl.BlockSpec((1, LANES), lambda i: (0, 0))],
        out_shape=[jax.ShapeDtypeStruct((t, D_MODEL), F32), jax.ShapeDtypeStruct((t, D_MODEL), BF16),
                   jax.ShapeDtypeStruct((1, D_MODEL), F32), jax.ShapeDtypeStruct((1, D_MODEL), F32),
                   jax.ShapeDtypeStruct((1, LANES), F32)],
        compiler_params=_cparams(("arbitrary",)),
    )(ff, h1, target, ln_g, ln_b)


def _mlp_down_bwd(dr2b, w_down, a_act):
    t = dr2b.shape[0]
    tm, tn = min(1024, t), 1024

    def epilogue(acc_ref, extra, outs):
        (a_ref,) = extra
        for r0, n in _row_chunks(tm, 256):
            rs = slice(r0, r0 + n)
            outs[0][rs, :] = (acc_ref[rs, :] * (2.0 * a_ref[rs, :].astype(F32))).astype(BF16)

    blk = pl.BlockSpec((tm, tn), lambda i, j, k: (i, j))
    return _matmul(
        "mlp_down_bwd", dr2b, w_down, NT, (t // tm, D_FF // tn, 1),
        pl.BlockSpec((tm, D_MODEL), lambda i, j, k: (i, 0)),
        pl.BlockSpec((tn, D_MODEL), lambda i, j, k: (j, 0)),
        jax.ShapeDtypeStruct((t, D_FF), BF16), blk, (tm, tn),
        extra=(a_act,), extra_specs=(blk,), epilogue=epilogue)


def _grad_w_down(hdn, dr2b):
    t = hdn.shape[0]
    tm, tn = 1024, 1024
    return _matmul(
        "grad_w_down", hdn, dr2b, TN, (D_MODEL // tn, D_FF // tm, 1),
        pl.BlockSpec((t, tm), lambda j, i, k: (0, i)),
        pl.BlockSpec((t, tn), lambda j, i, k: (0, j)),
        jax.ShapeDtypeStruct((D_FF, D_MODEL), F32),
        pl.BlockSpec((tm, tn), lambda j, i, k: (i, j)), (tm, tn))


def _grad_w_up(h1b, du, dep=None):
    t = h1b.shape[0]
    tm, tn = 1024, 1024
    return _matmul(
        "grad_w_up", h1b, du, TN, (N_DEV, D_MODEL // tm, 1),
        pl.BlockSpec((t, tm), lambda i, j, k: (0, j)),
        pl.BlockSpec((t, tn), lambda i, j, k: (0, i)),
        jax.ShapeDtypeStruct((N_DEV, D_MODEL, D_FF // N_DEV), F32),
        pl.BlockSpec((None, tm, tn), lambda i, j, k: (i, j, 0)), (tm, tn), dep=dep)


def _mlp_up_bwd_ln1(du, w_up, dr2, r1, ln_g, dep=None):
    t = du.shape[0]
    tm, tn, tk = min(1024, t), 1024, D_FF // N_DEV
    deps, dep_specs = _dep_operand(dep)

    def mm_body(a_ref, b_ref, *rest):
        o_ref = rest[-1]
        k = pl.program_id(2)
        part = _dot(a_ref[:, :tk], b_ref[0], NT) + _dot(a_ref[:, tk:], b_ref[1], NT)

        @pl.when(k == 0)
        def _():
            o_ref[...] = part

        @pl.when(k > 0)
        def _():
            o_ref[...] += part

    dff = pl.pallas_call(
        mm_body, name="mlp_up_bwd", grid=(t // tm, D_MODEL // tn, N_DEV // 2),
        in_specs=[pl.BlockSpec((tm, 2 * tk), lambda i, j, k: (i, k)),
                  pl.BlockSpec((2, tn, tk), lambda i, j, k: (k, j, 0)), *dep_specs],
        out_specs=pl.BlockSpec((tm, tn), lambda i, j, k: (i, j)),
        out_shape=jax.ShapeDtypeStruct((t, D_MODEL), F32),
        compiler_params=_cparams(("parallel", "parallel", "arbitrary")),
    )(du, w_up, *deps)
    r = min(256, t)

    def body(acc_ref, dr2_ref, r1_ref, g_ref, dr1_ref, dr1b_ref, gg_ref, gb_ref):
        @pl.when(pl.program_id(0) == 0)
        def _():
            gg_ref[...] = jnp.zeros_like(gg_ref)
            gb_ref[...] = jnp.zeros_like(gb_ref)

        for r0, n in _row_chunks(r, 64):
            rs = slice(r0, r0 + n)
            dh1 = ALPHA * dr2_ref[rs, :] + acc_ref[rs, :]
            xhat, rstd = _ln_stats(r1_ref[rs, :])
            gg_ref[...] += jnp.sum(dh1 * xhat, axis=0, keepdims=True)
            gb_ref[...] += jnp.sum(dh1, axis=0, keepdims=True)
            dr1 = _ln_bwd(dh1 * g_ref[...], xhat, rstd)
            dr1_ref[rs, :] = dr1
            dr1b_ref[rs, :] = dr1.astype(BF16)

    row = pl.BlockSpec((r, D_MODEL), lambda i: (i, 0))
    vec = pl.BlockSpec((1, D_MODEL), lambda i: (0, 0))
    return pl.pallas_call(
        body, name="ln1_bwd", grid=(t // r,),
        in_specs=[row, row, row, vec],
        out_specs=[row, row, vec, vec],
        out_shape=[jax.ShapeDtypeStruct((t, D_MODEL), F32), jax.ShapeDtypeStruct((t, D_MODEL), BF16),
                   jax.ShapeDtypeStruct((1, D_MODEL), F32), jax.ShapeDtypeStruct((1, D_MODEL), F32)],
        compiler_params=_cparams(("arbitrary",)),
    )(dff, dr2, r1, ln_g)


def _dcat(dr1b, w_out):
    t = dr1b.shape[0]
    tm, tn = min(1024, t), 1024
    return _matmul(
        "dcat", dr1b, w_out, NT, (t // tm, D_MODEL // tn, 1),
        pl.BlockSpec((tm, D_MODEL), lambda i, j, k: (i, 0)),
        pl.BlockSpec((tn, D_MODEL), lambda i, j, k: (j, 0)),
        jax.ShapeDtypeStruct((t, D_MODEL), F32),
        pl.BlockSpec((tm, tn), lambda i, j, k: (i, j)), (tm, tn))


def _grad_w_out(cat, dr1b, dep=None):
    t = cat.shape[0]
    tm, tn = 1024, 1024
    return _matmul(
        "grad_w_out", cat, dr1b, TN, (D_MODEL // tm, D_MODEL // tn, 1),
        pl.BlockSpec((t, tm), lambda i, j, k: (0, i)),
        pl.BlockSpec((t, tn), lambda i, j, k: (0, j)),
        jax.ShapeDtypeStruct((D_MODEL, D_MODEL), F32),
        pl.BlockSpec((tm, tn), lambda i, j, k: (i, j)), (tm, tn), dep=dep)


def _gla_norm_bwd(dcat, o_raw, proj, norm_w, dep=None):
    t = o_raw.shape[0]
    r = min(512, t)

    deps, dep_specs = _dep_operand(dep)

    def body(d_ref, o_ref, g_ref, w_ref, *rest):
        do_ref, dg_ref, dw_ref = rest[len(deps):]

        @pl.when(pl.program_id(0) == 0)
        def _():
            dw_ref[...] = jnp.zeros_like(dw_ref)

        w = w_ref[...]
        dw = jnp.zeros((1, GLA_DV), F32)
        for h in range(GLA_HEADS):
            sl = slice(h * GLA_DV, (h + 1) * GLA_DV)
            o = o_ref[:, sl]
            g = g_ref[:, sl]
            d = d_ref[:, sl]
            rr = lax.rsqrt(jnp.mean(o * o, axis=-1, keepdims=True) + RMS_EPS)
            on = o * rr
            sg = jax.nn.sigmoid(g)
            sil = g * sg
            dg_ref[:, sl] = (d * on * w * (sg * (1.0 + g * (1.0 - sg)))).astype(BF16)
            dw = dw + jnp.sum(d * on * sil, axis=0, keepdims=True)
            don = d * w * sil
            do_ref[:, sl] = rr * (don - on * jnp.mean(don * on, axis=-1, keepdims=True))
        dw_ref[...] += dw

    return pl.pallas_call(
        body, name="gla_norm_bwd", grid=(t // r,),
        in_specs=[pl.BlockSpec((r, D_GLA), lambda i: (i, 0)),
                  pl.BlockSpec((r, D_GLA), lambda i: (i, 0)),
                  pl.BlockSpec((r, D_GLA), lambda i: (i, C_GG // D_GLA)),
                  pl.BlockSpec((1, GLA_DV), lambda i: (0, 0)), *dep_specs],
        out_specs=[pl.BlockSpec((r, D_GLA), lambda i: (i, 0)),
                   pl.BlockSpec((r, D_GLA), lambda i: (i, 0)),
                   pl.BlockSpec((1, GLA_DV), lambda i: (0, 0))],
        out_shape=[jax.ShapeDtypeStruct((t, D_GLA), F32), jax.ShapeDtypeStruct((t, D_GLA), BF16),
                   jax.ShapeDtypeStruct((1, GLA_DV), F32)],
        compiler_params=_cparams(("arbitrary",)),
    )(dcat, o_raw, proj, norm_w, *deps)


def _gla_bwd(proj, bloc, do_raw, states, triu):
    t = proj.shape[0]
    r = min(256, t)
    ncb = r // GC
    nb = t // r
    scale = GLA_DK ** -0.5

    def body(q_ref, k_ref, v_ref, b_ref, do_ref, st_ref, u_ref, dq_ref, dk_ref, dv_ref, dg_ref, ds_scr):
        @pl.when(pl.program_id(1) == 0)
        def _():
            ds_scr[...] = jnp.zeros_like(ds_scr)

        rows = lax.broadcasted_iota(jnp.int32, (GC, 1), 0)

        def chunk(cc, carry):
            c = ncb - 1 - cc
            r0 = pl.multiple_of(c * GC, GC)
            q = q_ref[pl.ds(r0, GC), :] * scale
            k = k_ref[pl.ds(r0, GC), :]
            v = v_ref[pl.ds(r0, GC), :]
            b = b_ref[pl.ds(r0, GC), :]
            do = do_ref[pl.ds(r0, GC), :]
            st = st_ref[c]
            dsn = ds_scr[...]
            bl = b[GC - 1:GC, :]
            eb = jnp.exp(b)
            ekl = jnp.exp(bl - b)
            ebl = jnp.exp(bl)
            qh = q * eb
            kd = k * ekl
            dob = do.astype(BF16)
            dsb = dsn.astype(BF16)
            dqh = _dot(dob, st.astype(BF16), NN)
            dkd = _dot(v.astype(BF16), dsb, NN)
            dv = _dot(kd.astype(BF16), dsb, NT)
            dq_i = jnp.zeros((GC, GLA_DK), F32)
            dk_i = jnp.zeros((GC, GLA_DK), F32)
            dk_x = jnp.zeros((GC, GLA_DK), F32)
            kr = _bf16_round(k)
            vr = _bf16_round(v)
            dor = _bf16_round(do)
            for j in range(GC):
                w = jnp.exp(jnp.where(rows >= j, b - b[j:j + 1, :], NEG))
                qw = q * w
                qwr = _bf16_round(qw)
                a = _bf16_round(jnp.sum(qwr * kr[j:j + 1, :], axis=-1, keepdims=True))
                da = jnp.sum(dor * vr[j:j + 1, :], axis=-1, keepdims=True)
                dq_i = dq_i + da * (w * k[j:j + 1, :])
                dk_x = jnp.where(rows == j, jnp.sum(da * qw, axis=0, keepdims=True), dk_x)
                dk_i = jnp.where(rows == j, jnp.sum(_bf16_round(da) * qwr, axis=0, keepdims=True), dk_i)
                dv = dv + jnp.where(rows == j, jnp.sum(a * dor, axis=0, keepdims=True), 0.0)
            dqs = dqh * eb + dq_i
            dk = dkd * ekl + dk_i
            db_last = jnp.sum(dkd * kd, axis=0, keepdims=True) + ebl * jnp.sum(dsn * st, axis=0, keepdims=True)
            db = q * dqs - k * (dkd * ekl + dk_x) + jnp.where(rows == GC - 1, db_last, 0.0)
            dq_ref[pl.ds(r0, GC), :] = (dqs * scale).astype(BF16)
            dk_ref[pl.ds(r0, GC), :] = dk.astype(BF16)
            dv_ref[pl.ds(r0, GC), :] = dv.astype(BF16)
            dg_ref[pl.ds(r0, GC), :] = _dot(u_ref[...], db, NN, precision=lax.Precision.HIGHEST)
            ds_scr[...] = dsn * ebl + _dot(dob, qh.astype(BF16), TN)
            return carry

        lax.fori_loop(0, ncb, chunk, 0)

    rev = lambda i: nb - 1 - i
    return pl.pallas_call(
        body, name="gla_bwd", grid=(GLA_HEADS, nb),
        in_specs=[pl.BlockSpec((r, GLA_DK), lambda h, i: (rev(i), C_QG // GLA_DK + h)),
                  pl.BlockSpec((r, GLA_DK), lambda h, i: (rev(i), C_KG // GLA_DK + h)),
                  pl.BlockSpec((r, GLA_DV), lambda h, i: (rev(i), C_VG // GLA_DV + h)),
                  pl.BlockSpec((r, GLA_DK), lambda h, i: (rev(i), h)),
                  pl.BlockSpec((r, GLA_DV), lambda h, i: (rev(i), h)),
                  pl.BlockSpec((ncb, None, GLA_DV, GLA_DK), lambda h, i: (rev(i), h, 0, 0)),
                  pl.BlockSpec((GC, GC), lambda h, i: (0, 0))],
        out_specs=[pl.BlockSpec((r, GLA_DK), lambda h, i: (rev(i), h)),
                   pl.BlockSpec((r, GLA_DK), lambda h, i: (rev(i), h)),
                   pl.BlockSpec((r, GLA_DV), lambda h, i: (rev(i), h)),
                   pl.BlockSpec((r, GLA_DK), lambda h, i: (rev(i), h))],
        out_shape=[jax.ShapeDtypeStruct((t, GLA_KW), BF16), jax.ShapeDtypeStruct((t, GLA_KW), BF16),
                   jax.ShapeDtypeStruct((t, D_GLA), BF16), jax.ShapeDtypeStruct((t, GLA_KW), F32)],
        scratch_shapes=[pltpu.VMEM((GLA_DV, GLA_DK), F32)],
        compiler_params=_cparams(("parallel", "arbitrary")),
    )(proj, proj, proj, bloc, do_raw, states, triu)


def _gate_bwd(dg, dgdz, proj, w2p):
    t = dg.shape[0]
    r = min(512, t)

    def body(dg_ref, s_ref, lo_ref, w_ref, dlo_ref, gw_ref, gb_ref):
        @pl.when(pl.program_id(0) == 0)
        def _():
            gw_ref[...] = jnp.zeros_like(gw_ref)
            gb_ref[...] = jnp.zeros_like(gb_ref)

        dz = dg_ref[...] * s_ref[...]
        dzb = dz.astype(BF16)
        gb_ref[...] += jnp.sum(dz, axis=0, keepdims=True)
        gw_ref[...] += _dot(lo_ref[...].astype(BF16), dzb, TN)
        dlo_ref[...] = _dot(dzb, w_ref[...], NT).astype(BF16)

    return pl.pallas_call(
        body, name="gate_bwd", grid=(t // r,),
        in_specs=[pl.BlockSpec((r, GLA_KW), lambda i: (i, 0)),
                  pl.BlockSpec((r, GLA_KW), lambda i: (i, 0)),
                  pl.BlockSpec((r, LANES), lambda i: (i, C_LO // LANES)),
                  pl.BlockSpec((LANES, GLA_KW), lambda i: (0, 0))],
        out_specs=[pl.BlockSpec((r, LANES), lambda i: (i, 0)),
                   pl.BlockSpec((LANES, GLA_KW), lambda i: (0, 0)),
                   pl.BlockSpec((1, GLA_KW), lambda i: (0, 0))],
        out_shape=[jax.ShapeDtypeStruct((t, LANES), BF16), jax.ShapeDtypeStruct((LANES, GLA_KW), F32),
                   jax.ShapeDtypeStruct((1, GLA_KW), F32)],
        compiler_params=_cparams(("arbitrary",)),
    )(dg, dgdz, proj, w2p)


def _swa_bwd(proj, dcat, sinks, dep=None):
    t = proj.shape[0]

    deps, dep_specs = _dep_operand(dep)

    def body(sink_ref, q_ref, kp_ref, kc_ref, vp_ref, vc_ref, d_ref, *rest):
        dq_ref, dk_ref, dv_ref, dsink_ref = rest[len(deps):]
        i = pl.program_id(0)

        @pl.when(i == 0)
        def _():
            dk_ref[...] = jnp.zeros_like(dk_ref)
            dv_ref[...] = jnp.zeros_like(dv_ref)
            dsink_ref[...] = jnp.zeros_like(dsink_ref)

        valid, lo = _swa_masks(i)
        lane = lax.broadcasted_iota(jnp.int32, (1, LANES), 1)
        kb = jnp.concatenate([kp_ref[...], kc_ref[...]], axis=0)
        vb = jnp.concatenate([vp_ref[...], vc_ref[...]], axis=0)
        dsink = jnp.zeros((1, LANES), F32)
        folded_k, folded_v = [], []
        for kv in range(2):
            kdup = _dup_half(kb, lo, kv).astype(BF16)
            vdup = _dup_half(vb, lo, kv).astype(BF16)
            qm = _swa_stack(q_ref, 0, lo, kv)
            dom = _swa_stack(d_ref, 0, lo, kv)
            pn, psink = _swa_probs(qm, kdup, valid, _swa_sinks(sink_ref, kv))
            dpr = _dot(dom, vdup, NT)
            drow = jnp.sum(dpr * pn, axis=-1, keepdims=True)
            ds_col = psink * drow
            for h in range(SWA_GROUP):
                dsink = dsink + jnp.where(lane == SWA_GROUP * kv + h,
                                          -jnp.sum(ds_col[h * WINDOW:(h + 1) * WINDOW, :]), 0.0)
            dsb = (pn * (dpr - drow) * (SWA_DH ** -0.5)).astype(BF16)
            dq_all = _dot(dsb, kdup, NN)
            for p in range(4):
                c0 = LANES * (4 * kv + p)
                dq_ref[:, c0:c0 + LANES] = _swa_unstack(dq_all, lo, p).astype(BF16)
            dkd = _dot(dsb, qm, TN)
            dvd = _dot(pn.astype(BF16), dom, TN)
            folded_k.append(dkd + pltpu.roll(dkd, SWA_DH, axis=1))
            folded_v.append(dvd + pltpu.roll(dvd, SWA_DH, axis=1))
        dkb = jnp.where(lo, folded_k[0], folded_k[1])
        dvb = jnp.where(lo, folded_v[0], folded_v[1])
        dsink_ref[...] += dsink
        cur = pl.ds(pl.multiple_of(i * WINDOW, WINDOW), WINDOW)
        dk_ref[cur, :] += dkb[WINDOW:, :]
        dv_ref[cur, :] += dvb[WINDOW:, :]

        @pl.when(i > 0)
        def _():
            prev = pl.ds(pl.multiple_of((i - 1) * WINDOW, WINDOW), WINDOW)
            dk_ref[prev, :] += dkb[:WINDOW, :]
            dv_ref[prev, :] += dvb[:WINDOW, :]

    kvspec = lambda col, prev: pl.BlockSpec(
        (WINDOW, LANES), (lambda i: (jnp.maximum(i - 1, 0), col)) if prev else (lambda i: (i, col)))
    full = pl.BlockSpec((t, LANES), lambda i: (0, 0))
    return pl.pallas_call(
        body, name="swa_bwd", grid=(t // WINDOW,),
        in_specs=[pl.BlockSpec(memory_space=pltpu.SMEM),
                  pl.BlockSpec((WINDOW, D_SWA), lambda i: (i, C_QS // D_SWA)),
                  kvspec(C_KS // LANES, True), kvspec(C_KS // LANES, False),
                  kvspec(C_VS // LANES, True), kvspec(C_VS // LANES, False),
                  pl.BlockSpec((WINDOW, D_SWA), lambda i: (i, 1)), *dep_specs],
        out_specs=[pl.BlockSpec((WINDOW, D_SWA), lambda i: (i, 0)), full, full,
                   pl.BlockSpec((1, LANES), lambda i: (0, 0))],
        out_shape=[jax.ShapeDtypeStruct((t, D_SWA), BF16), jax.ShapeDtypeStruct((t, LANES), F32),
                   jax.ShapeDtypeStruct((t, LANES), F32), jax.ShapeDtypeStruct((1, LANES), F32)],
        compiler_params=_cparams(("arbitrary",)),
    )(sinks, proj, proj, proj, proj, proj, dcat, *deps)


def _grad_w_in(xb, dproj):
    t = xb.shape[0]
    tm, tn = 1024, 640
    return _matmul(
        "grad_w_in", xb, dproj, TN, (D_MODEL // tm, D_INP // tn, 1),
        pl.BlockSpec((t, tm), lambda i, j, k: (0, i)),
        pl.BlockSpec((t, tn), lambda i, j, k: (0, j)),
        jax.ShapeDtypeStruct((D_MODEL, D_INP), F32),
        pl.BlockSpec((tm, tn), lambda i, j, k: (i, j)), (tm, tn))


def _grad_x(dproj, w_in_p, dr1, dep=None):
    t = dproj.shape[0]
    tm, tn = min(512, t), 1024

    def epilogue(acc_ref, extra, outs):
        for r0, n in _row_chunks(tm):
            rs = slice(r0, r0 + n)
            outs[0][rs, :] = ALPHA * extra[0][rs, :] + acc_ref[rs, :]

    blk = pl.BlockSpec((tm, tn), lambda j, i, k: (i, j))
    return _matmul(
        "grad_x", dproj, w_in_p, NT, (D_MODEL // tn, t // tm, 1),
        pl.BlockSpec((tm, D_INP), lambda j, i, k: (i, 0)),
        pl.BlockSpec((tn, D_INP), lambda j, i, k: (j, 0)),
        jax.ShapeDtypeStruct((t, D_MODEL), F32), blk, (tm, tn),
        extra=(dr1,), extra_specs=(blk,), epilogue=epilogue, dep=dep)


def _place():
    x, y, c = lax.axis_index("x"), lax.axis_index("y"), lax.axis_index("c")
    chips = [(1 - x, y), (x, 1 - y), (1 - x, 1 - y)]
    return x, y, c, chips


def _all_gather(shards):
    n = len(shards)
    hbm = pl.BlockSpec(memory_space=pl.ANY)

    def body(*refs):
        ins, outs = refs[:n], refs[n:2 * n]
        send, recv, loc = refs[2 * n:]
        x, y, c, chips = _place()
        me, sib = (x, y, c), (x, y, 1 - c)

        def slot(t, px, py, pc):
            return outs[t].at[4 * px + 2 * py + pc]

        def copy(t, k, block, to, src=None):
            return pltpu.make_async_remote_copy(
                src_ref=slot(t, *block) if src is None else src, dst_ref=slot(t, *block),
                send_sem=send.at[7 * t + k], recv_sem=recv.at[7 * t + k], device_id=to, device_id_type=MESH)

        mine = [pltpu.make_async_copy(ins[t], slot(t, *me), loc.at[t]) for t in range(n)]
        for cp in mine:
            cp.start()
        sent = []
        for t in range(n):
            sent.append(copy(t, 0, me, sib, src=ins[t]))
            sent += [copy(t, 1 + j, me, (*chip, c), src=ins[t]) for j, chip in enumerate(chips)]
        for cp in sent:
            cp.start()
        for t in range(n):
            for j, chip in enumerate(chips):
                copy(t, 1 + j, (*chip, c), me).wait_recv()
                fwd = copy(t, 4 + j, (*chip, c), sib)
                fwd.start()
                sent.append(fwd)
        for t in range(n):
            copy(t, 0, sib, me).wait_recv()
            for j, chip in enumerate(chips):
                copy(t, 4 + j, (*chip, 1 - c), me).wait_recv()
        for cp in sent:
            cp.wait_send()
        for cp in mine:
            cp.wait()

    return pl.pallas_call(
        body, name="all_gather_weights",
        in_specs=[hbm] * n, out_specs=[hbm] * n,
        out_shape=[jax.ShapeDtypeStruct((N_DEV,) + s.shape, s.dtype) for s in shards],
        scratch_shapes=[pltpu.SemaphoreType.DMA((7 * n,)), pltpu.SemaphoreType.DMA((7 * n,)),
                        pltpu.SemaphoreType.DMA((n,))],
    )(*shards)


def _plan_gather_out(src, land, x, y, c, chips):
    me = 4 * x + 2 * y + c
    return [(src, land.at[me], to) for to in [(x, y, 1 - c)] + [(px, py, c) for px, py in chips]]


def _plan_gather_forward(src, land, x, y, c, chips):
    return [(land.at[4 * px + 2 * py + c], land.at[4 * px + 2 * py + c], (x, y, 1 - c)) for px, py in chips]


def _plan_sibling(src, land, x, y, c, chips):
    return [(src.at[2 * q + (1 - c)], land.at[q], (x, y, 1 - c)) for q in range(4)]


def _plan_chips(src, land, x, y, c, chips):
    return [(src.at[j], land.at[j], (px, py, c)) for j, (px, py) in enumerate(chips)]


_PLAN_COPIES = {_plan_gather_out: 4, _plan_gather_forward: 3, _plan_sibling: 4, _plan_chips: 3}
_HBM = pl.BlockSpec(memory_space=pltpu.HBM)
_SEM = pl.BlockSpec(memory_space=pltpu.SEMAPHORE)
_EFFECT = pltpu.SideEffectType.DATAFLOW_SIDE_EFFECTING


def _hbm(a):
    return pltpu.with_memory_space_constraint(a, pltpu.HBM)


def _plan_descriptors(plans, srcs, lands, send, recv):
    x, y, c, chips = _place()
    cps = []
    for plan, src, land in zip(plans, srcs, lands):
        for s_ref, d_ref, to in plan(src, land, x, y, c, chips):
            k = len(cps)
            cps.append(pltpu.make_async_remote_copy(src_ref=s_ref, dst_ref=d_ref, send_sem=send.at[k],
                                                    recv_sem=recv.at[k], device_id=to, device_id_type=MESH))
    return cps


def _copies_start(name, plans, srcs, lands, after):
    has_src = [s is not None for s in srcs]
    arrays = [s for s in srcs if s is not None] + list(lands)
    n_src = sum(has_src)
    n_cp = sum(_PLAN_COPIES[p] for p in plans)

    def body(*refs):
        ins = refs[:len(arrays)]
        send, recv = refs[len(arrays) + 1], refs[len(arrays) + 2]
        token = refs[-1]
        it = iter(ins[:n_src])
        src_refs = [next(it) if h else None for h in has_src]
        for cp in _plan_descriptors(plans, src_refs, ins[n_src:], send, recv):
            cp.start()
        token[...] = jnp.zeros_like(token)

    outs = pl.pallas_call(
        body, name=name,
        in_specs=[_HBM] * len(arrays) + [pl.BlockSpec(memory_space=pl.ANY)],
        out_specs=(_SEM, _SEM, *[_HBM] * len(arrays), pl.BlockSpec(memory_space=pltpu.VMEM)),
        out_shape=(pltpu.SemaphoreType.DMA((n_cp,)), pltpu.SemaphoreType.DMA((n_cp,)),
                   *[pltpu.HBM(a.shape, a.dtype) for a in arrays], jax.ShapeDtypeStruct((8, LANES), F32)),
        input_output_aliases={i: 2 + i for i in range(len(arrays))},
        compiler_params=pltpu.CompilerParams(has_side_effects=_EFFECT),
    )(*[_hbm(a) for a in arrays], after)
    send, recv = outs[0], outs[1]
    thru = list(outs[2:-1])
    it = iter(thru[:n_src])
    return send, recv, [next(it) if h else None for h in has_src], thru[n_src:], outs[-1]


def _copies_wait(name, plans, started, after):
    send, recv, srcs, lands, _ = started
    has_src = [s is not None for s in srcs]
    arrays = [s for s in srcs if s is not None] + list(lands)
    n_src = sum(has_src)

    def body(*refs):
        ins = refs[:len(arrays)]
        send_ref, recv_ref = refs[len(arrays)], refs[len(arrays) + 1]
        it = iter(ins[:n_src])
        src_refs = [next(it) if h else None for h in has_src]
        for cp in _plan_descriptors(plans, src_refs, ins[n_src:], send_ref, recv_ref):
            cp.wait_send()
            cp.wait_recv()

    outs = pl.pallas_call(
        body, name=name,
        in_specs=[_HBM] * len(arrays) + [_SEM, _SEM, pl.BlockSpec(memory_space=pl.ANY)],
        out_specs=tuple([_HBM] * len(arrays)),
        out_shape=tuple(pltpu.HBM(a.shape, a.dtype) for a in arrays),
        input_output_aliases={i: i for i in range(len(arrays))},
        compiler_params=pltpu.CompilerParams(has_side_effects=_EFFECT),
    )(*arrays, send, recv, after)
    return list(outs[:n_src]), list(outs[n_src:])


def _pair_sum(name, grad, from_sibling, blocks):
    _, rows, cols = grad.shape
    tr = 256

    def body(blk_ref, g_ref, s_ref, o_ref):
        o_ref[...] = (g_ref[...] + s_ref[...]).astype(BF16)

    return pl.pallas_call(
        body, name=name,
        grid_spec=pltpu.PrefetchScalarGridSpec(
            num_scalar_prefetch=1, grid=(3, rows // tr),
            in_specs=[pl.BlockSpec((None, tr, cols), lambda j, r, br: (br[j], r, 0)),
                      pl.BlockSpec((None, tr, cols), lambda j, r, br: (br[3 + j], r, 0))],
            out_specs=pl.BlockSpec((None, tr, cols), lambda j, r, br: (j, r, 0))),
        out_shape=jax.ShapeDtypeStruct((3, rows, cols), BF16),
        compiler_params=_cparams(("parallel", "parallel")),
    )(blocks, grad, from_sibling)


def _adam_math(g, w, m, v):
    m2 = ADAM_B1 * m + (1.0 - ADAM_B1) * g
    v2 = ADAM_B2 * v + (1.0 - ADAM_B2) * (g * g)
    m_hat = m2 / (1.0 - ADAM_B1 ** ADAM_STEP)
    v_hat = v2 / (1.0 - ADAM_B2 ** ADAM_STEP)
    delta = -ADAM_LR * (m_hat / (jnp.sqrt(v_hat) + ADAM_EPS) + ADAM_WD * w)
    return delta, m2, v2


def _sum_adam(name, grad, from_sibling, from_chips, own, w, m, v):
    rows, cols = w.shape
    tr = 128

    def body(own_ref, p_ref, s_ref, r_ref, w_ref, m_ref, v_ref, g_out, d_out, m_out, v_out):
        g = p_ref[...] + s_ref[...]
        for j in range(3):
            g = g + r_ref[j].astype(F32)
        d, m2, v2 = _adam_math(g, w_ref[...], m_ref[...], v_ref[...])
        g_out[...] = g
        d_out[...] = d
        m_out[...] = m2
        v_out[...] = v2

    blk = pl.BlockSpec((tr, cols), lambda r, cr: (r, 0))
    shp = jax.ShapeDtypeStruct((rows, cols), F32)
    return pl.pallas_call(
        body, name=name,
        grid_spec=pltpu.PrefetchScalarGridSpec(
            num_scalar_prefetch=1, grid=(rows // tr,),
            in_specs=[pl.BlockSpec((None, tr, cols), lambda r, cr: (cr[0], r, 0)),
                      pl.BlockSpec((None, tr, cols), lambda r, cr: (cr[1], r, 0)),
                      pl.BlockSpec((3, tr, cols), lambda r, cr: (0, r, 0)),
                      blk, blk, blk],
            out_specs=[blk, blk, blk, blk]),
        out_shape=[shp, shp, shp, shp],
        compiler_params=_cparams(("parallel",)),
    )(own, grad, from_sibling, from_chips, w, m, v)


def _adam_small(name, g, w, m, v):
    def body(g_ref, w_ref, m_ref, v_ref, d_out, m_out, v_out):
        d, m2, v2 = _adam_math(g_ref[...], w_ref[...], m_ref[...], v_ref[...])
        d_out[...] = d
        m_out[...] = m2
        v_out[...] = v2

    shp = jax.ShapeDtypeStruct(w.shape, F32)
    return pl.pallas_call(body, name=name, out_shape=[shp, shp, shp])(g, w, m, v)


def _all_reduce_small(pack):
    rows = pack.shape[0]

    def body(in_ref, out_ref, slots, send, recv):
        x, y, c, _ = _place()
        me = 4 * x + 2 * y + c
        slots[me] = in_ref[...]
        cps = []
        for k in range(1, N_DEV):
            dx, dy, dc = (k >> 2) & 1, (k >> 1) & 1, k & 1
            to = (jnp.bitwise_xor(x, dx), jnp.bitwise_xor(y, dy), jnp.bitwise_xor(c, dc))
            cps.append(pltpu.make_async_remote_copy(
                src_ref=in_ref, dst_ref=slots.at[me], send_sem=send.at[k - 1], recv_sem=recv.at[k - 1],
                device_id=to, device_id_type=MESH))
        for cp in cps:
            cp.start()
        for cp in cps:
            cp.wait()
        acc = slots[0]
        for d in range(1, N_DEV):
            acc = acc + slots[d]
        out_ref[...] = acc

    return pl.pallas_call(
        body, name="all_reduce_small",
        in_specs=[pl.BlockSpec(memory_space=pltpu.VMEM)],
        out_specs=pl.BlockSpec(memory_space=pltpu.VMEM),
        out_shape=jax.ShapeDtypeStruct((rows, LANES), F32),
        scratch_shapes=[pltpu.VMEM((N_DEV, rows, LANES), F32),
                        pltpu.SemaphoreType.DMA((N_DEV - 1,)), pltpu.SemaphoreType.DMA((N_DEV - 1,))],
    )(pack)


def _rows128(a):
    flat = a.reshape(-1)
    padn = (-flat.shape[0]) % (8 * LANES)
    if padn:
        flat = jnp.concatenate([flat, jnp.zeros((padn,), flat.dtype)])
    return flat.reshape(-1, LANES)


def kernel(x, w_in, w_gk2, b_gk, gla_norm_w, swa_sinks, w_out, ln1_g, ln1_b, w_up, w_down, ln2_g, ln2_b, loss_target, m_w_in, m_w_gk2, m_b_gk, m_gla_norm_w, m_swa_sinks, m_w_out, m_ln1_g, m_ln1_b, m_w_up, m_w_down, m_ln2_g, m_ln2_b, v_w_in, v_w_gk2, v_b_gk, v_gla_norm_w, v_swa_sinks, v_w_out, v_ln1_g, v_ln1_b, v_w_up, v_w_down, v_ln2_g, v_ln2_b):
    xc, yc, cc = lax.axis_index("x"), lax.axis_index("y"), lax.axis_index("c")
    me = 4 * xc + 2 * yc + cc

    x2 = x[0]
    t = x2.shape[0]
    xb = x2.astype(BF16)
    target = loss_target[0]

    first = _all_gather([w_in[0].astype(BF16), w_gk2[0].astype(BF16)])
    w_in_p = _to_padded_cols(first[0].transpose(1, 0, 2).reshape(D_MODEL, D_IN))
    w2 = first[1].transpose(1, 0, 2).reshape(GATE_RANK, GLA_KW)
    w2p = jnp.concatenate([w2, jnp.zeros((LANES - GATE_RANK, GLA_KW), BF16)], axis=0)
    shards = [w_out[0].astype(BF16), w_up[0].astype(BF16), w_down[0].astype(BF16)]
    lands = [lax.dynamic_update_index_in_dim(lax.empty((N_DEV,) + sh.shape, BF16), sh, me, 0) for sh in shards]
    out_plans = [_plan_gather_out] * 3
    fwd_plans = [_plan_gather_forward] * 3
    ag1 = _copies_start("gather_out_start", out_plans, shards, lands, first[0])

    proj = _proj(xb, w_in_p, dep=ag1[4])
    rt = min(256, t)
    ii = jnp.arange(rt)
    tri = ((ii[:, None] // GC == ii[None, :] // GC) & (ii[None, :] <= ii[:, None])).astype(F32)
    bloc, dgdz = _gate_fwd(proj, w2p, b_gk, tri)
    o_raw, states = _gla_fwd(proj, bloc)
    _, lands = _copies_wait("gather_out_wait", out_plans, ag1, o_raw)
    ag2 = _copies_start("gather_forward_start", fwd_plans, [None] * 3, lands, o_raw)
    gla_out = _gla_norm_fwd(o_raw, proj, gla_norm_w)
    swa_out = _swa_fwd(proj, swa_sinks[0], dep=ag2[4])
    cat = jnp.concatenate([gla_out, swa_out], axis=-1)
    _, gathered = _copies_wait("gather_forward_wait", fwd_plans, ag2, swa_out)
    w_out_f = gathered[0].reshape(D_MODEL, D_MODEL)
    w_up_f = gathered[1]
    w_down_f = gathered[2].reshape(D_FF, D_MODEL)
    r1, h1, h1b = _mix_ln1(cat, w_out_f, x2, ln1_g, ln1_b)
    a_act, hdn = _mlp_up(h1b, w_up_f)
    dr2, dr2b, g_ln2_g, g_ln2_b, loss_part = _mlp_down_loss(hdn, w_down_f, h1, target, ln2_g, ln2_b)

    others = [2 * (1 - xc) + yc, 2 * xc + (1 - yc), 2 * (1 - xc) + (1 - yc)]
    blocks = jnp.stack([2 * q + cc for q in others] + others).astype(jnp.int32)
    own = jnp.stack([me, 2 * xc + yc]).astype(jnp.int32)
    wmv = dict(w_in=(w_in, m_w_in, v_w_in), w_out=(w_out, m_w_out, v_w_out), w_up=(w_up, m_w_up, v_w_up),
               w_down=(w_down, m_w_down, v_w_down))
    big = {}

    def sib_land(g):
        return lax.empty((4,) + g.shape[1:], F32)

    def chip_land(g):
        return lax.empty((3,) + g.shape[1:], BF16)

    def finish(nm, g, from_sib, from_chips):
        w, m, v = wmv[nm]
        big[nm] = [o[None] for o in _sum_adam("sum_adam_" + nm, g, from_sib, from_chips, own, w[0], m[0], v[0])]

    du = _mlp_down_bwd(dr2b, w_down_f, a_act)
    g_down = _grad_w_down(hdn, dr2b).reshape(N_DEV, D_FF // N_DEV, D_MODEL)
    sa_down = _copies_start("sibling_down_start", [_plan_sibling], [g_down], [sib_land(g_down)], g_down)
    g_up = _grad_w_up(h1b, du, dep=sa_down[4])
    (g_down,), (fs_down,) = _copies_wait("sibling_down_wait", [_plan_sibling], sa_down, g_up)
    p_down = _pair_sum("pair_sum_w_down", g_down, fs_down, blocks)
    sb_down = _copies_start("chips_down_sibling_up_start", [_plan_chips, _plan_sibling], [p_down, g_up],
                            [chip_land(g_down), sib_land(g_up)], p_down)
    dr1, dr1b, g_ln1_g, g_ln1_b = _mlp_up_bwd_ln1(du, w_up_f, dr2, r1, ln1_g, dep=sb_down[4])
    dcat = _dcat(dr1b, w_out_f)
    (_, g_up), (fc_down, fs_up) = _copies_wait("chips_down_sibling_up_wait", [_plan_chips, _plan_sibling], sb_down,
                                               dcat)
    finish("w_down", g_down, fs_down, fc_down)
    p_up = _pair_sum("pair_sum_w_up", g_up, fs_up, blocks)
    sb_up = _copies_start("chips_up_start", [_plan_chips], [p_up], [chip_land(g_up)], p_up)
    do_raw, dg_g, g_norm_w = _gla_norm_bwd(dcat, o_raw, proj, gla_norm_w, dep=sb_up[4])
    jj = jnp.arange(GC)
    triu = (jj[None, :] >= jj[:, None]).astype(F32)
    dq_g, dk_g, dv_g, dgk = _gla_bwd(proj, bloc, do_raw, states, triu)
    _, (fc_up,) = _copies_wait("chips_up_wait", [_plan_chips], sb_up, dgk)
    finish("w_up", g_up, fs_up, fc_up)
    dlo, gw2_p, g_b_gk = _gate_bwd(dgk, dgdz, proj, w2p)
    dq_s, dk_s, dv_s, g_sinks = _swa_bwd(proj, dcat, swa_sinks[0])
    dproj = jnp.concatenate([dq_g, dk_g, dv_g, dg_g, dq_s, dk_s.astype(BF16), dv_s.astype(BF16), dlo], axis=-1)
    gw_in_p = _grad_w_in(xb, dproj)
    g_in = _from_padded_cols(gw_in_p).reshape(D_MODEL, N_DEV, D_IN // N_DEV).transpose(1, 0, 2)
    sa_in = _copies_start("sibling_in_start", [_plan_sibling], [g_in], [sib_land(g_in)], g_in)
    g_out = _grad_w_out(cat, dr1b, dep=sa_in[4]).reshape(N_DEV, D_MODEL // N_DEV, D_MODEL)
    (g_in,), (fs_in,) = _copies_wait("sibling_in_wait", [_plan_sibling], sa_in, g_out)
    p_in = _pair_sum("pair_sum_w_in", g_in, fs_in, blocks)
    sb_in = _copies_start("chips_in_sibling_out_start", [_plan_chips, _plan_sibling], [p_in, g_out],
                          [chip_land(g_in), sib_land(g_out)], p_in)
    grad_x = _grad_x(dproj, w_in_p, dr1, dep=sb_in[4])
    (_, g_out), (fc_in, fs_out) = _copies_wait("chips_in_sibling_out_wait", [_plan_chips, _plan_sibling], sb_in,
                                               grad_x)
    finish("w_in", g_in, fs_in, fc_in)
    p_out = _pair_sum("pair_sum_w_out", g_out, fs_out, blocks)
    sb_out = _copies_start("chips_out_start", [_plan_chips], [p_out], [chip_land(g_out)], p_out)
    _, (fc_out,) = _copies_wait("chips_out_wait", [_plan_chips], sb_out, p_out)
    finish("w_out", g_out, fs_out, fc_out)

    pieces = [loss_part, g_b_gk, g_norm_w, g_sinks[:, :SWA_HEADS], g_ln1_g, g_ln1_b, g_ln2_g, g_ln2_b,
              gw2_p[:GATE_RANK]]
    pack = jnp.concatenate([_rows128(p) for p in pieces], axis=0)
    tot = _all_reduce_small(pack)
    sizes = [p.size for p in pieces]
    offs = [0]
    for p in pieces:
        offs.append(offs[-1] + _rows128(p).shape[0])
    unpack = lambda i, shape: tot[offs[i]:offs[i + 1]].reshape(-1)[:sizes[i]].reshape(shape)
    loss = tot[0, 0]
    small_names = ["b_gk", "gla_norm_w", "swa_sinks", "ln1_g", "ln1_b", "ln2_g", "ln2_b"]
    small_w = dict(b_gk=(b_gk, m_b_gk, v_b_gk), gla_norm_w=(gla_norm_w, m_gla_norm_w, v_gla_norm_w),
                   swa_sinks=(swa_sinks, m_swa_sinks, v_swa_sinks), ln1_g=(ln1_g, m_ln1_g, v_ln1_g),
                   ln1_b=(ln1_b, m_ln1_b, v_ln1_b), ln2_g=(ln2_g, m_ln2_g, v_ln2_g),
                   ln2_b=(ln2_b, m_ln2_b, v_ln2_b))
    small_g = {nm: unpack(1 + i, small_w[nm][0].shape) for i, nm in enumerate(small_names)}
    g_pack = jnp.concatenate([_rows128(small_g[nm]) for nm in small_names], axis=0)
    wmv = [jnp.concatenate([_rows128(small_w[nm][k]) for nm in small_names], axis=0) for k in range(3)]
    small_out = _adam_small("adam_replicated", g_pack, *wmv)
    srow = [0]
    for nm in small_names:
        srow.append(srow[-1] + _rows128(small_w[nm][0]).shape[0])
    small = {}
    for i, nm in enumerate(small_names):
        shape = small_w[nm][0].shape
        n = small_w[nm][0].size
        small[nm] = [small_g[nm]] + [o[srow[i]:srow[i + 1]].reshape(-1)[:n].reshape(shape) for o in small_out]

    gw2_full = unpack(8, (GATE_RANK, GLA_KW))
    gw2_loc = lax.dynamic_slice_in_dim(gw2_full, me * (GLA_KW // N_DEV), GLA_KW // N_DEV, axis=1)
    gk2_out = _adam_small("adam_w_gk2", gw2_loc, w_gk2[0], m_w_gk2[0], v_w_gk2[0])
    big["w_gk2"] = [gw2_loc[None]] + [o[None] for o in gk2_out]

    order = ["w_in", "w_gk2", "b_gk", "gla_norm_w", "swa_sinks", "w_out", "ln1_g", "ln1_b", "w_up", "w_down",
             "ln2_g", "ln2_b"]
    res = {**big, **small}
    outs = [loss, grad_x[None]]
    for k in range(4):
        outs += [res[nm][k] for nm in order]
    return tuple(outs)
```

```python
import functools

import jax
import jax.numpy as jnp
from jax import lax
from jax.experimental import pallas as pl
from jax.experimental.pallas import tpu as pltpu

F32 = jnp.float32
BF16 = jnp.bfloat16

N_DEV = 8
D_MODEL = 2048
D_FF = 8192
GLA_HEADS = 4
GLA_DK = 128
GLA_DV = 256
GLA_KW = 512
D_GLA = 1024
GATE_RANK = 16
GATE_TAU = 16.0
SWA_HEADS = 16
SWA_DH = 64
SWA_GROUP = 8
WINDOW = 128
D_SWA = 1024
D_IN = 4368
ALPHA = 2.0 ** 0.25
LN_EPS = 1e-5
RMS_EPS = 1e-5
ADAM_LR = 0.001
ADAM_B1 = 0.9
ADAM_B2 = 0.999
ADAM_EPS = 1e-08
ADAM_WD = 0.01
ADAM_STEP = 10

C_QG, C_KG, C_VG, C_GG, C_QS, C_KS, C_VS, C_LO = 0, 512, 1024, 2048, 3072, 4096, 4224, 4352
D_INP = 4480
LANES = 128
GC = 16
NEG = -1e30

NN = ((1,), (0,))
NT = ((1,), (1,))
TN = ((0,), (0,))

VMEM_LIMIT = 52 * 1024 * 1024
MESH = pl.DeviceIdType.MESH


def _dot(a, b, dn, precision=None):
    return lax.dot_general(a, b, (dn, ((), ())), preferred_element_type=F32, precision=precision)


def _bf16_round(v):
    return v.astype(BF16).astype(F32)


def _cparams(dims):
    return pltpu.CompilerParams(dimension_semantics=dims, vmem_limit_bytes=VMEM_LIMIT)


def _dep_operand(dep):
    if dep is None:
        return (), ()
    return (dep,), (pl.BlockSpec(dep.shape, lambda *_: (0,) * dep.ndim),)


def _to_padded_cols(w):
    pad = jnp.zeros(w.shape[:-1] + (D_INP - D_IN,), w.dtype)
    return jnp.concatenate([w[..., :3072], w[..., 3088:], w[..., 3072:3088], pad], axis=-1)


def _from_padded_cols(g):
    return jnp.concatenate([g[..., :3072], g[..., C_LO:C_LO + GATE_RANK], g[..., 3072:C_LO]], axis=-1)


def _matmul(name, a, b, dn, grid, a_spec, b_spec, out_shape, out_specs, acc_shape, *,
            extra=(), extra_specs=(), epilogue=None, dims=("parallel", "parallel", "arbitrary"), dep=None):
    nk = grid[2]
    n_extra = len(extra)
    deps, dep_specs = _dep_operand(dep)
    direct = epilogue is None and (nk == 1 or (not isinstance(out_shape, (list, tuple)) and out_shape.dtype == F32))

    def body(a_ref, b_ref, *rest):
        extra_refs = rest[:n_extra]
        out_refs = rest[n_extra + len(deps):] if direct else rest[n_extra + len(deps):-1]
        acc_ref = out_refs[0] if direct else rest[-1]
        part = _dot(a_ref[...].astype(BF16), b_ref[...].astype(BF16), dn)

        def finish():
            if epilogue is None:
                out_refs[0][...] = acc_ref[...].astype(out_refs[0].dtype)
            else:
                epilogue(acc_ref, extra_refs, out_refs)

        if direct and nk == 1:
            out_refs[0][...] = part.astype(out_refs[0].dtype)
        elif nk == 1:
            acc_ref[...] = part
            finish()
        else:
            k = pl.program_id(2)

            @pl.when(k == 0)
            def _():
                acc_ref[...] = part

            @pl.when(k > 0)
            def _():
                acc_ref[...] += part

            if not direct:
                @pl.when(k == nk - 1)
                def _():
                    finish()

    return pl.pallas_call(
        body, name=name, grid=grid,
        in_specs=[a_spec, b_spec, *extra_specs, *dep_specs],
        out_specs=out_specs, out_shape=out_shape,
        scratch_shapes=[] if direct else [pltpu.VMEM(acc_shape, F32)],
        compiler_params=_cparams(dims),
    )(a, b, *extra, *deps)


def _row_chunks(rows, step=128):
    step = min(step, rows)
    return [(r, step) for r in range(0, rows, step)]


def _ln_stats(r):
    mu = jnp.mean(r, axis=-1, keepdims=True)
    xc = r - mu
    var = jnp.mean(xc * xc, axis=-1, keepdims=True)
    rstd = lax.rsqrt(var + LN_EPS)
    return xc * rstd, rstd


def _ln_bwd(dy_g, xhat, rstd):
    m1 = jnp.mean(dy_g, axis=-1, keepdims=True)
    m2 = jnp.mean(dy_g * xhat, axis=-1, keepdims=True)
    return rstd * (dy_g - m1 - xhat * m2)


def _proj(xb, w_in_p, dep=None):
    t = xb.shape[0]
    tm, tn = min(1024, t), 640
    return _matmul(
        "proj", xb, w_in_p, NN, (t // tm, D_INP // tn, 1),
        pl.BlockSpec((tm, D_MODEL), lambda i, j, k: (i, 0)),
        pl.BlockSpec((D_MODEL, tn), lambda i, j, k: (0, j)),
        jax.ShapeDtypeStruct((t, D_INP), F32),
        pl.BlockSpec((tm, tn), lambda i, j, k: (i, j)),
        (tm, tn), dep=dep)


def _gate_fwd(proj, w2p, b_gk, tri):
    t = proj.shape[0]
    r = tri.shape[0]

    def body(lo_ref, w_ref, b_ref, tri_ref, bloc_ref, dgdz_ref):
        z = _dot(lo_ref[...].astype(BF16), w_ref[...], NN) + b_ref[...]
        e = jnp.exp(-jnp.abs(z))
        gk = (jnp.minimum(z, 0.0) - jnp.log1p(e)) * (1.0 / GATE_TAU)
        inv = 1.0 / (1.0 + e)
        dgdz_ref[...] = jnp.where(z >= 0.0, e * inv, inv) * (1.0 / GATE_TAU)
        bloc_ref[...] = _dot(tri_ref[...], gk, NN, precision=lax.Precision.HIGHEST)

    return pl.pallas_call(
        body, name="gate_fwd", grid=(t // r,),
        in_specs=[pl.BlockSpec((r, LANES), lambda i: (i, C_LO // LANES)),
                  pl.BlockSpec((LANES, GLA_KW), lambda i: (0, 0)),
                  pl.BlockSpec((1, GLA_KW), lambda i: (0, 0)),
                  pl.BlockSpec((r, r), lambda i: (0, 0))],
        out_specs=[pl.BlockSpec((r, GLA_KW), lambda i: (i, 0)),
                   pl.BlockSpec((r, GLA_KW), lambda i: (i, 0))],
        out_shape=[jax.ShapeDtypeStruct((t, GLA_KW), F32), jax.ShapeDtypeStruct((t, GLA_KW), F32)],
        compiler_params=_cparams(("parallel",)),
    )(proj, w2p, b_gk, tri)


def _gla_fwd(proj, bloc):
    t = proj.shape[0]
    r = min(256, t)
    ncb = r // GC
    scale = GLA_DK ** -0.5

    def body(q_ref, k_ref, v_ref, b_ref, o_ref, st_ref, s_scr):
        @pl.when(pl.program_id(1) == 0)
        def _():
            s_scr[...] = jnp.zeros_like(s_scr)

        rows = lax.broadcasted_iota(jnp.int32, (GC, 1), 0)
        cols = lax.broadcasted_iota(jnp.int32, (1, GC), 1)

        def chunk(c, carry):
            r0 = pl.multiple_of(c * GC, GC)
            q = q_ref[pl.ds(r0, GC), :] * scale
            k = k_ref[pl.ds(r0, GC), :]
            v = v_ref[pl.ds(r0, GC), :]
            b = b_ref[pl.ds(r0, GC), :]
            st = s_scr[...]
            st_ref[c] = st
            bl = b[GC - 1:GC, :]
            kr = _bf16_round(k)
            att = jnp.zeros((GC, GC), F32)
            for j in range(GC):
                w = jnp.exp(jnp.where(rows >= j, b - b[j:j + 1, :], NEG))
                a = jnp.sum(_bf16_round(q * w) * kr[j:j + 1, :], axis=-1, keepdims=True)
                att = jnp.where(cols == j, a, att)
            vb = v.astype(BF16)
            o_ref[pl.ds(r0, GC), :] = (_dot((q * jnp.exp(b)).astype(BF16), st.astype(BF16), NT)
                                       + _dot(att.astype(BF16), vb, NN))
            kd = k * jnp.exp(bl - b)
            s_scr[...] = st * jnp.exp(bl) + _dot(vb, kd.astype(BF16), TN)
            return carry

        lax.fori_loop(0, ncb, chunk, 0, unroll=2)

    return pl.pallas_call(
        body, name="gla_fwd", grid=(GLA_HEADS, t // r),
        in_specs=[pl.BlockSpec((r, GLA_DK), lambda h, i: (i, C_QG // GLA_DK + h)),
                  pl.BlockSpec((r, GLA_DK), lambda h, i: (i, C_KG // GLA_DK + h)),
                  pl.BlockSpec((r, GLA_DV), lambda h, i: (i, C_VG // GLA_DV + h)),
                  pl.BlockSpec((r, GLA_DK), lambda h, i: (i, h))],
        out_specs=[pl.BlockSpec((r, GLA_DV), lambda h, i: (i, h)),
                   pl.BlockSpec((ncb, None, GLA_DV, GLA_DK), lambda h, i: (i, h, 0, 0))],
        out_shape=[jax.ShapeDtypeStruct((t, D_GLA), F32),
                   jax.ShapeDtypeStruct((t // GC, GLA_HEADS, GLA_DV, GLA_DK), F32)],
        scratch_shapes=[pltpu.VMEM((GLA_DV, GLA_DK), F32)],
        compiler_params=_cparams(("parallel", "arbitrary")),
    )(proj, proj, proj, bloc)


def _gla_norm_fwd(o_raw, proj, norm_w):
    t = o_raw.shape[0]
    r = min(512, t)

    def body(o_ref, g_ref, w_ref, out_ref):
        w = w_ref[...]
        for h in range(GLA_HEADS):
            sl = slice(h * GLA_DV, (h + 1) * GLA_DV)
            o = o_ref[:, sl]
            g = g_ref[:, sl]
            on = o * lax.rsqrt(jnp.mean(o * o, axis=-1, keepdims=True) + RMS_EPS)
            out_ref[:, sl] = (on * w * (g * jax.nn.sigmoid(g))).astype(BF16)

    return pl.pallas_call(
        body, name="gla_norm_fwd", grid=(t // r,),
        in_specs=[pl.BlockSpec((r, D_GLA), lambda i: (i, 0)),
                  pl.BlockSpec((r, D_GLA), lambda i: (i, C_GG // D_GLA)),
                  pl.BlockSpec((1, GLA_DV), lambda i: (0, 0))],
        out_specs=pl.BlockSpec((r, D_GLA), lambda i: (i, 0)),
        out_shape=jax.ShapeDtypeStruct((t, D_GLA), BF16),
        compiler_params=_cparams(("parallel",)),
    )(o_raw, proj, norm_w)


def _swa_masks(i, heads):
    qi = lax.broadcasted_iota(jnp.int32, (heads * WINDOW, 1), 0) & (WINDOW - 1)
    kj = lax.broadcasted_iota(jnp.int32, (1, 2 * WINDOW), 1)
    valid = (kj > qi) & (kj <= qi + WINDOW) & ((i > 0) | (kj >= WINDOW))
    lo = lax.broadcasted_iota(jnp.int32, (1, LANES), 1) < SWA_DH
    return valid, lo


def _dup_half(x, lo, kv):
    xr = pltpu.roll(x, SWA_DH, axis=1)
    return jnp.where(lo, x, xr) if kv == 0 else jnp.where(lo, xr, x)


def _swa_stack(ref, col0, lo, kv):
    parts = []
    for p in range(4):
        c0 = col0 + LANES * (4 * kv + p)
        xp = ref[:, c0:c0 + LANES]
        parts += [jnp.where(lo, xp, 0.0).astype(BF16), jnp.where(lo, 0.0, xp).astype(BF16)]
    return jnp.concatenate(parts, axis=0)


def _swa_unstack(x_all, lo, p):
    r0 = 2 * p * WINDOW
    return jnp.where(lo, x_all[r0:r0 + WINDOW, :], x_all[r0 + WINDOW:r0 + 2 * WINDOW, :])


def _swa_sinks(sink_ref, kv):
    return jnp.concatenate([jnp.full((WINDOW, 1), sink_ref[SWA_GROUP * kv + h], F32) for h in range(SWA_GROUP)],
                           axis=0)


def _swa_probs(qm, kdup, valid, sink):
    s = _dot(qm, kdup, NT) * (SWA_DH ** -0.5)
    s = jnp.where(valid, s, NEG)
    m = jnp.maximum(jnp.max(s, axis=-1, keepdims=True), sink)
    p = jnp.exp(s - m)
    es = jnp.exp(sink - m)
    inv = 1.0 / (jnp.sum(p, axis=-1, keepdims=True) + es)
    return p * inv, es * inv


def _swa_fwd(proj, sinks, dep=None):
    t = proj.shape[0]

    deps, dep_specs = _dep_operand(dep)

    def body(sink_ref, q_ref, kp_ref, kc_ref, vp_ref, vc_ref, *rest):
        o_ref = rest[-1]
        i = pl.program_id(0)
        valid, lo = _swa_masks(i, 1)
        kb = jnp.concatenate([kp_ref[...], kc_ref[...]], axis=0)
        vb = jnp.concatenate([vp_ref[...], vc_ref[...]], axis=0)
        for kv in range(2):
            kdup = _dup_half(kb, lo, kv).astype(BF16)
            vdup = _dup_half(vb, lo, kv).astype(BF16)
            for p in range(4):
                sl = slice(LANES * (4 * kv + p), LANES * (4 * kv + p + 1))
                qp = q_ref[:, sl]
                halves = []
                for e in range(2):
                    qm = jnp.where(lo if e == 0 else ~lo, qp, 0.0).astype(BF16)
                    pn, _ = _swa_probs(qm, kdup, valid, sink_ref[SWA_GROUP * kv + 2 * p + e])
                    halves.append(_dot(pn.astype(BF16), vdup, NN))
                o_ref[:, sl] = jnp.where(lo, halves[0], halves[1]).astype(BF16)

    kvspec = lambda col, prev: pl.BlockSpec(
        (WINDOW, LANES), (lambda i: (jnp.maximum(i - 1, 0), col)) if prev else (lambda i: (i, col)))
    return pl.pallas_call(
        body, name="swa_fwd", grid=(t // WINDOW,),
        in_specs=[pl.BlockSpec(memory_space=pltpu.SMEM),
                  pl.BlockSpec((WINDOW, D_SWA), lambda i: (i, C_QS // D_SWA)),
                  kvspec(C_KS // LANES, True), kvspec(C_KS // LANES, False),
                  kvspec(C_VS // LANES, True), kvspec(C_VS // LANES, False), *dep_specs],
        out_specs=pl.BlockSpec((WINDOW, D_SWA), lambda i: (i, 0)),
        out_shape=jax.ShapeDtypeStruct((t, D_SWA), BF16),
        compiler_params=_cparams(("parallel",)),
    )(sinks, proj, proj, proj, proj, proj, *deps)


def _mix_ln1(cat, w_out, x, ln_g, ln_b):
    t = cat.shape[0]
    tm, tk = min(512, t), 512

    def epilogue(acc_ref, extra, outs):
        x_ref, g_ref, b_ref = extra
        r1_ref, h1_ref, h1b_ref = outs
        for r0, n in _row_chunks(tm):
            rs = slice(r0, r0 + n)
            r1 = ALPHA * x_ref[rs, :] + acc_ref[rs, :]
            xhat, _ = _ln_stats(r1)
            h = xhat * g_ref[...] + b_ref[...]
            r1_ref[rs, :] = r1
            h1_ref[rs, :] = h
            h1b_ref[rs, :] = h.astype(BF16)

    row = pl.BlockSpec((tm, D_MODEL), lambda i, j, k: (i, 0))
    vec = pl.BlockSpec((1, D_MODEL), lambda i, j, k: (0, 0))
    return _matmul(
        "mix_ln1", cat, w_out, NN, (t // tm, 1, D_MODEL // tk),
        pl.BlockSpec((tm, tk), lambda i, j, k: (i, k)),
        pl.BlockSpec((tk, D_MODEL), lambda i, j, k: (k, 0)),
        [jax.ShapeDtypeStruct((t, D_MODEL), F32), jax.ShapeDtypeStruct((t, D_MODEL), F32),
         jax.ShapeDtypeStruct((t, D_MODEL), BF16)],
        [row, row, row], (tm, D_MODEL),
        extra=(x, ln_g, ln_b), extra_specs=(row, vec, vec), epilogue=epilogue)


def _mlp_up(h1b, w_up):
    t = h1b.shape[0]
    tm, tn = min(1024, t), 1024

    def epilogue(acc_ref, extra, outs):
        a_ref, hdn_ref = outs
        for r0, n in _row_chunks(tm, 256):
            rs = slice(r0, r0 + n)
            a = jnp.maximum(acc_ref[rs, :], 0.0)
            a_ref[rs, :] = a.astype(BF16)
            hdn_ref[rs, :] = (a * a).astype(BF16)

    out = pl.BlockSpec((tm, tn), lambda i, j, k: (i, j))
    return _matmul(
        "mlp_up", h1b, w_up, NN, (t // tm, D_FF // tn, 1),
        pl.BlockSpec((tm, D_MODEL), lambda i, j, k: (i, 0)),
        pl.BlockSpec((None, D_MODEL, tn), lambda i, j, k: (j, 0, 0)),
        [jax.ShapeDtypeStruct((t, D_FF), BF16), jax.ShapeDtypeStruct((t, D_FF), BF16)],
        [out, out], (tm, tn), epilogue=epilogue)


def _mlp_down_loss(hdn, w_down, h1, target, ln_g, ln_b):
    t = hdn.shape[0]
    tm, tn, tk = min(1024, t), 1024, 2048
    ff = _matmul(
        "mlp_down", hdn, w_down, NN, (t // tm, D_MODEL // tn, D_FF // tk),
        pl.BlockSpec((tm, tk), lambda i, j, k: (i, k)),
        pl.BlockSpec((tk, tn), lambda i, j, k: (k, j)),
        jax.ShapeDtypeStruct((t, D_MODEL), F32),
        pl.BlockSpec((tm, tn), lambda i, j, k: (i, j)), (tm, tn))
    r = min(256, t)

    def body(ff_ref, h1_ref, t_ref, g_ref, b_ref, dr2_ref, dr2b_ref, gg_ref, gb_ref, loss_ref):
        @pl.when(pl.program_id(0) == 0)
        def _():
            gg_ref[...] = jnp.zeros_like(gg_ref)
            gb_ref[...] = jnp.zeros_like(gb_ref)
            loss_ref[...] = jnp.zeros_like(loss_ref)

        for r0, n in _row_chunks(r, 64):
            rs = slice(r0, r0 + n)
            xhat, rstd = _ln_stats(ALPHA * h1_ref[rs, :] + ff_ref[rs, :])
            err = xhat * g_ref[...] + b_ref[...] - t_ref[rs, :]
            loss_ref[...] += 0.5 * jnp.sum(jnp.mean(err * err, axis=-1, keepdims=True))
            dy = err * (1.0 / D_MODEL)
            gg_ref[...] += jnp.sum(dy * xhat, axis=0, keepdims=True)
            gb_ref[...] += jnp.sum(dy, axis=0, keepdims=True)
            dr2 = _ln_bwd(dy * g_ref[...], xhat, rstd)
            dr2_ref[rs, :] = dr2
            dr2b_ref[rs, :] = dr2.astype(BF16)

    row = pl.BlockSpec((r, D_MODEL), lambda i: (i, 0))
    vec = pl.BlockSpec((1, D_MODEL), lambda i: (0, 0))
    return pl.pallas_call(
        body, name="ln2_loss", grid=(t // r,),
        in_specs=[row, row, row, vec, vec],
        out_specs=[row, row, vec, vec, pl.BlockSpec((1, LANES), lambda i: (0, 0))],
        out_shape=[jax.ShapeDtypeStruct((t, D_MODEL), F32), jax.ShapeDtypeStruct((t, D_MODEL), BF16),
                   jax.ShapeDtypeStruct((1, D_MODEL), F32), jax.ShapeDtypeStruct((1, D_MODEL), F32),
                   jax.ShapeDtypeStruct((1, LANES), F32)],
        compiler_params=_cparams(("arbitrary",)),
    )(ff, h1, target, ln_g, ln_b)


def _mlp_down_bwd(dr2b, w_down, a_act):
    t = dr2b.shape[0]
    tm, tn = min(1024, t), 1024

    def epilogue(acc_ref, extra, outs):
        (a_ref,) = extra
        for r0, n in _row_chunks(tm, 256):
            rs = slice(r0, r0 + n)
            outs[0][rs, :] = (acc_ref[rs, :] * (2.0 * a_ref[rs, :].astype(F32))).astype(BF16)

    blk = pl.BlockSpec((tm, tn), lambda i, j, k: (i, j))
    return _matmul(
        "mlp_down_bwd", dr2b, w_down, NT, (t // tm, D_FF // tn, 1),
        pl.BlockSpec((tm, D_MODEL), lambda i, j, k: (i, 0)),
        pl.BlockSpec((tn, D_MODEL), lambda i, j, k: (j, 0)),
        jax.ShapeDtypeStruct((t, D_FF), BF16), blk, (tm, tn),
        extra=(a_act,), extra_specs=(blk,), epilogue=epilogue)


def _grad_w_down(hdn, dr2b):
    t = hdn.shape[0]
    tm, tn = 1024, 1024
    return _matmul(
        "grad_w_down", hdn, dr2b, TN, (D_MODEL // tn, D_FF // tm, 1),
        pl.BlockSpec((t, tm), lambda j, i, k: (0, i)),
        pl.BlockSpec((t, tn), lambda j, i, k: (0, j)),
        jax.ShapeDtypeStruct((D_FF, D_MODEL), F32),
        pl.BlockSpec((tm, tn), lambda j, i, k: (i, j)), (tm, tn))


def _grad_w_up(h1b, du, dep=None):
    t = h1b.shape[0]
    tm, tn = 1024, 1024
    return _matmul(
        "grad_w_up", h1b, du, TN, (N_DEV, D_MODEL // tm, 1),
        pl.BlockSpec((t, tm), lambda i, j, k: (0, j)),
        pl.BlockSpec((t, tn), lambda i, j, k: (0, i)),
        jax.ShapeDtypeStruct((N_DEV, D_MODEL, D_FF // N_DEV), F32),
        pl.BlockSpec((None, tm, tn), lambda i, j, k: (i, j, 0)), (tm, tn), dep=dep)


def _mlp_up_bwd_ln1(du, w_up, dr2, r1, ln_g, dep=None):
    t = du.shape[0]
    tm, tn, tk = min(1024, t), 1024, D_FF // N_DEV
    deps, dep_specs = _dep_operand(dep)

    def mm_body(a_ref, b_ref, *rest):
        o_ref = rest[-1]
        k = pl.program_id(2)
        part = _dot(a_ref[:, :tk], b_ref[0], NT) + _dot(a_ref[:, tk:], b_ref[1], NT)

        @pl.when(k == 0)
        def _():
            o_ref[...] = part

        @pl.when(k > 0)
        def _():
            o_ref[...] += part

    dff = pl.pallas_call(
        mm_body, name="mlp_up_bwd", grid=(t // tm, D_MODEL // tn, N_DEV // 2),
        in_specs=[pl.BlockSpec((tm, 2 * tk), lambda i, j, k: (i, k)),
                  pl.BlockSpec((2, tn, tk), lambda i, j, k: (k, j, 0)), *dep_specs],
        out_specs=pl.BlockSpec((tm, tn), lambda i, j, k: (i, j)),
        out_shape=jax.ShapeDtypeStruct((t, D_MODEL), F32),
        compiler_params=_cparams(("parallel", "parallel", "arbitrary")),
    )(du, w_up, *deps)
    r = min(256, t)

    def body(acc_ref, dr2_ref, r1_ref, g_ref, dr1_ref, dr1b_ref, gg_ref, gb_ref):
        @pl.when(pl.program_id(0) == 0)
        def _():
            gg_ref[...] = jnp.zeros_like(gg_ref)
            gb_ref[...] = jnp.zeros_like(gb_ref)

        for r0, n in _row_chunks(r, 64):
            rs = slice(r0, r0 + n)
            dh1 = ALPHA * dr2_ref[rs, :] + acc_ref[rs, :]
            xhat, rstd = _ln_stats(r1_ref[rs, :])
            gg_ref[...] += jnp.sum(dh1 * xhat, axis=0, keepdims=True)
            gb_ref[...] += jnp.sum(dh1, axis=0, keepdims=True)
            dr1 = _ln_bwd(dh1 * g_ref[...], xhat, rstd)
            dr1_ref[rs, :] = dr1
            dr1b_ref[rs, :] = dr1.astype(BF16)

    row = pl.BlockSpec((r, D_MODEL), lambda i: (i, 0))
    vec = pl.BlockSpec((1, D_MODEL), lambda i: (0, 0))
    return pl.pallas_call(
        body, name="ln1_bwd", grid=(t // r,),
        in_specs=[row, row, row, vec],
        out_specs=[row, row, vec, vec],
        out_shape=[jax.ShapeDtypeStruct((t, D_MODEL), F32), jax.ShapeDtypeStruct((t, D_MODEL), BF16),
                   jax.ShapeDtypeStruct((1, D_MODEL), F32), jax.ShapeDtypeStruct((1, D_MODEL), F32)],
        compiler_params=_cparams(("arbitrary",)),
    )(dff, dr2, r1, ln_g)


def _dcat(dr1b, w_out):
    t = dr1b.shape[0]
    tm, tn = min(1024, t), 1024
    return _matmul(
        "dcat", dr1b, w_out, NT, (t // tm, D_MODEL // tn, 1),
        pl.BlockSpec((tm, D_MODEL), lambda i, j, k: (i, 0)),
        pl.BlockSpec((tn, D_MODEL), lambda i, j, k: (j, 0)),
        jax.ShapeDtypeStruct((t, D_MODEL), F32),
        pl.BlockSpec((tm, tn), lambda i, j, k: (i, j)), (tm, tn))


def _grad_w_out(cat, dr1b, dep=None):
    t = cat.shape[0]
    tm, tn = 1024, 1024
    return _matmul(
        "grad_w_out", cat, dr1b, TN, (D_MODEL // tm, D_MODEL // tn, 1),
        pl.BlockSpec((t, tm), lambda i, j, k: (0, i)),
        pl.BlockSpec((t, tn), lambda i, j, k: (0, j)),
        jax.ShapeDtypeStruct((D_MODEL, D_MODEL), F32),
        pl.BlockSpec((tm, tn), lambda i, j, k: (i, j)), (tm, tn), dep=dep)


def _gla_norm_bwd(dcat, o_raw, proj, norm_w, dep=None):
    t = o_raw.shape[0]
    r = min(512, t)

    deps, dep_specs = _dep_operand(dep)

    def body(d_ref, o_ref, g_ref, w_ref, *rest):
        do_ref, dg_ref, dw_ref = rest[len(deps):]

        @pl.when(pl.program_id(0) == 0)
        def _():
            dw_ref[...] = jnp.zeros_like(dw_ref)

        w = w_ref[...]
        dw = jnp.zeros((1, GLA_DV), F32)
        for h in range(GLA_HEADS):
            sl = slice(h * GLA_DV, (h + 1) * GLA_DV)
            o = o_ref[:, sl]
            g = g_ref[:, sl]
            d = d_ref[:, sl]
            rr = lax.rsqrt(jnp.mean(o * o, axis=-1, keepdims=True) + RMS_EPS)
            on = o * rr
            sg = jax.nn.sigmoid(g)
            sil = g * sg
            dg_ref[:, sl] = (d * on * w * (sg * (1.0 + g * (1.0 - sg)))).astype(BF16)
            dw = dw + jnp.sum(d * on * sil, axis=0, keepdims=True)
            don = d * w * sil
            do_ref[:, sl] = rr * (don - on * jnp.mean(don * on, axis=-1, keepdims=True))
        dw_ref[...] += dw

    return pl.pallas_call(
        body, name="gla_norm_bwd", grid=(t // r,),
        in_specs=[pl.BlockSpec((r, D_GLA), lambda i: (i, 0)),
                  pl.BlockSpec((r, D_GLA), lambda i: (i, 0)),
                  pl.BlockSpec((r, D_GLA), lambda i: (i, C_GG // D_GLA)),
                  pl.BlockSpec((1, GLA_DV), lambda i: (0, 0)), *dep_specs],
        out_specs=[pl.BlockSpec((r, D_GLA), lambda i: (i, 0)),
                   pl.BlockSpec((r, D_GLA), lambda i: (i, 0)),
                   pl.BlockSpec((1, GLA_DV), lambda i: (0, 0))],
        out_shape=[jax.ShapeDtypeStruct((t, D_GLA), F32), jax.ShapeDtypeStruct((t, D_GLA), BF16),
                   jax.ShapeDtypeStruct((1, GLA_DV), F32)],
        compiler_params=_cparams(("arbitrary",)),
    )(dcat, o_raw, proj, norm_w, *deps)


def _gla_bwd(proj, bloc, do_raw, states, triu):
    t = proj.shape[0]
    r = min(256, t)
    ncb = r // GC
    nb = t // r
    scale = GLA_DK ** -0.5

    def body(q_ref, k_ref, v_ref, b_ref, do_ref, st_ref, u_ref, dq_ref, dk_ref, dv_ref, dg_ref, ds_scr):
        @pl.when(pl.program_id(1) == 0)
        def _():
            ds_scr[...] = jnp.zeros_like(ds_scr)

        rows = lax.broadcasted_iota(jnp.int32, (GC, 1), 0)

        def chunk(cc, carry):
            c = ncb - 1 - cc
            r0 = pl.multiple_of(c * GC, GC)
            q = q_ref[pl.ds(r0, GC), :] * scale
            k = k_ref[pl.ds(r0, GC), :]
            v = v_ref[pl.ds(r0, GC), :]
            b = b_ref[pl.ds(r0, GC), :]
            do = do_ref[pl.ds(r0, GC), :]
            st = st_ref[c]
            dsn = ds_scr[...]
            bl = b[GC - 1:GC, :]
            eb = jnp.exp(b)
            ekl = jnp.exp(bl - b)
            ebl = jnp.exp(bl)
            qh = q * eb
            kd = k * ekl
            dob = do.astype(BF16)
            dsb = dsn.astype(BF16)
            dqh = _dot(dob, st.astype(BF16), NN)
            dkd = _dot(v.astype(BF16), dsb, NN)
            dv = _dot(kd.astype(BF16), dsb, NT)
            dq_i = jnp.zeros((GC, GLA_DK), F32)
            dk_i = jnp.zeros((GC, GLA_DK), F32)
            dk_x = jnp.zeros((GC, GLA_DK), F32)
            kr = _bf16_round(k)
            vr = _bf16_round(v)
            dor = _bf16_round(do)
            for j in range(GC):
                w = jnp.exp(jnp.where(rows >= j, b - b[j:j + 1, :], NEG))
                qw = q * w
                qwr = _bf16_round(qw)
                a = _bf16_round(jnp.sum(qwr * kr[j:j + 1, :], axis=-1, keepdims=True))
                da = jnp.sum(dor * vr[j:j + 1, :], axis=-1, keepdims=True)
                dq_i = dq_i + da * (w * k[j:j + 1, :])
                dk_x = jnp.where(rows == j, jnp.sum(da * qw, axis=0, keepdims=True), dk_x)
                dk_i = jnp.where(rows == j, jnp.sum(_bf16_round(da) * qwr, axis=0, keepdims=True), dk_i)
                dv = dv + jnp.where(rows == j, jnp.sum(a * dor, axis=0, keepdims=True), 0.0)
            dqs = dqh * eb + dq_i
            dk = dkd * ekl + dk_i
            db_last = jnp.sum(dkd * kd, axis=0, keepdims=True) + ebl * jnp.sum(dsn * st, axis=0, keepdims=True)
            db = q * dqs - k * (dkd * ekl + dk_x) + jnp.where(rows == GC - 1, db_last, 0.0)
            dq_ref[pl.ds(r0, GC), :] = (dqs * scale).astype(BF16)
            dk_ref[pl.ds(r0, GC), :] = dk.astype(BF16)
            dv_ref[pl.ds(r0, GC), :] = dv.astype(BF16)
            dg_ref[pl.ds(r0, GC), :] = _dot(u_ref[...], db, NN, precision=lax.Precision.HIGHEST)
            ds_scr[...] = dsn * ebl + _dot(dob, qh.astype(BF16), TN)
            return carry

        lax.fori_loop(0, ncb, chunk, 0)

    rev = lambda i: nb - 1 - i
    return pl.pallas_call(
        body, name="gla_bwd", grid=(GLA_HEADS, nb),
        in_specs=[pl.BlockSpec((r, GLA_DK), lambda h, i: (rev(i), C_QG // GLA_DK + h)),
                  pl.BlockSpec((r, GLA_DK), lambda h, i: (rev(i), C_KG // GLA_DK + h)),
                  pl.BlockSpec((r, GLA_DV), lambda h, i: (rev(i), C_VG // GLA_DV + h)),
                  pl.BlockSpec((r, GLA_DK), lambda h, i: (rev(i), h)),
                  pl.BlockSpec((r, GLA_DV), lambda h, i: (rev(i), h)),
                  pl.BlockSpec((ncb, None, GLA_DV, GLA_DK), lambda h, i: (rev(i), h, 0, 0)),
                  pl.BlockSpec((GC, GC), lambda h, i: (0, 0))],
        out_specs=[pl.BlockSpec((r, GLA_DK), lambda h, i: (rev(i), h)),
                   pl.BlockSpec((r, GLA_DK), lambda h, i: (rev(i), h)),
                   pl.BlockSpec((r, GLA_DV), lambda h, i: (rev(i), h)),
                   pl.BlockSpec((r, GLA_DK), lambda h, i: (rev(i), h))],
        out_shape=[jax.ShapeDtypeStruct((t, GLA_KW), BF16), jax.ShapeDtypeStruct((t, GLA_KW), BF16),
                   jax.ShapeDtypeStruct((t, D_GLA), BF16), jax.ShapeDtypeStruct((t, GLA_KW), F32)],
        scratch_shapes=[pltpu.VMEM((GLA_DV, GLA_DK), F32)],
        compiler_params=_cparams(("parallel", "arbitrary")),
    )(proj, proj, proj, bloc, do_raw, states, triu)


def _gate_bwd(dg, dgdz, proj, w2p):
    t = dg.shape[0]
    r = min(512, t)

    def body(dg_ref, s_ref, lo_ref, w_ref, dlo_ref, gw_ref, gb_ref):
        @pl.when(pl.program_id(0) == 0)
        def _():
            gw_ref[...] = jnp.zeros_like(gw_ref)
            gb_ref[...] = jnp.zeros_like(gb_ref)

        dz = dg_ref[...] * s_ref[...]
        dzb = dz.astype(BF16)
        gb_ref[...] += jnp.sum(dz, axis=0, keepdims=True)
        gw_ref[...] += _dot(lo_ref[...].astype(BF16), dzb, TN)
        dlo_ref[...] = _dot(dzb, w_ref[...], NT).astype(BF16)

    return pl.pallas_call(
        body, name="gate_bwd", grid=(t // r,),
        in_specs=[pl.BlockSpec((r, GLA_KW), lambda i: (i, 0)),
                  pl.BlockSpec((r, GLA_KW), lambda i: (i, 0)),
                  pl.BlockSpec((r, LANES), lambda i: (i, C_LO // LANES)),
                  pl.BlockSpec((LANES, GLA_KW), lambda i: (0, 0))],
        out_specs=[pl.BlockSpec((r, LANES), lambda i: (i, 0)),
                   pl.BlockSpec((LANES, GLA_KW), lambda i: (0, 0)),
                   pl.BlockSpec((1, GLA_KW), lambda i: (0, 0))],
        out_shape=[jax.ShapeDtypeStruct((t, LANES), BF16), jax.ShapeDtypeStruct((LANES, GLA_KW), F32),
                   jax.ShapeDtypeStruct((1, GLA_KW), F32)],
        compiler_params=_cparams(("arbitrary",)),
    )(dg, dgdz, proj, w2p)


def _swa_bwd(proj, dcat, sinks, dep=None):
    t = proj.shape[0]

    deps, dep_specs = _dep_operand(dep)

    def body(sink_ref, q_ref, kp_ref, kc_ref, vp_ref, vc_ref, d_ref, *rest):
        dq_ref, dk_ref, dv_ref, dsink_ref = rest[len(deps):]
        i = pl.program_id(0)

        @pl.when(i == 0)
        def _():
            dk_ref[...] = jnp.zeros_like(dk_ref)
            dv_ref[...] = jnp.zeros_like(dv_ref)
            dsink_ref[...] = jnp.zeros_like(dsink_ref)

        valid, lo = _swa_masks(i, SWA_GROUP)
        lane = lax.broadcasted_iota(jnp.int32, (1, LANES), 1)
        kb = jnp.concatenate([kp_ref[...], kc_ref[...]], axis=0)
        vb = jnp.concatenate([vp_ref[...], vc_ref[...]], axis=0)
        dsink = jnp.zeros((1, LANES), F32)
        folded_k, folded_v = [], []
        for kv in range(2):
            kdup = _dup_half(kb, lo, kv).astype(BF16)
            vdup = _dup_half(vb, lo, kv).astype(BF16)
            qm = _swa_stack(q_ref, 0, lo, kv)
            dom = _swa_stack(d_ref, 0, lo, kv)
            pn, psink = _swa_probs(qm, kdup, valid, _swa_sinks(sink_ref, kv))
            dpr = _dot(dom, vdup, NT)
            drow = jnp.sum(dpr * pn, axis=-1, keepdims=True)
            ds_col = psink * drow
            for h in range(SWA_GROUP):
                dsink = dsink + jnp.where(lane == SWA_GROUP * kv + h,
                                          -jnp.sum(ds_col[h * WINDOW:(h + 1) * WINDOW, :]), 0.0)
            dsb = (pn * (dpr - drow) * (SWA_DH ** -0.5)).astype(BF16)
            dq_all = _dot(dsb, kdup, NN)
            for p in range(4):
                c0 = LANES * (4 * kv + p)
                dq_ref[:, c0:c0 + LANES] = _swa_unstack(dq_all, lo, p).astype(BF16)
            dkd = _dot(dsb, qm, TN)
            dvd = _dot(pn.astype(BF16), dom, TN)
            folded_k.append(dkd + pltpu.roll(dkd, SWA_DH, axis=1))
            folded_v.append(dvd + pltpu.roll(dvd, SWA_DH, axis=1))
        dkb = jnp.where(lo, folded_k[0], folded_k[1])
        dvb = jnp.where(lo, folded_v[0], folded_v[1])
        dsink_ref[...] += dsink
        cur = pl.ds(pl.multiple_of(i * WINDOW, WINDOW), WINDOW)
        dk_ref[cur, :] += dkb[WINDOW:, :]
        dv_ref[cur, :] += dvb[WINDOW:, :]

        @pl.when(i > 0)
        def _():
            prev = pl.ds(pl.multiple_of((i - 1) * WINDOW, WINDOW), WINDOW)
            dk_ref[prev, :] += dkb[:WINDOW, :]
            dv_ref[prev, :] += dvb[:WINDOW, :]

    kvspec = lambda col, prev: pl.BlockSpec(
        (WINDOW, LANES), (lambda i: (jnp.maximum(i - 1, 0), col)) if prev else (lambda i: (i, col)))
    full = pl.BlockSpec((t, LANES), lambda i: (0, 0))
    return pl.pallas_call(
        body, name="swa_bwd", grid=(t // WINDOW,),
        in_specs=[pl.BlockSpec(memory_space=pltpu.SMEM),
                  pl.BlockSpec((WINDOW, D_SWA), lambda i: (i, C_QS // D_SWA)),
                  kvspec(C_KS // LANES, True), kvspec(C_KS // LANES, False),
                  kvspec(C_VS // LANES, True), kvspec(C_VS // LANES, False),
                  pl.BlockSpec((WINDOW, D_SWA), lambda i: (i, 1)), *dep_specs],
        out_specs=[pl.BlockSpec((WINDOW, D_SWA), lambda i: (i, 0)), full, full,
                   pl.BlockSpec((1, LANES), lambda i: (0, 0))],
        out_shape=[jax.ShapeDtypeStruct((t, D_SWA), BF16), jax.ShapeDtypeStruct((t, LANES), F32),
                   jax.ShapeDtypeStruct((t, LANES), F32), jax.ShapeDtypeStruct((1, LANES), F32)],
        compiler_params=_cparams(("arbitrary",)),
    )(sinks, proj, proj, proj, proj, proj, dcat, *deps)


def _grad_w_in(xb, dproj):
    t = xb.shape[0]
    tm, tn = 1024, 640
    return _matmul(
        "grad_w_in", xb, dproj, TN, (D_MODEL // tm, D_INP // tn, 1),
        pl.BlockSpec((t, tm), lambda i, j, k: (0, i)),
        pl.BlockSpec((t, tn), lambda i, j, k: (0, j)),
        jax.ShapeDtypeStruct((D_MODEL, D_INP), F32),
        pl.BlockSpec((tm, tn), lambda i, j, k: (i, j)), (tm, tn))


def _grad_x(dproj, w_in_p, dr1, dep=None):
    t = dproj.shape[0]
    tm, tn = min(512, t), 1024

    def epilogue(acc_ref, extra, outs):
        for r0, n in _row_chunks(tm):
            rs = slice(r0, r0 + n)
            outs[0][rs, :] = ALPHA * extra[0][rs, :] + acc_ref[rs, :]

    blk = pl.BlockSpec((tm, tn), lambda j, i, k: (i, j))
    return _matmul(
        "grad_x", dproj, w_in_p, NT, (D_MODEL // tn, t // tm, 1),
        pl.BlockSpec((tm, D_INP), lambda j, i, k: (i, 0)),
        pl.BlockSpec((tn, D_INP), lambda j, i, k: (j, 0)),
        jax.ShapeDtypeStruct((t, D_MODEL), F32), blk, (tm, tn),
        extra=(dr1,), extra_specs=(blk,), epilogue=epilogue, dep=dep)


def _place():
    x, y, c = lax.axis_index("x"), lax.axis_index("y"), lax.axis_index("c")
    chips = [(1 - x, y), (x, 1 - y), (1 - x, 1 - y)]
    return x, y, c, chips


def _all_gather(shards):
    n = len(shards)
    hbm = pl.BlockSpec(memory_space=pl.ANY)

    def body(*refs):
        ins, outs = refs[:n], refs[n:2 * n]
        send, recv, loc = refs[2 * n:]
        x, y, c, chips = _place()
        me, sib = (x, y, c), (x, y, 1 - c)

        def slot(t, px, py, pc):
            return outs[t].at[4 * px + 2 * py + pc]

        def copy(t, k, block, to, src=None):
            return pltpu.make_async_remote_copy(
                src_ref=slot(t, *block) if src is None else src, dst_ref=slot(t, *block),
                send_sem=send.at[7 * t + k], recv_sem=recv.at[7 * t + k], device_id=to, device_id_type=MESH)

        mine = [pltpu.make_async_copy(ins[t], slot(t, *me), loc.at[t]) for t in range(n)]
        for cp in mine:
            cp.start()
        sent = []
        for t in range(n):
            sent.append(copy(t, 0, me, sib, src=ins[t]))
            sent += [copy(t, 1 + j, me, (*chip, c), src=ins[t]) for j, chip in enumerate(chips)]
        for cp in sent:
            cp.start()
        for t in range(n):
            for j, chip in enumerate(chips):
                copy(t, 1 + j, (*chip, c), me).wait_recv()
                fwd = copy(t, 4 + j, (*chip, c), sib)
                fwd.start()
                sent.append(fwd)
        for t in range(n):
            copy(t, 0, sib, me).wait_recv()
            for j, chip in enumerate(chips):
                copy(t, 4 + j, (*chip, 1 - c), me).wait_recv()
        for cp in sent:
            cp.wait_send()
        for cp in mine:
            cp.wait()

    return pl.pallas_call(
        body, name="all_gather_weights",
        in_specs=[hbm] * n, out_specs=[hbm] * n,
        out_shape=[jax.ShapeDtypeStruct((N_DEV,) + s.shape, s.dtype) for s in shards],
        scratch_shapes=[pltpu.SemaphoreType.DMA((7 * n,)), pltpu.SemaphoreType.DMA((7 * n,)),
                        pltpu.SemaphoreType.DMA((n,))],
    )(*shards)


def _plan_gather_out(src, land, x, y, c, chips):
    me = 4 * x + 2 * y + c
    return [(src, land.at[me], to) for to in [(x, y, 1 - c)] + [(px, py, c) for px, py in chips]]


def _plan_gather_forward(src, land, x, y, c, chips):
    return [(land.at[4 * px + 2 * py + c], land.at[4 * px + 2 * py + c], (x, y, 1 - c)) for px, py in chips]


def _plan_sibling(src, land, x, y, c, chips):
    return [(src.at[2 * q + (1 - c)], land.at[q], (x, y, 1 - c)) for q in range(4)]


def _plan_chips(src, land, x, y, c, chips):
    return [(src.at[j], land.at[j], (px, py, c)) for j, (px, py) in enumerate(chips)]


_PLAN_COPIES = {_plan_gather_out: 4, _plan_gather_forward: 3, _plan_sibling: 4, _plan_chips: 3}
_HBM = pl.BlockSpec(memory_space=pltpu.HBM)
_SEM = pl.BlockSpec(memory_space=pltpu.SEMAPHORE)
_EFFECT = pltpu.SideEffectType.DATAFLOW_SIDE_EFFECTING


def _hbm(a):
    return pltpu.with_memory_space_constraint(a, pltpu.HBM)


def _plan_descriptors(plans, srcs, lands, send, recv):
    x, y, c, chips = _place()
    cps = []
    for plan, src, land in zip(plans, srcs, lands):
        for s_ref, d_ref, to in plan(src, land, x, y, c, chips):
            k = len(cps)
            cps.append(pltpu.make_async_remote_copy(src_ref=s_ref, dst_ref=d_ref, send_sem=send.at[k],
                                                    recv_sem=recv.at[k], device_id=to, device_id_type=MESH))
    return cps


def _copies_start(name, plans, srcs, lands, after):
    has_src = [s is not None for s in srcs]
    arrays = [s for s in srcs if s is not None] + list(lands)
    n_src = sum(has_src)
    n_cp = sum(_PLAN_COPIES[p] for p in plans)

    def body(*refs):
        ins = refs[:len(arrays)]
        send, recv = refs[len(arrays) + 1], refs[len(arrays) + 2]
        token = refs[-1]
        it = iter(ins[:n_src])
        src_refs = [next(it) if h else None for h in has_src]
        for cp in _plan_descriptors(plans, src_refs, ins[n_src:], send, recv):
            cp.start()
        token[...] = jnp.zeros_like(token)

    outs = pl.pallas_call(
        body, name=name,
        in_specs=[_HBM] * len(arrays) + [pl.BlockSpec(memory_space=pl.ANY)],
        out_specs=(_SEM, _SEM, *[_HBM] * len(arrays), pl.BlockSpec(memory_space=pltpu.VMEM)),
        out_shape=(pltpu.SemaphoreType.DMA((n_cp,)), pltpu.SemaphoreType.DMA((n_cp,)),
                   *[pltpu.HBM(a.shape, a.dtype) for a in arrays], jax.ShapeDtypeStruct((8, LANES), F32)),
        input_output_aliases={i: 2 + i for i in range(len(arrays))},
        compiler_params=pltpu.CompilerParams(has_side_effects=_EFFECT),
    )(*[_hbm(a) for a in arrays], after)
    send, recv = outs[0], outs[1]
    thru = list(outs[2:-1])
    it = iter(thru[:n_src])
    return send, recv, [next(it) if h else None for h in has_src], thru[n_src:], outs[-1]


def _copies_wait(name, plans, started, after):
    send, recv, srcs, lands, _ = started
    has_src = [s is not None for s in srcs]
    arrays = [s for s in srcs if s is not None] + list(lands)
    n_src = sum(has_src)

    def body(*refs):
        ins = refs[:len(arrays)]
        send_ref, recv_ref = refs[len(arrays)], refs[len(arrays) + 1]
        it = iter(ins[:n_src])
        src_refs = [next(it) if h else None for h in has_src]
        for cp in _plan_descriptors(plans, src_refs, ins[n_src:], send_ref, recv_ref):
            cp.wait_send()
            cp.wait_recv()

    outs = pl.pallas_call(
        body, name=name,
        in_specs=[_HBM] * len(arrays) + [_SEM, _SEM, pl.BlockSpec(memory_space=pl.ANY)],
        out_specs=tuple([_HBM] * len(arrays)),
        out_shape=tuple(pltpu.HBM(a.shape, a.dtype) for a in arrays),
        input_output_aliases={i: i for i in range(len(arrays))},
        compiler_params=pltpu.CompilerParams(has_side_effects=_EFFECT),
    )(*arrays, send, recv, after)
    return list(outs[:n_src]), list(outs[n_src:])


def _pair_sum(name, grad, from_sibling, blocks):
    _, rows, cols = grad.shape
    tr = 256

    def body(blk_ref, g_ref, s_ref, o_ref):
        o_ref[...] = (g_ref[...] + s_ref[...]).astype(BF16)

    return pl.pallas_call(
        body, name=name,
        grid_spec=pltpu.PrefetchScalarGridSpec(
            num_scalar_prefetch=1, grid=(3, rows // tr),
            in_specs=[pl.BlockSpec((None, tr, cols), lambda j, r, br: (br[j], r, 0)),
                      pl.BlockSpec((None, tr, cols), lambda j, r, br: (br[3 + j], r, 0))],
            out_specs=pl.BlockSpec((None, tr, cols), lambda j, r, br: (j, r, 0))),
        out_shape=jax.ShapeDtypeStruct((3, rows, cols), BF16),
        compiler_params=_cparams(("parallel", "parallel")),
    )(blocks, grad, from_sibling)


def _adam_math(g, w, m, v):
    m2 = ADAM_B1 * m + (1.0 - ADAM_B1) * g
    v2 = ADAM_B2 * v + (1.0 - ADAM_B2) * (g * g)
    m_hat = m2 / (1.0 - ADAM_B1 ** ADAM_STEP)
    v_hat = v2 / (1.0 - ADAM_B2 ** ADAM_STEP)
    delta = -ADAM_LR * (m_hat / (jnp.sqrt(v_hat) + ADAM_EPS) + ADAM_WD * w)
    return delta, m2, v2


def _sum_adam(name, grad, from_sibling, from_chips, own, w, m, v):
    rows, cols = w.shape
    tr = 128

    def body(own_ref, p_ref, s_ref, r_ref, w_ref, m_ref, v_ref, g_out, d_out, m_out, v_out):
        g = p_ref[...] + s_ref[...]
        for j in range(3):
            g = g + r_ref[j].astype(F32)
        d, m2, v2 = _adam_math(g, w_ref[...], m_ref[...], v_ref[...])
        g_out[...] = g
        d_out[...] = d
        m_out[...] = m2
        v_out[...] = v2

    blk = pl.BlockSpec((tr, cols), lambda r, cr: (r, 0))
    shp = jax.ShapeDtypeStruct((rows, cols), F32)
    return pl.pallas_call(
        body, name=name,
        grid_spec=pltpu.PrefetchScalarGridSpec(
            num_scalar_prefetch=1, grid=(rows // tr,),
            in_specs=[pl.BlockSpec((None, tr, cols), lambda r, cr: (cr[0], r, 0)),
                      pl.BlockSpec((None, tr, cols), lambda r, cr: (cr[1], r, 0)),
                      pl.BlockSpec((3, tr, cols), lambda r, cr: (0, r, 0)),
                      blk, blk, blk],
            out_specs=[blk, blk, blk, blk]),
        out_shape=[shp, shp, shp, shp],
        compiler_params=_cparams(("parallel",)),
    )(own, grad, from_sibling, from_chips, w, m, v)


def _adam_small(name, g, w, m, v):
    def body(g_ref, w_ref, m_ref, v_ref, d_out, m_out, v_out):
        d, m2, v2 = _adam_math(g_ref[...], w_ref[...], m_ref[...], v_ref[...])
        d_out[...] = d
        m_out[...] = m2
        v_out[...] = v2

    shp = jax.ShapeDtypeStruct(w.shape, F32)
    return pl.pallas_call(body, name=name, out_shape=[shp, shp, shp])(g, w, m, v)


def _all_reduce_small(pack):
    rows = pack.shape[0]

    def body(in_ref, out_ref, slots, send, recv):
        x, y, c, _ = _place()
        me = 4 * x + 2 * y + c
        slots[me] = in_ref[...]
        cps = []
        for k in range(1, N_DEV):
            dx, dy, dc = (k >> 2) & 1, (k >> 1) & 1, k & 1
            to = (jnp.bitwise_xor(x, dx), jnp.bitwise_xor(y, dy), jnp.bitwise_xor(c, dc))
            cps.append(pltpu.make_async_remote_copy(
                src_ref=in_ref, dst_ref=slots.at[me], send_sem=send.at[k - 1], recv_sem=recv.at[k - 1],
                device_id=to, device_id_type=MESH))
        for cp in cps:
            cp.start()
        for cp in cps:
            cp.wait()
        acc = slots[0]
        for d in range(1, N_DEV):
            acc = acc + slots[d]
        out_ref[...] = acc

    return pl.pallas_call(
        body, name="all_reduce_small",
        in_specs=[pl.BlockSpec(memory_space=pltpu.VMEM)],
        out_specs=pl.BlockSpec(memory_space=pltpu.VMEM),
        out_shape=jax.ShapeDtypeStruct((rows, LANES), F32),
        scratch_shapes=[pltpu.VMEM((N_DEV, rows, LANES), F32),
                        pltpu.SemaphoreType.DMA((N_DEV - 1,)), pltpu.SemaphoreType.DMA((N_DEV - 1,))],
    )(pack)


def _rows128(a):
    flat = a.reshape(-1)
    padn = (-flat.shape[0]) % (8 * LANES)
    if padn:
        flat = jnp.concatenate([flat, jnp.zeros((padn,), flat.dtype)])
    return flat.reshape(-1, LANES)


def kernel(x, w_in, w_gk2, b_gk, gla_norm_w, swa_sinks, w_out, ln1_g, ln1_b, w_up, w_down, ln2_g, ln2_b, loss_target, m_w_in, m_w_gk2, m_b_gk, m_gla_norm_w, m_swa_sinks, m_w_out, m_ln1_g, m_ln1_b, m_w_up, m_w_down, m_ln2_g, m_ln2_b, v_w_in, v_w_gk2, v_b_gk, v_gla_norm_w, v_swa_sinks, v_w_out, v_ln1_g, v_ln1_b, v_w_up, v_w_down, v_ln2_g, v_ln2_b):
    xc, yc, cc = lax.axis_index("x"), lax.axis_index("y"), lax.axis_index("c")
    me = 4 * xc + 2 * yc + cc

    x2 = x[0]
    t = x2.shape[0]
    xb = x2.astype(BF16)
    target = loss_target[0]

    first = _all_gather([w_in[0].astype(BF16), w_gk2[0].astype(BF16)])
    w_in_p = _to_padded_cols(first[0].transpose(1, 0, 2).reshape(D_MODEL, D_IN))
    w2 = first[1].transpose(1, 0, 2).reshape(GATE_RANK, GLA_KW)
    w2p = jnp.concatenate([w2, jnp.zeros((LANES - GATE_RANK, GLA_KW), BF16)], axis=0)
    shards = [w_out[0].astype(BF16), w_up[0].astype(BF16), w_down[0].astype(BF16)]
    lands = [lax.dynamic_update_index_in_dim(lax.empty((N_DEV,) + sh.shape, BF16), sh, me, 0) for sh in shards]
    out_plans = [_plan_gather_out] * 3
    fwd_plans = [_plan_gather_forward] * 3
    ag1 = _copies_start("gather_out_start", out_plans, shards, lands, first[0])

    proj = _proj(xb, w_in_p, dep=ag1[4])
    rt = min(256, t)
    ii = jnp.arange(rt)
    tri = ((ii[:, None] // GC == ii[None, :] // GC) & (ii[None, :] <= ii[:, None])).astype(F32)
    bloc, dgdz = _gate_fwd(proj, w2p, b_gk, tri)
    o_raw, states = _gla_fwd(proj, bloc)
    _, lands = _copies_wait("gather_out_wait", out_plans, ag1, o_raw)
    ag2 = _copies_start("gather_forward_start", fwd_plans, [None] * 3, lands, o_raw)
    gla_out = _gla_norm_fwd(o_raw, proj, gla_norm_w)
    swa_out = _swa_fwd(proj, swa_sinks[0], dep=ag2[4])
    cat = jnp.concatenate([gla_out, swa_out], axis=-1)
    _, gathered = _copies_wait("gather_forward_wait", fwd_plans, ag2, swa_out)
    w_out_f = gathered[0].reshape(D_MODEL, D_MODEL)
    w_up_f = gathered[1]
    w_down_f = gathered[2].reshape(D_FF, D_MODEL)
    r1, h1, h1b = _mix_ln1(cat, w_out_f, x2, ln1_g, ln1_b)
    a_act, hdn = _mlp_up(h1b, w_up_f)
    dr2, dr2b, g_ln2_g, g_ln2_b, loss_part = _mlp_down_loss(hdn, w_down_f, h1, target, ln2_g, ln2_b)

    others = [2 * (1 - xc) + yc, 2 * xc + (1 - yc), 2 * (1 - xc) + (1 - yc)]
    blocks = jnp.stack([2 * q + cc for q in others] + others).astype(jnp.int32)
    own = jnp.stack([me, 2 * xc + yc]).astype(jnp.int32)
    wmv = dict(w_in=(w_in, m_w_in, v_w_in), w_out=(w_out, m_w_out, v_w_out), w_up=(w_up, m_w_up, v_w_up),
               w_down=(w_down, m_w_down, v_w_down))
    big = {}

    def sib_land(g):
        return lax.empty((4,) + g.shape[1:], F32)

    def chip_land(g):
        return lax.empty((3,) + g.shape[1:], BF16)

    def finish(nm, g, from_sib, from_chips):
        w, m, v = wmv[nm]
        big[nm] = [o[None] for o in _sum_adam("sum_adam_" + nm, g, from_sib, from_chips, own, w[0], m[0], v[0])]

    du = _mlp_down_bwd(dr2b, w_down_f, a_act)
    g_down = _grad_w_down(hdn, dr2b).reshape(N_DEV, D_FF // N_DEV, D_MODEL)
    sa_down = _copies_start("sibling_down_start", [_plan_sibling], [g_down], [sib_land(g_down)], g_down)
    g_up = _grad_w_up(h1b, du, dep=sa_down[4])
    (g_down,), (fs_down,) = _copies_wait("sibling_down_wait", [_plan_sibling], sa_down, g_up)
    p_down = _pair_sum("pair_sum_w_down", g_down, fs_down, blocks)
    sb_down = _copies_start("chips_down_sibling_up_start", [_plan_chips, _plan_sibling], [p_down, g_up],
                            [chip_land(g_down), sib_land(g_up)], p_down)
    dr1, dr1b, g_ln1_g, g_ln1_b = _mlp_up_bwd_ln1(du, w_up_f, dr2, r1, ln1_g, dep=sb_down[4])
    dcat = _dcat(dr1b, w_out_f)
    (_, g_up), (fc_down, fs_up) = _copies_wait("chips_down_sibling_up_wait", [_plan_chips, _plan_sibling], sb_down,
                                               dcat)
    finish("w_down", g_down, fs_down, fc_down)
    p_up = _pair_sum("pair_sum_w_up", g_up, fs_up, blocks)
    sb_up = _copies_start("chips_up_start", [_plan_chips], [p_up], [chip_land(g_up)], p_up)
    do_raw, dg_g, g_norm_w = _gla_norm_bwd(dcat, o_raw, proj, gla_norm_w, dep=sb_up[4])
    jj = jnp.arange(GC)
    triu = (jj[None, :] >= jj[:, None]).astype(F32)
    dq_g, dk_g, dv_g, dgk = _gla_bwd(proj, bloc, do_raw, states, triu)
    _, (fc_up,) = _copies_wait("chips_up_wait", [_plan_chips], sb_up, dgk)
    finish("w_up", g_up, fs_up, fc_up)
    dlo, gw2_p, g_b_gk = _gate_bwd(dgk, dgdz, proj, w2p)
    dq_s, dk_s, dv_s, g_sinks = _swa_bwd(proj, dcat, swa_sinks[0])
    dproj = jnp.concatenate([dq_g, dk_g, dv_g, dg_g, dq_s, dk_s.astype(BF16), dv_s.astype(BF16), dlo], axis=-1)
    gw_in_p = _grad_w_in(xb, dproj)
    g_in = _from_padded_cols(gw_in_p).reshape(D_MODEL, N_DEV, D_IN // N_DEV).transpose(1, 0, 2)
    sa_in = _copies_start("sibling_in_start", [_plan_sibling], [g_in], [sib_land(g_in)], g_in)
    g_out = _grad_w_out(cat, dr1b, dep=sa_in[4]).reshape(N_DEV, D_MODEL // N_DEV, D_MODEL)
    (g_in,), (fs_in,) = _copies_wait("sibling_in_wait", [_plan_sibling], sa_in, g_out)
    p_in = _pair_sum("pair_sum_w_in", g_in, fs_in, blocks)
    sb_in = _copies_start("chips_in_sibling_out_start", [_plan_chips, _plan_sibling], [p_in, g_out],
                          [chip_land(g_in), sib_land(g_out)], p_in)
    grad_x = _grad_x(dproj, w_in_p, dr1, dep=sb_in[4])
    (_, g_out), (fc_in, fs_out) = _copies_wait("chips_in_sibling_out_wait", [_plan_chips, _plan_sibling], sb_in,
                                               grad_x)
    finish("w_in", g_in, fs_in, fc_in)
    p_out = _pair_sum("pair_sum_w_out", g_out, fs_out, blocks)
    sb_out = _copies_start("chips_out_start", [_plan_chips], [p_out], [chip_land(g_out)], p_out)
    _, (fc_out,) = _copies_wait("chips_out_wait", [_plan_chips], sb_out, p_out)
    finish("w_out", g_out, fs_out, fc_out)

    pieces = [loss_part, g_b_gk, g_norm_w, g_sinks[:, :SWA_HEADS], g_ln1_g, g_ln1_b, g_ln2_g, g_ln2_b,
              gw2_p[:GATE_RANK]]
    pack = jnp.concatenate([_rows128(p) for p in pieces], axis=0)
    tot = _all_reduce_small(pack)
    sizes = [p.size for p in pieces]
    offs = [0]
    for p in pieces:
        offs.append(offs[-1] + _rows128(p).shape[0])
    unpack = lambda i, shape: tot[offs[i]:offs[i + 1]].reshape(-1)[:sizes[i]].reshape(shape)
    loss = tot[0, 0]
    small_names = ["b_gk", "gla_norm_w", "swa_sinks", "ln1_g", "ln1_b", "ln2_g", "ln2_b"]
    small_w = dict(b_gk=(b_gk, m_b_gk, v_b_gk), gla_norm_w=(gla_norm_w, m_gla_norm_w, v_gla_norm_w),
                   swa_sinks=(swa_sinks, m_swa_sinks, v_swa_sinks), ln1_g=(ln1_g, m_ln1_g, v_ln1_g),
                   ln1_b=(ln1_b, m_ln1_b, v_ln1_b), ln2_g=(ln2_g, m_ln2_g, v_ln2_g),
                   ln2_b=(ln2_b, m_ln2_b, v_ln2_b))
    small_g = {nm: unpack(1 + i, small_w[nm][0].shape) for i, nm in enumerate(small_names)}
    g_pack = jnp.concatenate([_rows128(small_g[nm]) for nm in small_names], axis=0)
    wmv = [jnp.concatenate([_rows128(small_w[nm][k]) for nm in small_names], axis=0) for k in range(3)]
    small_out = _adam_small("adam_replicated", g_pack, *wmv)
    srow = [0]
    for nm in small_names:
        srow.append(srow[-1] + _rows128(small_w[nm][0]).shape[0])
    small = {}
    for i, nm in enumerate(small_names):
        shape = small_w[nm][0].shape
        n = small_w[nm][0].size
        small[nm] = [small_g[nm]] + [o[srow[i]:srow[i + 1]].reshape(-1)[:n].reshape(shape) for o in small_out]

    gw2_full = unpack(8, (GATE_RANK, GLA_KW))
    gw2_loc = lax.dynamic_slice_in_dim(gw2_full, me * (GLA_KW // N_DEV), GLA_KW // N_DEV, axis=1)
    gk2_out = _adam_small("adam_w_gk2", gw2_loc, w_gk2[0], m_w_gk2[0], v_w_gk2[0])
    big["w_gk2"] = [gw2_loc[None]] + [o[None] for o in gk2_out]

    order = ["w_in", "w_gk2", "b_gk", "gla_norm_w", "swa_sinks", "w_out", "ln1_g", "ln1_b", "w_up", "w_down",
             "ln2_g", "ln2_b"]
    res = {**big, **small}
    outs = [loss, grad_x[None]]
    for k in range(4):
        outs += [res[nm][k] for nm in order]
    return tuple(outs)
```

```python
import functools

import jax
import jax.numpy as jnp
from jax import lax
from jax.experimental import pallas as pl
from jax.experimental.pallas import tpu as pltpu

F32 = jnp.float32
BF16 = jnp.bfloat16

N_DEV = 8
D_MODEL = 2048
D_FF = 8192
GLA_HEADS = 4
GLA_DK = 128
GLA_DV = 256
GLA_KW = 512
D_GLA = 1024
GATE_RANK = 16
GATE_TAU = 16.0
SWA_HEADS = 16
SWA_DH = 64
SWA_GROUP = 8
WINDOW = 128
D_SWA = 1024
D_IN = 4368
ALPHA = 2.0 ** 0.25
LN_EPS = 1e-5
RMS_EPS = 1e-5
ADAM_LR = 0.001
ADAM_B1 = 0.9
ADAM_B2 = 0.999
ADAM_EPS = 1e-08
ADAM_WD = 0.01
ADAM_STEP = 10

C_QG, C_KG, C_VG, C_GG, C_QS, C_KS, C_VS, C_LO = 0, 512, 1024, 2048, 3072, 4096, 4224, 4352
D_INP = 4480
LANES = 128
GC = 16
NEG = -1e30

NN = ((1,), (0,))
NT = ((1,), (1,))
TN = ((0,), (0,))

VMEM_LIMIT = 52 * 1024 * 1024
MESH = pl.DeviceIdType.MESH


def _dot(a, b, dn, precision=None):
    return lax.dot_general(a, b, (dn, ((), ())), preferred_element_type=F32, precision=precision)


def _bf16_round(v):
    return v.astype(BF16).astype(F32)


def _cparams(dims):
    return pltpu.CompilerParams(dimension_semantics=dims, vmem_limit_bytes=VMEM_LIMIT)


def _dep_operand(dep):
    if dep is None:
        return (), ()
    return (dep,), (pl.BlockSpec(dep.shape, lambda *_: (0,) * dep.ndim),)


def _to_padded_rows(w):
    pad = jnp.zeros((D_INP - D_IN,) + w.shape[1:], w.dtype)
    return jnp.concatenate([w[:3072], w[3088:], w[3072:3088], pad], axis=0)


def _from_padded_rows(g):
    return jnp.concatenate([g[:3072], g[C_LO:C_LO + GATE_RANK], g[3072:C_LO]], axis=0)


def _matmul(name, a, b, dn, grid, a_spec, b_spec, out_shape, out_specs, acc_shape, *,
            extra=(), extra_specs=(), epilogue=None, dims=("parallel", "parallel", "arbitrary"), dep=None):
    nk = grid[2]
    n_extra = len(extra)
    deps, dep_specs = _dep_operand(dep)
    direct = epilogue is None and (nk == 1 or (not isinstance(out_shape, (list, tuple)) and out_shape.dtype == F32))

    def body(a_ref, b_ref, *rest):
        extra_refs = rest[:n_extra]
        out_refs = rest[n_extra + len(deps):] if direct else rest[n_extra + len(deps):-1]
        acc_ref = out_refs[0] if direct else rest[-1]
        part = _dot(a_ref[...].astype(BF16), b_ref[...].astype(BF16), dn)

        def finish():
            if epilogue is None:
                out_refs[0][...] = acc_ref[...].astype(out_refs[0].dtype)
            else:
                epilogue(acc_ref, extra_refs, out_refs)

        if direct and nk == 1:
            out_refs[0][...] = part.reshape(out_refs[0].shape).astype(out_refs[0].dtype)
        elif nk == 1:
            acc_ref[...] = part
            finish()
        else:
            k = pl.program_id(2)

            @pl.when(k == 0)
            def _():
                acc_ref[...] = part

            @pl.when(k > 0)
            def _():
                acc_ref[...] += part

            if not direct:
                @pl.when(k == nk - 1)
                def _():
                    finish()

    return pl.pallas_call(
        body, name=name, grid=grid,
        in_specs=[a_spec, b_spec, *extra_specs, *dep_specs],
        out_specs=out_specs, out_shape=out_shape,
        scratch_shapes=[] if direct else [pltpu.VMEM(acc_shape, F32)],
        compiler_params=_cparams(dims),
    )(a, b, *extra, *deps)


def _row_chunks(rows, step=128):
    step = min(step, rows)
    return [(r, step) for r in range(0, rows, step)]


def _ln_stats(r):
    mu = jnp.mean(r, axis=-1, keepdims=True)
    xc = r - mu
    var = jnp.mean(xc * xc, axis=-1, keepdims=True)
    rstd = lax.rsqrt(var + LN_EPS)
    return xc * rstd, rstd


def _ln_bwd(dy_g, xhat, rstd):
    m1 = jnp.mean(dy_g, axis=-1, keepdims=True)
    m2 = jnp.mean(dy_g * xhat, axis=-1, keepdims=True)
    return rstd * (dy_g - m1 - xhat * m2)


def _proj(xb, w_in_t, dep=None):
    t = xb.shape[0]
    tm, tn = min(1024, t), 640
    return _matmul(
        "proj", xb, w_in_t, NT, (t // tm, D_INP // tn, 1),
        pl.BlockSpec((tm, D_MODEL), lambda i, j, k: (i, 0)),
        pl.BlockSpec((tn, D_MODEL), lambda i, j, k: (j, 0)),
        jax.ShapeDtypeStruct((t, D_INP), F32),
        pl.BlockSpec((tm, tn), lambda i, j, k: (i, j)),
        (tm, tn), dep=dep)


def _gate_fwd(proj, w2p, b_gk, tri):
    t = proj.shape[0]
    r = tri.shape[0]

    def body(lo_ref, w_ref, b_ref, tri_ref, bloc_ref, dgdz_ref):
        z = _dot(lo_ref[...].astype(BF16), w_ref[...], NN) + b_ref[...]
        e = jnp.exp(-jnp.abs(z))
        gk = (jnp.minimum(z, 0.0) - jnp.log1p(e)) * (1.0 / GATE_TAU)
        inv = 1.0 / (1.0 + e)
        dgdz_ref[...] = jnp.where(z >= 0.0, e * inv, inv) * (1.0 / GATE_TAU)
        bloc_ref[...] = _dot(tri_ref[...], gk, NN, precision=lax.Precision.HIGHEST)

    return pl.pallas_call(
        body, name="gate_fwd", grid=(t // r,),
        in_specs=[pl.BlockSpec((r, LANES), lambda i: (i, C_LO // LANES)),
                  pl.BlockSpec((LANES, GLA_KW), lambda i: (0, 0)),
                  pl.BlockSpec((1, GLA_KW), lambda i: (0, 0)),
                  pl.BlockSpec((r, r), lambda i: (0, 0))],
        out_specs=[pl.BlockSpec((r, GLA_KW), lambda i: (i, 0)),
                   pl.BlockSpec((r, GLA_KW), lambda i: (i, 0))],
        out_shape=[jax.ShapeDtypeStruct((t, GLA_KW), F32), jax.ShapeDtypeStruct((t, GLA_KW), F32)],
        compiler_params=_cparams(("parallel",)),
    )(proj, w2p, b_gk, tri)


def _gla_fwd(proj, bloc):
    t = proj.shape[0]
    r = min(256, t)
    ncb = r // GC
    scale = GLA_DK ** -0.5

    def body(q_ref, k_ref, v_ref, b_ref, o_ref, st_ref, s_scr):
        @pl.when(pl.program_id(1) == 0)
        def _():
            s_scr[...] = jnp.zeros_like(s_scr)

        rows = lax.broadcasted_iota(jnp.int32, (GC, 1), 0)
        cols = lax.broadcasted_iota(jnp.int32, (1, GC), 1)

        def chunk(c, carry):
            r0 = pl.multiple_of(c * GC, GC)
            q = q_ref[pl.ds(r0, GC), :] * scale
            k = k_ref[pl.ds(r0, GC), :]
            v = v_ref[pl.ds(r0, GC), :]
            b = b_ref[pl.ds(r0, GC), :]
            st = s_scr[...]
            st_ref[c] = st
            bl = b[GC - 1:GC, :]
            kr = _bf16_round(k)
            att = jnp.zeros((GC, GC), F32)
            for j in range(GC):
                w = jnp.exp(jnp.where(rows >= j, b - b[j:j + 1, :], NEG))
                a = jnp.sum(_bf16_round(q * w) * kr[j:j + 1, :], axis=-1, keepdims=True)
                att = jnp.where(cols == j, a, att)
            vb = v.astype(BF16)
            o_ref[pl.ds(r0, GC), :] = (_dot((q * jnp.exp(b)).astype(BF16), st.astype(BF16), NT)
                                       + _dot(att.astype(BF16), vb, NN))
            kd = k * jnp.exp(bl - b)
            s_scr[...] = st * jnp.exp(bl) + _dot(vb, kd.astype(BF16), TN)
            return carry

        lax.fori_loop(0, ncb, chunk, 0, unroll=2)

    return pl.pallas_call(
        body, name="gla_fwd", grid=(GLA_HEADS, t // r),
        in_specs=[pl.BlockSpec((r, GLA_DK), lambda h, i: (i, C_QG // GLA_DK + h)),
                  pl.BlockSpec((r, GLA_DK), lambda h, i: (i, C_KG // GLA_DK + h)),
                  pl.BlockSpec((r, GLA_DV), lambda h, i: (i, C_VG // GLA_DV + h)),
                  pl.BlockSpec((r, GLA_DK), lambda h, i: (i, h))],
        out_specs=[pl.BlockSpec((r, GLA_DV), lambda h, i: (i, h)),
                   pl.BlockSpec((ncb, None, GLA_DV, GLA_DK), lambda h, i: (i, h, 0, 0))],
        out_shape=[jax.ShapeDtypeStruct((t, D_GLA), F32),
                   jax.ShapeDtypeStruct((t // GC, GLA_HEADS, GLA_DV, GLA_DK), F32)],
        scratch_shapes=[pltpu.VMEM((GLA_DV, GLA_DK), F32)],
        compiler_params=_cparams(("parallel", "arbitrary")),
    )(proj, proj, proj, bloc)


def _gla_norm_fwd(o_raw, proj, norm_w):
    t = o_raw.shape[0]
    r = min(512, t)

    def body(o_ref, g_ref, w_ref, out_ref):
        w = w_ref[...]
        for h in range(GLA_HEADS):
            sl = slice(h * GLA_DV, (h + 1) * GLA_DV)
            o = o_ref[:, sl]
            g = g_ref[:, sl]
            on = o * lax.rsqrt(jnp.mean(o * o, axis=-1, keepdims=True) + RMS_EPS)
            out_ref[:, sl] = (on * w * (g * jax.nn.sigmoid(g))).astype(BF16)

    return pl.pallas_call(
        body, name="gla_norm_fwd", grid=(t // r,),
        in_specs=[pl.BlockSpec((r, D_GLA), lambda i: (i, 0)),
                  pl.BlockSpec((r, D_GLA), lambda i: (i, C_GG // D_GLA)),
                  pl.BlockSpec((1, GLA_DV), lambda i: (0, 0))],
        out_specs=pl.BlockSpec((r, D_GLA), lambda i: (i, 0)),
        out_shape=jax.ShapeDtypeStruct((t, D_GLA), BF16),
        compiler_params=_cparams(("parallel",)),
    )(o_raw, proj, norm_w)


def _swa_masks(i, heads):
    qi = lax.broadcasted_iota(jnp.int32, (heads * WINDOW, 1), 0) & (WINDOW - 1)
    kj = lax.broadcasted_iota(jnp.int32, (1, 2 * WINDOW), 1)
    valid = (kj > qi) & (kj <= qi + WINDOW) & ((i > 0) | (kj >= WINDOW))
    lo = lax.broadcasted_iota(jnp.int32, (1, LANES), 1) < SWA_DH
    return valid, lo


def _dup_half(x, lo, kv):
    xr = pltpu.roll(x, SWA_DH, axis=1)
    return jnp.where(lo, x, xr) if kv == 0 else jnp.where(lo, xr, x)


def _swa_stack(ref, col0, lo, kv):
    parts = []
    for p in range(4):
        c0 = col0 + LANES * (4 * kv + p)
        xp = ref[:, c0:c0 + LANES]
        parts += [jnp.where(lo, xp, 0.0).astype(BF16), jnp.where(lo, 0.0, xp).astype(BF16)]
    return jnp.concatenate(parts, axis=0)


def _swa_unstack(x_all, lo, p):
    r0 = 2 * p * WINDOW
    return jnp.where(lo, x_all[r0:r0 + WINDOW, :], x_all[r0 + WINDOW:r0 + 2 * WINDOW, :])


def _swa_sinks(sink_ref, kv):
    return jnp.concatenate([jnp.full((WINDOW, 1), sink_ref[SWA_GROUP * kv + h], F32) for h in range(SWA_GROUP)],
                           axis=0)


def _swa_probs(qm, kdup, valid, sink):
    s = _dot(qm, kdup, NT) * (SWA_DH ** -0.5)
    s = jnp.where(valid, s, NEG)
    m = jnp.maximum(jnp.max(s, axis=-1, keepdims=True), sink)
    p = jnp.exp(s - m)
    es = jnp.exp(sink - m)
    inv = 1.0 / (jnp.sum(p, axis=-1, keepdims=True) + es)
    return p * inv, es * inv


def _swa_fwd(proj, sinks, gla_out, dep=None):
    t = proj.shape[0]

    deps, dep_specs = _dep_operand(dep)

    def body(sink_ref, q_ref, kp_ref, kc_ref, vp_ref, vc_ref, gla_ref, *rest):
        o_ref = rest[-1]
        i = pl.program_id(0)
        o_ref[:, :D_GLA] = gla_ref[...]
        valid, lo = _swa_masks(i, 1)
        kb = jnp.concatenate([kp_ref[...], kc_ref[...]], axis=0)
        vb = jnp.concatenate([vp_ref[...], vc_ref[...]], axis=0)
        for kv in range(2):
            kdup = _dup_half(kb, lo, kv).astype(BF16)
            vdup = _dup_half(vb, lo, kv).astype(BF16)
            for p in range(4):
                c0 = LANES * (4 * kv + p)
                qp = q_ref[:, c0:c0 + LANES]
                halves = []
                for e in range(2):
                    qm = jnp.where(lo if e == 0 else ~lo, qp, 0.0).astype(BF16)
                    pn, _ = _swa_probs(qm, kdup, valid, sink_ref[SWA_GROUP * kv + 2 * p + e])
                    halves.append(_dot(pn.astype(BF16), vdup, NN))
                o_ref[:, D_GLA + c0:D_GLA + c0 + LANES] = jnp.where(lo, halves[0], halves[1]).astype(BF16)

    kvspec = lambda col, prev: pl.BlockSpec(
        (WINDOW, LANES), (lambda i: (jnp.maximum(i - 1, 0), col)) if prev else (lambda i: (i, col)))
    return pl.pallas_call(
        body, name="swa_fwd", grid=(t // WINDOW,),
        in_specs=[pl.BlockSpec(memory_space=pltpu.SMEM),
                  pl.BlockSpec((WINDOW, D_SWA), lambda i: (i, C_QS // D_SWA)),
                  kvspec(C_KS // LANES, True), kvspec(C_KS // LANES, False),
                  kvspec(C_VS // LANES, True), kvspec(C_VS // LANES, False),
                  pl.BlockSpec((WINDOW, D_GLA), lambda i: (i, 0)), *dep_specs],
        out_specs=pl.BlockSpec((WINDOW, D_MODEL), lambda i: (i, 0)),
        out_shape=jax.ShapeDtypeStruct((t, D_MODEL), BF16),
        compiler_params=_cparams(("parallel",)),
    )(sinks, proj, proj, proj, proj, proj, gla_out, *deps)


def _mix_ln1(cat, w_out, x, ln_g, ln_b):
    t = cat.shape[0]
    tm, tk = min(512, t), 512

    def epilogue(acc_ref, extra, outs):
        x_ref, g_ref, b_ref = extra
        r1_ref, h1_ref, h1b_ref = outs
        for r0, n in _row_chunks(tm):
            rs = slice(r0, r0 + n)
            r1 = ALPHA * x_ref[rs, :] + acc_ref[rs, :]
            xhat, _ = _ln_stats(r1)
            h = xhat * g_ref[...] + b_ref[...]
            r1_ref[rs, :] = r1
            h1_ref[rs, :] = h
            h1b_ref[rs, :] = h.astype(BF16)

    row = pl.BlockSpec((tm, D_MODEL), lambda i, j, k: (i, 0))
    vec = pl.BlockSpec((1, D_MODEL), lambda i, j, k: (0, 0))
    return _matmul(
        "mix_ln1", cat, w_out, NN, (t // tm, 1, D_MODEL // tk),
        pl.BlockSpec((tm, tk), lambda i, j, k: (i, k)),
        pl.BlockSpec((tk, D_MODEL), lambda i, j, k: (k, 0)),
        [jax.ShapeDtypeStruct((t, D_MODEL), F32), jax.ShapeDtypeStruct((t, D_MODEL), F32),
         jax.ShapeDtypeStruct((t, D_MODEL), BF16)],
        [row, row, row], (tm, D_MODEL),
        extra=(x, ln_g, ln_b), extra_specs=(row, vec, vec), epilogue=epilogue)


def _mlp_up(h1b, w_up):
    t = h1b.shape[0]
    tm, tn = min(1024, t), 1024

    def epilogue(acc_ref, extra, outs):
        a_ref, hdn_ref = outs
        for r0, n in _row_chunks(tm, 256):
            rs = slice(r0, r0 + n)
            a = jnp.maximum(acc_ref[rs, :], 0.0)
            a_ref[rs, :] = a.astype(BF16)
            hdn_ref[rs, :] = (a * a).astype(BF16)

    out = pl.BlockSpec((tm, tn), lambda i, j, k: (i, j))
    return _matmul(
        "mlp_up", h1b, w_up, NN, (t // tm, D_FF // tn, 1),
        pl.BlockSpec((tm, D_MODEL), lambda i, j, k: (i, 0)),
        pl.BlockSpec((None, D_MODEL, tn), lambda i, j, k: (j, 0, 0)),
        [jax.ShapeDtypeStruct((t, D_FF), BF16), jax.ShapeDtypeStruct((t, D_FF), BF16)],
        [out, out], (tm, tn), epilogue=epilogue)


def _mlp_down_loss(hdn, w_down, h1, target, ln_g, ln_b):
    t = hdn.shape[0]
    tm, tn, tk = min(1024, t), 1024, 2048
    ff = _matmul(
        "mlp_down", hdn, w_down, NN, (t // tm, D_MODEL // tn, D_FF // tk),
        pl.BlockSpec((tm, tk), lambda i, j, k: (i, k)),
        pl.BlockSpec((tk, tn), lambda i, j, k: (k, j)),
        jax.ShapeDtypeStruct((t, D_MODEL), F32),
        pl.BlockSpec((tm, tn), lambda i, j, k: (i, j)), (tm, tn))
    r = min(256, t)

    def body(ff_ref, h1_ref, t_ref, g_ref, b_ref, dr2_ref, dr2b_ref, gg_ref, gb_ref, loss_ref):
        @pl.when(pl.program_id(0) == 0)
        def _():
            gg_ref[...] = jnp.zeros_like(gg_ref)
            gb_ref[...] = jnp.zeros_like(gb_ref)
            loss_ref[...] = jnp.zeros_like(loss_ref)

        for r0, n in _row_chunks(r, 64):
            rs = slice(r0, r0 + n)
            xhat, rstd = _ln_stats(ALPHA * h1_ref[rs, :] + ff_ref[rs, :])
            err = xhat * g_ref[...] + b_ref[...] - t_ref[rs, :]
            loss_ref[...] += 0.5 * jnp.sum(jnp.mean(err * err, axis=-1, keepdims=True))
            dy = err * (1.0 / D_MODEL)
            gg_ref[...] += jnp.sum(dy * xhat, axis=0, keepdims=True)
            gb_ref[...] += jnp.sum(dy, axis=0, keepdims=True)
            dr2 = _ln_bwd(dy * g_ref[...], xhat, rstd)
            dr2_ref[rs, :] = dr2
            dr2b_ref[rs, :] = dr2.astype(BF16)

    row = pl.BlockSpec((r, D_MODEL), lambda i: (i, 0))
    vec = pl.BlockSpec((1, D_MODEL), lambda i: (0, 0))
    return pl.pallas_call(
        body, name="ln2_loss", grid=(t // r,),
        in_specs=[row, row, row, vec, vec],
        out_specs=[row, row, vec, vec, pl.BlockSpec((1, LANES), lambda i: (0, 0))],
        out_shape=[jax.ShapeDtypeStruct((t, D_MODEL), F32), jax.ShapeDtypeStruct((t, D_MODEL), BF16),
                   jax.ShapeDtypeStruct((1, D_MODEL), F32), jax.ShapeDtypeStruct((1, D_MODEL), F32),
                   jax.ShapeDtypeStruct((1, LANES), F32)],
        compiler_params=_cparams(("arbitrary",)),
    )(ff, h1, target, ln_g, ln_b)


def _mlp_down_bwd(dr2b, w_down, a_act):
    t = dr2b.shape[0]
    tm, tn = min(1024, t), 1024

    def epilogue(acc_ref, extra, outs):
        (a_ref,) = extra
        for r0, n in _row_chunks(tm, 256):
            rs = slice(r0, r0 + n)
            outs[0][rs, :] = (acc_ref[rs, :] * (2.0 * a_ref[rs, :].astype(F32))).astype(BF16)

    blk = pl.BlockSpec((tm, tn), lambda i, j, k: (i, j))
    return _matmul(
        "mlp_down_bwd", dr2b, w_down, NT, (t // tm, D_FF // tn, 1),
        pl.BlockSpec((tm, D_MODEL), lambda i, j, k: (i, 0)),
        pl.BlockSpec((tn, D_MODEL), lambda i, j, k: (j, 0)),
        jax.ShapeDtypeStruct((t, D_FF), BF16), blk, (tm, tn),
        extra=(a_act,), extra_specs=(blk,), epilogue=epilogue)


def _grad_w_down(hdn, dr2b):
    t = hdn.shape[0]
    tm, tn = 1024, 1024
    return _matmul(
        "grad_w_down", hdn, dr2b, TN, (D_MODEL // tn, D_FF // tm, 1),
        pl.BlockSpec((t, tm), lambda j, i, k: (0, i)),
        pl.BlockSpec((t, tn), lambda j, i, k: (0, j)),
        jax.ShapeDtypeStruct((N_DEV, D_FF // N_DEV, D_MODEL), F32),
        pl.BlockSpec((None, tm, tn), lambda j, i, k: (i, 0, j)), (tm, tn))


def _grad_w_up(h1b, du, dep=None):
    t = h1b.shape[0]
    tm, tn = 1024, 1024
    return _matmul(
        "grad_w_up", h1b, du, TN, (N_DEV, D_MODEL // tm, 1),
        pl.BlockSpec((t, tm), lambda i, j, k: (0, j)),
        pl.BlockSpec((t, tn), lambda i, j, k: (0, i)),
        jax.ShapeDtypeStruct((N_DEV, D_MODEL, D_FF // N_DEV), F32),
        pl.BlockSpec((None, tm, tn), lambda i, j, k: (i, j, 0)), (tm, tn), dep=dep)


def _mlp_up_bwd_ln1(du, w_up, dr2, r1, ln_g, dep=None):
    t = du.shape[0]
    tm, tn, tk = min(1024, t), 1024, D_FF // N_DEV
    deps, dep_specs = _dep_operand(dep)

    def mm_body(a_ref, b_ref, *rest):
        o_ref = rest[-1]
        k = pl.program_id(2)
        part = _dot(a_ref[:, :tk], b_ref[0], NT) + _dot(a_ref[:, tk:], b_ref[1], NT)

        @pl.when(k == 0)
        def _():
            o_ref[...] = part

        @pl.when(k > 0)
        def _():
            o_ref[...] += part

    dff = pl.pallas_call(
        mm_body, name="mlp_up_bwd", grid=(t // tm, D_MODEL // tn, N_DEV // 2),
        in_specs=[pl.BlockSpec((tm, 2 * tk), lambda i, j, k: (i, k)),
                  pl.BlockSpec((2, tn, tk), lambda i, j, k: (k, j, 0)), *dep_specs],
        out_specs=pl.BlockSpec((tm, tn), lambda i, j, k: (i, j)),
        out_shape=jax.ShapeDtypeStruct((t, D_MODEL), F32),
        compiler_params=_cparams(("parallel", "parallel", "arbitrary")),
    )(du, w_up, *deps)
    r = min(256, t)

    def body(acc_ref, dr2_ref, r1_ref, g_ref, dr1_ref, dr1b_ref, gg_ref, gb_ref):
        @pl.when(pl.program_id(0) == 0)
        def _():
            gg_ref[...] = jnp.zeros_like(gg_ref)
            gb_ref[...] = jnp.zeros_like(gb_ref)

        for r0, n in _row_chunks(r, 64):
            rs = slice(r0, r0 + n)
            dh1 = ALPHA * dr2_ref[rs, :] + acc_ref[rs, :]
            xhat, rstd = _ln_stats(r1_ref[rs, :])
            gg_ref[...] += jnp.sum(dh1 * xhat, axis=0, keepdims=True)
            gb_ref[...] += jnp.sum(dh1, axis=0, keepdims=True)
            dr1 = _ln_bwd(dh1 * g_ref[...], xhat, rstd)
            dr1_ref[rs, :] = dr1
            dr1b_ref[rs, :] = dr1.astype(BF16)

    row = pl.BlockSpec((r, D_MODEL), lambda i: (i, 0))
    vec = pl.BlockSpec((1, D_MODEL), lambda i: (0, 0))
    return pl.pallas_call(
        body, name="ln1_bwd", grid=(t // r,),
        in_specs=[row, row, row, vec],
        out_specs=[row, row, vec, vec],
        out_shape=[jax.ShapeDtypeStruct((t, D_MODEL), F32), jax.ShapeDtypeStruct((t, D_MODEL), BF16),
                   jax.ShapeDtypeStruct((1, D_MODEL), F32), jax.ShapeDtypeStruct((1, D_MODEL), F32)],
        compiler_params=_cparams(("arbitrary",)),
    )(dff, dr2, r1, ln_g)


def _dcat(dr1b, w_out):
    t = dr1b.shape[0]
    tm, tn = min(1024, t), 1024
    return _matmul(
        "dcat", dr1b, w_out, NT, (t // tm, D_MODEL // tn, 1),
        pl.BlockSpec((tm, D_MODEL), lambda i, j, k: (i, 0)),
        pl.BlockSpec((tn, D_MODEL), lambda i, j, k: (j, 0)),
        jax.ShapeDtypeStruct((t, D_MODEL), F32),
        pl.BlockSpec((tm, tn), lambda i, j, k: (i, j)), (tm, tn))


def _grad_w_out(cat, dr1b, dep=None):
    t = cat.shape[0]
    tm, tn = 1024, 1024
    return _matmul(
        "grad_w_out", cat, dr1b, TN, (D_MODEL // tm, D_MODEL // tn, 1),
        pl.BlockSpec((t, tm), lambda i, j, k: (0, i)),
        pl.BlockSpec((t, tn), lambda i, j, k: (0, j)),
        jax.ShapeDtypeStruct((N_DEV, D_MODEL // N_DEV, D_MODEL), F32),
        pl.BlockSpec((tm // (D_MODEL // N_DEV), D_MODEL // N_DEV, tn), lambda i, j, k: (i, 0, j)), (tm, tn), dep=dep)


def _gla_norm_bwd(dcat, o_raw, proj, norm_w, dep=None):
    t = o_raw.shape[0]
    r = min(512, t)

    deps, dep_specs = _dep_operand(dep)

    def body(d_ref, o_ref, g_ref, w_ref, *rest):
        do_ref, dg_ref, dw_ref = rest[len(deps):]

        @pl.when(pl.program_id(0) == 0)
        def _():
            dw_ref[...] = jnp.zeros_like(dw_ref)

        w = w_ref[...]
        dw = jnp.zeros((1, GLA_DV), F32)
        for h in range(GLA_HEADS):
            sl = slice(h * GLA_DV, (h + 1) * GLA_DV)
            o = o_ref[:, sl]
            g = g_ref[:, sl]
            d = d_ref[:, sl]
            rr = lax.rsqrt(jnp.mean(o * o, axis=-1, keepdims=True) + RMS_EPS)
            on = o * rr
            sg = jax.nn.sigmoid(g)
            sil = g * sg
            dg_ref[:, sl] = (d * on * w * (sg * (1.0 + g * (1.0 - sg)))).astype(BF16)
            dw = dw + jnp.sum(d * on * sil, axis=0, keepdims=True)
            don = d * w * sil
            do_ref[:, sl] = rr * (don - on * jnp.mean(don * on, axis=-1, keepdims=True))
        dw_ref[...] += dw

    return pl.pallas_call(
        body, name="gla_norm_bwd", grid=(t // r,),
        in_specs=[pl.BlockSpec((r, D_GLA), lambda i: (i, 0)),
                  pl.BlockSpec((r, D_GLA), lambda i: (i, 0)),
                  pl.BlockSpec((r, D_GLA), lambda i: (i, C_GG // D_GLA)),
                  pl.BlockSpec((1, GLA_DV), lambda i: (0, 0)), *dep_specs],
        out_specs=[pl.BlockSpec((r, D_GLA), lambda i: (i, 0)),
                   pl.BlockSpec((r, D_GLA), lambda i: (i, 0)),
                   pl.BlockSpec((1, GLA_DV), lambda i: (0, 0))],
        out_shape=[jax.ShapeDtypeStruct((t, D_GLA), F32), jax.ShapeDtypeStruct((t, D_GLA), BF16),
                   jax.ShapeDtypeStruct((1, GLA_DV), F32)],
        compiler_params=_cparams(("arbitrary",)),
    )(dcat, o_raw, proj, norm_w, *deps)


def _gla_bwd(proj, bloc, do_raw, states, triu):
    t = proj.shape[0]
    r = min(256, t)
    ncb = r // GC
    nb = t // r
    scale = GLA_DK ** -0.5

    def body(q_ref, k_ref, v_ref, b_ref, do_ref, st_ref, u_ref, dq_ref, dk_ref, dv_ref, dg_ref, ds_scr):
        @pl.when(pl.program_id(1) == 0)
        def _():
            ds_scr[...] = jnp.zeros_like(ds_scr)

        rows = lax.broadcasted_iota(jnp.int32, (GC, 1), 0)

        def chunk(cc, carry):
            c = ncb - 1 - cc
            r0 = pl.multiple_of(c * GC, GC)
            q = q_ref[pl.ds(r0, GC), :] * scale
            k = k_ref[pl.ds(r0, GC), :]
            v = v_ref[pl.ds(r0, GC), :]
            b = b_ref[pl.ds(r0, GC), :]
            do = do_ref[pl.ds(r0, GC), :]
            st = st_ref[c]
            dsn = ds_scr[...]
            bl = b[GC - 1:GC, :]
            eb = jnp.exp(b)
            ekl = jnp.exp(bl - b)
            ebl = jnp.exp(bl)
            qh = q * eb
            kd = k * ekl
            dob = do.astype(BF16)
            dsb = dsn.astype(BF16)
            dqh = _dot(dob, st.astype(BF16), NN)
            dkd = _dot(v.astype(BF16), dsb, NN)
            dv = _dot(kd.astype(BF16), dsb, NT)
            dq_i = jnp.zeros((GC, GLA_DK), F32)
            dk_i = jnp.zeros((GC, GLA_DK), F32)
            dk_x = jnp.zeros((GC, GLA_DK), F32)
            kr = _bf16_round(k)
            vr = _bf16_round(v)
            dor = _bf16_round(do)
            for j in range(GC):
                w = jnp.exp(jnp.where(rows >= j, b - b[j:j + 1, :], NEG))
                qw = q * w
                qwr = _bf16_round(qw)
                a = _bf16_round(jnp.sum(qwr * kr[j:j + 1, :], axis=-1, keepdims=True))
                da = jnp.sum(dor * vr[j:j + 1, :], axis=-1, keepdims=True)
                dq_i = dq_i + da * (w * k[j:j + 1, :])
                dk_x = jnp.where(rows == j, jnp.sum(da * qw, axis=0, keepdims=True), dk_x)
                dk_i = jnp.where(rows == j, jnp.sum(_bf16_round(da) * qwr, axis=0, keepdims=True), dk_i)
                dv = dv + jnp.where(rows == j, jnp.sum(a * dor, axis=0, keepdims=True), 0.0)
            dqs = dqh * eb + dq_i
            dk = dkd * ekl + dk_i
            db_last = jnp.sum(dkd * kd, axis=0, keepdims=True) + ebl * jnp.sum(dsn * st, axis=0, keepdims=True)
            db = q * dqs - k * (dkd * ekl + dk_x) + jnp.where(rows == GC - 1, db_last, 0.0)
            dq_ref[pl.ds(r0, GC), :] = (dqs * scale).astype(BF16)
            dk_ref[pl.ds(r0, GC), :] = dk.astype(BF16)
            dv_ref[pl.ds(r0, GC), :] = dv.astype(BF16)
            dg_ref[pl.ds(r0, GC), :] = _dot(u_ref[...], db, NN, precision=lax.Precision.HIGHEST)
            ds_scr[...] = dsn * ebl + _dot(dob, qh.astype(BF16), TN)
            return carry

        lax.fori_loop(0, ncb, chunk, 0)

    rev = lambda i: nb - 1 - i
    return pl.pallas_call(
        body, name="gla_bwd", grid=(GLA_HEADS, nb),
        in_specs=[pl.BlockSpec((r, GLA_DK), lambda h, i: (rev(i), C_QG // GLA_DK + h)),
                  pl.BlockSpec((r, GLA_DK), lambda h, i: (rev(i), C_KG // GLA_DK + h)),
                  pl.BlockSpec((r, GLA_DV), lambda h, i: (rev(i), C_VG // GLA_DV + h)),
                  pl.BlockSpec((r, GLA_DK), lambda h, i: (rev(i), h)),
                  pl.BlockSpec((r, GLA_DV), lambda h, i: (rev(i), h)),
                  pl.BlockSpec((ncb, None, GLA_DV, GLA_DK), lambda h, i: (rev(i), h, 0, 0)),
                  pl.BlockSpec((GC, GC), lambda h, i: (0, 0))],
        out_specs=[pl.BlockSpec((r, GLA_DK), lambda h, i: (rev(i), h)),
                   pl.BlockSpec((r, GLA_DK), lambda h, i: (rev(i), h)),
                   pl.BlockSpec((r, GLA_DV), lambda h, i: (rev(i), h)),
                   pl.BlockSpec((r, GLA_DK), lambda h, i: (rev(i), h))],
        out_shape=[jax.ShapeDtypeStruct((t, GLA_KW), BF16), jax.ShapeDtypeStruct((t, GLA_KW), BF16),
                   jax.ShapeDtypeStruct((t, D_GLA), BF16), jax.ShapeDtypeStruct((t, GLA_KW), F32)],
        scratch_shapes=[pltpu.VMEM((GLA_DV, GLA_DK), F32)],
        compiler_params=_cparams(("parallel", "arbitrary")),
    )(proj, proj, proj, bloc, do_raw, states, triu)


def _gate_bwd(dg, dgdz, proj, w2p):
    t = dg.shape[0]
    r = min(512, t)

    def body(dg_ref, s_ref, lo_ref, w_ref, dlo_ref, gw_ref, gb_ref):
        @pl.when(pl.program_id(0) == 0)
        def _():
            gw_ref[...] = jnp.zeros_like(gw_ref)
            gb_ref[...] = jnp.zeros_like(gb_ref)

        dz = dg_ref[...] * s_ref[...]
        dzb = dz.astype(BF16)
        gb_ref[...] += jnp.sum(dz, axis=0, keepdims=True)
        gw_ref[...] += _dot(lo_ref[...].astype(BF16), dzb, TN)
        dlo_ref[...] = _dot(dzb, w_ref[...], NT).astype(BF16)

    return pl.pallas_call(
        body, name="gate_bwd", grid=(t // r,),
        in_specs=[pl.BlockSpec((r, GLA_KW), lambda i: (i, 0)),
                  pl.BlockSpec((r, GLA_KW), lambda i: (i, 0)),
                  pl.BlockSpec((r, LANES), lambda i: (i, C_LO // LANES)),
                  pl.BlockSpec((LANES, GLA_KW), lambda i: (0, 0))],
        out_specs=[pl.BlockSpec((r, LANES), lambda i: (i, 0)),
                   pl.BlockSpec((LANES, GLA_KW), lambda i: (0, 0)),
                   pl.BlockSpec((1, GLA_KW), lambda i: (0, 0))],
        out_shape=[jax.ShapeDtypeStruct((t, LANES), BF16), jax.ShapeDtypeStruct((LANES, GLA_KW), F32),
                   jax.ShapeDtypeStruct((1, GLA_KW), F32)],
        compiler_params=_cparams(("arbitrary",)),
    )(dg, dgdz, proj, w2p)


def _swa_bwd(proj, dcat, sinks, dep=None):
    t = proj.shape[0]

    deps, dep_specs = _dep_operand(dep)

    def body(sink_ref, q_ref, kp_ref, kc_ref, vp_ref, vc_ref, d_ref, *rest):
        dq_ref, dk_ref, dv_ref, dsink_ref = rest[len(deps):]
        i = pl.program_id(0)

        @pl.when(i == 0)
        def _():
            dk_ref[...] = jnp.zeros_like(dk_ref)
            dv_ref[...] = jnp.zeros_like(dv_ref)
            dsink_ref[...] = jnp.zeros_like(dsink_ref)

        valid, lo = _swa_masks(i, SWA_GROUP)
        lane = lax.broadcasted_iota(jnp.int32, (1, LANES), 1)
        kb = jnp.concatenate([kp_ref[...], kc_ref[...]], axis=0)
        vb = jnp.concatenate([vp_ref[...], vc_ref[...]], axis=0)
        dsink = jnp.zeros((1, LANES), F32)
        folded_k, folded_v = [], []
        for kv in range(2):
            kdup = _dup_half(kb, lo, kv).astype(BF16)
            vdup = _dup_half(vb, lo, kv).astype(BF16)
            qm = _swa_stack(q_ref, 0, lo, kv)
            dom = _swa_stack(d_ref, 0, lo, kv)
            pn, psink = _swa_probs(qm, kdup, valid, _swa_sinks(sink_ref, kv))
            dpr = _dot(dom, vdup, NT)
            drow = jnp.sum(dpr * pn, axis=-1, keepdims=True)
            ds_col = psink * drow
            for h in range(SWA_GROUP):
                dsink = dsink + jnp.where(lane == SWA_GROUP * kv + h,
                                          -jnp.sum(ds_col[h * WINDOW:(h + 1) * WINDOW, :]), 0.0)
            dsb = (pn * (dpr - drow) * (SWA_DH ** -0.5)).astype(BF16)
            dq_all = _dot(dsb, kdup, NN)
            for p in range(4):
                c0 = LANES * (4 * kv + p)
                dq_ref[:, c0:c0 + LANES] = _swa_unstack(dq_all, lo, p).astype(BF16)
            dkd = _dot(dsb, qm, TN)
            dvd = _dot(pn.astype(BF16), dom, TN)
            folded_k.append(dkd + pltpu.roll(dkd, SWA_DH, axis=1))
            folded_v.append(dvd + pltpu.roll(dvd, SWA_DH, axis=1))
        dkb = jnp.where(lo, folded_k[0], folded_k[1])
        dvb = jnp.where(lo, folded_v[0], folded_v[1])
        dsink_ref[...] += dsink
        cur = pl.ds(pl.multiple_of(i * WINDOW, WINDOW), WINDOW)
        dk_ref[cur, :] += dkb[WINDOW:, :]
        dv_ref[cur, :] += dvb[WINDOW:, :]

        @pl.when(i > 0)
        def _():
            prev = pl.ds(pl.multiple_of((i - 1) * WINDOW, WINDOW), WINDOW)
            dk_ref[prev, :] += dkb[:WINDOW, :]
            dv_ref[prev, :] += dvb[:WINDOW, :]

    kvspec = lambda col, prev: pl.BlockSpec(
        (WINDOW, LANES), (lambda i: (jnp.maximum(i - 1, 0), col)) if prev else (lambda i: (i, col)))
    full = pl.BlockSpec((t, LANES), lambda i: (0, 0))
    return pl.pallas_call(
        body, name="swa_bwd", grid=(t // WINDOW,),
        in_specs=[pl.BlockSpec(memory_space=pltpu.SMEM),
                  pl.BlockSpec((WINDOW, D_SWA), lambda i: (i, C_QS // D_SWA)),
                  kvspec(C_KS // LANES, True), kvspec(C_KS // LANES, False),
                  kvspec(C_VS // LANES, True), kvspec(C_VS // LANES, False),
                  pl.BlockSpec((WINDOW, D_SWA), lambda i: (i, 1)), *dep_specs],
        out_specs=[pl.BlockSpec((WINDOW, D_SWA), lambda i: (i, 0)), full, full,
                   pl.BlockSpec((1, LANES), lambda i: (0, 0))],
        out_shape=[jax.ShapeDtypeStruct((t, D_SWA), BF16), jax.ShapeDtypeStruct((t, LANES), F32),
                   jax.ShapeDtypeStruct((t, LANES), F32), jax.ShapeDtypeStruct((1, LANES), F32)],
        compiler_params=_cparams(("arbitrary",)),
    )(sinks, proj, proj, proj, proj, proj, dcat, *deps)


def _grad_w_in(xb, dproj):
    t = xb.shape[0]
    tm, tn = 640, 1024
    return _matmul(
        "grad_w_in", dproj, xb, TN, (D_INP // tm, D_MODEL // tn, 1),
        pl.BlockSpec((t, tm), lambda i, j, k: (0, i)),
        pl.BlockSpec((t, tn), lambda i, j, k: (0, j)),
        jax.ShapeDtypeStruct((D_INP, D_MODEL), F32),
        pl.BlockSpec((tm, tn), lambda i, j, k: (i, j)), (tm, tn))


def _grad_x(dproj, w_in_t, dr1, dep=None):
    t = dproj.shape[0]
    tm, tn = min(512, t), 1024

    def epilogue(acc_ref, extra, outs):
        for r0, n in _row_chunks(tm):
            rs = slice(r0, r0 + n)
            outs[0][rs, :] = ALPHA * extra[0][rs, :] + acc_ref[rs, :]

    blk = pl.BlockSpec((tm, tn), lambda j, i, k: (i, j))
    return _matmul(
        "grad_x", dproj, w_in_t, NN, (D_MODEL // tn, t // tm, 1),
        pl.BlockSpec((tm, D_INP), lambda j, i, k: (i, 0)),
        pl.BlockSpec((D_INP, tn), lambda j, i, k: (0, j)),
        jax.ShapeDtypeStruct((t, D_MODEL), F32), blk, (tm, tn),
        extra=(dr1,), extra_specs=(blk,), epilogue=epilogue, dep=dep)


def _place():
    x, y, c = lax.axis_index("x"), lax.axis_index("y"), lax.axis_index("c")
    chips = [(1 - x, y), (x, 1 - y), (1 - x, 1 - y)]
    return x, y, c, chips


def _all_gather(shards):
    n = len(shards)
    hbm = pl.BlockSpec(memory_space=pl.ANY)

    def body(*refs):
        ins, outs = refs[:n], refs[n:2 * n]
        send, recv, loc = refs[2 * n:]
        x, y, c, chips = _place()
        me, sib = (x, y, c), (x, y, 1 - c)

        def slot(t, px, py, pc):
            return outs[t].at[4 * px + 2 * py + pc]

        def copy(t, k, block, to, src=None):
            return pltpu.make_async_remote_copy(
                src_ref=slot(t, *block) if src is None else src, dst_ref=slot(t, *block),
                send_sem=send.at[7 * t + k], recv_sem=recv.at[7 * t + k], device_id=to, device_id_type=MESH)

        mine = [pltpu.make_async_copy(ins[t], slot(t, *me), loc.at[t]) for t in range(n)]
        for cp in mine:
            cp.start()
        sent = []
        for t in range(n):
            sent.append(copy(t, 0, me, sib, src=ins[t]))
            sent += [copy(t, 1 + j, me, (*chip, c), src=ins[t]) for j, chip in enumerate(chips)]
        for cp in sent:
            cp.start()
        for t in range(n):
            for j, chip in enumerate(chips):
                copy(t, 1 + j, (*chip, c), me).wait_recv()
                fwd = copy(t, 4 + j, (*chip, c), sib)
                fwd.start()
                sent.append(fwd)
        for t in range(n):
            copy(t, 0, sib, me).wait_recv()
            for j, chip in enumerate(chips):
                copy(t, 4 + j, (*chip, 1 - c), me).wait_recv()
        for cp in sent:
            cp.wait_send()
        for cp in mine:
            cp.wait()

    return pl.pallas_call(
        body, name="all_gather_weights",
        in_specs=[hbm] * n, out_specs=[hbm] * n,
        out_shape=[jax.ShapeDtypeStruct((N_DEV,) + s.shape, s.dtype) for s in shards],
        scratch_shapes=[pltpu.SemaphoreType.DMA((7 * n,)), pltpu.SemaphoreType.DMA((7 * n,)),
                        pltpu.SemaphoreType.DMA((n,))],
    )(*shards)


def _plan_gather_out(src, land, x, y, c, chips):
    me = 4 * x + 2 * y + c
    return [(src, land.at[me], to) for to in [(x, y, 1 - c)] + [(px, py, c) for px, py in chips]]


def _plan_gather_forward(src, land, x, y, c, chips):
    return [(land.at[4 * px + 2 * py + c], land.at[4 * px + 2 * py + c], (x, y, 1 - c)) for px, py in chips]


def _plan_sibling(src, land, x, y, c, chips):
    return [(src.at[2 * q + (1 - c)], land.at[q], (x, y, 1 - c)) for q in range(4)]


def _plan_chips(src, land, x, y, c, chips):
    return [(src.at[j], land.at[j], (px, py, c)) for j, (px, py) in enumerate(chips)]


_PLAN_COPIES = {_plan_gather_out: 4, _plan_gather_forward: 3, _plan_sibling: 4, _plan_chips: 3}
_HBM = pl.BlockSpec(memory_space=pltpu.HBM)
_SEM = pl.BlockSpec(memory_space=pltpu.SEMAPHORE)
_EFFECT = pltpu.SideEffectType.DATAFLOW_SIDE_EFFECTING


def _hbm(a):
    return pltpu.with_memory_space_constraint(a, pltpu.HBM)


def _plan_descriptors(plans, srcs, lands, send, recv):
    x, y, c, chips = _place()
    cps = []
    for plan, src, land in zip(plans, srcs, lands):
        for s_ref, d_ref, to in plan(src, land, x, y, c, chips):
            k = len(cps)
            cps.append(pltpu.make_async_remote_copy(src_ref=s_ref, dst_ref=d_ref, send_sem=send.at[k],
                                                    recv_sem=recv.at[k], device_id=to, device_id_type=MESH))
    return cps


def _copies_start(name, plans, srcs, lands, after):
    has_src = [s is not None for s in srcs]
    arrays = [s for s in srcs if s is not None] + list(lands)
    n_src = sum(has_src)
    n_cp = sum(_PLAN_COPIES[p] for p in plans)

    def body(*refs):
        ins = refs[:len(arrays)]
        send, recv = refs[len(arrays) + 1], refs[len(arrays) + 2]
        token = refs[-1]
        it = iter(ins[:n_src])
        src_refs = [next(it) if h else None for h in has_src]
        for cp in _plan_descriptors(plans, src_refs, ins[n_src:], send, recv):
            cp.start()
        token[...] = jnp.zeros_like(token)

    outs = pl.pallas_call(
        body, name=name,
        in_specs=[_HBM] * len(arrays) + [pl.BlockSpec(memory_space=pl.ANY)],
        out_specs=(_SEM, _SEM, *[_HBM] * len(arrays), pl.BlockSpec(memory_space=pltpu.VMEM)),
        out_shape=(pltpu.SemaphoreType.DMA((n_cp,)), pltpu.SemaphoreType.DMA((n_cp,)),
                   *[pltpu.HBM(a.shape, a.dtype) for a in arrays], jax.ShapeDtypeStruct((8, LANES), F32)),
        input_output_aliases={i: 2 + i for i in range(len(arrays))},
        compiler_params=pltpu.CompilerParams(has_side_effects=_EFFECT),
    )(*[_hbm(a) for a in arrays], after)
    send, recv = outs[0], outs[1]
    thru = list(outs[2:-1])
    it = iter(thru[:n_src])
    return send, recv, [next(it) if h else None for h in has_src], thru[n_src:], outs[-1]


def _copies_wait(name, plans, started, after):
    send, recv, srcs, lands, _ = started
    has_src = [s is not None for s in srcs]
    arrays = [s for s in srcs if s is not None] + list(lands)
    n_src = sum(has_src)

    def body(*refs):
        ins = refs[:len(arrays)]
        send_ref, recv_ref = refs[len(arrays)], refs[len(arrays) + 1]
        it = iter(ins[:n_src])
        src_refs = [next(it) if h else None for h in has_src]
        for cp in _plan_descriptors(plans, src_refs, ins[n_src:], send_ref, recv_ref):
            cp.wait_send()
            cp.wait_recv()

    outs = pl.pallas_call(
        body, name=name,
        in_specs=[_HBM] * len(arrays) + [_SEM, _SEM, pl.BlockSpec(memory_space=pl.ANY)],
        out_specs=tuple([_HBM] * len(arrays)),
        out_shape=tuple(pltpu.HBM(a.shape, a.dtype) for a in arrays),
        input_output_aliases={i: i for i in range(len(arrays))},
        compiler_params=pltpu.CompilerParams(has_side_effects=_EFFECT),
    )(*arrays, send, recv, after)
    return list(outs[:n_src]), list(outs[n_src:])


def _shard_tiles(rows, cols, tr):
    if rows % tr == 0:
        return (tr, cols), rows // tr, lambda r: (r, 0)
    tc = 2 * LANES
    return (rows, tc), cols // tc, lambda r: (0, r)


def _pair_sum(name, grad, from_sibling, blocks):
    _, rows, cols = grad.shape
    (br_, bc), steps, at = _shard_tiles(rows, cols, 256)

    def body(blk_ref, g_ref, s_ref, o_ref):
        o_ref[...] = (g_ref[...] + s_ref[...]).astype(BF16)

    return pl.pallas_call(
        body, name=name,
        grid_spec=pltpu.PrefetchScalarGridSpec(
            num_scalar_prefetch=1, grid=(3, steps),
            in_specs=[pl.BlockSpec((None, br_, bc), lambda j, r, br: (br[j], *at(r))),
                      pl.BlockSpec((None, br_, bc), lambda j, r, br: (br[3 + j], *at(r)))],
            out_specs=pl.BlockSpec((None, br_, bc), lambda j, r, br: (j, *at(r)))),
        out_shape=jax.ShapeDtypeStruct((3, rows, cols), BF16),
        compiler_params=_cparams(("parallel", "parallel")),
    )(blocks, grad, from_sibling)


def _adam_math(g, w, m, v):
    m2 = ADAM_B1 * m + (1.0 - ADAM_B1) * g
    v2 = ADAM_B2 * v + (1.0 - ADAM_B2) * (g * g)
    m_hat = m2 / (1.0 - ADAM_B1 ** ADAM_STEP)
    v_hat = v2 / (1.0 - ADAM_B2 ** ADAM_STEP)
    delta = -ADAM_LR * (m_hat / (jnp.sqrt(v_hat) + ADAM_EPS) + ADAM_WD * w)
    return delta, m2, v2


def _sum_adam(name, grad, from_sibling, from_chips, own, w, m, v):
    rows, cols = w.shape
    (br_, bc), steps, at = _shard_tiles(rows, cols, 128)

    def body(own_ref, p_ref, s_ref, r_ref, w_ref, m_ref, v_ref, g_out, d_out, m_out, v_out):
        g = p_ref[...] + s_ref[...]
        for j in range(3):
            g = g + r_ref[j].astype(F32)
        d, m2, v2 = _adam_math(g, w_ref[...], m_ref[...], v_ref[...])
        g_out[...] = g
        d_out[...] = d
        m_out[...] = m2
        v_out[...] = v2

    blk = pl.BlockSpec((br_, bc), lambda r, cr: at(r))
    shp = jax.ShapeDtypeStruct((rows, cols), F32)
    return pl.pallas_call(
        body, name=name,
        grid_spec=pltpu.PrefetchScalarGridSpec(
            num_scalar_prefetch=1, grid=(steps,),
            in_specs=[pl.BlockSpec((None, br_, bc), lambda r, cr: (cr[0], *at(r))),
                      pl.BlockSpec((None, br_, bc), lambda r, cr: (cr[1], *at(r))),
                      pl.BlockSpec((3, br_, bc), lambda r, cr: (0, *at(r))),
                      blk, blk, blk],
            out_specs=[blk, blk, blk, blk]),
        out_shape=[shp, shp, shp, shp],
        compiler_params=_cparams(("parallel",)),
    )(own, grad, from_sibling, from_chips, w, m, v)


def _adam_small(name, g, w, m, v):
    def body(g_ref, w_ref, m_ref, v_ref, d_out, m_out, v_out):
        d, m2, v2 = _adam_math(g_ref[...], w_ref[...], m_ref[...], v_ref[...])
        d_out[...] = d
        m_out[...] = m2
        v_out[...] = v2

    shp = jax.ShapeDtypeStruct(w.shape, F32)
    return pl.pallas_call(body, name=name, out_shape=[shp, shp, shp])(g, w, m, v)


def _all_reduce_small(pack):
    rows = pack.shape[0]

    def body(in_ref, out_ref, slots, send, recv):
        x, y, c, _ = _place()
        me = 4 * x + 2 * y + c
        slots[me] = in_ref[...]
        cps = []
        for k in range(1, N_DEV):
            dx, dy, dc = (k >> 2) & 1, (k >> 1) & 1, k & 1
            to = (jnp.bitwise_xor(x, dx), jnp.bitwise_xor(y, dy), jnp.bitwise_xor(c, dc))
            cps.append(pltpu.make_async_remote_copy(
                src_ref=in_ref, dst_ref=slots.at[me], send_sem=send.at[k - 1], recv_sem=recv.at[k - 1],
                device_id=to, device_id_type=MESH))
        for cp in cps:
            cp.start()
        for cp in cps:
            cp.wait()
        acc = slots[0]
        for d in range(1, N_DEV):
            acc = acc + slots[d]
        out_ref[...] = acc

    return pl.pallas_call(
        body, name="all_reduce_small",
        in_specs=[pl.BlockSpec(memory_space=pltpu.VMEM)],
        out_specs=pl.BlockSpec(memory_space=pltpu.VMEM),
        out_shape=jax.ShapeDtypeStruct((rows, LANES), F32),
        scratch_shapes=[pltpu.VMEM((N_DEV, rows, LANES), F32),
                        pltpu.SemaphoreType.DMA((N_DEV - 1,)), pltpu.SemaphoreType.DMA((N_DEV - 1,))],
    )(pack)


def _rows128(a):
    flat = a.reshape(-1)
    padn = (-flat.shape[0]) % (8 * LANES)
    if padn:
        flat = jnp.concatenate([flat, jnp.zeros((padn,), flat.dtype)])
    return flat.reshape(-1, LANES)


def kernel(x, w_in, w_gk2, b_gk, gla_norm_w, swa_sinks, w_out, ln1_g, ln1_b, w_up, w_down, ln2_g, ln2_b, loss_target, m_w_in, m_w_gk2, m_b_gk, m_gla_norm_w, m_swa_sinks, m_w_out, m_ln1_g, m_ln1_b, m_w_up, m_w_down, m_ln2_g, m_ln2_b, v_w_in, v_w_gk2, v_b_gk, v_gla_norm_w, v_swa_sinks, v_w_out, v_ln1_g, v_ln1_b, v_w_up, v_w_down, v_ln2_g, v_ln2_b):
    xc, yc, cc = lax.axis_index("x"), lax.axis_index("y"), lax.axis_index("c")
    me = 4 * xc + 2 * yc + cc

    x2 = x[0]
    t = x2.shape[0]
    xb = x2.astype(BF16)
    target = loss_target[0]

    first = _all_gather([w_in[0].T.astype(BF16), w_gk2[0].astype(BF16)])
    w_in_t = _to_padded_rows(first[0].reshape(D_IN, D_MODEL))
    w2 = first[1].transpose(1, 0, 2).reshape(GATE_RANK, GLA_KW)
    w2p = jnp.concatenate([w2, jnp.zeros((LANES - GATE_RANK, GLA_KW), BF16)], axis=0)
    shards = [w_out[0].astype(BF16), w_up[0].astype(BF16), w_down[0].astype(BF16)]
    lands = [lax.dynamic_update_index_in_dim(lax.empty((N_DEV,) + sh.shape, BF16), sh, me, 0) for sh in shards]
    out_plans = [_plan_gather_out] * 3
    fwd_plans = [_plan_gather_forward] * 3
    ag1 = _copies_start("gather_out_start", out_plans, shards, lands, first[0])

    proj = _proj(xb, w_in_t, dep=ag1[4])
    rt = min(256, t)
    ii = jnp.arange(rt)
    tri = ((ii[:, None] // GC == ii[None, :] // GC) & (ii[None, :] <= ii[:, None])).astype(F32)
    bloc, dgdz = _gate_fwd(proj, w2p, b_gk, tri)
    o_raw, states = _gla_fwd(proj, bloc)
    _, lands = _copies_wait("gather_out_wait", out_plans, ag1, o_raw)
    ag2 = _copies_start("gather_forward_start", fwd_plans, [None] * 3, lands, o_raw)
    gla_out = _gla_norm_fwd(o_raw, proj, gla_norm_w)
    cat = _swa_fwd(proj, swa_sinks[0], gla_out, dep=ag2[4])
    _, gathered = _copies_wait("gather_forward_wait", fwd_plans, ag2, cat)
    w_out_f = gathered[0].reshape(D_MODEL, D_MODEL)
    w_up_f = gathered[1]
    w_down_f = gathered[2].reshape(D_FF, D_MODEL)
    r1, h1, h1b = _mix_ln1(cat, w_out_f, x2, ln1_g, ln1_b)
    a_act, hdn = _mlp_up(h1b, w_up_f)
    dr2, dr2b, g_ln2_g, g_ln2_b, loss_part = _mlp_down_loss(hdn, w_down_f, h1, target, ln2_g, ln2_b)

    others = [2 * (1 - xc) + yc, 2 * xc + (1 - yc), 2 * (1 - xc) + (1 - yc)]
    blocks = jnp.stack([2 * q + cc for q in others] + others).astype(jnp.int32)
    own = jnp.stack([me, 2 * xc + yc]).astype(jnp.int32)
    wmv = dict(w_in=[a[0].T for a in (w_in, m_w_in, v_w_in)], w_out=[a[0] for a in (w_out, m_w_out, v_w_out)],
               w_up=[a[0] for a in (w_up, m_w_up, v_w_up)], w_down=[a[0] for a in (w_down, m_w_down, v_w_down)])
    big = {}

    def sib_land(g):
        return lax.empty((4,) + g.shape[1:], F32)

    def chip_land(g):
        return lax.empty((3,) + g.shape[1:], BF16)

    def finish(nm, g, from_sib, from_chips):
        outs = _sum_adam("sum_adam_" + nm, g, from_sib, from_chips, own, *wmv[nm])
        big[nm] = [(o.T if nm == "w_in" else o)[None] for o in outs]

    du = _mlp_down_bwd(dr2b, w_down_f, a_act)
    g_down = _grad_w_down(hdn, dr2b)
    sa_down = _copies_start("sibling_down_start", [_plan_sibling], [g_down], [sib_land(g_down)], g_down)
    g_up = _grad_w_up(h1b, du, dep=sa_down[4])
    (g_down,), (fs_down,) = _copies_wait("sibling_down_wait", [_plan_sibling], sa_down, g_up)
    p_down = _pair_sum("pair_sum_w_down", g_down, fs_down, blocks)
    sb_down = _copies_start("chips_down_sibling_up_start", [_plan_chips, _plan_sibling], [p_down, g_up],
                            [chip_land(g_down), sib_land(g_up)], p_down)
    dr1, dr1b, g_ln1_g, g_ln1_b = _mlp_up_bwd_ln1(du, w_up_f, dr2, r1, ln1_g, dep=sb_down[4])
    dcat = _dcat(dr1b, w_out_f)
    (_, g_up), (fc_down, fs_up) = _copies_wait("chips_down_sibling_up_wait", [_plan_chips, _plan_sibling], sb_down,
                                               dcat)
    finish("w_down", g_down, fs_down, fc_down)
    p_up = _pair_sum("pair_sum_w_up", g_up, fs_up, blocks)
    sb_up = _copies_start("chips_up_start", [_plan_chips], [p_up], [chip_land(g_up)], p_up)
    do_raw, dg_g, g_norm_w = _gla_norm_bwd(dcat, o_raw, proj, gla_norm_w, dep=sb_up[4])
    jj = jnp.arange(GC)
    triu = (jj[None, :] >= jj[:, None]).astype(F32)
    dq_g, dk_g, dv_g, dgk = _gla_bwd(proj, bloc, do_raw, states, triu)
    _, (fc_up,) = _copies_wait("chips_up_wait", [_plan_chips], sb_up, dgk)
    finish("w_up", g_up, fs_up, fc_up)
    dlo, gw2_p, g_b_gk = _gate_bwd(dgk, dgdz, proj, w2p)
    dq_s, dk_s, dv_s, g_sinks = _swa_bwd(proj, dcat, swa_sinks[0])
    dproj = jnp.concatenate([dq_g, dk_g, dv_g, dg_g, dq_s, dk_s.astype(BF16), dv_s.astype(BF16), dlo], axis=-1)
    gw_in_t = _grad_w_in(xb, dproj)
    g_in = _from_padded_rows(gw_in_t).reshape(N_DEV, D_IN // N_DEV, D_MODEL)
    sa_in = _copies_start("sibling_in_start", [_plan_sibling], [g_in], [sib_land(g_in)], g_in)
    g_out = _grad_w_out(cat, dr1b, dep=sa_in[4])
    (g_in,), (fs_in,) = _copies_wait("sibling_in_wait", [_plan_sibling], sa_in, g_out)
    p_in = _pair_sum("pair_sum_w_in", g_in, fs_in, blocks)
    sb_in = _copies_start("chips_in_sibling_out_start", [_plan_chips, _plan_sibling], [p_in, g_out],
                          [chip_land(g_in), sib_land(g_out)], p_in)
    grad_x = _grad_x(dproj, w_in_t, dr1, dep=sb_in[4])
    (_, g_out), (fc_in, fs_out) = _copies_wait("chips_in_sibling_out_wait", [_plan_chips, _plan_sibling], sb_in,
                                               grad_x)
    finish("w_in", g_in, fs_in, fc_in)
    p_out = _pair_sum("pair_sum_w_out", g_out, fs_out, blocks)
    sb_out = _copies_start("chips_out_start", [_plan_chips], [p_out], [chip_land(g_out)], p_out)
    _, (fc_out,) = _copies_wait("chips_out_wait", [_plan_chips], sb_out, p_out)
    finish("w_out", g_out, fs_out, fc_out)

    pieces = [loss_part, g_b_gk, g_norm_w, g_sinks[:, :SWA_HEADS], g_ln1_g, g_ln1_b, g_ln2_g, g_ln2_b,
              gw2_p[:GATE_RANK]]
    pack = jnp.concatenate([_rows128(p) for p in pieces], axis=0)
    tot = _all_reduce_small(pack)
    sizes = [p.size for p in pieces]
    offs = [0]
    for p in pieces:
        offs.append(offs[-1] + _rows128(p).shape[0])
    unpack = lambda i, shape: tot[offs[i]:offs[i + 1]].reshape(-1)[:sizes[i]].reshape(shape)
    loss = tot[0, 0]
    small_names = ["b_gk", "gla_norm_w", "swa_sinks", "ln1_g", "ln1_b", "ln2_g", "ln2_b"]
    small_w = dict(b_gk=(b_gk, m_b_gk, v_b_gk), gla_norm_w=(gla_norm_w, m_gla_norm_w, v_gla_norm_w),
                   swa_sinks=(swa_sinks, m_swa_sinks, v_swa_sinks), ln1_g=(ln1_g, m_ln1_g, v_ln1_g),
                   ln1_b=(ln1_b, m_ln1_b, v_ln1_b), ln2_g=(ln2_g, m_ln2_g, v_ln2_g),
                   ln2_b=(ln2_b, m_ln2_b, v_ln2_b))
    small_g = {nm: unpack(1 + i, small_w[nm][0].shape) for i, nm in enumerate(small_names)}
    g_pack = jnp.concatenate([_rows128(small_g[nm]) for nm in small_names], axis=0)
    wmv = [jnp.concatenate([_rows128(small_w[nm][k]) for nm in small_names], axis=0) for k in range(3)]
    small_out = _adam_small("adam_replicated", g_pack, *wmv)
    srow = [0]
    for nm in small_names:
        srow.append(srow[-1] + _rows128(small_w[nm][0]).shape[0])
    small = {}
    for i, nm in enumerate(small_names):
        shape = small_w[nm][0].shape
        n = small_w[nm][0].size
        small[nm] = [small_g[nm]] + [o[srow[i]:srow[i + 1]].reshape(-1)[:n].reshape(shape) for o in small_out]

    gw2_full = unpack(8, (GATE_RANK, GLA_KW))
    gw2_loc = lax.dynamic_slice_in_dim(gw2_full, me * (GLA_KW // N_DEV), GLA_KW // N_DEV, axis=1)
    gk2_out = _adam_small("adam_w_gk2", gw2_loc, w_gk2[0], m_w_gk2[0], v_w_gk2[0])
    big["w_gk2"] = [gw2_loc[None]] + [o[None] for o in gk2_out]

    order = ["w_in", "w_gk2", "b_gk", "gla_norm_w", "swa_sinks", "w_out", "ln1_g", "ln1_b", "w_up", "w_down",
             "ln2_g", "ln2_b"]
    res = {**big, **small}
    outs = [loss, grad_x[None]]
    for k in range(4):
        outs += [res[nm][k] for nm in order]
    return tuple(outs)
```

```python
import functools

import jax
import jax.numpy as jnp
from jax import lax
from jax.experimental import pallas as pl
from jax.experimental.pallas import tpu as pltpu

F32 = jnp.float32
BF16 = jnp.bfloat16

N_DEV = 8
D_MODEL = 2048
D_FF = 8192
GLA_HEADS = 4
GLA_DK = 128
GLA_DV = 256
GLA_KW = 512
D_GLA = 1024
GATE_RANK = 16
GATE_TAU = 16.0
SWA_HEADS = 16
SWA_DH = 64
SWA_GROUP = 8
WINDOW = 128
D_SWA = 1024
D_IN = 4368
ALPHA = 2.0 ** 0.25
LN_EPS = 1e-5
RMS_EPS = 1e-5
ADAM_LR = 0.001
ADAM_B1 = 0.9
ADAM_B2 = 0.999
ADAM_EPS = 1e-08
ADAM_WD = 0.01
ADAM_STEP = 10

C_QG, C_KG, C_VG, C_GG, C_QS, C_KS, C_VS, C_LO = 0, 512, 1024, 2048, 3072, 4096, 4224, 4352
D_INP = 4480
LANES = 128
GC = 16
NEG = -1e30

NN = ((1,), (0,))
NT = ((1,), (1,))
TN = ((0,), (0,))

VMEM_LIMIT = 52 * 1024 * 1024
MESH = pl.DeviceIdType.MESH


def _dot(a, b, dn, precision=None):
    return lax.dot_general(a, b, (dn, ((), ())), preferred_element_type=F32, precision=precision)


def _bf16_round(v):
    return v.astype(BF16).astype(F32)


def _cparams(dims):
    return pltpu.CompilerParams(dimension_semantics=dims, vmem_limit_bytes=VMEM_LIMIT)


def _dep_operand(dep):
    if dep is None:
        return (), ()
    return (dep,), (pl.BlockSpec(dep.shape, lambda *_: (0,) * dep.ndim),)


def _to_padded_rows(w):
    pad = jnp.zeros((D_INP - D_IN,) + w.shape[1:], w.dtype)
    return jnp.concatenate([w[:3072], w[3088:], w[3072:3088], pad], axis=0)


def _from_padded_rows(g):
    return jnp.concatenate([g[:3072], g[C_LO:C_LO + GATE_RANK], g[3072:C_LO]], axis=0)


def _matmul(name, a, b, dn, grid, a_spec, b_spec, out_shape, out_specs, acc_shape, *,
            extra=(), extra_specs=(), epilogue=None, dims=("parallel", "parallel", "arbitrary"), dep=None):
    nk = grid[2]
    n_extra = len(extra)
    deps, dep_specs = _dep_operand(dep)
    direct = epilogue is None and (nk == 1 or (not isinstance(out_shape, (list, tuple)) and out_shape.dtype == F32))

    def body(a_ref, b_ref, *rest):
        extra_refs = rest[:n_extra]
        out_refs = rest[n_extra + len(deps):] if direct else rest[n_extra + len(deps):-1]
        acc_ref = out_refs[0] if direct else rest[-1]
        part = _dot(a_ref[...].astype(BF16), b_ref[...].astype(BF16), dn)

        def finish():
            if epilogue is None:
                out_refs[0][...] = acc_ref[...].astype(out_refs[0].dtype)
            else:
                epilogue(acc_ref, extra_refs, out_refs)

        if direct and nk == 1:
            out_refs[0][...] = part.reshape(out_refs[0].shape).astype(out_refs[0].dtype)
        elif nk == 1:
            acc_ref[...] = part
            finish()
        else:
            k = pl.program_id(2)

            @pl.when(k == 0)
            def _():
                acc_ref[...] = part

            @pl.when(k > 0)
            def _():
                acc_ref[...] += part

            if not direct:
                @pl.when(k == nk - 1)
                def _():
                    finish()

    return pl.pallas_call(
        body, name=name, grid=grid,
        in_specs=[a_spec, b_spec, *extra_specs, *dep_specs],
        out_specs=out_specs, out_shape=out_shape,
        scratch_shapes=[] if direct else [pltpu.VMEM(acc_shape, F32)],
        compiler_params=_cparams(dims),
    )(a, b, *extra, *deps)


def _row_chunks(rows, step=128):
    step = min(step, rows)
    return [(r, step) for r in range(0, rows, step)]


def _ln_stats(r):
    mu = jnp.mean(r, axis=-1, keepdims=True)
    xc = r - mu
    var = jnp.mean(xc * xc, axis=-1, keepdims=True)
    rstd = lax.rsqrt(var + LN_EPS)
    return xc * rstd, rstd


def _ln_bwd(dy_g, xhat, rstd):
    m1 = jnp.mean(dy_g, axis=-1, keepdims=True)
    m2 = jnp.mean(dy_g * xhat, axis=-1, keepdims=True)
    return rstd * (dy_g - m1 - xhat * m2)


def _proj(xb, w_in_t, dep=None):
    t = xb.shape[0]
    tm, tn = min(1024, t), 640
    return _matmul(
        "proj", xb, w_in_t, NT, (t // tm, D_INP // tn, 1),
        pl.BlockSpec((tm, D_MODEL), lambda i, j, k: (i, 0)),
        pl.BlockSpec((tn, D_MODEL), lambda i, j, k: (j, 0)),
        jax.ShapeDtypeStruct((t, D_INP), F32),
        pl.BlockSpec((tm, tn), lambda i, j, k: (i, j)),
        (tm, tn), dep=dep)


def _gate_fwd(proj, w2p, b_gk, tri):
    t = proj.shape[0]
    r = tri.shape[0]

    def body(lo_ref, w_ref, b_ref, tri_ref, bloc_ref, dgdz_ref):
        z = _dot(lo_ref[...].astype(BF16), w_ref[...], NN) + b_ref[...]
        e = jnp.exp(-jnp.abs(z))
        gk = (jnp.minimum(z, 0.0) - jnp.log1p(e)) * (1.0 / GATE_TAU)
        inv = 1.0 / (1.0 + e)
        dgdz_ref[...] = jnp.where(z >= 0.0, e * inv, inv) * (1.0 / GATE_TAU)
        bloc_ref[...] = _dot(tri_ref[...], gk, NN, precision=lax.Precision.HIGHEST)

    return pl.pallas_call(
        body, name="gate_fwd", grid=(t // r,),
        in_specs=[pl.BlockSpec((r, LANES), lambda i: (i, C_LO // LANES)),
                  pl.BlockSpec((LANES, GLA_KW), lambda i: (0, 0)),
                  pl.BlockSpec((1, GLA_KW), lambda i: (0, 0)),
                  pl.BlockSpec((r, r), lambda i: (0, 0))],
        out_specs=[pl.BlockSpec((r, GLA_KW), lambda i: (i, 0)),
                   pl.BlockSpec((r, GLA_KW), lambda i: (i, 0))],
        out_shape=[jax.ShapeDtypeStruct((t, GLA_KW), F32), jax.ShapeDtypeStruct((t, GLA_KW), F32)],
        compiler_params=_cparams(("parallel",)),
    )(proj, w2p, b_gk, tri)


def _gla_fwd(proj, bloc):
    t = proj.shape[0]
    r = min(256, t)
    ncb = r // GC
    scale = GLA_DK ** -0.5

    def body(q_ref, k_ref, v_ref, b_ref, o_ref, st_ref, s_scr):
        @pl.when(pl.program_id(1) == 0)
        def _():
            s_scr[...] = jnp.zeros_like(s_scr)

        rows = lax.broadcasted_iota(jnp.int32, (GC, 1), 0)
        cols = lax.broadcasted_iota(jnp.int32, (1, GC), 1)

        def chunk(c, carry):
            r0 = pl.multiple_of(c * GC, GC)
            q = q_ref[pl.ds(r0, GC), :] * scale
            k = k_ref[pl.ds(r0, GC), :]
            v = v_ref[pl.ds(r0, GC), :]
            b = b_ref[pl.ds(r0, GC), :]
            st = s_scr[...]
            st_ref[c] = st
            bl = b[GC - 1:GC, :]
            kr = _bf16_round(k)
            att = jnp.zeros((GC, GC), F32)
            for j in range(GC):
                w = jnp.exp(jnp.where(rows >= j, b - b[j:j + 1, :], NEG))
                a = jnp.sum(_bf16_round(q * w) * kr[j:j + 1, :], axis=-1, keepdims=True)
                att = jnp.where(cols == j, a, att)
            vb = v.astype(BF16)
            o_ref[pl.ds(r0, GC), :] = (_dot((q * jnp.exp(b)).astype(BF16), st.astype(BF16), NT)
                                       + _dot(att.astype(BF16), vb, NN))
            kd = k * jnp.exp(bl - b)
            s_scr[...] = st * jnp.exp(bl) + _dot(vb, kd.astype(BF16), TN)
            return carry

        lax.fori_loop(0, ncb, chunk, 0, unroll=2)

    return pl.pallas_call(
        body, name="gla_fwd", grid=(GLA_HEADS, t // r),
        in_specs=[pl.BlockSpec((r, GLA_DK), lambda h, i: (i, C_QG // GLA_DK + h)),
                  pl.BlockSpec((r, GLA_DK), lambda h, i: (i, C_KG // GLA_DK + h)),
                  pl.BlockSpec((r, GLA_DV), lambda h, i: (i, C_VG // GLA_DV + h)),
                  pl.BlockSpec((r, GLA_DK), lambda h, i: (i, h))],
        out_specs=[pl.BlockSpec((r, GLA_DV), lambda h, i: (i, h)),
                   pl.BlockSpec((ncb, None, GLA_DV, GLA_DK), lambda h, i: (i, h, 0, 0))],
        out_shape=[jax.ShapeDtypeStruct((t, D_GLA), F32),
                   jax.ShapeDtypeStruct((t // GC, GLA_HEADS, GLA_DV, GLA_DK), F32)],
        scratch_shapes=[pltpu.VMEM((GLA_DV, GLA_DK), F32)],
        compiler_params=_cparams(("parallel", "arbitrary")),
    )(proj, proj, proj, bloc)


def _gla_norm_fwd(o_raw, proj, norm_w):
    t = o_raw.shape[0]
    r = min(512, t)

    def body(o_ref, g_ref, w_ref, out_ref):
        w = w_ref[...]
        for h in range(GLA_HEADS):
            sl = slice(h * GLA_DV, (h + 1) * GLA_DV)
            o = o_ref[:, sl]
            g = g_ref[:, sl]
            on = o * lax.rsqrt(jnp.mean(o * o, axis=-1, keepdims=True) + RMS_EPS)
            out_ref[:, sl] = (on * w * (g * jax.nn.sigmoid(g))).astype(BF16)

    return pl.pallas_call(
        body, name="gla_norm_fwd", grid=(t // r,),
        in_specs=[pl.BlockSpec((r, D_GLA), lambda i: (i, 0)),
                  pl.BlockSpec((r, D_GLA), lambda i: (i, C_GG // D_GLA)),
                  pl.BlockSpec((1, GLA_DV), lambda i: (0, 0))],
        out_specs=pl.BlockSpec((r, D_GLA), lambda i: (i, 0)),
        out_shape=jax.ShapeDtypeStruct((t, D_GLA), BF16),
        compiler_params=_cparams(("parallel",)),
    )(o_raw, proj, norm_w)


def _swa_masks(i, heads):
    qi = lax.broadcasted_iota(jnp.int32, (heads * WINDOW, 1), 0) & (WINDOW - 1)
    kj = lax.broadcasted_iota(jnp.int32, (1, 2 * WINDOW), 1)
    valid = (kj > qi) & (kj <= qi + WINDOW) & ((i > 0) | (kj >= WINDOW))
    lo = lax.broadcasted_iota(jnp.int32, (1, LANES), 1) < SWA_DH
    return valid, lo


def _dup_half(x, lo, kv):
    xr = pltpu.roll(x, SWA_DH, axis=1)
    return jnp.where(lo, x, xr) if kv == 0 else jnp.where(lo, xr, x)


def _swa_stack(ref, col0, lo, kv):
    parts = []
    for p in range(4):
        c0 = col0 + LANES * (4 * kv + p)
        xp = ref[:, c0:c0 + LANES]
        parts += [jnp.where(lo, xp, 0.0).astype(BF16), jnp.where(lo, 0.0, xp).astype(BF16)]
    return jnp.concatenate(parts, axis=0)


def _swa_unstack(x_all, lo, p):
    r0 = 2 * p * WINDOW
    return jnp.where(lo, x_all[r0:r0 + WINDOW, :], x_all[r0 + WINDOW:r0 + 2 * WINDOW, :])


def _swa_sinks(sink_ref, kv):
    return jnp.concatenate([jnp.full((WINDOW, 1), sink_ref[SWA_GROUP * kv + h], F32) for h in range(SWA_GROUP)],
                           axis=0)


def _swa_probs(qm, kdup, valid, sink):
    s = _dot(qm, kdup, NT) * (SWA_DH ** -0.5)
    s = jnp.where(valid, s, NEG)
    m = jnp.maximum(jnp.max(s, axis=-1, keepdims=True), sink)
    p = jnp.exp(s - m)
    es = jnp.exp(sink - m)
    inv = 1.0 / (jnp.sum(p, axis=-1, keepdims=True) + es)
    return p * inv, es * inv


def _swa_fwd(proj, sinks, gla_out, dep=None):
    t = proj.shape[0]

    deps, dep_specs = _dep_operand(dep)

    def body(sink_ref, q_ref, kp_ref, kc_ref, vp_ref, vc_ref, gla_ref, *rest):
        o_ref = rest[-1]
        i = pl.program_id(0)
        o_ref[:, :D_GLA] = gla_ref[...]
        valid, lo = _swa_masks(i, 1)
        kb = jnp.concatenate([kp_ref[...], kc_ref[...]], axis=0)
        vb = jnp.concatenate([vp_ref[...], vc_ref[...]], axis=0)
        for kv in range(2):
            kdup = _dup_half(kb, lo, kv).astype(BF16)
            vdup = _dup_half(vb, lo, kv).astype(BF16)
            for p in range(4):
                c0 = LANES * (4 * kv + p)
                qp = q_ref[:, c0:c0 + LANES]
                halves = []
                for e in range(2):
                    qm = jnp.where(lo if e == 0 else ~lo, qp, 0.0).astype(BF16)
                    pn, _ = _swa_probs(qm, kdup, valid, sink_ref[SWA_GROUP * kv + 2 * p + e])
                    halves.append(_dot(pn.astype(BF16), vdup, NN))
                o_ref[:, D_GLA + c0:D_GLA + c0 + LANES] = jnp.where(lo, halves[0], halves[1]).astype(BF16)

    kvspec = lambda col, prev: pl.BlockSpec(
        (WINDOW, LANES), (lambda i: (jnp.maximum(i - 1, 0), col)) if prev else (lambda i: (i, col)))
    return pl.pallas_call(
        body, name="swa_fwd", grid=(t // WINDOW,),
        in_specs=[pl.BlockSpec(memory_space=pltpu.SMEM),
                  pl.BlockSpec((WINDOW, D_SWA), lambda i: (i, C_QS // D_SWA)),
                  kvspec(C_KS // LANES, True), kvspec(C_KS // LANES, False),
                  kvspec(C_VS // LANES, True), kvspec(C_VS // LANES, False),
                  pl.BlockSpec((WINDOW, D_GLA), lambda i: (i, 0)), *dep_specs],
        out_specs=pl.BlockSpec((WINDOW, D_MODEL), lambda i: (i, 0)),
        out_shape=jax.ShapeDtypeStruct((t, D_MODEL), BF16),
        compiler_params=_cparams(("parallel",)),
    )(sinks, proj, proj, proj, proj, proj, gla_out, *deps)


def _mix_ln1(cat, w_out, x, ln_g, ln_b):
    t = cat.shape[0]
    tm, tk = min(512, t), 512

    def epilogue(acc_ref, extra, outs):
        x_ref, g_ref, b_ref = extra
        r1_ref, h1_ref, h1b_ref = outs
        for r0, n in _row_chunks(tm):
            rs = slice(r0, r0 + n)
            r1 = ALPHA * x_ref[rs, :] + acc_ref[rs, :]
            xhat, _ = _ln_stats(r1)
            h = xhat * g_ref[...] + b_ref[...]
            r1_ref[rs, :] = r1
            h1_ref[rs, :] = h
            h1b_ref[rs, :] = h.astype(BF16)

    row = pl.BlockSpec((tm, D_MODEL), lambda i, j, k: (i, 0))
    vec = pl.BlockSpec((1, D_MODEL), lambda i, j, k: (0, 0))
    return _matmul(
        "mix_ln1", cat, w_out, NN, (t // tm, 1, D_MODEL // tk),
        pl.BlockSpec((tm, tk), lambda i, j, k: (i, k)),
        pl.BlockSpec((tk, D_MODEL), lambda i, j, k: (k, 0)),
        [jax.ShapeDtypeStruct((t, D_MODEL), F32), jax.ShapeDtypeStruct((t, D_MODEL), F32),
         jax.ShapeDtypeStruct((t, D_MODEL), BF16)],
        [row, row, row], (tm, D_MODEL),
        extra=(x, ln_g, ln_b), extra_specs=(row, vec, vec), epilogue=epilogue)


def _mlp_up(h1b, w_up):
    t = h1b.shape[0]
    tm, tn = min(1024, t), 1024

    def epilogue(acc_ref, extra, outs):
        a_ref, hdn_ref = outs
        for r0, n in _row_chunks(tm, 256):
            rs = slice(r0, r0 + n)
            a = jnp.maximum(acc_ref[rs, :], 0.0)
            a_ref[rs, :] = a.astype(BF16)
            hdn_ref[rs, :] = (a * a).astype(BF16)

    out = pl.BlockSpec((tm, tn), lambda i, j, k: (i, j))
    return _matmul(
        "mlp_up", h1b, w_up, NN, (t // tm, D_FF // tn, 1),
        pl.BlockSpec((tm, D_MODEL), lambda i, j, k: (i, 0)),
        pl.BlockSpec((None, D_MODEL, tn), lambda i, j, k: (j, 0, 0)),
        [jax.ShapeDtypeStruct((t, D_FF), BF16), jax.ShapeDtypeStruct((t, D_FF), BF16)],
        [out, out], (tm, tn), epilogue=epilogue)


def _mlp_down_loss(hdn, w_down, h1, target, ln_g, ln_b):
    t = hdn.shape[0]
    tm, tn, tk = min(1024, t), 1024, 2048
    ff = _matmul(
        "mlp_down", hdn, w_down, NN, (t // tm, D_MODEL // tn, D_FF // tk),
        pl.BlockSpec((tm, tk), lambda i, j, k: (i, k)),
        pl.BlockSpec((tk, tn), lambda i, j, k: (k, j)),
        jax.ShapeDtypeStruct((t, D_MODEL), F32),
        pl.BlockSpec((tm, tn), lambda i, j, k: (i, j)), (tm, tn))
    r = min(256, t)

    def body(ff_ref, h1_ref, t_ref, g_ref, b_ref, dr2_ref, dr2b_ref, gg_ref, gb_ref, loss_ref):
        @pl.when(pl.program_id(0) == 0)
        def _():
            gg_ref[...] = jnp.zeros_like(gg_ref)
            gb_ref[...] = jnp.zeros_like(gb_ref)
            loss_ref[...] = jnp.zeros_like(loss_ref)

        for r0, n in _row_chunks(r, 64):
            rs = slice(r0, r0 + n)
            xhat, rstd = _ln_stats(ALPHA * h1_ref[rs, :] + ff_ref[rs, :])
            err = xhat * g_ref[...] + b_ref[...] - t_ref[rs, :]
            loss_ref[...] += 0.5 * jnp.sum(jnp.mean(err * err, axis=-1, keepdims=True))
            dy = err * (1.0 / D_MODEL)
            gg_ref[...] += jnp.sum(dy * xhat, axis=0, keepdims=True)
            gb_ref[...] += jnp.sum(dy, axis=0, keepdims=True)
            dr2 = _ln_bwd(dy * g_ref[...], xhat, rstd)
            dr2_ref[rs, :] = dr2
            dr2b_ref[rs, :] = dr2.astype(BF16)

    row = pl.BlockSpec((r, D_MODEL), lambda i: (i, 0))
    vec = pl.BlockSpec((1, D_MODEL), lambda i: (0, 0))
    return pl.pallas_call(
        body, name="ln2_loss", grid=(t // r,),
        in_specs=[row, row, row, vec, vec],
        out_specs=[row, row, vec, vec, pl.BlockSpec((1, LANES), lambda i: (0, 0))],
        out_shape=[jax.ShapeDtypeStruct((t, D_MODEL), F32), jax.ShapeDtypeStruct((t, D_MODEL), BF16),
                   jax.ShapeDtypeStruct((1, D_MODEL), F32), jax.ShapeDtypeStruct((1, D_MODEL), F32),
                   jax.ShapeDtypeStruct((1, LANES), F32)],
        compiler_params=_cparams(("arbitrary",)),
    )(ff, h1, target, ln_g, ln_b)


def _mlp_down_bwd(dr2b, w_down, a_act):
    t = dr2b.shape[0]
    tm, tn = min(1024, t), 1024

    def epilogue(acc_ref, extra, outs):
        (a_ref,) = extra
        for r0, n in _row_chunks(tm, 256):
            rs = slice(r0, r0 + n)
            outs[0][rs, :] = (acc_ref[rs, :] * (2.0 * a_ref[rs, :].astype(F32))).astype(BF16)

    blk = pl.BlockSpec((tm, tn), lambda i, j, k: (i, j))
    return _matmul(
        "mlp_down_bwd", dr2b, w_down, NT, (t // tm, D_FF // tn, 1),
        pl.BlockSpec((tm, D_MODEL), lambda i, j, k: (i, 0)),
        pl.BlockSpec((tn, D_MODEL), lambda i, j, k: (j, 0)),
        jax.ShapeDtypeStruct((t, D_FF), BF16), blk, (tm, tn),
        extra=(a_act,), extra_specs=(blk,), epilogue=epilogue)


def _grad_w_down(hdn, dr2b):
    t = hdn.shape[0]
    tm, tn = 1024, 1024
    return _matmul(
        "grad_w_down", hdn, dr2b, TN, (D_MODEL // tn, D_FF // tm, 1),
        pl.BlockSpec((t, tm), lambda j, i, k: (0, i)),
        pl.BlockSpec((t, tn), lambda j, i, k: (0, j)),
        jax.ShapeDtypeStruct((N_DEV, D_FF // N_DEV, D_MODEL), F32),
        pl.BlockSpec((None, tm, tn), lambda j, i, k: (i, 0, j)), (tm, tn))


def _grad_w_up(h1b, du, dep=None):
    t = h1b.shape[0]
    tm, tn = 1024, 1024
    return _matmul(
        "grad_w_up", h1b, du, TN, (N_DEV, D_MODEL // tm, 1),
        pl.BlockSpec((t, tm), lambda i, j, k: (0, j)),
        pl.BlockSpec((t, tn), lambda i, j, k: (0, i)),
        jax.ShapeDtypeStruct((N_DEV, D_MODEL, D_FF // N_DEV), F32),
        pl.BlockSpec((None, tm, tn), lambda i, j, k: (i, j, 0)), (tm, tn), dep=dep)


def _mlp_up_bwd_ln1(du, w_up, dr2, r1, ln_g, dep=None):
    t = du.shape[0]
    tm, tn, tk = min(1024, t), 1024, D_FF // N_DEV
    deps, dep_specs = _dep_operand(dep)

    def mm_body(a_ref, b_ref, *rest):
        o_ref = rest[-1]
        k = pl.program_id(2)
        part = _dot(a_ref[:, :tk], b_ref[0], NT) + _dot(a_ref[:, tk:], b_ref[1], NT)

        @pl.when(k == 0)
        def _():
            o_ref[...] = part

        @pl.when(k > 0)
        def _():
            o_ref[...] += part

    dff = pl.pallas_call(
        mm_body, name="mlp_up_bwd", grid=(t // tm, D_MODEL // tn, N_DEV // 2),
        in_specs=[pl.BlockSpec((tm, 2 * tk), lambda i, j, k: (i, k)),
                  pl.BlockSpec((2, tn, tk), lambda i, j, k: (k, j, 0)), *dep_specs],
        out_specs=pl.BlockSpec((tm, tn), lambda i, j, k: (i, j)),
        out_shape=jax.ShapeDtypeStruct((t, D_MODEL), F32),
        compiler_params=_cparams(("parallel", "parallel", "arbitrary")),
    )(du, w_up, *deps)
    r = min(256, t)

    def body(acc_ref, dr2_ref, r1_ref, g_ref, dr1_ref, dr1b_ref, gg_ref, gb_ref):
        @pl.when(pl.program_id(0) == 0)
        def _():
            gg_ref[...] = jnp.zeros_like(gg_ref)
            gb_ref[...] = jnp.zeros_like(gb_ref)

        for r0, n in _row_chunks(r, 64):
            rs = slice(r0, r0 + n)
            dh1 = ALPHA * dr2_ref[rs, :] + acc_ref[rs, :]
            xhat, rstd = _ln_stats(r1_ref[rs, :])
            gg_ref[...] += jnp.sum(dh1 * xhat, axis=0, keepdims=True)
            gb_ref[...] += jnp.sum(dh1, axis=0, keepdims=True)
            dr1 = _ln_bwd(dh1 * g_ref[...], xhat, rstd)
            dr1_ref[rs, :] = dr1
            dr1b_ref[rs, :] = dr1.astype(BF16)

    row = pl.BlockSpec((r, D_MODEL), lambda i: (i, 0))
    vec = pl.BlockSpec((1, D_MODEL), lambda i: (0, 0))
    return pl.pallas_call(
        body, name="ln1_bwd", grid=(t // r,),
        in_specs=[row, row, row, vec],
        out_specs=[row, row, vec, vec],
        out_shape=[jax.ShapeDtypeStruct((t, D_MODEL), F32), jax.ShapeDtypeStruct((t, D_MODEL), BF16),
                   jax.ShapeDtypeStruct((1, D_MODEL), F32), jax.ShapeDtypeStruct((1, D_MODEL), F32)],
        compiler_params=_cparams(("arbitrary",)),
    )(dff, dr2, r1, ln_g)


def _dcat(dr1b, w_out):
    t = dr1b.shape[0]
    tm, tn = min(1024, t), 1024
    return _matmul(
        "dcat", dr1b, w_out, NT, (t // tm, D_MODEL // tn, 1),
        pl.BlockSpec((tm, D_MODEL), lambda i, j, k: (i, 0)),
        pl.BlockSpec((tn, D_MODEL), lambda i, j, k: (j, 0)),
        jax.ShapeDtypeStruct((t, D_MODEL), F32),
        pl.BlockSpec((tm, tn), lambda i, j, k: (i, j)), (tm, tn))


def _grad_w_out(cat, dr1b, dep=None):
    t = cat.shape[0]
    tm, tn = 1024, 1024
    return _matmul(
        "grad_w_out", cat, dr1b, TN, (D_MODEL // tm, D_MODEL // tn, 1),
        pl.BlockSpec((t, tm), lambda i, j, k: (0, i)),
        pl.BlockSpec((t, tn), lambda i, j, k: (0, j)),
        jax.ShapeDtypeStruct((N_DEV, D_MODEL // N_DEV, D_MODEL), F32),
        pl.BlockSpec((tm // (D_MODEL // N_DEV), D_MODEL // N_DEV, tn), lambda i, j, k: (i, 0, j)), (tm, tn), dep=dep)


def _gla_norm_bwd(dcat, o_raw, proj, norm_w, dep=None):
    t = o_raw.shape[0]
    r = min(512, t)

    deps, dep_specs = _dep_operand(dep)

    def body(d_ref, o_ref, g_ref, w_ref, *rest):
        do_ref, dg_ref, dw_ref = rest[len(deps):]

        @pl.when(pl.program_id(0) == 0)
        def _():
            dw_ref[...] = jnp.zeros_like(dw_ref)

        w = w_ref[...]
        dw = jnp.zeros((1, GLA_DV), F32)
        for h in range(GLA_HEADS):
            sl = slice(h * GLA_DV, (h + 1) * GLA_DV)
            o = o_ref[:, sl]
            g = g_ref[:, sl]
            d = d_ref[:, sl]
            rr = lax.rsqrt(jnp.mean(o * o, axis=-1, keepdims=True) + RMS_EPS)
            on = o * rr
            sg = jax.nn.sigmoid(g)
            sil = g * sg
            dg_ref[:, sl] = (d * on * w * (sg * (1.0 + g * (1.0 - sg)))).astype(BF16)
            dw = dw + jnp.sum(d * on * sil, axis=0, keepdims=True)
            don = d * w * sil
            do_ref[:, sl] = rr * (don - on * jnp.mean(don * on, axis=-1, keepdims=True))
        dw_ref[...] += dw

    return pl.pallas_call(
        body, name="gla_norm_bwd", grid=(t // r,),
        in_specs=[pl.BlockSpec((r, D_GLA), lambda i: (i, 0)),
                  pl.BlockSpec((r, D_GLA), lambda i: (i, 0)),
                  pl.BlockSpec((r, D_GLA), lambda i: (i, C_GG // D_GLA)),
                  pl.BlockSpec((1, GLA_DV), lambda i: (0, 0)), *dep_specs],
        out_specs=[pl.BlockSpec((r, D_GLA), lambda i: (i, 0)),
                   pl.BlockSpec((r, D_GLA), lambda i: (i, 0)),
                   pl.BlockSpec((1, GLA_DV), lambda i: (0, 0))],
        out_shape=[jax.ShapeDtypeStruct((t, D_GLA), F32), jax.ShapeDtypeStruct((t, D_GLA), BF16),
                   jax.ShapeDtypeStruct((1, GLA_DV), F32)],
        compiler_params=_cparams(("arbitrary",)),
    )(dcat, o_raw, proj, norm_w, *deps)


def _gla_bwd(proj, bloc, do_raw, states, triu):
    t = proj.shape[0]
    r = min(256, t)
    ncb = r // GC
    nb = t // r
    scale = GLA_DK ** -0.5

    def body(q_ref, k_ref, v_ref, b_ref, do_ref, st_ref, u_ref, dq_ref, dk_ref, dv_ref, dg_ref, ds_scr):
        @pl.when(pl.program_id(1) == 0)
        def _():
            ds_scr[...] = jnp.zeros_like(ds_scr)

        rows = lax.broadcasted_iota(jnp.int32, (GC, 1), 0)

        def chunk(cc, carry):
            c = ncb - 1 - cc
            r0 = pl.multiple_of(c * GC, GC)
            q = q_ref[pl.ds(r0, GC), :] * scale
            k = k_ref[pl.ds(r0, GC), :]
            v = v_ref[pl.ds(r0, GC), :]
            b = b_ref[pl.ds(r0, GC), :]
            do = do_ref[pl.ds(r0, GC), :]
            st = st_ref[c]
            dsn = ds_scr[...]
            bl = b[GC - 1:GC, :]
            eb = jnp.exp(b)
            ekl = jnp.exp(bl - b)
            ebl = jnp.exp(bl)
            qh = q * eb
            kd = k * ekl
            dob = do.astype(BF16)
            dsb = dsn.astype(BF16)
            dqh = _dot(dob, st.astype(BF16), NN)
            dkd = _dot(v.astype(BF16), dsb, NN)
            dv = _dot(kd.astype(BF16), dsb, NT)
            dq_i = jnp.zeros((GC, GLA_DK), F32)
            dk_i = jnp.zeros((GC, GLA_DK), F32)
            dk_x = jnp.zeros((GC, GLA_DK), F32)
            kr = _bf16_round(k)
            vr = _bf16_round(v)
            dor = _bf16_round(do)
            for j in range(GC):
                w = jnp.exp(jnp.where(rows >= j, b - b[j:j + 1, :], NEG))
                qw = q * w
                qwr = _bf16_round(qw)
                a = _bf16_round(jnp.sum(qwr * kr[j:j + 1, :], axis=-1, keepdims=True))
                da = jnp.sum(dor * vr[j:j + 1, :], axis=-1, keepdims=True)
                dq_i = dq_i + da * (w * k[j:j + 1, :])
                dk_x = jnp.where(rows == j, jnp.sum(da * qw, axis=0, keepdims=True), dk_x)
                dk_i = jnp.where(rows == j, jnp.sum(_bf16_round(da) * qwr, axis=0, keepdims=True), dk_i)
                dv = dv + jnp.where(rows == j, jnp.sum(a * dor, axis=0, keepdims=True), 0.0)
            dqs = dqh * eb + dq_i
            dk = dkd * ekl + dk_i
            db_last = jnp.sum(dkd * kd, axis=0, keepdims=True) + ebl * jnp.sum(dsn * st, axis=0, keepdims=True)
            db = q * dqs - k * (dkd * ekl + dk_x) + jnp.where(rows == GC - 1, db_last, 0.0)
            dq_ref[pl.ds(r0, GC), :] = (dqs * scale).astype(BF16)
            dk_ref[pl.ds(r0, GC), :] = dk.astype(BF16)
            dv_ref[pl.ds(r0, GC), :] = dv.astype(BF16)
            dg_ref[pl.ds(r0, GC), :] = _dot(u_ref[...], db, NN, precision=lax.Precision.HIGHEST)
            ds_scr[...] = dsn * ebl + _dot(dob, qh.astype(BF16), TN)
            return carry

        lax.fori_loop(0, ncb, chunk, 0)

    rev = lambda i: nb - 1 - i
    return pl.pallas_call(
        body, name="gla_bwd", grid=(GLA_HEADS, nb),
        in_specs=[pl.BlockSpec((r, GLA_DK), lambda h, i: (rev(i), C_QG // GLA_DK + h)),
                  pl.BlockSpec((r, GLA_DK), lambda h, i: (rev(i), C_KG // GLA_DK + h)),
                  pl.BlockSpec((r, GLA_DV), lambda h, i: (rev(i), C_VG // GLA_DV + h)),
                  pl.BlockSpec((r, GLA_DK), lambda h, i: (rev(i), h)),
                  pl.BlockSpec((r, GLA_DV), lambda h, i: (rev(i), h)),
                  pl.BlockSpec((ncb, None, GLA_DV, GLA_DK), lambda h, i: (rev(i), h, 0, 0)),
                  pl.BlockSpec((GC, GC), lambda h, i: (0, 0))],
        out_specs=[pl.BlockSpec((r, GLA_DK), lambda h, i: (rev(i), h)),
                   pl.BlockSpec((r, GLA_DK), lambda h, i: (rev(i), h)),
                   pl.BlockSpec((r, GLA_DV), lambda h, i: (rev(i), h)),
                   pl.BlockSpec((r, GLA_DK), lambda h, i: (rev(i), h))],
        out_shape=[jax.ShapeDtypeStruct((t, GLA_KW), BF16), jax.ShapeDtypeStruct((t, GLA_KW), BF16),
                   jax.ShapeDtypeStruct((t, D_GLA), BF16), jax.ShapeDtypeStruct((t, GLA_KW), F32)],
        scratch_shapes=[pltpu.VMEM((GLA_DV, GLA_DK), F32)],
        compiler_params=_cparams(("parallel", "arbitrary")),
    )(proj, proj, proj, bloc, do_raw, states, triu)


def _gate_bwd(dg, dgdz, proj, w2p):
    t = dg.shape[0]
    r = min(512, t)

    def body(dg_ref, s_ref, lo_ref, w_ref, dlo_ref, gw_ref, gb_ref):
        @pl.when(pl.program_id(0) == 0)
        def _():
            gw_ref[...] = jnp.zeros_like(gw_ref)
            gb_ref[...] = jnp.zeros_like(gb_ref)

        dz = dg_ref[...] * s_ref[...]
        dzb = dz.astype(BF16)
        gb_ref[...] += jnp.sum(dz, axis=0, keepdims=True)
        gw_ref[...] += _dot(lo_ref[...].astype(BF16), dzb, TN)
        dlo_ref[...] = _dot(dzb, w_ref[...], NT).astype(BF16)

    return pl.pallas_call(
        body, name="gate_bwd", grid=(t // r,),
        in_specs=[pl.BlockSpec((r, GLA_KW), lambda i: (i, 0)),
                  pl.BlockSpec((r, GLA_KW), lambda i: (i, 0)),
                  pl.BlockSpec((r, LANES), lambda i: (i, C_LO // LANES)),
                  pl.BlockSpec((LANES, GLA_KW), lambda i: (0, 0))],
        out_specs=[pl.BlockSpec((r, LANES), lambda i: (i, 0)),
                   pl.BlockSpec((LANES, GLA_KW), lambda i: (0, 0)),
                   pl.BlockSpec((1, GLA_KW), lambda i: (0, 0))],
        out_shape=[jax.ShapeDtypeStruct((t, LANES), BF16), jax.ShapeDtypeStruct((LANES, GLA_KW), F32),
                   jax.ShapeDtypeStruct((1, GLA_KW), F32)],
        compiler_params=_cparams(("arbitrary",)),
    )(dg, dgdz, proj, w2p)


def _swa_bwd(proj, dcat, sinks, dep=None):
    t = proj.shape[0]

    deps, dep_specs = _dep_operand(dep)

    def body(sink_ref, q_ref, kp_ref, kc_ref, vp_ref, vc_ref, d_ref, *rest):
        dq_ref, dk_ref, dv_ref, dsink_ref = rest[len(deps):]
        i = pl.program_id(0)

        @pl.when(i == 0)
        def _():
            dk_ref[...] = jnp.zeros_like(dk_ref)
            dv_ref[...] = jnp.zeros_like(dv_ref)
            dsink_ref[...] = jnp.zeros_like(dsink_ref)

        valid, lo = _swa_masks(i, SWA_GROUP)
        lane = lax.broadcasted_iota(jnp.int32, (1, LANES), 1)
        kb = jnp.concatenate([kp_ref[...], kc_ref[...]], axis=0)
        vb = jnp.concatenate([vp_ref[...], vc_ref[...]], axis=0)
        dsink = jnp.zeros((1, LANES), F32)
        folded_k, folded_v = [], []
        for kv in range(2):
            kdup = _dup_half(kb, lo, kv).astype(BF16)
            vdup = _dup_half(vb, lo, kv).astype(BF16)
            qm = _swa_stack(q_ref, 0, lo, kv)
            dom = _swa_stack(d_ref, 0, lo, kv)
            pn, psink = _swa_probs(qm, kdup, valid, _swa_sinks(sink_ref, kv))
            dpr = _dot(dom, vdup, NT)
            drow = jnp.sum(dpr * pn, axis=-1, keepdims=True)
            ds_col = psink * drow
            for h in range(SWA_GROUP):
                dsink = dsink + jnp.where(lane == SWA_GROUP * kv + h,
                                          -jnp.sum(ds_col[h * WINDOW:(h + 1) * WINDOW, :]), 0.0)
            dsb = (pn * (dpr - drow) * (SWA_DH ** -0.5)).astype(BF16)
            dq_all = _dot(dsb, kdup, NN)
            for p in range(4):
                c0 = LANES * (4 * kv + p)
                dq_ref[:, c0:c0 + LANES] = _swa_unstack(dq_all, lo, p).astype(BF16)
            dkd = _dot(dsb, qm, TN)
            dvd = _dot(pn.astype(BF16), dom, TN)
            folded_k.append(dkd + pltpu.roll(dkd, SWA_DH, axis=1))
            folded_v.append(dvd + pltpu.roll(dvd, SWA_DH, axis=1))
        dkb = jnp.where(lo, folded_k[0], folded_k[1])
        dvb = jnp.where(lo, folded_v[0], folded_v[1])
        dsink_ref[...] += dsink
        cur = pl.ds(pl.multiple_of(i * WINDOW, WINDOW), WINDOW)
        dk_ref[cur, :] += dkb[WINDOW:, :]
        dv_ref[cur, :] += dvb[WINDOW:, :]

        @pl.when(i > 0)
        def _():
            prev = pl.ds(pl.multiple_of((i - 1) * WINDOW, WINDOW), WINDOW)
            dk_ref[prev, :] += dkb[:WINDOW, :]
            dv_ref[prev, :] += dvb[:WINDOW, :]

    kvspec = lambda col, prev: pl.BlockSpec(
        (WINDOW, LANES), (lambda i: (jnp.maximum(i - 1, 0), col)) if prev else (lambda i: (i, col)))
    full = pl.BlockSpec((t, LANES), lambda i: (0, 0))
    return pl.pallas_call(
        body, name="swa_bwd", grid=(t // WINDOW,),
        in_specs=[pl.BlockSpec(memory_space=pltpu.SMEM),
                  pl.BlockSpec((WINDOW, D_SWA), lambda i: (i, C_QS // D_SWA)),
                  kvspec(C_KS // LANES, True), kvspec(C_KS // LANES, False),
                  kvspec(C_VS // LANES, True), kvspec(C_VS // LANES, False),
                  pl.BlockSpec((WINDOW, D_SWA), lambda i: (i, 1)), *dep_specs],
        out_specs=[pl.BlockSpec((WINDOW, D_SWA), lambda i: (i, 0)), full, full,
                   pl.BlockSpec((1, LANES), lambda i: (0, 0))],
        out_shape=[jax.ShapeDtypeStruct((t, D_SWA), BF16), jax.ShapeDtypeStruct((t, LANES), F32),
                   jax.ShapeDtypeStruct((t, LANES), F32), jax.ShapeDtypeStruct((1, LANES), F32)],
        compiler_params=_cparams(("arbitrary",)),
    )(sinks, proj, proj, proj, proj, proj, dcat, *deps)


def _grad_w_in(xb, dproj):
    t = xb.shape[0]
    tm, tn = 640, 1024
    return _matmul(
        "grad_w_in", dproj, xb, TN, (D_INP // tm, D_MODEL // tn, 1),
        pl.BlockSpec((t, tm), lambda i, j, k: (0, i)),
        pl.BlockSpec((t, tn), lambda i, j, k: (0, j)),
        jax.ShapeDtypeStruct((D_INP, D_MODEL), F32),
        pl.BlockSpec((tm, tn), lambda i, j, k: (i, j)), (tm, tn))


def _grad_x(dproj, w_in_t, dr1, dep=None):
    t = dproj.shape[0]
    tm, tn = min(512, t), 1024

    def epilogue(acc_ref, extra, outs):
        for r0, n in _row_chunks(tm):
            rs = slice(r0, r0 + n)
            outs[0][rs, :] = ALPHA * extra[0][rs, :] + acc_ref[rs, :]

    blk = pl.BlockSpec((tm, tn), lambda j, i, k: (i, j))
    return _matmul(
        "grad_x", dproj, w_in_t, NN, (D_MODEL // tn, t // tm, 1),
        pl.BlockSpec((tm, D_INP), lambda j, i, k: (i, 0)),
        pl.BlockSpec((D_INP, tn), lambda j, i, k: (0, j)),
        jax.ShapeDtypeStruct((t, D_MODEL), F32), blk, (tm, tn),
        extra=(dr1,), extra_specs=(blk,), epilogue=epilogue, dep=dep)


def _place():
    x, y, c = lax.axis_index("x"), lax.axis_index("y"), lax.axis_index("c")
    chips = [(1 - x, y), (x, 1 - y), (1 - x, 1 - y)]
    return x, y, c, chips


def _all_gather(shards):
    n = len(shards)
    hbm = pl.BlockSpec(memory_space=pl.ANY)

    def body(*refs):
        ins, outs = refs[:n], refs[n:2 * n]
        send, recv, loc = refs[2 * n:]
        x, y, c, chips = _place()
        me, sib = (x, y, c), (x, y, 1 - c)

        def slot(t, px, py, pc):
            return outs[t].at[4 * px + 2 * py + pc]

        def copy(t, k, block, to, src=None):
            return pltpu.make_async_remote_copy(
                src_ref=slot(t, *block) if src is None else src, dst_ref=slot(t, *block),
                send_sem=send.at[7 * t + k], recv_sem=recv.at[7 * t + k], device_id=to, device_id_type=MESH)

        mine = [pltpu.make_async_copy(ins[t], slot(t, *me), loc.at[t]) for t in range(n)]
        for cp in mine:
            cp.start()
        sent = []
        for t in range(n):
            sent.append(copy(t, 0, me, sib, src=ins[t]))
            sent += [copy(t, 1 + j, me, (*chip, c), src=ins[t]) for j, chip in enumerate(chips)]
        for cp in sent:
            cp.start()
        for t in range(n):
            for j, chip in enumerate(chips):
                copy(t, 1 + j, (*chip, c), me).wait_recv()
                fwd = copy(t, 4 + j, (*chip, c), sib)
                fwd.start()
                sent.append(fwd)
        for t in range(n):
            copy(t, 0, sib, me).wait_recv()
            for j, chip in enumerate(chips):
                copy(t, 4 + j, (*chip, 1 - c), me).wait_recv()
        for cp in sent:
            cp.wait_send()
        for cp in mine:
            cp.wait()

    return pl.pallas_call(
        body, name="all_gather_weights",
        in_specs=[hbm] * n, out_specs=[hbm] * n,
        out_shape=[jax.ShapeDtypeStruct((N_DEV,) + s.shape, s.dtype) for s in shards],
        scratch_shapes=[pltpu.SemaphoreType.DMA((7 * n,)), pltpu.SemaphoreType.DMA((7 * n,)),
                        pltpu.SemaphoreType.DMA((n,))],
    )(*shards)


def _plan_gather_out(src, land, x, y, c, chips):
    me = 4 * x + 2 * y + c
    return [(src, land.at[me], to) for to in [(x, y, 1 - c)] + [(px, py, c) for px, py in chips]]


def _plan_gather_forward(src, land, x, y, c, chips):
    return [(land.at[4 * px + 2 * py + c], land.at[4 * px + 2 * py + c], (x, y, 1 - c)) for px, py in chips]


def _plan_sibling(src, land, x, y, c, chips):
    return [(src.at[2 * q + (1 - c)], land.at[q], (x, y, 1 - c)) for q in range(4)]


def _plan_chips(src, land, x, y, c, chips):
    return [(src.at[j], land.at[j], (px, py, c)) for j, (px, py) in enumerate(chips)]


_PLAN_COPIES = {_plan_gather_out: 4, _plan_gather_forward: 3, _plan_sibling: 4, _plan_chips: 3}
_HBM = pl.BlockSpec(memory_space=pltpu.HBM)
_SEM = pl.BlockSpec(memory_space=pltpu.SEMAPHORE)
_EFFECT = pltpu.SideEffectType.DATAFLOW_SIDE_EFFECTING


def _hbm(a):
    return pltpu.with_memory_space_constraint(a, pltpu.HBM)


def _plan_descriptors(plans, srcs, lands, send, recv):
    x, y, c, chips = _place()
    cps = []
    for plan, src, land in zip(plans, srcs, lands):
        for s_ref, d_ref, to in plan(src, land, x, y, c, chips):
            k = len(cps)
            cps.append(pltpu.make_async_remote_copy(src_ref=s_ref, dst_ref=d_ref, send_sem=send.at[k],
                                                    recv_sem=recv.at[k], device_id=to, device_id_type=MESH))
    return cps


def _copies_start(name, plans, srcs, lands, after=None):
    has_src = [s is not None for s in srcs]
    arrays = [s for s in srcs if s is not None] + list(lands)
    n_src = sum(has_src)
    n_cp = sum(_PLAN_COPIES[p] for p in plans)
    afters = [] if after is None else [after]

    def body(*refs):
        ins = refs[:len(arrays)]
        send, recv = refs[len(arrays) + len(afters)], refs[len(arrays) + len(afters) + 1]
        token = refs[-1]
        it = iter(ins[:n_src])
        src_refs = [next(it) if h else None for h in has_src]
        for cp in _plan_descriptors(plans, src_refs, ins[n_src:], send, recv):
            cp.start()
        token[...] = jnp.zeros_like(token)

    outs = pl.pallas_call(
        body, name=name,
        in_specs=[_HBM] * len(arrays) + [pl.BlockSpec(memory_space=pl.ANY)] * len(afters),
        out_specs=(_SEM, _SEM, *[_HBM] * len(arrays), pl.BlockSpec(memory_space=pltpu.VMEM)),
        out_shape=(pltpu.SemaphoreType.DMA((n_cp,)), pltpu.SemaphoreType.DMA((n_cp,)),
                   *[pltpu.HBM(a.shape, a.dtype) for a in arrays], jax.ShapeDtypeStruct((8, LANES), F32)),
        input_output_aliases={i: 2 + i for i in range(len(arrays))},
        compiler_params=pltpu.CompilerParams(has_side_effects=_EFFECT),
    )(*[_hbm(a) for a in arrays], *afters)
    send, recv = outs[0], outs[1]
    thru = list(outs[2:-1])
    it = iter(thru[:n_src])
    return send, recv, [next(it) if h else None for h in has_src], thru[n_src:], outs[-1]


def _copies_wait(name, plans, started, after):
    send, recv, srcs, lands, _ = started
    has_src = [s is not None for s in srcs]
    arrays = [s for s in srcs if s is not None] + list(lands)
    n_src = sum(has_src)

    def body(*refs):
        ins = refs[:len(arrays)]
        send_ref, recv_ref = refs[len(arrays)], refs[len(arrays) + 1]
        it = iter(ins[:n_src])
        src_refs = [next(it) if h else None for h in has_src]
        for cp in _plan_descriptors(plans, src_refs, ins[n_src:], send_ref, recv_ref):
            cp.wait_send()
            cp.wait_recv()

    outs = pl.pallas_call(
        body, name=name,
        in_specs=[_HBM] * len(arrays) + [_SEM, _SEM, pl.BlockSpec(memory_space=pl.ANY)],
        out_specs=tuple([_HBM] * len(arrays)),
        out_shape=tuple(pltpu.HBM(a.shape, a.dtype) for a in arrays),
        input_output_aliases={i: i for i in range(len(arrays))},
        compiler_params=pltpu.CompilerParams(has_side_effects=_EFFECT),
    )(*arrays, send, recv, after)
    return list(outs[:n_src]), list(outs[n_src:])


def _shard_tiles(rows, cols, tr):
    if rows % tr == 0:
        return (tr, cols), rows // tr, lambda r: (r, 0)
    tc = 2 * LANES
    return (rows, tc), cols // tc, lambda r: (0, r)


def _pair_sum(name, grad, from_sibling, blocks):
    _, rows, cols = grad.shape
    (br_, bc), steps, at = _shard_tiles(rows, cols, 256)

    def body(blk_ref, g_ref, s_ref, o_ref):
        o_ref[...] = (g_ref[...] + s_ref[...]).astype(BF16)

    return pl.pallas_call(
        body, name=name,
        grid_spec=pltpu.PrefetchScalarGridSpec(
            num_scalar_prefetch=1, grid=(3, steps),
            in_specs=[pl.BlockSpec((None, br_, bc), lambda j, r, br: (br[j], *at(r))),
                      pl.BlockSpec((None, br_, bc), lambda j, r, br: (br[3 + j], *at(r)))],
            out_specs=pl.BlockSpec((None, br_, bc), lambda j, r, br: (j, *at(r)))),
        out_shape=jax.ShapeDtypeStruct((3, rows, cols), BF16),
        compiler_params=_cparams(("parallel", "parallel")),
    )(blocks, grad, from_sibling)


def _adam_math(g, w, m, v):
    m2 = ADAM_B1 * m + (1.0 - ADAM_B1) * g
    v2 = ADAM_B2 * v + (1.0 - ADAM_B2) * (g * g)
    m_hat = m2 / (1.0 - ADAM_B1 ** ADAM_STEP)
    v_hat = v2 / (1.0 - ADAM_B2 ** ADAM_STEP)
    delta = -ADAM_LR * (m_hat / (jnp.sqrt(v_hat) + ADAM_EPS) + ADAM_WD * w)
    return delta, m2, v2


def _sum_adam(name, grad, from_sibling, from_chips, own, w, m, v, dep=None):
    rows, cols = w.shape
    (br_, bc), steps, at = _shard_tiles(rows, cols, 128)

    deps, dep_specs = _dep_operand(dep)

    def body(own_ref, p_ref, s_ref, r_ref, w_ref, m_ref, v_ref, *rest):
        g_out, d_out, m_out, v_out = rest[len(deps):]
        g = p_ref[...] + s_ref[...]
        for j in range(3):
            g = g + r_ref[j].astype(F32)
        d, m2, v2 = _adam_math(g, w_ref[...], m_ref[...], v_ref[...])
        g_out[...] = g
        d_out[...] = d
        m_out[...] = m2
        v_out[...] = v2

    blk = pl.BlockSpec((br_, bc), lambda r, cr: at(r))
    shp = jax.ShapeDtypeStruct((rows, cols), F32)
    return pl.pallas_call(
        body, name=name,
        grid_spec=pltpu.PrefetchScalarGridSpec(
            num_scalar_prefetch=1, grid=(steps,),
            in_specs=[pl.BlockSpec((None, br_, bc), lambda r, cr: (cr[0], *at(r))),
                      pl.BlockSpec((None, br_, bc), lambda r, cr: (cr[1], *at(r))),
                      pl.BlockSpec((3, br_, bc), lambda r, cr: (0, *at(r))),
                      blk, blk, blk, *dep_specs],
            out_specs=[blk, blk, blk, blk]),
        out_shape=[shp, shp, shp, shp],
        compiler_params=_cparams(("parallel",)),
    )(own, grad, from_sibling, from_chips, w, m, v, *deps)


def _adam_small(name, g, w, m, v):
    def body(g_ref, w_ref, m_ref, v_ref, d_out, m_out, v_out):
        d, m2, v2 = _adam_math(g_ref[...], w_ref[...], m_ref[...], v_ref[...])
        d_out[...] = d
        m_out[...] = m2
        v_out[...] = v2

    shp = jax.ShapeDtypeStruct(w.shape, F32)
    return pl.pallas_call(body, name=name, out_shape=[shp, shp, shp])(g, w, m, v)


def _all_reduce_small(pack):
    rows = pack.shape[0]

    def body(in_ref, out_ref, slots, send, recv):
        x, y, c, _ = _place()
        me = 4 * x + 2 * y + c
        slots[me] = in_ref[...]
        cps = []
        for k in range(1, N_DEV):
            dx, dy, dc = (k >> 2) & 1, (k >> 1) & 1, k & 1
            to = (jnp.bitwise_xor(x, dx), jnp.bitwise_xor(y, dy), jnp.bitwise_xor(c, dc))
            cps.append(pltpu.make_async_remote_copy(
                src_ref=in_ref, dst_ref=slots.at[me], send_sem=send.at[k - 1], recv_sem=recv.at[k - 1],
                device_id=to, device_id_type=MESH))
        for cp in cps:
            cp.start()
        for cp in cps:
            cp.wait()
        acc = slots[0]
        for d in range(1, N_DEV):
            acc = acc + slots[d]
        out_ref[...] = acc

    return pl.pallas_call(
        body, name="all_reduce_small",
        in_specs=[pl.BlockSpec(memory_space=pltpu.VMEM)],
        out_specs=pl.BlockSpec(memory_space=pltpu.VMEM),
        out_shape=jax.ShapeDtypeStruct((rows, LANES), F32),
        scratch_shapes=[pltpu.VMEM((N_DEV, rows, LANES), F32),
                        pltpu.SemaphoreType.DMA((N_DEV - 1,)), pltpu.SemaphoreType.DMA((N_DEV - 1,))],
    )(pack)


def _rows128(a):
    flat = a.reshape(-1)
    padn = (-flat.shape[0]) % (8 * LANES)
    if padn:
        flat = jnp.concatenate([flat, jnp.zeros((padn,), flat.dtype)])
    return flat.reshape(-1, LANES)


def kernel(x, w_in, w_gk2, b_gk, gla_norm_w, swa_sinks, w_out, ln1_g, ln1_b, w_up, w_down, ln2_g, ln2_b, loss_target, m_w_in, m_w_gk2, m_b_gk, m_gla_norm_w, m_swa_sinks, m_w_out, m_ln1_g, m_ln1_b, m_w_up, m_w_down, m_ln2_g, m_ln2_b, v_w_in, v_w_gk2, v_b_gk, v_gla_norm_w, v_swa_sinks, v_w_out, v_ln1_g, v_ln1_b, v_w_up, v_w_down, v_ln2_g, v_ln2_b):
    xc, yc, cc = lax.axis_index("x"), lax.axis_index("y"), lax.axis_index("c")
    me = 4 * xc + 2 * yc + cc

    x2 = x[0]
    t = x2.shape[0]
    xb = x2.astype(BF16)
    target = loss_target[0]

    first = _all_gather([w_in[0].T.astype(BF16), w_gk2[0].astype(BF16)])
    w_in_t = _to_padded_rows(first[0].reshape(D_IN, D_MODEL))
    w2 = first[1].transpose(1, 0, 2).reshape(GATE_RANK, GLA_KW)
    w2p = jnp.concatenate([w2, jnp.zeros((LANES - GATE_RANK, GLA_KW), BF16)], axis=0)
    shards = [w_out[0].astype(BF16), w_up[0].astype(BF16), w_down[0].astype(BF16)]
    lands = [lax.dynamic_update_index_in_dim(lax.empty((N_DEV,) + sh.shape, BF16), sh, me, 0) for sh in shards]
    out_plans = [_plan_gather_out] * 3
    fwd_plans = [_plan_gather_forward] * 3
    ag1 = _copies_start("gather_out_start", out_plans, shards, lands, first[0])

    proj = _proj(xb, w_in_t, dep=ag1[4])
    rt = min(256, t)
    ii = jnp.arange(rt)
    tri = ((ii[:, None] // GC == ii[None, :] // GC) & (ii[None, :] <= ii[:, None])).astype(F32)
    bloc, dgdz = _gate_fwd(proj, w2p, b_gk, tri)
    o_raw, states = _gla_fwd(proj, bloc)
    _, lands = _copies_wait("gather_out_wait", out_plans, ag1, o_raw)
    ag2 = _copies_start("gather_forward_start", fwd_plans, [None] * 3, lands, o_raw)
    gla_out = _gla_norm_fwd(o_raw, proj, gla_norm_w)
    cat = _swa_fwd(proj, swa_sinks[0], gla_out, dep=ag2[4])
    _, gathered = _copies_wait("gather_forward_wait", fwd_plans, ag2, cat)
    w_out_f = gathered[0].reshape(D_MODEL, D_MODEL)
    w_up_f = gathered[1]
    w_down_f = gathered[2].reshape(D_FF, D_MODEL)
    r1, h1, h1b = _mix_ln1(cat, w_out_f, x2, ln1_g, ln1_b)
    a_act, hdn = _mlp_up(h1b, w_up_f)
    dr2, dr2b, g_ln2_g, g_ln2_b, loss_part = _mlp_down_loss(hdn, w_down_f, h1, target, ln2_g, ln2_b)

    others = [2 * (1 - xc) + yc, 2 * xc + (1 - yc), 2 * (1 - xc) + (1 - yc)]
    blocks = jnp.stack([2 * q + cc for q in others] + others).astype(jnp.int32)
    own = jnp.stack([me, 2 * xc + yc]).astype(jnp.int32)
    wmv = dict(w_in=[a[0].T for a in (w_in, m_w_in, v_w_in)], w_out=[a[0] for a in (w_out, m_w_out, v_w_out)],
               w_up=[a[0] for a in (w_up, m_w_up, v_w_up)], w_down=[a[0] for a in (w_down, m_w_down, v_w_down)])
    big = {}

    def sib_land(g):
        return lax.empty((4,) + g.shape[1:], F32)

    def chip_land(g):
        return lax.empty((3,) + g.shape[1:], BF16)

    def finish(nm, g, from_sib, from_chips, dep=None):
        outs = _sum_adam("sum_adam_" + nm, g, from_sib, from_chips, own, *wmv[nm], dep=dep)
        big[nm] = [(o.T if nm == "w_in" else o)[None] for o in outs]

    du = _mlp_down_bwd(dr2b, w_down_f, a_act)
    g_down = _grad_w_down(hdn, dr2b)
    sa_down = _copies_start("sibling_down_start", [_plan_sibling], [g_down], [sib_land(g_down)])
    g_up = _grad_w_up(h1b, du, dep=sa_down[4])
    (g_down,), (fs_down,) = _copies_wait("sibling_down_wait", [_plan_sibling], sa_down, g_up)
    p_down = _pair_sum("pair_sum_w_down", g_down, fs_down, blocks)
    sb_down = _copies_start("chips_down_sibling_up_start", [_plan_chips, _plan_sibling], [p_down, g_up],
                            [chip_land(g_down), sib_land(g_up)])
    dr1, dr1b, g_ln1_g, g_ln1_b = _mlp_up_bwd_ln1(du, w_up_f, dr2, r1, ln1_g, dep=sb_down[4])
    dcat = _dcat(dr1b, w_out_f)
    (_, g_up), (fc_down, fs_up) = _copies_wait("chips_down_sibling_up_wait", [_plan_chips, _plan_sibling], sb_down,
                                               dcat)
    finish("w_down", g_down, fs_down, fc_down)
    p_up = _pair_sum("pair_sum_w_up", g_up, fs_up, blocks)
    sb_up = _copies_start("chips_up_start", [_plan_chips], [p_up], [chip_land(g_up)])
    do_raw, dg_g, g_norm_w = _gla_norm_bwd(dcat, o_raw, proj, gla_norm_w, dep=sb_up[4])
    jj = jnp.arange(GC)
    triu = (jj[None, :] >= jj[:, None]).astype(F32)
    dq_g, dk_g, dv_g, dgk = _gla_bwd(proj, bloc, do_raw, states, triu)
    _, (fc_up,) = _copies_wait("chips_up_wait", [_plan_chips], sb_up, dgk)
    finish("w_up", g_up, fs_up, fc_up)
    dlo, gw2_p, g_b_gk = _gate_bwd(dgk, dgdz, proj, w2p)
    dq_s, dk_s, dv_s, g_sinks = _swa_bwd(proj, dcat, swa_sinks[0])
    dproj = jnp.concatenate([dq_g, dk_g, dv_g, dg_g, dq_s, dk_s.astype(BF16), dv_s.astype(BF16), dlo], axis=-1)
    gw_in_t = _grad_w_in(xb, dproj)
    g_in = _from_padded_rows(gw_in_t).reshape(N_DEV, D_IN // N_DEV, D_MODEL)
    sa_in = _copies_start("sibling_in_start", [_plan_sibling], [g_in], [sib_land(g_in)])
    g_out = _grad_w_out(cat, dr1b, dep=sa_in[4])
    (g_in,), (fs_in,) = _copies_wait("sibling_in_wait", [_plan_sibling], sa_in, g_out)
    p_in = _pair_sum("pair_sum_w_in", g_in, fs_in, blocks)
    sb_in = _copies_start("chips_in_sibling_out_start", [_plan_chips, _plan_sibling], [p_in, g_out],
                          [chip_land(g_in), sib_land(g_out)])
    grad_x = _grad_x(dproj, w_in_t, dr1, dep=sb_in[4])
    (_, g_out), (fc_in, fs_out) = _copies_wait("chips_in_sibling_out_wait", [_plan_chips, _plan_sibling], sb_in,
                                               grad_x)
    p_out = _pair_sum("pair_sum_w_out", g_out, fs_out, blocks)
    sb_out = _copies_start("chips_out_start", [_plan_chips], [p_out], [chip_land(g_out)])
    finish("w_in", g_in, fs_in, fc_in, dep=sb_out[4])

    pieces = [loss_part, g_b_gk, g_norm_w, g_sinks[:, :SWA_HEADS], g_ln1_g, g_ln1_b, g_ln2_g, g_ln2_b,
              gw2_p[:GATE_RANK]]
    pack = jnp.concatenate([_rows128(p) for p in pieces], axis=0)
    tot = _all_reduce_small(pack + sb_out[4][:1, :1])
    sizes = [p.size for p in pieces]
    offs = [0]
    for p in pieces:
        offs.append(offs[-1] + _rows128(p).shape[0])
    unpack = lambda i, shape: tot[offs[i]:offs[i + 1]].reshape(-1)[:sizes[i]].reshape(shape)
    loss = tot[0, 0]
    small_names = ["b_gk", "gla_norm_w", "swa_sinks", "ln1_g", "ln1_b", "ln2_g", "ln2_b"]
    small_w = dict(b_gk=(b_gk, m_b_gk, v_b_gk), gla_norm_w=(gla_norm_w, m_gla_norm_w, v_gla_norm_w),
                   swa_sinks=(swa_sinks, m_swa_sinks, v_swa_sinks), ln1_g=(ln1_g, m_ln1_g, v_ln1_g),
                   ln1_b=(ln1_b, m_ln1_b, v_ln1_b), ln2_g=(ln2_g, m_ln2_g, v_ln2_g),
                   ln2_b=(ln2_b, m_ln2_b, v_ln2_b))
    small_g = {nm: unpack(1 + i, small_w[nm][0].shape) for i, nm in enumerate(small_names)}
    g_pack = jnp.concatenate([_rows128(small_g[nm]) for nm in small_names], axis=0)
    small_wmv = [jnp.concatenate([_rows128(small_w[nm][k]) for nm in small_names], axis=0) for k in range(3)]
    small_out = _adam_small("adam_replicated", g_pack, *small_wmv)
    srow = [0]
    for nm in small_names:
        srow.append(srow[-1] + _rows128(small_w[nm][0]).shape[0])
    small = {}
    for i, nm in enumerate(small_names):
        shape = small_w[nm][0].shape
        n = small_w[nm][0].size
        small[nm] = [small_g[nm]] + [o[srow[i]:srow[i + 1]].reshape(-1)[:n].reshape(shape) for o in small_out]

    gw2_full = unpack(8, (GATE_RANK, GLA_KW))
    gw2_loc = lax.dynamic_slice_in_dim(gw2_full, me * (GLA_KW // N_DEV), GLA_KW // N_DEV, axis=1)
    gk2_out = _adam_small("adam_w_gk2", gw2_loc, w_gk2[0], m_w_gk2[0], v_w_gk2[0])
    big["w_gk2"] = [gw2_loc[None]] + [o[None] for o in gk2_out]

    _, (fc_out,) = _copies_wait("chips_out_wait", [_plan_chips], sb_out, tot)
    finish("w_out", g_out, fs_out, fc_out)

    order = ["w_in", "w_gk2", "b_gk", "gla_norm_w", "swa_sinks", "w_out", "ln1_g", "ln1_b", "w_up", "w_down",
             "ln2_g", "ln2_b"]
    res = {**big, **small}
    outs = [loss, grad_x[None]]
    for k in range(4):
        outs += [res[nm][k] for nm in order]
    return tuple(outs)
```

```python
import functools

import jax
import jax.numpy as jnp
from jax import lax
from jax.experimental import pallas as pl
from jax.experimental.pallas import tpu as pltpu

F32 = jnp.float32
BF16 = jnp.bfloat16

N_DEV = 8
D_MODEL = 2048
D_FF = 8192
GLA_HEADS = 4
GLA_DK = 128
GLA_DV = 256
GLA_KW = 512
D_GLA = 1024
GATE_RANK = 16
GATE_TAU = 16.0
SWA_HEADS = 16
SWA_DH = 64
SWA_GROUP = 8
WINDOW = 128
D_SWA = 1024
D_IN = 4368
ALPHA = 2.0 ** 0.25
LN_EPS = 1e-5
RMS_EPS = 1e-5
ADAM_LR = 0.001
ADAM_B1 = 0.9
ADAM_B2 = 0.999
ADAM_EPS = 1e-08
ADAM_WD = 0.01
ADAM_STEP = 10

C_QG, C_KG, C_VG, C_GG, C_QS, C_KS, C_VS, C_LO = 0, 512, 1024, 2048, 3072, 4096, 4224, 4352
D_INP = 4480
LANES = 128
GC = 16
NEG = -1e30

NN = ((1,), (0,))
NT = ((1,), (1,))
TN = ((0,), (0,))

VMEM_LIMIT = 52 * 1024 * 1024
MESH = pl.DeviceIdType.MESH


def _dot(a, b, dn, precision=None):
    return lax.dot_general(a, b, (dn, ((), ())), preferred_element_type=F32, precision=precision)


def _bf16_round(v):
    return v.astype(BF16).astype(F32)


def _cparams(dims):
    return pltpu.CompilerParams(dimension_semantics=dims, vmem_limit_bytes=VMEM_LIMIT)


def _dep_operand(dep):
    if dep is None:
        return (), ()
    return (dep,), (pl.BlockSpec(dep.shape, lambda *_: (0,) * dep.ndim),)


def _to_padded_rows(w):
    pad = jnp.zeros((D_INP - D_IN,) + w.shape[1:], w.dtype)
    return jnp.concatenate([w[:3072], w[3088:], w[3072:3088], pad], axis=0)


def _from_padded_rows(g):
    return jnp.concatenate([g[:3072], g[C_LO:C_LO + GATE_RANK], g[3072:C_LO]], axis=0)


def _matmul(name, a, b, dn, grid, a_spec, b_spec, out_shape, out_specs, acc_shape, *,
            extra=(), extra_specs=(), epilogue=None, dims=("parallel", "parallel", "arbitrary"), dep=None):
    nk = grid[2]
    n_extra = len(extra)
    deps, dep_specs = _dep_operand(dep)
    direct = epilogue is None and (nk == 1 or (not isinstance(out_shape, (list, tuple)) and out_shape.dtype == F32))

    def body(a_ref, b_ref, *rest):
        extra_refs = rest[:n_extra]
        out_refs = rest[n_extra + len(deps):] if direct else rest[n_extra + len(deps):-1]
        acc_ref = out_refs[0] if direct else rest[-1]
        part = _dot(a_ref[...].astype(BF16), b_ref[...].astype(BF16), dn)

        def finish():
            if epilogue is None:
                out_refs[0][...] = acc_ref[...].astype(out_refs[0].dtype)
            else:
                epilogue(acc_ref, extra_refs, out_refs)

        if direct and nk == 1:
            out_refs[0][...] = part.reshape(out_refs[0].shape).astype(out_refs[0].dtype)
        elif nk == 1:
            acc_ref[...] = part
            finish()
        else:
            k = pl.program_id(2)

            @pl.when(k == 0)
            def _():
                acc_ref[...] = part

            @pl.when(k > 0)
            def _():
                acc_ref[...] += part

            if not direct:
                @pl.when(k == nk - 1)
                def _():
                    finish()

    return pl.pallas_call(
        body, name=name, grid=grid,
        in_specs=[a_spec, b_spec, *extra_specs, *dep_specs],
        out_specs=out_specs, out_shape=out_shape,
        scratch_shapes=[] if direct else [pltpu.VMEM(acc_shape, F32)],
        compiler_params=_cparams(dims),
    )(a, b, *extra, *deps)


def _row_chunks(rows, step=128):
    step = min(step, rows)
    return [(r, step) for r in range(0, rows, step)]


def _ln_stats(r):
    mu = jnp.mean(r, axis=-1, keepdims=True)
    xc = r - mu
    var = jnp.mean(xc * xc, axis=-1, keepdims=True)
    rstd = lax.rsqrt(var + LN_EPS)
    return xc * rstd, rstd


def _ln_bwd(dy_g, xhat, rstd):
    m1 = jnp.mean(dy_g, axis=-1, keepdims=True)
    m2 = jnp.mean(dy_g * xhat, axis=-1, keepdims=True)
    return rstd * (dy_g - m1 - xhat * m2)


def _proj(xb, w_in_t, dep=None):
    t = xb.shape[0]
    tm, tn = min(1024, t), 640
    return _matmul(
        "proj", xb, w_in_t, NT, (t // tm, D_INP // tn, 1),
        pl.BlockSpec((tm, D_MODEL), lambda i, j, k: (i, 0)),
        pl.BlockSpec((tn, D_MODEL), lambda i, j, k: (j, 0)),
        jax.ShapeDtypeStruct((t, D_INP), F32),
        pl.BlockSpec((tm, tn), lambda i, j, k: (i, j)),
        (tm, tn), dep=dep)


def _gate_fwd(proj, w2p, b_gk, tri):
    t = proj.shape[0]
    r = tri.shape[0]

    def body(lo_ref, w_ref, b_ref, tri_ref, bloc_ref, dgdz_ref):
        z = _dot(lo_ref[...].astype(BF16), w_ref[...], NN) + b_ref[...]
        e = jnp.exp(-jnp.abs(z))
        gk = (jnp.minimum(z, 0.0) - jnp.log1p(e)) * (1.0 / GATE_TAU)
        inv = 1.0 / (1.0 + e)
        dgdz_ref[...] = jnp.where(z >= 0.0, e * inv, inv) * (1.0 / GATE_TAU)
        bloc_ref[...] = _dot(tri_ref[...], gk, NN, precision=lax.Precision.HIGHEST)

    return pl.pallas_call(
        body, name="gate_fwd", grid=(t // r,),
        in_specs=[pl.BlockSpec((r, LANES), lambda i: (i, C_LO // LANES)),
                  pl.BlockSpec((LANES, GLA_KW), lambda i: (0, 0)),
                  pl.BlockSpec((1, GLA_KW), lambda i: (0, 0)),
                  pl.BlockSpec((r, r), lambda i: (0, 0))],
        out_specs=[pl.BlockSpec((r, GLA_KW), lambda i: (i, 0)),
                   pl.BlockSpec((r, GLA_KW), lambda i: (i, 0))],
        out_shape=[jax.ShapeDtypeStruct((t, GLA_KW), F32), jax.ShapeDtypeStruct((t, GLA_KW), F32)],
        compiler_params=_cparams(("parallel",)),
    )(proj, w2p, b_gk, tri)


def _gla_fwd(proj, bloc):
    t = proj.shape[0]
    r = min(256, t)
    ncb = r // GC
    scale = GLA_DK ** -0.5

    def body(q_ref, k_ref, v_ref, b_ref, o_ref, st_ref, s_scr):
        @pl.when(pl.program_id(1) == 0)
        def _():
            s_scr[...] = jnp.zeros_like(s_scr)

        rows = lax.broadcasted_iota(jnp.int32, (GC, 1), 0)
        cols = lax.broadcasted_iota(jnp.int32, (1, GC), 1)

        def chunk(c, carry):
            r0 = pl.multiple_of(c * GC, GC)
            q = q_ref[pl.ds(r0, GC), :] * scale
            k = k_ref[pl.ds(r0, GC), :]
            v = v_ref[pl.ds(r0, GC), :]
            b = b_ref[pl.ds(r0, GC), :]
            st = s_scr[...]
            st_ref[c] = st
            bl = b[GC - 1:GC, :]
            kr = _bf16_round(k)
            att = jnp.zeros((GC, GC), F32)
            for j in range(GC):
                w = jnp.exp(jnp.where(rows >= j, b - b[j:j + 1, :], NEG))
                a = jnp.sum(_bf16_round(q * w) * kr[j:j + 1, :], axis=-1, keepdims=True)
                att = jnp.where(cols == j, a, att)
            vb = v.astype(BF16)
            o_ref[pl.ds(r0, GC), :] = (_dot((q * jnp.exp(b)).astype(BF16), st.astype(BF16), NT)
                                       + _dot(att.astype(BF16), vb, NN))
            kd = k * jnp.exp(bl - b)
            s_scr[...] = st * jnp.exp(bl) + _dot(vb, kd.astype(BF16), TN)
            return carry

        lax.fori_loop(0, ncb, chunk, 0, unroll=2)

    return pl.pallas_call(
        body, name="gla_fwd", grid=(GLA_HEADS, t // r),
        in_specs=[pl.BlockSpec((r, GLA_DK), lambda h, i: (i, C_QG // GLA_DK + h)),
                  pl.BlockSpec((r, GLA_DK), lambda h, i: (i, C_KG // GLA_DK + h)),
                  pl.BlockSpec((r, GLA_DV), lambda h, i: (i, C_VG // GLA_DV + h)),
                  pl.BlockSpec((r, GLA_DK), lambda h, i: (i, h))],
        out_specs=[pl.BlockSpec((r, GLA_DV), lambda h, i: (i, h)),
                   pl.BlockSpec((ncb, None, GLA_DV, GLA_DK), lambda h, i: (i, h, 0, 0))],
        out_shape=[jax.ShapeDtypeStruct((t, D_GLA), F32),
                   jax.ShapeDtypeStruct((t // GC, GLA_HEADS, GLA_DV, GLA_DK), F32)],
        scratch_shapes=[pltpu.VMEM((GLA_DV, GLA_DK), F32)],
        compiler_params=_cparams(("parallel", "arbitrary")),
    )(proj, proj, proj, bloc)


def _gla_norm_fwd(o_raw, proj, norm_w):
    t = o_raw.shape[0]
    r = min(512, t)

    def body(o_ref, g_ref, w_ref, out_ref):
        w = w_ref[...]
        for h in range(GLA_HEADS):
            sl = slice(h * GLA_DV, (h + 1) * GLA_DV)
            o = o_ref[:, sl]
            g = g_ref[:, sl]
            on = o * lax.rsqrt(jnp.mean(o * o, axis=-1, keepdims=True) + RMS_EPS)
            out_ref[:, sl] = (on * w * (g * jax.nn.sigmoid(g))).astype(BF16)

    return pl.pallas_call(
        body, name="gla_norm_fwd", grid=(t // r,),
        in_specs=[pl.BlockSpec((r, D_GLA), lambda i: (i, 0)),
                  pl.BlockSpec((r, D_GLA), lambda i: (i, C_GG // D_GLA)),
                  pl.BlockSpec((1, GLA_DV), lambda i: (0, 0))],
        out_specs=pl.BlockSpec((r, D_GLA), lambda i: (i, 0)),
        out_shape=jax.ShapeDtypeStruct((t, D_GLA), BF16),
        compiler_params=_cparams(("parallel",)),
    )(o_raw, proj, norm_w)


def _swa_masks(i, heads):
    qi = lax.broadcasted_iota(jnp.int32, (heads * WINDOW, 1), 0) & (WINDOW - 1)
    kj = lax.broadcasted_iota(jnp.int32, (1, 2 * WINDOW), 1)
    valid = (kj > qi) & (kj <= qi + WINDOW) & ((i > 0) | (kj >= WINDOW))
    lo = lax.broadcasted_iota(jnp.int32, (1, LANES), 1) < SWA_DH
    return valid, lo


def _dup_half(x, lo, kv):
    xr = pltpu.roll(x, SWA_DH, axis=1)
    return jnp.where(lo, x, xr) if kv == 0 else jnp.where(lo, xr, x)


def _swa_stack(ref, col0, lo, kv):
    parts = []
    for p in range(4):
        c0 = col0 + LANES * (4 * kv + p)
        xp = ref[:, c0:c0 + LANES]
        parts += [jnp.where(lo, xp, 0.0).astype(BF16), jnp.where(lo, 0.0, xp).astype(BF16)]
    return jnp.concatenate(parts, axis=0)


def _swa_unstack(x_all, lo, p):
    r0 = 2 * p * WINDOW
    return jnp.where(lo, x_all[r0:r0 + WINDOW, :], x_all[r0 + WINDOW:r0 + 2 * WINDOW, :])


def _swa_sinks(sink_ref, kv):
    return jnp.concatenate([jnp.full((WINDOW, 1), sink_ref[SWA_GROUP * kv + h], F32) for h in range(SWA_GROUP)],
                           axis=0)


def _swa_probs(qm, kdup, valid, sink):
    s = _dot(qm, kdup, NT) * (SWA_DH ** -0.5)
    s = jnp.where(valid, s, NEG)
    m = jnp.maximum(jnp.max(s, axis=-1, keepdims=True), sink)
    p = jnp.exp(s - m)
    es = jnp.exp(sink - m)
    inv = 1.0 / (jnp.sum(p, axis=-1, keepdims=True) + es)
    return p * inv, es * inv


def _swa_fwd(proj, sinks, gla_out, dep=None):
    t = proj.shape[0]

    deps, dep_specs = _dep_operand(dep)

    def body(sink_ref, q_ref, kp_ref, kc_ref, vp_ref, vc_ref, gla_ref, *rest):
        o_ref = rest[-1]
        i = pl.program_id(0)
        o_ref[:, :D_GLA] = gla_ref[...]
        valid, lo = _swa_masks(i, 1)
        kb = jnp.concatenate([kp_ref[...], kc_ref[...]], axis=0)
        vb = jnp.concatenate([vp_ref[...], vc_ref[...]], axis=0)
        for kv in range(2):
            kdup = _dup_half(kb, lo, kv).astype(BF16)
            vdup = _dup_half(vb, lo, kv).astype(BF16)
            for p in range(4):
                c0 = LANES * (4 * kv + p)
                qp = q_ref[:, c0:c0 + LANES]
                halves = []
                for e in range(2):
                    qm = jnp.where(lo if e == 0 else ~lo, qp, 0.0).astype(BF16)
                    pn, _ = _swa_probs(qm, kdup, valid, sink_ref[SWA_GROUP * kv + 2 * p + e])
                    halves.append(_dot(pn.astype(BF16), vdup, NN))
                o_ref[:, D_GLA + c0:D_GLA + c0 + LANES] = jnp.where(lo, halves[0], halves[1]).astype(BF16)

    kvspec = lambda col, prev: pl.BlockSpec(
        (WINDOW, LANES), (lambda i: (jnp.maximum(i - 1, 0), col)) if prev else (lambda i: (i, col)))
    return pl.pallas_call(
        body, name="swa_fwd", grid=(t // WINDOW,),
        in_specs=[pl.BlockSpec(memory_space=pltpu.SMEM),
                  pl.BlockSpec((WINDOW, D_SWA), lambda i: (i, C_QS // D_SWA)),
                  kvspec(C_KS // LANES, True), kvspec(C_KS // LANES, False),
                  kvspec(C_VS // LANES, True), kvspec(C_VS // LANES, False),
                  pl.BlockSpec((WINDOW, D_GLA), lambda i: (i, 0)), *dep_specs],
        out_specs=pl.BlockSpec((WINDOW, D_MODEL), lambda i: (i, 0)),
        out_shape=jax.ShapeDtypeStruct((t, D_MODEL), BF16),
        compiler_params=_cparams(("parallel",)),
    )(sinks, proj, proj, proj, proj, proj, gla_out, *deps)


def _mix_ln1(cat, w_out, x, ln_g, ln_b):
    t = cat.shape[0]
    tm, tk = min(512, t), 512

    def epilogue(acc_ref, extra, outs):
        x_ref, g_ref, b_ref = extra
        r1_ref, h1_ref, h1b_ref = outs
        for r0, n in _row_chunks(tm):
            rs = slice(r0, r0 + n)
            r1 = ALPHA * x_ref[rs, :] + acc_ref[rs, :]
            xhat, _ = _ln_stats(r1)
            h = xhat * g_ref[...] + b_ref[...]
            r1_ref[rs, :] = r1
            h1_ref[rs, :] = h
            h1b_ref[rs, :] = h.astype(BF16)

    row = pl.BlockSpec((tm, D_MODEL), lambda i, j, k: (i, 0))
    vec = pl.BlockSpec((1, D_MODEL), lambda i, j, k: (0, 0))
    return _matmul(
        "mix_ln1", cat, w_out, NN, (t // tm, 1, D_MODEL // tk),
        pl.BlockSpec((tm, tk), lambda i, j, k: (i, k)),
        pl.BlockSpec((tk, D_MODEL), lambda i, j, k: (k, 0)),
        [jax.ShapeDtypeStruct((t, D_MODEL), F32), jax.ShapeDtypeStruct((t, D_MODEL), F32),
         jax.ShapeDtypeStruct((t, D_MODEL), BF16)],
        [row, row, row], (tm, D_MODEL),
        extra=(x, ln_g, ln_b), extra_specs=(row, vec, vec), epilogue=epilogue)


def _mlp_up(h1b, w_up):
    t = h1b.shape[0]
    tm, tn = min(1024, t), 1024

    def epilogue(acc_ref, extra, outs):
        a_ref, hdn_ref = outs
        for r0, n in _row_chunks(tm, 256):
            rs = slice(r0, r0 + n)
            a = jnp.maximum(acc_ref[rs, :], 0.0)
            a_ref[rs, :] = a.astype(BF16)
            hdn_ref[rs, :] = (a * a).astype(BF16)

    out = pl.BlockSpec((tm, tn), lambda i, j, k: (i, j))
    return _matmul(
        "mlp_up", h1b, w_up, NN, (t // tm, D_FF // tn, 1),
        pl.BlockSpec((tm, D_MODEL), lambda i, j, k: (i, 0)),
        pl.BlockSpec((None, D_MODEL, tn), lambda i, j, k: (j, 0, 0)),
        [jax.ShapeDtypeStruct((t, D_FF), BF16), jax.ShapeDtypeStruct((t, D_FF), BF16)],
        [out, out], (tm, tn), epilogue=epilogue)


def _mlp_down_loss(hdn, w_down, h1, target, ln_g, ln_b):
    t = hdn.shape[0]
    tm, tn, tk = min(1024, t), 1024, 2048
    ff = _matmul(
        "mlp_down", hdn, w_down, NN, (t // tm, D_MODEL // tn, D_FF // tk),
        pl.BlockSpec((tm, tk), lambda i, j, k: (i, k)),
        pl.BlockSpec((tk, tn), lambda i, j, k: (k, j)),
        jax.ShapeDtypeStruct((t, D_MODEL), F32),
        pl.BlockSpec((tm, tn), lambda i, j, k: (i, j)), (tm, tn))
    r = min(256, t)

    def body(ff_ref, h1_ref, t_ref, g_ref, b_ref, dr2_ref, dr2b_ref, gg_ref, gb_ref, loss_ref):
        @pl.when(pl.program_id(0) == 0)
        def _():
            gg_ref[...] = jnp.zeros_like(gg_ref)
            gb_ref[...] = jnp.zeros_like(gb_ref)
            loss_ref[...] = jnp.zeros_like(loss_ref)

        for r0, n in _row_chunks(r, 64):
            rs = slice(r0, r0 + n)
            xhat, rstd = _ln_stats(ALPHA * h1_ref[rs, :] + ff_ref[rs, :])
            err = xhat * g_ref[...] + b_ref[...] - t_ref[rs, :]
            loss_ref[...] += 0.5 * jnp.sum(jnp.mean(err * err, axis=-1, keepdims=True))
            dy = err * (1.0 / D_MODEL)
            gg_ref[...] += jnp.sum(dy * xhat, axis=0, keepdims=True)
            gb_ref[...] += jnp.sum(dy, axis=0, keepdims=True)
            dr2 = _ln_bwd(dy * g_ref[...], xhat, rstd)
            dr2_ref[rs, :] = dr2
            dr2b_ref[rs, :] = dr2.astype(BF16)

    row = pl.BlockSpec((r, D_MODEL), lambda i: (i, 0))
    vec = pl.BlockSpec((1, D_MODEL), lambda i: (0, 0))
    return pl.pallas_call(
        body, name="ln2_loss", grid=(t // r,),
        in_specs=[row, row, row, vec, vec],
        out_specs=[row, row, vec, vec, pl.BlockSpec((1, LANES), lambda i: (0, 0))],
        out_shape=[jax.ShapeDtypeStruct((t, D_MODEL), F32), jax.ShapeDtypeStruct((t, D_MODEL), BF16),
                   jax.ShapeDtypeStruct((1, D_MODEL), F32), jax.ShapeDtypeStruct((1, D_MODEL), F32),
                   jax.ShapeDtypeStruct((1, LANES), F32)],
        compiler_params=_cparams(("arbitrary",)),
    )(ff, h1, target, ln_g, ln_b)


def _mlp_down_bwd(dr2b, w_down, a_act):
    t = dr2b.shape[0]
    tm, tn = min(1024, t), 1024

    def epilogue(acc_ref, extra, outs):
        (a_ref,) = extra
        for r0, n in _row_chunks(tm, 256):
            rs = slice(r0, r0 + n)
            outs[0][rs, :] = (acc_ref[rs, :] * (2.0 * a_ref[rs, :].astype(F32))).astype(BF16)

    blk = pl.BlockSpec((tm, tn), lambda i, j, k: (i, j))
    return _matmul(
        "mlp_down_bwd", dr2b, w_down, NT, (t // tm, D_FF // tn, 1),
        pl.BlockSpec((tm, D_MODEL), lambda i, j, k: (i, 0)),
        pl.BlockSpec((tn, D_MODEL), lambda i, j, k: (j, 0)),
        jax.ShapeDtypeStruct((t, D_FF), BF16), blk, (tm, tn),
        extra=(a_act,), extra_specs=(blk,), epilogue=epilogue)


def _grad_w_down(hdn, dr2b):
    t = hdn.shape[0]
    tm, tn = 1024, 1024
    return _matmul(
        "grad_w_down", hdn, dr2b, TN, (D_MODEL // tn, D_FF // tm, 1),
        pl.BlockSpec((t, tm), lambda j, i, k: (0, i)),
        pl.BlockSpec((t, tn), lambda j, i, k: (0, j)),
        jax.ShapeDtypeStruct((N_DEV, D_FF // N_DEV, D_MODEL), F32),
        pl.BlockSpec((None, tm, tn), lambda j, i, k: (i, 0, j)), (tm, tn))


def _grad_w_up(h1b, du, dep=None):
    t = h1b.shape[0]
    tm, tn = 1024, 1024
    return _matmul(
        "grad_w_up", h1b, du, TN, (N_DEV, D_MODEL // tm, 1),
        pl.BlockSpec((t, tm), lambda i, j, k: (0, j)),
        pl.BlockSpec((t, tn), lambda i, j, k: (0, i)),
        jax.ShapeDtypeStruct((N_DEV, D_MODEL, D_FF // N_DEV), F32),
        pl.BlockSpec((None, tm, tn), lambda i, j, k: (i, j, 0)), (tm, tn), dep=dep)


def _mlp_up_bwd_ln1(du, w_up, dr2, r1, ln_g, dep=None):
    t = du.shape[0]
    tm, tn, tk = min(1024, t), 1024, D_FF // N_DEV
    deps, dep_specs = _dep_operand(dep)

    def mm_body(a_ref, b_ref, *rest):
        o_ref = rest[-1]
        k = pl.program_id(2)
        part = _dot(a_ref[:, :tk], b_ref[0], NT) + _dot(a_ref[:, tk:], b_ref[1], NT)

        @pl.when(k == 0)
        def _():
            o_ref[...] = part

        @pl.when(k > 0)
        def _():
            o_ref[...] += part

    dff = pl.pallas_call(
        mm_body, name="mlp_up_bwd", grid=(t // tm, D_MODEL // tn, N_DEV // 2),
        in_specs=[pl.BlockSpec((tm, 2 * tk), lambda i, j, k: (i, k)),
                  pl.BlockSpec((2, tn, tk), lambda i, j, k: (k, j, 0)), *dep_specs],
        out_specs=pl.BlockSpec((tm, tn), lambda i, j, k: (i, j)),
        out_shape=jax.ShapeDtypeStruct((t, D_MODEL), F32),
        compiler_params=_cparams(("parallel", "parallel", "arbitrary")),
    )(du, w_up, *deps)
    r = min(256, t)

    def body(acc_ref, dr2_ref, r1_ref, g_ref, dr1_ref, dr1b_ref, gg_ref, gb_ref):
        @pl.when(pl.program_id(0) == 0)
        def _():
            gg_ref[...] = jnp.zeros_like(gg_ref)
            gb_ref[...] = jnp.zeros_like(gb_ref)

        for r0, n in _row_chunks(r, 64):
            rs = slice(r0, r0 + n)
            dh1 = ALPHA * dr2_ref[rs, :] + acc_ref[rs, :]
            xhat, rstd = _ln_stats(r1_ref[rs, :])
            gg_ref[...] += jnp.sum(dh1 * xhat, axis=0, keepdims=True)
            gb_ref[...] += jnp.sum(dh1, axis=0, keepdims=True)
            dr1 = _ln_bwd(dh1 * g_ref[...], xhat, rstd)
            dr1_ref[rs, :] = dr1
            dr1b_ref[rs, :] = dr1.astype(BF16)

    row = pl.BlockSpec((r, D_MODEL), lambda i: (i, 0))
    vec = pl.BlockSpec((1, D_MODEL), lambda i: (0, 0))
    return pl.pallas_call(
        body, name="ln1_bwd", grid=(t // r,),
        in_specs=[row, row, row, vec],
        out_specs=[row, row, vec, vec],
        out_shape=[jax.ShapeDtypeStruct((t, D_MODEL), F32), jax.ShapeDtypeStruct((t, D_MODEL), BF16),
                   jax.ShapeDtypeStruct((1, D_MODEL), F32), jax.ShapeDtypeStruct((1, D_MODEL), F32)],
        compiler_params=_cparams(("arbitrary",)),
    )(dff, dr2, r1, ln_g)


def _dcat(dr1b, w_out):
    t = dr1b.shape[0]
    tm, tn = min(1024, t), 1024
    return _matmul(
        "dcat", dr1b, w_out, NT, (t // tm, D_MODEL // tn, 1),
        pl.BlockSpec((tm, D_MODEL), lambda i, j, k: (i, 0)),
        pl.BlockSpec((tn, D_MODEL), lambda i, j, k: (j, 0)),
        jax.ShapeDtypeStruct((t, D_MODEL), F32),
        pl.BlockSpec((tm, tn), lambda i, j, k: (i, j)), (tm, tn))


def _grad_w_out(cat, dr1b, dep=None):
    t = cat.shape[0]
    tm, tn = 1024, 1024
    return _matmul(
        "grad_w_out", cat, dr1b, TN, (D_MODEL // tm, D_MODEL // tn, 1),
        pl.BlockSpec((t, tm), lambda i, j, k: (0, i)),
        pl.BlockSpec((t, tn), lambda i, j, k: (0, j)),
        jax.ShapeDtypeStruct((N_DEV, D_MODEL // N_DEV, D_MODEL), F32),
        pl.BlockSpec((tm // (D_MODEL // N_DEV), D_MODEL // N_DEV, tn), lambda i, j, k: (i, 0, j)), (tm, tn), dep=dep)


def _gla_norm_bwd(dcat, o_raw, proj, norm_w, dep=None):
    t = o_raw.shape[0]
    r = min(512, t)

    deps, dep_specs = _dep_operand(dep)

    def body(d_ref, o_ref, g_ref, w_ref, *rest):
        do_ref, dg_ref, dw_ref = rest[len(deps):]

        @pl.when(pl.program_id(0) == 0)
        def _():
            dw_ref[...] = jnp.zeros_like(dw_ref)

        w = w_ref[...]
        dw = jnp.zeros((1, GLA_DV), F32)
        for h in range(GLA_HEADS):
            sl = slice(h * GLA_DV, (h + 1) * GLA_DV)
            o = o_ref[:, sl]
            g = g_ref[:, sl]
            d = d_ref[:, sl]
            rr = lax.rsqrt(jnp.mean(o * o, axis=-1, keepdims=True) + RMS_EPS)
            on = o * rr
            sg = jax.nn.sigmoid(g)
            sil = g * sg
            dg_ref[:, sl] = (d * on * w * (sg * (1.0 + g * (1.0 - sg)))).astype(BF16)
            dw = dw + jnp.sum(d * on * sil, axis=0, keepdims=True)
            don = d * w * sil
            do_ref[:, sl] = rr * (don - on * jnp.mean(don * on, axis=-1, keepdims=True))
        dw_ref[...] += dw

    return pl.pallas_call(
        body, name="gla_norm_bwd", grid=(t // r,),
        in_specs=[pl.BlockSpec((r, D_GLA), lambda i: (i, 0)),
                  pl.BlockSpec((r, D_GLA), lambda i: (i, 0)),
                  pl.BlockSpec((r, D_GLA), lambda i: (i, C_GG // D_GLA)),
                  pl.BlockSpec((1, GLA_DV), lambda i: (0, 0)), *dep_specs],
        out_specs=[pl.BlockSpec((r, D_GLA), lambda i: (i, 0)),
                   pl.BlockSpec((r, D_GLA), lambda i: (i, 0)),
                   pl.BlockSpec((1, GLA_DV), lambda i: (0, 0))],
        out_shape=[jax.ShapeDtypeStruct((t, D_GLA), F32), jax.ShapeDtypeStruct((t, D_GLA), BF16),
                   jax.ShapeDtypeStruct((1, GLA_DV), F32)],
        compiler_params=_cparams(("arbitrary",)),
    )(dcat, o_raw, proj, norm_w, *deps)


def _gla_bwd(proj, bloc, do_raw, states):
    t = proj.shape[0]
    r = min(256, t)
    ncb = r // GC
    nb = t // r
    scale = GLA_DK ** -0.5

    def body(q_ref, k_ref, v_ref, b_ref, do_ref, st_ref, dq_ref, dk_ref, dv_ref, db_ref, ds_scr):
        @pl.when(pl.program_id(1) == 0)
        def _():
            ds_scr[...] = jnp.zeros_like(ds_scr)

        rows = lax.broadcasted_iota(jnp.int32, (GC, 1), 0)

        def chunk(cc, carry):
            c = ncb - 1 - cc
            r0 = pl.multiple_of(c * GC, GC)
            q = q_ref[pl.ds(r0, GC), :] * scale
            k = k_ref[pl.ds(r0, GC), :]
            v = v_ref[pl.ds(r0, GC), :]
            b = b_ref[pl.ds(r0, GC), :]
            do = do_ref[pl.ds(r0, GC), :]
            st = st_ref[c]
            dsn = ds_scr[...]
            bl = b[GC - 1:GC, :]
            eb = jnp.exp(b)
            ekl = jnp.exp(bl - b)
            ebl = jnp.exp(bl)
            qh = q * eb
            kd = k * ekl
            dob = do.astype(BF16)
            dsb = dsn.astype(BF16)
            dqh = _dot(dob, st.astype(BF16), NN)
            dkd = _dot(v.astype(BF16), dsb, NN)
            dv = _dot(kd.astype(BF16), dsb, NT)
            dq_i = jnp.zeros((GC, GLA_DK), F32)
            dk_i = jnp.zeros((GC, GLA_DK), F32)
            dk_x = jnp.zeros((GC, GLA_DK), F32)
            kr = _bf16_round(k)
            vr = _bf16_round(v)
            dor = _bf16_round(do)
            for i in range(GC):
                w = jnp.exp(jnp.where(rows <= i, b[i:i + 1, :] - b, NEG))
                qw = q[i:i + 1, :] * w
                qwr = _bf16_round(qw)
                a = _bf16_round(jnp.sum(qwr * kr, axis=-1, keepdims=True))
                da = jnp.sum(vr * dor[i:i + 1, :], axis=-1, keepdims=True)
                dv = dv + a * dor[i:i + 1, :]
                dk_x = dk_x + da * qw
                dk_i = dk_i + _bf16_round(da) * qwr
                dq_i = jnp.where(rows == i, jnp.sum(da * (w * k), axis=0, keepdims=True), dq_i)
            dqs = dqh * eb + dq_i
            dk = dkd * ekl + dk_i
            db_last = jnp.sum(dkd * kd, axis=0, keepdims=True) + ebl * jnp.sum(dsn * st, axis=0, keepdims=True)
            db = q * dqs - k * (dkd * ekl + dk_x) + jnp.where(rows == GC - 1, db_last, 0.0)
            dq_ref[pl.ds(r0, GC), :] = (dqs * scale).astype(BF16)
            dk_ref[pl.ds(r0, GC), :] = dk.astype(BF16)
            dv_ref[pl.ds(r0, GC), :] = dv.astype(BF16)
            db_ref[pl.ds(r0, GC), :] = db
            ds_scr[...] = dsn * ebl + _dot(dob, qh.astype(BF16), TN)
            return carry

        lax.fori_loop(0, ncb, chunk, 0)

    rev = lambda i: nb - 1 - i
    return pl.pallas_call(
        body, name="gla_bwd", grid=(GLA_HEADS, nb),
        in_specs=[pl.BlockSpec((r, GLA_DK), lambda h, i: (rev(i), C_QG // GLA_DK + h)),
                  pl.BlockSpec((r, GLA_DK), lambda h, i: (rev(i), C_KG // GLA_DK + h)),
                  pl.BlockSpec((r, GLA_DV), lambda h, i: (rev(i), C_VG // GLA_DV + h)),
                  pl.BlockSpec((r, GLA_DK), lambda h, i: (rev(i), h)),
                  pl.BlockSpec((r, GLA_DV), lambda h, i: (rev(i), h)),
                  pl.BlockSpec((ncb, None, GLA_DV, GLA_DK), lambda h, i: (rev(i), h, 0, 0))],
        out_specs=[pl.BlockSpec((r, GLA_DK), lambda h, i: (rev(i), h)),
                   pl.BlockSpec((r, GLA_DK), lambda h, i: (rev(i), h)),
                   pl.BlockSpec((r, GLA_DV), lambda h, i: (rev(i), h)),
                   pl.BlockSpec((r, GLA_DK), lambda h, i: (rev(i), h))],
        out_shape=[jax.ShapeDtypeStruct((t, GLA_KW), BF16), jax.ShapeDtypeStruct((t, GLA_KW), BF16),
                   jax.ShapeDtypeStruct((t, D_GLA), BF16), jax.ShapeDtypeStruct((t, GLA_KW), F32)],
        scratch_shapes=[pltpu.VMEM((GLA_DV, GLA_DK), F32)],
        compiler_params=_cparams(("parallel", "arbitrary")),
    )(proj, proj, proj, bloc, do_raw, states)


def _gate_bwd(db, dgdz, proj, w2p, triu):
    t = db.shape[0]
    r = triu.shape[0]

    def body(db_ref, s_ref, lo_ref, w_ref, u_ref, dlo_ref, gw_ref, gb_ref):
        @pl.when(pl.program_id(0) == 0)
        def _():
            gw_ref[...] = jnp.zeros_like(gw_ref)
            gb_ref[...] = jnp.zeros_like(gb_ref)

        dz = _dot(u_ref[...], db_ref[...], NN, precision=lax.Precision.HIGHEST) * s_ref[...]
        dzb = dz.astype(BF16)
        gb_ref[...] += jnp.sum(dz, axis=0, keepdims=True)
        gw_ref[...] += _dot(lo_ref[...].astype(BF16), dzb, TN)
        dlo_ref[...] = _dot(dzb, w_ref[...], NT).astype(BF16)

    return pl.pallas_call(
        body, name="gate_bwd", grid=(t // r,),
        in_specs=[pl.BlockSpec((r, GLA_KW), lambda i: (i, 0)),
                  pl.BlockSpec((r, GLA_KW), lambda i: (i, 0)),
                  pl.BlockSpec((r, LANES), lambda i: (i, C_LO // LANES)),
                  pl.BlockSpec((LANES, GLA_KW), lambda i: (0, 0)),
                  pl.BlockSpec((r, r), lambda i: (0, 0))],
        out_specs=[pl.BlockSpec((r, LANES), lambda i: (i, 0)),
                   pl.BlockSpec((LANES, GLA_KW), lambda i: (0, 0)),
                   pl.BlockSpec((1, GLA_KW), lambda i: (0, 0))],
        out_shape=[jax.ShapeDtypeStruct((t, LANES), BF16), jax.ShapeDtypeStruct((LANES, GLA_KW), F32),
                   jax.ShapeDtypeStruct((1, GLA_KW), F32)],
        compiler_params=_cparams(("arbitrary",)),
    )(db, dgdz, proj, w2p, triu)


def _swa_bwd(proj, dcat, sinks, dep=None):
    t = proj.shape[0]

    deps, dep_specs = _dep_operand(dep)

    def body(sink_ref, q_ref, kp_ref, kc_ref, vp_ref, vc_ref, d_ref, *rest):
        dq_ref, dk_ref, dv_ref, dsink_ref = rest[len(deps):]
        i = pl.program_id(0)

        @pl.when(i == 0)
        def _():
            dk_ref[...] = jnp.zeros_like(dk_ref)
            dv_ref[...] = jnp.zeros_like(dv_ref)
            dsink_ref[...] = jnp.zeros_like(dsink_ref)

        valid, lo = _swa_masks(i, SWA_GROUP)
        lane = lax.broadcasted_iota(jnp.int32, (1, LANES), 1)
        kb = jnp.concatenate([kp_ref[...], kc_ref[...]], axis=0)
        vb = jnp.concatenate([vp_ref[...], vc_ref[...]], axis=0)
        dsink = jnp.zeros((1, LANES), F32)
        folded_k, folded_v = [], []
        for kv in range(2):
            kdup = _dup_half(kb, lo, kv).astype(BF16)
            vdup = _dup_half(vb, lo, kv).astype(BF16)
            qm = _swa_stack(q_ref, 0, lo, kv)
            dom = _swa_stack(d_ref, 0, lo, kv)
            pn, psink = _swa_probs(qm, kdup, valid, _swa_sinks(sink_ref, kv))
            dpr = _dot(dom, vdup, NT)
            drow = jnp.sum(dpr * pn, axis=-1, keepdims=True)
            ds_col = psink * drow
            for h in range(SWA_GROUP):
                dsink = dsink + jnp.where(lane == SWA_GROUP * kv + h,
                                          -jnp.sum(ds_col[h * WINDOW:(h + 1) * WINDOW, :]), 0.0)
            dsb = (pn * (dpr - drow) * (SWA_DH ** -0.5)).astype(BF16)
            dq_all = _dot(dsb, kdup, NN)
            for p in range(4):
                c0 = LANES * (4 * kv + p)
                dq_ref[:, c0:c0 + LANES] = _swa_unstack(dq_all, lo, p).astype(BF16)
            dkd = _dot(dsb, qm, TN)
            dvd = _dot(pn.astype(BF16), dom, TN)
            folded_k.append(dkd + pltpu.roll(dkd, SWA_DH, axis=1))
            folded_v.append(dvd + pltpu.roll(dvd, SWA_DH, axis=1))
        dkb = jnp.where(lo, folded_k[0], folded_k[1])
        dvb = jnp.where(lo, folded_v[0], folded_v[1])
        dsink_ref[...] += dsink
        cur = pl.ds(pl.multiple_of(i * WINDOW, WINDOW), WINDOW)
        dk_ref[cur, :] += dkb[WINDOW:, :]
        dv_ref[cur, :] += dvb[WINDOW:, :]

        @pl.when(i > 0)
        def _():
            prev = pl.ds(pl.multiple_of((i - 1) * WINDOW, WINDOW), WINDOW)
            dk_ref[prev, :] += dkb[:WINDOW, :]
            dv_ref[prev, :] += dvb[:WINDOW, :]

    kvspec = lambda col, prev: pl.BlockSpec(
        (WINDOW, LANES), (lambda i: (jnp.maximum(i - 1, 0), col)) if prev else (lambda i: (i, col)))
    full = pl.BlockSpec((t, LANES), lambda i: (0, 0))
    return pl.pallas_call(
        body, name="swa_bwd", grid=(t // WINDOW,),
        in_specs=[pl.BlockSpec(memory_space=pltpu.SMEM),
                  pl.BlockSpec((WINDOW, D_SWA), lambda i: (i, C_QS // D_SWA)),
                  kvspec(C_KS // LANES, True), kvspec(C_KS // LANES, False),
                  kvspec(C_VS // LANES, True), kvspec(C_VS // LANES, False),
                  pl.BlockSpec((WINDOW, D_SWA), lambda i: (i, 1)), *dep_specs],
        out_specs=[pl.BlockSpec((WINDOW, D_SWA), lambda i: (i, 0)), full, full,
                   pl.BlockSpec((1, LANES), lambda i: (0, 0))],
        out_shape=[jax.ShapeDtypeStruct((t, D_SWA), BF16), jax.ShapeDtypeStruct((t, LANES), F32),
                   jax.ShapeDtypeStruct((t, LANES), F32), jax.ShapeDtypeStruct((1, LANES), F32)],
        compiler_params=_cparams(("arbitrary",)),
    )(sinks, proj, proj, proj, proj, proj, dcat, *deps)


def _grad_w_in(xb, dproj):
    t = xb.shape[0]
    tm, tn = 640, 1024
    return _matmul(
        "grad_w_in", dproj, xb, TN, (D_INP // tm, D_MODEL // tn, 1),
        pl.BlockSpec((t, tm), lambda i, j, k: (0, i)),
        pl.BlockSpec((t, tn), lambda i, j, k: (0, j)),
        jax.ShapeDtypeStruct((D_INP, D_MODEL), F32),
        pl.BlockSpec((tm, tn), lambda i, j, k: (i, j)), (tm, tn))


def _grad_x(dproj, w_in_t, dr1, dep=None):
    t = dproj.shape[0]
    tm, tn = min(512, t), 1024

    def epilogue(acc_ref, extra, outs):
        for r0, n in _row_chunks(tm):
            rs = slice(r0, r0 + n)
            outs[0][rs, :] = ALPHA * extra[0][rs, :] + acc_ref[rs, :]

    blk = pl.BlockSpec((tm, tn), lambda j, i, k: (i, j))
    return _matmul(
        "grad_x", dproj, w_in_t, NN, (D_MODEL // tn, t // tm, 1),
        pl.BlockSpec((tm, D_INP), lambda j, i, k: (i, 0)),
        pl.BlockSpec((D_INP, tn), lambda j, i, k: (0, j)),
        jax.ShapeDtypeStruct((t, D_MODEL), F32), blk, (tm, tn),
        extra=(dr1,), extra_specs=(blk,), epilogue=epilogue, dep=dep)


def _place():
    x, y, c = lax.axis_index("x"), lax.axis_index("y"), lax.axis_index("c")
    chips = [(1 - x, y), (x, 1 - y), (1 - x, 1 - y)]
    return x, y, c, chips


def _all_gather(shards):
    n = len(shards)
    hbm = pl.BlockSpec(memory_space=pl.ANY)

    def body(*refs):
        ins, outs = refs[:n], refs[n:2 * n]
        send, recv, loc = refs[2 * n:]
        x, y, c, chips = _place()
        me, sib = (x, y, c), (x, y, 1 - c)

        def slot(t, px, py, pc):
            return outs[t].at[4 * px + 2 * py + pc]

        def copy(t, k, block, to, src=None):
            return pltpu.make_async_remote_copy(
                src_ref=slot(t, *block) if src is None else src, dst_ref=slot(t, *block),
                send_sem=send.at[7 * t + k], recv_sem=recv.at[7 * t + k], device_id=to, device_id_type=MESH)

        mine = [pltpu.make_async_copy(ins[t], slot(t, *me), loc.at[t]) for t in range(n)]
        for cp in mine:
            cp.start()
        sent = []
        for t in range(n):
            sent.append(copy(t, 0, me, sib, src=ins[t]))
            sent += [copy(t, 1 + j, me, (*chip, c), src=ins[t]) for j, chip in enumerate(chips)]
        for cp in sent:
            cp.start()
        for t in range(n):
            for j, chip in enumerate(chips):
                copy(t, 1 + j, (*chip, c), me).wait_recv()
                fwd = copy(t, 4 + j, (*chip, c), sib)
                fwd.start()
                sent.append(fwd)
        for t in range(n):
            copy(t, 0, sib, me).wait_recv()
            for j, chip in enumerate(chips):
                copy(t, 4 + j, (*chip, 1 - c), me).wait_recv()
        for cp in sent:
            cp.wait_send()
        for cp in mine:
            cp.wait()

    return pl.pallas_call(
        body, name="all_gather_weights",
        in_specs=[hbm] * n, out_specs=[hbm] * n,
        out_shape=[jax.ShapeDtypeStruct((N_DEV,) + s.shape, s.dtype) for s in shards],
        scratch_shapes=[pltpu.SemaphoreType.DMA((7 * n,)), pltpu.SemaphoreType.DMA((7 * n,)),
                        pltpu.SemaphoreType.DMA((n,))],
    )(*shards)


def _plan_gather_out(src, land, x, y, c, chips):
    me = 4 * x + 2 * y + c
    return [(src, land.at[me], to) for to in [(x, y, 1 - c)] + [(px, py, c) for px, py in chips]]


def _plan_gather_forward(src, land, x, y, c, chips):
    return [(land.at[4 * px + 2 * py + c], land.at[4 * px + 2 * py + c], (x, y, 1 - c)) for px, py in chips]


def _plan_sibling(src, land, x, y, c, chips):
    return [(src.at[2 * q + (1 - c)], land.at[q], (x, y, 1 - c)) for q in range(4)]


def _plan_chips(src, land, x, y, c, chips):
    return [(src.at[j], land.at[j], (px, py, c)) for j, (px, py) in enumerate(chips)]


_PLAN_COPIES = {_plan_gather_out: 4, _plan_gather_forward: 3, _plan_sibling: 4, _plan_chips: 3}
_HBM = pl.BlockSpec(memory_space=pltpu.HBM)
_SEM = pl.BlockSpec(memory_space=pltpu.SEMAPHORE)
_EFFECT = pltpu.SideEffectType.DATAFLOW_SIDE_EFFECTING


def _hbm(a):
    return pltpu.with_memory_space_constraint(a, pltpu.HBM)


def _plan_descriptors(plans, srcs, lands, send, recv):
    x, y, c, chips = _place()
    cps = []
    for plan, src, land in zip(plans, srcs, lands):
        for s_ref, d_ref, to in plan(src, land, x, y, c, chips):
            k = len(cps)
            cps.append(pltpu.make_async_remote_copy(src_ref=s_ref, dst_ref=d_ref, send_sem=send.at[k],
                                                    recv_sem=recv.at[k], device_id=to, device_id_type=MESH))
    return cps


def _copies_start(name, plans, srcs, lands, after=None):
    has_src = [s is not None for s in srcs]
    arrays = [s for s in srcs if s is not None] + list(lands)
    n_src = sum(has_src)
    n_cp = sum(_PLAN_COPIES[p] for p in plans)
    afters = [] if after is None else [after]

    def body(*refs):
        ins = refs[:len(arrays)]
        send, recv = refs[len(arrays) + len(afters)], refs[len(arrays) + len(afters) + 1]
        token = refs[-1]
        it = iter(ins[:n_src])
        src_refs = [next(it) if h else None for h in has_src]
        for cp in _plan_descriptors(plans, src_refs, ins[n_src:], send, recv):
            cp.start()
        token[...] = jnp.zeros_like(token)

    outs = pl.pallas_call(
        body, name=name,
        in_specs=[_HBM] * len(arrays) + [pl.BlockSpec(memory_space=pl.ANY)] * len(afters),
        out_specs=(_SEM, _SEM, *[_HBM] * len(arrays), pl.BlockSpec(memory_space=pltpu.VMEM)),
        out_shape=(pltpu.SemaphoreType.DMA((n_cp,)), pltpu.SemaphoreType.DMA((n_cp,)),
                   *[pltpu.HBM(a.shape, a.dtype) for a in arrays], jax.ShapeDtypeStruct((8, LANES), F32)),
        input_output_aliases={i: 2 + i for i in range(len(arrays))},
        compiler_params=pltpu.CompilerParams(has_side_effects=_EFFECT),
    )(*[_hbm(a) for a in arrays], *afters)
    send, recv = outs[0], outs[1]
    thru = list(outs[2:-1])
    it = iter(thru[:n_src])
    return send, recv, [next(it) if h else None for h in has_src], thru[n_src:], outs[-1]


def _copies_wait(name, plans, started, after):
    send, recv, srcs, lands, _ = started
    has_src = [s is not None for s in srcs]
    arrays = [s for s in srcs if s is not None] + list(lands)
    n_src = sum(has_src)

    def body(*refs):
        ins = refs[:len(arrays)]
        send_ref, recv_ref = refs[len(arrays)], refs[len(arrays) + 1]
        it = iter(ins[:n_src])
        src_refs = [next(it) if h else None for h in has_src]
        for cp in _plan_descriptors(plans, src_refs, ins[n_src:], send_ref, recv_ref):
            cp.wait_send()
            cp.wait_recv()

    outs = pl.pallas_call(
        body, name=name,
        in_specs=[_HBM] * len(arrays) + [_SEM, _SEM, pl.BlockSpec(memory_space=pl.ANY)],
        out_specs=tuple([_HBM] * len(arrays)),
        out_shape=tuple(pltpu.HBM(a.shape, a.dtype) for a in arrays),
        input_output_aliases={i: i for i in range(len(arrays))},
        compiler_params=pltpu.CompilerParams(has_side_effects=_EFFECT),
    )(*arrays, send, recv, after)
    return list(outs[:n_src]), list(outs[n_src:])


def _shard_tiles(rows, cols, tr):
    if rows % tr == 0:
        return (tr, cols), rows // tr, lambda r: (r, 0)
    tc = 2 * LANES
    return (rows, tc), cols // tc, lambda r: (0, r)


def _pair_sum(name, grad, from_sibling, blocks):
    _, rows, cols = grad.shape
    (br_, bc), steps, at = _shard_tiles(rows, cols, 256)

    def body(blk_ref, g_ref, s_ref, o_ref):
        o_ref[...] = (g_ref[...] + s_ref[...]).astype(BF16)

    return pl.pallas_call(
        body, name=name,
        grid_spec=pltpu.PrefetchScalarGridSpec(
            num_scalar_prefetch=1, grid=(3, steps),
            in_specs=[pl.BlockSpec((None, br_, bc), lambda j, r, br: (br[j], *at(r))),
                      pl.BlockSpec((None, br_, bc), lambda j, r, br: (br[3 + j], *at(r)))],
            out_specs=pl.BlockSpec((None, br_, bc), lambda j, r, br: (j, *at(r)))),
        out_shape=jax.ShapeDtypeStruct((3, rows, cols), BF16),
        compiler_params=_cparams(("parallel", "parallel")),
    )(blocks, grad, from_sibling)


def _adam_math(g, w, m, v):
    m2 = ADAM_B1 * m + (1.0 - ADAM_B1) * g
    v2 = ADAM_B2 * v + (1.0 - ADAM_B2) * (g * g)
    m_hat = m2 / (1.0 - ADAM_B1 ** ADAM_STEP)
    v_hat = v2 / (1.0 - ADAM_B2 ** ADAM_STEP)
    delta = -ADAM_LR * (m_hat / (jnp.sqrt(v_hat) + ADAM_EPS) + ADAM_WD * w)
    return delta, m2, v2


def _sum_adam(name, grad, from_sibling, from_chips, own, w, m, v, dep=None):
    rows, cols = w.shape
    (br_, bc), steps, at = _shard_tiles(rows, cols, 128)

    deps, dep_specs = _dep_operand(dep)

    def body(own_ref, p_ref, s_ref, r_ref, w_ref, m_ref, v_ref, *rest):
        g_out, d_out, m_out, v_out = rest[len(deps):]
        g = p_ref[...] + s_ref[...]
        for j in range(3):
            g = g + r_ref[j].astype(F32)
        d, m2, v2 = _adam_math(g, w_ref[...], m_ref[...], v_ref[...])
        g_out[...] = g
        d_out[...] = d
        m_out[...] = m2
        v_out[...] = v2

    blk = pl.BlockSpec((br_, bc), lambda r, cr: at(r))
    shp = jax.ShapeDtypeStruct((rows, cols), F32)
    return pl.pallas_call(
        body, name=name,
        grid_spec=pltpu.PrefetchScalarGridSpec(
            num_scalar_prefetch=1, grid=(steps,),
            in_specs=[pl.BlockSpec((None, br_, bc), lambda r, cr: (cr[0], *at(r))),
                      pl.BlockSpec((None, br_, bc), lambda r, cr: (cr[1], *at(r))),
                      pl.BlockSpec((3, br_, bc), lambda r, cr: (0, *at(r))),
                      blk, blk, blk, *dep_specs],
            out_specs=[blk, blk, blk, blk]),
        out_shape=[shp, shp, shp, shp],
        compiler_params=_cparams(("parallel",)),
    )(own, grad, from_sibling, from_chips, w, m, v, *deps)


def _adam_small(name, g, w, m, v):
    def body(g_ref, w_ref, m_ref, v_ref, d_out, m_out, v_out):
        d, m2, v2 = _adam_math(g_ref[...], w_ref[...], m_ref[...], v_ref[...])
        d_out[...] = d
        m_out[...] = m2
        v_out[...] = v2

    shp = jax.ShapeDtypeStruct(w.shape, F32)
    return pl.pallas_call(body, name=name, out_shape=[shp, shp, shp])(g, w, m, v)


def _all_reduce_small(pack):
    rows = pack.shape[0]

    def body(in_ref, out_ref, slots, send, recv):
        x, y, c, _ = _place()
        me = 4 * x + 2 * y + c
        slots[me] = in_ref[...]
        cps = []
        for k in range(1, N_DEV):
            dx, dy, dc = (k >> 2) & 1, (k >> 1) & 1, k & 1
            to = (jnp.bitwise_xor(x, dx), jnp.bitwise_xor(y, dy), jnp.bitwise_xor(c, dc))
            cps.append(pltpu.make_async_remote_copy(
                src_ref=in_ref, dst_ref=slots.at[me], send_sem=send.at[k - 1], recv_sem=recv.at[k - 1],
                device_id=to, device_id_type=MESH))
        for cp in cps:
            cp.start()
        for cp in cps:
            cp.wait()
        acc = slots[0]
        for d in range(1, N_DEV):
            acc = acc + slots[d]
        out_ref[...] = acc

    return pl.pallas_call(
        body, name="all_reduce_small",
        in_specs=[pl.BlockSpec(memory_space=pltpu.VMEM)],
        out_specs=pl.BlockSpec(memory_space=pltpu.VMEM),
        out_shape=jax.ShapeDtypeStruct((rows, LANES), F32),
        scratch_shapes=[pltpu.VMEM((N_DEV, rows, LANES), F32),
                        pltpu.SemaphoreType.DMA((N_DEV - 1,)), pltpu.SemaphoreType.DMA((N_DEV - 1,))],
    )(pack)


def _rows128(a):
    flat = a.reshape(-1)
    padn = (-flat.shape[0]) % (8 * LANES)
    if padn:
        flat = jnp.concatenate([flat, jnp.zeros((padn,), flat.dtype)])
    return flat.reshape(-1, LANES)


def kernel(x, w_in, w_gk2, b_gk, gla_norm_w, swa_sinks, w_out, ln1_g, ln1_b, w_up, w_down, ln2_g, ln2_b, loss_target, m_w_in, m_w_gk2, m_b_gk, m_gla_norm_w, m_swa_sinks, m_w_out, m_ln1_g, m_ln1_b, m_w_up, m_w_down, m_ln2_g, m_ln2_b, v_w_in, v_w_gk2, v_b_gk, v_gla_norm_w, v_swa_sinks, v_w_out, v_ln1_g, v_ln1_b, v_w_up, v_w_down, v_ln2_g, v_ln2_b):
    xc, yc, cc = lax.axis_index("x"), lax.axis_index("y"), lax.axis_index("c")
    me = 4 * xc + 2 * yc + cc

    x2 = x[0]
    t = x2.shape[0]
    xb = x2.astype(BF16)
    target = loss_target[0]

    first = _all_gather([w_in[0].T.astype(BF16), w_gk2[0].astype(BF16)])
    w_in_t = _to_padded_rows(first[0].reshape(D_IN, D_MODEL))
    w2 = first[1].transpose(1, 0, 2).reshape(GATE_RANK, GLA_KW)
    w2p = jnp.concatenate([w2, jnp.zeros((LANES - GATE_RANK, GLA_KW), BF16)], axis=0)
    shards = [w_out[0].astype(BF16), w_up[0].astype(BF16), w_down[0].astype(BF16)]
    lands = [lax.dynamic_update_index_in_dim(lax.empty((N_DEV,) + sh.shape, BF16), sh, me, 0) for sh in shards]
    out_plans = [_plan_gather_out] * 3
    fwd_plans = [_plan_gather_forward] * 3
    ag1 = _copies_start("gather_out_start", out_plans, shards, lands, first[0])

    proj = _proj(xb, w_in_t, dep=ag1[4])
    rt = min(256, t)
    ii = jnp.arange(rt)
    tri = ((ii[:, None] // GC == ii[None, :] // GC) & (ii[None, :] <= ii[:, None])).astype(F32)
    bloc, dgdz = _gate_fwd(proj, w2p, b_gk, tri)
    o_raw, states = _gla_fwd(proj, bloc)
    _, lands = _copies_wait("gather_out_wait", out_plans, ag1, o_raw)
    ag2 = _copies_start("gather_forward_start", fwd_plans, [None] * 3, lands, o_raw)
    gla_out = _gla_norm_fwd(o_raw, proj, gla_norm_w)
    cat = _swa_fwd(proj, swa_sinks[0], gla_out, dep=ag2[4])
    _, gathered = _copies_wait("gather_forward_wait", fwd_plans, ag2, cat)
    w_out_f = gathered[0].reshape(D_MODEL, D_MODEL)
    w_up_f = gathered[1]
    w_down_f = gathered[2].reshape(D_FF, D_MODEL)
    r1, h1, h1b = _mix_ln1(cat, w_out_f, x2, ln1_g, ln1_b)
    a_act, hdn = _mlp_up(h1b, w_up_f)
    dr2, dr2b, g_ln2_g, g_ln2_b, loss_part = _mlp_down_loss(hdn, w_down_f, h1, target, ln2_g, ln2_b)

    others = [2 * (1 - xc) + yc, 2 * xc + (1 - yc), 2 * (1 - xc) + (1 - yc)]
    blocks = jnp.stack([2 * q + cc for q in others] + others).astype(jnp.int32)
    own = jnp.stack([me, 2 * xc + yc]).astype(jnp.int32)
    wmv = dict(w_in=[a[0].T for a in (w_in, m_w_in, v_w_in)], w_out=[a[0] for a in (w_out, m_w_out, v_w_out)],
               w_up=[a[0] for a in (w_up, m_w_up, v_w_up)], w_down=[a[0] for a in (w_down, m_w_down, v_w_down)])
    big = {}

    def sib_land(g):
        return lax.empty((4,) + g.shape[1:], F32)

    def chip_land(g):
        return lax.empty((3,) + g.shape[1:], BF16)

    def finish(nm, g, from_sib, from_chips, dep=None):
        outs = _sum_adam("sum_adam_" + nm, g, from_sib, from_chips, own, *wmv[nm], dep=dep)
        big[nm] = [(o.T if nm == "w_in" else o)[None] for o in outs]

    du = _mlp_down_bwd(dr2b, w_down_f, a_act)
    g_down = _grad_w_down(hdn, dr2b)
    sa_down = _copies_start("sibling_down_start", [_plan_sibling], [g_down], [sib_land(g_down)])
    g_up = _grad_w_up(h1b, du, dep=sa_down[4])
    (g_down,), (fs_down,) = _copies_wait("sibling_down_wait", [_plan_sibling], sa_down, g_up)
    p_down = _pair_sum("pair_sum_w_down", g_down, fs_down, blocks)
    sb_down = _copies_start("chips_down_sibling_up_start", [_plan_chips, _plan_sibling], [p_down, g_up],
                            [chip_land(g_down), sib_land(g_up)])
    dr1, dr1b, g_ln1_g, g_ln1_b = _mlp_up_bwd_ln1(du, w_up_f, dr2, r1, ln1_g, dep=sb_down[4])
    dcat = _dcat(dr1b, w_out_f)
    (_, g_up), (fc_down, fs_up) = _copies_wait("chips_down_sibling_up_wait", [_plan_chips, _plan_sibling], sb_down,
                                               dcat)
    finish("w_down", g_down, fs_down, fc_down)
    p_up = _pair_sum("pair_sum_w_up", g_up, fs_up, blocks)
    sb_up = _copies_start("chips_up_start", [_plan_chips], [p_up], [chip_land(g_up)])
    do_raw, dg_g, g_norm_w = _gla_norm_bwd(dcat, o_raw, proj, gla_norm_w, dep=sb_up[4])
    dq_g, dk_g, dv_g, dgk = _gla_bwd(proj, bloc, do_raw, states)
    _, (fc_up,) = _copies_wait("chips_up_wait", [_plan_chips], sb_up, dgk)
    finish("w_up", g_up, fs_up, fc_up)
    dlo, gw2_p, g_b_gk = _gate_bwd(dgk, dgdz, proj, w2p, tri.T)
    dq_s, dk_s, dv_s, g_sinks = _swa_bwd(proj, dcat, swa_sinks[0])
    dproj = jnp.concatenate([dq_g, dk_g, dv_g, dg_g, dq_s, dk_s.astype(BF16), dv_s.astype(BF16), dlo], axis=-1)
    gw_in_t = _grad_w_in(xb, dproj)
    g_in = _from_padded_rows(gw_in_t).reshape(N_DEV, D_IN // N_DEV, D_MODEL)
    sa_in = _copies_start("sibling_in_start", [_plan_sibling], [g_in], [sib_land(g_in)])
    g_out = _grad_w_out(cat, dr1b, dep=sa_in[4])
    (g_in,), (fs_in,) = _copies_wait("sibling_in_wait", [_plan_sibling], sa_in, g_out)
    p_in = _pair_sum("pair_sum_w_in", g_in, fs_in, blocks)
    sb_in = _copies_start("chips_in_sibling_out_start", [_plan_chips, _plan_sibling], [p_in, g_out],
                          [chip_land(g_in), sib_land(g_out)])
    grad_x = _grad_x(dproj, w_in_t, dr1, dep=sb_in[4])
    (_, g_out), (fc_in, fs_out) = _copies_wait("chips_in_sibling_out_wait", [_plan_chips, _plan_sibling], sb_in,
                                               grad_x)
    p_out = _pair_sum("pair_sum_w_out", g_out, fs_out, blocks)
    sb_out = _copies_start("chips_out_start", [_plan_chips], [p_out], [chip_land(g_out)])
    finish("w_in", g_in, fs_in, fc_in, dep=sb_out[4])

    pieces = [loss_part, g_b_gk, g_norm_w, g_sinks[:, :SWA_HEADS], g_ln1_g, g_ln1_b, g_ln2_g, g_ln2_b,
              gw2_p[:GATE_RANK]]
    pack = jnp.concatenate([_rows128(p) for p in pieces], axis=0)
    tot = _all_reduce_small(pack + sb_out[4][:1, :1])
    sizes = [p.size for p in pieces]
    offs = [0]
    for p in pieces:
        offs.append(offs[-1] + _rows128(p).shape[0])
    unpack = lambda i, shape: tot[offs[i]:offs[i + 1]].reshape(-1)[:sizes[i]].reshape(shape)
    loss = tot[0, 0]
    small_names = ["b_gk", "gla_norm_w", "swa_sinks", "ln1_g", "ln1_b", "ln2_g", "ln2_b"]
    small_w = dict(b_gk=(b_gk, m_b_gk, v_b_gk), gla_norm_w=(gla_norm_w, m_gla_norm_w, v_gla_norm_w),
                   swa_sinks=(swa_sinks, m_swa_sinks, v_swa_sinks), ln1_g=(ln1_g, m_ln1_g, v_ln1_g),
                   ln1_b=(ln1_b, m_ln1_b, v_ln1_b), ln2_g=(ln2_g, m_ln2_g, v_ln2_g),
                   ln2_b=(ln2_b, m_ln2_b, v_ln2_b))
    small_g = {nm: unpack(1 + i, small_w[nm][0].shape) for i, nm in enumerate(small_names)}
    g_pack = jnp.concatenate([_rows128(small_g[nm]) for nm in small_names], axis=0)
    small_wmv = [jnp.concatenate([_rows128(small_w[nm][k]) for nm in small_names], axis=0) for k in range(3)]
    small_out = _adam_small("adam_replicated", g_pack, *small_wmv)
    srow = [0]
    for nm in small_names:
        srow.append(srow[-1] + _rows128(small_w[nm][0]).shape[0])
    small = {}
    for i, nm in enumerate(small_names):
        shape = small_w[nm][0].shape
        n = small_w[nm][0].size
        small[nm] = [small_g[nm]] + [o[srow[i]:srow[i + 1]].reshape(-1)[:n].reshape(shape) for o in small_out]

    gw2_full = unpack(8, (GATE_RANK, GLA_KW))
    gw2_loc = lax.dynamic_slice_in_dim(gw2_full, me * (GLA_KW // N_DEV), GLA_KW // N_DEV, axis=1)
    gk2_out = _adam_small("adam_w_gk2", gw2_loc, w_gk2[0], m_w_gk2[0], v_w_gk2[0])
    big["w_gk2"] = [gw2_loc[None]] + [o[None] for o in gk2_out]

    _, (fc_out,) = _copies_wait("chips_out_wait", [_plan_chips], sb_out, tot)
    finish("w_out", g_out, fs_out, fc_out)

    order = ["w_in", "w_gk2", "b_gk", "gla_norm_w", "swa_sinks", "w_out", "ln1_g", "ln1_b", "w_up", "w_down",
             "ln2_g", "ln2_b"]
    res = {**big, **small}
    outs = [loss, grad_x[None]]
    for k in range(4):
        outs += [res[nm][k] for nm in order]
    return tuple(outs)
```

```python
import functools

import jax
import jax.numpy as jnp
from jax import lax
from jax.experimental import pallas as pl
from jax.experimental.pallas import tpu as pltpu

F32 = jnp.float32
BF16 = jnp.bfloat16

N_DEV = 8
D_MODEL = 2048
D_FF = 8192
GLA_HEADS = 4
GLA_DK = 128
GLA_DV = 256
GLA_KW = 512
D_GLA = 1024
GATE_RANK = 16
GATE_TAU = 16.0
SWA_HEADS = 16
SWA_DH = 64
SWA_GROUP = 8
WINDOW = 128
D_SWA = 1024
D_IN = 4368
ALPHA = 2.0 ** 0.25
LN_EPS = 1e-5
RMS_EPS = 1e-5
ADAM_LR = 0.001
ADAM_B1 = 0.9
ADAM_B2 = 0.999
ADAM_EPS = 1e-08
ADAM_WD = 0.01
ADAM_STEP = 10

C_QG, C_KG, C_VG, C_GG, C_QS, C_KS, C_VS, C_LO = 0, 512, 1024, 2048, 3072, 4096, 4224, 4352
D_INP = 4480
LANES = 128
GC = 16
NEG = -1e30

NN = ((1,), (0,))
NT = ((1,), (1,))
TN = ((0,), (0,))

VMEM_LIMIT = 52 * 1024 * 1024
MESH = pl.DeviceIdType.MESH


def _dot(a, b, dn, precision=None):
    return lax.dot_general(a, b, (dn, ((), ())), preferred_element_type=F32, precision=precision)


def _bf16_round(v):
    return v.astype(BF16).astype(F32)


def _cparams(dims):
    return pltpu.CompilerParams(dimension_semantics=dims, vmem_limit_bytes=VMEM_LIMIT)


def _dep_operand(dep):
    if dep is None:
        return (), ()
    return (dep,), (pl.BlockSpec(dep.shape, lambda *_: (0,) * dep.ndim),)


def _to_padded_rows(w):
    pad = jnp.zeros((D_INP - D_IN,) + w.shape[1:], w.dtype)
    return jnp.concatenate([w[:3072], w[3088:], w[3072:3088], pad], axis=0)


def _from_padded_rows(g):
    return jnp.concatenate([g[:3072], g[C_LO:C_LO + GATE_RANK], g[3072:C_LO]], axis=0)


def _matmul(name, a, b, dn, grid, a_spec, b_spec, out_shape, out_specs, acc_shape, *,
            extra=(), extra_specs=(), epilogue=None, dims=("parallel", "parallel", "arbitrary"), dep=None):
    nk = grid[2]
    n_extra = len(extra)
    deps, dep_specs = _dep_operand(dep)
    direct = epilogue is None and (nk == 1 or (not isinstance(out_shape, (list, tuple)) and out_shape.dtype == F32))

    scratch = [] if direct or nk == 1 else [pltpu.VMEM(acc_shape, F32)]

    def body(a_ref, b_ref, *rest):
        extra_refs = rest[:n_extra]
        out_refs = rest[n_extra + len(deps):len(rest) - len(scratch)]
        acc_ref = rest[-1] if scratch else out_refs[0]
        part = _dot(a_ref[...].astype(BF16), b_ref[...].astype(BF16), dn)

        def finish():
            if epilogue is None:
                out_refs[0][...] = acc_ref[...].astype(out_refs[0].dtype)
            else:
                epilogue(acc_ref, extra_refs, out_refs)

        if direct and nk == 1:
            out_refs[0][...] = part.reshape(out_refs[0].shape).astype(out_refs[0].dtype)
        elif nk == 1:
            epilogue(part, extra_refs, out_refs)
        else:
            k = pl.program_id(2)

            @pl.when(k == 0)
            def _():
                acc_ref[...] = part

            @pl.when(k > 0)
            def _():
                acc_ref[...] += part

            if not direct:
                @pl.when(k == nk - 1)
                def _():
                    finish()

    return pl.pallas_call(
        body, name=name, grid=grid,
        in_specs=[a_spec, b_spec, *extra_specs, *dep_specs],
        out_specs=out_specs, out_shape=out_shape,
        scratch_shapes=scratch,
        compiler_params=_cparams(dims),
    )(a, b, *extra, *deps)


def _row_chunks(rows, step=128):
    step = min(step, rows)
    return [(r, step) for r in range(0, rows, step)]


def _ln_stats(r):
    mu = jnp.mean(r, axis=-1, keepdims=True)
    xc = r - mu
    var = jnp.mean(xc * xc, axis=-1, keepdims=True)
    rstd = lax.rsqrt(var + LN_EPS)
    return xc * rstd, rstd


def _ln_bwd(dy_g, xhat, rstd):
    m1 = jnp.mean(dy_g, axis=-1, keepdims=True)
    m2 = jnp.mean(dy_g * xhat, axis=-1, keepdims=True)
    return rstd * (dy_g - m1 - xhat * m2)


def _proj(xb, w_in_t, dep=None):
    t = xb.shape[0]
    tm, tn = min(1024, t), 640
    return _matmul(
        "proj", xb, w_in_t, NT, (t // tm, D_INP // tn, 1),
        pl.BlockSpec((tm, D_MODEL), lambda i, j, k: (i, 0)),
        pl.BlockSpec((tn, D_MODEL), lambda i, j, k: (j, 0)),
        jax.ShapeDtypeStruct((t, D_INP), F32),
        pl.BlockSpec((tm, tn), lambda i, j, k: (i, j)),
        (tm, tn), dep=dep)


def _gate_fwd(proj, w2p, b_gk, tri):
    t = proj.shape[0]
    r = tri.shape[0]

    def body(lo_ref, w_ref, b_ref, tri_ref, bloc_ref, dgdz_ref):
        z = _dot(lo_ref[...].astype(BF16), w_ref[...], NN) + b_ref[...]
        e = jnp.exp(-jnp.abs(z))
        gk = (jnp.minimum(z, 0.0) - jnp.log1p(e)) * (1.0 / GATE_TAU)
        inv = 1.0 / (1.0 + e)
        dgdz_ref[...] = jnp.where(z >= 0.0, e * inv, inv) * (1.0 / GATE_TAU)
        bloc_ref[...] = _dot(tri_ref[...], gk, NN, precision=lax.Precision.HIGHEST)

    return pl.pallas_call(
        body, name="gate_fwd", grid=(t // r,),
        in_specs=[pl.BlockSpec((r, LANES), lambda i: (i, C_LO // LANES)),
                  pl.BlockSpec((LANES, GLA_KW), lambda i: (0, 0)),
                  pl.BlockSpec((1, GLA_KW), lambda i: (0, 0)),
                  pl.BlockSpec((r, r), lambda i: (0, 0))],
        out_specs=[pl.BlockSpec((r, GLA_KW), lambda i: (i, 0)),
                   pl.BlockSpec((r, GLA_KW), lambda i: (i, 0))],
        out_shape=[jax.ShapeDtypeStruct((t, GLA_KW), F32), jax.ShapeDtypeStruct((t, GLA_KW), F32)],
        compiler_params=_cparams(("parallel",)),
    )(proj, w2p, b_gk, tri)


def _gla_fwd(proj, bloc):
    t = proj.shape[0]
    r = min(256, t)
    ncb = r // GC
    scale = GLA_DK ** -0.5

    def body(q_ref, k_ref, v_ref, b_ref, o_ref, st_ref, s_scr, m_scr):
        @pl.when(pl.program_id(1) == 0)
        def _():
            s_scr[...] = jnp.zeros_like(s_scr)

        rows = lax.broadcasted_iota(jnp.int32, (GC, 1), 0)
        cols = lax.broadcasted_iota(jnp.int32, (1, GC), 1)

        def increment(c, carry):
            r0 = pl.multiple_of(c * GC, GC)
            k = k_ref[pl.ds(r0, GC), :]
            b = b_ref[pl.ds(r0, GC), :]
            kd = k * jnp.exp(b[GC - 1:GC, :] - b)
            m_scr[c] = _dot(v_ref[pl.ds(r0, GC), :].astype(BF16), kd.astype(BF16), TN)
            return carry

        lax.fori_loop(0, ncb, increment, 0, unroll=8)

        def recur(c, st):
            st_ref[c] = st
            r0 = pl.multiple_of(c * GC, GC)
            bl = b_ref[pl.ds(r0, GC), :][GC - 1:GC, :]
            return st * jnp.exp(bl) + m_scr[c]

        s_scr[...] = lax.fori_loop(0, ncb, recur, s_scr[...])

        def output(c, carry):
            r0 = pl.multiple_of(c * GC, GC)
            q = q_ref[pl.ds(r0, GC), :] * scale
            b = b_ref[pl.ds(r0, GC), :]
            kr = _bf16_round(k_ref[pl.ds(r0, GC), :])
            att = jnp.zeros((GC, GC), F32)
            for j in range(GC):
                w = jnp.exp(jnp.where(rows >= j, b - b[j:j + 1, :], NEG))
                a = jnp.sum(_bf16_round(q * w) * kr[j:j + 1, :], axis=-1, keepdims=True)
                att = jnp.where(cols == j, a, att)
            o_ref[pl.ds(r0, GC), :] = (_dot((q * jnp.exp(b)).astype(BF16), st_ref[c].astype(BF16), NT)
                                       + _dot(att.astype(BF16), v_ref[pl.ds(r0, GC), :].astype(BF16), NN))
            return carry

        lax.fori_loop(0, ncb, output, 0, unroll=8)

    return pl.pallas_call(
        body, name="gla_fwd", grid=(GLA_HEADS, t // r),
        in_specs=[pl.BlockSpec((r, GLA_DK), lambda h, i: (i, C_QG // GLA_DK + h)),
                  pl.BlockSpec((r, GLA_DK), lambda h, i: (i, C_KG // GLA_DK + h)),
                  pl.BlockSpec((r, GLA_DV), lambda h, i: (i, C_VG // GLA_DV + h)),
                  pl.BlockSpec((r, GLA_DK), lambda h, i: (i, h))],
        out_specs=[pl.BlockSpec((r, GLA_DV), lambda h, i: (i, h)),
                   pl.BlockSpec((ncb, None, GLA_DV, GLA_DK), lambda h, i: (i, h, 0, 0))],
        out_shape=[jax.ShapeDtypeStruct((t, D_GLA), F32),
                   jax.ShapeDtypeStruct((t // GC, GLA_HEADS, GLA_DV, GLA_DK), F32)],
        scratch_shapes=[pltpu.VMEM((GLA_DV, GLA_DK), F32), pltpu.VMEM((ncb, GLA_DV, GLA_DK), F32)],
        compiler_params=_cparams(("parallel", "arbitrary")),
    )(proj, proj, proj, bloc)


def _gla_norm_fwd(o_raw, proj, norm_w):
    t = o_raw.shape[0]
    r = min(512, t)

    def body(o_ref, g_ref, w_ref, out_ref):
        w = w_ref[...]
        for h in range(GLA_HEADS):
            sl = slice(h * GLA_DV, (h + 1) * GLA_DV)
            o = o_ref[:, sl]
            g = g_ref[:, sl]
            on = o * lax.rsqrt(jnp.mean(o * o, axis=-1, keepdims=True) + RMS_EPS)
            out_ref[:, sl] = (on * w * (g * jax.nn.sigmoid(g))).astype(BF16)

    return pl.pallas_call(
        body, name="gla_norm_fwd", grid=(t // r,),
        in_specs=[pl.BlockSpec((r, D_GLA), lambda i: (i, 0)),
                  pl.BlockSpec((r, D_GLA), lambda i: (i, C_GG // D_GLA)),
                  pl.BlockSpec((1, GLA_DV), lambda i: (0, 0))],
        out_specs=pl.BlockSpec((r, D_GLA), lambda i: (i, 0)),
        out_shape=jax.ShapeDtypeStruct((t, D_GLA), BF16),
        compiler_params=_cparams(("parallel",)),
    )(o_raw, proj, norm_w)


def _swa_masks(i, heads):
    qi = lax.broadcasted_iota(jnp.int32, (heads * WINDOW, 1), 0) & (WINDOW - 1)
    kj = lax.broadcasted_iota(jnp.int32, (1, 2 * WINDOW), 1)
    valid = (kj > qi) & (kj <= qi + WINDOW) & ((i > 0) | (kj >= WINDOW))
    lo = lax.broadcasted_iota(jnp.int32, (1, LANES), 1) < SWA_DH
    return valid, lo


def _dup_half(x, lo, kv):
    xr = pltpu.roll(x, SWA_DH, axis=1)
    return jnp.where(lo, x, xr) if kv == 0 else jnp.where(lo, xr, x)


def _swa_stack(ref, col0, lo, kv):
    parts = []
    for p in range(4):
        c0 = col0 + LANES * (4 * kv + p)
        xp = ref[:, c0:c0 + LANES]
        parts += [jnp.where(lo, xp, 0.0).astype(BF16), jnp.where(lo, 0.0, xp).astype(BF16)]
    return jnp.concatenate(parts, axis=0)


def _swa_unstack(x_all, lo, p):
    r0 = 2 * p * WINDOW
    return jnp.where(lo, x_all[r0:r0 + WINDOW, :], x_all[r0 + WINDOW:r0 + 2 * WINDOW, :])


def _swa_sinks(sink_ref, kv):
    return jnp.concatenate([jnp.full((WINDOW, 1), sink_ref[SWA_GROUP * kv + h], F32) for h in range(SWA_GROUP)],
                           axis=0)


def _swa_probs(qm, kdup, valid, sink):
    s = _dot(qm, kdup, NT) * (SWA_DH ** -0.5)
    s = jnp.where(valid, s, NEG)
    m = jnp.maximum(jnp.max(s, axis=-1, keepdims=True), sink)
    p = jnp.exp(s - m)
    es = jnp.exp(sink - m)
    inv = 1.0 / (jnp.sum(p, axis=-1, keepdims=True) + es)
    return p * inv, es * inv


def _swa_fwd(proj, sinks, gla_out, dep=None):
    t = proj.shape[0]

    deps, dep_specs = _dep_operand(dep)

    def body(sink_ref, q_ref, kp_ref, kc_ref, vp_ref, vc_ref, gla_ref, *rest):
        o_ref = rest[-1]
        i = pl.program_id(0)
        o_ref[:, :D_GLA] = gla_ref[...]
        valid, lo = _swa_masks(i, 1)
        kb = jnp.concatenate([kp_ref[...], kc_ref[...]], axis=0)
        vb = jnp.concatenate([vp_ref[...], vc_ref[...]], axis=0)
        for kv in range(2):
            kdup = _dup_half(kb, lo, kv).astype(BF16)
            vdup = _dup_half(vb, lo, kv).astype(BF16)
            for p in range(4):
                c0 = LANES * (4 * kv + p)
                qp = q_ref[:, c0:c0 + LANES]
                halves = []
                for e in range(2):
                    qm = jnp.where(lo if e == 0 else ~lo, qp, 0.0).astype(BF16)
                    pn, _ = _swa_probs(qm, kdup, valid, sink_ref[SWA_GROUP * kv + 2 * p + e])
                    halves.append(_dot(pn.astype(BF16), vdup, NN))
                o_ref[:, D_GLA + c0:D_GLA + c0 + LANES] = jnp.where(lo, halves[0], halves[1]).astype(BF16)

    kvspec = lambda col, prev: pl.BlockSpec(
        (WINDOW, LANES), (lambda i: (jnp.maximum(i - 1, 0), col)) if prev else (lambda i: (i, col)))
    return pl.pallas_call(
        body, name="swa_fwd", grid=(t // WINDOW,),
        in_specs=[pl.BlockSpec(memory_space=pltpu.SMEM),
                  pl.BlockSpec((WINDOW, D_SWA), lambda i: (i, C_QS // D_SWA)),
                  kvspec(C_KS // LANES, True), kvspec(C_KS // LANES, False),
                  kvspec(C_VS // LANES, True), kvspec(C_VS // LANES, False),
                  pl.BlockSpec((WINDOW, D_GLA), lambda i: (i, 0)), *dep_specs],
        out_specs=pl.BlockSpec((WINDOW, D_MODEL), lambda i: (i, 0)),
        out_shape=jax.ShapeDtypeStruct((t, D_MODEL), BF16),
        compiler_params=_cparams(("parallel",)),
    )(sinks, proj, proj, proj, proj, proj, gla_out, *deps)


def _mix_ln1(cat, w_out, x, ln_g, ln_b):
    t = cat.shape[0]
    tm, tk = min(512, t), 512

    def epilogue(acc_ref, extra, outs):
        x_ref, g_ref, b_ref = extra
        r1_ref, h1_ref, h1b_ref = outs
        for r0, n in _row_chunks(tm):
            rs = slice(r0, r0 + n)
            r1 = ALPHA * x_ref[rs, :] + acc_ref[rs, :]
            xhat, _ = _ln_stats(r1)
            h = xhat * g_ref[...] + b_ref[...]
            r1_ref[rs, :] = r1
            h1_ref[rs, :] = h
            h1b_ref[rs, :] = h.astype(BF16)

    row = pl.BlockSpec((tm, D_MODEL), lambda i, j, k: (i, 0))
    vec = pl.BlockSpec((1, D_MODEL), lambda i, j, k: (0, 0))
    return _matmul(
        "mix_ln1", cat, w_out, NN, (t // tm, 1, D_MODEL // tk),
        pl.BlockSpec((tm, tk), lambda i, j, k: (i, k)),
        pl.BlockSpec((tk, D_MODEL), lambda i, j, k: (k, 0)),
        [jax.ShapeDtypeStruct((t, D_MODEL), F32), jax.ShapeDtypeStruct((t, D_MODEL), F32),
         jax.ShapeDtypeStruct((t, D_MODEL), BF16)],
        [row, row, row], (tm, D_MODEL),
        extra=(x, ln_g, ln_b), extra_specs=(row, vec, vec), epilogue=epilogue)


def _mlp_up(h1b, w_up):
    t = h1b.shape[0]
    tm, tn = min(1024, t), 1024

    def epilogue(acc_ref, extra, outs):
        a_ref, hdn_ref = outs
        for r0, n in _row_chunks(tm, 256):
            rs = slice(r0, r0 + n)
            a = jnp.maximum(acc_ref[rs, :], 0.0)
            a_ref[rs, :] = a.astype(BF16)
            hdn_ref[rs, :] = (a * a).astype(BF16)

    out = pl.BlockSpec((tm, tn), lambda i, j, k: (i, j))
    return _matmul(
        "mlp_up", h1b, w_up, NN, (t // tm, D_FF // tn, 1),
        pl.BlockSpec((tm, D_MODEL), lambda i, j, k: (i, 0)),
        pl.BlockSpec((None, D_MODEL, tn), lambda i, j, k: (j, 0, 0)),
        [jax.ShapeDtypeStruct((t, D_FF), BF16), jax.ShapeDtypeStruct((t, D_FF), BF16)],
        [out, out], (tm, tn), epilogue=epilogue)


def _mlp_down_loss(hdn, w_down, h1, target, ln_g, ln_b):
    t = hdn.shape[0]
    tm, tn, tk = min(1024, t), 1024, 4096
    ff = _matmul(
        "mlp_down", hdn, w_down, NN, (t // tm, D_MODEL // tn, D_FF // tk),
        pl.BlockSpec((tm, tk), lambda i, j, k: (i, k)),
        pl.BlockSpec((tk, tn), lambda i, j, k: (k, j)),
        jax.ShapeDtypeStruct((t, D_MODEL), F32),
        pl.BlockSpec((tm, tn), lambda i, j, k: (i, j)), (tm, tn))
    r = min(256, t)

    def body(ff_ref, h1_ref, t_ref, g_ref, b_ref, dr2_ref, dr2b_ref, gg_ref, gb_ref, loss_ref):
        @pl.when(pl.program_id(0) == 0)
        def _():
            gg_ref[...] = jnp.zeros_like(gg_ref)
            gb_ref[...] = jnp.zeros_like(gb_ref)
            loss_ref[...] = jnp.zeros_like(loss_ref)

        for r0, n in _row_chunks(r, 64):
            rs = slice(r0, r0 + n)
            xhat, rstd = _ln_stats(ALPHA * h1_ref[rs, :] + ff_ref[rs, :])
            err = xhat * g_ref[...] + b_ref[...] - t_ref[rs, :]
            loss_ref[...] += 0.5 * jnp.sum(jnp.mean(err * err, axis=-1, keepdims=True))
            dy = err * (1.0 / D_MODEL)
            gg_ref[...] += jnp.sum(dy * xhat, axis=0, keepdims=True)
            gb_ref[...] += jnp.sum(dy, axis=0, keepdims=True)
            dr2 = _ln_bwd(dy * g_ref[...], xhat, rstd)
            dr2_ref[rs, :] = dr2
            dr2b_ref[rs, :] = dr2.astype(BF16)

    row = pl.BlockSpec((r, D_MODEL), lambda i: (i, 0))
    vec = pl.BlockSpec((1, D_MODEL), lambda i: (0, 0))
    return pl.pallas_call(
        body, name="ln2_loss", grid=(t // r,),
        in_specs=[row, row, row, vec, vec],
        out_specs=[row, row, vec, vec, pl.BlockSpec((1, LANES), lambda i: (0, 0))],
        out_shape=[jax.ShapeDtypeStruct((t, D_MODEL), F32), jax.ShapeDtypeStruct((t, D_MODEL), BF16),
                   jax.ShapeDtypeStruct((1, D_MODEL), F32), jax.ShapeDtypeStruct((1, D_MODEL), F32),
                   jax.ShapeDtypeStruct((1, LANES), F32)],
        compiler_params=_cparams(("arbitrary",)),
    )(ff, h1, target, ln_g, ln_b)


def _mlp_down_bwd(dr2b, w_down, a_act):
    t = dr2b.shape[0]
    tm, tn = min(1024, t), 1024

    def epilogue(acc_ref, extra, outs):
        (a_ref,) = extra
        for r0, n in _row_chunks(tm, 256):
            rs = slice(r0, r0 + n)
            outs[0][rs, :] = (acc_ref[rs, :] * (2.0 * a_ref[rs, :].astype(F32))).astype(BF16)

    blk = pl.BlockSpec((tm, tn), lambda i, j, k: (i, j))
    return _matmul(
        "mlp_down_bwd", dr2b, w_down, NT, (t // tm, D_FF // tn, 1),
        pl.BlockSpec((tm, D_MODEL), lambda i, j, k: (i, 0)),
        pl.BlockSpec((tn, D_MODEL), lambda i, j, k: (j, 0)),
        jax.ShapeDtypeStruct((t, D_FF), BF16), blk, (tm, tn),
        extra=(a_act,), extra_specs=(blk,), epilogue=epilogue)


def _grad_w_down(hdn, dr2b):
    t = hdn.shape[0]
    tm, tn = 1024, 1024
    return _matmul(
        "grad_w_down", hdn, dr2b, TN, (D_MODEL // tn, D_FF // tm, 1),
        pl.BlockSpec((t, tm), lambda j, i, k: (0, i)),
        pl.BlockSpec((t, tn), lambda j, i, k: (0, j)),
        jax.ShapeDtypeStruct((N_DEV, D_FF // N_DEV, D_MODEL), F32),
        pl.BlockSpec((None, tm, tn), lambda j, i, k: (i, 0, j)), (tm, tn))


def _grad_w_up(h1b, du, dep=None):
    t = h1b.shape[0]
    tm, tn = 1024, 1024
    return _matmul(
        "grad_w_up", h1b, du, TN, (N_DEV, D_MODEL // tm, 1),
        pl.BlockSpec((t, tm), lambda i, j, k: (0, j)),
        pl.BlockSpec((t, tn), lambda i, j, k: (0, i)),
        jax.ShapeDtypeStruct((N_DEV, D_MODEL, D_FF // N_DEV), F32),
        pl.BlockSpec((None, tm, tn), lambda i, j, k: (i, j, 0)), (tm, tn), dep=dep)


def _mlp_up_bwd_ln1(du, w_up, dr2, r1, ln_g, dep=None):
    t = du.shape[0]
    tm, tn, tk, nb = min(1024, t), 1024, D_FF // N_DEV, 4
    deps, dep_specs = _dep_operand(dep)

    def mm_body(a_ref, b_ref, *rest):
        o_ref = rest[-1]
        k = pl.program_id(2)
        part = _dot(a_ref[:, :tk], b_ref[0], NT)
        for d in range(1, nb):
            part = part + _dot(a_ref[:, d * tk:(d + 1) * tk], b_ref[d], NT)

        @pl.when(k == 0)
        def _():
            o_ref[...] = part

        @pl.when(k > 0)
        def _():
            o_ref[...] += part

    dff = pl.pallas_call(
        mm_body, name="mlp_up_bwd", grid=(t // tm, D_MODEL // tn, N_DEV // nb),
        in_specs=[pl.BlockSpec((tm, nb * tk), lambda i, j, k: (i, k)),
                  pl.BlockSpec((nb, tn, tk), lambda i, j, k: (k, j, 0)), *dep_specs],
        out_specs=pl.BlockSpec((tm, tn), lambda i, j, k: (i, j)),
        out_shape=jax.ShapeDtypeStruct((t, D_MODEL), F32),
        compiler_params=_cparams(("parallel", "parallel", "arbitrary")),
    )(du, w_up, *deps)
    r = min(256, t)

    def body(acc_ref, dr2_ref, r1_ref, g_ref, dr1_ref, dr1b_ref, gg_ref, gb_ref):
        @pl.when(pl.program_id(0) == 0)
        def _():
            gg_ref[...] = jnp.zeros_like(gg_ref)
            gb_ref[...] = jnp.zeros_like(gb_ref)

        for r0, n in _row_chunks(r, 64):
            rs = slice(r0, r0 + n)
            dh1 = ALPHA * dr2_ref[rs, :] + acc_ref[rs, :]
            xhat, rstd = _ln_stats(r1_ref[rs, :])
            gg_ref[...] += jnp.sum(dh1 * xhat, axis=0, keepdims=True)
            gb_ref[...] += jnp.sum(dh1, axis=0, keepdims=True)
            dr1 = _ln_bwd(dh1 * g_ref[...], xhat, rstd)
            dr1_ref[rs, :] = dr1
            dr1b_ref[rs, :] = dr1.astype(BF16)

    row = pl.BlockSpec((r, D_MODEL), lambda i: (i, 0))
    vec = pl.BlockSpec((1, D_MODEL), lambda i: (0, 0))
    return pl.pallas_call(
        body, name="ln1_bwd", grid=(t // r,),
        in_specs=[row, row, row, vec],
        out_specs=[row, row, vec, vec],
        out_shape=[jax.ShapeDtypeStruct((t, D_MODEL), F32), jax.ShapeDtypeStruct((t, D_MODEL), BF16),
                   jax.ShapeDtypeStruct((1, D_MODEL), F32), jax.ShapeDtypeStruct((1, D_MODEL), F32)],
        compiler_params=_cparams(("arbitrary",)),
    )(dff, dr2, r1, ln_g)


def _dcat(dr1b, w_out):
    t = dr1b.shape[0]
    tm, tn = min(1024, t), 1024
    return _matmul(
        "dcat", dr1b, w_out, NT, (t // tm, D_MODEL // tn, 1),
        pl.BlockSpec((tm, D_MODEL), lambda i, j, k: (i, 0)),
        pl.BlockSpec((tn, D_MODEL), lambda i, j, k: (j, 0)),
        jax.ShapeDtypeStruct((t, D_MODEL), F32),
        pl.BlockSpec((tm, tn), lambda i, j, k: (i, j)), (tm, tn))


def _grad_w_out(cat, dr1b, dep=None):
    t = cat.shape[0]
    tm, tn = 1024, 1024
    return _matmul(
        "grad_w_out", cat, dr1b, TN, (D_MODEL // tm, D_MODEL // tn, 1),
        pl.BlockSpec((t, tm), lambda i, j, k: (0, i)),
        pl.BlockSpec((t, tn), lambda i, j, k: (0, j)),
        jax.ShapeDtypeStruct((N_DEV, D_MODEL // N_DEV, D_MODEL), F32),
        pl.BlockSpec((tm // (D_MODEL // N_DEV), D_MODEL // N_DEV, tn), lambda i, j, k: (i, 0, j)), (tm, tn), dep=dep)


def _gla_norm_bwd(dcat, o_raw, proj, norm_w, dep=None):
    t = o_raw.shape[0]
    r = min(512, t)

    deps, dep_specs = _dep_operand(dep)

    def body(d_ref, o_ref, g_ref, w_ref, *rest):
        do_ref, dg_ref, dw_ref = rest[len(deps):]

        @pl.when(pl.program_id(0) == 0)
        def _():
            dw_ref[...] = jnp.zeros_like(dw_ref)

        w = w_ref[...]
        dw = jnp.zeros((1, GLA_DV), F32)
        for h in range(GLA_HEADS):
            sl = slice(h * GLA_DV, (h + 1) * GLA_DV)
            o = o_ref[:, sl]
            g = g_ref[:, sl]
            d = d_ref[:, sl]
            rr = lax.rsqrt(jnp.mean(o * o, axis=-1, keepdims=True) + RMS_EPS)
            on = o * rr
            sg = jax.nn.sigmoid(g)
            sil = g * sg
            dg_ref[:, sl] = (d * on * w * (sg * (1.0 + g * (1.0 - sg)))).astype(BF16)
            dw = dw + jnp.sum(d * on * sil, axis=0, keepdims=True)
            don = d * w * sil
            do_ref[:, sl] = rr * (don - on * jnp.mean(don * on, axis=-1, keepdims=True))
        dw_ref[...] += dw

    return pl.pallas_call(
        body, name="gla_norm_bwd", grid=(t // r,),
        in_specs=[pl.BlockSpec((r, D_GLA), lambda i: (i, 0)),
                  pl.BlockSpec((r, D_GLA), lambda i: (i, 0)),
                  pl.BlockSpec((r, D_GLA), lambda i: (i, C_GG // D_GLA)),
                  pl.BlockSpec((1, GLA_DV), lambda i: (0, 0)), *dep_specs],
        out_specs=[pl.BlockSpec((r, D_GLA), lambda i: (i, 0)),
                   pl.BlockSpec((r, D_GLA), lambda i: (i, 0)),
                   pl.BlockSpec((1, GLA_DV), lambda i: (0, 0))],
        out_shape=[jax.ShapeDtypeStruct((t, D_GLA), F32), jax.ShapeDtypeStruct((t, D_GLA), BF16),
                   jax.ShapeDtypeStruct((1, GLA_DV), F32)],
        compiler_params=_cparams(("arbitrary",)),
    )(dcat, o_raw, proj, norm_w, *deps)


def _gla_bwd(proj, bloc, do_raw, states):
    t = proj.shape[0]
    r = min(256, t)
    ncb = r // GC
    nb = t // r
    scale = GLA_DK ** -0.5

    def body(q_ref, k_ref, v_ref, b_ref, do_ref, st_ref, dq_ref, dk_ref, dv_ref, db_ref, ds_scr):
        @pl.when(pl.program_id(1) == 0)
        def _():
            ds_scr[...] = jnp.zeros_like(ds_scr)

        rows = lax.broadcasted_iota(jnp.int32, (GC, 1), 0)

        def chunk(cc, carry):
            c = ncb - 1 - cc
            r0 = pl.multiple_of(c * GC, GC)
            q = q_ref[pl.ds(r0, GC), :] * scale
            k = k_ref[pl.ds(r0, GC), :]
            v = v_ref[pl.ds(r0, GC), :]
            b = b_ref[pl.ds(r0, GC), :]
            do = do_ref[pl.ds(r0, GC), :]
            st = st_ref[c]
            dsn = ds_scr[...]
            bl = b[GC - 1:GC, :]
            eb = jnp.exp(b)
            ekl = jnp.exp(bl - b)
            ebl = jnp.exp(bl)
            qh = q * eb
            kd = k * ekl
            dob = do.astype(BF16)
            dsb = dsn.astype(BF16)
            dqh = _dot(dob, st.astype(BF16), NN)
            dkd = _dot(v.astype(BF16), dsb, NN)
            dv = _dot(kd.astype(BF16), dsb, NT)
            dq_i = jnp.zeros((GC, GLA_DK), F32)
            dk_i = jnp.zeros((GC, GLA_DK), F32)
            dk_x = jnp.zeros((GC, GLA_DK), F32)
            kr = _bf16_round(k)
            vr = _bf16_round(v)
            dor = _bf16_round(do)
            for i in range(GC):
                w = jnp.exp(jnp.where(rows <= i, b[i:i + 1, :] - b, NEG))
                qw = q[i:i + 1, :] * w
                qwr = _bf16_round(qw)
                a = _bf16_round(jnp.sum(qwr * kr, axis=-1, keepdims=True))
                da = jnp.sum(vr * dor[i:i + 1, :], axis=-1, keepdims=True)
                dv = dv + a * dor[i:i + 1, :]
                dk_x = dk_x + da * qw
                dk_i = dk_i + _bf16_round(da) * qwr
                dq_i = jnp.where(rows == i, jnp.sum(da * (w * k), axis=0, keepdims=True), dq_i)
            dqs = dqh * eb + dq_i
            dk = dkd * ekl + dk_i
            db_last = jnp.sum(dkd * kd, axis=0, keepdims=True) + ebl * jnp.sum(dsn * st, axis=0, keepdims=True)
            db = q * dqs - k * (dkd * ekl + dk_x) + jnp.where(rows == GC - 1, db_last, 0.0)
            dq_ref[pl.ds(r0, GC), :] = (dqs * scale).astype(BF16)
            dk_ref[pl.ds(r0, GC), :] = dk.astype(BF16)
            dv_ref[pl.ds(r0, GC), :] = dv.astype(BF16)
            db_ref[pl.ds(r0, GC), :] = db
            ds_scr[...] = dsn * ebl + _dot(dob, qh.astype(BF16), TN)
            return carry

        lax.fori_loop(0, ncb, chunk, 0)

    rev = lambda i: nb - 1 - i
    return pl.pallas_call(
        body, name="gla_bwd", grid=(GLA_HEADS, nb),
        in_specs=[pl.BlockSpec((r, GLA_DK), lambda h, i: (rev(i), C_QG // GLA_DK + h)),
                  pl.BlockSpec((r, GLA_DK), lambda h, i: (rev(i), C_KG // GLA_DK + h)),
                  pl.BlockSpec((r, GLA_DV), lambda h, i: (rev(i), C_VG // GLA_DV + h)),
                  pl.BlockSpec((r, GLA_DK), lambda h, i: (rev(i), h)),
                  pl.BlockSpec((r, GLA_DV), lambda h, i: (rev(i), h)),
                  pl.BlockSpec((ncb, None, GLA_DV, GLA_DK), lambda h, i: (rev(i), h, 0, 0))],
        out_specs=[pl.BlockSpec((r, GLA_DK), lambda h, i: (rev(i), h)),
                   pl.BlockSpec((r, GLA_DK), lambda h, i: (rev(i), h)),
                   pl.BlockSpec((r, GLA_DV), lambda h, i: (rev(i), h)),
                   pl.BlockSpec((r, GLA_DK), lambda h, i: (rev(i), h))],
        out_shape=[jax.ShapeDtypeStruct((t, GLA_KW), BF16), jax.ShapeDtypeStruct((t, GLA_KW), BF16),
                   jax.ShapeDtypeStruct((t, D_GLA), BF16), jax.ShapeDtypeStruct((t, GLA_KW), F32)],
        scratch_shapes=[pltpu.VMEM((GLA_DV, GLA_DK), F32)],
        compiler_params=_cparams(("parallel", "arbitrary")),
    )(proj, proj, proj, bloc, do_raw, states)


def _gate_bwd(db, dgdz, proj, w2p, triu):
    t = db.shape[0]
    r = triu.shape[0]

    def body(db_ref, s_ref, lo_ref, w_ref, u_ref, dlo_ref, gw_ref, gb_ref):
        @pl.when(pl.program_id(0) == 0)
        def _():
            gw_ref[...] = jnp.zeros_like(gw_ref)
            gb_ref[...] = jnp.zeros_like(gb_ref)

        dz = _dot(u_ref[...], db_ref[...], NN, precision=lax.Precision.HIGHEST) * s_ref[...]
        dzb = dz.astype(BF16)
        gb_ref[...] += jnp.sum(dz, axis=0, keepdims=True)
        gw_ref[...] += _dot(lo_ref[...].astype(BF16), dzb, TN)
        dlo_ref[...] = _dot(dzb, w_ref[...], NT).astype(BF16)

    return pl.pallas_call(
        body, name="gate_bwd", grid=(t // r,),
        in_specs=[pl.BlockSpec((r, GLA_KW), lambda i: (i, 0)),
                  pl.BlockSpec((r, GLA_KW), lambda i: (i, 0)),
                  pl.BlockSpec((r, LANES), lambda i: (i, C_LO // LANES)),
                  pl.BlockSpec((LANES, GLA_KW), lambda i: (0, 0)),
                  pl.BlockSpec((r, r), lambda i: (0, 0))],
        out_specs=[pl.BlockSpec((r, LANES), lambda i: (i, 0)),
                   pl.BlockSpec((LANES, GLA_KW), lambda i: (0, 0)),
                   pl.BlockSpec((1, GLA_KW), lambda i: (0, 0))],
        out_shape=[jax.ShapeDtypeStruct((t, LANES), BF16), jax.ShapeDtypeStruct((LANES, GLA_KW), F32),
                   jax.ShapeDtypeStruct((1, GLA_KW), F32)],
        compiler_params=_cparams(("arbitrary",)),
    )(db, dgdz, proj, w2p, triu)


def _swa_bwd(proj, dcat, sinks, dep=None):
    t = proj.shape[0]

    deps, dep_specs = _dep_operand(dep)

    def body(sink_ref, q_ref, kp_ref, kc_ref, vp_ref, vc_ref, d_ref, *rest):
        dq_ref, dk_ref, dv_ref, dsink_ref = rest[len(deps):]
        i = pl.program_id(0)

        @pl.when(i == 0)
        def _():
            dk_ref[...] = jnp.zeros_like(dk_ref)
            dv_ref[...] = jnp.zeros_like(dv_ref)
            dsink_ref[...] = jnp.zeros_like(dsink_ref)

        valid, lo = _swa_masks(i, SWA_GROUP)
        lane = lax.broadcasted_iota(jnp.int32, (1, LANES), 1)
        kb = jnp.concatenate([kp_ref[...], kc_ref[...]], axis=0)
        vb = jnp.concatenate([vp_ref[...], vc_ref[...]], axis=0)
        dsink = jnp.zeros((1, LANES), F32)
        folded_k, folded_v = [], []
        for kv in range(2):
            kdup = _dup_half(kb, lo, kv).astype(BF16)
            vdup = _dup_half(vb, lo, kv).astype(BF16)
            qm = _swa_stack(q_ref, 0, lo, kv)
            dom = _swa_stack(d_ref, 0, lo, kv)
            pn, psink = _swa_probs(qm, kdup, valid, _swa_sinks(sink_ref, kv))
            dpr = _dot(dom, vdup, NT)
            drow = jnp.sum(dpr * pn, axis=-1, keepdims=True)
            ds_col = psink * drow
            for h in range(SWA_GROUP):
                dsink = dsink + jnp.where(lane == SWA_GROUP * kv + h,
                                          -jnp.sum(ds_col[h * WINDOW:(h + 1) * WINDOW, :]), 0.0)
            dsb = (pn * (dpr - drow) * (SWA_DH ** -0.5)).astype(BF16)
            dq_all = _dot(dsb, kdup, NN)
            for p in range(4):
                c0 = LANES * (4 * kv + p)
                dq_ref[:, c0:c0 + LANES] = _swa_unstack(dq_all, lo, p).astype(BF16)
            dkd = _dot(dsb, qm, TN)
            dvd = _dot(pn.astype(BF16), dom, TN)
            folded_k.append(dkd + pltpu.roll(dkd, SWA_DH, axis=1))
            folded_v.append(dvd + pltpu.roll(dvd, SWA_DH, axis=1))
        dkb = jnp.where(lo, folded_k[0], folded_k[1])
        dvb = jnp.where(lo, folded_v[0], folded_v[1])
        dsink_ref[...] += dsink
        cur = pl.ds(pl.multiple_of(i * WINDOW, WINDOW), WINDOW)
        dk_ref[cur, :] += dkb[WINDOW:, :]
        dv_ref[cur, :] += dvb[WINDOW:, :]

        @pl.when(i > 0)
        def _():
            prev = pl.ds(pl.multiple_of((i - 1) * WINDOW, WINDOW), WINDOW)
            dk_ref[prev, :] += dkb[:WINDOW, :]
            dv_ref[prev, :] += dvb[:WINDOW, :]

    kvspec = lambda col, prev: pl.BlockSpec(
        (WINDOW, LANES), (lambda i: (jnp.maximum(i - 1, 0), col)) if prev else (lambda i: (i, col)))
    full = pl.BlockSpec((t, LANES), lambda i: (0, 0))
    return pl.pallas_call(
        body, name="swa_bwd", grid=(t // WINDOW,),
        in_specs=[pl.BlockSpec(memory_space=pltpu.SMEM),
                  pl.BlockSpec((WINDOW, D_SWA), lambda i: (i, C_QS // D_SWA)),
                  kvspec(C_KS // LANES, True), kvspec(C_KS // LANES, False),
                  kvspec(C_VS // LANES, True), kvspec(C_VS // LANES, False),
                  pl.BlockSpec((WINDOW, D_SWA), lambda i: (i, 1)), *dep_specs],
        out_specs=[pl.BlockSpec((WINDOW, D_SWA), lambda i: (i, 0)), full, full,
                   pl.BlockSpec((1, LANES), lambda i: (0, 0))],
        out_shape=[jax.ShapeDtypeStruct((t, D_SWA), BF16), jax.ShapeDtypeStruct((t, LANES), F32),
                   jax.ShapeDtypeStruct((t, LANES), F32), jax.ShapeDtypeStruct((1, LANES), F32)],
        compiler_params=_cparams(("arbitrary",)),
    )(sinks, proj, proj, proj, proj, proj, dcat, *deps)


def _grad_w_in(xb, dproj):
    t = xb.shape[0]
    tm, tn = 640, 1024
    return _matmul(
        "grad_w_in", dproj, xb, TN, (D_INP // tm, D_MODEL // tn, 1),
        pl.BlockSpec((t, tm), lambda i, j, k: (0, i)),
        pl.BlockSpec((t, tn), lambda i, j, k: (0, j)),
        jax.ShapeDtypeStruct((D_INP, D_MODEL), F32),
        pl.BlockSpec((tm, tn), lambda i, j, k: (i, j)), (tm, tn))


def _grad_x(dproj, w_in_t, dr1, dep=None):
    t = dproj.shape[0]
    tm, tn = min(512, t), 1024

    def epilogue(acc_ref, extra, outs):
        for r0, n in _row_chunks(tm):
            rs = slice(r0, r0 + n)
            outs[0][rs, :] = ALPHA * extra[0][rs, :] + acc_ref[rs, :]

    blk = pl.BlockSpec((tm, tn), lambda j, i, k: (i, j))
    return _matmul(
        "grad_x", dproj, w_in_t, NN, (D_MODEL // tn, t // tm, 1),
        pl.BlockSpec((tm, D_INP), lambda j, i, k: (i, 0)),
        pl.BlockSpec((D_INP, tn), lambda j, i, k: (0, j)),
        jax.ShapeDtypeStruct((t, D_MODEL), F32), blk, (tm, tn),
        extra=(dr1,), extra_specs=(blk,), epilogue=epilogue, dep=dep)


def _place():
    x, y, c = lax.axis_index("x"), lax.axis_index("y"), lax.axis_index("c")
    chips = [(1 - x, y), (x, 1 - y), (1 - x, 1 - y)]
    return x, y, c, chips


def _all_gather(shards):
    n = len(shards)
    hbm = pl.BlockSpec(memory_space=pl.ANY)

    def body(*refs):
        ins, outs = refs[:n], refs[n:2 * n]
        send, recv, loc = refs[2 * n:]
        x, y, c, chips = _place()
        me, sib = (x, y, c), (x, y, 1 - c)

        def slot(t, px, py, pc):
            return outs[t].at[4 * px + 2 * py + pc]

        def copy(t, k, block, to, src=None):
            return pltpu.make_async_remote_copy(
                src_ref=slot(t, *block) if src is None else src, dst_ref=slot(t, *block),
                send_sem=send.at[7 * t + k], recv_sem=recv.at[7 * t + k], device_id=to, device_id_type=MESH)

        mine = [pltpu.make_async_copy(ins[t], slot(t, *me), loc.at[t]) for t in range(n)]
        for cp in mine:
            cp.start()
        sent = []
        for t in range(n):
            sent.append(copy(t, 0, me, sib, src=ins[t]))
            sent += [copy(t, 1 + j, me, (*chip, c), src=ins[t]) for j, chip in enumerate(chips)]
        for cp in sent:
            cp.start()
        for t in range(n):
            for j, chip in enumerate(chips):
                copy(t, 1 + j, (*chip, c), me).wait_recv()
                fwd = copy(t, 4 + j, (*chip, c), sib)
                fwd.start()
                sent.append(fwd)
        for t in range(n):
            copy(t, 0, sib, me).wait_recv()
            for j, chip in enumerate(chips):
                copy(t, 4 + j, (*chip, 1 - c), me).wait_recv()
        for cp in sent:
            cp.wait_send()
        for cp in mine:
            cp.wait()

    return pl.pallas_call(
        body, name="all_gather_weights",
        in_specs=[hbm] * n, out_specs=[hbm] * n,
        out_shape=[jax.ShapeDtypeStruct((N_DEV,) + s.shape, s.dtype) for s in shards],
        scratch_shapes=[pltpu.SemaphoreType.DMA((7 * n,)), pltpu.SemaphoreType.DMA((7 * n,)),
                        pltpu.SemaphoreType.DMA((n,))],
    )(*shards)


def _plan_gather_out(src, land, x, y, c, chips):
    me = 4 * x + 2 * y + c
    return [(src, land.at[me], to) for to in [(x, y, 1 - c)] + [(px, py, c) for px, py in chips]]


def _plan_gather_forward(src, land, x, y, c, chips):
    return [(land.at[4 * px + 2 * py + c], land.at[4 * px + 2 * py + c], (x, y, 1 - c)) for px, py in chips]


def _plan_sibling(src, land, x, y, c, chips):
    return [(src.at[2 * q + (1 - c)], land.at[q], (x, y, 1 - c)) for q in range(4)]


def _plan_chips(src, land, x, y, c, chips):
    return [(src.at[j], land.at[j], (px, py, c)) for j, (px, py) in enumerate(chips)]


_PLAN_COPIES = {_plan_gather_out: 4, _plan_gather_forward: 3, _plan_sibling: 4, _plan_chips: 3}
_HBM = pl.BlockSpec(memory_space=pltpu.HBM)
_SEM = pl.BlockSpec(memory_space=pltpu.SEMAPHORE)
_EFFECT = pltpu.SideEffectType.DATAFLOW_SIDE_EFFECTING


def _hbm(a):
    return pltpu.with_memory_space_constraint(a, pltpu.HBM)


def _plan_descriptors(plans, srcs, lands, send, recv):
    x, y, c, chips = _place()
    cps = []
    for plan, src, land in zip(plans, srcs, lands):
        for s_ref, d_ref, to in plan(src, land, x, y, c, chips):
            k = len(cps)
            cps.append(pltpu.make_async_remote_copy(src_ref=s_ref, dst_ref=d_ref, send_sem=send.at[k],
                                                    recv_sem=recv.at[k], device_id=to, device_id_type=MESH))
    return cps


def _copies_start(name, plans, srcs, lands, after=None):
    has_src = [s is not None for s in srcs]
    arrays = [s for s in srcs if s is not None] + list(lands)
    n_src = sum(has_src)
    n_cp = sum(_PLAN_COPIES[p] for p in plans)
    afters = [] if after is None else [after]

    def body(*refs):
        ins = refs[:len(arrays)]
        send, recv = refs[len(arrays) + len(afters)], refs[len(arrays) + len(afters) + 1]
        token = refs[-1]
        it = iter(ins[:n_src])
        src_refs = [next(it) if h else None for h in has_src]
        for cp in _plan_descriptors(plans, src_refs, ins[n_src:], send, recv):
            cp.start()
        token[...] = jnp.zeros_like(token)

    outs = pl.pallas_call(
        body, name=name,
        in_specs=[_HBM] * len(arrays) + [pl.BlockSpec(memory_space=pl.ANY)] * len(afters),
        out_specs=(_SEM, _SEM, *[_HBM] * len(arrays), pl.BlockSpec(memory_space=pltpu.VMEM)),
        out_shape=(pltpu.SemaphoreType.DMA((n_cp,)), pltpu.SemaphoreType.DMA((n_cp,)),
                   *[pltpu.HBM(a.shape, a.dtype) for a in arrays], jax.ShapeDtypeStruct((8, LANES), F32)),
        input_output_aliases={i: 2 + i for i in range(len(arrays))},
        compiler_params=pltpu.CompilerParams(has_side_effects=_EFFECT),
    )(*[_hbm(a) for a in arrays], *afters)
    send, recv = outs[0], outs[1]
    thru = list(outs[2:-1])
    it = iter(thru[:n_src])
    return send, recv, [next(it) if h else None for h in has_src], thru[n_src:], outs[-1]


def _copies_wait(name, plans, started, after):
    send, recv, srcs, lands, _ = started
    has_src = [s is not None for s in srcs]
    arrays = [s for s in srcs if s is not None] + list(lands)
    n_src = sum(has_src)

    def body(*refs):
        ins = refs[:len(arrays)]
        send_ref, recv_ref = refs[len(arrays)], refs[len(arrays) + 1]
        it = iter(ins[:n_src])
        src_refs = [next(it) if h else None for h in has_src]
        for cp in _plan_descriptors(plans, src_refs, ins[n_src:], send_ref, recv_ref):
            cp.wait_send()
            cp.wait_recv()

    outs = pl.pallas_call(
        body, name=name,
        in_specs=[_HBM] * len(arrays) + [_SEM, _SEM, pl.BlockSpec(memory_space=pl.ANY)],
        out_specs=tuple([_HBM] * len(arrays)),
        out_shape=tuple(pltpu.HBM(a.shape, a.dtype) for a in arrays),
        input_output_aliases={i: i for i in range(len(arrays))},
        compiler_params=pltpu.CompilerParams(has_side_effects=_EFFECT),
    )(*arrays, send, recv, after)
    return list(outs[:n_src]), list(outs[n_src:])


def _shard_tiles(rows, cols, tr):
    if rows % tr == 0:
        return (tr, cols), rows // tr, lambda r: (r, 0)
    tc = 2 * LANES
    return (rows, tc), cols // tc, lambda r: (0, r)


def _pair_sum(name, grad, from_sibling, blocks):
    _, rows, cols = grad.shape
    (br_, bc), steps, at = _shard_tiles(rows, cols, 256)

    def body(blk_ref, g_ref, s_ref, o_ref):
        o_ref[...] = (g_ref[...] + s_ref[...]).astype(BF16)

    return pl.pallas_call(
        body, name=name,
        grid_spec=pltpu.PrefetchScalarGridSpec(
            num_scalar_prefetch=1, grid=(3, steps),
            in_specs=[pl.BlockSpec((None, br_, bc), lambda j, r, br: (br[j], *at(r))),
                      pl.BlockSpec((None, br_, bc), lambda j, r, br: (br[3 + j], *at(r)))],
            out_specs=pl.BlockSpec((None, br_, bc), lambda j, r, br: (j, *at(r)))),
        out_shape=jax.ShapeDtypeStruct((3, rows, cols), BF16),
        compiler_params=_cparams(("parallel", "parallel")),
    )(blocks, grad, from_sibling)


def _adam_math(g, w, m, v):
    m2 = ADAM_B1 * m + (1.0 - ADAM_B1) * g
    v2 = ADAM_B2 * v + (1.0 - ADAM_B2) * (g * g)
    m_hat = m2 / (1.0 - ADAM_B1 ** ADAM_STEP)
    v_hat = v2 / (1.0 - ADAM_B2 ** ADAM_STEP)
    delta = -ADAM_LR * (m_hat / (jnp.sqrt(v_hat) + ADAM_EPS) + ADAM_WD * w)
    return delta, m2, v2


def _sum_adam(name, grad, from_sibling, from_chips, own, w, m, v, dep=None):
    rows, cols = w.shape
    (br_, bc), steps, at = _shard_tiles(rows, cols, 128)

    deps, dep_specs = _dep_operand(dep)

    def body(own_ref, p_ref, s_ref, r_ref, w_ref, m_ref, v_ref, *rest):
        g_out, d_out, m_out, v_out = rest[len(deps):]
        g = p_ref[...] + s_ref[...]
        for j in range(3):
            g = g + r_ref[j].astype(F32)
        d, m2, v2 = _adam_math(g, w_ref[...], m_ref[...], v_ref[...])
        g_out[...] = g
        d_out[...] = d
        m_out[...] = m2
        v_out[...] = v2

    blk = pl.BlockSpec((br_, bc), lambda r, cr: at(r))
    shp = jax.ShapeDtypeStruct((rows, cols), F32)
    return pl.pallas_call(
        body, name=name,
        grid_spec=pltpu.PrefetchScalarGridSpec(
            num_scalar_prefetch=1, grid=(steps,),
            in_specs=[pl.BlockSpec((None, br_, bc), lambda r, cr: (cr[0], *at(r))),
                      pl.BlockSpec((None, br_, bc), lambda r, cr: (cr[1], *at(r))),
                      pl.BlockSpec((3, br_, bc), lambda r, cr: (0, *at(r))),
                      blk, blk, blk, *dep_specs],
            out_specs=[blk, blk, blk, blk]),
        out_shape=[shp, shp, shp, shp],
        compiler_params=_cparams(("parallel",)),
    )(own, grad, from_sibling, from_chips, w, m, v, *deps)


def _adam_small(name, g, w, m, v):
    def body(g_ref, w_ref, m_ref, v_ref, d_out, m_out, v_out):
        d, m2, v2 = _adam_math(g_ref[...], w_ref[...], m_ref[...], v_ref[...])
        d_out[...] = d
        m_out[...] = m2
        v_out[...] = v2

    shp = jax.ShapeDtypeStruct(w.shape, F32)
    return pl.pallas_call(body, name=name, out_shape=[shp, shp, shp])(g, w, m, v)


def _all_reduce_small(pack):
    rows = pack.shape[0]

    def body(in_ref, out_ref, slots, send, recv):
        x, y, c, _ = _place()
        me = 4 * x + 2 * y + c
        slots[me] = in_ref[...]
        cps = []
        for k in range(1, N_DEV):
            dx, dy, dc = (k >> 2) & 1, (k >> 1) & 1, k & 1
            to = (jnp.bitwise_xor(x, dx), jnp.bitwise_xor(y, dy), jnp.bitwise_xor(c, dc))
            cps.append(pltpu.make_async_remote_copy(
                src_ref=in_ref, dst_ref=slots.at[me], send_sem=send.at[k - 1], recv_sem=recv.at[k - 1],
                device_id=to, device_id_type=MESH))
        for cp in cps:
            cp.start()
        for cp in cps:
            cp.wait()
        acc = slots[0]
        for d in range(1, N_DEV):
            acc = acc + slots[d]
        out_ref[...] = acc

    return pl.pallas_call(
        body, name="all_reduce_small",
        in_specs=[pl.BlockSpec(memory_space=pltpu.VMEM)],
        out_specs=pl.BlockSpec(memory_space=pltpu.VMEM),
        out_shape=jax.ShapeDtypeStruct((rows, LANES), F32),
        scratch_shapes=[pltpu.VMEM((N_DEV, rows, LANES), F32),
                        pltpu.SemaphoreType.DMA((N_DEV - 1,)), pltpu.SemaphoreType.DMA((N_DEV - 1,))],
    )(pack)


def _rows128(a):
    flat = a.reshape(-1)
    padn = (-flat.shape[0]) % (8 * LANES)
    if padn:
        flat = jnp.concatenate([flat, jnp.zeros((padn,), flat.dtype)])
    return flat.reshape(-1, LANES)


def kernel(x, w_in, w_gk2, b_gk, gla_norm_w, swa_sinks, w_out, ln1_g, ln1_b, w_up, w_down, ln2_g, ln2_b, loss_target, m_w_in, m_w_gk2, m_b_gk, m_gla_norm_w, m_swa_sinks, m_w_out, m_ln1_g, m_ln1_b, m_w_up, m_w_down, m_ln2_g, m_ln2_b, v_w_in, v_w_gk2, v_b_gk, v_gla_norm_w, v_swa_sinks, v_w_out, v_ln1_g, v_ln1_b, v_w_up, v_w_down, v_ln2_g, v_ln2_b):
    xc, yc, cc = lax.axis_index("x"), lax.axis_index("y"), lax.axis_index("c")
    me = 4 * xc + 2 * yc + cc

    x2 = x[0]
    t = x2.shape[0]
    xb = x2.astype(BF16)
    target = loss_target[0]

    first = _all_gather([w_in[0].T.astype(BF16), w_gk2[0].astype(BF16)])
    w_in_t = _to_padded_rows(first[0].reshape(D_IN, D_MODEL))
    w2 = first[1].transpose(1, 0, 2).reshape(GATE_RANK, GLA_KW)
    w2p = jnp.concatenate([w2, jnp.zeros((LANES - GATE_RANK, GLA_KW), BF16)], axis=0)
    shards = [w_out[0].astype(BF16), w_up[0].astype(BF16), w_down[0].astype(BF16)]
    lands = [lax.dynamic_update_index_in_dim(lax.empty((N_DEV,) + sh.shape, BF16), sh, me, 0) for sh in shards]
    out_plans = [_plan_gather_out] * 3
    fwd_plans = [_plan_gather_forward] * 3
    ag1 = _copies_start("gather_out_start", out_plans, shards, lands, first[0])

    proj = _proj(xb, w_in_t, dep=ag1[4])
    rt = min(256, t)
    ii = jnp.arange(rt)
    tri = ((ii[:, None] // GC == ii[None, :] // GC) & (ii[None, :] <= ii[:, None])).astype(F32)
    bloc, dgdz = _gate_fwd(proj, w2p, b_gk, tri)
    o_raw, states = _gla_fwd(proj, bloc)
    _, lands = _copies_wait("gather_out_wait", out_plans, ag1, o_raw)
    ag2 = _copies_start("gather_forward_start", fwd_plans, [None] * 3, lands, o_raw)
    gla_out = _gla_norm_fwd(o_raw, proj, gla_norm_w)
    cat = _swa_fwd(proj, swa_sinks[0], gla_out, dep=ag2[4])
    _, gathered = _copies_wait("gather_forward_wait", fwd_plans, ag2, cat)
    w_out_f = gathered[0].reshape(D_MODEL, D_MODEL)
    w_up_f = gathered[1]
    w_down_f = gathered[2].reshape(D_FF, D_MODEL)
    r1, h1, h1b = _mix_ln1(cat, w_out_f, x2, ln1_g, ln1_b)
    a_act, hdn = _mlp_up(h1b, w_up_f)
    dr2, dr2b, g_ln2_g, g_ln2_b, loss_part = _mlp_down_loss(hdn, w_down_f, h1, target, ln2_g, ln2_b)

    others = [2 * (1 - xc) + yc, 2 * xc + (1 - yc), 2 * (1 - xc) + (1 - yc)]
    blocks = jnp.stack([2 * q + cc for q in others] + others).astype(jnp.int32)
    own = jnp.stack([me, 2 * xc + yc]).astype(jnp.int32)
    wmv = dict(w_in=[a[0].T for a in (w_in, m_w_in, v_w_in)], w_out=[a[0] for a in (w_out, m_w_out, v_w_out)],
               w_up=[a[0] for a in (w_up, m_w_up, v_w_up)], w_down=[a[0] for a in (w_down, m_w_down, v_w_down)])
    big = {}

    def sib_land(g):
        return lax.empty((4,) + g.shape[1:], F32)

    def chip_land(g):
        return lax.empty((3,) + g.shape[1:], BF16)

    def finish(nm, g, from_sib, from_chips, dep=None):
        outs = _sum_adam("sum_adam_" + nm, g, from_sib, from_chips, own, *wmv[nm], dep=dep)
        big[nm] = [(o.T if nm == "w_in" else o)[None] for o in outs]

    du = _mlp_down_bwd(dr2b, w_down_f, a_act)
    g_down = _grad_w_down(hdn, dr2b)
    sa_down = _copies_start("sibling_down_start", [_plan_sibling], [g_down], [sib_land(g_down)])
    g_up = _grad_w_up(h1b, du, dep=sa_down[4])
    (g_down,), (fs_down,) = _copies_wait("sibling_down_wait", [_plan_sibling], sa_down, g_up)
    p_down = _pair_sum("pair_sum_w_down", g_down, fs_down, blocks)
    sb_down = _copies_start("chips_down_sibling_up_start", [_plan_chips, _plan_sibling], [p_down, g_up],
                            [chip_land(g_down), sib_land(g_up)])
    dr1, dr1b, g_ln1_g, g_ln1_b = _mlp_up_bwd_ln1(du, w_up_f, dr2, r1, ln1_g, dep=sb_down[4])
    dcat = _dcat(dr1b, w_out_f)
    (_, g_up), (fc_down, fs_up) = _copies_wait("chips_down_sibling_up_wait", [_plan_chips, _plan_sibling], sb_down,
                                               dcat)
    finish("w_down", g_down, fs_down, fc_down)
    p_up = _pair_sum("pair_sum_w_up", g_up, fs_up, blocks)
    sb_up = _copies_start("chips_up_start", [_plan_chips], [p_up], [chip_land(g_up)])
    do_raw, dg_g, g_norm_w = _gla_norm_bwd(dcat, o_raw, proj, gla_norm_w, dep=sb_up[4])
    dq_g, dk_g, dv_g, dgk = _gla_bwd(proj, bloc, do_raw, states)
    _, (fc_up,) = _copies_wait("chips_up_wait", [_plan_chips], sb_up, dgk)
    finish("w_up", g_up, fs_up, fc_up)
    dlo, gw2_p, g_b_gk = _gate_bwd(dgk, dgdz, proj, w2p, tri.T)
    dq_s, dk_s, dv_s, g_sinks = _swa_bwd(proj, dcat, swa_sinks[0])
    dproj = jnp.concatenate([dq_g, dk_g, dv_g, dg_g, dq_s, dk_s.astype(BF16), dv_s.astype(BF16), dlo], axis=-1)
    gw_in_t = _grad_w_in(xb, dproj)
    g_in = _from_padded_rows(gw_in_t).reshape(N_DEV, D_IN // N_DEV, D_MODEL)
    sa_in = _copies_start("sibling_in_start", [_plan_sibling], [g_in], [sib_land(g_in)])
    g_out = _grad_w_out(cat, dr1b, dep=sa_in[4])
    (g_in,), (fs_in,) = _copies_wait("sibling_in_wait", [_plan_sibling], sa_in, g_out)
    p_in = _pair_sum("pair_sum_w_in", g_in, fs_in, blocks)
    sb_in = _copies_start("chips_in_sibling_out_start", [_plan_chips, _plan_sibling], [p_in, g_out],
                          [chip_land(g_in), sib_land(g_out)])
    grad_x = _grad_x(dproj, w_in_t, dr1, dep=sb_in[4])
    (_, g_out), (fc_in, fs_out) = _copies_wait("chips_in_sibling_out_wait", [_plan_chips, _plan_sibling], sb_in,
                                               grad_x)
    p_out = _pair_sum("pair_sum_w_out", g_out, fs_out, blocks)
    sb_out = _copies_start("chips_out_start", [_plan_chips], [p_out], [chip_land(g_out)])
    finish("w_in", g_in, fs_in, fc_in, dep=sb_out[4])

    pieces = [loss_part, g_b_gk, g_norm_w, g_sinks[:, :SWA_HEADS], g_ln1_g, g_ln1_b, g_ln2_g, g_ln2_b,
              gw2_p[:GATE_RANK]]
    pack = jnp.concatenate([_rows128(p) for p in pieces], axis=0)
    tot = _all_reduce_small(pack + sb_out[4][:1, :1])
    sizes = [p.size for p in pieces]
    offs = [0]
    for p in pieces:
        offs.append(offs[-1] + _rows128(p).shape[0])
    unpack = lambda i, shape: tot[offs[i]:offs[i + 1]].reshape(-1)[:sizes[i]].reshape(shape)
    loss = tot[0, 0]
    small_names = ["b_gk", "gla_norm_w", "swa_sinks", "ln1_g", "ln1_b", "ln2_g", "ln2_b"]
    small_w = dict(b_gk=(b_gk, m_b_gk, v_b_gk), gla_norm_w=(gla_norm_w, m_gla_norm_w, v_gla_norm_w),
                   swa_sinks=(swa_sinks, m_swa_sinks, v_swa_sinks), ln1_g=(ln1_g, m_ln1_g, v_ln1_g),
                   ln1_b=(ln1_b, m_ln1_b, v_ln1_b), ln2_g=(ln2_g, m_ln2_g, v_ln2_g),
                   ln2_b=(ln2_b, m_ln2_b, v_ln2_b))
    small_g = {nm: unpack(1 + i, small_w[nm][0].shape) for i, nm in enumerate(small_names)}
    g_pack = jnp.concatenate([_rows128(small_g[nm]) for nm in small_names], axis=0)
    small_wmv = [jnp.concatenate([_rows128(small_w[nm][k]) for nm in small_names], axis=0) for k in range(3)]
    small_out = _adam_small("adam_replicated", g_pack, *small_wmv)
    srow = [0]
    for nm in small_names:
        srow.append(srow[-1] + _rows128(small_w[nm][0]).shape[0])
    small = {}
    for i, nm in enumerate(small_names):
        shape = small_w[nm][0].shape
        n = small_w[nm][0].size
        small[nm] = [small_g[nm]] + [o[srow[i]:srow[i + 1]].reshape(-1)[:n].reshape(shape) for o in small_out]

    gw2_full = unpack(8, (GATE_RANK, GLA_KW))
    gw2_loc = lax.dynamic_slice_in_dim(gw2_full, me * (GLA_KW // N_DEV), GLA_KW // N_DEV, axis=1)
    gk2_out = _adam_small("adam_w_gk2", gw2_loc, w_gk2[0], m_w_gk2[0], v_w_gk2[0])
    big["w_gk2"] = [gw2_loc[None]] + [o[None] for o in gk2_out]

    _, (fc_out,) = _copies_wait("chips_out_wait", [_plan_chips], sb_out, tot)
    finish("w_out", g_out, fs_out, fc_out)

    order = ["w_in", "w_gk2", "b_gk", "gla_norm_w", "swa_sinks", "w_out", "ln1_g", "ln1_b", "w_up", "w_down",
             "ln2_g", "ln2_b"]
    res = {**big, **small}
    outs = [loss, grad_x[None]]
    for k in range(4):
        outs += [res[nm][k] for nm in order]
    return tuple(outs)
```

```python
import functools

import jax
import jax.numpy as jnp
from jax import lax
from jax.experimental import pallas as pl
from jax.experimental.pallas import tpu as pltpu

F32 = jnp.float32
BF16 = jnp.bfloat16

N_DEV = 8
D_MODEL = 2048
D_FF = 8192
GLA_HEADS = 4
GLA_DK = 128
GLA_DV = 256
GLA_KW = 512
D_GLA = 1024
GATE_RANK = 16
GATE_TAU = 16.0
SWA_HEADS = 16
SWA_DH = 64
SWA_GROUP = 8
WINDOW = 128
D_SWA = 1024
D_IN = 4368
ALPHA = 2.0 ** 0.25
LN_EPS = 1e-5
RMS_EPS = 1e-5
ADAM_LR = 0.001
ADAM_B1 = 0.9
ADAM_B2 = 0.999
ADAM_EPS = 1e-08
ADAM_WD = 0.01
ADAM_STEP = 10

C_QG, C_KG, C_VG, C_GG, C_QS, C_KS, C_VS, C_LO = 0, 512, 1024, 2048, 3072, 4096, 4224, 4352
D_INP = 4480
LANES = 128
GC = 16
NEG = -1e30

NN = ((1,), (0,))
NT = ((1,), (1,))
TN = ((0,), (0,))

VMEM_LIMIT = 52 * 1024 * 1024
MESH = pl.DeviceIdType.MESH


def _dot(a, b, dn, precision=None):
    return lax.dot_general(a, b, (dn, ((), ())), preferred_element_type=F32, precision=precision)


def _bf16_round(v):
    return v.astype(BF16).astype(F32)


def _cparams(dims):
    return pltpu.CompilerParams(dimension_semantics=dims, vmem_limit_bytes=VMEM_LIMIT)


def _dep_operand(dep):
    if dep is None:
        return (), ()
    return (dep,), (pl.BlockSpec(dep.shape, lambda *_: (0,) * dep.ndim),)


def _to_padded_rows(w):
    pad = jnp.zeros((D_INP - D_IN,) + w.shape[1:], w.dtype)
    return jnp.concatenate([w[:3072], w[3088:], w[3072:3088], pad], axis=0)


def _from_padded_rows(g):
    return jnp.concatenate([g[:3072], g[C_LO:C_LO + GATE_RANK], g[3072:C_LO]], axis=0)


def _matmul(name, a, b, dn, grid, a_spec, b_spec, out_shape, out_specs, acc_shape, *,
            extra=(), extra_specs=(), epilogue=None, dims=("parallel", "parallel", "arbitrary"), dep=None):
    nk = grid[2]
    n_extra = len(extra)
    deps, dep_specs = _dep_operand(dep)
    direct = epilogue is None and (nk == 1 or (not isinstance(out_shape, (list, tuple)) and out_shape.dtype == F32))

    scratch = [] if direct or nk == 1 else [pltpu.VMEM(acc_shape, F32)]

    def body(a_ref, b_ref, *rest):
        extra_refs = rest[:n_extra]
        out_refs = rest[n_extra + len(deps):len(rest) - len(scratch)]
        acc_ref = rest[-1] if scratch else out_refs[0]
        part = _dot(a_ref[...].astype(BF16), b_ref[...].astype(BF16), dn)

        def finish():
            if epilogue is None:
                out_refs[0][...] = acc_ref[...].astype(out_refs[0].dtype)
            else:
                epilogue(acc_ref, extra_refs, out_refs)

        if direct and nk == 1:
            out_refs[0][...] = part.reshape(out_refs[0].shape).astype(out_refs[0].dtype)
        elif nk == 1:
            epilogue(part, extra_refs, out_refs)
        else:
            k = pl.program_id(2)

            @pl.when(k == 0)
            def _():
                acc_ref[...] = part

            @pl.when(k > 0)
            def _():
                acc_ref[...] += part

            if not direct:
                @pl.when(k == nk - 1)
                def _():
                    finish()

    return pl.pallas_call(
        body, name=name, grid=grid,
        in_specs=[a_spec, b_spec, *extra_specs, *dep_specs],
        out_specs=out_specs, out_shape=out_shape,
        scratch_shapes=scratch,
        compiler_params=_cparams(dims),
    )(a, b, *extra, *deps)


def _row_chunks(rows, step=128):
    step = min(step, rows)
    return [(r, step) for r in range(0, rows, step)]


def _ln_stats(r):
    mu = jnp.mean(r, axis=-1, keepdims=True)
    xc = r - mu
    var = jnp.mean(xc * xc, axis=-1, keepdims=True)
    rstd = lax.rsqrt(var + LN_EPS)
    return xc * rstd, rstd


def _ln_bwd(dy_g, xhat, rstd):
    m1 = jnp.mean(dy_g, axis=-1, keepdims=True)
    m2 = jnp.mean(dy_g * xhat, axis=-1, keepdims=True)
    return rstd * (dy_g - m1 - xhat * m2)


def _proj(xb, w_in_t, dep=None):
    t = xb.shape[0]
    tm, tn = min(1024, t), 640
    return _matmul(
        "proj", xb, w_in_t, NT, (t // tm, D_INP // tn, 1),
        pl.BlockSpec((tm, D_MODEL), lambda i, j, k: (i, 0)),
        pl.BlockSpec((tn, D_MODEL), lambda i, j, k: (j, 0)),
        jax.ShapeDtypeStruct((t, D_INP), F32),
        pl.BlockSpec((tm, tn), lambda i, j, k: (i, j)),
        (tm, tn), dep=dep)


def _gate_fwd(proj, w2p, b_gk, tri):
    t = proj.shape[0]
    r = tri.shape[0]

    def body(lo_ref, w_ref, b_ref, tri_ref, bloc_ref, dgdz_ref):
        z = _dot(lo_ref[...].astype(BF16), w_ref[...], NN) + b_ref[...]
        e = jnp.exp(-jnp.abs(z))
        gk = (jnp.minimum(z, 0.0) - jnp.log1p(e)) * (1.0 / GATE_TAU)
        inv = 1.0 / (1.0 + e)
        dgdz_ref[...] = jnp.where(z >= 0.0, e * inv, inv) * (1.0 / GATE_TAU)
        bloc_ref[...] = _dot(tri_ref[...], gk, NN, precision=lax.Precision.HIGHEST)

    return pl.pallas_call(
        body, name="gate_fwd", grid=(t // r,),
        in_specs=[pl.BlockSpec((r, LANES), lambda i: (i, C_LO // LANES)),
                  pl.BlockSpec((LANES, GLA_KW), lambda i: (0, 0)),
                  pl.BlockSpec((1, GLA_KW), lambda i: (0, 0)),
                  pl.BlockSpec((r, r), lambda i: (0, 0))],
        out_specs=[pl.BlockSpec((r, GLA_KW), lambda i: (i, 0)),
                   pl.BlockSpec((r, GLA_KW), lambda i: (i, 0))],
        out_shape=[jax.ShapeDtypeStruct((t, GLA_KW), F32), jax.ShapeDtypeStruct((t, GLA_KW), F32)],
        compiler_params=_cparams(("parallel",)),
    )(proj, w2p, b_gk, tri)


def _gla_fwd(proj, bloc):
    t = proj.shape[0]
    r = min(256, t)
    ncb = r // GC
    scale = GLA_DK ** -0.5

    def body(q_ref, k_ref, v_ref, b_ref, o_ref, st_ref, s_scr, m_scr):
        @pl.when(pl.program_id(1) == 0)
        def _():
            s_scr[...] = jnp.zeros_like(s_scr)

        rows = lax.broadcasted_iota(jnp.int32, (GC, 1), 0)
        cols = lax.broadcasted_iota(jnp.int32, (1, GC), 1)

        def increment(c, carry):
            r0 = pl.multiple_of(c * GC, GC)
            k = k_ref[pl.ds(r0, GC), :]
            b = b_ref[pl.ds(r0, GC), :]
            kd = k * jnp.exp(b[GC - 1:GC, :] - b)
            m_scr[c] = _dot(v_ref[pl.ds(r0, GC), :].astype(BF16), kd.astype(BF16), TN)
            return carry

        lax.fori_loop(0, ncb, increment, 0, unroll=8)

        def recur(c, st):
            st_ref[c] = st
            r0 = pl.multiple_of(c * GC, GC)
            bl = b_ref[pl.ds(r0, GC), :][GC - 1:GC, :]
            return st * jnp.exp(bl) + m_scr[c]

        s_scr[...] = lax.fori_loop(0, ncb, recur, s_scr[...])

        def output(c, carry):
            r0 = pl.multiple_of(c * GC, GC)
            q = q_ref[pl.ds(r0, GC), :] * scale
            b = b_ref[pl.ds(r0, GC), :]
            kr = _bf16_round(k_ref[pl.ds(r0, GC), :])
            att = jnp.zeros((GC, GC), F32)
            for j in range(GC):
                w = jnp.exp(jnp.where(rows >= j, b - b[j:j + 1, :], NEG))
                a = jnp.sum(_bf16_round(q * w) * kr[j:j + 1, :], axis=-1, keepdims=True)
                att = jnp.where(cols == j, a, att)
            o_ref[pl.ds(r0, GC), :] = (_dot((q * jnp.exp(b)).astype(BF16), st_ref[c].astype(BF16), NT)
                                       + _dot(att.astype(BF16), v_ref[pl.ds(r0, GC), :].astype(BF16), NN))
            return carry

        lax.fori_loop(0, ncb, output, 0, unroll=8)

    return pl.pallas_call(
        body, name="gla_fwd", grid=(GLA_HEADS, t // r),
        in_specs=[pl.BlockSpec((r, GLA_DK), lambda h, i: (i, C_QG // GLA_DK + h)),
                  pl.BlockSpec((r, GLA_DK), lambda h, i: (i, C_KG // GLA_DK + h)),
                  pl.BlockSpec((r, GLA_DV), lambda h, i: (i, C_VG // GLA_DV + h)),
                  pl.BlockSpec((r, GLA_DK), lambda h, i: (i, h))],
        out_specs=[pl.BlockSpec((r, GLA_DV), lambda h, i: (i, h)),
                   pl.BlockSpec((ncb, None, GLA_DV, GLA_DK), lambda h, i: (i, h, 0, 0))],
        out_shape=[jax.ShapeDtypeStruct((t, D_GLA), F32),
                   jax.ShapeDtypeStruct((t // GC, GLA_HEADS, GLA_DV, GLA_DK), F32)],
        scratch_shapes=[pltpu.VMEM((GLA_DV, GLA_DK), F32), pltpu.VMEM((ncb, GLA_DV, GLA_DK), F32)],
        compiler_params=_cparams(("parallel", "arbitrary")),
    )(proj, proj, proj, bloc)


def _gla_norm_fwd(o_raw, proj, norm_w):
    t = o_raw.shape[0]
    r = min(512, t)

    def body(o_ref, g_ref, w_ref, out_ref):
        w = w_ref[...]
        for h in range(GLA_HEADS):
            sl = slice(h * GLA_DV, (h + 1) * GLA_DV)
            o = o_ref[:, sl]
            g = g_ref[:, sl]
            on = o * lax.rsqrt(jnp.mean(o * o, axis=-1, keepdims=True) + RMS_EPS)
            out_ref[:, sl] = (on * w * (g * jax.nn.sigmoid(g))).astype(BF16)

    return pl.pallas_call(
        body, name="gla_norm_fwd", grid=(t // r,),
        in_specs=[pl.BlockSpec((r, D_GLA), lambda i: (i, 0)),
                  pl.BlockSpec((r, D_GLA), lambda i: (i, C_GG // D_GLA)),
                  pl.BlockSpec((1, GLA_DV), lambda i: (0, 0))],
        out_specs=pl.BlockSpec((r, D_GLA), lambda i: (i, 0)),
        out_shape=jax.ShapeDtypeStruct((t, D_GLA), BF16),
        compiler_params=_cparams(("parallel",)),
    )(o_raw, proj, norm_w)


def _swa_masks(i, heads):
    qi = lax.broadcasted_iota(jnp.int32, (heads * WINDOW, 1), 0) & (WINDOW - 1)
    kj = lax.broadcasted_iota(jnp.int32, (1, 2 * WINDOW), 1)
    valid = (kj > qi) & (kj <= qi + WINDOW) & ((i > 0) | (kj >= WINDOW))
    lo = lax.broadcasted_iota(jnp.int32, (1, LANES), 1) < SWA_DH
    return valid, lo


def _dup_half(x, lo, kv):
    xr = pltpu.roll(x, SWA_DH, axis=1)
    return jnp.where(lo, x, xr) if kv == 0 else jnp.where(lo, xr, x)


def _swa_stack(ref, col0, lo, kv):
    parts = []
    for p in range(4):
        c0 = col0 + LANES * (4 * kv + p)
        xp = ref[:, c0:c0 + LANES]
        parts += [jnp.where(lo, xp, 0.0).astype(BF16), jnp.where(lo, 0.0, xp).astype(BF16)]
    return jnp.concatenate(parts, axis=0)


def _swa_unstack(x_all, lo, p):
    r0 = 2 * p * WINDOW
    return jnp.where(lo, x_all[r0:r0 + WINDOW, :], x_all[r0 + WINDOW:r0 + 2 * WINDOW, :])


def _swa_sinks(sink_ref, kv):
    return jnp.concatenate([jnp.full((WINDOW, 1), sink_ref[SWA_GROUP * kv + h], F32) for h in range(SWA_GROUP)],
                           axis=0)


def _swa_probs(qm, kdup, valid, sink):
    s = _dot(qm, kdup, NT) * (SWA_DH ** -0.5)
    s = jnp.where(valid, s, NEG)
    m = jnp.maximum(jnp.max(s, axis=-1, keepdims=True), sink)
    p = jnp.exp(s - m)
    es = jnp.exp(sink - m)
    inv = 1.0 / (jnp.sum(p, axis=-1, keepdims=True) + es)
    return p * inv, es * inv


def _swa_fwd(proj, sinks, gla_out, dep=None):
    t = proj.shape[0]

    deps, dep_specs = _dep_operand(dep)

    def body(sink_ref, q_ref, kp_ref, kc_ref, vp_ref, vc_ref, gla_ref, *rest):
        o_ref = rest[-1]
        i = pl.program_id(0)
        o_ref[:, :D_GLA] = gla_ref[...]
        valid, lo = _swa_masks(i, 1)
        kb = jnp.concatenate([kp_ref[...], kc_ref[...]], axis=0)
        vb = jnp.concatenate([vp_ref[...], vc_ref[...]], axis=0)
        for kv in range(2):
            kdup = _dup_half(kb, lo, kv).astype(BF16)
            vdup = _dup_half(vb, lo, kv).astype(BF16)
            for p in range(4):
                c0 = LANES * (4 * kv + p)
                qp = q_ref[:, c0:c0 + LANES]
                halves = []
                for e in range(2):
                    qm = jnp.where(lo if e == 0 else ~lo, qp, 0.0).astype(BF16)
                    pn, _ = _swa_probs(qm, kdup, valid, sink_ref[SWA_GROUP * kv + 2 * p + e])
                    halves.append(_dot(pn.astype(BF16), vdup, NN))
                o_ref[:, D_GLA + c0:D_GLA + c0 + LANES] = jnp.where(lo, halves[0], halves[1]).astype(BF16)

    kvspec = lambda col, prev: pl.BlockSpec(
        (WINDOW, LANES), (lambda i: (jnp.maximum(i - 1, 0), col)) if prev else (lambda i: (i, col)))
    return pl.pallas_call(
        body, name="swa_fwd", grid=(t // WINDOW,),
        in_specs=[pl.BlockSpec(memory_space=pltpu.SMEM),
                  pl.BlockSpec((WINDOW, D_SWA), lambda i: (i, C_QS // D_SWA)),
                  kvspec(C_KS // LANES, True), kvspec(C_KS // LANES, False),
                  kvspec(C_VS // LANES, True), kvspec(C_VS // LANES, False),
                  pl.BlockSpec((WINDOW, D_GLA), lambda i: (i, 0)), *dep_specs],
        out_specs=pl.BlockSpec((WINDOW, D_MODEL), lambda i: (i, 0)),
        out_shape=jax.ShapeDtypeStruct((t, D_MODEL), BF16),
        compiler_params=_cparams(("parallel",)),
    )(sinks, proj, proj, proj, proj, proj, gla_out, *deps)


def _mix_ln1(cat, w_out, x, ln_g, ln_b):
    t = cat.shape[0]
    tm, tk = min(256, t), D_MODEL

    def epilogue(acc_ref, extra, outs):
        x_ref, g_ref, b_ref = extra
        r1_ref, h1_ref, h1b_ref = outs
        for r0, n in _row_chunks(tm):
            rs = slice(r0, r0 + n)
            r1 = ALPHA * x_ref[rs, :] + acc_ref[rs, :]
            xhat, _ = _ln_stats(r1)
            h = xhat * g_ref[...] + b_ref[...]
            r1_ref[rs, :] = r1
            h1_ref[rs, :] = h
            h1b_ref[rs, :] = h.astype(BF16)

    row = pl.BlockSpec((tm, D_MODEL), lambda i, j, k: (i, 0))
    vec = pl.BlockSpec((1, D_MODEL), lambda i, j, k: (0, 0))
    return _matmul(
        "mix_ln1", cat, w_out, NN, (t // tm, 1, D_MODEL // tk),
        pl.BlockSpec((tm, tk), lambda i, j, k: (i, k)),
        pl.BlockSpec((tk, D_MODEL), lambda i, j, k: (k, 0)),
        [jax.ShapeDtypeStruct((t, D_MODEL), F32), jax.ShapeDtypeStruct((t, D_MODEL), F32),
         jax.ShapeDtypeStruct((t, D_MODEL), BF16)],
        [row, row, row], (tm, D_MODEL),
        extra=(x, ln_g, ln_b), extra_specs=(row, vec, vec), epilogue=epilogue)


def _mlp_up(h1b, w_up):
    t = h1b.shape[0]
    tm, tn = min(1024, t), 1024

    def epilogue(acc_ref, extra, outs):
        a_ref, hdn_ref = outs
        for r0, n in _row_chunks(tm, 256):
            rs = slice(r0, r0 + n)
            a = jnp.maximum(acc_ref[rs, :], 0.0)
            a_ref[rs, :] = a.astype(BF16)
            hdn_ref[rs, :] = (a * a).astype(BF16)

    out = pl.BlockSpec((tm, tn), lambda i, j, k: (i, j))
    return _matmul(
        "mlp_up", h1b, w_up, NN, (t // tm, D_FF // tn, 1),
        pl.BlockSpec((tm, D_MODEL), lambda i, j, k: (i, 0)),
        pl.BlockSpec((None, D_MODEL, tn), lambda i, j, k: (j, 0, 0)),
        [jax.ShapeDtypeStruct((t, D_FF), BF16), jax.ShapeDtypeStruct((t, D_FF), BF16)],
        [out, out], (tm, tn), epilogue=epilogue)


def _mlp_down_loss(hdn, w_down, h1, target, ln_g, ln_b):
    t = hdn.shape[0]
    tm, tn, tk = min(1024, t), 1024, 4096
    ff = _matmul(
        "mlp_down", hdn, w_down, NN, (t // tm, D_MODEL // tn, D_FF // tk),
        pl.BlockSpec((tm, tk), lambda i, j, k: (i, k)),
        pl.BlockSpec((tk, tn), lambda i, j, k: (k, j)),
        jax.ShapeDtypeStruct((t, D_MODEL), F32),
        pl.BlockSpec((tm, tn), lambda i, j, k: (i, j)), (tm, tn))
    r = min(256, t)

    def body(ff_ref, h1_ref, t_ref, g_ref, b_ref, dr2_ref, dr2b_ref, gg_ref, gb_ref, loss_ref):
        @pl.when(pl.program_id(0) == 0)
        def _():
            gg_ref[...] = jnp.zeros_like(gg_ref)
            gb_ref[...] = jnp.zeros_like(gb_ref)
            loss_ref[...] = jnp.zeros_like(loss_ref)

        for r0, n in _row_chunks(r, 64):
            rs = slice(r0, r0 + n)
            xhat, rstd = _ln_stats(ALPHA * h1_ref[rs, :] + ff_ref[rs, :])
            err = xhat * g_ref[...] + b_ref[...] - t_ref[rs, :]
            loss_ref[...] += 0.5 * jnp.sum(jnp.mean(err * err, axis=-1, keepdims=True))
            dy = err * (1.0 / D_MODEL)
            gg_ref[...] += jnp.sum(dy * xhat, axis=0, keepdims=True)
            gb_ref[...] += jnp.sum(dy, axis=0, keepdims=True)
            dr2 = _ln_bwd(dy * g_ref[...], xhat, rstd)
            dr2_ref[rs, :] = dr2
            dr2b_ref[rs, :] = dr2.astype(BF16)

    row = pl.BlockSpec((r, D_MODEL), lambda i: (i, 0))
    vec = pl.BlockSpec((1, D_MODEL), lambda i: (0, 0))
    return pl.pallas_call(
        body, name="ln2_loss", grid=(t // r,),
        in_specs=[row, row, row, vec, vec],
        out_specs=[row, row, vec, vec, pl.BlockSpec((1, LANES), lambda i: (0, 0))],
        out_shape=[jax.ShapeDtypeStruct((t, D_MODEL), F32), jax.ShapeDtypeStruct((t, D_MODEL), BF16),
                   jax.ShapeDtypeStruct((1, D_MODEL), F32), jax.ShapeDtypeStruct((1, D_MODEL), F32),
                   jax.ShapeDtypeStruct((1, LANES), F32)],
        compiler_params=_cparams(("arbitrary",)),
    )(ff, h1, target, ln_g, ln_b)


def _mlp_down_bwd(dr2b, w_down, a_act):
    t = dr2b.shape[0]
    tm, tn = min(1024, t), 1024

    def epilogue(acc_ref, extra, outs):
        (a_ref,) = extra
        for r0, n in _row_chunks(tm, 256):
            rs = slice(r0, r0 + n)
            outs[0][rs, :] = (acc_ref[rs, :] * (2.0 * a_ref[rs, :].astype(F32))).astype(BF16)

    blk = pl.BlockSpec((tm, tn), lambda i, j, k: (i, j))
    return _matmul(
        "mlp_down_bwd", dr2b, w_down, NT, (t // tm, D_FF // tn, 1),
        pl.BlockSpec((tm, D_MODEL), lambda i, j, k: (i, 0)),
        pl.BlockSpec((tn, D_MODEL), lambda i, j, k: (j, 0)),
        jax.ShapeDtypeStruct((t, D_FF), BF16), blk, (tm, tn),
        extra=(a_act,), extra_specs=(blk,), epilogue=epilogue)


def _grad_w_down(hdn, dr2b):
    t = hdn.shape[0]
    tm, tn = 1024, 1024
    return _matmul(
        "grad_w_down", hdn, dr2b, TN, (D_MODEL // tn, D_FF // tm, 1),
        pl.BlockSpec((t, tm), lambda j, i, k: (0, i)),
        pl.BlockSpec((t, tn), lambda j, i, k: (0, j)),
        jax.ShapeDtypeStruct((N_DEV, D_FF // N_DEV, D_MODEL), F32),
        pl.BlockSpec((None, tm, tn), lambda j, i, k: (i, 0, j)), (tm, tn))


def _grad_w_up(h1b, du, dep=None):
    t = h1b.shape[0]
    tm, tn = 1024, 1024
    return _matmul(
        "grad_w_up", h1b, du, TN, (N_DEV, D_MODEL // tm, 1),
        pl.BlockSpec((t, tm), lambda i, j, k: (0, j)),
        pl.BlockSpec((t, tn), lambda i, j, k: (0, i)),
        jax.ShapeDtypeStruct((N_DEV, D_MODEL, D_FF // N_DEV), F32),
        pl.BlockSpec((None, tm, tn), lambda i, j, k: (i, j, 0)), (tm, tn), dep=dep)


def _mlp_up_bwd_ln1(du, w_up, dr2, r1, ln_g, dep=None):
    t = du.shape[0]
    tm, tn, tk, nb = min(1024, t), 1024, D_FF // N_DEV, 4
    deps, dep_specs = _dep_operand(dep)

    def mm_body(a_ref, b_ref, *rest):
        o_ref = rest[-1]
        k = pl.program_id(2)
        part = _dot(a_ref[:, :tk], b_ref[0], NT)
        for d in range(1, nb):
            part = part + _dot(a_ref[:, d * tk:(d + 1) * tk], b_ref[d], NT)

        @pl.when(k == 0)
        def _():
            o_ref[...] = part

        @pl.when(k > 0)
        def _():
            o_ref[...] += part

    dff = pl.pallas_call(
        mm_body, name="mlp_up_bwd", grid=(t // tm, D_MODEL // tn, N_DEV // nb),
        in_specs=[pl.BlockSpec((tm, nb * tk), lambda i, j, k: (i, k)),
                  pl.BlockSpec((nb, tn, tk), lambda i, j, k: (k, j, 0)), *dep_specs],
        out_specs=pl.BlockSpec((tm, tn), lambda i, j, k: (i, j)),
        out_shape=jax.ShapeDtypeStruct((t, D_MODEL), F32),
        compiler_params=_cparams(("parallel", "parallel", "arbitrary")),
    )(du, w_up, *deps)
    r = min(256, t)

    def body(acc_ref, dr2_ref, r1_ref, g_ref, dr1_ref, dr1b_ref, gg_ref, gb_ref):
        @pl.when(pl.program_id(0) == 0)
        def _():
            gg_ref[...] = jnp.zeros_like(gg_ref)
            gb_ref[...] = jnp.zeros_like(gb_ref)

        for r0, n in _row_chunks(r, 64):
            rs = slice(r0, r0 + n)
            dh1 = ALPHA * dr2_ref[rs, :] + acc_ref[rs, :]
            xhat, rstd = _ln_stats(r1_ref[rs, :])
            gg_ref[...] += jnp.sum(dh1 * xhat, axis=0, keepdims=True)
            gb_ref[...] += jnp.sum(dh1, axis=0, keepdims=True)
            dr1 = _ln_bwd(dh1 * g_ref[...], xhat, rstd)
            dr1_ref[rs, :] = dr1
            dr1b_ref[rs, :] = dr1.astype(BF16)

    row = pl.BlockSpec((r, D_MODEL), lambda i: (i, 0))
    vec = pl.BlockSpec((1, D_MODEL), lambda i: (0, 0))
    return pl.pallas_call(
        body, name="ln1_bwd", grid=(t // r,),
        in_specs=[row, row, row, vec],
        out_specs=[row, row, vec, vec],
        out_shape=[jax.ShapeDtypeStruct((t, D_MODEL), F32), jax.ShapeDtypeStruct((t, D_MODEL), BF16),
                   jax.ShapeDtypeStruct((1, D_MODEL), F32), jax.ShapeDtypeStruct((1, D_MODEL), F32)],
        compiler_params=_cparams(("arbitrary",)),
    )(dff, dr2, r1, ln_g)


def _dcat(dr1b, w_out):
    t = dr1b.shape[0]
    tm, tn = min(1024, t), 1024
    return _matmul(
        "dcat", dr1b, w_out, NT, (t // tm, D_MODEL // tn, 1),
        pl.BlockSpec((tm, D_MODEL), lambda i, j, k: (i, 0)),
        pl.BlockSpec((tn, D_MODEL), lambda i, j, k: (j, 0)),
        jax.ShapeDtypeStruct((t, D_MODEL), F32),
        pl.BlockSpec((tm, tn), lambda i, j, k: (i, j)), (tm, tn))


def _grad_w_out(cat, dr1b, dep=None):
    t = cat.shape[0]
    tm, tn = 1024, 1024
    return _matmul(
        "grad_w_out", cat, dr1b, TN, (D_MODEL // tm, D_MODEL // tn, 1),
        pl.BlockSpec((t, tm), lambda i, j, k: (0, i)),
        pl.BlockSpec((t, tn), lambda i, j, k: (0, j)),
        jax.ShapeDtypeStruct((N_DEV, D_MODEL // N_DEV, D_MODEL), F32),
        pl.BlockSpec((tm // (D_MODEL // N_DEV), D_MODEL // N_DEV, tn), lambda i, j, k: (i, 0, j)), (tm, tn), dep=dep)


def _gla_norm_bwd(dcat, o_raw, proj, norm_w, dep=None):
    t = o_raw.shape[0]
    r = min(512, t)

    deps, dep_specs = _dep_operand(dep)

    def body(d_ref, o_ref, g_ref, w_ref, *rest):
        do_ref, dg_ref, dw_ref = rest[len(deps):]

        @pl.when(pl.program_id(0) == 0)
        def _():
            dw_ref[...] = jnp.zeros_like(dw_ref)

        w = w_ref[...]
        dw = jnp.zeros((1, GLA_DV), F32)
        for h in range(GLA_HEADS):
            sl = slice(h * GLA_DV, (h + 1) * GLA_DV)
            o = o_ref[:, sl]
            g = g_ref[:, sl]
            d = d_ref[:, sl]
            rr = lax.rsqrt(jnp.mean(o * o, axis=-1, keepdims=True) + RMS_EPS)
            on = o * rr
            sg = jax.nn.sigmoid(g)
            sil = g * sg
            dg_ref[:, sl] = (d * on * w * (sg * (1.0 + g * (1.0 - sg)))).astype(BF16)
            dw = dw + jnp.sum(d * on * sil, axis=0, keepdims=True)
            don = d * w * sil
            do_ref[:, sl] = rr * (don - on * jnp.mean(don * on, axis=-1, keepdims=True))
        dw_ref[...] += dw

    return pl.pallas_call(
        body, name="gla_norm_bwd", grid=(t // r,),
        in_specs=[pl.BlockSpec((r, D_GLA), lambda i: (i, 0)),
                  pl.BlockSpec((r, D_GLA), lambda i: (i, 0)),
                  pl.BlockSpec((r, D_GLA), lambda i: (i, C_GG // D_GLA)),
                  pl.BlockSpec((1, GLA_DV), lambda i: (0, 0)), *dep_specs],
        out_specs=[pl.BlockSpec((r, D_GLA), lambda i: (i, 0)),
                   pl.BlockSpec((r, D_GLA), lambda i: (i, 0)),
                   pl.BlockSpec((1, GLA_DV), lambda i: (0, 0))],
        out_shape=[jax.ShapeDtypeStruct((t, D_GLA), F32), jax.ShapeDtypeStruct((t, D_GLA), BF16),
                   jax.ShapeDtypeStruct((1, GLA_DV), F32)],
        compiler_params=_cparams(("arbitrary",)),
    )(dcat, o_raw, proj, norm_w, *deps)


def _gla_bwd(proj, bloc, do_raw, states):
    t = proj.shape[0]
    r = min(256, t)
    ncb = r // GC
    nb = t // r
    scale = GLA_DK ** -0.5

    def body(q_ref, k_ref, v_ref, b_ref, do_ref, st_ref, dq_ref, dk_ref, dv_ref, db_ref, ds_scr):
        @pl.when(pl.program_id(1) == 0)
        def _():
            ds_scr[...] = jnp.zeros_like(ds_scr)

        rows = lax.broadcasted_iota(jnp.int32, (GC, 1), 0)

        def chunk(cc, carry):
            c = ncb - 1 - cc
            r0 = pl.multiple_of(c * GC, GC)
            q = q_ref[pl.ds(r0, GC), :] * scale
            k = k_ref[pl.ds(r0, GC), :]
            v = v_ref[pl.ds(r0, GC), :]
            b = b_ref[pl.ds(r0, GC), :]
            do = do_ref[pl.ds(r0, GC), :]
            st = st_ref[c]
            dsn = ds_scr[...]
            bl = b[GC - 1:GC, :]
            eb = jnp.exp(b)
            ekl = jnp.exp(bl - b)
            ebl = jnp.exp(bl)
            qh = q * eb
            kd = k * ekl
            dob = do.astype(BF16)
            dsb = dsn.astype(BF16)
            dqh = _dot(dob, st.astype(BF16), NN)
            dkd = _dot(v.astype(BF16), dsb, NN)
            dv = _dot(kd.astype(BF16), dsb, NT)
            dq_i = jnp.zeros((GC, GLA_DK), F32)
            dk_i = jnp.zeros((GC, GLA_DK), F32)
            dk_x = jnp.zeros((GC, GLA_DK), F32)
            kr = _bf16_round(k)
            vr = _bf16_round(v)
            dor = _bf16_round(do)
            for i in range(GC):
                w = jnp.exp(jnp.where(rows <= i, b[i:i + 1, :] - b, NEG))
                qw = q[i:i + 1, :] * w
                qwr = _bf16_round(qw)
                a = _bf16_round(jnp.sum(qwr * kr, axis=-1, keepdims=True))
                da = jnp.sum(vr * dor[i:i + 1, :], axis=-1, keepdims=True)
                dv = dv + a * dor[i:i + 1, :]
                dk_x = dk_x + da * qw
                dk_i = dk_i + _bf16_round(da) * qwr
                dq_i = jnp.where(rows == i, jnp.sum(da * (w * k), axis=0, keepdims=True), dq_i)
            dqs = dqh * eb + dq_i
            dk = dkd * ekl + dk_i
            db_last = jnp.sum(dkd * kd, axis=0, keepdims=True) + ebl * jnp.sum(dsn * st, axis=0, keepdims=True)
            db = q * dqs - k * (dkd * ekl + dk_x) + jnp.where(rows == GC - 1, db_last, 0.0)
            dq_ref[pl.ds(r0, GC), :] = (dqs * scale).astype(BF16)
            dk_ref[pl.ds(r0, GC), :] = dk.astype(BF16)
            dv_ref[pl.ds(r0, GC), :] = dv.astype(BF16)
            db_ref[pl.ds(r0, GC), :] = db
            ds_scr[...] = dsn * ebl + _dot(dob, qh.astype(BF16), TN)
            return carry

        lax.fori_loop(0, ncb, chunk, 0)

    rev = lambda i: nb - 1 - i
    return pl.pallas_call(
        body, name="gla_bwd", grid=(GLA_HEADS, nb),
        in_specs=[pl.BlockSpec((r, GLA_DK), lambda h, i: (rev(i), C_QG // GLA_DK + h)),
                  pl.BlockSpec((r, GLA_DK), lambda h, i: (rev(i), C_KG // GLA_DK + h)),
                  pl.BlockSpec((r, GLA_DV), lambda h, i: (rev(i), C_VG // GLA_DV + h)),
                  pl.BlockSpec((r, GLA_DK), lambda h, i: (rev(i), h)),
                  pl.BlockSpec((r, GLA_DV), lambda h, i: (rev(i), h)),
                  pl.BlockSpec((ncb, None, GLA_DV, GLA_DK), lambda h, i: (rev(i), h, 0, 0))],
        out_specs=[pl.BlockSpec((r, GLA_DK), lambda h, i: (rev(i), h)),
                   pl.BlockSpec((r, GLA_DK), lambda h, i: (rev(i), h)),
                   pl.BlockSpec((r, GLA_DV), lambda h, i: (rev(i), h)),
                   pl.BlockSpec((r, GLA_DK), lambda h, i: (rev(i), h))],
        out_shape=[jax.ShapeDtypeStruct((t, GLA_KW), BF16), jax.ShapeDtypeStruct((t, GLA_KW), BF16),
                   jax.ShapeDtypeStruct((t, D_GLA), BF16), jax.ShapeDtypeStruct((t, GLA_KW), F32)],
        scratch_shapes=[pltpu.VMEM((GLA_DV, GLA_DK), F32)],
        compiler_params=_cparams(("parallel", "arbitrary")),
    )(proj, proj, proj, bloc, do_raw, states)


def _gate_bwd(db, dgdz, proj, w2p, triu):
    t = db.shape[0]
    r = triu.shape[0]

    def body(db_ref, s_ref, lo_ref, w_ref, u_ref, dlo_ref, gw_ref, gb_ref):
        @pl.when(pl.program_id(0) == 0)
        def _():
            gw_ref[...] = jnp.zeros_like(gw_ref)
            gb_ref[...] = jnp.zeros_like(gb_ref)

        dz = _dot(u_ref[...], db_ref[...], NN, precision=lax.Precision.HIGHEST) * s_ref[...]
        dzb = dz.astype(BF16)
        gb_ref[...] += jnp.sum(dz, axis=0, keepdims=True)
        gw_ref[...] += _dot(lo_ref[...].astype(BF16), dzb, TN)
        dlo_ref[...] = _dot(dzb, w_ref[...], NT).astype(BF16)

    return pl.pallas_call(
        body, name="gate_bwd", grid=(t // r,),
        in_specs=[pl.BlockSpec((r, GLA_KW), lambda i: (i, 0)),
                  pl.BlockSpec((r, GLA_KW), lambda i: (i, 0)),
                  pl.BlockSpec((r, LANES), lambda i: (i, C_LO // LANES)),
                  pl.BlockSpec((LANES, GLA_KW), lambda i: (0, 0)),
                  pl.BlockSpec((r, r), lambda i: (0, 0))],
        out_specs=[pl.BlockSpec((r, LANES), lambda i: (i, 0)),
                   pl.BlockSpec((LANES, GLA_KW), lambda i: (0, 0)),
                   pl.BlockSpec((1, GLA_KW), lambda i: (0, 0))],
        out_shape=[jax.ShapeDtypeStruct((t, LANES), BF16), jax.ShapeDtypeStruct((LANES, GLA_KW), F32),
                   jax.ShapeDtypeStruct((1, GLA_KW), F32)],
        compiler_params=_cparams(("arbitrary",)),
    )(db, dgdz, proj, w2p, triu)


def _swa_bwd(proj, dcat, sinks, dep=None):
    t = proj.shape[0]

    deps, dep_specs = _dep_operand(dep)

    def body(sink_ref, q_ref, kp_ref, kc_ref, vp_ref, vc_ref, d_ref, *rest):
        dq_ref, dk_ref, dv_ref, dsink_ref = rest[len(deps):]
        i = pl.program_id(0)

        @pl.when(i == 0)
        def _():
            dk_ref[...] = jnp.zeros_like(dk_ref)
            dv_ref[...] = jnp.zeros_like(dv_ref)
            dsink_ref[...] = jnp.zeros_like(dsink_ref)

        valid, lo = _swa_masks(i, SWA_GROUP)
        lane = lax.broadcasted_iota(jnp.int32, (1, LANES), 1)
        kb = jnp.concatenate([kp_ref[...], kc_ref[...]], axis=0)
        vb = jnp.concatenate([vp_ref[...], vc_ref[...]], axis=0)
        dsink = jnp.zeros((1, LANES), F32)
        folded_k, folded_v = [], []
        for kv in range(2):
            kdup = _dup_half(kb, lo, kv).astype(BF16)
            vdup = _dup_half(vb, lo, kv).astype(BF16)
            qm = _swa_stack(q_ref, 0, lo, kv)
            dom = _swa_stack(d_ref, 0, lo, kv)
            pn, psink = _swa_probs(qm, kdup, valid, _swa_sinks(sink_ref, kv))
            dpr = _dot(dom, vdup, NT)
            drow = jnp.sum(dpr * pn, axis=-1, keepdims=True)
            ds_col = psink * drow
            for h in range(SWA_GROUP):
                dsink = dsink + jnp.where(lane == SWA_GROUP * kv + h,
                                          -jnp.sum(ds_col[h * WINDOW:(h + 1) * WINDOW, :]), 0.0)
            dsb = (pn * (dpr - drow) * (SWA_DH ** -0.5)).astype(BF16)
            dq_all = _dot(dsb, kdup, NN)
            for p in range(4):
                c0 = LANES * (4 * kv + p)
                dq_ref[:, c0:c0 + LANES] = _swa_unstack(dq_all, lo, p).astype(BF16)
            dkd = _dot(dsb, qm, TN)
            dvd = _dot(pn.astype(BF16), dom, TN)
            folded_k.append(dkd + pltpu.roll(dkd, SWA_DH, axis=1))
            folded_v.append(dvd + pltpu.roll(dvd, SWA_DH, axis=1))
        dkb = jnp.where(lo, folded_k[0], folded_k[1])
        dvb = jnp.where(lo, folded_v[0], folded_v[1])
        dsink_ref[...] += dsink
        cur = pl.ds(pl.multiple_of(i * WINDOW, WINDOW), WINDOW)
        dk_ref[cur, :] += dkb[WINDOW:, :]
        dv_ref[cur, :] += dvb[WINDOW:, :]

        @pl.when(i > 0)
        def _():
            prev = pl.ds(pl.multiple_of((i - 1) * WINDOW, WINDOW), WINDOW)
            dk_ref[prev, :] += dkb[:WINDOW, :]
            dv_ref[prev, :] += dvb[:WINDOW, :]

    kvspec = lambda col, prev: pl.BlockSpec(
        (WINDOW, LANES), (lambda i: (jnp.maximum(i - 1, 0), col)) if prev else (lambda i: (i, col)))
    full = pl.BlockSpec((t, LANES), lambda i: (0, 0))
    return pl.pallas_call(
        body, name="swa_bwd", grid=(t // WINDOW,),
        in_specs=[pl.BlockSpec(memory_space=pltpu.SMEM),
                  pl.BlockSpec((WINDOW, D_SWA), lambda i: (i, C_QS // D_SWA)),
                  kvspec(C_KS // LANES, True), kvspec(C_KS // LANES, False),
                  kvspec(C_VS // LANES, True), kvspec(C_VS // LANES, False),
                  pl.BlockSpec((WINDOW, D_SWA), lambda i: (i, 1)), *dep_specs],
        out_specs=[pl.BlockSpec((WINDOW, D_SWA), lambda i: (i, 0)), full, full,
                   pl.BlockSpec((1, LANES), lambda i: (0, 0))],
        out_shape=[jax.ShapeDtypeStruct((t, D_SWA), BF16), jax.ShapeDtypeStruct((t, LANES), F32),
                   jax.ShapeDtypeStruct((t, LANES), F32), jax.ShapeDtypeStruct((1, LANES), F32)],
        compiler_params=_cparams(("arbitrary",)),
    )(sinks, proj, proj, proj, proj, proj, dcat, *deps)


def _grad_w_in(xb, dproj):
    t = xb.shape[0]
    tm, tn = 640, 1024
    return _matmul(
        "grad_w_in", dproj, xb, TN, (D_INP // tm, D_MODEL // tn, 1),
        pl.BlockSpec((t, tm), lambda i, j, k: (0, i)),
        pl.BlockSpec((t, tn), lambda i, j, k: (0, j)),
        jax.ShapeDtypeStruct((D_INP, D_MODEL), F32),
        pl.BlockSpec((tm, tn), lambda i, j, k: (i, j)), (tm, tn))


def _grad_x(dproj, w_in_t, dr1, dep=None):
    t = dproj.shape[0]
    tm, tn = min(512, t), 1024

    def epilogue(acc_ref, extra, outs):
        for r0, n in _row_chunks(tm):
            rs = slice(r0, r0 + n)
            outs[0][rs, :] = ALPHA * extra[0][rs, :] + acc_ref[rs, :]

    blk = pl.BlockSpec((tm, tn), lambda j, i, k: (i, j))
    return _matmul(
        "grad_x", dproj, w_in_t, NN, (D_MODEL // tn, t // tm, 1),
        pl.BlockSpec((tm, D_INP), lambda j, i, k: (i, 0)),
        pl.BlockSpec((D_INP, tn), lambda j, i, k: (0, j)),
        jax.ShapeDtypeStruct((t, D_MODEL), F32), blk, (tm, tn),
        extra=(dr1,), extra_specs=(blk,), epilogue=epilogue, dep=dep)


def _place():
    x, y, c = lax.axis_index("x"), lax.axis_index("y"), lax.axis_index("c")
    chips = [(1 - x, y), (x, 1 - y), (1 - x, 1 - y)]
    return x, y, c, chips


def _all_gather(shards):
    n = len(shards)
    hbm = pl.BlockSpec(memory_space=pl.ANY)

    def body(*refs):
        ins, outs = refs[:n], refs[n:2 * n]
        send, recv, loc = refs[2 * n:]
        x, y, c, chips = _place()
        me, sib = (x, y, c), (x, y, 1 - c)

        def slot(t, px, py, pc):
            return outs[t].at[4 * px + 2 * py + pc]

        def copy(t, k, block, to, src=None):
            return pltpu.make_async_remote_copy(
                src_ref=slot(t, *block) if src is None else src, dst_ref=slot(t, *block),
                send_sem=send.at[7 * t + k], recv_sem=recv.at[7 * t + k], device_id=to, device_id_type=MESH)

        mine = [pltpu.make_async_copy(ins[t], slot(t, *me), loc.at[t]) for t in range(n)]
        for cp in mine:
            cp.start()
        sent = []
        for t in range(n):
            sent.append(copy(t, 0, me, sib, src=ins[t]))
            sent += [copy(t, 1 + j, me, (*chip, c), src=ins[t]) for j, chip in enumerate(chips)]
        for cp in sent:
            cp.start()
        for t in range(n):
            for j, chip in enumerate(chips):
                copy(t, 1 + j, (*chip, c), me).wait_recv()
                fwd = copy(t, 4 + j, (*chip, c), sib)
                fwd.start()
                sent.append(fwd)
        for t in range(n):
            copy(t, 0, sib, me).wait_recv()
            for j, chip in enumerate(chips):
                copy(t, 4 + j, (*chip, 1 - c), me).wait_recv()
        for cp in sent:
            cp.wait_send()
        for cp in mine:
            cp.wait()

    return pl.pallas_call(
        body, name="all_gather_weights",
        in_specs=[hbm] * n, out_specs=[hbm] * n,
        out_shape=[jax.ShapeDtypeStruct((N_DEV,) + s.shape, s.dtype) for s in shards],
        scratch_shapes=[pltpu.SemaphoreType.DMA((7 * n,)), pltpu.SemaphoreType.DMA((7 * n,)),
                        pltpu.SemaphoreType.DMA((n,))],
    )(*shards)


def _plan_gather_out(src, land, x, y, c, chips):
    me = 4 * x + 2 * y + c
    return [(src, land.at[me], to) for to in [(x, y, 1 - c)] + [(px, py, c) for px, py in chips]]


def _plan_gather_forward(src, land, x, y, c, chips):
    return [(land.at[4 * px + 2 * py + c], land.at[4 * px + 2 * py + c], (x, y, 1 - c)) for px, py in chips]


def _plan_sibling(src, land, x, y, c, chips):
    return [(src.at[2 * q + (1 - c)], land.at[q], (x, y, 1 - c)) for q in range(4)]


def _plan_chips(src, land, x, y, c, chips):
    return [(src.at[j], land.at[j], (px, py, c)) for j, (px, py) in enumerate(chips)]


_PLAN_COPIES = {_plan_gather_out: 4, _plan_gather_forward: 3, _plan_sibling: 4, _plan_chips: 3}
_HBM = pl.BlockSpec(memory_space=pltpu.HBM)
_SEM = pl.BlockSpec(memory_space=pltpu.SEMAPHORE)
_EFFECT = pltpu.SideEffectType.DATAFLOW_SIDE_EFFECTING


def _hbm(a):
    return pltpu.with_memory_space_constraint(a, pltpu.HBM)


def _plan_descriptors(plans, srcs, lands, send, recv):
    x, y, c, chips = _place()
    cps = []
    for plan, src, land in zip(plans, srcs, lands):
        for s_ref, d_ref, to in plan(src, land, x, y, c, chips):
            k = len(cps)
            cps.append(pltpu.make_async_remote_copy(src_ref=s_ref, dst_ref=d_ref, send_sem=send.at[k],
                                                    recv_sem=recv.at[k], device_id=to, device_id_type=MESH))
    return cps


def _copies_start(name, plans, srcs, lands, after=None):
    has_src = [s is not None for s in srcs]
    arrays = [s for s in srcs if s is not None] + list(lands)
    n_src = sum(has_src)
    n_cp = sum(_PLAN_COPIES[p] for p in plans)
    afters = [] if after is None else [after]

    def body(*refs):
        ins = refs[:len(arrays)]
        send, recv = refs[len(arrays) + len(afters)], refs[len(arrays) + len(afters) + 1]
        token = refs[-1]
        it = iter(ins[:n_src])
        src_refs = [next(it) if h else None for h in has_src]
        for cp in _plan_descriptors(plans, src_refs, ins[n_src:], send, recv):
            cp.start()
        token[...] = jnp.zeros_like(token)

    outs = pl.pallas_call(
        body, name=name,
        in_specs=[_HBM] * len(arrays) + [pl.BlockSpec(memory_space=pl.ANY)] * len(afters),
        out_specs=(_SEM, _SEM, *[_HBM] * len(arrays), pl.BlockSpec(memory_space=pltpu.VMEM)),
        out_shape=(pltpu.SemaphoreType.DMA((n_cp,)), pltpu.SemaphoreType.DMA((n_cp,)),
                   *[pltpu.HBM(a.shape, a.dtype) for a in arrays], jax.ShapeDtypeStruct((8, LANES), F32)),
        input_output_aliases={i: 2 + i for i in range(len(arrays))},
        compiler_params=pltpu.CompilerParams(has_side_effects=_EFFECT),
    )(*[_hbm(a) for a in arrays], *afters)
    send, recv = outs[0], outs[1]
    thru = list(outs[2:-1])
    it = iter(thru[:n_src])
    return send, recv, [next(it) if h else None for h in has_src], thru[n_src:], outs[-1]


def _copies_wait(name, plans, started, after):
    send, recv, srcs, lands, _ = started
    has_src = [s is not None for s in srcs]
    arrays = [s for s in srcs if s is not None] + list(lands)
    n_src = sum(has_src)

    def body(*refs):
        ins = refs[:len(arrays)]
        send_ref, recv_ref = refs[len(arrays)], refs[len(arrays) + 1]
        it = iter(ins[:n_src])
        src_refs = [next(it) if h else None for h in has_src]
        for cp in _plan_descriptors(plans, src_refs, ins[n_src:], send_ref, recv_ref):
            cp.wait_send()
            cp.wait_recv()

    outs = pl.pallas_call(
        body, name=name,
        in_specs=[_HBM] * len(arrays) + [_SEM, _SEM, pl.BlockSpec(memory_space=pl.ANY)],
        out_specs=tuple([_HBM] * len(arrays)),
        out_shape=tuple(pltpu.HBM(a.shape, a.dtype) for a in arrays),
        input_output_aliases={i: i for i in range(len(arrays))},
        compiler_params=pltpu.CompilerParams(has_side_effects=_EFFECT),
    )(*arrays, send, recv, after)
    return list(outs[:n_src]), list(outs[n_src:])


def _shard_tiles(rows, cols, tr):
    if rows % tr == 0:
        return (tr, cols), rows // tr, lambda r: (r, 0)
    tc = 2 * LANES
    return (rows, tc), cols // tc, lambda r: (0, r)


def _pair_sum(name, grad, from_sibling, blocks):
    _, rows, cols = grad.shape
    (br_, bc), steps, at = _shard_tiles(rows, cols, 256)

    def body(blk_ref, g_ref, s_ref, o_ref):
        o_ref[...] = (g_ref[...] + s_ref[...]).astype(BF16)

    return pl.pallas_call(
        body, name=name,
        grid_spec=pltpu.PrefetchScalarGridSpec(
            num_scalar_prefetch=1, grid=(3, steps),
            in_specs=[pl.BlockSpec((None, br_, bc), lambda j, r, br: (br[j], *at(r))),
                      pl.BlockSpec((None, br_, bc), lambda j, r, br: (br[3 + j], *at(r)))],
            out_specs=pl.BlockSpec((None, br_, bc), lambda j, r, br: (j, *at(r)))),
        out_shape=jax.ShapeDtypeStruct((3, rows, cols), BF16),
        compiler_params=_cparams(("parallel", "parallel")),
    )(blocks, grad, from_sibling)


def _adam_math(g, w, m, v):
    m2 = ADAM_B1 * m + (1.0 - ADAM_B1) * g
    v2 = ADAM_B2 * v + (1.0 - ADAM_B2) * (g * g)
    m_hat = m2 / (1.0 - ADAM_B1 ** ADAM_STEP)
    v_hat = v2 / (1.0 - ADAM_B2 ** ADAM_STEP)
    delta = -ADAM_LR * (m_hat / (jnp.sqrt(v_hat) + ADAM_EPS) + ADAM_WD * w)
    return delta, m2, v2


def _sum_adam(name, grad, from_sibling, from_chips, own, w, m, v, dep=None):
    rows, cols = w.shape
    (br_, bc), steps, at = _shard_tiles(rows, cols, 128)

    deps, dep_specs = _dep_operand(dep)

    def body(own_ref, p_ref, s_ref, r_ref, w_ref, m_ref, v_ref, *rest):
        g_out, d_out, m_out, v_out = rest[len(deps):]
        g = p_ref[...] + s_ref[...]
        for j in range(3):
            g = g + r_ref[j].astype(F32)
        d, m2, v2 = _adam_math(g, w_ref[...], m_ref[...], v_ref[...])
        g_out[...] = g
        d_out[...] = d
        m_out[...] = m2
        v_out[...] = v2

    blk = pl.BlockSpec((br_, bc), lambda r, cr: at(r))
    shp = jax.ShapeDtypeStruct((rows, cols), F32)
    return pl.pallas_call(
        body, name=name,
        grid_spec=pltpu.PrefetchScalarGridSpec(
            num_scalar_prefetch=1, grid=(steps,),
            in_specs=[pl.BlockSpec((None, br_, bc), lambda r, cr: (cr[0], *at(r))),
                      pl.BlockSpec((None, br_, bc), lambda r, cr: (cr[1], *at(r))),
                      pl.BlockSpec((3, br_, bc), lambda r, cr: (0, *at(r))),
                      blk, blk, blk, *dep_specs],
            out_specs=[blk, blk, blk, blk]),
        out_shape=[shp, shp, shp, shp],
        compiler_params=_cparams(("parallel",)),
    )(own, grad, from_sibling, from_chips, w, m, v, *deps)


def _adam_small(name, g, w, m, v):
    def body(g_ref, w_ref, m_ref, v_ref, d_out, m_out, v_out):
        d, m2, v2 = _adam_math(g_ref[...], w_ref[...], m_ref[...], v_ref[...])
        d_out[...] = d
        m_out[...] = m2
        v_out[...] = v2

    shp = jax.ShapeDtypeStruct(w.shape, F32)
    return pl.pallas_call(body, name=name, out_shape=[shp, shp, shp])(g, w, m, v)


def _all_reduce_small(pack):
    rows = pack.shape[0]

    def body(in_ref, out_ref, slots, send, recv):
        x, y, c, _ = _place()
        me = 4 * x + 2 * y + c
        slots[me] = in_ref[...]
        cps = []
        for k in range(1, N_DEV):
            dx, dy, dc = (k >> 2) & 1, (k >> 1) & 1, k & 1
            to = (jnp.bitwise_xor(x, dx), jnp.bitwise_xor(y, dy), jnp.bitwise_xor(c, dc))
            cps.append(pltpu.make_async_remote_copy(
                src_ref=in_ref, dst_ref=slots.at[me], send_sem=send.at[k - 1], recv_sem=recv.at[k - 1],
                device_id=to, device_id_type=MESH))
        for cp in cps:
            cp.start()
        for cp in cps:
            cp.wait()
        acc = slots[0]
        for d in range(1, N_DEV):
            acc = acc + slots[d]
        out_ref[...] = acc

    return pl.pallas_call(
        body, name="all_reduce_small",
        in_specs=[pl.BlockSpec(memory_space=pltpu.VMEM)],
        out_specs=pl.BlockSpec(memory_space=pltpu.VMEM),
        out_shape=jax.ShapeDtypeStruct((rows, LANES), F32),
        scratch_shapes=[pltpu.VMEM((N_DEV, rows, LANES), F32),
                        pltpu.SemaphoreType.DMA((N_DEV - 1,)), pltpu.SemaphoreType.DMA((N_DEV - 1,))],
    )(pack)


def _rows128(a):
    flat = a.reshape(-1)
    padn = (-flat.shape[0]) % (8 * LANES)
    if padn:
        flat = jnp.concatenate([flat, jnp.zeros((padn,), flat.dtype)])
    return flat.reshape(-1, LANES)


def kernel(x, w_in, w_gk2, b_gk, gla_norm_w, swa_sinks, w_out, ln1_g, ln1_b, w_up, w_down, ln2_g, ln2_b, loss_target, m_w_in, m_w_gk2, m_b_gk, m_gla_norm_w, m_swa_sinks, m_w_out, m_ln1_g, m_ln1_b, m_w_up, m_w_down, m_ln2_g, m_ln2_b, v_w_in, v_w_gk2, v_b_gk, v_gla_norm_w, v_swa_sinks, v_w_out, v_ln1_g, v_ln1_b, v_w_up, v_w_down, v_ln2_g, v_ln2_b):
    xc, yc, cc = lax.axis_index("x"), lax.axis_index("y"), lax.axis_index("c")
    me = 4 * xc + 2 * yc + cc

    x2 = x[0]
    t = x2.shape[0]
    target = loss_target[0]

    def land_of(sh):
        return lax.dynamic_update_index_in_dim(lax.empty((N_DEV,) + sh.shape, BF16), sh, me, 0)

    out_plans = [_plan_gather_out] * 3
    fwd_plans = [_plan_gather_forward] * 3
    in_shards = [w_in[0].T.astype(BF16), w_gk2[0].astype(BF16)]
    ag0 = _copies_start("gather_in_start", out_plans[:2], in_shards, [land_of(sh) for sh in in_shards])
    zero = ag0[4][0, 0]
    xb = (x2 + zero).astype(BF16)
    shards = [(w[0] + zero).astype(BF16) for w in (w_out, w_up, w_down)]
    lands = [land_of(sh) for sh in shards]
    _, in_lands = _copies_wait("gather_in_wait", out_plans[:2], ag0, lands[2])
    ag0f = _copies_start("gather_in_forward_start", fwd_plans[:2], [None] * 2, in_lands)
    _, first = _copies_wait("gather_in_forward_wait", fwd_plans[:2], ag0f, ag0f[4])
    w_in_t = _to_padded_rows(first[0].reshape(D_IN, D_MODEL))
    w2 = first[1].transpose(1, 0, 2).reshape(GATE_RANK, GLA_KW)
    w2p = jnp.concatenate([w2, jnp.zeros((LANES - GATE_RANK, GLA_KW), BF16)], axis=0)
    ag1 = _copies_start("gather_out_start", out_plans, shards, lands, first[0])

    proj = _proj(xb, w_in_t, dep=ag1[4])
    rt = min(256, t)
    ii = jnp.arange(rt)
    tri = ((ii[:, None] // GC == ii[None, :] // GC) & (ii[None, :] <= ii[:, None])).astype(F32)
    bloc, dgdz = _gate_fwd(proj, w2p, b_gk, tri)
    o_raw, states = _gla_fwd(proj, bloc)
    _, lands = _copies_wait("gather_out_wait", out_plans, ag1, o_raw)
    ag2 = _copies_start("gather_forward_start", fwd_plans, [None] * 3, lands, o_raw)
    gla_out = _gla_norm_fwd(o_raw, proj, gla_norm_w)
    cat = _swa_fwd(proj, swa_sinks[0], gla_out, dep=ag2[4])
    _, gathered = _copies_wait("gather_forward_wait", fwd_plans, ag2, cat)
    w_out_f = gathered[0].reshape(D_MODEL, D_MODEL)
    w_up_f = gathered[1]
    w_down_f = gathered[2].reshape(D_FF, D_MODEL)
    r1, h1, h1b = _mix_ln1(cat, w_out_f, x2, ln1_g, ln1_b)
    a_act, hdn = _mlp_up(h1b, w_up_f)
    dr2, dr2b, g_ln2_g, g_ln2_b, loss_part = _mlp_down_loss(hdn, w_down_f, h1, target, ln2_g, ln2_b)

    others = [2 * (1 - xc) + yc, 2 * xc + (1 - yc), 2 * (1 - xc) + (1 - yc)]
    blocks = jnp.stack([2 * q + cc for q in others] + others).astype(jnp.int32)
    own = jnp.stack([me, 2 * xc + yc]).astype(jnp.int32)
    wmv = dict(w_in=[a[0].T for a in (w_in, m_w_in, v_w_in)], w_out=[a[0] for a in (w_out, m_w_out, v_w_out)],
               w_up=[a[0] for a in (w_up, m_w_up, v_w_up)], w_down=[a[0] for a in (w_down, m_w_down, v_w_down)])
    big = {}

    def sib_land(g):
        return lax.empty((4,) + g.shape[1:], F32)

    def chip_land(g):
        return lax.empty((3,) + g.shape[1:], BF16)

    def finish(nm, g, from_sib, from_chips, dep=None):
        outs = _sum_adam("sum_adam_" + nm, g, from_sib, from_chips, own, *wmv[nm], dep=dep)
        big[nm] = [(o.T if nm == "w_in" else o)[None] for o in outs]

    du = _mlp_down_bwd(dr2b, w_down_f, a_act)
    g_down = _grad_w_down(hdn, dr2b)
    sa_down = _copies_start("sibling_down_start", [_plan_sibling], [g_down], [sib_land(g_down)])
    g_up = _grad_w_up(h1b, du, dep=sa_down[4])
    (g_down,), (fs_down,) = _copies_wait("sibling_down_wait", [_plan_sibling], sa_down, g_up)
    p_down = _pair_sum("pair_sum_w_down", g_down, fs_down, blocks)
    sb_down = _copies_start("chips_down_sibling_up_start", [_plan_chips, _plan_sibling], [p_down, g_up],
                            [chip_land(g_down), sib_land(g_up)])
    dr1, dr1b, g_ln1_g, g_ln1_b = _mlp_up_bwd_ln1(du, w_up_f, dr2, r1, ln1_g, dep=sb_down[4])
    dcat = _dcat(dr1b, w_out_f)
    (_, g_up), (fc_down, fs_up) = _copies_wait("chips_down_sibling_up_wait", [_plan_chips, _plan_sibling], sb_down,
                                               dcat)
    finish("w_down", g_down, fs_down, fc_down)
    p_up = _pair_sum("pair_sum_w_up", g_up, fs_up, blocks)
    sb_up = _copies_start("chips_up_start", [_plan_chips], [p_up], [chip_land(g_up)])
    do_raw, dg_g, g_norm_w = _gla_norm_bwd(dcat, o_raw, proj, gla_norm_w, dep=sb_up[4])
    dq_g, dk_g, dv_g, dgk = _gla_bwd(proj, bloc, do_raw, states)
    _, (fc_up,) = _copies_wait("chips_up_wait", [_plan_chips], sb_up, dgk)
    finish("w_up", g_up, fs_up, fc_up)
    dlo, gw2_p, g_b_gk = _gate_bwd(dgk, dgdz, proj, w2p, tri.T)
    dq_s, dk_s, dv_s, g_sinks = _swa_bwd(proj, dcat, swa_sinks[0])
    dproj = jnp.concatenate([dq_g, dk_g, dv_g, dg_g, dq_s, dk_s.astype(BF16), dv_s.astype(BF16), dlo], axis=-1)
    gw_in_t = _grad_w_in(xb, dproj)
    g_in = _from_padded_rows(gw_in_t).reshape(N_DEV, D_IN // N_DEV, D_MODEL)
    sa_in = _copies_start("sibling_in_start", [_plan_sibling], [g_in], [sib_land(g_in)])
    g_out = _grad_w_out(cat, dr1b, dep=sa_in[4])
    (g_in,), (fs_in,) = _copies_wait("sibling_in_wait", [_plan_sibling], sa_in, g_out)
    p_in = _pair_sum("pair_sum_w_in", g_in, fs_in, blocks)
    sb_in = _copies_start("chips_in_sibling_out_start", [_plan_chips, _plan_sibling], [p_in, g_out],
                          [chip_land(g_in), sib_land(g_out)])
    grad_x = _grad_x(dproj, w_in_t, dr1, dep=sb_in[4])
    (_, g_out), (fc_in, fs_out) = _copies_wait("chips_in_sibling_out_wait", [_plan_chips, _plan_sibling], sb_in,
                                               grad_x)
    p_out = _pair_sum("pair_sum_w_out", g_out, fs_out, blocks)
    sb_out = _copies_start("chips_out_start", [_plan_chips], [p_out], [chip_land(g_out)])
    finish("w_in", g_in, fs_in, fc_in, dep=sb_out[4])

    pieces = [loss_part, g_b_gk, g_norm_w, g_sinks[:, :SWA_HEADS], g_ln1_g, g_ln1_b, g_ln2_g, g_ln2_b,
              gw2_p[:GATE_RANK]]
    pack = jnp.concatenate([_rows128(p) for p in pieces], axis=0)
    tot = _all_reduce_small(pack + sb_out[4][:1, :1])
    sizes = [p.size for p in pieces]
    offs = [0]
    for p in pieces:
        offs.append(offs[-1] + _rows128(p).shape[0])
    unpack = lambda i, shape: tot[offs[i]:offs[i + 1]].reshape(-1)[:sizes[i]].reshape(shape)
    loss = tot[0, 0]
    small_names = ["b_gk", "gla_norm_w", "swa_sinks", "ln1_g", "ln1_b", "ln2_g", "ln2_b"]
    small_w = dict(b_gk=(b_gk, m_b_gk, v_b_gk), gla_norm_w=(gla_norm_w, m_gla_norm_w, v_gla_norm_w),
                   swa_sinks=(swa_sinks, m_swa_sinks, v_swa_sinks), ln1_g=(ln1_g, m_ln1_g, v_ln1_g),
                   ln1_b=(ln1_b, m_ln1_b, v_ln1_b), ln2_g=(ln2_g, m_ln2_g, v_ln2_g),
                   ln2_b=(ln2_b, m_ln2_b, v_ln2_b))
    small_g = {nm: unpack(1 + i, small_w[nm][0].shape) for i, nm in enumerate(small_names)}
    g_pack = jnp.concatenate([_rows128(small_g[nm]) for nm in small_names], axis=0)
    small_wmv = [jnp.concatenate([_rows128(small_w[nm][k]) for nm in small_names], axis=0) for k in range(3)]
    small_out = _adam_small("adam_replicated", g_pack, *small_wmv)
    srow = [0]
    for nm in small_names:
        srow.append(srow[-1] + _rows128(small_w[nm][0]).shape[0])
    small = {}
    for i, nm in enumerate(small_names):
        shape = small_w[nm][0].shape
        n = small_w[nm][0].size
        small[nm] = [small_g[nm]] + [o[srow[i]:srow[i + 1]].reshape(-1)[:n].reshape(shape) for o in small_out]

    gw2_full = unpack(8, (GATE_RANK, GLA_KW))
    gw2_loc = lax.dynamic_slice_in_dim(gw2_full, me * (GLA_KW // N_DEV), GLA_KW // N_DEV, axis=1)
    gk2_out = _adam_small("adam_w_gk2", gw2_loc, w_gk2[0], m_w_gk2[0], v_w_gk2[0])
    big["w_gk2"] = [gw2_loc[None]] + [o[None] for o in gk2_out]

    _, (fc_out,) = _copies_wait("chips_out_wait", [_plan_chips], sb_out, tot)
    finish("w_out", g_out, fs_out, fc_out)

    order = ["w_in", "w_gk2", "b_gk", "gla_norm_w", "swa_sinks", "w_out", "ln1_g", "ln1_b", "w_up", "w_down",
             "ln2_g", "ln2_b"]
    res = {**big, **small}
    outs = [loss, grad_x[None]]
    for k in range(4):
        outs += [res[nm][k] for nm in order]
    return tuple(outs)
```

```python
import functools

import jax
import jax.numpy as jnp
from jax import lax
from jax.experimental import pallas as pl
from jax.experimental.pallas import tpu as pltpu

F32 = jnp.float32
BF16 = jnp.bfloat16

N_DEV = 8
D_MODEL = 2048
D_FF = 8192
GLA_HEADS = 4
GLA_DK = 128
GLA_DV = 256
GLA_KW = 512
D_GLA = 1024
GATE_RANK = 16
GATE_TAU = 16.0
SWA_HEADS = 16
SWA_DH = 64
SWA_GROUP = 8
WINDOW = 128
D_SWA = 1024
D_IN = 4368
ALPHA = 2.0 ** 0.25
LN_EPS = 1e-5
RMS_EPS = 1e-5
ADAM_LR = 0.001
ADAM_B1 = 0.9
ADAM_B2 = 0.999
ADAM_EPS = 1e-08
ADAM_WD = 0.01
ADAM_STEP = 10

C_QG, C_KG, C_VG, C_GG, C_QS, C_KS, C_VS, C_LO = 0, 512, 1024, 2048, 3072, 4096, 4224, 4352
D_INP = 4480
LANES = 128
GC = 16
NEG = -1e30

NN = ((1,), (0,))
NT = ((1,), (1,))
TN = ((0,), (0,))

VMEM_LIMIT = 52 * 1024 * 1024
MESH = pl.DeviceIdType.MESH


def _dot(a, b, dn, precision=None):
    return lax.dot_general(a, b, (dn, ((), ())), preferred_element_type=F32, precision=precision)


def _bf16_round(v):
    return v.astype(BF16).astype(F32)


def _cparams(dims):
    return pltpu.CompilerParams(dimension_semantics=dims, vmem_limit_bytes=VMEM_LIMIT)


def _dep_operand(dep):
    if dep is None:
        return (), ()
    return (dep,), (pl.BlockSpec(dep.shape, lambda *_: (0,) * dep.ndim),)


def _to_padded_rows(w):
    pad = jnp.zeros((D_INP - D_IN,) + w.shape[1:], w.dtype)
    return jnp.concatenate([w[:3072], w[3088:], w[3072:3088], pad], axis=0)


def _from_padded_rows(g):
    return jnp.concatenate([g[:3072], g[C_LO:C_LO + GATE_RANK], g[3072:C_LO]], axis=0)


def _matmul(name, a, b, dn, grid, a_spec, b_spec, out_shape, out_specs, acc_shape, *,
            extra=(), extra_specs=(), epilogue=None, dims=("parallel", "parallel", "arbitrary"), dep=None):
    nk = grid[2]
    n_extra = len(extra)
    deps, dep_specs = _dep_operand(dep)
    direct = epilogue is None and (nk == 1 or (not isinstance(out_shape, (list, tuple)) and out_shape.dtype == F32))

    scratch = [] if direct or nk == 1 else [pltpu.VMEM(acc_shape, F32)]

    def body(a_ref, b_ref, *rest):
        extra_refs = rest[:n_extra]
        out_refs = rest[n_extra + len(deps):len(rest) - len(scratch)]
        acc_ref = rest[-1] if scratch else out_refs[0]
        part = _dot(a_ref[...].astype(BF16), b_ref[...].astype(BF16), dn)

        def finish():
            if epilogue is None:
                out_refs[0][...] = acc_ref[...].astype(out_refs[0].dtype)
            else:
                epilogue(acc_ref, extra_refs, out_refs)

        if direct and nk == 1:
            out_refs[0][...] = part.reshape(out_refs[0].shape).astype(out_refs[0].dtype)
        elif nk == 1:
            epilogue(part, extra_refs, out_refs)
        else:
            k = pl.program_id(2)

            @pl.when(k == 0)
            def _():
                acc_ref[...] = part

            @pl.when(k > 0)
            def _():
                acc_ref[...] += part

            if not direct:
                @pl.when(k == nk - 1)
                def _():
                    finish()

    return pl.pallas_call(
        body, name=name, grid=grid,
        in_specs=[a_spec, b_spec, *extra_specs, *dep_specs],
        out_specs=out_specs, out_shape=out_shape,
        scratch_shapes=scratch,
        compiler_params=_cparams(dims),
    )(a, b, *extra, *deps)


def _row_chunks(rows, step=128):
    step = min(step, rows)
    return [(r, step) for r in range(0, rows, step)]


def _ln_stats(r):
    mu = jnp.mean(r, axis=-1, keepdims=True)
    xc = r - mu
    var = jnp.mean(xc * xc, axis=-1, keepdims=True)
    rstd = lax.rsqrt(var + LN_EPS)
    return xc * rstd, rstd


def _ln_bwd(dy_g, xhat, rstd):
    m1 = jnp.mean(dy_g, axis=-1, keepdims=True)
    m2 = jnp.mean(dy_g * xhat, axis=-1, keepdims=True)
    return rstd * (dy_g - m1 - xhat * m2)


def _proj(xb, w_in_t, dep=None):
    t = xb.shape[0]
    tm, tn = min(1024, t), 640
    return _matmul(
        "proj", xb, w_in_t, NT, (t // tm, D_INP // tn, 1),
        pl.BlockSpec((tm, D_MODEL), lambda i, j, k: (i, 0)),
        pl.BlockSpec((tn, D_MODEL), lambda i, j, k: (j, 0)),
        jax.ShapeDtypeStruct((t, D_INP), F32),
        pl.BlockSpec((tm, tn), lambda i, j, k: (i, j)),
        (tm, tn), dep=dep)


def _gate_fwd(proj, w2p, b_gk, tri):
    t = proj.shape[0]
    r = tri.shape[0]

    def body(lo_ref, w_ref, b_ref, tri_ref, bloc_ref, dgdz_ref):
        z = _dot(lo_ref[...].astype(BF16), w_ref[...], NN) + b_ref[...]
        e = jnp.exp(-jnp.abs(z))
        gk = (jnp.minimum(z, 0.0) - jnp.log1p(e)) * (1.0 / GATE_TAU)
        inv = 1.0 / (1.0 + e)
        dgdz_ref[...] = jnp.where(z >= 0.0, e * inv, inv) * (1.0 / GATE_TAU)
        bloc_ref[...] = _dot(tri_ref[...], gk, NN, precision=lax.Precision.HIGHEST)

    return pl.pallas_call(
        body, name="gate_fwd", grid=(t // r,),
        in_specs=[pl.BlockSpec((r, LANES), lambda i: (i, C_LO // LANES)),
                  pl.BlockSpec((LANES, GLA_KW), lambda i: (0, 0)),
                  pl.BlockSpec((1, GLA_KW), lambda i: (0, 0)),
                  pl.BlockSpec((r, r), lambda i: (0, 0))],
        out_specs=[pl.BlockSpec((r, GLA_KW), lambda i: (i, 0)),
                   pl.BlockSpec((r, GLA_KW), lambda i: (i, 0))],
        out_shape=[jax.ShapeDtypeStruct((t, GLA_KW), F32), jax.ShapeDtypeStruct((t, GLA_KW), F32)],
        compiler_params=_cparams(("parallel",)),
    )(proj, w2p, b_gk, tri)


def _gla_fwd(proj, bloc):
    t = proj.shape[0]
    r = min(256, t)
    ncb = r // GC
    scale = GLA_DK ** -0.5

    def body(q_ref, k_ref, v_ref, b_ref, o_ref, st_ref, s_scr, m_scr):
        @pl.when(pl.program_id(1) == 0)
        def _():
            s_scr[...] = jnp.zeros_like(s_scr)

        rows = lax.broadcasted_iota(jnp.int32, (GC, 1), 0)
        cols = lax.broadcasted_iota(jnp.int32, (1, GC), 1)

        def increment(c, carry):
            r0 = pl.multiple_of(c * GC, GC)
            k = k_ref[pl.ds(r0, GC), :]
            b = b_ref[pl.ds(r0, GC), :]
            kd = k * jnp.exp(b[GC - 1:GC, :] - b)
            m_scr[c] = _dot(v_ref[pl.ds(r0, GC), :].astype(BF16), kd.astype(BF16), TN)
            return carry

        lax.fori_loop(0, ncb, increment, 0, unroll=8)

        def recur(c, st):
            st_ref[c] = st
            r0 = pl.multiple_of(c * GC, GC)
            bl = b_ref[pl.ds(r0, GC), :][GC - 1:GC, :]
            return st * jnp.exp(bl) + m_scr[c]

        s_scr[...] = lax.fori_loop(0, ncb, recur, s_scr[...])

        def output(c, carry):
            r0 = pl.multiple_of(c * GC, GC)
            q = q_ref[pl.ds(r0, GC), :] * scale
            b = b_ref[pl.ds(r0, GC), :]
            kr = _bf16_round(k_ref[pl.ds(r0, GC), :])
            att = jnp.zeros((GC, GC), F32)
            for j in range(GC):
                w = jnp.exp(jnp.where(rows >= j, b - b[j:j + 1, :], NEG))
                a = jnp.sum(_bf16_round(q * w) * kr[j:j + 1, :], axis=-1, keepdims=True)
                att = jnp.where(cols == j, a, att)
            o_ref[pl.ds(r0, GC), :] = (_dot((q * jnp.exp(b)).astype(BF16), st_ref[c].astype(BF16), NT)
                                       + _dot(att.astype(BF16), v_ref[pl.ds(r0, GC), :].astype(BF16), NN))
            return carry

        lax.fori_loop(0, ncb, output, 0, unroll=8)

    return pl.pallas_call(
        body, name="gla_fwd", grid=(GLA_HEADS, t // r),
        in_specs=[pl.BlockSpec((r, GLA_DK), lambda h, i: (i, C_QG // GLA_DK + h)),
                  pl.BlockSpec((r, GLA_DK), lambda h, i: (i, C_KG // GLA_DK + h)),
                  pl.BlockSpec((r, GLA_DV), lambda h, i: (i, C_VG // GLA_DV + h)),
                  pl.BlockSpec((r, GLA_DK), lambda h, i: (i, h))],
        out_specs=[pl.BlockSpec((r, GLA_DV), lambda h, i: (i, h)),
                   pl.BlockSpec((ncb, None, GLA_DV, GLA_DK), lambda h, i: (i, h, 0, 0))],
        out_shape=[jax.ShapeDtypeStruct((t, D_GLA), F32),
                   jax.ShapeDtypeStruct((t // GC, GLA_HEADS, GLA_DV, GLA_DK), F32)],
        scratch_shapes=[pltpu.VMEM((GLA_DV, GLA_DK), F32), pltpu.VMEM((ncb, GLA_DV, GLA_DK), F32)],
        compiler_params=_cparams(("parallel", "arbitrary")),
    )(proj, proj, proj, bloc)


def _gla_norm_fwd(o_raw, proj, norm_w):
    t = o_raw.shape[0]
    r = min(512, t)

    def body(o_ref, g_ref, w_ref, out_ref):
        w = w_ref[...]
        for h in range(GLA_HEADS):
            sl = slice(h * GLA_DV, (h + 1) * GLA_DV)
            o = o_ref[:, sl]
            g = g_ref[:, sl]
            on = o * lax.rsqrt(jnp.mean(o * o, axis=-1, keepdims=True) + RMS_EPS)
            out_ref[:, sl] = (on * w * (g * jax.nn.sigmoid(g))).astype(BF16)

    return pl.pallas_call(
        body, name="gla_norm_fwd", grid=(t // r,),
        in_specs=[pl.BlockSpec((r, D_GLA), lambda i: (i, 0)),
                  pl.BlockSpec((r, D_GLA), lambda i: (i, C_GG // D_GLA)),
                  pl.BlockSpec((1, GLA_DV), lambda i: (0, 0))],
        out_specs=pl.BlockSpec((r, D_GLA), lambda i: (i, 0)),
        out_shape=jax.ShapeDtypeStruct((t, D_GLA), BF16),
        compiler_params=_cparams(("parallel",)),
    )(o_raw, proj, norm_w)


def _swa_masks(i, heads):
    qi = lax.broadcasted_iota(jnp.int32, (heads * WINDOW, 1), 0) & (WINDOW - 1)
    kj = lax.broadcasted_iota(jnp.int32, (1, 2 * WINDOW), 1)
    valid = (kj > qi) & (kj <= qi + WINDOW) & ((i > 0) | (kj >= WINDOW))
    lo = lax.broadcasted_iota(jnp.int32, (1, LANES), 1) < SWA_DH
    return valid, lo


def _dup_half(x, lo, kv):
    xr = pltpu.roll(x, SWA_DH, axis=1)
    return jnp.where(lo, x, xr) if kv == 0 else jnp.where(lo, xr, x)


def _swa_stack(ref, col0, lo, kv):
    parts = []
    for p in range(4):
        c0 = col0 + LANES * (4 * kv + p)
        xp = ref[:, c0:c0 + LANES]
        parts += [jnp.where(lo, xp, 0.0).astype(BF16), jnp.where(lo, 0.0, xp).astype(BF16)]
    return jnp.concatenate(parts, axis=0)


def _swa_unstack(x_all, lo, p):
    r0 = 2 * p * WINDOW
    return jnp.where(lo, x_all[r0:r0 + WINDOW, :], x_all[r0 + WINDOW:r0 + 2 * WINDOW, :])


def _swa_sinks(sink_ref, kv):
    return jnp.concatenate([jnp.full((WINDOW, 1), sink_ref[SWA_GROUP * kv + h], F32) for h in range(SWA_GROUP)],
                           axis=0)


def _swa_probs(qm, kdup, valid, sink):
    s = _dot(qm, kdup, NT) * (SWA_DH ** -0.5)
    s = jnp.where(valid, s, NEG)
    m = jnp.maximum(jnp.max(s, axis=-1, keepdims=True), sink)
    p = jnp.exp(s - m)
    es = jnp.exp(sink - m)
    inv = 1.0 / (jnp.sum(p, axis=-1, keepdims=True) + es)
    return p * inv, es * inv


def _swa_fwd(proj, sinks, gla_out, dep=None):
    t = proj.shape[0]

    deps, dep_specs = _dep_operand(dep)

    def body(sink_ref, q_ref, kp_ref, kc_ref, vp_ref, vc_ref, gla_ref, *rest):
        o_ref = rest[-1]
        i = pl.program_id(0)
        o_ref[:, :D_GLA] = gla_ref[...]
        valid, lo = _swa_masks(i, 1)
        kb = jnp.concatenate([kp_ref[...], kc_ref[...]], axis=0)
        vb = jnp.concatenate([vp_ref[...], vc_ref[...]], axis=0)
        for kv in range(2):
            kdup = _dup_half(kb, lo, kv).astype(BF16)
            vdup = _dup_half(vb, lo, kv).astype(BF16)
            for p in range(4):
                c0 = LANES * (4 * kv + p)
                qp = q_ref[:, c0:c0 + LANES]
                halves = []
                for e in range(2):
                    qm = jnp.where(lo if e == 0 else ~lo, qp, 0.0).astype(BF16)
                    pn, _ = _swa_probs(qm, kdup, valid, sink_ref[SWA_GROUP * kv + 2 * p + e])
                    halves.append(_dot(pn.astype(BF16), vdup, NN))
                o_ref[:, D_GLA + c0:D_GLA + c0 + LANES] = jnp.where(lo, halves[0], halves[1]).astype(BF16)

    kvspec = lambda col, prev: pl.BlockSpec(
        (WINDOW, LANES), (lambda i: (jnp.maximum(i - 1, 0), col)) if prev else (lambda i: (i, col)))
    return pl.pallas_call(
        body, name="swa_fwd", grid=(t // WINDOW,),
        in_specs=[pl.BlockSpec(memory_space=pltpu.SMEM),
                  pl.BlockSpec((WINDOW, D_SWA), lambda i: (i, C_QS // D_SWA)),
                  kvspec(C_KS // LANES, True), kvspec(C_KS // LANES, False),
                  kvspec(C_VS // LANES, True), kvspec(C_VS // LANES, False),
                  pl.BlockSpec((WINDOW, D_GLA), lambda i: (i, 0)), *dep_specs],
        out_specs=pl.BlockSpec((WINDOW, D_MODEL), lambda i: (i, 0)),
        out_shape=jax.ShapeDtypeStruct((t, D_MODEL), BF16),
        compiler_params=_cparams(("parallel",)),
    )(sinks, proj, proj, proj, proj, proj, gla_out, *deps)


def _mix_ln1(cat, w_out, x, ln_g, ln_b):
    t = cat.shape[0]
    tm, tk = min(256, t), D_MODEL

    def epilogue(acc_ref, extra, outs):
        x_ref, g_ref, b_ref = extra
        r1_ref, h1_ref, h1b_ref = outs
        for r0, n in _row_chunks(tm):
            rs = slice(r0, r0 + n)
            r1 = ALPHA * x_ref[rs, :] + acc_ref[rs, :]
            xhat, _ = _ln_stats(r1)
            h = xhat * g_ref[...] + b_ref[...]
            r1_ref[rs, :] = r1
            h1_ref[rs, :] = h
            h1b_ref[rs, :] = h.astype(BF16)

    row = pl.BlockSpec((tm, D_MODEL), lambda i, j, k: (i, 0))
    vec = pl.BlockSpec((1, D_MODEL), lambda i, j, k: (0, 0))
    return _matmul(
        "mix_ln1", cat, w_out, NN, (t // tm, 1, D_MODEL // tk),
        pl.BlockSpec((tm, tk), lambda i, j, k: (i, k)),
        pl.BlockSpec((tk, D_MODEL), lambda i, j, k: (k, 0)),
        [jax.ShapeDtypeStruct((t, D_MODEL), F32), jax.ShapeDtypeStruct((t, D_MODEL), F32),
         jax.ShapeDtypeStruct((t, D_MODEL), BF16)],
        [row, row, row], (tm, D_MODEL),
        extra=(x, ln_g, ln_b), extra_specs=(row, vec, vec), epilogue=epilogue)


def _mlp_up(h1b, w_up):
    t = h1b.shape[0]
    tm, tn = min(1024, t), 1024

    def epilogue(acc_ref, extra, outs):
        a_ref, hdn_ref = outs
        for r0, n in _row_chunks(tm, 256):
            rs = slice(r0, r0 + n)
            a = jnp.maximum(acc_ref[rs, :], 0.0)
            a_ref[rs, :] = a.astype(BF16)
            hdn_ref[rs, :] = (a * a).astype(BF16)

    out = pl.BlockSpec((tm, tn), lambda i, j, k: (i, j))
    return _matmul(
        "mlp_up", h1b, w_up, NN, (t // tm, D_FF // tn, 1),
        pl.BlockSpec((tm, D_MODEL), lambda i, j, k: (i, 0)),
        pl.BlockSpec((None, D_MODEL, tn), lambda i, j, k: (j, 0, 0)),
        [jax.ShapeDtypeStruct((t, D_FF), BF16), jax.ShapeDtypeStruct((t, D_FF), BF16)],
        [out, out], (tm, tn), epilogue=epilogue)


def _mlp_down_loss(hdn, w_down, h1, target, ln_g, ln_b):
    t = hdn.shape[0]
    tm, tn, tk = min(1024, t), 1024, 4096
    ff = _matmul(
        "mlp_down", hdn, w_down, NN, (t // tm, D_MODEL // tn, D_FF // tk),
        pl.BlockSpec((tm, tk), lambda i, j, k: (i, k)),
        pl.BlockSpec((tk, tn), lambda i, j, k: (k, j)),
        jax.ShapeDtypeStruct((t, D_MODEL), F32),
        pl.BlockSpec((tm, tn), lambda i, j, k: (i, j)), (tm, tn))
    r = min(256, t)

    def body(ff_ref, h1_ref, t_ref, g_ref, b_ref, dr2_ref, dr2b_ref, gg_ref, gb_ref, loss_ref):
        @pl.when(pl.program_id(0) == 0)
        def _():
            gg_ref[...] = jnp.zeros_like(gg_ref)
            gb_ref[...] = jnp.zeros_like(gb_ref)
            loss_ref[...] = jnp.zeros_like(loss_ref)

        for r0, n in _row_chunks(r, 64):
            rs = slice(r0, r0 + n)
            xhat, rstd = _ln_stats(ALPHA * h1_ref[rs, :] + ff_ref[rs, :])
            err = xhat * g_ref[...] + b_ref[...] - t_ref[rs, :]
            loss_ref[...] += 0.5 * jnp.sum(jnp.mean(err * err, axis=-1, keepdims=True))
            dy = err * (1.0 / D_MODEL)
            gg_ref[...] += jnp.sum(dy * xhat, axis=0, keepdims=True)
            gb_ref[...] += jnp.sum(dy, axis=0, keepdims=True)
            dr2 = _ln_bwd(dy * g_ref[...], xhat, rstd)
            dr2_ref[rs, :] = dr2
            dr2b_ref[rs, :] = dr2.astype(BF16)

    row = pl.BlockSpec((r, D_MODEL), lambda i: (i, 0))
    vec = pl.BlockSpec((1, D_MODEL), lambda i: (0, 0))
    return pl.pallas_call(
        body, name="ln2_loss", grid=(t // r,),
        in_specs=[row, row, row, vec, vec],
        out_specs=[row, row, vec, vec, pl.BlockSpec((1, LANES), lambda i: (0, 0))],
        out_shape=[jax.ShapeDtypeStruct((t, D_MODEL), F32), jax.ShapeDtypeStruct((t, D_MODEL), BF16),
                   jax.ShapeDtypeStruct((1, D_MODEL), F32), jax.ShapeDtypeStruct((1, D_MODEL), F32),
                   jax.ShapeDtypeStruct((1, LANES), F32)],
        compiler_params=_cparams(("arbitrary",)),
    )(ff, h1, target, ln_g, ln_b)


def _mlp_down_bwd(dr2b, w_down, a_act):
    t = dr2b.shape[0]
    tm, tn = min(1024, t), 1024

    def epilogue(acc_ref, extra, outs):
        (a_ref,) = extra
        for r0, n in _row_chunks(tm, 256):
            rs = slice(r0, r0 + n)
            outs[0][rs, :] = (acc_ref[rs, :] * (2.0 * a_ref[rs, :].astype(F32))).astype(BF16)

    blk = pl.BlockSpec((tm, tn), lambda i, j, k: (i, j))
    return _matmul(
        "mlp_down_bwd", dr2b, w_down, NT, (t // tm, D_FF // tn, 1),
        pl.BlockSpec((tm, D_MODEL), lambda i, j, k: (i, 0)),
        pl.BlockSpec((tn, D_MODEL), lambda i, j, k: (j, 0)),
        jax.ShapeDtypeStruct((t, D_FF), BF16), blk, (tm, tn),
        extra=(a_act,), extra_specs=(blk,), epilogue=epilogue)


def _grad_w_down(hdn, dr2b):
    t = hdn.shape[0]
    tm, tn = 1024, 1024
    return _matmul(
        "grad_w_down", hdn, dr2b, TN, (D_MODEL // tn, D_FF // tm, 1),
        pl.BlockSpec((t, tm), lambda j, i, k: (0, i)),
        pl.BlockSpec((t, tn), lambda j, i, k: (0, j)),
        jax.ShapeDtypeStruct((N_DEV, D_FF // N_DEV, D_MODEL), F32),
        pl.BlockSpec((None, tm, tn), lambda j, i, k: (i, 0, j)), (tm, tn))


def _grad_w_up(h1b, du, dep=None):
    t = h1b.shape[0]
    tm, tn = 1024, 1024
    return _matmul(
        "grad_w_up", h1b, du, TN, (N_DEV, D_MODEL // tm, 1),
        pl.BlockSpec((t, tm), lambda i, j, k: (0, j)),
        pl.BlockSpec((t, tn), lambda i, j, k: (0, i)),
        jax.ShapeDtypeStruct((N_DEV, D_MODEL, D_FF // N_DEV), F32),
        pl.BlockSpec((None, tm, tn), lambda i, j, k: (i, j, 0)), (tm, tn), dep=dep)


def _mlp_up_bwd_ln1(du, w_up, dr2, r1, ln_g, dep=None):
    t = du.shape[0]
    tm, tn, tk, nb = min(1024, t), 1024, D_FF // N_DEV, 4
    deps, dep_specs = _dep_operand(dep)

    def mm_body(a_ref, b_ref, *rest):
        o_ref = rest[-1]
        k = pl.program_id(2)
        part = _dot(a_ref[:, :tk], b_ref[0], NT)
        for d in range(1, nb):
            part = part + _dot(a_ref[:, d * tk:(d + 1) * tk], b_ref[d], NT)

        @pl.when(k == 0)
        def _():
            o_ref[...] = part

        @pl.when(k > 0)
        def _():
            o_ref[...] += part

    dff = pl.pallas_call(
        mm_body, name="mlp_up_bwd", grid=(t // tm, D_MODEL // tn, N_DEV // nb),
        in_specs=[pl.BlockSpec((tm, nb * tk), lambda i, j, k: (i, k)),
                  pl.BlockSpec((nb, tn, tk), lambda i, j, k: (k, j, 0)), *dep_specs],
        out_specs=pl.BlockSpec((tm, tn), lambda i, j, k: (i, j)),
        out_shape=jax.ShapeDtypeStruct((t, D_MODEL), F32),
        compiler_params=_cparams(("parallel", "parallel", "arbitrary")),
    )(du, w_up, *deps)
    r = min(256, t)

    def body(acc_ref, dr2_ref, r1_ref, g_ref, dr1_ref, dr1b_ref, gg_ref, gb_ref):
        @pl.when(pl.program_id(0) == 0)
        def _():
            gg_ref[...] = jnp.zeros_like(gg_ref)
            gb_ref[...] = jnp.zeros_like(gb_ref)

        for r0, n in _row_chunks(r, 64):
            rs = slice(r0, r0 + n)
            dh1 = ALPHA * dr2_ref[rs, :] + acc_ref[rs, :]
            xhat, rstd = _ln_stats(r1_ref[rs, :])
            gg_ref[...] += jnp.sum(dh1 * xhat, axis=0, keepdims=True)
            gb_ref[...] += jnp.sum(dh1, axis=0, keepdims=True)
            dr1 = _ln_bwd(dh1 * g_ref[...], xhat, rstd)
            dr1_ref[rs, :] = dr1
            dr1b_ref[rs, :] = dr1.astype(BF16)

    row = pl.BlockSpec((r, D_MODEL), lambda i: (i, 0))
    vec = pl.BlockSpec((1, D_MODEL), lambda i: (0, 0))
    return pl.pallas_call(
        body, name="ln1_bwd", grid=(t // r,),
        in_specs=[row, row, row, vec],
        out_specs=[row, row, vec, vec],
        out_shape=[jax.ShapeDtypeStruct((t, D_MODEL), F32), jax.ShapeDtypeStruct((t, D_MODEL), BF16),
                   jax.ShapeDtypeStruct((1, D_MODEL), F32), jax.ShapeDtypeStruct((1, D_MODEL), F32)],
        compiler_params=_cparams(("arbitrary",)),
    )(dff, dr2, r1, ln_g)


def _dcat(dr1b, w_out):
    t = dr1b.shape[0]
    tm, tn = min(1024, t), 1024
    return _matmul(
        "dcat", dr1b, w_out, NT, (t // tm, D_MODEL // tn, 1),
        pl.BlockSpec((tm, D_MODEL), lambda i, j, k: (i, 0)),
        pl.BlockSpec((tn, D_MODEL), lambda i, j, k: (j, 0)),
        jax.ShapeDtypeStruct((t, D_MODEL), F32),
        pl.BlockSpec((tm, tn), lambda i, j, k: (i, j)), (tm, tn))


def _grad_w_out(cat, dr1b, dep=None):
    t = cat.shape[0]
    tm, tn = 1024, 1024
    return _matmul(
        "grad_w_out", cat, dr1b, TN, (D_MODEL // tm, D_MODEL // tn, 1),
        pl.BlockSpec((t, tm), lambda i, j, k: (0, i)),
        pl.BlockSpec((t, tn), lambda i, j, k: (0, j)),
        jax.ShapeDtypeStruct((N_DEV, D_MODEL // N_DEV, D_MODEL), F32),
        pl.BlockSpec((tm // (D_MODEL // N_DEV), D_MODEL // N_DEV, tn), lambda i, j, k: (i, 0, j)), (tm, tn), dep=dep)


def _gla_norm_bwd(dcat, o_raw, proj, norm_w, dep=None):
    t = o_raw.shape[0]
    r = min(512, t)

    deps, dep_specs = _dep_operand(dep)

    def body(d_ref, o_ref, g_ref, w_ref, *rest):
        do_ref, dg_ref, dw_ref = rest[len(deps):]

        @pl.when(pl.program_id(0) == 0)
        def _():
            dw_ref[...] = jnp.zeros_like(dw_ref)

        w = w_ref[...]
        dw = jnp.zeros((1, GLA_DV), F32)
        for h in range(GLA_HEADS):
            sl = slice(h * GLA_DV, (h + 1) * GLA_DV)
            o = o_ref[:, sl]
            g = g_ref[:, sl]
            d = d_ref[:, sl]
            rr = lax.rsqrt(jnp.mean(o * o, axis=-1, keepdims=True) + RMS_EPS)
            on = o * rr
            sg = jax.nn.sigmoid(g)
            sil = g * sg
            dg_ref[:, sl] = (d * on * w * (sg * (1.0 + g * (1.0 - sg)))).astype(BF16)
            dw = dw + jnp.sum(d * on * sil, axis=0, keepdims=True)
            don = d * w * sil
            do_ref[:, sl] = rr * (don - on * jnp.mean(don * on, axis=-1, keepdims=True))
        dw_ref[...] += dw

    return pl.pallas_call(
        body, name="gla_norm_bwd", grid=(t // r,),
        in_specs=[pl.BlockSpec((r, D_GLA), lambda i: (i, 0)),
                  pl.BlockSpec((r, D_GLA), lambda i: (i, 0)),
                  pl.BlockSpec((r, D_GLA), lambda i: (i, C_GG // D_GLA)),
                  pl.BlockSpec((1, GLA_DV), lambda i: (0, 0)), *dep_specs],
        out_specs=[pl.BlockSpec((r, D_GLA), lambda i: (i, 0)),
                   pl.BlockSpec((r, D_GLA), lambda i: (i, 0)),
                   pl.BlockSpec((1, GLA_DV), lambda i: (0, 0))],
        out_shape=[jax.ShapeDtypeStruct((t, D_GLA), F32), jax.ShapeDtypeStruct((t, D_GLA), BF16),
                   jax.ShapeDtypeStruct((1, GLA_DV), F32)],
        compiler_params=_cparams(("arbitrary",)),
    )(dcat, o_raw, proj, norm_w, *deps)


def _gla_bwd(proj, bloc, do_raw, states):
    t = proj.shape[0]
    r = min(256, t)
    ncb = r // GC
    nb = t // r
    scale = GLA_DK ** -0.5

    def body(q_ref, k_ref, v_ref, b_ref, do_ref, st_ref, dq_ref, dk_ref, dv_ref, db_ref, ds_scr):
        @pl.when(pl.program_id(1) == 0)
        def _():
            ds_scr[...] = jnp.zeros_like(ds_scr)

        rows = lax.broadcasted_iota(jnp.int32, (GC, 1), 0)

        def chunk(cc, carry):
            c = ncb - 1 - cc
            r0 = pl.multiple_of(c * GC, GC)
            q = q_ref[pl.ds(r0, GC), :] * scale
            k = k_ref[pl.ds(r0, GC), :]
            v = v_ref[pl.ds(r0, GC), :]
            b = b_ref[pl.ds(r0, GC), :]
            do = do_ref[pl.ds(r0, GC), :]
            st = st_ref[c]
            dsn = ds_scr[...]
            bl = b[GC - 1:GC, :]
            eb = jnp.exp(b)
            ekl = jnp.exp(bl - b)
            ebl = jnp.exp(bl)
            qh = q * eb
            kd = k * ekl
            dob = do.astype(BF16)
            dsb = dsn.astype(BF16)
            dqh = _dot(dob, st.astype(BF16), NN)
            dkd = _dot(v.astype(BF16), dsb, NN)
            dv = _dot(kd.astype(BF16), dsb, NT)
            dq_i = jnp.zeros((GC, GLA_DK), F32)
            dk_i = jnp.zeros((GC, GLA_DK), F32)
            dk_x = jnp.zeros((GC, GLA_DK), F32)
            kr = _bf16_round(k)
            vr = _bf16_round(v)
            dor = _bf16_round(do)
            for i in range(GC):
                w = jnp.exp(jnp.where(rows <= i, b[i:i + 1, :] - b, NEG))
                qw = q[i:i + 1, :] * w
                qwr = _bf16_round(qw)
                a = _bf16_round(jnp.sum(qwr * kr, axis=-1, keepdims=True))
                da = jnp.sum(vr * dor[i:i + 1, :], axis=-1, keepdims=True)
                dv = dv + a * dor[i:i + 1, :]
                dk_x = dk_x + da * qw
                dk_i = dk_i + _bf16_round(da) * qwr
                dq_i = jnp.where(rows == i, jnp.sum(da * (w * k), axis=0, keepdims=True), dq_i)
            dqs = dqh * eb + dq_i
            dk = dkd * ekl + dk_i
            db_last = jnp.sum(dkd * kd, axis=0, keepdims=True) + ebl * jnp.sum(dsn * st, axis=0, keepdims=True)
            db = q * dqs - k * (dkd * ekl + dk_x) + jnp.where(rows == GC - 1, db_last, 0.0)
            dq_ref[pl.ds(r0, GC), :] = (dqs * scale).astype(BF16)
            dk_ref[pl.ds(r0, GC), :] = dk.astype(BF16)
            dv_ref[pl.ds(r0, GC), :] = dv.astype(BF16)
            db_ref[pl.ds(r0, GC), :] = db
            ds_scr[...] = dsn * ebl + _dot(dob, qh.astype(BF16), TN)
            return carry

        lax.fori_loop(0, ncb, chunk, 0)

    rev = lambda i: nb - 1 - i
    return pl.pallas_call(
        body, name="gla_bwd", grid=(GLA_HEADS, nb),
        in_specs=[pl.BlockSpec((r, GLA_DK), lambda h, i: (rev(i), C_QG // GLA_DK + h)),
                  pl.BlockSpec((r, GLA_DK), lambda h, i: (rev(i), C_KG // GLA_DK + h)),
                  pl.BlockSpec((r, GLA_DV), lambda h, i: (rev(i), C_VG // GLA_DV + h)),
                  pl.BlockSpec((r, GLA_DK), lambda h, i: (rev(i), h)),
                  pl.BlockSpec((r, GLA_DV), lambda h, i: (rev(i), h)),
                  pl.BlockSpec((ncb, None, GLA_DV, GLA_DK), lambda h, i: (rev(i), h, 0, 0))],
        out_specs=[pl.BlockSpec((r, GLA_DK), lambda h, i: (rev(i), h)),
                   pl.BlockSpec((r, GLA_DK), lambda h, i: (rev(i), h)),
                   pl.BlockSpec((r, GLA_DV), lambda h, i: (rev(i), h)),
                   pl.BlockSpec((r, GLA_DK), lambda h, i: (rev(i), h))],
        out_shape=[jax.ShapeDtypeStruct((t, GLA_KW), BF16), jax.ShapeDtypeStruct((t, GLA_KW), BF16),
                   jax.ShapeDtypeStruct((t, D_GLA), BF16), jax.ShapeDtypeStruct((t, GLA_KW), F32)],
        scratch_shapes=[pltpu.VMEM((GLA_DV, GLA_DK), F32)],
        compiler_params=_cparams(("parallel", "arbitrary")),
    )(proj, proj, proj, bloc, do_raw, states)


def _gate_bwd(db, dgdz, proj, w2p, triu):
    t = db.shape[0]
    r = triu.shape[0]

    def body(db_ref, s_ref, lo_ref, w_ref, u_ref, dlo_ref, gw_ref, gb_ref):
        @pl.when(pl.program_id(0) == 0)
        def _():
            gw_ref[...] = jnp.zeros_like(gw_ref)
            gb_ref[...] = jnp.zeros_like(gb_ref)

        dz = _dot(u_ref[...], db_ref[...], NN, precision=lax.Precision.HIGHEST) * s_ref[...]
        dzb = dz.astype(BF16)
        gb_ref[...] += jnp.sum(dz, axis=0, keepdims=True)
        gw_ref[...] += _dot(lo_ref[...].astype(BF16), dzb, TN)
        dlo_ref[...] = _dot(dzb, w_ref[...], NT).astype(BF16)

    return pl.pallas_call(
        body, name="gate_bwd", grid=(t // r,),
        in_specs=[pl.BlockSpec((r, GLA_KW), lambda i: (i, 0)),
                  pl.BlockSpec((r, GLA_KW), lambda i: (i, 0)),
                  pl.BlockSpec((r, LANES), lambda i: (i, C_LO // LANES)),
                  pl.BlockSpec((LANES, GLA_KW), lambda i: (0, 0)),
                  pl.BlockSpec((r, r), lambda i: (0, 0))],
        out_specs=[pl.BlockSpec((r, LANES), lambda i: (i, 0)),
                   pl.BlockSpec((LANES, GLA_KW), lambda i: (0, 0)),
                   pl.BlockSpec((1, GLA_KW), lambda i: (0, 0))],
        out_shape=[jax.ShapeDtypeStruct((t, LANES), BF16), jax.ShapeDtypeStruct((LANES, GLA_KW), F32),
                   jax.ShapeDtypeStruct((1, GLA_KW), F32)],
        compiler_params=_cparams(("arbitrary",)),
    )(db, dgdz, proj, w2p, triu)


def _swa_bwd(proj, dcat, sinks, dep=None):
    t = proj.shape[0]

    deps, dep_specs = _dep_operand(dep)

    def body(sink_ref, q_ref, kp_ref, kc_ref, vp_ref, vc_ref, d_ref, *rest):
        dq_ref, dk_ref, dv_ref, dsink_ref = rest[len(deps):]
        i = pl.program_id(0)

        @pl.when(i == 0)
        def _():
            dk_ref[...] = jnp.zeros_like(dk_ref)
            dv_ref[...] = jnp.zeros_like(dv_ref)
            dsink_ref[...] = jnp.zeros_like(dsink_ref)

        valid, lo = _swa_masks(i, SWA_GROUP)
        lane = lax.broadcasted_iota(jnp.int32, (1, LANES), 1)
        kb = jnp.concatenate([kp_ref[...], kc_ref[...]], axis=0)
        vb = jnp.concatenate([vp_ref[...], vc_ref[...]], axis=0)
        dsink = jnp.zeros((1, LANES), F32)
        folded_k, folded_v = [], []
        for kv in range(2):
            kdup = _dup_half(kb, lo, kv).astype(BF16)
            vdup = _dup_half(vb, lo, kv).astype(BF16)
            qm = _swa_stack(q_ref, 0, lo, kv)
            dom = _swa_stack(d_ref, 0, lo, kv)
            pn, psink = _swa_probs(qm, kdup, valid, _swa_sinks(sink_ref, kv))
            dpr = _dot(dom, vdup, NT)
            drow = jnp.sum(dpr * pn, axis=-1, keepdims=True)
            ds_col = psink * drow
            for h in range(SWA_GROUP):
                dsink = dsink + jnp.where(lane == SWA_GROUP * kv + h,
                                          -jnp.sum(ds_col[h * WINDOW:(h + 1) * WINDOW, :]), 0.0)
            dsb = (pn * (dpr - drow) * (SWA_DH ** -0.5)).astype(BF16)
            dq_all = _dot(dsb, kdup, NN)
            for p in range(4):
                c0 = LANES * (4 * kv + p)
                dq_ref[:, c0:c0 + LANES] = _swa_unstack(dq_all, lo, p).astype(BF16)
            dkd = _dot(dsb, qm, TN)
            dvd = _dot(pn.astype(BF16), dom, TN)
            folded_k.append(dkd + pltpu.roll(dkd, SWA_DH, axis=1))
            folded_v.append(dvd + pltpu.roll(dvd, SWA_DH, axis=1))
        dkb = jnp.where(lo, folded_k[0], folded_k[1])
        dvb = jnp.where(lo, folded_v[0], folded_v[1])
        dsink_ref[...] += dsink
        cur = pl.ds(pl.multiple_of(i * WINDOW, WINDOW), WINDOW)
        dk_ref[cur, :] += dkb[WINDOW:, :]
        dv_ref[cur, :] += dvb[WINDOW:, :]

        @pl.when(i > 0)
        def _():
            prev = pl.ds(pl.multiple_of((i - 1) * WINDOW, WINDOW), WINDOW)
            dk_ref[prev, :] += dkb[:WINDOW, :]
            dv_ref[prev, :] += dvb[:WINDOW, :]

    kvspec = lambda col, prev: pl.BlockSpec(
        (WINDOW, LANES), (lambda i: (jnp.maximum(i - 1, 0), col)) if prev else (lambda i: (i, col)))
    full = pl.BlockSpec((t, LANES), lambda i: (0, 0))
    return pl.pallas_call(
        body, name="swa_bwd", grid=(t // WINDOW,),
        in_specs=[pl.BlockSpec(memory_space=pltpu.SMEM),
                  pl.BlockSpec((WINDOW, D_SWA), lambda i: (i, C_QS // D_SWA)),
                  kvspec(C_KS // LANES, True), kvspec(C_KS // LANES, False),
                  kvspec(C_VS // LANES, True), kvspec(C_VS // LANES, False),
                  pl.BlockSpec((WINDOW, D_SWA), lambda i: (i, 1)), *dep_specs],
        out_specs=[pl.BlockSpec((WINDOW, D_SWA), lambda i: (i, 0)), full, full,
                   pl.BlockSpec((1, LANES), lambda i: (0, 0))],
        out_shape=[jax.ShapeDtypeStruct((t, D_SWA), BF16), jax.ShapeDtypeStruct((t, LANES), F32),
                   jax.ShapeDtypeStruct((t, LANES), F32), jax.ShapeDtypeStruct((1, LANES), F32)],
        compiler_params=_cparams(("arbitrary",)),
    )(sinks, proj, proj, proj, proj, proj, dcat, *deps)


def _grad_w_in(xb, dproj):
    t = xb.shape[0]
    tm, tn = 640, 1024
    return _matmul(
        "grad_w_in", dproj, xb, TN, (D_INP // tm, D_MODEL // tn, 1),
        pl.BlockSpec((t, tm), lambda i, j, k: (0, i)),
        pl.BlockSpec((t, tn), lambda i, j, k: (0, j)),
        jax.ShapeDtypeStruct((D_INP, D_MODEL), F32),
        pl.BlockSpec((tm, tn), lambda i, j, k: (i, j)), (tm, tn))


def _grad_x(dproj, w_in_t, dr1, dep=None):
    t = dproj.shape[0]
    tm, tn = min(512, t), 1024

    def epilogue(acc_ref, extra, outs):
        for r0, n in _row_chunks(tm):
            rs = slice(r0, r0 + n)
            outs[0][rs, :] = ALPHA * extra[0][rs, :] + acc_ref[rs, :]

    blk = pl.BlockSpec((tm, tn), lambda j, i, k: (i, j))
    return _matmul(
        "grad_x", dproj, w_in_t, NN, (D_MODEL // tn, t // tm, 1),
        pl.BlockSpec((tm, D_INP), lambda j, i, k: (i, 0)),
        pl.BlockSpec((D_INP, tn), lambda j, i, k: (0, j)),
        jax.ShapeDtypeStruct((t, D_MODEL), F32), blk, (tm, tn),
        extra=(dr1,), extra_specs=(blk,), epilogue=epilogue, dep=dep)


def _place():
    x, y, c = lax.axis_index("x"), lax.axis_index("y"), lax.axis_index("c")
    chips = [(1 - x, y), (x, 1 - y), (1 - x, 1 - y)]
    return x, y, c, chips


def _plan_gather_out(src, land, x, y, c, chips):
    me = 4 * x + 2 * y + c
    return [(src, land.at[me], to) for to in [(x, y, 1 - c)] + [(px, py, c) for px, py in chips]]


def _plan_gather_forward(src, land, x, y, c, chips):
    return [(land.at[4 * px + 2 * py + c], land.at[4 * px + 2 * py + c], (x, y, 1 - c)) for px, py in chips]


def _plan_sibling(src, land, x, y, c, chips):
    return [(src.at[2 * q + (1 - c)], land.at[q], (x, y, 1 - c)) for q in range(4)]


def _plan_chips(src, land, x, y, c, chips):
    return [(src.at[j], land.at[j], (px, py, c)) for j, (px, py) in enumerate(chips)]


_PLAN_COPIES = {_plan_gather_out: 4, _plan_gather_forward: 3, _plan_sibling: 4, _plan_chips: 3}
_HBM = pl.BlockSpec(memory_space=pltpu.HBM)
_SEM = pl.BlockSpec(memory_space=pltpu.SEMAPHORE)
_EFFECT = pltpu.SideEffectType.DATAFLOW_SIDE_EFFECTING


def _hbm(a):
    return pltpu.with_memory_space_constraint(a, pltpu.HBM)


def _plan_descriptors(plans, srcs, lands, send, recv):
    x, y, c, chips = _place()
    cps = []
    for plan, src, land in zip(plans, srcs, lands):
        for s_ref, d_ref, to in plan(src, land, x, y, c, chips):
            k = len(cps)
            cps.append(pltpu.make_async_remote_copy(src_ref=s_ref, dst_ref=d_ref, send_sem=send.at[k],
                                                    recv_sem=recv.at[k], device_id=to, device_id_type=MESH))
    return cps


def _copies_start(name, plans, srcs, lands, after=None):
    has_src = [s is not None for s in srcs]
    arrays = [s for s in srcs if s is not None] + list(lands)
    n_src = sum(has_src)
    n_cp = sum(_PLAN_COPIES[p] for p in plans)
    afters = [] if after is None else [after]

    def body(*refs):
        ins = refs[:len(arrays)]
        send, recv = refs[len(arrays) + len(afters)], refs[len(arrays) + len(afters) + 1]
        token = refs[-1]
        it = iter(ins[:n_src])
        src_refs = [next(it) if h else None for h in has_src]
        for cp in _plan_descriptors(plans, src_refs, ins[n_src:], send, recv):
            cp.start()
        token[...] = jnp.zeros_like(token)

    outs = pl.pallas_call(
        body, name=name,
        in_specs=[_HBM] * len(arrays) + [pl.BlockSpec(memory_space=pl.ANY)] * len(afters),
        out_specs=(_SEM, _SEM, *[_HBM] * len(arrays), pl.BlockSpec(memory_space=pltpu.VMEM)),
        out_shape=(pltpu.SemaphoreType.DMA((n_cp,)), pltpu.SemaphoreType.DMA((n_cp,)),
                   *[pltpu.HBM(a.shape, a.dtype) for a in arrays], jax.ShapeDtypeStruct((8, LANES), F32)),
        input_output_aliases={i: 2 + i for i in range(len(arrays))},
        compiler_params=pltpu.CompilerParams(has_side_effects=_EFFECT),
    )(*[_hbm(a) for a in arrays], *afters)
    send, recv = outs[0], outs[1]
    thru = list(outs[2:-1])
    it = iter(thru[:n_src])
    return send, recv, [next(it) if h else None for h in has_src], thru[n_src:], outs[-1]


def _copies_wait(name, plans, started, after):
    send, recv, srcs, lands, _ = started
    has_src = [s is not None for s in srcs]
    arrays = [s for s in srcs if s is not None] + list(lands)
    n_src = sum(has_src)

    def body(*refs):
        ins = refs[:len(arrays)]
        send_ref, recv_ref = refs[len(arrays)], refs[len(arrays) + 1]
        it = iter(ins[:n_src])
        src_refs = [next(it) if h else None for h in has_src]
        for cp in _plan_descriptors(plans, src_refs, ins[n_src:], send_ref, recv_ref):
            cp.wait_send()
            cp.wait_recv()

    outs = pl.pallas_call(
        body, name=name,
        in_specs=[_HBM] * len(arrays) + [_SEM, _SEM, pl.BlockSpec(memory_space=pl.ANY)],
        out_specs=tuple([_HBM] * len(arrays)),
        out_shape=tuple(pltpu.HBM(a.shape, a.dtype) for a in arrays),
        input_output_aliases={i: i for i in range(len(arrays))},
        compiler_params=pltpu.CompilerParams(has_side_effects=_EFFECT),
    )(*arrays, send, recv, after)
    return list(outs[:n_src]), list(outs[n_src:])


def _shard_tiles(rows, cols, tr):
    if rows % tr == 0:
        return (tr, cols), rows // tr, lambda r: (r, 0)
    tc = 2 * LANES
    return (rows, tc), cols // tc, lambda r: (0, r)


def _pair_sum(name, grad, from_sibling, blocks):
    _, rows, cols = grad.shape
    (br_, bc), steps, at = _shard_tiles(rows, cols, 256)

    def body(blk_ref, g_ref, s_ref, o_ref):
        o_ref[...] = (g_ref[...] + s_ref[...]).astype(BF16)

    return pl.pallas_call(
        body, name=name,
        grid_spec=pltpu.PrefetchScalarGridSpec(
            num_scalar_prefetch=1, grid=(3, steps),
            in_specs=[pl.BlockSpec((None, br_, bc), lambda j, r, br: (br[j], *at(r))),
                      pl.BlockSpec((None, br_, bc), lambda j, r, br: (br[3 + j], *at(r)))],
            out_specs=pl.BlockSpec((None, br_, bc), lambda j, r, br: (j, *at(r)))),
        out_shape=jax.ShapeDtypeStruct((3, rows, cols), BF16),
        compiler_params=_cparams(("parallel", "parallel")),
    )(blocks, grad, from_sibling)


def _adam_math(g, w, m, v):
    m2 = ADAM_B1 * m + (1.0 - ADAM_B1) * g
    v2 = ADAM_B2 * v + (1.0 - ADAM_B2) * (g * g)
    m_hat = m2 / (1.0 - ADAM_B1 ** ADAM_STEP)
    v_hat = v2 / (1.0 - ADAM_B2 ** ADAM_STEP)
    delta = -ADAM_LR * (m_hat / (jnp.sqrt(v_hat) + ADAM_EPS) + ADAM_WD * w)
    return delta, m2, v2


def _sum_adam(name, grad, from_sibling, from_chips, own, w, m, v, dep=None):
    rows, cols = w.shape
    (br_, bc), steps, at = _shard_tiles(rows, cols, 128)

    deps, dep_specs = _dep_operand(dep)

    def body(own_ref, p_ref, s_ref, r_ref, w_ref, m_ref, v_ref, *rest):
        g_out, d_out, m_out, v_out = rest[len(deps):]
        g = p_ref[...] + s_ref[...]
        for j in range(3):
            g = g + r_ref[j].astype(F32)
        d, m2, v2 = _adam_math(g, w_ref[...], m_ref[...], v_ref[...])
        g_out[...] = g
        d_out[...] = d
        m_out[...] = m2
        v_out[...] = v2

    blk = pl.BlockSpec((br_, bc), lambda r, cr: at(r))
    shp = jax.ShapeDtypeStruct((rows, cols), F32)
    return pl.pallas_call(
        body, name=name,
        grid_spec=pltpu.PrefetchScalarGridSpec(
            num_scalar_prefetch=1, grid=(steps,),
            in_specs=[pl.BlockSpec((None, br_, bc), lambda r, cr: (cr[0], *at(r))),
                      pl.BlockSpec((None, br_, bc), lambda r, cr: (cr[1], *at(r))),
                      pl.BlockSpec((3, br_, bc), lambda r, cr: (0, *at(r))),
                      blk, blk, blk, *dep_specs],
            out_specs=[blk, blk, blk, blk]),
        out_shape=[shp, shp, shp, shp],
        compiler_params=_cparams(("parallel",)),
    )(own, grad, from_sibling, from_chips, w, m, v, *deps)


def _adam_small(name, g, w, m, v):
    def body(g_ref, w_ref, m_ref, v_ref, d_out, m_out, v_out):
        d, m2, v2 = _adam_math(g_ref[...], w_ref[...], m_ref[...], v_ref[...])
        d_out[...] = d
        m_out[...] = m2
        v_out[...] = v2

    shp = jax.ShapeDtypeStruct(w.shape, F32)
    return pl.pallas_call(body, name=name, out_shape=[shp, shp, shp])(g, w, m, v)


def _all_reduce_small(pack):
    rows = pack.shape[0]

    def body(in_ref, out_ref, slots, send, recv):
        x, y, c, _ = _place()
        me = 4 * x + 2 * y + c
        slots[me] = in_ref[...]
        cps = []
        for k in range(1, N_DEV):
            dx, dy, dc = (k >> 2) & 1, (k >> 1) & 1, k & 1
            to = (jnp.bitwise_xor(x, dx), jnp.bitwise_xor(y, dy), jnp.bitwise_xor(c, dc))
            cps.append(pltpu.make_async_remote_copy(
                src_ref=in_ref, dst_ref=slots.at[me], send_sem=send.at[k - 1], recv_sem=recv.at[k - 1],
                device_id=to, device_id_type=MESH))
        for cp in cps:
            cp.start()
        for cp in cps:
            cp.wait()
        acc = slots[0]
        for d in range(1, N_DEV):
            acc = acc + slots[d]
        out_ref[...] = acc

    return pl.pallas_call(
        body, name="all_reduce_small",
        in_specs=[pl.BlockSpec(memory_space=pltpu.VMEM)],
        out_specs=pl.BlockSpec(memory_space=pltpu.VMEM),
        out_shape=jax.ShapeDtypeStruct((rows, LANES), F32),
        scratch_shapes=[pltpu.VMEM((N_DEV, rows, LANES), F32),
                        pltpu.SemaphoreType.DMA((N_DEV - 1,)), pltpu.SemaphoreType.DMA((N_DEV - 1,))],
    )(pack)


def _rows128(a):
    flat = a.reshape(-1)
    padn = (-flat.shape[0]) % (8 * LANES)
    if padn:
        flat = jnp.concatenate([flat, jnp.zeros((padn,), flat.dtype)])
    return flat.reshape(-1, LANES)


def kernel(x, w_in, w_gk2, b_gk, gla_norm_w, swa_sinks, w_out, ln1_g, ln1_b, w_up, w_down, ln2_g, ln2_b, loss_target, m_w_in, m_w_gk2, m_b_gk, m_gla_norm_w, m_swa_sinks, m_w_out, m_ln1_g, m_ln1_b, m_w_up, m_w_down, m_ln2_g, m_ln2_b, v_w_in, v_w_gk2, v_b_gk, v_gla_norm_w, v_swa_sinks, v_w_out, v_ln1_g, v_ln1_b, v_w_up, v_w_down, v_ln2_g, v_ln2_b):
    xc, yc, cc = lax.axis_index("x"), lax.axis_index("y"), lax.axis_index("c")
    me = 4 * xc + 2 * yc + cc

    x2 = x[0]
    t = x2.shape[0]
    target = loss_target[0]

    def land_of(sh):
        return lax.dynamic_update_index_in_dim(lax.empty((N_DEV,) + sh.shape, BF16), sh, me, 0)

    out_plans = [_plan_gather_out] * 3
    fwd_plans = [_plan_gather_forward] * 3
    in_shards = [w_in[0].T.astype(BF16), w_gk2[0].astype(BF16)]
    ag0 = _copies_start("gather_in_start", out_plans[:2], in_shards, [land_of(sh) for sh in in_shards])
    zero = ag0[4][0, 0]
    xb = (x2 + zero).astype(BF16)
    shards = [(w[0] + zero).astype(BF16) for w in (w_out, w_up, w_down)]
    lands = [land_of(sh) for sh in shards]
    _, in_lands = _copies_wait("gather_in_wait", out_plans[:2], ag0, lands[2])
    ag0f = _copies_start("gather_in_forward_start", fwd_plans[:2], [None] * 2, in_lands)
    _, first = _copies_wait("gather_in_forward_wait", fwd_plans[:2], ag0f, ag0f[4])
    w_in_t = _to_padded_rows(first[0].reshape(D_IN, D_MODEL))
    w2 = first[1].transpose(1, 0, 2).reshape(GATE_RANK, GLA_KW)
    w2p = jnp.concatenate([w2, jnp.zeros((LANES - GATE_RANK, GLA_KW), BF16)], axis=0)
    ag1 = _copies_start("gather_out_start", out_plans, shards, lands, first[0])

    proj = _proj(xb, w_in_t, dep=ag1[4])
    rt = min(256, t)
    ii = jnp.arange(rt)
    tri = ((ii[:, None] // GC == ii[None, :] // GC) & (ii[None, :] <= ii[:, None])).astype(F32)
    bloc, dgdz = _gate_fwd(proj, w2p, b_gk, tri)
    o_raw, states = _gla_fwd(proj, bloc)
    gla_out = _gla_norm_fwd(o_raw, proj, gla_norm_w)
    cat = _swa_fwd(proj, swa_sinks[0], gla_out)
    _, lands = _copies_wait("gather_out_wait", out_plans, ag1, cat)
    ag2 = _copies_start("gather_forward_out_start", fwd_plans[:1], [None], lands[:1])
    ag3 = _copies_start("gather_forward_mlp_start", fwd_plans[:2], [None] * 2, lands[1:])
    _, (w_out_g,) = _copies_wait("gather_forward_out_wait", fwd_plans[:1], ag2, ag3[4])
    w_out_f = w_out_g.reshape(D_MODEL, D_MODEL)
    r1, h1, h1b = _mix_ln1(cat, w_out_f, x2, ln1_g, ln1_b)
    _, (w_up_f, w_down_g) = _copies_wait("gather_forward_mlp_wait", fwd_plans[:2], ag3, h1b)
    w_down_f = w_down_g.reshape(D_FF, D_MODEL)
    a_act, hdn = _mlp_up(h1b, w_up_f)
    dr2, dr2b, g_ln2_g, g_ln2_b, loss_part = _mlp_down_loss(hdn, w_down_f, h1, target, ln2_g, ln2_b)

    others = [2 * (1 - xc) + yc, 2 * xc + (1 - yc), 2 * (1 - xc) + (1 - yc)]
    blocks = jnp.stack([2 * q + cc for q in others] + others).astype(jnp.int32)
    own = jnp.stack([me, 2 * xc + yc]).astype(jnp.int32)
    wmv = dict(w_in=[a[0].T for a in (w_in, m_w_in, v_w_in)], w_out=[a[0] for a in (w_out, m_w_out, v_w_out)],
               w_up=[a[0] for a in (w_up, m_w_up, v_w_up)], w_down=[a[0] for a in (w_down, m_w_down, v_w_down)])
    big = {}

    def sib_land(g):
        return lax.empty((4,) + g.shape[1:], F32)

    def chip_land(g):
        return lax.empty((3,) + g.shape[1:], BF16)

    def finish(nm, g, from_sib, from_chips, dep=None):
        outs = _sum_adam("sum_adam_" + nm, g, from_sib, from_chips, own, *wmv[nm], dep=dep)
        big[nm] = [(o.T if nm == "w_in" else o)[None] for o in outs]

    du = _mlp_down_bwd(dr2b, w_down_f, a_act)
    g_down = _grad_w_down(hdn, dr2b)
    sa_down = _copies_start("sibling_down_start", [_plan_sibling], [g_down], [sib_land(g_down)])
    g_up = _grad_w_up(h1b, du, dep=sa_down[4])
    (g_down,), (fs_down,) = _copies_wait("sibling_down_wait", [_plan_sibling], sa_down, g_up)
    p_down = _pair_sum("pair_sum_w_down", g_down, fs_down, blocks)
    sb_down = _copies_start("chips_down_sibling_up_start", [_plan_chips, _plan_sibling], [p_down, g_up],
                            [chip_land(g_down), sib_land(g_up)])
    dr1, dr1b, g_ln1_g, g_ln1_b = _mlp_up_bwd_ln1(du, w_up_f, dr2, r1, ln1_g, dep=sb_down[4])
    dcat = _dcat(dr1b, w_out_f)
    (_, g_up), (fc_down, fs_up) = _copies_wait("chips_down_sibling_up_wait", [_plan_chips, _plan_sibling], sb_down,
                                               dcat)
    finish("w_down", g_down, fs_down, fc_down)
    p_up = _pair_sum("pair_sum_w_up", g_up, fs_up, blocks)
    sb_up = _copies_start("chips_up_start", [_plan_chips], [p_up], [chip_land(g_up)])
    do_raw, dg_g, g_norm_w = _gla_norm_bwd(dcat, o_raw, proj, gla_norm_w, dep=sb_up[4])
    dq_g, dk_g, dv_g, dgk = _gla_bwd(proj, bloc, do_raw, states)
    _, (fc_up,) = _copies_wait("chips_up_wait", [_plan_chips], sb_up, dgk)
    finish("w_up", g_up, fs_up, fc_up)
    dlo, gw2_p, g_b_gk = _gate_bwd(dgk, dgdz, proj, w2p, tri.T)
    dq_s, dk_s, dv_s, g_sinks = _swa_bwd(proj, dcat, swa_sinks[0])
    dproj = jnp.concatenate([dq_g, dk_g, dv_g, dg_g, dq_s, dk_s.astype(BF16), dv_s.astype(BF16), dlo], axis=-1)
    gw_in_t = _grad_w_in(xb, dproj)
    g_in = _from_padded_rows(gw_in_t).reshape(N_DEV, D_IN // N_DEV, D_MODEL)
    sa_in = _copies_start("sibling_in_start", [_plan_sibling], [g_in], [sib_land(g_in)])
    g_out = _grad_w_out(cat, dr1b, dep=sa_in[4])
    (g_in,), (fs_in,) = _copies_wait("sibling_in_wait", [_plan_sibling], sa_in, g_out)
    p_in = _pair_sum("pair_sum_w_in", g_in, fs_in, blocks)
    sb_in = _copies_start("chips_in_sibling_out_start", [_plan_chips, _plan_sibling], [p_in, g_out],
                          [chip_land(g_in), sib_land(g_out)])
    grad_x = _grad_x(dproj, w_in_t, dr1, dep=sb_in[4])
    (_, g_out), (fc_in, fs_out) = _copies_wait("chips_in_sibling_out_wait", [_plan_chips, _plan_sibling], sb_in,
                                               grad_x)
    p_out = _pair_sum("pair_sum_w_out", g_out, fs_out, blocks)
    sb_out = _copies_start("chips_out_start", [_plan_chips], [p_out], [chip_land(g_out)])
    finish("w_in", g_in, fs_in, fc_in, dep=sb_out[4])

    pieces = [loss_part, g_b_gk, g_norm_w, g_sinks[:, :SWA_HEADS], g_ln1_g, g_ln1_b, g_ln2_g, g_ln2_b,
              gw2_p[:GATE_RANK]]
    pack = jnp.concatenate([_rows128(p) for p in pieces], axis=0)
    tot = _all_reduce_small(pack + sb_out[4][:1, :1])
    sizes = [p.size for p in pieces]
    offs = [0]
    for p in pieces:
        offs.append(offs[-1] + _rows128(p).shape[0])
    unpack = lambda i, shape: tot[offs[i]:offs[i + 1]].reshape(-1)[:sizes[i]].reshape(shape)
    loss = tot[0, 0]
    small_names = ["b_gk", "gla_norm_w", "swa_sinks", "ln1_g", "ln1_b", "ln2_g", "ln2_b"]
    small_w = dict(b_gk=(b_gk, m_b_gk, v_b_gk), gla_norm_w=(gla_norm_w, m_gla_norm_w, v_gla_norm_w),
                   swa_sinks=(swa_sinks, m_swa_sinks, v_swa_sinks), ln1_g=(ln1_g, m_ln1_g, v_ln1_g),
                   ln1_b=(ln1_b, m_ln1_b, v_ln1_b), ln2_g=(ln2_g, m_ln2_g, v_ln2_g),
                   ln2_b=(ln2_b, m_ln2_b, v_ln2_b))
    small_g = {nm: unpack(1 + i, small_w[nm][0].shape) for i, nm in enumerate(small_names)}
    g_pack = jnp.concatenate([_rows128(small_g[nm]) for nm in small_names], axis=0)
    small_wmv = [jnp.concatenate([_rows128(small_w[nm][k]) for nm in small_names], axis=0) for k in range(3)]
    small_out = _adam_small("adam_replicated", g_pack, *small_wmv)
    srow = [0]
    for nm in small_names:
        srow.append(srow[-1] + _rows128(small_w[nm][0]).shape[0])
    small = {}
    for i, nm in enumerate(small_names):
        shape = small_w[nm][0].shape
        n = small_w[nm][0].size
        small[nm] = [small_g[nm]] + [o[srow[i]:srow[i + 1]].reshape(-1)[:n].reshape(shape) for o in small_out]

    gw2_full = unpack(8, (GATE_RANK, GLA_KW))
    gw2_loc = lax.dynamic_slice_in_dim(gw2_full, me * (GLA_KW // N_DEV), GLA_KW // N_DEV, axis=1)
    gk2_out = _adam_small("adam_w_gk2", gw2_loc, w_gk2[0], m_w_gk2[0], v_w_gk2[0])
    big["w_gk2"] = [gw2_loc[None]] + [o[None] for o in gk2_out]

    _, (fc_out,) = _copies_wait("chips_out_wait", [_plan_chips], sb_out, tot)
    finish("w_out", g_out, fs_out, fc_out)

    order = ["w_in", "w_gk2", "b_gk", "gla_norm_w", "swa_sinks", "w_out", "ln1_g", "ln1_b", "w_up", "w_down",
             "ln2_g", "ln2_b"]
    res = {**big, **small}
    outs = [loss, grad_x[None]]
    for k in range(4):
        outs += [res[nm][k] for nm in order]
    return tuple(outs)
```

```python
import functools

import jax
import jax.numpy as jnp
from jax import lax
from jax.experimental import pallas as pl
from jax.experimental.pallas import tpu as pltpu

F32 = jnp.float32
BF16 = jnp.bfloat16

N_DEV = 8
D_MODEL = 2048
D_FF = 8192
GLA_HEADS = 4
GLA_DK = 128
GLA_DV = 256
GLA_KW = 512
D_GLA = 1024
GATE_RANK = 16
GATE_TAU = 16.0
SWA_HEADS = 16
SWA_DH = 64
SWA_GROUP = 8
WINDOW = 128
D_SWA = 1024
D_IN = 4368
ALPHA = 2.0 ** 0.25
LN_EPS = 1e-5
RMS_EPS = 1e-5
ADAM_LR = 0.001
ADAM_B1 = 0.9
ADAM_B2 = 0.999
ADAM_EPS = 1e-08
ADAM_WD = 0.01
ADAM_STEP = 10

C_QG, C_KG, C_VG, C_GG, C_QS, C_KS, C_VS, C_LO = 0, 512, 1024, 2048, 3072, 4096, 4224, 4352
D_INP = 4480
LANES = 128
GC = 16
NEG = -1e30

NN = ((1,), (0,))
NT = ((1,), (1,))
TN = ((0,), (0,))

VMEM_LIMIT = 52 * 1024 * 1024
MESH = pl.DeviceIdType.MESH


def _dot(a, b, dn, precision=None):
    return lax.dot_general(a, b, (dn, ((), ())), preferred_element_type=F32, precision=precision)


def _bf16_round(v):
    return v.astype(BF16).astype(F32)


def _cparams(dims):
    return pltpu.CompilerParams(dimension_semantics=dims, vmem_limit_bytes=VMEM_LIMIT)


def _dep_operand(dep):
    if dep is None:
        return (), ()
    return (dep,), (pl.BlockSpec(dep.shape, lambda *_: (0,) * dep.ndim),)


def _to_padded_rows(w):
    pad = jnp.zeros((D_INP - D_IN,) + w.shape[1:], w.dtype)
    return jnp.concatenate([w[:3072], w[3088:], w[3072:3088], pad], axis=0)


def _matmul(name, a, b, dn, grid, a_spec, b_spec, out_shape, out_specs, acc_shape, *,
            extra=(), extra_specs=(), epilogue=None, dims=("parallel", "parallel", "arbitrary"), dep=None):
    nk = grid[2]
    n_extra = len(extra)
    deps, dep_specs = _dep_operand(dep)
    direct = epilogue is None and (nk == 1 or (not isinstance(out_shape, (list, tuple)) and out_shape.dtype == F32))

    scratch = [] if direct or nk == 1 else [pltpu.VMEM(acc_shape, F32)]

    def body(a_ref, b_ref, *rest):
        extra_refs = rest[:n_extra]
        out_refs = rest[n_extra + len(deps):len(rest) - len(scratch)]
        acc_ref = rest[-1] if scratch else out_refs[0]
        part = _dot(a_ref[...].astype(BF16), b_ref[...].astype(BF16), dn)

        def finish():
            if epilogue is None:
                out_refs[0][...] = acc_ref[...].astype(out_refs[0].dtype)
            else:
                epilogue(acc_ref, extra_refs, out_refs)

        if direct and nk == 1:
            out_refs[0][...] = part.reshape(out_refs[0].shape).astype(out_refs[0].dtype)
        elif nk == 1:
            epilogue(part, extra_refs, out_refs)
        else:
            k = pl.program_id(2)

            @pl.when(k == 0)
            def _():
                acc_ref[...] = part

            @pl.when(k > 0)
            def _():
                acc_ref[...] += part

            if not direct:
                @pl.when(k == nk - 1)
                def _():
                    finish()

    return pl.pallas_call(
        body, name=name, grid=grid,
        in_specs=[a_spec, b_spec, *extra_specs, *dep_specs],
        out_specs=out_specs, out_shape=out_shape,
        scratch_shapes=scratch,
        compiler_params=_cparams(dims),
    )(a, b, *extra, *deps)


def _row_chunks(rows, step=128):
    step = min(step, rows)
    return [(r, step) for r in range(0, rows, step)]


def _ln_stats(r):
    mu = jnp.mean(r, axis=-1, keepdims=True)
    xc = r - mu
    var = jnp.mean(xc * xc, axis=-1, keepdims=True)
    rstd = lax.rsqrt(var + LN_EPS)
    return xc * rstd, rstd


def _ln_bwd(dy_g, xhat, rstd):
    m1 = jnp.mean(dy_g, axis=-1, keepdims=True)
    m2 = jnp.mean(dy_g * xhat, axis=-1, keepdims=True)
    return rstd * (dy_g - m1 - xhat * m2)


def _proj(xb, w_in_t, dep=None):
    t = xb.shape[0]
    tm, tn = min(1024, t), 640
    return _matmul(
        "proj", xb, w_in_t, NT, (t // tm, D_INP // tn, 1),
        pl.BlockSpec((tm, D_MODEL), lambda i, j, k: (i, 0)),
        pl.BlockSpec((tn, D_MODEL), lambda i, j, k: (j, 0)),
        jax.ShapeDtypeStruct((t, D_INP), F32),
        pl.BlockSpec((tm, tn), lambda i, j, k: (i, j)),
        (tm, tn), dep=dep)


def _gate_fwd(proj, w2p, b_gk, tri):
    t = proj.shape[0]
    r = tri.shape[0]

    def body(lo_ref, w_ref, b_ref, tri_ref, bloc_ref, dgdz_ref):
        z = _dot(lo_ref[...].astype(BF16), w_ref[...], NN) + b_ref[...]
        e = jnp.exp(-jnp.abs(z))
        gk = (jnp.minimum(z, 0.0) - jnp.log1p(e)) * (1.0 / GATE_TAU)
        inv = 1.0 / (1.0 + e)
        dgdz_ref[...] = jnp.where(z >= 0.0, e * inv, inv) * (1.0 / GATE_TAU)
        bloc_ref[...] = _dot(tri_ref[...], gk, NN, precision=lax.Precision.HIGHEST)

    return pl.pallas_call(
        body, name="gate_fwd", grid=(t // r,),
        in_specs=[pl.BlockSpec((r, LANES), lambda i: (i, C_LO // LANES)),
                  pl.BlockSpec((LANES, GLA_KW), lambda i: (0, 0)),
                  pl.BlockSpec((1, GLA_KW), lambda i: (0, 0)),
                  pl.BlockSpec((r, r), lambda i: (0, 0))],
        out_specs=[pl.BlockSpec((r, GLA_KW), lambda i: (i, 0)),
                   pl.BlockSpec((r, GLA_KW), lambda i: (i, 0))],
        out_shape=[jax.ShapeDtypeStruct((t, GLA_KW), F32), jax.ShapeDtypeStruct((t, GLA_KW), F32)],
        compiler_params=_cparams(("parallel",)),
    )(proj, w2p, b_gk, tri)


def _gla_fwd(proj, bloc):
    t = proj.shape[0]
    r = min(256, t)
    ncb = r // GC
    scale = GLA_DK ** -0.5

    def body(q_ref, k_ref, v_ref, b_ref, o_ref, st_ref, s_scr, m_scr):
        @pl.when(pl.program_id(1) == 0)
        def _():
            s_scr[...] = jnp.zeros_like(s_scr)

        rows = lax.broadcasted_iota(jnp.int32, (GC, 1), 0)
        cols = lax.broadcasted_iota(jnp.int32, (1, GC), 1)

        def increment(c, carry):
            r0 = pl.multiple_of(c * GC, GC)
            k = k_ref[pl.ds(r0, GC), :]
            b = b_ref[pl.ds(r0, GC), :]
            kd = k * jnp.exp(b[GC - 1:GC, :] - b)
            m_scr[c] = _dot(v_ref[pl.ds(r0, GC), :].astype(BF16), kd.astype(BF16), TN)
            return carry

        lax.fori_loop(0, ncb, increment, 0, unroll=8)

        def recur(c, st):
            st_ref[c] = st
            r0 = pl.multiple_of(c * GC, GC)
            bl = b_ref[pl.ds(r0, GC), :][GC - 1:GC, :]
            return st * jnp.exp(bl) + m_scr[c]

        s_scr[...] = lax.fori_loop(0, ncb, recur, s_scr[...])

        def output(c, carry):
            r0 = pl.multiple_of(c * GC, GC)
            q = q_ref[pl.ds(r0, GC), :] * scale
            b = b_ref[pl.ds(r0, GC), :]
            kr = _bf16_round(k_ref[pl.ds(r0, GC), :])
            att = jnp.zeros((GC, GC), F32)
            for j in range(GC):
                w = jnp.exp(jnp.where(rows >= j, b - b[j:j + 1, :], NEG))
                a = jnp.sum(_bf16_round(q * w) * kr[j:j + 1, :], axis=-1, keepdims=True)
                att = jnp.where(cols == j, a, att)
            o_ref[pl.ds(r0, GC), :] = (_dot((q * jnp.exp(b)).astype(BF16), st_ref[c].astype(BF16), NT)
                                       + _dot(att.astype(BF16), v_ref[pl.ds(r0, GC), :].astype(BF16), NN))
            return carry

        lax.fori_loop(0, ncb, output, 0, unroll=8)

    return pl.pallas_call(
        body, name="gla_fwd", grid=(GLA_HEADS, t // r),
        in_specs=[pl.BlockSpec((r, GLA_DK), lambda h, i: (i, C_QG // GLA_DK + h)),
                  pl.BlockSpec((r, GLA_DK), lambda h, i: (i, C_KG // GLA_DK + h)),
                  pl.BlockSpec((r, GLA_DV), lambda h, i: (i, C_VG // GLA_DV + h)),
                  pl.BlockSpec((r, GLA_DK), lambda h, i: (i, h))],
        out_specs=[pl.BlockSpec((r, GLA_DV), lambda h, i: (i, h)),
                   pl.BlockSpec((ncb, None, GLA_DV, GLA_DK), lambda h, i: (i, h, 0, 0))],
        out_shape=[jax.ShapeDtypeStruct((t, D_GLA), F32),
                   jax.ShapeDtypeStruct((t // GC, GLA_HEADS, GLA_DV, GLA_DK), F32)],
        scratch_shapes=[pltpu.VMEM((GLA_DV, GLA_DK), F32), pltpu.VMEM((ncb, GLA_DV, GLA_DK), F32)],
        compiler_params=_cparams(("parallel", "arbitrary")),
    )(proj, proj, proj, bloc)


def _gla_norm_fwd(o_raw, proj, norm_w):
    t = o_raw.shape[0]
    r = min(512, t)

    def body(o_ref, g_ref, w_ref, out_ref):
        w = w_ref[...]
        for h in range(GLA_HEADS):
            sl = slice(h * GLA_DV, (h + 1) * GLA_DV)
            o = o_ref[:, sl]
            g = g_ref[:, sl]
            on = o * lax.rsqrt(jnp.mean(o * o, axis=-1, keepdims=True) + RMS_EPS)
            out_ref[:, sl] = (on * w * (g * jax.nn.sigmoid(g))).astype(BF16)

    return pl.pallas_call(
        body, name="gla_norm_fwd", grid=(t // r,),
        in_specs=[pl.BlockSpec((r, D_GLA), lambda i: (i, 0)),
                  pl.BlockSpec((r, D_GLA), lambda i: (i, C_GG // D_GLA)),
                  pl.BlockSpec((1, GLA_DV), lambda i: (0, 0))],
        out_specs=pl.BlockSpec((r, D_GLA), lambda i: (i, 0)),
        out_shape=jax.ShapeDtypeStruct((t, D_GLA), BF16),
        compiler_params=_cparams(("parallel",)),
    )(o_raw, proj, norm_w)


def _swa_masks(i, heads):
    qi = lax.broadcasted_iota(jnp.int32, (heads * WINDOW, 1), 0) & (WINDOW - 1)
    kj = lax.broadcasted_iota(jnp.int32, (1, 2 * WINDOW), 1)
    valid = (kj > qi) & (kj <= qi + WINDOW) & ((i > 0) | (kj >= WINDOW))
    lo = lax.broadcasted_iota(jnp.int32, (1, LANES), 1) < SWA_DH
    return valid, lo


def _dup_half(x, lo, kv):
    xr = pltpu.roll(x, SWA_DH, axis=1)
    return jnp.where(lo, x, xr) if kv == 0 else jnp.where(lo, xr, x)


def _swa_stack(ref, col0, lo, kv):
    parts = []
    for p in range(4):
        c0 = col0 + LANES * (4 * kv + p)
        xp = ref[:, c0:c0 + LANES]
        parts += [jnp.where(lo, xp, 0.0).astype(BF16), jnp.where(lo, 0.0, xp).astype(BF16)]
    return jnp.concatenate(parts, axis=0)


def _swa_unstack(x_all, lo, p):
    r0 = 2 * p * WINDOW
    return jnp.where(lo, x_all[r0:r0 + WINDOW, :], x_all[r0 + WINDOW:r0 + 2 * WINDOW, :])


def _swa_sinks(sink_ref, kv):
    return jnp.concatenate([jnp.full((WINDOW, 1), sink_ref[SWA_GROUP * kv + h], F32) for h in range(SWA_GROUP)],
                           axis=0)


def _swa_probs(qm, kdup, valid, sink):
    s = _dot(qm, kdup, NT) * (SWA_DH ** -0.5)
    s = jnp.where(valid, s, NEG)
    m = jnp.maximum(jnp.max(s, axis=-1, keepdims=True), sink)
    p = jnp.exp(s - m)
    es = jnp.exp(sink - m)
    inv = 1.0 / (jnp.sum(p, axis=-1, keepdims=True) + es)
    return p * inv, es * inv


def _swa_fwd(proj, sinks, gla_out, dep=None):
    t = proj.shape[0]

    deps, dep_specs = _dep_operand(dep)

    def body(sink_ref, q_ref, kp_ref, kc_ref, vp_ref, vc_ref, gla_ref, *rest):
        o_ref = rest[-1]
        i = pl.program_id(0)
        o_ref[:, :D_GLA] = gla_ref[...]
        valid, lo = _swa_masks(i, 1)
        kb = jnp.concatenate([kp_ref[...], kc_ref[...]], axis=0)
        vb = jnp.concatenate([vp_ref[...], vc_ref[...]], axis=0)
        for kv in range(2):
            kdup = _dup_half(kb, lo, kv).astype(BF16)
            vdup = _dup_half(vb, lo, kv).astype(BF16)
            for p in range(4):
                c0 = LANES * (4 * kv + p)
                qp = q_ref[:, c0:c0 + LANES]
                halves = []
                for e in range(2):
                    qm = jnp.where(lo if e == 0 else ~lo, qp, 0.0).astype(BF16)
                    pn, _ = _swa_probs(qm, kdup, valid, sink_ref[SWA_GROUP * kv + 2 * p + e])
                    halves.append(_dot(pn.astype(BF16), vdup, NN))
                o_ref[:, D_GLA + c0:D_GLA + c0 + LANES] = jnp.where(lo, halves[0], halves[1]).astype(BF16)

    kvspec = lambda col, prev: pl.BlockSpec(
        (WINDOW, LANES), (lambda i: (jnp.maximum(i - 1, 0), col)) if prev else (lambda i: (i, col)))
    return pl.pallas_call(
        body, name="swa_fwd", grid=(t // WINDOW,),
        in_specs=[pl.BlockSpec(memory_space=pltpu.SMEM),
                  pl.BlockSpec((WINDOW, D_SWA), lambda i: (i, C_QS // D_SWA)),
                  kvspec(C_KS // LANES, True), kvspec(C_KS // LANES, False),
                  kvspec(C_VS // LANES, True), kvspec(C_VS // LANES, False),
                  pl.BlockSpec((WINDOW, D_GLA), lambda i: (i, 0)), *dep_specs],
        out_specs=pl.BlockSpec((WINDOW, D_MODEL), lambda i: (i, 0)),
        out_shape=jax.ShapeDtypeStruct((t, D_MODEL), BF16),
        compiler_params=_cparams(("parallel",)),
    )(sinks, proj, proj, proj, proj, proj, gla_out, *deps)


def _mix_ln1(cat, w_out, x, ln_g, ln_b, dep=None):
    t = cat.shape[0]
    tm, tk = min(256, t), D_MODEL

    def epilogue(acc_ref, extra, outs):
        x_ref, g_ref, b_ref = extra
        r1_ref, h1_ref, h1b_ref = outs
        for r0, n in _row_chunks(tm):
            rs = slice(r0, r0 + n)
            r1 = ALPHA * x_ref[rs, :] + acc_ref[rs, :]
            xhat, _ = _ln_stats(r1)
            h = xhat * g_ref[...] + b_ref[...]
            r1_ref[rs, :] = r1
            h1_ref[rs, :] = h
            h1b_ref[rs, :] = h.astype(BF16)

    row = pl.BlockSpec((tm, D_MODEL), lambda i, j, k: (i, 0))
    vec = pl.BlockSpec((1, D_MODEL), lambda i, j, k: (0, 0))
    return _matmul(
        "mix_ln1", cat, w_out, NN, (t // tm, 1, D_MODEL // tk),
        pl.BlockSpec((tm, tk), lambda i, j, k: (i, k)),
        pl.BlockSpec((tk, D_MODEL), lambda i, j, k: (k, 0)),
        [jax.ShapeDtypeStruct((t, D_MODEL), F32), jax.ShapeDtypeStruct((t, D_MODEL), F32),
         jax.ShapeDtypeStruct((t, D_MODEL), BF16)],
        [row, row, row], (tm, D_MODEL),
        extra=(x, ln_g, ln_b), extra_specs=(row, vec, vec), epilogue=epilogue, dep=dep)


def _mlp_up(h1b, w_up):
    t = h1b.shape[0]
    tm, tn = min(1024, t), 1024

    def epilogue(acc_ref, extra, outs):
        a_ref, hdn_ref = outs
        for r0, n in _row_chunks(tm, 256):
            rs = slice(r0, r0 + n)
            a = jnp.maximum(acc_ref[rs, :], 0.0)
            a_ref[rs, :] = a.astype(BF16)
            hdn_ref[rs, :] = (a * a).astype(BF16)

    out = pl.BlockSpec((tm, tn), lambda i, j, k: (i, j))
    return _matmul(
        "mlp_up", h1b, w_up, NN, (t // tm, D_FF // tn, 1),
        pl.BlockSpec((tm, D_MODEL), lambda i, j, k: (i, 0)),
        pl.BlockSpec((None, D_MODEL, tn), lambda i, j, k: (j, 0, 0)),
        [jax.ShapeDtypeStruct((t, D_FF), BF16), jax.ShapeDtypeStruct((t, D_FF), BF16)],
        [out, out], (tm, tn), epilogue=epilogue)


def _mlp_down_loss(hdn, w_down, h1, target, ln_g, ln_b):
    t = hdn.shape[0]
    tm, tn, tk = min(1024, t), 1024, 4096
    ff = _matmul(
        "mlp_down", hdn, w_down, NN, (t // tm, D_MODEL // tn, D_FF // tk),
        pl.BlockSpec((tm, tk), lambda i, j, k: (i, k)),
        pl.BlockSpec((tk, tn), lambda i, j, k: (k, j)),
        jax.ShapeDtypeStruct((t, D_MODEL), F32),
        pl.BlockSpec((tm, tn), lambda i, j, k: (i, j)), (tm, tn))
    r = min(256, t)

    def body(ff_ref, h1_ref, t_ref, g_ref, b_ref, dr2_ref, dr2b_ref, gg_ref, gb_ref, loss_ref):
        @pl.when(pl.program_id(0) == 0)
        def _():
            gg_ref[...] = jnp.zeros_like(gg_ref)
            gb_ref[...] = jnp.zeros_like(gb_ref)
            loss_ref[...] = jnp.zeros_like(loss_ref)

        for r0, n in _row_chunks(r, 64):
            rs = slice(r0, r0 + n)
            xhat, rstd = _ln_stats(ALPHA * h1_ref[rs, :] + ff_ref[rs, :])
            err = xhat * g_ref[...] + b_ref[...] - t_ref[rs, :]
            loss_ref[...] += 0.5 * jnp.sum(jnp.mean(err * err, axis=-1, keepdims=True))
            dy = err * (1.0 / D_MODEL)
            gg_ref[...] += jnp.sum(dy * xhat, axis=0, keepdims=True)
            gb_ref[...] += jnp.sum(dy, axis=0, keepdims=True)
            dr2 = _ln_bwd(dy * g_ref[...], xhat, rstd)
            dr2_ref[rs, :] = dr2
            dr2b_ref[rs, :] = dr2.astype(BF16)

    row = pl.BlockSpec((r, D_MODEL), lambda i: (i, 0))
    vec = pl.BlockSpec((1, D_MODEL), lambda i: (0, 0))
    return pl.pallas_call(
        body, name="ln2_loss", grid=(t // r,),
        in_specs=[row, row, row, vec, vec],
        out_specs=[row, row, vec, vec, pl.BlockSpec((1, LANES), lambda i: (0, 0))],
        out_shape=[jax.ShapeDtypeStruct((t, D_MODEL), F32), jax.ShapeDtypeStruct((t, D_MODEL), BF16),
                   jax.ShapeDtypeStruct((1, D_MODEL), F32), jax.ShapeDtypeStruct((1, D_MODEL), F32),
                   jax.ShapeDtypeStruct((1, LANES), F32)],
        compiler_params=_cparams(("arbitrary",)),
    )(ff, h1, target, ln_g, ln_b)


def _mlp_down_bwd(dr2b, w_down, a_act):
    t = dr2b.shape[0]
    tm, tn = min(1024, t), 1024

    def epilogue(acc_ref, extra, outs):
        (a_ref,) = extra
        for r0, n in _row_chunks(tm, 256):
            rs = slice(r0, r0 + n)
            outs[0][rs, :] = (acc_ref[rs, :] * (2.0 * a_ref[rs, :].astype(F32))).astype(BF16)

    blk = pl.BlockSpec((tm, tn), lambda i, j, k: (i, j))
    return _matmul(
        "mlp_down_bwd", dr2b, w_down, NT, (t // tm, D_FF // tn, 1),
        pl.BlockSpec((tm, D_MODEL), lambda i, j, k: (i, 0)),
        pl.BlockSpec((tn, D_MODEL), lambda i, j, k: (j, 0)),
        jax.ShapeDtypeStruct((t, D_FF), BF16), blk, (tm, tn),
        extra=(a_act,), extra_specs=(blk,), epilogue=epilogue)


def _grad_w_down(hdn, dr2b):
    t = hdn.shape[0]
    tm, tn = 1024, 1024
    return _matmul(
        "grad_w_down", hdn, dr2b, TN, (D_MODEL // tn, D_FF // tm, 1),
        pl.BlockSpec((t, tm), lambda j, i, k: (0, i)),
        pl.BlockSpec((t, tn), lambda j, i, k: (0, j)),
        jax.ShapeDtypeStruct((N_DEV, D_FF // N_DEV, D_MODEL), F32),
        pl.BlockSpec((None, tm, tn), lambda j, i, k: (i, 0, j)), (tm, tn))


def _grad_w_up(h1b, du, dep=None):
    t = h1b.shape[0]
    tm, tn = 1024, 1024
    return _matmul(
        "grad_w_up", h1b, du, TN, (N_DEV, D_MODEL // tm, 1),
        pl.BlockSpec((t, tm), lambda i, j, k: (0, j)),
        pl.BlockSpec((t, tn), lambda i, j, k: (0, i)),
        jax.ShapeDtypeStruct((N_DEV, D_MODEL, D_FF // N_DEV), F32),
        pl.BlockSpec((None, tm, tn), lambda i, j, k: (i, j, 0)), (tm, tn), dep=dep)


def _mlp_up_bwd_ln1(du, w_up, dr2, r1, ln_g, dep=None):
    t = du.shape[0]
    tm, tn, tk, nb = min(1024, t), 1024, D_FF // N_DEV, 4
    deps, dep_specs = _dep_operand(dep)

    def mm_body(a_ref, b_ref, *rest):
        o_ref = rest[-1]
        k = pl.program_id(2)
        part = _dot(a_ref[:, :tk], b_ref[0], NT)
        for d in range(1, nb):
            part = part + _dot(a_ref[:, d * tk:(d + 1) * tk], b_ref[d], NT)

        @pl.when(k == 0)
        def _():
            o_ref[...] = part

        @pl.when(k > 0)
        def _():
            o_ref[...] += part

    dff = pl.pallas_call(
        mm_body, name="mlp_up_bwd", grid=(t // tm, D_MODEL // tn, N_DEV // nb),
        in_specs=[pl.BlockSpec((tm, nb * tk), lambda i, j, k: (i, k)),
                  pl.BlockSpec((nb, tn, tk), lambda i, j, k: (k, j, 0)), *dep_specs],
        out_specs=pl.BlockSpec((tm, tn), lambda i, j, k: (i, j)),
        out_shape=jax.ShapeDtypeStruct((t, D_MODEL), F32),
        compiler_params=_cparams(("parallel", "parallel", "arbitrary")),
    )(du, w_up, *deps)
    r = min(256, t)

    def body(acc_ref, dr2_ref, r1_ref, g_ref, dr1_ref, dr1b_ref, gg_ref, gb_ref):
        @pl.when(pl.program_id(0) == 0)
        def _():
            gg_ref[...] = jnp.zeros_like(gg_ref)
            gb_ref[...] = jnp.zeros_like(gb_ref)

        for r0, n in _row_chunks(r, 64):
            rs = slice(r0, r0 + n)
            dh1 = ALPHA * dr2_ref[rs, :] + acc_ref[rs, :]
            xhat, rstd = _ln_stats(r1_ref[rs, :])
            gg_ref[...] += jnp.sum(dh1 * xhat, axis=0, keepdims=True)
            gb_ref[...] += jnp.sum(dh1, axis=0, keepdims=True)
            dr1 = _ln_bwd(dh1 * g_ref[...], xhat, rstd)
            dr1_ref[rs, :] = dr1
            dr1b_ref[rs, :] = dr1.astype(BF16)

    row = pl.BlockSpec((r, D_MODEL), lambda i: (i, 0))
    vec = pl.BlockSpec((1, D_MODEL), lambda i: (0, 0))
    return pl.pallas_call(
        body, name="ln1_bwd", grid=(t // r,),
        in_specs=[row, row, row, vec],
        out_specs=[row, row, vec, vec],
        out_shape=[jax.ShapeDtypeStruct((t, D_MODEL), F32), jax.ShapeDtypeStruct((t, D_MODEL), BF16),
                   jax.ShapeDtypeStruct((1, D_MODEL), F32), jax.ShapeDtypeStruct((1, D_MODEL), F32)],
        compiler_params=_cparams(("arbitrary",)),
    )(dff, dr2, r1, ln_g)


def _dcat(dr1b, w_out):
    t = dr1b.shape[0]
    tm, tn = min(1024, t), 1024
    return _matmul(
        "dcat", dr1b, w_out, NT, (t // tm, D_MODEL // tn, 1),
        pl.BlockSpec((tm, D_MODEL), lambda i, j, k: (i, 0)),
        pl.BlockSpec((tn, D_MODEL), lambda i, j, k: (j, 0)),
        jax.ShapeDtypeStruct((t, D_MODEL), F32),
        pl.BlockSpec((tm, tn), lambda i, j, k: (i, j)), (tm, tn))


def _grad_w_out(cat, dr1b, dep=None):
    t = cat.shape[0]
    tm, tn = 1024, 1024
    return _matmul(
        "grad_w_out", cat, dr1b, TN, (D_MODEL // tm, D_MODEL // tn, 1),
        pl.BlockSpec((t, tm), lambda i, j, k: (0, i)),
        pl.BlockSpec((t, tn), lambda i, j, k: (0, j)),
        jax.ShapeDtypeStruct((N_DEV, D_MODEL // N_DEV, D_MODEL), F32),
        pl.BlockSpec((tm // (D_MODEL // N_DEV), D_MODEL // N_DEV, tn), lambda i, j, k: (i, 0, j)), (tm, tn), dep=dep)


def _gla_norm_bwd(dcat, o_raw, proj, norm_w, dep=None):
    t = o_raw.shape[0]
    r = min(512, t)

    deps, dep_specs = _dep_operand(dep)

    def body(d_ref, o_ref, g_ref, w_ref, *rest):
        do_ref, dg_ref, dw_ref = rest[len(deps):]

        @pl.when(pl.program_id(0) == 0)
        def _():
            dw_ref[...] = jnp.zeros_like(dw_ref)

        w = w_ref[...]
        dw = jnp.zeros((1, GLA_DV), F32)
        for h in range(GLA_HEADS):
            sl = slice(h * GLA_DV, (h + 1) * GLA_DV)
            o = o_ref[:, sl]
            g = g_ref[:, sl]
            d = d_ref[:, sl]
            rr = lax.rsqrt(jnp.mean(o * o, axis=-1, keepdims=True) + RMS_EPS)
            on = o * rr
            sg = jax.nn.sigmoid(g)
            sil = g * sg
            dg_ref[:, sl] = (d * on * w * (sg * (1.0 + g * (1.0 - sg)))).astype(BF16)
            dw = dw + jnp.sum(d * on * sil, axis=0, keepdims=True)
            don = d * w * sil
            do_ref[:, sl] = rr * (don - on * jnp.mean(don * on, axis=-1, keepdims=True))
        dw_ref[...] += dw

    return pl.pallas_call(
        body, name="gla_norm_bwd", grid=(t // r,),
        in_specs=[pl.BlockSpec((r, D_GLA), lambda i: (i, 0)),
                  pl.BlockSpec((r, D_GLA), lambda i: (i, 0)),
                  pl.BlockSpec((r, D_GLA), lambda i: (i, C_GG // D_GLA)),
                  pl.BlockSpec((1, GLA_DV), lambda i: (0, 0)), *dep_specs],
        out_specs=[pl.BlockSpec((r, D_GLA), lambda i: (i, 0)),
                   pl.BlockSpec((r, D_GLA), lambda i: (i, 0)),
                   pl.BlockSpec((1, GLA_DV), lambda i: (0, 0))],
        out_shape=[jax.ShapeDtypeStruct((t, D_GLA), F32), jax.ShapeDtypeStruct((t, D_GLA), BF16),
                   jax.ShapeDtypeStruct((1, GLA_DV), F32)],
        compiler_params=_cparams(("arbitrary",)),
    )(dcat, o_raw, proj, norm_w, *deps)


def _gla_bwd(proj, bloc, do_raw, states):
    t = proj.shape[0]
    r = min(256, t)
    ncb = r // GC
    nb = t // r
    scale = GLA_DK ** -0.5

    def body(q_ref, k_ref, v_ref, b_ref, do_ref, st_ref, dq_ref, dk_ref, dv_ref, db_ref, ds_scr):
        @pl.when(pl.program_id(1) == 0)
        def _():
            ds_scr[...] = jnp.zeros_like(ds_scr)

        rows = lax.broadcasted_iota(jnp.int32, (GC, 1), 0)

        def chunk(cc, carry):
            c = ncb - 1 - cc
            r0 = pl.multiple_of(c * GC, GC)
            q = q_ref[pl.ds(r0, GC), :] * scale
            k = k_ref[pl.ds(r0, GC), :]
            v = v_ref[pl.ds(r0, GC), :]
            b = b_ref[pl.ds(r0, GC), :]
            do = do_ref[pl.ds(r0, GC), :]
            st = st_ref[c]
            dsn = ds_scr[...]
            bl = b[GC - 1:GC, :]
            eb = jnp.exp(b)
            ekl = jnp.exp(bl - b)
            ebl = jnp.exp(bl)
            qh = q * eb
            kd = k * ekl
            dob = do.astype(BF16)
            dsb = dsn.astype(BF16)
            dqh = _dot(dob, st.astype(BF16), NN)
            dkd = _dot(v.astype(BF16), dsb, NN)
            dv = _dot(kd.astype(BF16), dsb, NT)
            dq_i = jnp.zeros((GC, GLA_DK), F32)
            dk_i = jnp.zeros((GC, GLA_DK), F32)
            dk_x = jnp.zeros((GC, GLA_DK), F32)
            kr = _bf16_round(k)
            vr = _bf16_round(v)
            dor = _bf16_round(do)
            for i in range(GC):
                w = jnp.exp(jnp.where(rows <= i, b[i:i + 1, :] - b, NEG))
                qw = q[i:i + 1, :] * w
                qwr = _bf16_round(qw)
                a = _bf16_round(jnp.sum(qwr * kr, axis=-1, keepdims=True))
                da = jnp.sum(vr * dor[i:i + 1, :], axis=-1, keepdims=True)
                dv = dv + a * dor[i:i + 1, :]
                dk_x = dk_x + da * qw
                dk_i = dk_i + _bf16_round(da) * qwr
                dq_i = jnp.where(rows == i, jnp.sum(da * (w * k), axis=0, keepdims=True), dq_i)
            dqs = dqh * eb + dq_i
            dk = dkd * ekl + dk_i
            db_last = jnp.sum(dkd * kd, axis=0, keepdims=True) + ebl * jnp.sum(dsn * st, axis=0, keepdims=True)
            db = q * dqs - k * (dkd * ekl + dk_x) + jnp.where(rows == GC - 1, db_last, 0.0)
            dq_ref[pl.ds(r0, GC), :] = (dqs * scale).astype(BF16)
            dk_ref[pl.ds(r0, GC), :] = dk.astype(BF16)
            dv_ref[pl.ds(r0, GC), :] = dv.astype(BF16)
            db_ref[pl.ds(r0, GC), :] = db
            ds_scr[...] = dsn * ebl + _dot(dob, qh.astype(BF16), TN)
            return carry

        lax.fori_loop(0, ncb, chunk, 0)

    rev = lambda i: nb - 1 - i
    return pl.pallas_call(
        body, name="gla_bwd", grid=(GLA_HEADS, nb),
        in_specs=[pl.BlockSpec((r, GLA_DK), lambda h, i: (rev(i), C_QG // GLA_DK + h)),
                  pl.BlockSpec((r, GLA_DK), lambda h, i: (rev(i), C_KG // GLA_DK + h)),
                  pl.BlockSpec((r, GLA_DV), lambda h, i: (rev(i), C_VG // GLA_DV + h)),
                  pl.BlockSpec((r, GLA_DK), lambda h, i: (rev(i), h)),
                  pl.BlockSpec((r, GLA_DV), lambda h, i: (rev(i), h)),
                  pl.BlockSpec((ncb, None, GLA_DV, GLA_DK), lambda h, i: (rev(i), h, 0, 0))],
        out_specs=[pl.BlockSpec((r, GLA_DK), lambda h, i: (rev(i), h)),
                   pl.BlockSpec((r, GLA_DK), lambda h, i: (rev(i), h)),
                   pl.BlockSpec((r, GLA_DV), lambda h, i: (rev(i), h)),
                   pl.BlockSpec((r, GLA_DK), lambda h, i: (rev(i), h))],
        out_shape=[jax.ShapeDtypeStruct((t, GLA_KW), BF16), jax.ShapeDtypeStruct((t, GLA_KW), BF16),
                   jax.ShapeDtypeStruct((t, D_GLA), BF16), jax.ShapeDtypeStruct((t, GLA_KW), F32)],
        scratch_shapes=[pltpu.VMEM((GLA_DV, GLA_DK), F32)],
        compiler_params=_cparams(("parallel", "arbitrary")),
    )(proj, proj, proj, bloc, do_raw, states)


def _gate_bwd(db, dgdz, proj, w2p, triu):
    t = db.shape[0]
    r = triu.shape[0]

    def body(db_ref, s_ref, lo_ref, w_ref, u_ref, dlo_ref, gw_ref, gb_ref):
        @pl.when(pl.program_id(0) == 0)
        def _():
            gw_ref[...] = jnp.zeros_like(gw_ref)
            gb_ref[...] = jnp.zeros_like(gb_ref)

        dz = _dot(u_ref[...], db_ref[...], NN, precision=lax.Precision.HIGHEST) * s_ref[...]
        dzb = dz.astype(BF16)
        gb_ref[...] += jnp.sum(dz, axis=0, keepdims=True)
        gw_ref[...] += _dot(lo_ref[...].astype(BF16), dzb, TN)
        dlo_ref[...] = _dot(dzb, w_ref[...], NT).astype(BF16)

    return pl.pallas_call(
        body, name="gate_bwd", grid=(t // r,),
        in_specs=[pl.BlockSpec((r, GLA_KW), lambda i: (i, 0)),
                  pl.BlockSpec((r, GLA_KW), lambda i: (i, 0)),
                  pl.BlockSpec((r, LANES), lambda i: (i, C_LO // LANES)),
                  pl.BlockSpec((LANES, GLA_KW), lambda i: (0, 0)),
                  pl.BlockSpec((r, r), lambda i: (0, 0))],
        out_specs=[pl.BlockSpec((r, LANES), lambda i: (i, 0)),
                   pl.BlockSpec((LANES, GLA_KW), lambda i: (0, 0)),
                   pl.BlockSpec((1, GLA_KW), lambda i: (0, 0))],
        out_shape=[jax.ShapeDtypeStruct((t, LANES), BF16), jax.ShapeDtypeStruct((LANES, GLA_KW), F32),
                   jax.ShapeDtypeStruct((1, GLA_KW), F32)],
        compiler_params=_cparams(("arbitrary",)),
    )(db, dgdz, proj, w2p, triu)


def _swa_bwd(proj, dcat, sinks, dep=None):
    t = proj.shape[0]

    deps, dep_specs = _dep_operand(dep)

    def body(sink_ref, q_ref, kp_ref, kc_ref, vp_ref, vc_ref, d_ref, *rest):
        dq_ref, dk_ref, dv_ref, dsink_ref = rest[len(deps):]
        i = pl.program_id(0)

        @pl.when(i == 0)
        def _():
            dk_ref[...] = jnp.zeros_like(dk_ref)
            dv_ref[...] = jnp.zeros_like(dv_ref)
            dsink_ref[...] = jnp.zeros_like(dsink_ref)

        valid, lo = _swa_masks(i, SWA_GROUP)
        lane = lax.broadcasted_iota(jnp.int32, (1, LANES), 1)
        kb = jnp.concatenate([kp_ref[...], kc_ref[...]], axis=0)
        vb = jnp.concatenate([vp_ref[...], vc_ref[...]], axis=0)
        dsink = jnp.zeros((1, LANES), F32)
        folded_k, folded_v = [], []
        for kv in range(2):
            kdup = _dup_half(kb, lo, kv).astype(BF16)
            vdup = _dup_half(vb, lo, kv).astype(BF16)
            qm = _swa_stack(q_ref, 0, lo, kv)
            dom = _swa_stack(d_ref, 0, lo, kv)
            pn, psink = _swa_probs(qm, kdup, valid, _swa_sinks(sink_ref, kv))
            dpr = _dot(dom, vdup, NT)
            drow = jnp.sum(dpr * pn, axis=-1, keepdims=True)
            ds_col = psink * drow
            for h in range(SWA_GROUP):
                dsink = dsink + jnp.where(lane == SWA_GROUP * kv + h,
                                          -jnp.sum(ds_col[h * WINDOW:(h + 1) * WINDOW, :]), 0.0)
            dsb = (pn * (dpr - drow) * (SWA_DH ** -0.5)).astype(BF16)
            dq_all = _dot(dsb, kdup, NN)
            for p in range(4):
                c0 = LANES * (4 * kv + p)
                dq_ref[:, c0:c0 + LANES] = _swa_unstack(dq_all, lo, p).astype(BF16)
            dkd = _dot(dsb, qm, TN)
            dvd = _dot(pn.astype(BF16), dom, TN)
            folded_k.append(dkd + pltpu.roll(dkd, SWA_DH, axis=1))
            folded_v.append(dvd + pltpu.roll(dvd, SWA_DH, axis=1))
        dkb = jnp.where(lo, folded_k[0], folded_k[1])
        dvb = jnp.where(lo, folded_v[0], folded_v[1])
        dsink_ref[...] += dsink
        cur = pl.ds(pl.multiple_of(i * WINDOW, WINDOW), WINDOW)
        dk_ref[cur, :] += dkb[WINDOW:, :]
        dv_ref[cur, :] += dvb[WINDOW:, :]

        @pl.when(i > 0)
        def _():
            prev = pl.ds(pl.multiple_of((i - 1) * WINDOW, WINDOW), WINDOW)
            dk_ref[prev, :] += dkb[:WINDOW, :]
            dv_ref[prev, :] += dvb[:WINDOW, :]

    kvspec = lambda col, prev: pl.BlockSpec(
        (WINDOW, LANES), (lambda i: (jnp.maximum(i - 1, 0), col)) if prev else (lambda i: (i, col)))
    full = pl.BlockSpec((t, LANES), lambda i: (0, 0))
    return pl.pallas_call(
        body, name="swa_bwd", grid=(t // WINDOW,),
        in_specs=[pl.BlockSpec(memory_space=pltpu.SMEM),
                  pl.BlockSpec((WINDOW, D_SWA), lambda i: (i, C_QS // D_SWA)),
                  kvspec(C_KS // LANES, True), kvspec(C_KS // LANES, False),
                  kvspec(C_VS // LANES, True), kvspec(C_VS // LANES, False),
                  pl.BlockSpec((WINDOW, D_SWA), lambda i: (i, 1)), *dep_specs],
        out_specs=[pl.BlockSpec((WINDOW, D_SWA), lambda i: (i, 0)), full, full,
                   pl.BlockSpec((1, LANES), lambda i: (0, 0))],
        out_shape=[jax.ShapeDtypeStruct((t, D_SWA), BF16), jax.ShapeDtypeStruct((t, LANES), F32),
                   jax.ShapeDtypeStruct((t, LANES), F32), jax.ShapeDtypeStruct((1, LANES), F32)],
        compiler_params=_cparams(("arbitrary",)),
    )(sinks, proj, proj, proj, proj, proj, dcat, *deps)


def _grad_w_in(xb, dproj):
    t = xb.shape[0]
    tm, tn = 640, 1024
    return _matmul(
        "grad_w_in", dproj, xb, TN, (D_INP // tm, D_MODEL // tn, 1),
        pl.BlockSpec((t, tm), lambda i, j, k: (0, i)),
        pl.BlockSpec((t, tn), lambda i, j, k: (0, j)),
        jax.ShapeDtypeStruct((D_INP, D_MODEL), F32),
        pl.BlockSpec((tm, tn), lambda i, j, k: (i, j)), (tm, tn))


def _grad_x(dproj, w_in_t, dr1, dep=None):
    t = dproj.shape[0]
    tm, tn = min(512, t), 1024

    def epilogue(acc_ref, extra, outs):
        for r0, n in _row_chunks(tm):
            rs = slice(r0, r0 + n)
            outs[0][rs, :] = ALPHA * extra[0][rs, :] + acc_ref[rs, :]

    blk = pl.BlockSpec((tm, tn), lambda j, i, k: (i, j))
    return _matmul(
        "grad_x", dproj, w_in_t, NN, (D_MODEL // tn, t // tm, 1),
        pl.BlockSpec((tm, D_INP), lambda j, i, k: (i, 0)),
        pl.BlockSpec((D_INP, tn), lambda j, i, k: (0, j)),
        jax.ShapeDtypeStruct((t, D_MODEL), F32), blk, (tm, tn),
        extra=(dr1,), extra_specs=(blk,), epilogue=epilogue, dep=dep)


def _place():
    x, y, c = lax.axis_index("x"), lax.axis_index("y"), lax.axis_index("c")
    chips = [(1 - x, y), (x, 1 - y), (1 - x, 1 - y)]
    return x, y, c, chips


def _plan_gather_out(src, land, x, y, c, chips):
    me = 4 * x + 2 * y + c
    return [(src, land.at[me], to) for to in [(x, y, 1 - c)] + [(px, py, c) for px, py in chips]]


def _plan_gather_forward(src, land, x, y, c, chips):
    return [(land.at[4 * px + 2 * py + c], land.at[4 * px + 2 * py + c], (x, y, 1 - c)) for px, py in chips]


def _plan_sibling(src, land, x, y, c, chips):
    return [(src.at[2 * q + (1 - c)], land.at[q], (x, y, 1 - c)) for q in range(4)]


def _plan_chips(src, land, x, y, c, chips):
    return [(src.at[j], land.at[j], (px, py, c)) for j, (px, py) in enumerate(chips)]


_PLAN_COPIES = {_plan_gather_out: 4, _plan_gather_forward: 3, _plan_sibling: 4, _plan_chips: 3}
_HBM = pl.BlockSpec(memory_space=pltpu.HBM)
_SEM = pl.BlockSpec(memory_space=pltpu.SEMAPHORE)
_EFFECT = pltpu.SideEffectType.DATAFLOW_SIDE_EFFECTING


def _hbm(a):
    return pltpu.with_memory_space_constraint(a, pltpu.HBM)


def _plan_descriptors(plans, srcs, lands, send, recv):
    x, y, c, chips = _place()
    cps = []
    for plan, src, land in zip(plans, srcs, lands):
        for s_ref, d_ref, to in plan(src, land, x, y, c, chips):
            k = len(cps)
            cps.append(pltpu.make_async_remote_copy(src_ref=s_ref, dst_ref=d_ref, send_sem=send.at[k],
                                                    recv_sem=recv.at[k], device_id=to, device_id_type=MESH))
    return cps


def _copies_start(name, plans, srcs, lands, after=None):
    has_src = [s is not None for s in srcs]
    arrays = [s for s in srcs if s is not None] + list(lands)
    n_src = sum(has_src)
    n_cp = sum(_PLAN_COPIES[p] for p in plans)
    afters = [] if after is None else [after]

    def body(*refs):
        ins = refs[:len(arrays)]
        send, recv = refs[len(arrays) + len(afters)], refs[len(arrays) + len(afters) + 1]
        token = refs[-1]
        it = iter(ins[:n_src])
        src_refs = [next(it) if h else None for h in has_src]
        for cp in _plan_descriptors(plans, src_refs, ins[n_src:], send, recv):
            cp.start()
        token[...] = jnp.zeros_like(token)

    outs = pl.pallas_call(
        body, name=name,
        in_specs=[_HBM] * len(arrays) + [pl.BlockSpec(memory_space=pl.ANY)] * len(afters),
        out_specs=(_SEM, _SEM, *[_HBM] * len(arrays), pl.BlockSpec(memory_space=pltpu.VMEM)),
        out_shape=(pltpu.SemaphoreType.DMA((n_cp,)), pltpu.SemaphoreType.DMA((n_cp,)),
                   *[pltpu.HBM(a.shape, a.dtype) for a in arrays], jax.ShapeDtypeStruct((8, LANES), F32)),
        input_output_aliases={i: 2 + i for i in range(len(arrays))},
        compiler_params=pltpu.CompilerParams(has_side_effects=_EFFECT),
    )(*[_hbm(a) for a in arrays], *afters)
    send, recv = outs[0], outs[1]
    thru = list(outs[2:-1])
    it = iter(thru[:n_src])
    return send, recv, [next(it) if h else None for h in has_src], thru[n_src:], outs[-1]


def _copies_wait(name, plans, started, after):
    send, recv, srcs, lands, _ = started
    has_src = [s is not None for s in srcs]
    arrays = [s for s in srcs if s is not None] + list(lands)
    n_src = sum(has_src)

    def body(*refs):
        ins = refs[:len(arrays)]
        send_ref, recv_ref = refs[len(arrays)], refs[len(arrays) + 1]
        it = iter(ins[:n_src])
        src_refs = [next(it) if h else None for h in has_src]
        for cp in _plan_descriptors(plans, src_refs, ins[n_src:], send_ref, recv_ref):
            cp.wait_send()
            cp.wait_recv()

    outs = pl.pallas_call(
        body, name=name,
        in_specs=[_HBM] * len(arrays) + [_SEM, _SEM, pl.BlockSpec(memory_space=pl.ANY)],
        out_specs=tuple([_HBM] * len(arrays)),
        out_shape=tuple(pltpu.HBM(a.shape, a.dtype) for a in arrays),
        input_output_aliases={i: i for i in range(len(arrays))},
        compiler_params=pltpu.CompilerParams(has_side_effects=_EFFECT),
    )(*arrays, send, recv, after)
    return list(outs[:n_src]), list(outs[n_src:])


def _shard_tiles(rows, cols, tr):
    if rows % tr == 0:
        return (tr, cols), rows // tr, lambda r: (r, 0)
    tc = 2 * LANES
    return (rows, tc), cols // tc, lambda r: (0, r)


def _pair_sum(name, grad, from_sibling, blocks):
    _, rows, cols = grad.shape
    (br_, bc), steps, at = _shard_tiles(rows, cols, 256)

    def body(blk_ref, g_ref, s_ref, o_ref):
        o_ref[...] = (g_ref[...] + s_ref[...]).astype(BF16)

    return pl.pallas_call(
        body, name=name,
        grid_spec=pltpu.PrefetchScalarGridSpec(
            num_scalar_prefetch=1, grid=(3, steps),
            in_specs=[pl.BlockSpec((None, br_, bc), lambda j, r, br: (br[j], *at(r))),
                      pl.BlockSpec((None, br_, bc), lambda j, r, br: (br[3 + j], *at(r)))],
            out_specs=pl.BlockSpec((None, br_, bc), lambda j, r, br: (j, *at(r)))),
        out_shape=jax.ShapeDtypeStruct((3, rows, cols), BF16),
        compiler_params=_cparams(("parallel", "parallel")),
    )(blocks, grad, from_sibling)


def _adam_math(g, w, m, v):
    m2 = ADAM_B1 * m + (1.0 - ADAM_B1) * g
    v2 = ADAM_B2 * v + (1.0 - ADAM_B2) * (g * g)
    m_hat = m2 / (1.0 - ADAM_B1 ** ADAM_STEP)
    v_hat = v2 / (1.0 - ADAM_B2 ** ADAM_STEP)
    delta = -ADAM_LR * (m_hat / (jnp.sqrt(v_hat) + ADAM_EPS) + ADAM_WD * w)
    return delta, m2, v2


def _sum_adam(name, grad, from_sibling, from_chips, own, w, m, v, dep=None):
    rows, cols = w.shape
    (br_, bc), steps, at = _shard_tiles(rows, cols, 128)

    deps, dep_specs = _dep_operand(dep)

    def body(own_ref, p_ref, s_ref, r_ref, w_ref, m_ref, v_ref, *rest):
        g_out, d_out, m_out, v_out = rest[len(deps):]
        g = p_ref[...] + s_ref[...]
        for j in range(3):
            g = g + r_ref[j].astype(F32)
        d, m2, v2 = _adam_math(g, w_ref[...], m_ref[...], v_ref[...])
        g_out[...] = g
        d_out[...] = d
        m_out[...] = m2
        v_out[...] = v2

    blk = pl.BlockSpec((br_, bc), lambda r, cr: at(r))
    shp = jax.ShapeDtypeStruct((rows, cols), F32)
    return pl.pallas_call(
        body, name=name,
        grid_spec=pltpu.PrefetchScalarGridSpec(
            num_scalar_prefetch=1, grid=(steps,),
            in_specs=[pl.BlockSpec((None, br_, bc), lambda r, cr: (cr[0], *at(r))),
                      pl.BlockSpec((None, br_, bc), lambda r, cr: (cr[1], *at(r))),
                      pl.BlockSpec((3, br_, bc), lambda r, cr: (0, *at(r))),
                      blk, blk, blk, *dep_specs],
            out_specs=[blk, blk, blk, blk]),
        out_shape=[shp, shp, shp, shp],
        compiler_params=_cparams(("parallel",)),
    )(own, grad, from_sibling, from_chips, w, m, v, *deps)


def _adam_small(name, g, w, m, v):
    def body(g_ref, w_ref, m_ref, v_ref, d_out, m_out, v_out):
        d, m2, v2 = _adam_math(g_ref[...], w_ref[...], m_ref[...], v_ref[...])
        d_out[...] = d
        m_out[...] = m2
        v_out[...] = v2

    shp = jax.ShapeDtypeStruct(w.shape, F32)
    return pl.pallas_call(body, name=name, out_shape=[shp, shp, shp])(g, w, m, v)


def _all_reduce_small(pack):
    rows = pack.shape[0]

    def body(in_ref, out_ref, slots, send, recv):
        x, y, c, _ = _place()
        me = 4 * x + 2 * y + c
        slots[me] = in_ref[...]
        cps = []
        for k in range(1, N_DEV):
            dx, dy, dc = (k >> 2) & 1, (k >> 1) & 1, k & 1
            to = (jnp.bitwise_xor(x, dx), jnp.bitwise_xor(y, dy), jnp.bitwise_xor(c, dc))
            cps.append(pltpu.make_async_remote_copy(
                src_ref=in_ref, dst_ref=slots.at[me], send_sem=send.at[k - 1], recv_sem=recv.at[k - 1],
                device_id=to, device_id_type=MESH))
        for cp in cps:
            cp.start()
        for cp in cps:
            cp.wait()
        acc = slots[0]
        for d in range(1, N_DEV):
            acc = acc + slots[d]
        out_ref[...] = acc

    return pl.pallas_call(
        body, name="all_reduce_small",
        in_specs=[pl.BlockSpec(memory_space=pltpu.VMEM)],
        out_specs=pl.BlockSpec(memory_space=pltpu.VMEM),
        out_shape=jax.ShapeDtypeStruct((rows, LANES), F32),
        scratch_shapes=[pltpu.VMEM((N_DEV, rows, LANES), F32),
                        pltpu.SemaphoreType.DMA((N_DEV - 1,)), pltpu.SemaphoreType.DMA((N_DEV - 1,))],
    )(pack)


def _rows128(a):
    flat = a.reshape(-1)
    padn = (-flat.shape[0]) % (8 * LANES)
    if padn:
        flat = jnp.concatenate([flat, jnp.zeros((padn,), flat.dtype)])
    return flat.reshape(-1, LANES)


def kernel(x, w_in, w_gk2, b_gk, gla_norm_w, swa_sinks, w_out, ln1_g, ln1_b, w_up, w_down, ln2_g, ln2_b, loss_target, m_w_in, m_w_gk2, m_b_gk, m_gla_norm_w, m_swa_sinks, m_w_out, m_ln1_g, m_ln1_b, m_w_up, m_w_down, m_ln2_g, m_ln2_b, v_w_in, v_w_gk2, v_b_gk, v_gla_norm_w, v_swa_sinks, v_w_out, v_ln1_g, v_ln1_b, v_w_up, v_w_down, v_ln2_g, v_ln2_b):
    xc, yc, cc = lax.axis_index("x"), lax.axis_index("y"), lax.axis_index("c")
    me = 4 * xc + 2 * yc + cc

    x2 = x[0]
    t = x2.shape[0]
    target = loss_target[0]

    def land_of(sh):
        return lax.dynamic_update_index_in_dim(lax.empty((N_DEV,) + sh.shape, BF16), sh, me, 0)

    out_plans = [_plan_gather_out] * 3
    fwd_plans = [_plan_gather_forward] * 3
    in_shards = [w_in[0].T.astype(BF16), w_gk2[0].astype(BF16)]
    ag0 = _copies_start("gather_in_start", out_plans[:2], in_shards, [land_of(sh) for sh in in_shards])
    zero = ag0[4][0, 0]
    xb = (x2 + zero).astype(BF16)
    shards = [(w[0] + zero).astype(BF16) for w in (w_out, w_up, w_down)]
    lands = [land_of(sh) for sh in shards]
    _, in_lands = _copies_wait("gather_in_wait", out_plans[:2], ag0, lands[2])
    ag0f = _copies_start("gather_in_forward_start", fwd_plans[:2], [None] * 2, in_lands)
    _, first = _copies_wait("gather_in_forward_wait", fwd_plans[:2], ag0f, ag0f[4])
    w_in_rows = first[0].reshape(D_IN, D_MODEL)
    w_in_t = _to_padded_rows(w_in_rows)
    w_in_o = jnp.concatenate([w_in_rows, jnp.zeros((D_INP - D_IN, D_MODEL), BF16)], axis=0)
    w2 = first[1].transpose(1, 0, 2).reshape(GATE_RANK, GLA_KW)
    w2p = jnp.concatenate([w2, jnp.zeros((LANES - GATE_RANK, GLA_KW), BF16)], axis=0)
    ag1 = _copies_start("gather_out_start", out_plans, shards, lands, first[0])

    proj = _proj(xb, w_in_t, dep=ag1[4])
    rt = min(256, t)
    ii = jnp.arange(rt)
    tri = ((ii[:, None] // GC == ii[None, :] // GC) & (ii[None, :] <= ii[:, None])).astype(F32)
    bloc, dgdz = _gate_fwd(proj, w2p, b_gk, tri)
    o_raw, states = _gla_fwd(proj, bloc)
    gla_out = _gla_norm_fwd(o_raw, proj, gla_norm_w)
    cat = _swa_fwd(proj, swa_sinks[0], gla_out)
    _, lands = _copies_wait("gather_out_wait", out_plans, ag1, cat)
    ag2 = _copies_start("gather_forward_out_start", fwd_plans[:1], [None], lands[:1])
    _, (w_out_g,) = _copies_wait("gather_forward_out_wait", fwd_plans[:1], ag2, ag2[4])
    ag3 = _copies_start("gather_forward_mlp_start", fwd_plans[:2], [None] * 2, lands[1:], w_out_g)
    w_out_f = w_out_g.reshape(D_MODEL, D_MODEL)
    r1, h1, h1b = _mix_ln1(cat, w_out_f, x2, ln1_g, ln1_b, dep=ag3[4])
    _, (w_up_f, w_down_g) = _copies_wait("gather_forward_mlp_wait", fwd_plans[:2], ag3, h1b)
    w_down_f = w_down_g.reshape(D_FF, D_MODEL)
    a_act, hdn = _mlp_up(h1b, w_up_f)
    dr2, dr2b, g_ln2_g, g_ln2_b, loss_part = _mlp_down_loss(hdn, w_down_f, h1, target, ln2_g, ln2_b)

    others = [2 * (1 - xc) + yc, 2 * xc + (1 - yc), 2 * (1 - xc) + (1 - yc)]
    blocks = jnp.stack([2 * q + cc for q in others] + others).astype(jnp.int32)
    own = jnp.stack([me, 2 * xc + yc]).astype(jnp.int32)
    wmv = dict(w_in=[a[0].T for a in (w_in, m_w_in, v_w_in)], w_out=[a[0] for a in (w_out, m_w_out, v_w_out)],
               w_up=[a[0] for a in (w_up, m_w_up, v_w_up)], w_down=[a[0] for a in (w_down, m_w_down, v_w_down)])
    big = {}

    def sib_land(g):
        return lax.empty((4,) + g.shape[1:], F32)

    def chip_land(g):
        return lax.empty((3,) + g.shape[1:], BF16)

    def finish(nm, g, from_sib, from_chips, dep=None):
        outs = _sum_adam("sum_adam_" + nm, g, from_sib, from_chips, own, *wmv[nm], dep=dep)
        big[nm] = [(o.T if nm == "w_in" else o)[None] for o in outs]

    du = _mlp_down_bwd(dr2b, w_down_f, a_act)
    g_down = _grad_w_down(hdn, dr2b)
    sa_down = _copies_start("sibling_down_start", [_plan_sibling], [g_down], [sib_land(g_down)])
    g_up = _grad_w_up(h1b, du, dep=sa_down[4])
    (g_down,), (fs_down,) = _copies_wait("sibling_down_wait", [_plan_sibling], sa_down, g_up)
    p_down = _pair_sum("pair_sum_w_down", g_down, fs_down, blocks)
    sb_down = _copies_start("chips_down_sibling_up_start", [_plan_chips, _plan_sibling], [p_down, g_up],
                            [chip_land(g_down), sib_land(g_up)])
    dr1, dr1b, g_ln1_g, g_ln1_b = _mlp_up_bwd_ln1(du, w_up_f, dr2, r1, ln1_g, dep=sb_down[4])
    dcat = _dcat(dr1b, w_out_f)
    (_, g_up), (fc_down, fs_up) = _copies_wait("chips_down_sibling_up_wait", [_plan_chips, _plan_sibling], sb_down,
                                               dcat)
    finish("w_down", g_down, fs_down, fc_down)
    p_up = _pair_sum("pair_sum_w_up", g_up, fs_up, blocks)
    sb_up = _copies_start("chips_up_start", [_plan_chips], [p_up], [chip_land(g_up)])
    do_raw, dg_g, g_norm_w = _gla_norm_bwd(dcat, o_raw, proj, gla_norm_w, dep=sb_up[4])
    dq_g, dk_g, dv_g, dgk = _gla_bwd(proj, bloc, do_raw, states)
    _, (fc_up,) = _copies_wait("chips_up_wait", [_plan_chips], sb_up, dgk)
    finish("w_up", g_up, fs_up, fc_up)
    dlo, gw2_p, g_b_gk = _gate_bwd(dgk, dgdz, proj, w2p, tri.T)
    dq_s, dk_s, dv_s, g_sinks = _swa_bwd(proj, dcat, swa_sinks[0])
    dproj = jnp.concatenate([dq_g, dk_g, dv_g, dg_g, dlo[:, :GATE_RANK], dq_s, dk_s.astype(BF16), dv_s.astype(BF16),
                             jnp.zeros((t, D_INP - D_IN), BF16)], axis=-1)
    gw_in_t = _grad_w_in(xb, dproj)
    g_in = gw_in_t[:D_IN].reshape(N_DEV, D_IN // N_DEV, D_MODEL)
    sa_in = _copies_start("sibling_in_start", [_plan_sibling], [g_in], [sib_land(g_in)])
    g_out = _grad_w_out(cat, dr1b, dep=sa_in[4])
    (g_in,), (fs_in,) = _copies_wait("sibling_in_wait", [_plan_sibling], sa_in, g_out)
    p_in = _pair_sum("pair_sum_w_in", g_in, fs_in, blocks)
    sb_in = _copies_start("chips_in_sibling_out_start", [_plan_chips, _plan_sibling], [p_in, g_out],
                          [chip_land(g_in), sib_land(g_out)])
    grad_x = _grad_x(dproj, w_in_o, dr1, dep=sb_in[4])
    (_, g_out), (fc_in, fs_out) = _copies_wait("chips_in_sibling_out_wait", [_plan_chips, _plan_sibling], sb_in,
                                               grad_x)
    p_out = _pair_sum("pair_sum_w_out", g_out, fs_out, blocks)
    sb_out = _copies_start("chips_out_start", [_plan_chips], [p_out], [chip_land(g_out)])
    finish("w_in", g_in, fs_in, fc_in, dep=sb_out[4])

    pieces = [loss_part, g_b_gk, g_norm_w, g_sinks[:, :SWA_HEADS], g_ln1_g, g_ln1_b, g_ln2_g, g_ln2_b,
              gw2_p[:GATE_RANK]]
    pack = jnp.concatenate([_rows128(p) for p in pieces], axis=0)
    tot = _all_reduce_small(pack + sb_out[4][:1, :1])
    sizes = [p.size for p in pieces]
    offs = [0]
    for p in pieces:
        offs.append(offs[-1] + _rows128(p).shape[0])
    unpack = lambda i, shape: tot[offs[i]:offs[i + 1]].reshape(-1)[:sizes[i]].reshape(shape)
    loss = tot[0, 0]
    small_names = ["b_gk", "gla_norm_w", "swa_sinks", "ln1_g", "ln1_b", "ln2_g", "ln2_b"]
    small_w = dict(b_gk=(b_gk, m_b_gk, v_b_gk), gla_norm_w=(gla_norm_w, m_gla_norm_w, v_gla_norm_w),
                   swa_sinks=(swa_sinks, m_swa_sinks, v_swa_sinks), ln1_g=(ln1_g, m_ln1_g, v_ln1_g),
                   ln1_b=(ln1_b, m_ln1_b, v_ln1_b), ln2_g=(ln2_g, m_ln2_g, v_ln2_g),
                   ln2_b=(ln2_b, m_ln2_b, v_ln2_b))
    small_g = {nm: unpack(1 + i, small_w[nm][0].shape) for i, nm in enumerate(small_names)}
    g_pack = jnp.concatenate([_rows128(small_g[nm]) for nm in small_names], axis=0)
    small_wmv = [jnp.concatenate([_rows128(small_w[nm][k]) for nm in small_names], axis=0) for k in range(3)]
    small_out = _adam_small("adam_replicated", g_pack, *small_wmv)
    srow = [0]
    for nm in small_names:
        srow.append(srow[-1] + _rows128(small_w[nm][0]).shape[0])
    small = {}
    for i, nm in enumerate(small_names):
        shape = small_w[nm][0].shape
        n = small_w[nm][0].size
        small[nm] = [small_g[nm]] + [o[srow[i]:srow[i + 1]].reshape(-1)[:n].reshape(shape) for o in small_out]

    gw2_full = unpack(8, (GATE_RANK, GLA_KW))
    gw2_loc = lax.dynamic_slice_in_dim(gw2_full, me * (GLA_KW // N_DEV), GLA_KW // N_DEV, axis=1)
    gk2_out = _adam_small("adam_w_gk2", gw2_loc, w_gk2[0], m_w_gk2[0], v_w_gk2[0])
    big["w_gk2"] = [gw2_loc[None]] + [o[None] for o in gk2_out]

    _, (fc_out,) = _copies_wait("chips_out_wait", [_plan_chips], sb_out, tot)
    finish("w_out", g_out, fs_out, fc_out)

    order = ["w_in", "w_gk2", "b_gk", "gla_norm_w", "swa_sinks", "w_out", "ln1_g", "ln1_b", "w_up", "w_down",
             "ln2_g", "ln2_b"]
    res = {**big, **small}
    outs = [loss, grad_x[None]]
    for k in range(4):
        outs += [res[nm][k] for nm in order]
    return tuple(outs)
```

```python
import functools

import jax
import jax.numpy as jnp
from jax import lax
from jax.experimental import pallas as pl
from jax.experimental.pallas import tpu as pltpu

F32 = jnp.float32
BF16 = jnp.bfloat16

N_DEV = 8
D_MODEL = 2048
D_FF = 8192
GLA_HEADS = 4
GLA_DK = 128
GLA_DV = 256
GLA_KW = 512
D_GLA = 1024
GATE_RANK = 16
GATE_TAU = 16.0
SWA_HEADS = 16
SWA_DH = 64
SWA_GROUP = 8
WINDOW = 128
D_SWA = 1024
D_IN = 4368
ALPHA = 2.0 ** 0.25
LN_EPS = 1e-5
RMS_EPS = 1e-5
ADAM_LR = 0.001
ADAM_B1 = 0.9
ADAM_B2 = 0.999
ADAM_EPS = 1e-08
ADAM_WD = 0.01
ADAM_STEP = 10

C_QG, C_KG, C_VG, C_GG, C_QS, C_KS, C_VS, C_LO = 0, 512, 1024, 2048, 3072, 4096, 4224, 4352
D_INP = 4480
LANES = 128
GC = 16
NEG = -1e30

NN = ((1,), (0,))
NT = ((1,), (1,))
TN = ((0,), (0,))

VMEM_LIMIT = 52 * 1024 * 1024
MESH = pl.DeviceIdType.MESH


def _dot(a, b, dn, precision=None):
    return lax.dot_general(a, b, (dn, ((), ())), preferred_element_type=F32, precision=precision)


def _bf16_round(v):
    return v.astype(BF16).astype(F32)


def _cparams(dims):
    return pltpu.CompilerParams(dimension_semantics=dims, vmem_limit_bytes=VMEM_LIMIT)


def _dep_operand(dep):
    if dep is None:
        return (), ()
    return (dep,), (pl.BlockSpec(dep.shape, lambda *_: (0,) * dep.ndim),)


def _to_padded_rows(w):
    pad = jnp.zeros((D_INP - D_IN,) + w.shape[1:], w.dtype)
    return jnp.concatenate([w[:3072], w[3088:], w[3072:3088], pad], axis=0)


def _matmul(name, a, b, dn, grid, a_spec, b_spec, out_shape, out_specs, acc_shape, *,
            extra=(), extra_specs=(), epilogue=None, dims=("parallel", "parallel", "arbitrary"), dep=None):
    nk = grid[2]
    n_extra = len(extra)
    deps, dep_specs = _dep_operand(dep)
    direct = epilogue is None and (nk == 1 or (not isinstance(out_shape, (list, tuple)) and out_shape.dtype == F32))

    scratch = [] if direct or nk == 1 else [pltpu.VMEM(acc_shape, F32)]

    def body(a_ref, b_ref, *rest):
        extra_refs = rest[:n_extra]
        out_refs = rest[n_extra + len(deps):len(rest) - len(scratch)]
        acc_ref = rest[-1] if scratch else out_refs[0]
        part = _dot(a_ref[...].astype(BF16), b_ref[...].astype(BF16), dn)

        def finish():
            if epilogue is None:
                out_refs[0][...] = acc_ref[...].astype(out_refs[0].dtype)
            else:
                epilogue(acc_ref, extra_refs, out_refs)

        if direct and nk == 1:
            out_refs[0][...] = part.reshape(out_refs[0].shape).astype(out_refs[0].dtype)
        elif nk == 1:
            epilogue(part, extra_refs, out_refs)
        else:
            k = pl.program_id(2)

            @pl.when(k == 0)
            def _():
                acc_ref[...] = part

            @pl.when(k > 0)
            def _():
                acc_ref[...] += part

            if not direct:
                @pl.when(k == nk - 1)
                def _():
                    finish()

    return pl.pallas_call(
        body, name=name, grid=grid,
        in_specs=[a_spec, b_spec, *extra_specs, *dep_specs],
        out_specs=out_specs, out_shape=out_shape,
        scratch_shapes=scratch,
        compiler_params=_cparams(dims),
    )(a, b, *extra, *deps)


def _row_chunks(rows, step=128):
    step = min(step, rows)
    return [(r, step) for r in range(0, rows, step)]


def _ln_stats(r):
    mu = jnp.mean(r, axis=-1, keepdims=True)
    xc = r - mu
    var = jnp.mean(xc * xc, axis=-1, keepdims=True)
    rstd = lax.rsqrt(var + LN_EPS)
    return xc * rstd, rstd


def _ln_bwd(dy_g, xhat, rstd):
    m1 = jnp.mean(dy_g, axis=-1, keepdims=True)
    m2 = jnp.mean(dy_g * xhat, axis=-1, keepdims=True)
    return rstd * (dy_g - m1 - xhat * m2)


def _proj(xb, w_in_t, dep=None):
    t = xb.shape[0]
    tm, tn = min(1024, t), 896
    return _matmul(
        "proj", xb, w_in_t, NT, (t // tm, D_INP // tn, 1),
        pl.BlockSpec((tm, D_MODEL), lambda i, j, k: (i, 0)),
        pl.BlockSpec((tn, D_MODEL), lambda i, j, k: (j, 0)),
        jax.ShapeDtypeStruct((t, D_INP), F32),
        pl.BlockSpec((tm, tn), lambda i, j, k: (i, j)),
        (tm, tn), dep=dep)


def _gate_fwd(proj, w2p, b_gk, tri):
    t = proj.shape[0]
    r = tri.shape[0]

    def body(lo_ref, w_ref, b_ref, tri_ref, bloc_ref, dgdz_ref):
        z = _dot(lo_ref[...].astype(BF16), w_ref[...], NN) + b_ref[...]
        e = jnp.exp(-jnp.abs(z))
        gk = (jnp.minimum(z, 0.0) - jnp.log1p(e)) * (1.0 / GATE_TAU)
        inv = 1.0 / (1.0 + e)
        dgdz_ref[...] = jnp.where(z >= 0.0, e * inv, inv) * (1.0 / GATE_TAU)
        bloc_ref[...] = _dot(tri_ref[...], gk, NN, precision=lax.Precision.HIGHEST)

    return pl.pallas_call(
        body, name="gate_fwd", grid=(t // r,),
        in_specs=[pl.BlockSpec((r, LANES), lambda i: (i, C_LO // LANES)),
                  pl.BlockSpec((LANES, GLA_KW), lambda i: (0, 0)),
                  pl.BlockSpec((1, GLA_KW), lambda i: (0, 0)),
                  pl.BlockSpec((r, r), lambda i: (0, 0))],
        out_specs=[pl.BlockSpec((r, GLA_KW), lambda i: (i, 0)),
                   pl.BlockSpec((r, GLA_KW), lambda i: (i, 0))],
        out_shape=[jax.ShapeDtypeStruct((t, GLA_KW), F32), jax.ShapeDtypeStruct((t, GLA_KW), F32)],
        compiler_params=_cparams(("parallel",)),
    )(proj, w2p, b_gk, tri)


def _gla_fwd(proj, bloc):
    t = proj.shape[0]
    r = min(256, t)
    ncb = r // GC
    scale = GLA_DK ** -0.5

    def body(q_ref, k_ref, v_ref, b_ref, o_ref, st_ref, s_scr, m_scr):
        @pl.when(pl.program_id(1) == 0)
        def _():
            s_scr[...] = jnp.zeros_like(s_scr)

        rows = lax.broadcasted_iota(jnp.int32, (GC, 1), 0)
        cols = lax.broadcasted_iota(jnp.int32, (1, GC), 1)

        def increment(c, carry):
            r0 = pl.multiple_of(c * GC, GC)
            k = k_ref[pl.ds(r0, GC), :]
            b = b_ref[pl.ds(r0, GC), :]
            kd = k * jnp.exp(b[GC - 1:GC, :] - b)
            m_scr[c] = _dot(v_ref[pl.ds(r0, GC), :].astype(BF16), kd.astype(BF16), TN)
            return carry

        lax.fori_loop(0, ncb, increment, 0, unroll=8)

        def recur(c, st):
            st_ref[c] = st
            r0 = pl.multiple_of(c * GC, GC)
            bl = b_ref[pl.ds(r0, GC), :][GC - 1:GC, :]
            return st * jnp.exp(bl) + m_scr[c]

        s_scr[...] = lax.fori_loop(0, ncb, recur, s_scr[...])

        def output(c, carry):
            r0 = pl.multiple_of(c * GC, GC)
            q = q_ref[pl.ds(r0, GC), :] * scale
            b = b_ref[pl.ds(r0, GC), :]
            kr = _bf16_round(k_ref[pl.ds(r0, GC), :])
            att = jnp.zeros((GC, GC), F32)
            for j in range(GC):
                w = jnp.exp(jnp.where(rows >= j, b - b[j:j + 1, :], NEG))
                a = jnp.sum(_bf16_round(q * w) * kr[j:j + 1, :], axis=-1, keepdims=True)
                att = jnp.where(cols == j, a, att)
            o_ref[pl.ds(r0, GC), :] = (_dot((q * jnp.exp(b)).astype(BF16), st_ref[c].astype(BF16), NT)
                                       + _dot(att.astype(BF16), v_ref[pl.ds(r0, GC), :].astype(BF16), NN))
            return carry

        lax.fori_loop(0, ncb, output, 0, unroll=8)

    return pl.pallas_call(
        body, name="gla_fwd", grid=(GLA_HEADS, t // r),
        in_specs=[pl.BlockSpec((r, GLA_DK), lambda h, i: (i, C_QG // GLA_DK + h)),
                  pl.BlockSpec((r, GLA_DK), lambda h, i: (i, C_KG // GLA_DK + h)),
                  pl.BlockSpec((r, GLA_DV), lambda h, i: (i, C_VG // GLA_DV + h)),
                  pl.BlockSpec((r, GLA_DK), lambda h, i: (i, h))],
        out_specs=[pl.BlockSpec((r, GLA_DV), lambda h, i: (i, h)),
                   pl.BlockSpec((ncb, None, GLA_DV, GLA_DK), lambda h, i: (i, h, 0, 0))],
        out_shape=[jax.ShapeDtypeStruct((t, D_GLA), F32),
                   jax.ShapeDtypeStruct((t // GC, GLA_HEADS, GLA_DV, GLA_DK), F32)],
        scratch_shapes=[pltpu.VMEM((GLA_DV, GLA_DK), F32), pltpu.VMEM((ncb, GLA_DV, GLA_DK), F32)],
        compiler_params=_cparams(("parallel", "arbitrary")),
    )(proj, proj, proj, bloc)


def _gla_norm_fwd(o_raw, proj, norm_w):
    t = o_raw.shape[0]
    r = min(512, t)

    def body(o_ref, g_ref, w_ref, out_ref):
        w = w_ref[...]
        for h in range(GLA_HEADS):
            sl = slice(h * GLA_DV, (h + 1) * GLA_DV)
            o = o_ref[:, sl]
            g = g_ref[:, sl]
            on = o * lax.rsqrt(jnp.mean(o * o, axis=-1, keepdims=True) + RMS_EPS)
            out_ref[:, sl] = (on * w * (g * jax.nn.sigmoid(g))).astype(BF16)

    return pl.pallas_call(
        body, name="gla_norm_fwd", grid=(t // r,),
        in_specs=[pl.BlockSpec((r, D_GLA), lambda i: (i, 0)),
                  pl.BlockSpec((r, D_GLA), lambda i: (i, C_GG // D_GLA)),
                  pl.BlockSpec((1, GLA_DV), lambda i: (0, 0))],
        out_specs=pl.BlockSpec((r, D_GLA), lambda i: (i, 0)),
        out_shape=jax.ShapeDtypeStruct((t, D_GLA), BF16),
        compiler_params=_cparams(("parallel",)),
    )(o_raw, proj, norm_w)


def _swa_masks(i, heads):
    qi = lax.broadcasted_iota(jnp.int32, (heads * WINDOW, 1), 0) & (WINDOW - 1)
    kj = lax.broadcasted_iota(jnp.int32, (1, 2 * WINDOW), 1)
    valid = (kj > qi) & (kj <= qi + WINDOW) & ((i > 0) | (kj >= WINDOW))
    lo = lax.broadcasted_iota(jnp.int32, (1, LANES), 1) < SWA_DH
    return valid, lo


def _dup_half(x, lo, kv):
    xr = pltpu.roll(x, SWA_DH, axis=1)
    return jnp.where(lo, x, xr) if kv == 0 else jnp.where(lo, xr, x)


def _swa_stack(ref, col0, lo, kv):
    parts = []
    for p in range(4):
        c0 = col0 + LANES * (4 * kv + p)
        xp = ref[:, c0:c0 + LANES]
        parts += [jnp.where(lo, xp, 0.0).astype(BF16), jnp.where(lo, 0.0, xp).astype(BF16)]
    return jnp.concatenate(parts, axis=0)


def _swa_unstack(x_all, lo, p):
    r0 = 2 * p * WINDOW
    return jnp.where(lo, x_all[r0:r0 + WINDOW, :], x_all[r0 + WINDOW:r0 + 2 * WINDOW, :])


def _swa_sinks(sink_ref, kv):
    return jnp.concatenate([jnp.full((WINDOW, 1), sink_ref[SWA_GROUP * kv + h], F32) for h in range(SWA_GROUP)],
                           axis=0)


def _swa_probs(qm, kdup, valid, sink):
    s = _dot(qm, kdup, NT) * (SWA_DH ** -0.5)
    s = jnp.where(valid, s, NEG)
    m = jnp.maximum(jnp.max(s, axis=-1, keepdims=True), sink)
    p = jnp.exp(s - m)
    es = jnp.exp(sink - m)
    inv = 1.0 / (jnp.sum(p, axis=-1, keepdims=True) + es)
    return p * inv, es * inv


def _swa_fwd(proj, sinks, gla_out, dep=None):
    t = proj.shape[0]

    deps, dep_specs = _dep_operand(dep)

    def body(sink_ref, q_ref, kp_ref, kc_ref, vp_ref, vc_ref, gla_ref, *rest):
        o_ref = rest[-1]
        i = pl.program_id(0)
        o_ref[:, :D_GLA] = gla_ref[...]
        valid, lo = _swa_masks(i, 1)
        kb = jnp.concatenate([kp_ref[...], kc_ref[...]], axis=0)
        vb = jnp.concatenate([vp_ref[...], vc_ref[...]], axis=0)
        for kv in range(2):
            kdup = _dup_half(kb, lo, kv).astype(BF16)
            vdup = _dup_half(vb, lo, kv).astype(BF16)
            for p in range(4):
                c0 = LANES * (4 * kv + p)
                qp = q_ref[:, c0:c0 + LANES]
                halves = []
                for e in range(2):
                    qm = jnp.where(lo if e == 0 else ~lo, qp, 0.0).astype(BF16)
                    pn, _ = _swa_probs(qm, kdup, valid, sink_ref[SWA_GROUP * kv + 2 * p + e])
                    halves.append(_dot(pn.astype(BF16), vdup, NN))
                o_ref[:, D_GLA + c0:D_GLA + c0 + LANES] = jnp.where(lo, halves[0], halves[1]).astype(BF16)

    kvspec = lambda col, prev: pl.BlockSpec(
        (WINDOW, LANES), (lambda i: (jnp.maximum(i - 1, 0), col)) if prev else (lambda i: (i, col)))
    return pl.pallas_call(
        body, name="swa_fwd", grid=(t // WINDOW,),
        in_specs=[pl.BlockSpec(memory_space=pltpu.SMEM),
                  pl.BlockSpec((WINDOW, D_SWA), lambda i: (i, C_QS // D_SWA)),
                  kvspec(C_KS // LANES, True), kvspec(C_KS // LANES, False),
                  kvspec(C_VS // LANES, True), kvspec(C_VS // LANES, False),
                  pl.BlockSpec((WINDOW, D_GLA), lambda i: (i, 0)), *dep_specs],
        out_specs=pl.BlockSpec((WINDOW, D_MODEL), lambda i: (i, 0)),
        out_shape=jax.ShapeDtypeStruct((t, D_MODEL), BF16),
        compiler_params=_cparams(("parallel",)),
    )(sinks, proj, proj, proj, proj, proj, gla_out, *deps)


def _mix_ln1(cat, w_out, x, ln_g, ln_b, dep=None):
    t = cat.shape[0]
    tm, tk = min(256, t), D_MODEL

    def epilogue(acc_ref, extra, outs):
        x_ref, g_ref, b_ref = extra
        r1_ref, h1_ref, h1b_ref = outs
        for r0, n in _row_chunks(tm):
            rs = slice(r0, r0 + n)
            r1 = ALPHA * x_ref[rs, :] + acc_ref[rs, :]
            xhat, _ = _ln_stats(r1)
            h = xhat * g_ref[...] + b_ref[...]
            r1_ref[rs, :] = r1
            h1_ref[rs, :] = h
            h1b_ref[rs, :] = h.astype(BF16)

    row = pl.BlockSpec((tm, D_MODEL), lambda i, j, k: (i, 0))
    vec = pl.BlockSpec((1, D_MODEL), lambda i, j, k: (0, 0))
    return _matmul(
        "mix_ln1", cat, w_out, NN, (t // tm, 1, D_MODEL // tk),
        pl.BlockSpec((tm, tk), lambda i, j, k: (i, k)),
        pl.BlockSpec((tk, D_MODEL), lambda i, j, k: (k, 0)),
        [jax.ShapeDtypeStruct((t, D_MODEL), F32), jax.ShapeDtypeStruct((t, D_MODEL), F32),
         jax.ShapeDtypeStruct((t, D_MODEL), BF16)],
        [row, row, row], (tm, D_MODEL),
        extra=(x, ln_g, ln_b), extra_specs=(row, vec, vec), epilogue=epilogue, dep=dep)


def _mlp_up(h1b, w_up):
    t = h1b.shape[0]
    tm, tn = min(1024, t), 1024

    def epilogue(acc_ref, extra, outs):
        a_ref, hdn_ref = outs
        for r0, n in _row_chunks(tm, 256):
            rs = slice(r0, r0 + n)
            a = jnp.maximum(acc_ref[rs, :], 0.0)
            a_ref[rs, :] = a.astype(BF16)
            hdn_ref[rs, :] = (a * a).astype(BF16)

    out = pl.BlockSpec((tm, tn), lambda i, j, k: (i, j))
    return _matmul(
        "mlp_up", h1b, w_up, NN, (t // tm, D_FF // tn, 1),
        pl.BlockSpec((tm, D_MODEL), lambda i, j, k: (i, 0)),
        pl.BlockSpec((None, D_MODEL, tn), lambda i, j, k: (j, 0, 0)),
        [jax.ShapeDtypeStruct((t, D_FF), BF16), jax.ShapeDtypeStruct((t, D_FF), BF16)],
        [out, out], (tm, tn), epilogue=epilogue)


def _mlp_down_loss(hdn, w_down, h1, target, ln_g, ln_b):
    t = hdn.shape[0]
    tm, tn, tk = min(1024, t), 1024, 4096
    ff = _matmul(
        "mlp_down", hdn, w_down, NN, (t // tm, D_MODEL // tn, D_FF // tk),
        pl.BlockSpec((tm, tk), lambda i, j, k: (i, k)),
        pl.BlockSpec((tk, tn), lambda i, j, k: (k, j)),
        jax.ShapeDtypeStruct((t, D_MODEL), F32),
        pl.BlockSpec((tm, tn), lambda i, j, k: (i, j)), (tm, tn))
    r = min(256, t)

    def body(ff_ref, h1_ref, t_ref, g_ref, b_ref, dr2_ref, dr2b_ref, gg_ref, gb_ref, loss_ref):
        @pl.when(pl.program_id(0) == 0)
        def _():
            gg_ref[...] = jnp.zeros_like(gg_ref)
            gb_ref[...] = jnp.zeros_like(gb_ref)
            loss_ref[...] = jnp.zeros_like(loss_ref)

        for r0, n in _row_chunks(r, 64):
            rs = slice(r0, r0 + n)
            xhat, rstd = _ln_stats(ALPHA * h1_ref[rs, :] + ff_ref[rs, :])
            err = xhat * g_ref[...] + b_ref[...] - t_ref[rs, :]
            loss_ref[...] += 0.5 * jnp.sum(jnp.mean(err * err, axis=-1, keepdims=True))
            dy = err * (1.0 / D_MODEL)
            gg_ref[...] += jnp.sum(dy * xhat, axis=0, keepdims=True)
            gb_ref[...] += jnp.sum(dy, axis=0, keepdims=True)
            dr2 = _ln_bwd(dy * g_ref[...], xhat, rstd)
            dr2_ref[rs, :] = dr2
            dr2b_ref[rs, :] = dr2.astype(BF16)

    row = pl.BlockSpec((r, D_MODEL), lambda i: (i, 0))
    vec = pl.BlockSpec((1, D_MODEL), lambda i: (0, 0))
    return pl.pallas_call(
        body, name="ln2_loss", grid=(t // r,),
        in_specs=[row, row, row, vec, vec],
        out_specs=[row, row, vec, vec, pl.BlockSpec((1, LANES), lambda i: (0, 0))],
        out_shape=[jax.ShapeDtypeStruct((t, D_MODEL), F32), jax.ShapeDtypeStruct((t, D_MODEL), BF16),
                   jax.ShapeDtypeStruct((1, D_MODEL), F32), jax.ShapeDtypeStruct((1, D_MODEL), F32),
                   jax.ShapeDtypeStruct((1, LANES), F32)],
        compiler_params=_cparams(("arbitrary",)),
    )(ff, h1, target, ln_g, ln_b)


def _mlp_down_bwd(dr2b, w_down, a_act):
    t = dr2b.shape[0]
    tm, tn = min(1024, t), 1024

    def epilogue(acc_ref, extra, outs):
        (a_ref,) = extra
        for r0, n in _row_chunks(tm, 256):
            rs = slice(r0, r0 + n)
            outs[0][rs, :] = (acc_ref[rs, :] * (2.0 * a_ref[rs, :].astype(F32))).astype(BF16)

    blk = pl.BlockSpec((tm, tn), lambda i, j, k: (i, j))
    return _matmul(
        "mlp_down_bwd", dr2b, w_down, NT, (t // tm, D_FF // tn, 1),
        pl.BlockSpec((tm, D_MODEL), lambda i, j, k: (i, 0)),
        pl.BlockSpec((tn, D_MODEL), lambda i, j, k: (j, 0)),
        jax.ShapeDtypeStruct((t, D_FF), BF16), blk, (tm, tn),
        extra=(a_act,), extra_specs=(blk,), epilogue=epilogue)


def _grad_w_down(hdn, dr2b):
    t = hdn.shape[0]
    tm, tn = 1024, 1024
    return _matmul(
        "grad_w_down", hdn, dr2b, TN, (D_MODEL // tn, D_FF // tm, 1),
        pl.BlockSpec((t, tm), lambda j, i, k: (0, i)),
        pl.BlockSpec((t, tn), lambda j, i, k: (0, j)),
        jax.ShapeDtypeStruct((N_DEV, D_FF // N_DEV, D_MODEL), F32),
        pl.BlockSpec((None, tm, tn), lambda j, i, k: (i, 0, j)), (tm, tn))


def _grad_w_up(h1b, du, dep=None):
    t = h1b.shape[0]
    tm, tn = 1024, 1024
    return _matmul(
        "grad_w_up", h1b, du, TN, (N_DEV, D_MODEL // tm, 1),
        pl.BlockSpec((t, tm), lambda i, j, k: (0, j)),
        pl.BlockSpec((t, tn), lambda i, j, k: (0, i)),
        jax.ShapeDtypeStruct((N_DEV, D_MODEL, D_FF // N_DEV), F32),
        pl.BlockSpec((None, tm, tn), lambda i, j, k: (i, j, 0)), (tm, tn), dep=dep)


def _mlp_up_bwd_ln1(du, w_up, dr2, r1, ln_g, dep=None):
    t = du.shape[0]
    tm, tn, tk, nb = min(1024, t), 1024, D_FF // N_DEV, 4
    deps, dep_specs = _dep_operand(dep)

    def mm_body(a_ref, b_ref, *rest):
        o_ref = rest[-1]
        k = pl.program_id(2)
        part = _dot(a_ref[:, :tk], b_ref[0], NT)
        for d in range(1, nb):
            part = part + _dot(a_ref[:, d * tk:(d + 1) * tk], b_ref[d], NT)

        @pl.when(k == 0)
        def _():
            o_ref[...] = part

        @pl.when(k > 0)
        def _():
            o_ref[...] += part

    dff = pl.pallas_call(
        mm_body, name="mlp_up_bwd", grid=(t // tm, D_MODEL // tn, N_DEV // nb),
        in_specs=[pl.BlockSpec((tm, nb * tk), lambda i, j, k: (i, k)),
                  pl.BlockSpec((nb, tn, tk), lambda i, j, k: (k, j, 0)), *dep_specs],
        out_specs=pl.BlockSpec((tm, tn), lambda i, j, k: (i, j)),
        out_shape=jax.ShapeDtypeStruct((t, D_MODEL), F32),
        compiler_params=_cparams(("parallel", "parallel", "arbitrary")),
    )(du, w_up, *deps)
    r = min(256, t)

    def body(acc_ref, dr2_ref, r1_ref, g_ref, dr1_ref, dr1b_ref, gg_ref, gb_ref):
        @pl.when(pl.program_id(0) == 0)
        def _():
            gg_ref[...] = jnp.zeros_like(gg_ref)
            gb_ref[...] = jnp.zeros_like(gb_ref)

        for r0, n in _row_chunks(r, 64):
            rs = slice(r0, r0 + n)
            dh1 = ALPHA * dr2_ref[rs, :] + acc_ref[rs, :]
            xhat, rstd = _ln_stats(r1_ref[rs, :])
            gg_ref[...] += jnp.sum(dh1 * xhat, axis=0, keepdims=True)
            gb_ref[...] += jnp.sum(dh1, axis=0, keepdims=True)
            dr1 = _ln_bwd(dh1 * g_ref[...], xhat, rstd)
            dr1_ref[rs, :] = dr1
            dr1b_ref[rs, :] = dr1.astype(BF16)

    row = pl.BlockSpec((r, D_MODEL), lambda i: (i, 0))
    vec = pl.BlockSpec((1, D_MODEL), lambda i: (0, 0))
    return pl.pallas_call(
        body, name="ln1_bwd", grid=(t // r,),
        in_specs=[row, row, row, vec],
        out_specs=[row, row, vec, vec],
        out_shape=[jax.ShapeDtypeStruct((t, D_MODEL), F32), jax.ShapeDtypeStruct((t, D_MODEL), BF16),
                   jax.ShapeDtypeStruct((1, D_MODEL), F32), jax.ShapeDtypeStruct((1, D_MODEL), F32)],
        compiler_params=_cparams(("arbitrary",)),
    )(dff, dr2, r1, ln_g)


def _dcat(dr1b, w_out):
    t = dr1b.shape[0]
    tm, tn = min(1024, t), 1024
    return _matmul(
        "dcat", dr1b, w_out, NT, (t // tm, D_MODEL // tn, 1),
        pl.BlockSpec((tm, D_MODEL), lambda i, j, k: (i, 0)),
        pl.BlockSpec((tn, D_MODEL), lambda i, j, k: (j, 0)),
        jax.ShapeDtypeStruct((t, D_MODEL), F32),
        pl.BlockSpec((tm, tn), lambda i, j, k: (i, j)), (tm, tn))


def _grad_w_out(cat, dr1b, dep=None):
    t = cat.shape[0]
    tm, tn = 1024, 1024
    return _matmul(
        "grad_w_out", cat, dr1b, TN, (D_MODEL // tm, D_MODEL // tn, 1),
        pl.BlockSpec((t, tm), lambda i, j, k: (0, i)),
        pl.BlockSpec((t, tn), lambda i, j, k: (0, j)),
        jax.ShapeDtypeStruct((N_DEV, D_MODEL // N_DEV, D_MODEL), F32),
        pl.BlockSpec((tm // (D_MODEL // N_DEV), D_MODEL // N_DEV, tn), lambda i, j, k: (i, 0, j)), (tm, tn), dep=dep)


def _gla_norm_bwd(dcat, o_raw, proj, norm_w, dep=None):
    t = o_raw.shape[0]
    r = min(512, t)

    deps, dep_specs = _dep_operand(dep)

    def body(d_ref, o_ref, g_ref, w_ref, *rest):
        do_ref, dg_ref, dw_ref = rest[len(deps):]

        @pl.when(pl.program_id(0) == 0)
        def _():
            dw_ref[...] = jnp.zeros_like(dw_ref)

        w = w_ref[...]
        dw = jnp.zeros((1, GLA_DV), F32)
        for h in range(GLA_HEADS):
            sl = slice(h * GLA_DV, (h + 1) * GLA_DV)
            o = o_ref[:, sl]
            g = g_ref[:, sl]
            d = d_ref[:, sl]
            rr = lax.rsqrt(jnp.mean(o * o, axis=-1, keepdims=True) + RMS_EPS)
            on = o * rr
            sg = jax.nn.sigmoid(g)
            sil = g * sg
            dg_ref[:, sl] = (d * on * w * (sg * (1.0 + g * (1.0 - sg)))).astype(BF16)
            dw = dw + jnp.sum(d * on * sil, axis=0, keepdims=True)
            don = d * w * sil
            do_ref[:, sl] = rr * (don - on * jnp.mean(don * on, axis=-1, keepdims=True))
        dw_ref[...] += dw

    return pl.pallas_call(
        body, name="gla_norm_bwd", grid=(t // r,),
        in_specs=[pl.BlockSpec((r, D_GLA), lambda i: (i, 0)),
                  pl.BlockSpec((r, D_GLA), lambda i: (i, 0)),
                  pl.BlockSpec((r, D_GLA), lambda i: (i, C_GG // D_GLA)),
                  pl.BlockSpec((1, GLA_DV), lambda i: (0, 0)), *dep_specs],
        out_specs=[pl.BlockSpec((r, D_GLA), lambda i: (i, 0)),
                   pl.BlockSpec((r, D_GLA), lambda i: (i, 0)),
                   pl.BlockSpec((1, GLA_DV), lambda i: (0, 0))],
        out_shape=[jax.ShapeDtypeStruct((t, D_GLA), F32), jax.ShapeDtypeStruct((t, D_GLA), BF16),
                   jax.ShapeDtypeStruct((1, GLA_DV), F32)],
        compiler_params=_cparams(("arbitrary",)),
    )(dcat, o_raw, proj, norm_w, *deps)


def _gla_bwd(proj, bloc, do_raw, states):
    t = proj.shape[0]
    r = min(256, t)
    ncb = r // GC
    nb = t // r
    scale = GLA_DK ** -0.5

    def body(q_ref, k_ref, v_ref, b_ref, do_ref, st_ref, dq_ref, dk_ref, dv_ref, db_ref, ds_scr):
        @pl.when(pl.program_id(1) == 0)
        def _():
            ds_scr[...] = jnp.zeros_like(ds_scr)

        rows = lax.broadcasted_iota(jnp.int32, (GC, 1), 0)

        def chunk(cc, carry):
            c = ncb - 1 - cc
            r0 = pl.multiple_of(c * GC, GC)
            q = q_ref[pl.ds(r0, GC), :] * scale
            k = k_ref[pl.ds(r0, GC), :]
            v = v_ref[pl.ds(r0, GC), :]
            b = b_ref[pl.ds(r0, GC), :]
            do = do_ref[pl.ds(r0, GC), :]
            st = st_ref[c]
            dsn = ds_scr[...]
            bl = b[GC - 1:GC, :]
            eb = jnp.exp(b)
            ekl = jnp.exp(bl - b)
            ebl = jnp.exp(bl)
            qh = q * eb
            kd = k * ekl
            dob = do.astype(BF16)
            dsb = dsn.astype(BF16)
            dqh = _dot(dob, st.astype(BF16), NN)
            dkd = _dot(v.astype(BF16), dsb, NN)
            dv = _dot(kd.astype(BF16), dsb, NT)
            dq_i = jnp.zeros((GC, GLA_DK), F32)
            dk_i = jnp.zeros((GC, GLA_DK), F32)
            dk_x = jnp.zeros((GC, GLA_DK), F32)
            kr = _bf16_round(k)
            vr = _bf16_round(v)
            dor = _bf16_round(do)
            for i in range(GC):
                w = jnp.exp(jnp.where(rows <= i, b[i:i + 1, :] - b, NEG))
                qw = q[i:i + 1, :] * w
                qwr = _bf16_round(qw)
                a = _bf16_round(jnp.sum(qwr * kr, axis=-1, keepdims=True))
                da = jnp.sum(vr * dor[i:i + 1, :], axis=-1, keepdims=True)
                dv = dv + a * dor[i:i + 1, :]
                dk_x = dk_x + da * qw
                dk_i = dk_i + _bf16_round(da) * qwr
                dq_i = jnp.where(rows == i, jnp.sum(da * (w * k), axis=0, keepdims=True), dq_i)
            dqs = dqh * eb + dq_i
            dk = dkd * ekl + dk_i
            db_last = jnp.sum(dkd * kd, axis=0, keepdims=True) + ebl * jnp.sum(dsn * st, axis=0, keepdims=True)
            db = q * dqs - k * (dkd * ekl + dk_x) + jnp.where(rows == GC - 1, db_last, 0.0)
            dq_ref[pl.ds(r0, GC), :] = (dqs * scale).astype(BF16)
            dk_ref[pl.ds(r0, GC), :] = dk.astype(BF16)
            dv_ref[pl.ds(r0, GC), :] = dv.astype(BF16)
            db_ref[pl.ds(r0, GC), :] = db
            ds_scr[...] = dsn * ebl + _dot(dob, qh.astype(BF16), TN)
            return carry

        lax.fori_loop(0, ncb, chunk, 0)

    rev = lambda i: nb - 1 - i
    return pl.pallas_call(
        body, name="gla_bwd", grid=(GLA_HEADS, nb),
        in_specs=[pl.BlockSpec((r, GLA_DK), lambda h, i: (rev(i), C_QG // GLA_DK + h)),
                  pl.BlockSpec((r, GLA_DK), lambda h, i: (rev(i), C_KG // GLA_DK + h)),
                  pl.BlockSpec((r, GLA_DV), lambda h, i: (rev(i), C_VG // GLA_DV + h)),
                  pl.BlockSpec((r, GLA_DK), lambda h, i: (rev(i), h)),
                  pl.BlockSpec((r, GLA_DV), lambda h, i: (rev(i), h)),
                  pl.BlockSpec((ncb, None, GLA_DV, GLA_DK), lambda h, i: (rev(i), h, 0, 0))],
        out_specs=[pl.BlockSpec((r, GLA_DK), lambda h, i: (rev(i), h)),
                   pl.BlockSpec((r, GLA_DK), lambda h, i: (rev(i), h)),
                   pl.BlockSpec((r, GLA_DV), lambda h, i: (rev(i), h)),
                   pl.BlockSpec((r, GLA_DK), lambda h, i: (rev(i), h))],
        out_shape=[jax.ShapeDtypeStruct((t, GLA_KW), BF16), jax.ShapeDtypeStruct((t, GLA_KW), BF16),
                   jax.ShapeDtypeStruct((t, D_GLA), BF16), jax.ShapeDtypeStruct((t, GLA_KW), F32)],
        scratch_shapes=[pltpu.VMEM((GLA_DV, GLA_DK), F32)],
        compiler_params=_cparams(("parallel", "arbitrary")),
    )(proj, proj, proj, bloc, do_raw, states)


def _gate_bwd(db, dgdz, proj, w2p, triu):
    t = db.shape[0]
    r = triu.shape[0]

    def body(db_ref, s_ref, lo_ref, w_ref, u_ref, dlo_ref, gw_ref, gb_ref):
        @pl.when(pl.program_id(0) == 0)
        def _():
            gw_ref[...] = jnp.zeros_like(gw_ref)
            gb_ref[...] = jnp.zeros_like(gb_ref)

        dz = _dot(u_ref[...], db_ref[...], NN, precision=lax.Precision.HIGHEST) * s_ref[...]
        dzb = dz.astype(BF16)
        gb_ref[...] += jnp.sum(dz, axis=0, keepdims=True)
        gw_ref[...] += _dot(lo_ref[...].astype(BF16), dzb, TN)
        dlo_ref[...] = _dot(dzb, w_ref[...], NT).astype(BF16)

    return pl.pallas_call(
        body, name="gate_bwd", grid=(t // r,),
        in_specs=[pl.BlockSpec((r, GLA_KW), lambda i: (i, 0)),
                  pl.BlockSpec((r, GLA_KW), lambda i: (i, 0)),
                  pl.BlockSpec((r, LANES), lambda i: (i, C_LO // LANES)),
                  pl.BlockSpec((LANES, GLA_KW), lambda i: (0, 0)),
                  pl.BlockSpec((r, r), lambda i: (0, 0))],
        out_specs=[pl.BlockSpec((r, LANES), lambda i: (i, 0)),
                   pl.BlockSpec((LANES, GLA_KW), lambda i: (0, 0)),
                   pl.BlockSpec((1, GLA_KW), lambda i: (0, 0))],
        out_shape=[jax.ShapeDtypeStruct((t, LANES), BF16), jax.ShapeDtypeStruct((LANES, GLA_KW), F32),
                   jax.ShapeDtypeStruct((1, GLA_KW), F32)],
        compiler_params=_cparams(("arbitrary",)),
    )(db, dgdz, proj, w2p, triu)


def _swa_bwd(proj, dcat, sinks, dep=None):
    t = proj.shape[0]

    deps, dep_specs = _dep_operand(dep)

    def body(sink_ref, q_ref, kp_ref, kc_ref, vp_ref, vc_ref, d_ref, *rest):
        dq_ref, dk_ref, dv_ref, dsink_ref = rest[len(deps):]
        i = pl.program_id(0)

        @pl.when(i == 0)
        def _():
            dk_ref[...] = jnp.zeros_like(dk_ref)
            dv_ref[...] = jnp.zeros_like(dv_ref)
            dsink_ref[...] = jnp.zeros_like(dsink_ref)

        valid, lo = _swa_masks(i, SWA_GROUP)
        lane = lax.broadcasted_iota(jnp.int32, (1, LANES), 1)
        kb = jnp.concatenate([kp_ref[...], kc_ref[...]], axis=0)
        vb = jnp.concatenate([vp_ref[...], vc_ref[...]], axis=0)
        dsink = jnp.zeros((1, LANES), F32)
        folded_k, folded_v = [], []
        for kv in range(2):
            kdup = _dup_half(kb, lo, kv).astype(BF16)
            vdup = _dup_half(vb, lo, kv).astype(BF16)
            qm = _swa_stack(q_ref, 0, lo, kv)
            dom = _swa_stack(d_ref, 0, lo, kv)
            pn, psink = _swa_probs(qm, kdup, valid, _swa_sinks(sink_ref, kv))
            dpr = _dot(dom, vdup, NT)
            drow = jnp.sum(dpr * pn, axis=-1, keepdims=True)
            ds_col = psink * drow
            for h in range(SWA_GROUP):
                dsink = dsink + jnp.where(lane == SWA_GROUP * kv + h,
                                          -jnp.sum(ds_col[h * WINDOW:(h + 1) * WINDOW, :]), 0.0)
            dsb = (pn * (dpr - drow) * (SWA_DH ** -0.5)).astype(BF16)
            dq_all = _dot(dsb, kdup, NN)
            for p in range(4):
                c0 = LANES * (4 * kv + p)
                dq_ref[:, c0:c0 + LANES] = _swa_unstack(dq_all, lo, p).astype(BF16)
            dkd = _dot(dsb, qm, TN)
            dvd = _dot(pn.astype(BF16), dom, TN)
            folded_k.append(dkd + pltpu.roll(dkd, SWA_DH, axis=1))
            folded_v.append(dvd + pltpu.roll(dvd, SWA_DH, axis=1))
        dkb = jnp.where(lo, folded_k[0], folded_k[1])
        dvb = jnp.where(lo, folded_v[0], folded_v[1])
        dsink_ref[...] += dsink
        cur = pl.ds(pl.multiple_of(i * WINDOW, WINDOW), WINDOW)
        dk_ref[cur, :] += dkb[WINDOW:, :]
        dv_ref[cur, :] += dvb[WINDOW:, :]

        @pl.when(i > 0)
        def _():
            prev = pl.ds(pl.multiple_of((i - 1) * WINDOW, WINDOW), WINDOW)
            dk_ref[prev, :] += dkb[:WINDOW, :]
            dv_ref[prev, :] += dvb[:WINDOW, :]

    kvspec = lambda col, prev: pl.BlockSpec(
        (WINDOW, LANES), (lambda i: (jnp.maximum(i - 1, 0), col)) if prev else (lambda i: (i, col)))
    full = pl.BlockSpec((t, LANES), lambda i: (0, 0))
    return pl.pallas_call(
        body, name="swa_bwd", grid=(t // WINDOW,),
        in_specs=[pl.BlockSpec(memory_space=pltpu.SMEM),
                  pl.BlockSpec((WINDOW, D_SWA), lambda i: (i, C_QS // D_SWA)),
                  kvspec(C_KS // LANES, True), kvspec(C_KS // LANES, False),
                  kvspec(C_VS // LANES, True), kvspec(C_VS // LANES, False),
                  pl.BlockSpec((WINDOW, D_SWA), lambda i: (i, 1)), *dep_specs],
        out_specs=[pl.BlockSpec((WINDOW, D_SWA), lambda i: (i, 0)), full, full,
                   pl.BlockSpec((1, LANES), lambda i: (0, 0))],
        out_shape=[jax.ShapeDtypeStruct((t, D_SWA), BF16), jax.ShapeDtypeStruct((t, LANES), F32),
                   jax.ShapeDtypeStruct((t, LANES), F32), jax.ShapeDtypeStruct((1, LANES), F32)],
        compiler_params=_cparams(("arbitrary",)),
    )(sinks, proj, proj, proj, proj, proj, dcat, *deps)


def _grad_w_in(xb, dproj):
    t = xb.shape[0]
    tm, tn = 896, 1024
    return _matmul(
        "grad_w_in", dproj, xb, TN, (D_INP // tm, D_MODEL // tn, 1),
        pl.BlockSpec((t, tm), lambda i, j, k: (0, i)),
        pl.BlockSpec((t, tn), lambda i, j, k: (0, j)),
        jax.ShapeDtypeStruct((D_IN, D_MODEL), F32),
        pl.BlockSpec((tm, tn), lambda i, j, k: (i, j)), (tm, tn))


def _grad_x(dproj, w_in_t, dr1, dep=None):
    t = dproj.shape[0]
    tm, tn = min(512, t), 1024

    def epilogue(acc_ref, extra, outs):
        for r0, n in _row_chunks(tm):
            rs = slice(r0, r0 + n)
            outs[0][rs, :] = ALPHA * extra[0][rs, :] + acc_ref[rs, :]

    blk = pl.BlockSpec((tm, tn), lambda j, i, k: (i, j))
    return _matmul(
        "grad_x", dproj, w_in_t, NN, (D_MODEL // tn, t // tm, 1),
        pl.BlockSpec((tm, D_INP), lambda j, i, k: (i, 0)),
        pl.BlockSpec((D_INP, tn), lambda j, i, k: (0, j)),
        jax.ShapeDtypeStruct((t, D_MODEL), F32), blk, (tm, tn),
        extra=(dr1,), extra_specs=(blk,), epilogue=epilogue, dep=dep)


def _place():
    x, y, c = lax.axis_index("x"), lax.axis_index("y"), lax.axis_index("c")
    chips = [(1 - x, y), (x, 1 - y), (1 - x, 1 - y)]
    return x, y, c, chips


def _plan_gather_out(src, land, x, y, c, chips):
    me = 4 * x + 2 * y + c
    return [(src, land.at[me], to) for to in [(x, y, 1 - c)] + [(px, py, c) for px, py in chips]]


def _plan_gather_forward(src, land, x, y, c, chips):
    return [(land.at[4 * px + 2 * py + c], land.at[4 * px + 2 * py + c], (x, y, 1 - c)) for px, py in chips]


def _plan_sibling(src, land, x, y, c, chips):
    return [(src.at[2 * q + (1 - c)], land.at[q], (x, y, 1 - c)) for q in range(4)]


def _plan_chips(src, land, x, y, c, chips):
    return [(src.at[j], land.at[j], (px, py, c)) for j, (px, py) in enumerate(chips)]


_PLAN_COPIES = {_plan_gather_out: 4, _plan_gather_forward: 3, _plan_sibling: 4, _plan_chips: 3}
_HBM = pl.BlockSpec(memory_space=pltpu.HBM)
_SEM = pl.BlockSpec(memory_space=pltpu.SEMAPHORE)
_EFFECT = pltpu.SideEffectType.DATAFLOW_SIDE_EFFECTING


def _hbm(a):
    return pltpu.with_memory_space_constraint(a, pltpu.HBM)


def _plan_descriptors(plans, srcs, lands, send, recv):
    x, y, c, chips = _place()
    cps = []
    for plan, src, land in zip(plans, srcs, lands):
        for s_ref, d_ref, to in plan(src, land, x, y, c, chips):
            k = len(cps)
            cps.append(pltpu.make_async_remote_copy(src_ref=s_ref, dst_ref=d_ref, send_sem=send.at[k],
                                                    recv_sem=recv.at[k], device_id=to, device_id_type=MESH))
    return cps


def _copies_start(name, plans, srcs, lands, after=None):
    has_src = [s is not None for s in srcs]
    arrays = [s for s in srcs if s is not None] + list(lands)
    n_src = sum(has_src)
    n_cp = sum(_PLAN_COPIES[p] for p in plans)
    afters = [] if after is None else [after]

    def body(*refs):
        ins = refs[:len(arrays)]
        send, recv = refs[len(arrays) + len(afters)], refs[len(arrays) + len(afters) + 1]
        token = refs[-1]
        it = iter(ins[:n_src])
        src_refs = [next(it) if h else None for h in has_src]
        for cp in _plan_descriptors(plans, src_refs, ins[n_src:], send, recv):
            cp.start()
        token[...] = jnp.zeros_like(token)

    outs = pl.pallas_call(
        body, name=name,
        in_specs=[_HBM] * len(arrays) + [pl.BlockSpec(memory_space=pl.ANY)] * len(afters),
        out_specs=(_SEM, _SEM, *[_HBM] * len(arrays), pl.BlockSpec(memory_space=pltpu.VMEM)),
        out_shape=(pltpu.SemaphoreType.DMA((n_cp,)), pltpu.SemaphoreType.DMA((n_cp,)),
                   *[pltpu.HBM(a.shape, a.dtype) for a in arrays], jax.ShapeDtypeStruct((8, LANES), F32)),
        input_output_aliases={i: 2 + i for i in range(len(arrays))},
        compiler_params=pltpu.CompilerParams(has_side_effects=_EFFECT),
    )(*[_hbm(a) for a in arrays], *afters)
    send, recv = outs[0], outs[1]
    thru = list(outs[2:-1])
    it = iter(thru[:n_src])
    return send, recv, [next(it) if h else None for h in has_src], thru[n_src:], outs[-1]


def _copies_wait(name, plans, started, after):
    send, recv, srcs, lands, _ = started
    has_src = [s is not None for s in srcs]
    arrays = [s for s in srcs if s is not None] + list(lands)
    n_src = sum(has_src)

    def body(*refs):
        ins = refs[:len(arrays)]
        send_ref, recv_ref = refs[len(arrays)], refs[len(arrays) + 1]
        it = iter(ins[:n_src])
        src_refs = [next(it) if h else None for h in has_src]
        for cp in _plan_descriptors(plans, src_refs, ins[n_src:], send_ref, recv_ref):
            cp.wait_send()
            cp.wait_recv()

    outs = pl.pallas_call(
        body, name=name,
        in_specs=[_HBM] * len(arrays) + [_SEM, _SEM, pl.BlockSpec(memory_space=pl.ANY)],
        out_specs=tuple([_HBM] * len(arrays)),
        out_shape=tuple(pltpu.HBM(a.shape, a.dtype) for a in arrays),
        input_output_aliases={i: i for i in range(len(arrays))},
        compiler_params=pltpu.CompilerParams(has_side_effects=_EFFECT),
    )(*arrays, send, recv, after)
    return list(outs[:n_src]), list(outs[n_src:])


def _shard_tiles(rows, cols, tr):
    if rows % tr == 0:
        return (tr, cols), rows // tr, lambda r: (r, 0)
    tc = 2 * LANES
    return (rows, tc), cols // tc, lambda r: (0, r)


def _pair_sum(name, grad, from_sibling, blocks):
    _, rows, cols = grad.shape
    (br_, bc), steps, at = _shard_tiles(rows, cols, 256)

    def body(blk_ref, g_ref, s_ref, o_ref):
        o_ref[...] = (g_ref[...] + s_ref[...]).astype(BF16)

    return pl.pallas_call(
        body, name=name,
        grid_spec=pltpu.PrefetchScalarGridSpec(
            num_scalar_prefetch=1, grid=(3, steps),
            in_specs=[pl.BlockSpec((None, br_, bc), lambda j, r, br: (br[j], *at(r))),
                      pl.BlockSpec((None, br_, bc), lambda j, r, br: (br[3 + j], *at(r)))],
            out_specs=pl.BlockSpec((None, br_, bc), lambda j, r, br: (j, *at(r)))),
        out_shape=jax.ShapeDtypeStruct((3, rows, cols), BF16),
        compiler_params=_cparams(("parallel", "parallel")),
    )(blocks, grad, from_sibling)


def _adam_math(g, w, m, v):
    m2 = ADAM_B1 * m + (1.0 - ADAM_B1) * g
    v2 = ADAM_B2 * v + (1.0 - ADAM_B2) * (g * g)
    m_hat = m2 / (1.0 - ADAM_B1 ** ADAM_STEP)
    v_hat = v2 / (1.0 - ADAM_B2 ** ADAM_STEP)
    delta = -ADAM_LR * (m_hat / (jnp.sqrt(v_hat) + ADAM_EPS) + ADAM_WD * w)
    return delta, m2, v2


def _sum_adam(name, grad, from_sibling, from_chips, own, w, m, v, dep=None):
    rows, cols = w.shape
    (br_, bc), steps, at = _shard_tiles(rows, cols, 128)

    deps, dep_specs = _dep_operand(dep)

    def body(own_ref, p_ref, s_ref, r_ref, w_ref, m_ref, v_ref, *rest):
        g_out, d_out, m_out, v_out = rest[len(deps):]
        g = p_ref[...] + s_ref[...]
        for j in range(3):
            g = g + r_ref[j].astype(F32)
        d, m2, v2 = _adam_math(g, w_ref[...], m_ref[...], v_ref[...])
        g_out[...] = g
        d_out[...] = d
        m_out[...] = m2
        v_out[...] = v2

    blk = pl.BlockSpec((br_, bc), lambda r, cr: at(r))
    shp = jax.ShapeDtypeStruct((rows, cols), F32)
    return pl.pallas_call(
        body, name=name,
        grid_spec=pltpu.PrefetchScalarGridSpec(
            num_scalar_prefetch=1, grid=(steps,),
            in_specs=[pl.BlockSpec((None, br_, bc), lambda r, cr: (cr[0], *at(r))),
                      pl.BlockSpec((None, br_, bc), lambda r, cr: (cr[1], *at(r))),
                      pl.BlockSpec((3, br_, bc), lambda r, cr: (0, *at(r))),
                      blk, blk, blk, *dep_specs],
            out_specs=[blk, blk, blk, blk]),
        out_shape=[shp, shp, shp, shp],
        compiler_params=_cparams(("parallel",)),
    )(own, grad, from_sibling, from_chips, w, m, v, *deps)


def _adam_small(name, g, w, m, v):
    def body(g_ref, w_ref, m_ref, v_ref, d_out, m_out, v_out):
        d, m2, v2 = _adam_math(g_ref[...], w_ref[...], m_ref[...], v_ref[...])
        d_out[...] = d
        m_out[...] = m2
        v_out[...] = v2

    shp = jax.ShapeDtypeStruct(w.shape, F32)
    return pl.pallas_call(body, name=name, out_shape=[shp, shp, shp])(g, w, m, v)


def _all_reduce_small(pack):
    rows = pack.shape[0]

    def body(in_ref, out_ref, slots, send, recv):
        x, y, c, _ = _place()
        me = 4 * x + 2 * y + c
        slots[me] = in_ref[...]
        cps = []
        for k in range(1, N_DEV):
            dx, dy, dc = (k >> 2) & 1, (k >> 1) & 1, k & 1
            to = (jnp.bitwise_xor(x, dx), jnp.bitwise_xor(y, dy), jnp.bitwise_xor(c, dc))
            cps.append(pltpu.make_async_remote_copy(
                src_ref=in_ref, dst_ref=slots.at[me], send_sem=send.at[k - 1], recv_sem=recv.at[k - 1],
                device_id=to, device_id_type=MESH))
        for cp in cps:
            cp.start()
        for cp in cps:
            cp.wait()
        acc = slots[0]
        for d in range(1, N_DEV):
            acc = acc + slots[d]
        out_ref[...] = acc

    return pl.pallas_call(
        body, name="all_reduce_small",
        in_specs=[pl.BlockSpec(memory_space=pltpu.VMEM)],
        out_specs=pl.BlockSpec(memory_space=pltpu.VMEM),
        out_shape=jax.ShapeDtypeStruct((rows, LANES), F32),
        scratch_shapes=[pltpu.VMEM((N_DEV, rows, LANES), F32),
                        pltpu.SemaphoreType.DMA((N_DEV - 1,)), pltpu.SemaphoreType.DMA((N_DEV - 1,))],
    )(pack)


def _rows128(a):
    flat = a.reshape(-1)
    padn = (-flat.shape[0]) % (8 * LANES)
    if padn:
        flat = jnp.concatenate([flat, jnp.zeros((padn,), flat.dtype)])
    return flat.reshape(-1, LANES)


def kernel(x, w_in, w_gk2, b_gk, gla_norm_w, swa_sinks, w_out, ln1_g, ln1_b, w_up, w_down, ln2_g, ln2_b, loss_target, m_w_in, m_w_gk2, m_b_gk, m_gla_norm_w, m_swa_sinks, m_w_out, m_ln1_g, m_ln1_b, m_w_up, m_w_down, m_ln2_g, m_ln2_b, v_w_in, v_w_gk2, v_b_gk, v_gla_norm_w, v_swa_sinks, v_w_out, v_ln1_g, v_ln1_b, v_w_up, v_w_down, v_ln2_g, v_ln2_b):
    xc, yc, cc = lax.axis_index("x"), lax.axis_index("y"), lax.axis_index("c")
    me = 4 * xc + 2 * yc + cc

    x2 = x[0]
    t = x2.shape[0]
    target = loss_target[0]

    def land_of(sh):
        return lax.dynamic_update_index_in_dim(lax.empty((N_DEV,) + sh.shape, BF16), sh, me, 0)

    out_plans = [_plan_gather_out] * 3
    fwd_plans = [_plan_gather_forward] * 3
    in_shards = [w_in[0].T.astype(BF16), w_gk2[0].astype(BF16)]
    ag0 = _copies_start("gather_in_start", out_plans[:2], in_shards, [land_of(sh) for sh in in_shards])
    zero = ag0[4][0, 0]
    xb = (x2 + zero).astype(BF16)
    shards = [(w[0] + zero).astype(BF16) for w in (w_out, w_up, w_down)]
    lands = [land_of(sh) for sh in shards]
    _, in_lands = _copies_wait("gather_in_wait", out_plans[:2], ag0, lands[2])
    ag0f = _copies_start("gather_in_forward_start", fwd_plans[:2], [None] * 2, in_lands)
    _, first = _copies_wait("gather_in_forward_wait", fwd_plans[:2], ag0f, ag0f[4])
    w_in_rows = first[0].reshape(D_IN, D_MODEL)
    w_in_t = _to_padded_rows(w_in_rows)
    w_in_o = jnp.concatenate([w_in_rows, jnp.zeros((D_INP - D_IN, D_MODEL), BF16)], axis=0)
    w2 = first[1].transpose(1, 0, 2).reshape(GATE_RANK, GLA_KW)
    w2p = jnp.concatenate([w2, jnp.zeros((LANES - GATE_RANK, GLA_KW), BF16)], axis=0)
    ag1 = _copies_start("gather_out_start", out_plans, shards, lands, first[0])

    proj = _proj(xb, w_in_t, dep=ag1[4])
    rt = min(256, t)
    ii = jnp.arange(rt)
    tri = ((ii[:, None] // GC == ii[None, :] // GC) & (ii[None, :] <= ii[:, None])).astype(F32)
    bloc, dgdz = _gate_fwd(proj, w2p, b_gk, tri)
    o_raw, states = _gla_fwd(proj, bloc)
    gla_out = _gla_norm_fwd(o_raw, proj, gla_norm_w)
    cat = _swa_fwd(proj, swa_sinks[0], gla_out)
    _, lands = _copies_wait("gather_out_wait", out_plans, ag1, cat)
    ag2 = _copies_start("gather_forward_out_start", fwd_plans[:1], [None], lands[:1])
    _, (w_out_g,) = _copies_wait("gather_forward_out_wait", fwd_plans[:1], ag2, ag2[4])
    ag3 = _copies_start("gather_forward_mlp_start", fwd_plans[:2], [None] * 2, lands[1:], w_out_g)
    w_out_f = w_out_g.reshape(D_MODEL, D_MODEL)
    r1, h1, h1b = _mix_ln1(cat, w_out_f, x2, ln1_g, ln1_b, dep=ag3[4])
    _, (w_up_f, w_down_g) = _copies_wait("gather_forward_mlp_wait", fwd_plans[:2], ag3, h1b)
    w_down_f = w_down_g.reshape(D_FF, D_MODEL)
    a_act, hdn = _mlp_up(h1b, w_up_f)
    dr2, dr2b, g_ln2_g, g_ln2_b, loss_part = _mlp_down_loss(hdn, w_down_f, h1, target, ln2_g, ln2_b)

    others = [2 * (1 - xc) + yc, 2 * xc + (1 - yc), 2 * (1 - xc) + (1 - yc)]
    blocks = jnp.stack([2 * q + cc for q in others] + others).astype(jnp.int32)
    own = jnp.stack([me, 2 * xc + yc]).astype(jnp.int32)
    wmv = dict(w_in=[a[0].T for a in (w_in, m_w_in, v_w_in)], w_out=[a[0] for a in (w_out, m_w_out, v_w_out)],
               w_up=[a[0] for a in (w_up, m_w_up, v_w_up)], w_down=[a[0] for a in (w_down, m_w_down, v_w_down)])
    big = {}

    def sib_land(g):
        return lax.empty((4,) + g.shape[1:], F32)

    def chip_land(g):
        return lax.empty((3,) + g.shape[1:], BF16)

    def finish(nm, g, from_sib, from_chips, dep=None):
        outs = _sum_adam("sum_adam_" + nm, g, from_sib, from_chips, own, *wmv[nm], dep=dep)
        big[nm] = [(o.T if nm == "w_in" else o)[None] for o in outs]

    du = _mlp_down_bwd(dr2b, w_down_f, a_act)
    g_down = _grad_w_down(hdn, dr2b)
    sa_down = _copies_start("sibling_down_start", [_plan_sibling], [g_down], [sib_land(g_down)])
    g_up = _grad_w_up(h1b, du, dep=sa_down[4])
    (g_down,), (fs_down,) = _copies_wait("sibling_down_wait", [_plan_sibling], sa_down, g_up)
    p_down = _pair_sum("pair_sum_w_down", g_down, fs_down, blocks)
    sb_down = _copies_start("chips_down_sibling_up_start", [_plan_chips, _plan_sibling], [p_down, g_up],
                            [chip_land(g_down), sib_land(g_up)])
    dr1, dr1b, g_ln1_g, g_ln1_b = _mlp_up_bwd_ln1(du, w_up_f, dr2, r1, ln1_g, dep=sb_down[4])
    dcat = _dcat(dr1b, w_out_f)
    (_, g_up), (fc_down, fs_up) = _copies_wait("chips_down_sibling_up_wait", [_plan_chips, _plan_sibling], sb_down,
                                               dcat)
    finish("w_down", g_down, fs_down, fc_down)
    p_up = _pair_sum("pair_sum_w_up", g_up, fs_up, blocks)
    sb_up = _copies_start("chips_up_start", [_plan_chips], [p_up], [chip_land(g_up)])
    do_raw, dg_g, g_norm_w = _gla_norm_bwd(dcat, o_raw, proj, gla_norm_w, dep=sb_up[4])
    dq_g, dk_g, dv_g, dgk = _gla_bwd(proj, bloc, do_raw, states)
    _, (fc_up,) = _copies_wait("chips_up_wait", [_plan_chips], sb_up, dgk)
    finish("w_up", g_up, fs_up, fc_up)
    dlo, gw2_p, g_b_gk = _gate_bwd(dgk, dgdz, proj, w2p, tri.T)
    dq_s, dk_s, dv_s, g_sinks = _swa_bwd(proj, dcat, swa_sinks[0])
    dproj = jnp.concatenate([dq_g, dk_g, dv_g, dg_g, dlo[:, :GATE_RANK], dq_s, dk_s.astype(BF16), dv_s.astype(BF16),
                             jnp.zeros((t, D_INP - D_IN), BF16)], axis=-1)
    gw_in_t = _grad_w_in(xb, dproj)
    g_in = gw_in_t.reshape(N_DEV, D_IN // N_DEV, D_MODEL)
    sa_in = _copies_start("sibling_in_start", [_plan_sibling], [g_in], [sib_land(g_in)])
    g_out = _grad_w_out(cat, dr1b, dep=sa_in[4])
    (g_in,), (fs_in,) = _copies_wait("sibling_in_wait", [_plan_sibling], sa_in, g_out)
    p_in = _pair_sum("pair_sum_w_in", g_in, fs_in, blocks)
    sb_in = _copies_start("chips_in_sibling_out_start", [_plan_chips, _plan_sibling], [p_in, g_out],
                          [chip_land(g_in), sib_land(g_out)])
    grad_x = _grad_x(dproj, w_in_o, dr1, dep=sb_in[4])
    (_, g_out), (fc_in, fs_out) = _copies_wait("chips_in_sibling_out_wait", [_plan_chips, _plan_sibling], sb_in,
                                               grad_x)
    p_out = _pair_sum("pair_sum_w_out", g_out, fs_out, blocks)
    sb_out = _copies_start("chips_out_start", [_plan_chips], [p_out], [chip_land(g_out)])
    finish("w_in", g_in, fs_in, fc_in, dep=sb_out[4])

    pieces = [loss_part, g_b_gk, g_norm_w, g_sinks[:, :SWA_HEADS], g_ln1_g, g_ln1_b, g_ln2_g, g_ln2_b,
              gw2_p[:GATE_RANK]]
    pack = jnp.concatenate([_rows128(p) for p in pieces], axis=0)
    tot = _all_reduce_small(pack + sb_out[4][:1, :1])
    sizes = [p.size for p in pieces]
    offs = [0]
    for p in pieces:
        offs.append(offs[-1] + _rows128(p).shape[0])
    unpack = lambda i, shape: tot[offs[i]:offs[i + 1]].reshape(-1)[:sizes[i]].reshape(shape)
    loss = tot[0, 0]
    small_names = ["b_gk", "gla_norm_w", "swa_sinks", "ln1_g", "ln1_b", "ln2_g", "ln2_b"]
    small_w = dict(b_gk=(b_gk, m_b_gk, v_b_gk), gla_norm_w=(gla_norm_w, m_gla_norm_w, v_gla_norm_w),
                   swa_sinks=(swa_sinks, m_swa_sinks, v_swa_sinks), ln1_g=(ln1_g, m_ln1_g, v_ln1_g),
                   ln1_b=(ln1_b, m_ln1_b, v_ln1_b), ln2_g=(ln2_g, m_ln2_g, v_ln2_g),
                   ln2_b=(ln2_b, m_ln2_b, v_ln2_b))
    small_g = {nm: unpack(1 + i, small_w[nm][0].shape) for i, nm in enumerate(small_names)}
    g_pack = jnp.concatenate([_rows128(small_g[nm]) for nm in small_names], axis=0)
    small_wmv = [jnp.concatenate([_rows128(small_w[nm][k]) for nm in small_names], axis=0) for k in range(3)]
    small_out = _adam_small("adam_replicated", g_pack, *small_wmv)
    srow = [0]
    for nm in small_names:
        srow.append(srow[-1] + _rows128(small_w[nm][0]).shape[0])
    small = {}
    for i, nm in enumerate(small_names):
        shape = small_w[nm][0].shape
        n = small_w[nm][0].size
        small[nm] = [small_g[nm]] + [o[srow[i]:srow[i + 1]].reshape(-1)[:n].reshape(shape) for o in small_out]

    gw2_full = unpack(8, (GATE_RANK, GLA_KW))
    gw2_loc = lax.dynamic_slice_in_dim(gw2_full, me * (GLA_KW // N_DEV), GLA_KW // N_DEV, axis=1)
    gk2_out = _adam_small("adam_w_gk2", gw2_loc, w_gk2[0], m_w_gk2[0], v_w_gk2[0])
    big["w_gk2"] = [gw2_loc[None]] + [o[None] for o in gk2_out]

    _, (fc_out,) = _copies_wait("chips_out_wait", [_plan_chips], sb_out, tot)
    finish("w_out", g_out, fs_out, fc_out)

    order = ["w_in", "w_gk2", "b_gk", "gla_norm_w", "swa_sinks", "w_out", "ln1_g", "ln1_b", "w_up", "w_down",
             "ln2_g", "ln2_b"]
    res = {**big, **small}
    outs = [loss, grad_x[None]]
    for k in range(4):
        outs += [res[nm][k] for nm in order]
    return tuple(outs)
```

```python
import functools

import jax
import jax.numpy as jnp
from jax import lax
from jax.experimental import pallas as pl
from jax.experimental.pallas import tpu as pltpu

F32 = jnp.float32
BF16 = jnp.bfloat16

N_DEV = 8
D_MODEL = 2048
D_FF = 8192
GLA_HEADS = 4
GLA_DK = 128
GLA_DV = 256
GLA_KW = 512
D_GLA = 1024
GATE_RANK = 16
GATE_TAU = 16.0
SWA_HEADS = 16
SWA_DH = 64
SWA_GROUP = 8
WINDOW = 128
D_SWA = 1024
D_IN = 4368
ALPHA = 2.0 ** 0.25
LN_EPS = 1e-5
RMS_EPS = 1e-5
ADAM_LR = 0.001
ADAM_B1 = 0.9
ADAM_B2 = 0.999
ADAM_EPS = 1e-08
ADAM_WD = 0.01
ADAM_STEP = 10

C_QG, C_KG, C_VG, C_GG, C_QS, C_KS, C_VS, C_LO = 0, 512, 1024, 2048, 3072, 4096, 4224, 4352
D_INP = 4480
LANES = 128
GC = 16
NEG = -1e30

NN = ((1,), (0,))
NT = ((1,), (1,))
TN = ((0,), (0,))

VMEM_LIMIT = 52 * 1024 * 1024
MESH = pl.DeviceIdType.MESH


def _dot(a, b, dn, precision=None):
    return lax.dot_general(a, b, (dn, ((), ())), preferred_element_type=F32, precision=precision)


def _bf16_round(v):
    return v.astype(BF16).astype(F32)


def _cparams(dims):
    return pltpu.CompilerParams(dimension_semantics=dims, vmem_limit_bytes=VMEM_LIMIT)


def _dep_operand(dep):
    if dep is None:
        return (), ()
    return (dep,), (pl.BlockSpec(dep.shape, lambda *_: (0,) * dep.ndim),)


def _to_padded_rows(w):
    pad = jnp.zeros((D_INP - D_IN,) + w.shape[1:], w.dtype)
    return jnp.concatenate([w[:3072], w[3088:], w[3072:3088], pad], axis=0)


def _matmul(name, a, b, dn, grid, a_spec, b_spec, out_shape, out_specs, acc_shape, *,
            extra=(), extra_specs=(), epilogue=None, dims=("parallel", "parallel", "arbitrary"), dep=None):
    nk = grid[2]
    n_extra = len(extra)
    deps, dep_specs = _dep_operand(dep)
    direct = epilogue is None and (nk == 1 or (not isinstance(out_shape, (list, tuple)) and out_shape.dtype == F32))

    scratch = [] if direct or nk == 1 else [pltpu.VMEM(acc_shape, F32)]

    def body(a_ref, b_ref, *rest):
        extra_refs = rest[:n_extra]
        out_refs = rest[n_extra + len(deps):len(rest) - len(scratch)]
        acc_ref = rest[-1] if scratch else out_refs[0]
        part = _dot(a_ref[...].astype(BF16), b_ref[...].astype(BF16), dn)

        def finish():
            if epilogue is None:
                out_refs[0][...] = acc_ref[...].astype(out_refs[0].dtype)
            else:
                epilogue(acc_ref, extra_refs, out_refs)

        if direct and nk == 1:
            out_refs[0][...] = part.reshape(out_refs[0].shape).astype(out_refs[0].dtype)
        elif nk == 1:
            epilogue(part, extra_refs, out_refs)
        else:
            k = pl.program_id(2)

            @pl.when(k == 0)
            def _():
                acc_ref[...] = part

            @pl.when(k > 0)
            def _():
                acc_ref[...] += part

            if not direct:
                @pl.when(k == nk - 1)
                def _():
                    finish()

    return pl.pallas_call(
        body, name=name, grid=grid,
        in_specs=[a_spec, b_spec, *extra_specs, *dep_specs],
        out_specs=out_specs, out_shape=out_shape,
        scratch_shapes=scratch,
        compiler_params=_cparams(dims),
    )(a, b, *extra, *deps)


def _row_chunks(rows, step=128):
    step = min(step, rows)
    return [(r, step) for r in range(0, rows, step)]


def _ln_stats(r):
    mu = jnp.mean(r, axis=-1, keepdims=True)
    xc = r - mu
    var = jnp.mean(xc * xc, axis=-1, keepdims=True)
    rstd = lax.rsqrt(var + LN_EPS)
    return xc * rstd, rstd


def _ln_bwd(dy_g, xhat, rstd):
    m1 = jnp.mean(dy_g, axis=-1, keepdims=True)
    m2 = jnp.mean(dy_g * xhat, axis=-1, keepdims=True)
    return rstd * (dy_g - m1 - xhat * m2)


def _proj(xb, w_in_t, dep=None):
    t = xb.shape[0]
    tm, tn = min(1024, t), 896
    return _matmul(
        "proj", xb, w_in_t, NT, (t // tm, D_INP // tn, 1),
        pl.BlockSpec((tm, D_MODEL), lambda i, j, k: (i, 0)),
        pl.BlockSpec((tn, D_MODEL), lambda i, j, k: (j, 0)),
        jax.ShapeDtypeStruct((t, D_INP), F32),
        pl.BlockSpec((tm, tn), lambda i, j, k: (i, j)),
        (tm, tn), dep=dep)


def _gate_fwd(proj, w2p, b_gk, tri):
    t = proj.shape[0]
    r = tri.shape[0]

    def body(lo_ref, w_ref, b_ref, tri_ref, bloc_ref, dgdz_ref):
        z = _dot(lo_ref[...].astype(BF16), w_ref[...], NN) + b_ref[...]
        e = jnp.exp(-jnp.abs(z))
        gk = (jnp.minimum(z, 0.0) - jnp.log1p(e)) * (1.0 / GATE_TAU)
        inv = 1.0 / (1.0 + e)
        dgdz_ref[...] = jnp.where(z >= 0.0, e * inv, inv) * (1.0 / GATE_TAU)
        bloc_ref[...] = _dot(tri_ref[...], gk, NN, precision=lax.Precision.HIGHEST)

    return pl.pallas_call(
        body, name="gate_fwd", grid=(t // r,),
        in_specs=[pl.BlockSpec((r, LANES), lambda i: (i, C_LO // LANES)),
                  pl.BlockSpec((LANES, GLA_KW), lambda i: (0, 0)),
                  pl.BlockSpec((1, GLA_KW), lambda i: (0, 0)),
                  pl.BlockSpec((r, r), lambda i: (0, 0))],
        out_specs=[pl.BlockSpec((r, GLA_KW), lambda i: (i, 0)),
                   pl.BlockSpec((r, GLA_KW), lambda i: (i, 0))],
        out_shape=[jax.ShapeDtypeStruct((t, GLA_KW), F32), jax.ShapeDtypeStruct((t, GLA_KW), F32)],
        compiler_params=_cparams(("parallel",)),
    )(proj, w2p, b_gk, tri)


def _gla_fwd(proj, bloc):
    t = proj.shape[0]
    r = min(256, t)
    ncb = r // GC
    scale = GLA_DK ** -0.5

    def body(q_ref, k_ref, v_ref, b_ref, o_ref, st_ref, s_scr, m_scr):
        @pl.when(pl.program_id(1) == 0)
        def _():
            s_scr[...] = jnp.zeros_like(s_scr)

        rows = lax.broadcasted_iota(jnp.int32, (GC, 1), 0)
        cols = lax.broadcasted_iota(jnp.int32, (1, GC), 1)

        def increment(c, carry):
            r0 = pl.multiple_of(c * GC, GC)
            k = k_ref[pl.ds(r0, GC), :]
            b = b_ref[pl.ds(r0, GC), :]
            kd = k * jnp.exp(b[GC - 1:GC, :] - b)
            m_scr[c] = _dot(v_ref[pl.ds(r0, GC), :].astype(BF16), kd.astype(BF16), TN)
            return carry

        lax.fori_loop(0, ncb, increment, 0, unroll=8)

        def recur(c, st):
            st_ref[c] = st
            r0 = pl.multiple_of(c * GC, GC)
            bl = b_ref[pl.ds(r0, GC), :][GC - 1:GC, :]
            return st * jnp.exp(bl) + m_scr[c]

        s_scr[...] = lax.fori_loop(0, ncb, recur, s_scr[...])

        def output(c, carry):
            r0 = pl.multiple_of(c * GC, GC)
            q = q_ref[pl.ds(r0, GC), :] * scale
            b = b_ref[pl.ds(r0, GC), :]
            kr = _bf16_round(k_ref[pl.ds(r0, GC), :])
            att = jnp.zeros((GC, GC), F32)
            for j in range(GC):
                w = jnp.exp(jnp.where(rows >= j, b - b[j:j + 1, :], NEG))
                a = jnp.sum(_bf16_round(q * w) * kr[j:j + 1, :], axis=-1, keepdims=True)
                att = jnp.where(cols == j, a, att)
            o_ref[pl.ds(r0, GC), :] = (_dot((q * jnp.exp(b)).astype(BF16), st_ref[c].astype(BF16), NT)
                                       + _dot(att.astype(BF16), v_ref[pl.ds(r0, GC), :].astype(BF16), NN))
            return carry

        lax.fori_loop(0, ncb, output, 0, unroll=8)

    return pl.pallas_call(
        body, name="gla_fwd", grid=(GLA_HEADS, t // r),
        in_specs=[pl.BlockSpec((r, GLA_DK), lambda h, i: (i, C_QG // GLA_DK + h)),
                  pl.BlockSpec((r, GLA_DK), lambda h, i: (i, C_KG // GLA_DK + h)),
                  pl.BlockSpec((r, GLA_DV), lambda h, i: (i, C_VG // GLA_DV + h)),
                  pl.BlockSpec((r, GLA_DK), lambda h, i: (i, h))],
        out_specs=[pl.BlockSpec((r, GLA_DV), lambda h, i: (i, h)),
                   pl.BlockSpec((ncb, None, GLA_DV, GLA_DK), lambda h, i: (i, h, 0, 0))],
        out_shape=[jax.ShapeDtypeStruct((t, D_GLA), F32),
                   jax.ShapeDtypeStruct((t // GC, GLA_HEADS, GLA_DV, GLA_DK), F32)],
        scratch_shapes=[pltpu.VMEM((GLA_DV, GLA_DK), F32), pltpu.VMEM((ncb, GLA_DV, GLA_DK), F32)],
        compiler_params=_cparams(("parallel", "arbitrary")),
    )(proj, proj, proj, bloc)


def _gla_norm_fwd(o_raw, proj, norm_w):
    t = o_raw.shape[0]
    r = min(512, t)

    def body(o_ref, g_ref, w_ref, out_ref):
        w = w_ref[...]
        for h in range(GLA_HEADS):
            sl = slice(h * GLA_DV, (h + 1) * GLA_DV)
            o = o_ref[:, sl]
            g = g_ref[:, sl]
            on = o * lax.rsqrt(jnp.mean(o * o, axis=-1, keepdims=True) + RMS_EPS)
            out_ref[:, sl] = (on * w * (g * jax.nn.sigmoid(g))).astype(BF16)

    return pl.pallas_call(
        body, name="gla_norm_fwd", grid=(t // r,),
        in_specs=[pl.BlockSpec((r, D_GLA), lambda i: (i, 0)),
                  pl.BlockSpec((r, D_GLA), lambda i: (i, C_GG // D_GLA)),
                  pl.BlockSpec((1, GLA_DV), lambda i: (0, 0))],
        out_specs=pl.BlockSpec((r, D_GLA), lambda i: (i, 0)),
        out_shape=jax.ShapeDtypeStruct((t, D_GLA), BF16),
        compiler_params=_cparams(("parallel",)),
    )(o_raw, proj, norm_w)


def _swa_masks(i, heads):
    qi = lax.broadcasted_iota(jnp.int32, (heads * WINDOW, 1), 0) & (WINDOW - 1)
    kj = lax.broadcasted_iota(jnp.int32, (1, 2 * WINDOW), 1)
    valid = (kj > qi) & (kj <= qi + WINDOW) & ((i > 0) | (kj >= WINDOW))
    lo = lax.broadcasted_iota(jnp.int32, (1, LANES), 1) < SWA_DH
    return valid, lo


def _dup_half(x, lo, kv):
    xr = pltpu.roll(x, SWA_DH, axis=1)
    return jnp.where(lo, x, xr) if kv == 0 else jnp.where(lo, xr, x)


def _swa_stack(ref, col0, lo, kv):
    parts = []
    for p in range(4):
        c0 = col0 + LANES * (4 * kv + p)
        xp = ref[:, c0:c0 + LANES]
        parts += [jnp.where(lo, xp, 0.0).astype(BF16), jnp.where(lo, 0.0, xp).astype(BF16)]
    return jnp.concatenate(parts, axis=0)


def _swa_unstack(x_all, lo, p):
    r0 = 2 * p * WINDOW
    return jnp.where(lo, x_all[r0:r0 + WINDOW, :], x_all[r0 + WINDOW:r0 + 2 * WINDOW, :])


def _swa_sinks(sink_ref, kv):
    return jnp.concatenate([jnp.full((WINDOW, 1), sink_ref[SWA_GROUP * kv + h], F32) for h in range(SWA_GROUP)],
                           axis=0)


def _swa_probs(qm, kdup, valid, sink):
    s = _dot(qm, kdup, NT) * (SWA_DH ** -0.5)
    s = jnp.where(valid, s, NEG)
    m = jnp.maximum(jnp.max(s, axis=-1, keepdims=True), sink)
    p = jnp.exp(s - m)
    es = jnp.exp(sink - m)
    inv = 1.0 / (jnp.sum(p, axis=-1, keepdims=True) + es)
    return p * inv, es * inv


def _swa_fwd(proj, sinks, gla_out, dep=None):
    t = proj.shape[0]

    deps, dep_specs = _dep_operand(dep)

    def body(sink_ref, q_ref, kp_ref, kc_ref, vp_ref, vc_ref, gla_ref, *rest):
        o_ref = rest[-1]
        i = pl.program_id(0)
        o_ref[:, :D_GLA] = gla_ref[...]
        valid, lo = _swa_masks(i, 1)
        kb = jnp.concatenate([kp_ref[...], kc_ref[...]], axis=0)
        vb = jnp.concatenate([vp_ref[...], vc_ref[...]], axis=0)
        for kv in range(2):
            kdup = _dup_half(kb, lo, kv).astype(BF16)
            vdup = _dup_half(vb, lo, kv).astype(BF16)
            for p in range(4):
                c0 = LANES * (4 * kv + p)
                qp = q_ref[:, c0:c0 + LANES]
                halves = []
                for e in range(2):
                    qm = jnp.where(lo if e == 0 else ~lo, qp, 0.0).astype(BF16)
                    pn, _ = _swa_probs(qm, kdup, valid, sink_ref[SWA_GROUP * kv + 2 * p + e])
                    halves.append(_dot(pn.astype(BF16), vdup, NN))
                o_ref[:, D_GLA + c0:D_GLA + c0 + LANES] = jnp.where(lo, halves[0], halves[1]).astype(BF16)

    kvspec = lambda col, prev: pl.BlockSpec(
        (WINDOW, LANES), (lambda i: (jnp.maximum(i - 1, 0), col)) if prev else (lambda i: (i, col)))
    return pl.pallas_call(
        body, name="swa_fwd", grid=(t // WINDOW,),
        in_specs=[pl.BlockSpec(memory_space=pltpu.SMEM),
                  pl.BlockSpec((WINDOW, D_SWA), lambda i: (i, C_QS // D_SWA)),
                  kvspec(C_KS // LANES, True), kvspec(C_KS // LANES, False),
                  kvspec(C_VS // LANES, True), kvspec(C_VS // LANES, False),
                  pl.BlockSpec((WINDOW, D_GLA), lambda i: (i, 0)), *dep_specs],
        out_specs=pl.BlockSpec((WINDOW, D_MODEL), lambda i: (i, 0)),
        out_shape=jax.ShapeDtypeStruct((t, D_MODEL), BF16),
        compiler_params=_cparams(("parallel",)),
    )(sinks, proj, proj, proj, proj, proj, gla_out, *deps)


def _mix_ln1(cat, w_out, x, ln_g, ln_b, dep=None):
    t = cat.shape[0]
    tm, tk = min(256, t), D_MODEL

    def epilogue(acc_ref, extra, outs):
        x_ref, g_ref, b_ref = extra
        r1_ref, h1_ref, h1b_ref = outs
        for r0, n in _row_chunks(tm):
            rs = slice(r0, r0 + n)
            r1 = ALPHA * x_ref[rs, :] + acc_ref[rs, :]
            xhat, _ = _ln_stats(r1)
            h = xhat * g_ref[...] + b_ref[...]
            r1_ref[rs, :] = r1
            h1_ref[rs, :] = h
            h1b_ref[rs, :] = h.astype(BF16)

    row = pl.BlockSpec((tm, D_MODEL), lambda i, j, k: (i, 0))
    vec = pl.BlockSpec((1, D_MODEL), lambda i, j, k: (0, 0))
    return _matmul(
        "mix_ln1", cat, w_out, NN, (t // tm, 1, D_MODEL // tk),
        pl.BlockSpec((tm, tk), lambda i, j, k: (i, k)),
        pl.BlockSpec((tk, D_MODEL), lambda i, j, k: (k, 0)),
        [jax.ShapeDtypeStruct((t, D_MODEL), F32), jax.ShapeDtypeStruct((t, D_MODEL), F32),
         jax.ShapeDtypeStruct((t, D_MODEL), BF16)],
        [row, row, row], (tm, D_MODEL),
        extra=(x, ln_g, ln_b), extra_specs=(row, vec, vec), epilogue=epilogue, dep=dep)


def _mlp_up(h1b, w_up):
    t = h1b.shape[0]
    tm, tn = min(1024, t), 1024

    def epilogue(acc_ref, extra, outs):
        a_ref, hdn_ref = outs
        for r0, n in _row_chunks(tm, 256):
            rs = slice(r0, r0 + n)
            a = jnp.maximum(acc_ref[rs, :], 0.0)
            a_ref[rs, :] = a.astype(BF16)
            hdn_ref[rs, :] = (a * a).astype(BF16)

    out = pl.BlockSpec((tm, tn), lambda i, j, k: (i, j))
    return _matmul(
        "mlp_up", h1b, w_up, NN, (t // tm, D_FF // tn, 1),
        pl.BlockSpec((tm, D_MODEL), lambda i, j, k: (i, 0)),
        pl.BlockSpec((None, D_MODEL, tn), lambda i, j, k: (j, 0, 0)),
        [jax.ShapeDtypeStruct((t, D_FF), BF16), jax.ShapeDtypeStruct((t, D_FF), BF16)],
        [out, out], (tm, tn), epilogue=epilogue)


def _mlp_down_loss(hdn, w_down, h1, target, ln_g, ln_b):
    t = hdn.shape[0]
    tm, tn, tk = min(1024, t), 1024, 4096
    ff = _matmul(
        "mlp_down", hdn, w_down, NN, (t // tm, D_MODEL // tn, D_FF // tk),
        pl.BlockSpec((tm, tk), lambda i, j, k: (i, k)),
        pl.BlockSpec((tk, tn), lambda i, j, k: (k, j)),
        jax.ShapeDtypeStruct((t, D_MODEL), F32),
        pl.BlockSpec((tm, tn), lambda i, j, k: (i, j)), (tm, tn))
    r = min(256, t)

    def body(ff_ref, h1_ref, t_ref, g_ref, b_ref, dr2_ref, dr2b_ref, gg_ref, gb_ref, loss_ref):
        @pl.when(pl.program_id(0) == 0)
        def _():
            gg_ref[...] = jnp.zeros_like(gg_ref)
            gb_ref[...] = jnp.zeros_like(gb_ref)
            loss_ref[...] = jnp.zeros_like(loss_ref)

        for r0, n in _row_chunks(r, 64):
            rs = slice(r0, r0 + n)
            xhat, rstd = _ln_stats(ALPHA * h1_ref[rs, :] + ff_ref[rs, :])
            err = xhat * g_ref[...] + b_ref[...] - t_ref[rs, :]
            loss_ref[...] += 0.5 * jnp.sum(jnp.mean(err * err, axis=-1, keepdims=True))
            dy = err * (1.0 / D_MODEL)
            gg_ref[...] += jnp.sum(dy * xhat, axis=0, keepdims=True)
            gb_ref[...] += jnp.sum(dy, axis=0, keepdims=True)
            dr2 = _ln_bwd(dy * g_ref[...], xhat, rstd)
            dr2_ref[rs, :] = dr2
            dr2b_ref[rs, :] = dr2.astype(BF16)

    row = pl.BlockSpec((r, D_MODEL), lambda i: (i, 0))
    vec = pl.BlockSpec((1, D_MODEL), lambda i: (0, 0))
    return pl.pallas_call(
        body, name="ln2_loss", grid=(t // r,),
        in_specs=[row, row, row, vec, vec],
        out_specs=[row, row, vec, vec, pl.BlockSpec((1, LANES), lambda i: (0, 0))],
        out_shape=[jax.ShapeDtypeStruct((t, D_MODEL), F32), jax.ShapeDtypeStruct((t, D_MODEL), BF16),
                   jax.ShapeDtypeStruct((1, D_MODEL), F32), jax.ShapeDtypeStruct((1, D_MODEL), F32),
                   jax.ShapeDtypeStruct((1, LANES), F32)],
        compiler_params=_cparams(("arbitrary",)),
    )(ff, h1, target, ln_g, ln_b)


def _mlp_down_bwd(dr2b, w_down, a_act):
    t = dr2b.shape[0]
    tm, tn = min(1024, t), 1024

    def epilogue(acc_ref, extra, outs):
        (a_ref,) = extra
        for r0, n in _row_chunks(tm, 256):
            rs = slice(r0, r0 + n)
            outs[0][rs, :] = (acc_ref[rs, :] * (2.0 * a_ref[rs, :].astype(F32))).astype(BF16)

    blk = pl.BlockSpec((tm, tn), lambda i, j, k: (i, j))
    return _matmul(
        "mlp_down_bwd", dr2b, w_down, NT, (t // tm, D_FF // tn, 1),
        pl.BlockSpec((tm, D_MODEL), lambda i, j, k: (i, 0)),
        pl.BlockSpec((tn, D_MODEL), lambda i, j, k: (j, 0)),
        jax.ShapeDtypeStruct((t, D_FF), BF16), blk, (tm, tn),
        extra=(a_act,), extra_specs=(blk,), epilogue=epilogue)


def _grad_w_down(hdn, dr2b):
    t = hdn.shape[0]
    tm, tn = 1024, 1024
    return _matmul(
        "grad_w_down", hdn, dr2b, TN, (D_MODEL // tn, D_FF // tm, 1),
        pl.BlockSpec((t, tm), lambda j, i, k: (0, i)),
        pl.BlockSpec((t, tn), lambda j, i, k: (0, j)),
        jax.ShapeDtypeStruct((N_DEV, D_FF // N_DEV, D_MODEL), F32),
        pl.BlockSpec((None, tm, tn), lambda j, i, k: (i, 0, j)), (tm, tn))


def _grad_w_up(h1b, du, dep=None):
    t = h1b.shape[0]
    tm, tn = 1024, 1024
    return _matmul(
        "grad_w_up", h1b, du, TN, (N_DEV, D_MODEL // tm, 1),
        pl.BlockSpec((t, tm), lambda i, j, k: (0, j)),
        pl.BlockSpec((t, tn), lambda i, j, k: (0, i)),
        jax.ShapeDtypeStruct((N_DEV, D_MODEL, D_FF // N_DEV), F32),
        pl.BlockSpec((None, tm, tn), lambda i, j, k: (i, j, 0)), (tm, tn), dep=dep)


def _mlp_up_bwd_ln1(du, w_up, dr2, r1, ln_g, dep=None):
    t = du.shape[0]
    tm, tn, tk, nb = min(1024, t), 1024, D_FF // N_DEV, 4
    deps, dep_specs = _dep_operand(dep)

    def mm_body(a_ref, b_ref, *rest):
        o_ref = rest[-1]
        k = pl.program_id(2)
        part = _dot(a_ref[:, :tk], b_ref[0], NT)
        for d in range(1, nb):
            part = part + _dot(a_ref[:, d * tk:(d + 1) * tk], b_ref[d], NT)

        @pl.when(k == 0)
        def _():
            o_ref[...] = part

        @pl.when(k > 0)
        def _():
            o_ref[...] += part

    dff = pl.pallas_call(
        mm_body, name="mlp_up_bwd", grid=(t // tm, D_MODEL // tn, N_DEV // nb),
        in_specs=[pl.BlockSpec((tm, nb * tk), lambda i, j, k: (i, k)),
                  pl.BlockSpec((nb, tn, tk), lambda i, j, k: (k, j, 0)), *dep_specs],
        out_specs=pl.BlockSpec((tm, tn), lambda i, j, k: (i, j)),
        out_shape=jax.ShapeDtypeStruct((t, D_MODEL), F32),
        compiler_params=_cparams(("parallel", "parallel", "arbitrary")),
    )(du, w_up, *deps)
    r = min(256, t)

    def body(acc_ref, dr2_ref, r1_ref, g_ref, dr1_ref, dr1b_ref, gg_ref, gb_ref):
        @pl.when(pl.program_id(0) == 0)
        def _():
            gg_ref[...] = jnp.zeros_like(gg_ref)
            gb_ref[...] = jnp.zeros_like(gb_ref)

        for r0, n in _row_chunks(r, 64):
            rs = slice(r0, r0 + n)
            dh1 = ALPHA * dr2_ref[rs, :] + acc_ref[rs, :]
            xhat, rstd = _ln_stats(r1_ref[rs, :])
            gg_ref[...] += jnp.sum(dh1 * xhat, axis=0, keepdims=True)
            gb_ref[...] += jnp.sum(dh1, axis=0, keepdims=True)
            dr1 = _ln_bwd(dh1 * g_ref[...], xhat, rstd)
            dr1_ref[rs, :] = dr1
            dr1b_ref[rs, :] = dr1.astype(BF16)

    row = pl.BlockSpec((r, D_MODEL), lambda i: (i, 0))
    vec = pl.BlockSpec((1, D_MODEL), lambda i: (0, 0))
    return pl.pallas_call(
        body, name="ln1_bwd", grid=(t // r,),
        in_specs=[row, row, row, vec],
        out_specs=[row, row, vec, vec],
        out_shape=[jax.ShapeDtypeStruct((t, D_MODEL), F32), jax.ShapeDtypeStruct((t, D_MODEL), BF16),
                   jax.ShapeDtypeStruct((1, D_MODEL), F32), jax.ShapeDtypeStruct((1, D_MODEL), F32)],
        compiler_params=_cparams(("arbitrary",)),
    )(dff, dr2, r1, ln_g)


def _dcat(dr1b, w_out):
    t = dr1b.shape[0]
    tm, tn = min(1024, t), 1024
    return _matmul(
        "dcat", dr1b, w_out, NT, (t // tm, D_MODEL // tn, 1),
        pl.BlockSpec((tm, D_MODEL), lambda i, j, k: (i, 0)),
        pl.BlockSpec((tn, D_MODEL), lambda i, j, k: (j, 0)),
        jax.ShapeDtypeStruct((t, D_MODEL), F32),
        pl.BlockSpec((tm, tn), lambda i, j, k: (i, j)), (tm, tn))


def _grad_w_out(cat, dr1b, dep=None):
    t = cat.shape[0]
    tm, tn = 1024, 1024
    return _matmul(
        "grad_w_out", cat, dr1b, TN, (D_MODEL // tm, D_MODEL // tn, 1),
        pl.BlockSpec((t, tm), lambda i, j, k: (0, i)),
        pl.BlockSpec((t, tn), lambda i, j, k: (0, j)),
        jax.ShapeDtypeStruct((N_DEV, D_MODEL // N_DEV, D_MODEL), F32),
        pl.BlockSpec((tm // (D_MODEL // N_DEV), D_MODEL // N_DEV, tn), lambda i, j, k: (i, 0, j)), (tm, tn), dep=dep)


def _gla_norm_bwd(dcat, o_raw, proj, norm_w, dep=None):
    t = o_raw.shape[0]
    r = min(512, t)

    deps, dep_specs = _dep_operand(dep)

    def body(d_ref, o_ref, g_ref, w_ref, *rest):
        do_ref, dg_ref, dw_ref = rest[len(deps):]

        @pl.when(pl.program_id(0) == 0)
        def _():
            dw_ref[...] = jnp.zeros_like(dw_ref)

        w = w_ref[...]
        dw = jnp.zeros((1, GLA_DV), F32)
        for h in range(GLA_HEADS):
            sl = slice(h * GLA_DV, (h + 1) * GLA_DV)
            o = o_ref[:, sl]
            g = g_ref[:, sl]
            d = d_ref[:, sl]
            rr = lax.rsqrt(jnp.mean(o * o, axis=-1, keepdims=True) + RMS_EPS)
            on = o * rr
            sg = jax.nn.sigmoid(g)
            sil = g * sg
            dg_ref[:, sl] = (d * on * w * (sg * (1.0 + g * (1.0 - sg)))).astype(BF16)
            dw = dw + jnp.sum(d * on * sil, axis=0, keepdims=True)
            don = d * w * sil
            do_ref[:, sl] = rr * (don - on * jnp.mean(don * on, axis=-1, keepdims=True))
        dw_ref[...] += dw

    return pl.pallas_call(
        body, name="gla_norm_bwd", grid=(t // r,),
        in_specs=[pl.BlockSpec((r, D_GLA), lambda i: (i, 0)),
                  pl.BlockSpec((r, D_GLA), lambda i: (i, 0)),
                  pl.BlockSpec((r, D_GLA), lambda i: (i, C_GG // D_GLA)),
                  pl.BlockSpec((1, GLA_DV), lambda i: (0, 0)), *dep_specs],
        out_specs=[pl.BlockSpec((r, D_GLA), lambda i: (i, 0)),
                   pl.BlockSpec((r, D_GLA), lambda i: (i, 0)),
                   pl.BlockSpec((1, GLA_DV), lambda i: (0, 0))],
        out_shape=[jax.ShapeDtypeStruct((t, D_GLA), F32), jax.ShapeDtypeStruct((t, D_GLA), BF16),
                   jax.ShapeDtypeStruct((1, GLA_DV), F32)],
        compiler_params=_cparams(("arbitrary",)),
    )(dcat, o_raw, proj, norm_w, *deps)


def _gla_bwd(proj, bloc, do_raw, states):
    t = proj.shape[0]
    r = min(256, t)
    ncb = r // GC
    nb = t // r
    scale = GLA_DK ** -0.5

    def body(q_ref, k_ref, v_ref, b_ref, do_ref, st_ref, dq_ref, dk_ref, dv_ref, db_ref, ds_scr):
        @pl.when(pl.program_id(1) == 0)
        def _():
            ds_scr[...] = jnp.zeros_like(ds_scr)

        rows = lax.broadcasted_iota(jnp.int32, (GC, 1), 0)

        def chunk(cc, carry):
            c = ncb - 1 - cc
            r0 = pl.multiple_of(c * GC, GC)
            q = q_ref[pl.ds(r0, GC), :] * scale
            k = k_ref[pl.ds(r0, GC), :]
            v = v_ref[pl.ds(r0, GC), :]
            b = b_ref[pl.ds(r0, GC), :]
            do = do_ref[pl.ds(r0, GC), :]
            st = st_ref[c]
            dsn = ds_scr[...]
            bl = b[GC - 1:GC, :]
            eb = jnp.exp(b)
            ekl = jnp.exp(bl - b)
            ebl = jnp.exp(bl)
            qh = q * eb
            kd = k * ekl
            dob = do.astype(BF16)
            dsb = dsn.astype(BF16)
            dqh = _dot(dob, st.astype(BF16), NN)
            dkd = _dot(v.astype(BF16), dsb, NN)
            dv = _dot(kd.astype(BF16), dsb, NT)
            dq_i = jnp.zeros((GC, GLA_DK), F32)
            dk_i = jnp.zeros((GC, GLA_DK), F32)
            dk_x = jnp.zeros((GC, GLA_DK), F32)
            kr = _bf16_round(k)
            vr = _bf16_round(v)
            dor = _bf16_round(do)
            for i in range(GC):
                w = jnp.exp(jnp.where(rows <= i, b[i:i + 1, :] - b, NEG))
                qw = q[i:i + 1, :] * w
                qwr = _bf16_round(qw)
                a = _bf16_round(jnp.sum(qwr * kr, axis=-1, keepdims=True))
                da = jnp.sum(vr * dor[i:i + 1, :], axis=-1, keepdims=True)
                dv = dv + a * dor[i:i + 1, :]
                dk_x = dk_x + da * qw
                dk_i = dk_i + _bf16_round(da) * qwr
                dq_i = jnp.where(rows == i, jnp.sum(da * (w * k), axis=0, keepdims=True), dq_i)
            dqs = dqh * eb + dq_i
            dk = dkd * ekl + dk_i
            db_last = jnp.sum(dkd * kd, axis=0, keepdims=True) + ebl * jnp.sum(dsn * st, axis=0, keepdims=True)
            db = q * dqs - k * (dkd * ekl + dk_x) + jnp.where(rows == GC - 1, db_last, 0.0)
            dq_ref[pl.ds(r0, GC), :] = (dqs * scale).astype(BF16)
            dk_ref[pl.ds(r0, GC), :] = dk.astype(BF16)
            dv_ref[pl.ds(r0, GC), :] = dv.astype(BF16)
            db_ref[pl.ds(r0, GC), :] = db
            ds_scr[...] = dsn * ebl + _dot(dob, qh.astype(BF16), TN)
            return carry

        lax.fori_loop(0, ncb, chunk, 0)

    rev = lambda i: nb - 1 - i
    return pl.pallas_call(
        body, name="gla_bwd", grid=(GLA_HEADS, nb),
        in_specs=[pl.BlockSpec((r, GLA_DK), lambda h, i: (rev(i), C_QG // GLA_DK + h)),
                  pl.BlockSpec((r, GLA_DK), lambda h, i: (rev(i), C_KG // GLA_DK + h)),
                  pl.BlockSpec((r, GLA_DV), lambda h, i: (rev(i), C_VG // GLA_DV + h)),
                  pl.BlockSpec((r, GLA_DK), lambda h, i: (rev(i), h)),
                  pl.BlockSpec((r, GLA_DV), lambda h, i: (rev(i), h)),
                  pl.BlockSpec((ncb, None, GLA_DV, GLA_DK), lambda h, i: (rev(i), h, 0, 0))],
        out_specs=[pl.BlockSpec((r, GLA_DK), lambda h, i: (rev(i), h)),
                   pl.BlockSpec((r, GLA_DK), lambda h, i: (rev(i), h)),
                   pl.BlockSpec((r, GLA_DV), lambda h, i: (rev(i), h)),
                   pl.BlockSpec((r, GLA_DK), lambda h, i: (rev(i), h))],
        out_shape=[jax.ShapeDtypeStruct((t, GLA_KW), BF16), jax.ShapeDtypeStruct((t, GLA_KW), BF16),
                   jax.ShapeDtypeStruct((t, D_GLA), BF16), jax.ShapeDtypeStruct((t, GLA_KW), F32)],
        scratch_shapes=[pltpu.VMEM((GLA_DV, GLA_DK), F32)],
        compiler_params=_cparams(("parallel", "arbitrary")),
    )(proj, proj, proj, bloc, do_raw, states)


def _gate_bwd(db, dgdz, proj, w2p, triu):
    t = db.shape[0]
    r = triu.shape[0]

    def body(db_ref, s_ref, lo_ref, w_ref, u_ref, dlo_ref, gw_ref, gb_ref):
        @pl.when(pl.program_id(0) == 0)
        def _():
            gw_ref[...] = jnp.zeros_like(gw_ref)
            gb_ref[...] = jnp.zeros_like(gb_ref)

        dz = _dot(u_ref[...], db_ref[...], NN, precision=lax.Precision.HIGHEST) * s_ref[...]
        dzb = dz.astype(BF16)
        gb_ref[...] += jnp.sum(dz, axis=0, keepdims=True)
        gw_ref[...] += _dot(lo_ref[...].astype(BF16), dzb, TN)
        dlo_ref[...] = _dot(dzb, w_ref[...], NT).astype(BF16)

    return pl.pallas_call(
        body, name="gate_bwd", grid=(t // r,),
        in_specs=[pl.BlockSpec((r, GLA_KW), lambda i: (i, 0)),
                  pl.BlockSpec((r, GLA_KW), lambda i: (i, 0)),
                  pl.BlockSpec((r, LANES), lambda i: (i, C_LO // LANES)),
                  pl.BlockSpec((LANES, GLA_KW), lambda i: (0, 0)),
                  pl.BlockSpec((r, r), lambda i: (0, 0))],
        out_specs=[pl.BlockSpec((r, LANES), lambda i: (i, 0)),
                   pl.BlockSpec((LANES, GLA_KW), lambda i: (0, 0)),
                   pl.BlockSpec((1, GLA_KW), lambda i: (0, 0))],
        out_shape=[jax.ShapeDtypeStruct((t, LANES), BF16), jax.ShapeDtypeStruct((LANES, GLA_KW), F32),
                   jax.ShapeDtypeStruct((1, GLA_KW), F32)],
        compiler_params=_cparams(("arbitrary",)),
    )(db, dgdz, proj, w2p, triu)


def _swa_bwd(proj, dcat, sinks, dep=None):
    t = proj.shape[0]

    deps, dep_specs = _dep_operand(dep)

    def body(sink_ref, q_ref, kp_ref, kc_ref, vp_ref, vc_ref, d_ref, *rest):
        dq_ref, dk_ref, dv_ref, dsink_ref = rest[len(deps):]
        i = pl.program_id(0)

        @pl.when(i == 0)
        def _():
            dk_ref[...] = jnp.zeros_like(dk_ref)
            dv_ref[...] = jnp.zeros_like(dv_ref)
            dsink_ref[...] = jnp.zeros_like(dsink_ref)

        valid, lo = _swa_masks(i, SWA_GROUP)
        lane = lax.broadcasted_iota(jnp.int32, (1, LANES), 1)
        kb = jnp.concatenate([kp_ref[...], kc_ref[...]], axis=0)
        vb = jnp.concatenate([vp_ref[...], vc_ref[...]], axis=0)
        dsink = jnp.zeros((1, LANES), F32)
        folded_k, folded_v = [], []
        for kv in range(2):
            kdup = _dup_half(kb, lo, kv).astype(BF16)
            vdup = _dup_half(vb, lo, kv).astype(BF16)
            qm = _swa_stack(q_ref, 0, lo, kv)
            dom = _swa_stack(d_ref, 0, lo, kv)
            pn, psink = _swa_probs(qm, kdup, valid, _swa_sinks(sink_ref, kv))
            dpr = _dot(dom, vdup, NT)
            drow = jnp.sum(dpr * pn, axis=-1, keepdims=True)
            ds_col = psink * drow
            for h in range(SWA_GROUP):
                dsink = dsink + jnp.where(lane == SWA_GROUP * kv + h,
                                          -jnp.sum(ds_col[h * WINDOW:(h + 1) * WINDOW, :]), 0.0)
            dsb = (pn * (dpr - drow) * (SWA_DH ** -0.5)).astype(BF16)
            dq_all = _dot(dsb, kdup, NN)
            for p in range(4):
                c0 = LANES * (4 * kv + p)
                dq_ref[:, c0:c0 + LANES] = _swa_unstack(dq_all, lo, p).astype(BF16)
            dkd = _dot(dsb, qm, TN)
            dvd = _dot(pn.astype(BF16), dom, TN)
            folded_k.append(dkd + pltpu.roll(dkd, SWA_DH, axis=1))
            folded_v.append(dvd + pltpu.roll(dvd, SWA_DH, axis=1))
        dkb = jnp.where(lo, folded_k[0], folded_k[1])
        dvb = jnp.where(lo, folded_v[0], folded_v[1])
        dsink_ref[...] += dsink
        cur = pl.ds(pl.multiple_of(i * WINDOW, WINDOW), WINDOW)
        dk_ref[cur, :] += dkb[WINDOW:, :]
        dv_ref[cur, :] += dvb[WINDOW:, :]

        @pl.when(i > 0)
        def _():
            prev = pl.ds(pl.multiple_of((i - 1) * WINDOW, WINDOW), WINDOW)
            dk_ref[prev, :] += dkb[:WINDOW, :]
            dv_ref[prev, :] += dvb[:WINDOW, :]

    kvspec = lambda col, prev: pl.BlockSpec(
        (WINDOW, LANES), (lambda i: (jnp.maximum(i - 1, 0), col)) if prev else (lambda i: (i, col)))
    full = pl.BlockSpec((t, LANES), lambda i: (0, 0))
    return pl.pallas_call(
        body, name="swa_bwd", grid=(t // WINDOW,),
        in_specs=[pl.BlockSpec(memory_space=pltpu.SMEM),
                  pl.BlockSpec((WINDOW, D_SWA), lambda i: (i, C_QS // D_SWA)),
                  kvspec(C_KS // LANES, True), kvspec(C_KS // LANES, False),
                  kvspec(C_VS // LANES, True), kvspec(C_VS // LANES, False),
                  pl.BlockSpec((WINDOW, D_SWA), lambda i: (i, 1)), *dep_specs],
        out_specs=[pl.BlockSpec((WINDOW, D_SWA), lambda i: (i, 0)), full, full,
                   pl.BlockSpec((1, LANES), lambda i: (0, 0))],
        out_shape=[jax.ShapeDtypeStruct((t, D_SWA), BF16), jax.ShapeDtypeStruct((t, LANES), F32),
                   jax.ShapeDtypeStruct((t, LANES), F32), jax.ShapeDtypeStruct((1, LANES), F32)],
        compiler_params=_cparams(("arbitrary",)),
    )(sinks, proj, proj, proj, proj, proj, dcat, *deps)


def _grad_w_in(xb, dproj):
    t = xb.shape[0]
    tm, tn = 896, 1024
    return _matmul(
        "grad_w_in", dproj, xb, TN, (D_INP // tm, D_MODEL // tn, 1),
        pl.BlockSpec((t, tm), lambda i, j, k: (0, i)),
        pl.BlockSpec((t, tn), lambda i, j, k: (0, j)),
        jax.ShapeDtypeStruct((D_IN, D_MODEL), F32),
        pl.BlockSpec((tm, tn), lambda i, j, k: (i, j)), (tm, tn))


def _grad_x(dproj, w_in_t, dr1, dep=None):
    t = dproj.shape[0]
    tm, tn = min(512, t), 1024

    def epilogue(acc_ref, extra, outs):
        for r0, n in _row_chunks(tm):
            rs = slice(r0, r0 + n)
            outs[0][rs, :] = ALPHA * extra[0][rs, :] + acc_ref[rs, :]

    blk = pl.BlockSpec((tm, tn), lambda j, i, k: (i, j))
    return _matmul(
        "grad_x", dproj, w_in_t, NN, (D_MODEL // tn, t // tm, 1),
        pl.BlockSpec((tm, D_INP), lambda j, i, k: (i, 0)),
        pl.BlockSpec((D_INP, tn), lambda j, i, k: (0, j)),
        jax.ShapeDtypeStruct((t, D_MODEL), F32), blk, (tm, tn),
        extra=(dr1,), extra_specs=(blk,), epilogue=epilogue, dep=dep)


def _place():
    x, y, c = lax.axis_index("x"), lax.axis_index("y"), lax.axis_index("c")
    chips = [(1 - x, y), (x, 1 - y), (1 - x, 1 - y)]
    return x, y, c, chips


def _plan_gather_out(src, land, x, y, c, chips):
    me = 4 * x + 2 * y + c
    return [(src, land.at[me], to) for to in [(x, y, 1 - c)] + [(px, py, c) for px, py in chips]]


def _plan_gather_forward(src, land, x, y, c, chips):
    return [(land.at[4 * px + 2 * py + c], land.at[4 * px + 2 * py + c], (x, y, 1 - c)) for px, py in chips]


def _plan_sibling(src, land, x, y, c, chips):
    return [(src.at[2 * q + (1 - c)], land.at[q], (x, y, 1 - c)) for q in range(4)]


def _plan_chips(src, land, x, y, c, chips):
    return [(src.at[j], land.at[j], (px, py, c)) for j, (px, py) in enumerate(chips)]


_PLAN_COPIES = {_plan_gather_out: 4, _plan_gather_forward: 3, _plan_sibling: 4, _plan_chips: 3}
_HBM = pl.BlockSpec(memory_space=pltpu.HBM)
_SEM = pl.BlockSpec(memory_space=pltpu.SEMAPHORE)
_EFFECT = pltpu.SideEffectType.DATAFLOW_SIDE_EFFECTING


def _hbm(a):
    return pltpu.with_memory_space_constraint(a, pltpu.HBM)


def _plan_descriptors(plans, srcs, lands, send, recv):
    x, y, c, chips = _place()
    cps = []
    for plan, src, land in zip(plans, srcs, lands):
        for s_ref, d_ref, to in plan(src, land, x, y, c, chips):
            k = len(cps)
            cps.append(pltpu.make_async_remote_copy(src_ref=s_ref, dst_ref=d_ref, send_sem=send.at[k],
                                                    recv_sem=recv.at[k], device_id=to, device_id_type=MESH))
    return cps


def _copies_start(name, plans, srcs, lands, after=None):
    has_src = [s is not None for s in srcs]
    arrays = [s for s in srcs if s is not None] + list(lands)
    n_src = sum(has_src)
    n_cp = sum(_PLAN_COPIES[p] for p in plans)
    afters = [] if after is None else [after]

    def body(*refs):
        ins = refs[:len(arrays)]
        send, recv = refs[len(arrays) + len(afters)], refs[len(arrays) + len(afters) + 1]
        token = refs[-1]
        it = iter(ins[:n_src])
        src_refs = [next(it) if h else None for h in has_src]
        for cp in _plan_descriptors(plans, src_refs, ins[n_src:], send, recv):
            cp.start()
        token[...] = jnp.zeros_like(token)

    outs = pl.pallas_call(
        body, name=name,
        in_specs=[_HBM] * len(arrays) + [pl.BlockSpec(memory_space=pl.ANY)] * len(afters),
        out_specs=(_SEM, _SEM, *[_HBM] * len(arrays), pl.BlockSpec(memory_space=pltpu.VMEM)),
        out_shape=(pltpu.SemaphoreType.DMA((n_cp,)), pltpu.SemaphoreType.DMA((n_cp,)),
                   *[pltpu.HBM(a.shape, a.dtype) for a in arrays], jax.ShapeDtypeStruct((8, LANES), F32)),
        input_output_aliases={i: 2 + i for i in range(len(arrays))},
        compiler_params=pltpu.CompilerParams(has_side_effects=_EFFECT),
    )(*[_hbm(a) for a in arrays], *afters)
    send, recv = outs[0], outs[1]
    thru = list(outs[2:-1])
    it = iter(thru[:n_src])
    return send, recv, [next(it) if h else None for h in has_src], thru[n_src:], outs[-1]


def _copies_wait(name, plans, started, after):
    send, recv, srcs, lands, _ = started
    has_src = [s is not None for s in srcs]
    arrays = [s for s in srcs if s is not None] + list(lands)
    n_src = sum(has_src)

    def body(*refs):
        ins = refs[:len(arrays)]
        send_ref, recv_ref = refs[len(arrays)], refs[len(arrays) + 1]
        it = iter(ins[:n_src])
        src_refs = [next(it) if h else None for h in has_src]
        for cp in _plan_descriptors(plans, src_refs, ins[n_src:], send_ref, recv_ref):
            cp.wait_send()
            cp.wait_recv()

    outs = pl.pallas_call(
        body, name=name,
        in_specs=[_HBM] * len(arrays) + [_SEM, _SEM, pl.BlockSpec(memory_space=pl.ANY)],
        out_specs=tuple([_HBM] * len(arrays)),
        out_shape=tuple(pltpu.HBM(a.shape, a.dtype) for a in arrays),
        input_output_aliases={i: i for i in range(len(arrays))},
        compiler_params=pltpu.CompilerParams(has_side_effects=_EFFECT),
    )(*arrays, send, recv, after)
    return list(outs[:n_src]), list(outs[n_src:])


def _shard_tiles(rows, cols, tr):
    if rows % tr == 0:
        return (tr, cols), rows // tr, lambda r: (r, 0)
    tc = 2 * LANES
    return (rows, tc), cols // tc, lambda r: (0, r)


def _pair_sum(name, grad, from_sibling, blocks):
    _, rows, cols = grad.shape
    (br_, bc), steps, at = _shard_tiles(rows, cols, 256)

    def body(blk_ref, g_ref, s_ref, o_ref):
        o_ref[...] = (g_ref[...] + s_ref[...]).astype(BF16)

    return pl.pallas_call(
        body, name=name,
        grid_spec=pltpu.PrefetchScalarGridSpec(
            num_scalar_prefetch=1, grid=(3, steps),
            in_specs=[pl.BlockSpec((None, br_, bc), lambda j, r, br: (br[j], *at(r))),
                      pl.BlockSpec((None, br_, bc), lambda j, r, br: (br[3 + j], *at(r)))],
            out_specs=pl.BlockSpec((None, br_, bc), lambda j, r, br: (j, *at(r)))),
        out_shape=jax.ShapeDtypeStruct((3, rows, cols), BF16),
        compiler_params=_cparams(("parallel", "parallel")),
    )(blocks, grad, from_sibling)


def _adam_math(g, w, m, v):
    m2 = ADAM_B1 * m + (1.0 - ADAM_B1) * g
    v2 = ADAM_B2 * v + (1.0 - ADAM_B2) * (g * g)
    m_hat = m2 / (1.0 - ADAM_B1 ** ADAM_STEP)
    v_hat = v2 / (1.0 - ADAM_B2 ** ADAM_STEP)
    delta = -ADAM_LR * (m_hat / (jnp.sqrt(v_hat) + ADAM_EPS) + ADAM_WD * w)
    return delta, m2, v2


def _sum_adam(name, grad, from_sibling, from_chips, own, w, m, v, dep=None):
    rows, cols = w.shape
    (br_, bc), steps, at = _shard_tiles(rows, cols, 128)

    deps, dep_specs = _dep_operand(dep)

    def body(own_ref, p_ref, s_ref, r_ref, w_ref, m_ref, v_ref, *rest):
        g_out, d_out, m_out, v_out = rest[len(deps):]
        g = p_ref[...] + s_ref[...]
        for j in range(3):
            g = g + r_ref[j].astype(F32)
        d, m2, v2 = _adam_math(g, w_ref[...], m_ref[...], v_ref[...])
        g_out[...] = g
        d_out[...] = d
        m_out[...] = m2
        v_out[...] = v2

    blk = pl.BlockSpec((br_, bc), lambda r, cr: at(r))
    shp = jax.ShapeDtypeStruct((rows, cols), F32)
    return pl.pallas_call(
        body, name=name,
        grid_spec=pltpu.PrefetchScalarGridSpec(
            num_scalar_prefetch=1, grid=(steps,),
            in_specs=[pl.BlockSpec((None, br_, bc), lambda r, cr: (cr[0], *at(r))),
                      pl.BlockSpec((None, br_, bc), lambda r, cr: (cr[1], *at(r))),
                      pl.BlockSpec((3, br_, bc), lambda r, cr: (0, *at(r))),
                      blk, blk, blk, *dep_specs],
            out_specs=[blk, blk, blk, blk]),
        out_shape=[shp, shp, shp, shp],
        compiler_params=_cparams(("parallel",)),
    )(own, grad, from_sibling, from_chips, w, m, v, *deps)


def _adam_small(name, g, w, m, v):
    def body(g_ref, w_ref, m_ref, v_ref, d_out, m_out, v_out):
        d, m2, v2 = _adam_math(g_ref[...], w_ref[...], m_ref[...], v_ref[...])
        d_out[...] = d
        m_out[...] = m2
        v_out[...] = v2

    shp = jax.ShapeDtypeStruct(w.shape, F32)
    return pl.pallas_call(body, name=name, out_shape=[shp, shp, shp])(g, w, m, v)


def _all_reduce_small(pack):
    rows = pack.shape[0]

    def body(in_ref, out_ref, slots, send, recv):
        x, y, c, _ = _place()
        me = 4 * x + 2 * y + c
        slots[me] = in_ref[...]
        cps = []
        for k in range(1, N_DEV):
            dx, dy, dc = (k >> 2) & 1, (k >> 1) & 1, k & 1
            to = (jnp.bitwise_xor(x, dx), jnp.bitwise_xor(y, dy), jnp.bitwise_xor(c, dc))
            cps.append(pltpu.make_async_remote_copy(
                src_ref=in_ref, dst_ref=slots.at[me], send_sem=send.at[k - 1], recv_sem=recv.at[k - 1],
                device_id=to, device_id_type=MESH))
        for cp in cps:
            cp.start()
        for cp in cps:
            cp.wait()
        acc = slots[0]
        for d in range(1, N_DEV):
            acc = acc + slots[d]
        out_ref[...] = acc

    return pl.pallas_call(
        body, name="all_reduce_small",
        in_specs=[pl.BlockSpec(memory_space=pltpu.VMEM)],
        out_specs=pl.BlockSpec(memory_space=pltpu.VMEM),
        out_shape=jax.ShapeDtypeStruct((rows, LANES), F32),
        scratch_shapes=[pltpu.VMEM((N_DEV, rows, LANES), F32),
                        pltpu.SemaphoreType.DMA((N_DEV - 1,)), pltpu.SemaphoreType.DMA((N_DEV - 1,))],
    )(pack)


def _rows128(a):
    flat = a.reshape(-1)
    padn = (-flat.shape[0]) % (8 * LANES)
    if padn:
        flat = jnp.concatenate([flat, jnp.zeros((padn,), flat.dtype)])
    return flat.reshape(-1, LANES)


def kernel(x, w_in, w_gk2, b_gk, gla_norm_w, swa_sinks, w_out, ln1_g, ln1_b, w_up, w_down, ln2_g, ln2_b, loss_target, m_w_in, m_w_gk2, m_b_gk, m_gla_norm_w, m_swa_sinks, m_w_out, m_ln1_g, m_ln1_b, m_w_up, m_w_down, m_ln2_g, m_ln2_b, v_w_in, v_w_gk2, v_b_gk, v_gla_norm_w, v_swa_sinks, v_w_out, v_ln1_g, v_ln1_b, v_w_up, v_w_down, v_ln2_g, v_ln2_b):
    xc, yc, cc = lax.axis_index("x"), lax.axis_index("y"), lax.axis_index("c")
    me = 4 * xc + 2 * yc + cc

    x2 = x[0]
    t = x2.shape[0]
    target = loss_target[0]

    def land_of(sh):
        return lax.dynamic_update_index_in_dim(lax.empty((N_DEV,) + sh.shape, BF16), sh, me, 0)

    out_plans = [_plan_gather_out] * 3
    fwd_plans = [_plan_gather_forward] * 3
    in_shards = [w_in[0].T.astype(BF16), w_gk2[0].astype(BF16)]
    ag0 = _copies_start("gather_in_start", out_plans[:2], in_shards, [land_of(sh) for sh in in_shards])
    zero = ag0[4][0, 0]
    xb = (x2 + zero).astype(BF16)
    shards = [(w[0] + zero).astype(BF16) for w in (w_out, w_up, w_down)]
    lands = [land_of(sh) for sh in shards]
    _, in_lands = _copies_wait("gather_in_wait", out_plans[:2], ag0, lands[2])
    ag0f = _copies_start("gather_in_forward_start", fwd_plans[:2], [None] * 2, in_lands)
    _, first = _copies_wait("gather_in_forward_wait", fwd_plans[:2], ag0f, ag0f[4])
    w_in_rows = first[0].reshape(D_IN, D_MODEL)
    w_in_t = _to_padded_rows(w_in_rows)
    w_in_o = jnp.concatenate([w_in_rows, jnp.zeros((D_INP - D_IN, D_MODEL), BF16)], axis=0)
    w2 = first[1].transpose(1, 0, 2).reshape(GATE_RANK, GLA_KW)
    w2p = jnp.concatenate([w2, jnp.zeros((LANES - GATE_RANK, GLA_KW), BF16)], axis=0)
    ag1 = _copies_start("gather_out_start", out_plans, shards, lands, first[0])

    proj = _proj(xb, w_in_t, dep=ag1[4])
    rt = min(256, t)
    ii = jnp.arange(rt)
    tri = ((ii[:, None] // GC == ii[None, :] // GC) & (ii[None, :] <= ii[:, None])).astype(F32)
    bloc, dgdz = _gate_fwd(proj, w2p, b_gk, tri)
    o_raw, states = _gla_fwd(proj, bloc)
    gla_out = _gla_norm_fwd(o_raw, proj, gla_norm_w)
    cat = _swa_fwd(proj, swa_sinks[0], gla_out)
    _, lands = _copies_wait("gather_out_wait", out_plans, ag1, cat)
    ag2 = _copies_start("gather_forward_out_start", fwd_plans[:1], [None], lands[:1])
    _, (w_out_g,) = _copies_wait("gather_forward_out_wait", fwd_plans[:1], ag2, ag2[4])
    ag3 = _copies_start("gather_forward_mlp_start", fwd_plans[:2], [None] * 2, lands[1:], w_out_g)
    w_out_f = w_out_g.reshape(D_MODEL, D_MODEL)
    r1, h1, h1b = _mix_ln1(cat, w_out_f, x2, ln1_g, ln1_b, dep=ag3[4])
    _, (w_up_f, w_down_g) = _copies_wait("gather_forward_mlp_wait", fwd_plans[:2], ag3, h1b)
    w_down_f = w_down_g.reshape(D_FF, D_MODEL)
    a_act, hdn = _mlp_up(h1b, w_up_f)
    dr2, dr2b, g_ln2_g, g_ln2_b, loss_part = _mlp_down_loss(hdn, w_down_f, h1, target, ln2_g, ln2_b)

    others = [2 * (1 - xc) + yc, 2 * xc + (1 - yc), 2 * (1 - xc) + (1 - yc)]
    blocks = jnp.stack([2 * q + cc for q in others] + others).astype(jnp.int32)
    own = jnp.stack([me, 2 * xc + yc]).astype(jnp.int32)
    wmv = dict(w_in=[a[0].T for a in (w_in, m_w_in, v_w_in)], w_out=[a[0] for a in (w_out, m_w_out, v_w_out)],
               w_up=[a[0] for a in (w_up, m_w_up, v_w_up)], w_down=[a[0] for a in (w_down, m_w_down, v_w_down)])
    big = {}

    def sib_land(g):
        return lax.empty((4,) + g.shape[1:], F32)

    def chip_land(g):
        return lax.empty((3,) + g.shape[1:], BF16)

    def finish(nm, g, from_sib, from_chips, dep=None):
        outs = _sum_adam("sum_adam_" + nm, g, from_sib, from_chips, own, *wmv[nm], dep=dep)
        big[nm] = [(o.T if nm == "w_in" else o)[None] for o in outs]

    du = _mlp_down_bwd(dr2b, w_down_f, a_act)
    g_down = _grad_w_down(hdn, dr2b)
    sa_down = _copies_start("sibling_down_start", [_plan_sibling], [g_down], [sib_land(g_down)])
    g_up = _grad_w_up(h1b, du, dep=sa_down[4])
    (g_down,), (fs_down,) = _copies_wait("sibling_down_wait", [_plan_sibling], sa_down, g_up)
    p_down = _pair_sum("pair_sum_w_down", g_down, fs_down, blocks)
    sb_down = _copies_start("chips_down_sibling_up_start", [_plan_chips, _plan_sibling], [p_down, g_up],
                            [chip_land(g_down), sib_land(g_up)])
    dr1, dr1b, g_ln1_g, g_ln1_b = _mlp_up_bwd_ln1(du, w_up_f, dr2, r1, ln1_g, dep=sb_down[4])
    dcat = _dcat(dr1b, w_out_f)
    (_, g_up), (fc_down, fs_up) = _copies_wait("chips_down_sibling_up_wait", [_plan_chips, _plan_sibling], sb_down,
                                               dcat)
    p_up = _pair_sum("pair_sum_w_up", g_up, fs_up, blocks)
    sb_up = _copies_start("chips_up_start", [_plan_chips], [p_up], [chip_land(g_up)])
    do_raw, dg_g, g_norm_w = _gla_norm_bwd(dcat, o_raw, proj, gla_norm_w, dep=sb_up[4])
    dq_g, dk_g, dv_g, dgk = _gla_bwd(proj, bloc, do_raw, states)
    _, (fc_up,) = _copies_wait("chips_up_wait", [_plan_chips], sb_up, dgk)
    dlo, gw2_p, g_b_gk = _gate_bwd(dgk, dgdz, proj, w2p, tri.T)
    dq_s, dk_s, dv_s, g_sinks = _swa_bwd(proj, dcat, swa_sinks[0])
    dproj = jnp.concatenate([dq_g, dk_g, dv_g, dg_g, dlo[:, :GATE_RANK], dq_s, dk_s.astype(BF16), dv_s.astype(BF16),
                             jnp.zeros((t, D_INP - D_IN), BF16)], axis=-1)
    gw_in_t = _grad_w_in(xb, dproj)
    g_in = gw_in_t.reshape(N_DEV, D_IN // N_DEV, D_MODEL)
    sa_in = _copies_start("sibling_in_start", [_plan_sibling], [g_in], [sib_land(g_in)])
    g_out = _grad_w_out(cat, dr1b, dep=sa_in[4])
    (g_in,), (fs_in,) = _copies_wait("sibling_in_wait", [_plan_sibling], sa_in, g_out)
    p_in = _pair_sum("pair_sum_w_in", g_in, fs_in, blocks)
    sb_in = _copies_start("chips_in_sibling_out_start", [_plan_chips, _plan_sibling], [p_in, g_out],
                          [chip_land(g_in), sib_land(g_out)])
    grad_x = _grad_x(dproj, w_in_o, dr1, dep=sb_in[4])
    (_, g_out), (fc_in, fs_out) = _copies_wait("chips_in_sibling_out_wait", [_plan_chips, _plan_sibling], sb_in,
                                               grad_x)
    p_out = _pair_sum("pair_sum_w_out", g_out, fs_out, blocks)
    sb_out = _copies_start("chips_out_start", [_plan_chips], [p_out], [chip_land(g_out)])
    def done(nm):
        return big[nm][0][0, :8, :LANES]

    finish("w_in", g_in, fs_in, fc_in, dep=sb_out[4])
    finish("w_down", g_down, fs_down, fc_down, dep=done("w_in"))
    finish("w_up", g_up, fs_up, fc_up, dep=done("w_down"))

    pieces = [loss_part, g_b_gk, g_norm_w, g_sinks[:, :SWA_HEADS], g_ln1_g, g_ln1_b, g_ln2_g, g_ln2_b,
              gw2_p[:GATE_RANK]]
    pack = jnp.concatenate([_rows128(p) for p in pieces], axis=0)
    tot = _all_reduce_small(pack + sb_out[4][:1, :1])
    sizes = [p.size for p in pieces]
    offs = [0]
    for p in pieces:
        offs.append(offs[-1] + _rows128(p).shape[0])
    unpack = lambda i, shape: tot[offs[i]:offs[i + 1]].reshape(-1)[:sizes[i]].reshape(shape)
    loss = tot[0, 0]
    small_names = ["b_gk", "gla_norm_w", "swa_sinks", "ln1_g", "ln1_b", "ln2_g", "ln2_b"]
    small_w = dict(b_gk=(b_gk, m_b_gk, v_b_gk), gla_norm_w=(gla_norm_w, m_gla_norm_w, v_gla_norm_w),
                   swa_sinks=(swa_sinks, m_swa_sinks, v_swa_sinks), ln1_g=(ln1_g, m_ln1_g, v_ln1_g),
                   ln1_b=(ln1_b, m_ln1_b, v_ln1_b), ln2_g=(ln2_g, m_ln2_g, v_ln2_g),
                   ln2_b=(ln2_b, m_ln2_b, v_ln2_b))
    small_g = {nm: unpack(1 + i, small_w[nm][0].shape) for i, nm in enumerate(small_names)}
    g_pack = jnp.concatenate([_rows128(small_g[nm]) for nm in small_names], axis=0)
    small_wmv = [jnp.concatenate([_rows128(small_w[nm][k]) for nm in small_names], axis=0) for k in range(3)]
    small_out = _adam_small("adam_replicated", g_pack, *small_wmv)
    srow = [0]
    for nm in small_names:
        srow.append(srow[-1] + _rows128(small_w[nm][0]).shape[0])
    small = {}
    for i, nm in enumerate(small_names):
        shape = small_w[nm][0].shape
        n = small_w[nm][0].size
        small[nm] = [small_g[nm]] + [o[srow[i]:srow[i + 1]].reshape(-1)[:n].reshape(shape) for o in small_out]

    gw2_full = unpack(8, (GATE_RANK, GLA_KW))
    gw2_loc = lax.dynamic_slice_in_dim(gw2_full, me * (GLA_KW // N_DEV), GLA_KW // N_DEV, axis=1)
    gk2_out = _adam_small("adam_w_gk2", gw2_loc, w_gk2[0], m_w_gk2[0], v_w_gk2[0])
    big["w_gk2"] = [gw2_loc[None]] + [o[None] for o in gk2_out]

    _, (fc_out,) = _copies_wait("chips_out_wait", [_plan_chips], sb_out, tot[:1, :1] + done("w_up")[:1, :1])
    finish("w_out", g_out, fs_out, fc_out)

    order = ["w_in", "w_gk2", "b_gk", "gla_norm_w", "swa_sinks", "w_out", "ln1_g", "ln1_b", "w_up", "w_down",
             "ln2_g", "ln2_b"]
    res = {**big, **small}
    outs = [loss, grad_x[None]]
    for k in range(4):
        outs += [res[nm][k] for nm in order]
    return tuple(outs)
```

```python
import functools

import jax
import jax.numpy as jnp
from jax import lax
from jax.experimental import pallas as pl
from jax.experimental.pallas import tpu as pltpu

F32 = jnp.float32
BF16 = jnp.bfloat16

N_DEV = 8
D_MODEL = 2048
D_FF = 8192
GLA_HEADS = 4
GLA_DK = 128
GLA_DV = 256
GLA_KW = 512
D_GLA = 1024
GATE_RANK = 16
GATE_TAU = 16.0
SWA_HEADS = 16
SWA_DH = 64
SWA_GROUP = 8
WINDOW = 128
D_SWA = 1024
D_IN = 4368
ALPHA = 2.0 ** 0.25
LN_EPS = 1e-5
RMS_EPS = 1e-5
ADAM_LR = 0.001
ADAM_B1 = 0.9
ADAM_B2 = 0.999
ADAM_EPS = 1e-08
ADAM_WD = 0.01
ADAM_STEP = 10

C_QG, C_KG, C_VG, C_GG, C_QS, C_KS, C_VS, C_LO = 0, 512, 1024, 2048, 3072, 4096, 4224, 4352
D_INP = 4480
LANES = 128
GC = 16
NEG = -1e30

NN = ((1,), (0,))
NT = ((1,), (1,))
TN = ((0,), (0,))

VMEM_LIMIT = 52 * 1024 * 1024
MESH = pl.DeviceIdType.MESH


def _dot(a, b, dn, precision=None):
    return lax.dot_general(a, b, (dn, ((), ())), preferred_element_type=F32, precision=precision)


def _bf16_round(v):
    return v.astype(BF16).astype(F32)


def _cparams(dims):
    return pltpu.CompilerParams(dimension_semantics=dims, vmem_limit_bytes=VMEM_LIMIT)


def _dep_operand(dep):
    if dep is None:
        return (), ()
    return (dep,), (pl.BlockSpec(dep.shape, lambda *_: (0,) * dep.ndim),)


def _to_padded_rows(w):
    pad = jnp.zeros((D_INP - D_IN,) + w.shape[1:], w.dtype)
    return jnp.concatenate([w[:3072], w[3088:], w[3072:3088], pad], axis=0)


def _matmul(name, a, b, dn, grid, a_spec, b_spec, out_shape, out_specs, acc_shape, *,
            extra=(), extra_specs=(), epilogue=None, dims=("parallel", "parallel", "arbitrary"), dep=None):
    nk = grid[2]
    n_extra = len(extra)
    deps, dep_specs = _dep_operand(dep)
    direct = epilogue is None and (nk == 1 or (not isinstance(out_shape, (list, tuple)) and out_shape.dtype == F32))

    scratch = [] if direct or nk == 1 else [pltpu.VMEM(acc_shape, F32)]

    def body(a_ref, b_ref, *rest):
        extra_refs = rest[:n_extra]
        out_refs = rest[n_extra + len(deps):len(rest) - len(scratch)]
        acc_ref = rest[-1] if scratch else out_refs[0]
        part = _dot(a_ref[...].astype(BF16), b_ref[...].astype(BF16), dn)

        def finish():
            if epilogue is None:
                out_refs[0][...] = acc_ref[...].astype(out_refs[0].dtype)
            else:
                epilogue(acc_ref, extra_refs, out_refs)

        if direct and nk == 1:
            out_refs[0][...] = part.reshape(out_refs[0].shape).astype(out_refs[0].dtype)
        elif nk == 1:
            epilogue(part, extra_refs, out_refs)
        else:
            k = pl.program_id(2)

            @pl.when(k == 0)
            def _():
                acc_ref[...] = part

            @pl.when(k > 0)
            def _():
                acc_ref[...] += part

            if not direct:
                @pl.when(k == nk - 1)
                def _():
                    finish()

    return pl.pallas_call(
        body, name=name, grid=grid,
        in_specs=[a_spec, b_spec, *extra_specs, *dep_specs],
        out_specs=out_specs, out_shape=out_shape,
        scratch_shapes=scratch,
        compiler_params=_cparams(dims),
    )(a, b, *extra, *deps)


def _row_chunks(rows, step=128):
    step = min(step, rows)
    return [(r, step) for r in range(0, rows, step)]


def _ln_stats(r):
    mu = jnp.mean(r, axis=-1, keepdims=True)
    xc = r - mu
    var = jnp.mean(xc * xc, axis=-1, keepdims=True)
    rstd = lax.rsqrt(var + LN_EPS)
    return xc * rstd, rstd


def _ln_bwd(dy_g, xhat, rstd):
    m1 = jnp.mean(dy_g, axis=-1, keepdims=True)
    m2 = jnp.mean(dy_g * xhat, axis=-1, keepdims=True)
    return rstd * (dy_g - m1 - xhat * m2)


def _proj(xb, w_in_t, dep=None):
    t = xb.shape[0]
    tm, tn = min(1024, t), 896
    return _matmul(
        "proj", xb, w_in_t, NT, (t // tm, D_INP // tn, 1),
        pl.BlockSpec((tm, D_MODEL), lambda i, j, k: (i, 0)),
        pl.BlockSpec((tn, D_MODEL), lambda i, j, k: (j, 0)),
        jax.ShapeDtypeStruct((t, D_INP), F32),
        pl.BlockSpec((tm, tn), lambda i, j, k: (i, j)),
        (tm, tn), dep=dep)


def _gate_fwd(proj, w2p, b_gk, tri):
    t = proj.shape[0]
    r = tri.shape[0]

    def body(lo_ref, w_ref, b_ref, tri_ref, bloc_ref, dgdz_ref):
        z = _dot(lo_ref[...].astype(BF16), w_ref[...], NN) + b_ref[...]
        e = jnp.exp(-jnp.abs(z))
        gk = (jnp.minimum(z, 0.0) - jnp.log1p(e)) * (1.0 / GATE_TAU)
        inv = 1.0 / (1.0 + e)
        dgdz_ref[...] = jnp.where(z >= 0.0, e * inv, inv) * (1.0 / GATE_TAU)
        bloc_ref[...] = _dot(tri_ref[...], gk, NN, precision=lax.Precision.HIGHEST)

    return pl.pallas_call(
        body, name="gate_fwd", grid=(t // r,),
        in_specs=[pl.BlockSpec((r, LANES), lambda i: (i, C_LO // LANES)),
                  pl.BlockSpec((LANES, GLA_KW), lambda i: (0, 0)),
                  pl.BlockSpec((1, GLA_KW), lambda i: (0, 0)),
                  pl.BlockSpec((r, r), lambda i: (0, 0))],
        out_specs=[pl.BlockSpec((r, GLA_KW), lambda i: (i, 0)),
                   pl.BlockSpec((r, GLA_KW), lambda i: (i, 0))],
        out_shape=[jax.ShapeDtypeStruct((t, GLA_KW), F32), jax.ShapeDtypeStruct((t, GLA_KW), F32)],
        compiler_params=_cparams(("parallel",)),
    )(proj, w2p, b_gk, tri)


def _gla_fwd(proj, bloc):
    t = proj.shape[0]
    r = min(256, t)
    ncb = r // GC
    scale = GLA_DK ** -0.5

    def body(q_ref, k_ref, v_ref, b_ref, o_ref, st_ref, s_scr, m_scr):
        @pl.when(pl.program_id(1) == 0)
        def _():
            s_scr[...] = jnp.zeros_like(s_scr)

        rows = lax.broadcasted_iota(jnp.int32, (GC, 1), 0)
        cols = lax.broadcasted_iota(jnp.int32, (1, GC), 1)

        def increment(c, carry):
            r0 = pl.multiple_of(c * GC, GC)
            k = k_ref[pl.ds(r0, GC), :]
            b = b_ref[pl.ds(r0, GC), :]
            kd = k * jnp.exp(b[GC - 1:GC, :] - b)
            m_scr[c] = _dot(v_ref[pl.ds(r0, GC), :].astype(BF16), kd.astype(BF16), TN)
            return carry

        lax.fori_loop(0, ncb, increment, 0, unroll=8)

        def recur(c, st):
            st_ref[c] = st
            r0 = pl.multiple_of(c * GC, GC)
            bl = b_ref[pl.ds(r0, GC), :][GC - 1:GC, :]
            return st * jnp.exp(bl) + m_scr[c]

        s_scr[...] = lax.fori_loop(0, ncb, recur, s_scr[...])

        def output(c, carry):
            r0 = pl.multiple_of(c * GC, GC)
            q = q_ref[pl.ds(r0, GC), :] * scale
            b = b_ref[pl.ds(r0, GC), :]
            kr = _bf16_round(k_ref[pl.ds(r0, GC), :])
            att = jnp.zeros((GC, GC), F32)
            for j in range(GC):
                w = jnp.exp(jnp.where(rows >= j, b - b[j:j + 1, :], NEG))
                a = jnp.sum(_bf16_round(q * w) * kr[j:j + 1, :], axis=-1, keepdims=True)
                att = jnp.where(cols == j, a, att)
            o_ref[pl.ds(r0, GC), :] = (_dot((q * jnp.exp(b)).astype(BF16), st_ref[c].astype(BF16), NT)
                                       + _dot(att.astype(BF16), v_ref[pl.ds(r0, GC), :].astype(BF16), NN))
            return carry

        lax.fori_loop(0, ncb, output, 0, unroll=8)

    return pl.pallas_call(
        body, name="gla_fwd", grid=(GLA_HEADS, t // r),
        in_specs=[pl.BlockSpec((r, GLA_DK), lambda h, i: (i, C_QG // GLA_DK + h)),
                  pl.BlockSpec((r, GLA_DK), lambda h, i: (i, C_KG // GLA_DK + h)),
                  pl.BlockSpec((r, GLA_DV), lambda h, i: (i, C_VG // GLA_DV + h)),
                  pl.BlockSpec((r, GLA_DK), lambda h, i: (i, h))],
        out_specs=[pl.BlockSpec((r, GLA_DV), lambda h, i: (i, h)),
                   pl.BlockSpec((ncb, None, GLA_DV, GLA_DK), lambda h, i: (i, h, 0, 0))],
        out_shape=[jax.ShapeDtypeStruct((t, D_GLA), F32),
                   jax.ShapeDtypeStruct((t // GC, GLA_HEADS, GLA_DV, GLA_DK), F32)],
        scratch_shapes=[pltpu.VMEM((GLA_DV, GLA_DK), F32), pltpu.VMEM((ncb, GLA_DV, GLA_DK), F32)],
        compiler_params=_cparams(("parallel", "arbitrary")),
    )(proj, proj, proj, bloc)


def _gla_norm_fwd(o_raw, proj, norm_w):
    t = o_raw.shape[0]
    r = min(512, t)

    def body(o_ref, g_ref, w_ref, out_ref):
        w = w_ref[...]
        for h in range(GLA_HEADS):
            sl = slice(h * GLA_DV, (h + 1) * GLA_DV)
            o = o_ref[:, sl]
            g = g_ref[:, sl]
            on = o * lax.rsqrt(jnp.mean(o * o, axis=-1, keepdims=True) + RMS_EPS)
            out_ref[:, sl] = (on * w * (g * jax.nn.sigmoid(g))).astype(BF16)

    return pl.pallas_call(
        body, name="gla_norm_fwd", grid=(t // r,),
        in_specs=[pl.BlockSpec((r, D_GLA), lambda i: (i, 0)),
                  pl.BlockSpec((r, D_GLA), lambda i: (i, C_GG // D_GLA)),
                  pl.BlockSpec((1, GLA_DV), lambda i: (0, 0))],
        out_specs=pl.BlockSpec((r, D_GLA), lambda i: (i, 0)),
        out_shape=jax.ShapeDtypeStruct((t, D_GLA), BF16),
        compiler_params=_cparams(("parallel",)),
    )(o_raw, proj, norm_w)


def _swa_masks(i, heads):
    qi = lax.broadcasted_iota(jnp.int32, (heads * WINDOW, 1), 0) & (WINDOW - 1)
    kj = lax.broadcasted_iota(jnp.int32, (1, 2 * WINDOW), 1)
    valid = (kj > qi) & (kj <= qi + WINDOW) & ((i > 0) | (kj >= WINDOW))
    lo = lax.broadcasted_iota(jnp.int32, (1, LANES), 1) < SWA_DH
    return valid, lo


def _dup_half(x, lo, kv):
    xr = pltpu.roll(x, SWA_DH, axis=1)
    return jnp.where(lo, x, xr) if kv == 0 else jnp.where(lo, xr, x)


def _swa_stack(ref, col0, lo, kv):
    parts = []
    for p in range(4):
        c0 = col0 + LANES * (4 * kv + p)
        xp = ref[:, c0:c0 + LANES]
        parts += [jnp.where(lo, xp, 0.0).astype(BF16), jnp.where(lo, 0.0, xp).astype(BF16)]
    return jnp.concatenate(parts, axis=0)


def _swa_unstack(x_all, lo, p):
    r0 = 2 * p * WINDOW
    return jnp.where(lo, x_all[r0:r0 + WINDOW, :], x_all[r0 + WINDOW:r0 + 2 * WINDOW, :])


def _swa_sinks(sink_ref, kv):
    return jnp.concatenate([jnp.full((WINDOW, 1), sink_ref[SWA_GROUP * kv + h], F32) for h in range(SWA_GROUP)],
                           axis=0)


def _swa_probs(qm, kdup, valid, sink):
    s = _dot(qm, kdup, NT) * (SWA_DH ** -0.5)
    s = jnp.where(valid, s, NEG)
    m = jnp.maximum(jnp.max(s, axis=-1, keepdims=True), sink)
    p = jnp.exp(s - m)
    es = jnp.exp(sink - m)
    inv = 1.0 / (jnp.sum(p, axis=-1, keepdims=True) + es)
    return p * inv, es * inv


def _swa_fwd(proj, sinks, gla_out, dep=None):
    t = proj.shape[0]

    deps, dep_specs = _dep_operand(dep)

    def body(sink_ref, q_ref, kp_ref, kc_ref, vp_ref, vc_ref, gla_ref, *rest):
        o_ref = rest[-1]
        i = pl.program_id(0)
        o_ref[:, :D_GLA] = gla_ref[...]
        valid, lo = _swa_masks(i, 1)
        kb = jnp.concatenate([kp_ref[...], kc_ref[...]], axis=0)
        vb = jnp.concatenate([vp_ref[...], vc_ref[...]], axis=0)
        for kv in range(2):
            kdup = _dup_half(kb, lo, kv).astype(BF16)
            vdup = _dup_half(vb, lo, kv).astype(BF16)
            for p in range(4):
                c0 = LANES * (4 * kv + p)
                qp = q_ref[:, c0:c0 + LANES]
                halves = []
                for e in range(2):
                    qm = jnp.where(lo if e == 0 else ~lo, qp, 0.0).astype(BF16)
                    pn, _ = _swa_probs(qm, kdup, valid, sink_ref[SWA_GROUP * kv + 2 * p + e])
                    halves.append(_dot(pn.astype(BF16), vdup, NN))
                o_ref[:, D_GLA + c0:D_GLA + c0 + LANES] = jnp.where(lo, halves[0], halves[1]).astype(BF16)

    kvspec = lambda col, prev: pl.BlockSpec(
        (WINDOW, LANES), (lambda i: (jnp.maximum(i - 1, 0), col)) if prev else (lambda i: (i, col)))
    return pl.pallas_call(
        body, name="swa_fwd", grid=(t // WINDOW,),
        in_specs=[pl.BlockSpec(memory_space=pltpu.SMEM),
                  pl.BlockSpec((WINDOW, D_SWA), lambda i: (i, C_QS // D_SWA)),
                  kvspec(C_KS // LANES, True), kvspec(C_KS // LANES, False),
                  kvspec(C_VS // LANES, True), kvspec(C_VS // LANES, False),
                  pl.BlockSpec((WINDOW, D_GLA), lambda i: (i, 0)), *dep_specs],
        out_specs=pl.BlockSpec((WINDOW, D_MODEL), lambda i: (i, 0)),
        out_shape=jax.ShapeDtypeStruct((t, D_MODEL), BF16),
        compiler_params=_cparams(("parallel",)),
    )(sinks, proj, proj, proj, proj, proj, gla_out, *deps)


def _mix_ln1(cat, w_out, x, ln_g, ln_b, dep=None):
    t = cat.shape[0]
    tm, tk = min(256, t), D_MODEL

    def epilogue(acc_ref, extra, outs):
        x_ref, g_ref, b_ref = extra
        r1_ref, h1_ref, h1b_ref = outs
        for r0, n in _row_chunks(tm):
            rs = slice(r0, r0 + n)
            r1 = ALPHA * x_ref[rs, :] + acc_ref[rs, :]
            xhat, _ = _ln_stats(r1)
            h = xhat * g_ref[...] + b_ref[...]
            r1_ref[rs, :] = r1
            h1_ref[rs, :] = h
            h1b_ref[rs, :] = h.astype(BF16)

    row = pl.BlockSpec((tm, D_MODEL), lambda i, j, k: (i, 0))
    vec = pl.BlockSpec((1, D_MODEL), lambda i, j, k: (0, 0))
    return _matmul(
        "mix_ln1", cat, w_out, NN, (t // tm, 1, D_MODEL // tk),
        pl.BlockSpec((tm, tk), lambda i, j, k: (i, k)),
        pl.BlockSpec((tk, D_MODEL), lambda i, j, k: (k, 0)),
        [jax.ShapeDtypeStruct((t, D_MODEL), F32), jax.ShapeDtypeStruct((t, D_MODEL), F32),
         jax.ShapeDtypeStruct((t, D_MODEL), BF16)],
        [row, row, row], (tm, D_MODEL),
        extra=(x, ln_g, ln_b), extra_specs=(row, vec, vec), epilogue=epilogue, dep=dep)


def _mlp_up(h1b, w_up):
    t = h1b.shape[0]
    tm, tn = min(1024, t), 1024

    def epilogue(acc_ref, extra, outs):
        a_ref, hdn_ref = outs
        for r0, n in _row_chunks(tm, 256):
            rs = slice(r0, r0 + n)
            a = jnp.maximum(acc_ref[rs, :], 0.0)
            a_ref[rs, :] = a.astype(BF16)
            hdn_ref[rs, :] = (a * a).astype(BF16)

    out = pl.BlockSpec((tm, tn), lambda i, j, k: (i, j))
    return _matmul(
        "mlp_up", h1b, w_up, NN, (t // tm, D_FF // tn, 1),
        pl.BlockSpec((tm, D_MODEL), lambda i, j, k: (i, 0)),
        pl.BlockSpec((None, D_MODEL, tn), lambda i, j, k: (j, 0, 0)),
        [jax.ShapeDtypeStruct((t, D_FF), BF16), jax.ShapeDtypeStruct((t, D_FF), BF16)],
        [out, out], (tm, tn), epilogue=epilogue)


def _mlp_down_loss(hdn, w_down, h1, target, ln_g, ln_b):
    t = hdn.shape[0]
    tm, tn, tk = min(1024, t), 1024, 4096
    ff = _matmul(
        "mlp_down", hdn, w_down, NN, (t // tm, D_MODEL // tn, D_FF // tk),
        pl.BlockSpec((tm, tk), lambda i, j, k: (i, k)),
        pl.BlockSpec((tk, tn), lambda i, j, k: (k, j)),
        jax.ShapeDtypeStruct((t, D_MODEL), F32),
        pl.BlockSpec((tm, tn), lambda i, j, k: (i, j)), (tm, tn))
    r = min(256, t)

    def body(ff_ref, h1_ref, t_ref, g_ref, b_ref, dr2_ref, dr2b_ref, gg_ref, gb_ref, loss_ref):
        @pl.when(pl.program_id(0) == 0)
        def _():
            gg_ref[...] = jnp.zeros_like(gg_ref)
            gb_ref[...] = jnp.zeros_like(gb_ref)
            loss_ref[...] = jnp.zeros_like(loss_ref)

        for r0, n in _row_chunks(r, 64):
            rs = slice(r0, r0 + n)
            xhat, rstd = _ln_stats(ALPHA * h1_ref[rs, :] + ff_ref[rs, :])
            err = xhat * g_ref[...] + b_ref[...] - t_ref[rs, :]
            loss_ref[...] += 0.5 * jnp.sum(jnp.mean(err * err, axis=-1, keepdims=True))
            dy = err * (1.0 / D_MODEL)
            gg_ref[...] += jnp.sum(dy * xhat, axis=0, keepdims=True)
            gb_ref[...] += jnp.sum(dy, axis=0, keepdims=True)
            dr2 = _ln_bwd(dy * g_ref[...], xhat, rstd)
            dr2_ref[rs, :] = dr2
            dr2b_ref[rs, :] = dr2.astype(BF16)

    row = pl.BlockSpec((r, D_MODEL), lambda i: (i, 0))
    vec = pl.BlockSpec((1, D_MODEL), lambda i: (0, 0))
    return pl.pallas_call(
        body, name="ln2_loss", grid=(t // r,),
        in_specs=[row, row, row, vec, vec],
        out_specs=[row, row, vec, vec, pl.BlockSpec((1, LANES), lambda i: (0, 0))],
        out_shape=[jax.ShapeDtypeStruct((t, D_MODEL), F32), jax.ShapeDtypeStruct((t, D_MODEL), BF16),
                   jax.ShapeDtypeStruct((1, D_MODEL), F32), jax.ShapeDtypeStruct((1, D_MODEL), F32),
                   jax.ShapeDtypeStruct((1, LANES), F32)],
        compiler_params=_cparams(("arbitrary",)),
    )(ff, h1, target, ln_g, ln_b)


def _mlp_down_bwd(dr2b, w_down, a_act):
    t = dr2b.shape[0]
    tm, tn = min(1024, t), 1024

    def epilogue(acc_ref, extra, outs):
        (a_ref,) = extra
        for r0, n in _row_chunks(tm, 256):
            rs = slice(r0, r0 + n)
            outs[0][rs, :] = (acc_ref[rs, :] * (2.0 * a_ref[rs, :].astype(F32))).astype(BF16)

    blk = pl.BlockSpec((tm, tn), lambda i, j, k: (i, j))
    return _matmul(
        "mlp_down_bwd", dr2b, w_down, NT, (t // tm, D_FF // tn, 1),
        pl.BlockSpec((tm, D_MODEL), lambda i, j, k: (i, 0)),
        pl.BlockSpec((tn, D_MODEL), lambda i, j, k: (j, 0)),
        jax.ShapeDtypeStruct((t, D_FF), BF16), blk, (tm, tn),
        extra=(a_act,), extra_specs=(blk,), epilogue=epilogue)


def _grad_w_down(hdn, dr2b):
    t = hdn.shape[0]
    tm, tn = 1024, 1024
    return _matmul(
        "grad_w_down", hdn, dr2b, TN, (D_MODEL // tn, D_FF // tm, 1),
        pl.BlockSpec((t, tm), lambda j, i, k: (0, i)),
        pl.BlockSpec((t, tn), lambda j, i, k: (0, j)),
        jax.ShapeDtypeStruct((N_DEV, D_FF // N_DEV, D_MODEL), F32),
        pl.BlockSpec((None, tm, tn), lambda j, i, k: (i, 0, j)), (tm, tn))


def _grad_w_up(h1b, du, dep=None):
    t = h1b.shape[0]
    tm, tn = 1024, 1024
    return _matmul(
        "grad_w_up", h1b, du, TN, (N_DEV, D_MODEL // tm, 1),
        pl.BlockSpec((t, tm), lambda i, j, k: (0, j)),
        pl.BlockSpec((t, tn), lambda i, j, k: (0, i)),
        jax.ShapeDtypeStruct((N_DEV, D_MODEL, D_FF // N_DEV), F32),
        pl.BlockSpec((None, tm, tn), lambda i, j, k: (i, j, 0)), (tm, tn), dep=dep)


def _mlp_up_bwd_ln1(du, w_up, dr2, r1, ln_g, dep=None):
    t = du.shape[0]
    tm, tn, tk, nb = min(1024, t), 1024, D_FF // N_DEV, 4
    deps, dep_specs = _dep_operand(dep)

    def mm_body(a_ref, b_ref, *rest):
        o_ref = rest[-1]
        k = pl.program_id(2)
        part = _dot(a_ref[:, :tk], b_ref[0], NT)
        for d in range(1, nb):
            part = part + _dot(a_ref[:, d * tk:(d + 1) * tk], b_ref[d], NT)

        @pl.when(k == 0)
        def _():
            o_ref[...] = part

        @pl.when(k > 0)
        def _():
            o_ref[...] += part

    dff = pl.pallas_call(
        mm_body, name="mlp_up_bwd", grid=(t // tm, D_MODEL // tn, N_DEV // nb),
        in_specs=[pl.BlockSpec((tm, nb * tk), lambda i, j, k: (i, k)),
                  pl.BlockSpec((nb, tn, tk), lambda i, j, k: (k, j, 0)), *dep_specs],
        out_specs=pl.BlockSpec((tm, tn), lambda i, j, k: (i, j)),
        out_shape=jax.ShapeDtypeStruct((t, D_MODEL), F32),
        compiler_params=_cparams(("parallel", "parallel", "arbitrary")),
    )(du, w_up, *deps)
    r = min(256, t)

    def body(acc_ref, dr2_ref, r1_ref, g_ref, dr1_ref, dr1b_ref, gg_ref, gb_ref):
        @pl.when(pl.program_id(0) == 0)
        def _():
            gg_ref[...] = jnp.zeros_like(gg_ref)
            gb_ref[...] = jnp.zeros_like(gb_ref)

        for r0, n in _row_chunks(r, 64):
            rs = slice(r0, r0 + n)
            dh1 = ALPHA * dr2_ref[rs, :] + acc_ref[rs, :]
            xhat, rstd = _ln_stats(r1_ref[rs, :])
            gg_ref[...] += jnp.sum(dh1 * xhat, axis=0, keepdims=True)
            gb_ref[...] += jnp.sum(dh1, axis=0, keepdims=True)
            dr1 = _ln_bwd(dh1 * g_ref[...], xhat, rstd)
            dr1_ref[rs, :] = dr1
            dr1b_ref[rs, :] = dr1.astype(BF16)

    row = pl.BlockSpec((r, D_MODEL), lambda i: (i, 0))
    vec = pl.BlockSpec((1, D_MODEL), lambda i: (0, 0))
    return pl.pallas_call(
        body, name="ln1_bwd", grid=(t // r,),
        in_specs=[row, row, row, vec],
        out_specs=[row, row, vec, vec],
        out_shape=[jax.ShapeDtypeStruct((t, D_MODEL), F32), jax.ShapeDtypeStruct((t, D_MODEL), BF16),
                   jax.ShapeDtypeStruct((1, D_MODEL), F32), jax.ShapeDtypeStruct((1, D_MODEL), F32)],
        compiler_params=_cparams(("arbitrary",)),
    )(dff, dr2, r1, ln_g)


def _dcat(dr1b, w_out):
    t = dr1b.shape[0]
    tm, tn = min(1024, t), 1024
    return _matmul(
        "dcat", dr1b, w_out, NT, (t // tm, D_MODEL // tn, 1),
        pl.BlockSpec((tm, D_MODEL), lambda i, j, k: (i, 0)),
        pl.BlockSpec((tn, D_MODEL), lambda i, j, k: (j, 0)),
        jax.ShapeDtypeStruct((t, D_MODEL), F32),
        pl.BlockSpec((tm, tn), lambda i, j, k: (i, j)), (tm, tn))


def _grad_w_out(cat, dr1b, dep=None):
    t = cat.shape[0]
    tm, tn = 1024, 1024
    return _matmul(
        "grad_w_out", cat, dr1b, TN, (D_MODEL // tm, D_MODEL // tn, 1),
        pl.BlockSpec((t, tm), lambda i, j, k: (0, i)),
        pl.BlockSpec((t, tn), lambda i, j, k: (0, j)),
        jax.ShapeDtypeStruct((N_DEV, D_MODEL // N_DEV, D_MODEL), F32),
        pl.BlockSpec((tm // (D_MODEL // N_DEV), D_MODEL // N_DEV, tn), lambda i, j, k: (i, 0, j)), (tm, tn), dep=dep)


def _gla_norm_bwd(dcat, o_raw, proj, norm_w, dep=None):
    t = o_raw.shape[0]
    r = min(512, t)

    deps, dep_specs = _dep_operand(dep)

    def body(d_ref, o_ref, g_ref, w_ref, *rest):
        do_ref, dg_ref, dw_ref = rest[len(deps):]

        @pl.when(pl.program_id(0) == 0)
        def _():
            dw_ref[...] = jnp.zeros_like(dw_ref)

        w = w_ref[...]
        dw = jnp.zeros((1, GLA_DV), F32)
        for h in range(GLA_HEADS):
            sl = slice(h * GLA_DV, (h + 1) * GLA_DV)
            o = o_ref[:, sl]
            g = g_ref[:, sl]
            d = d_ref[:, sl]
            rr = lax.rsqrt(jnp.mean(o * o, axis=-1, keepdims=True) + RMS_EPS)
            on = o * rr
            sg = jax.nn.sigmoid(g)
            sil = g * sg
            dg_ref[:, sl] = (d * on * w * (sg * (1.0 + g * (1.0 - sg)))).astype(BF16)
            dw = dw + jnp.sum(d * on * sil, axis=0, keepdims=True)
            don = d * w * sil
            do_ref[:, sl] = rr * (don - on * jnp.mean(don * on, axis=-1, keepdims=True))
        dw_ref[...] += dw

    return pl.pallas_call(
        body, name="gla_norm_bwd", grid=(t // r,),
        in_specs=[pl.BlockSpec((r, D_GLA), lambda i: (i, 0)),
                  pl.BlockSpec((r, D_GLA), lambda i: (i, 0)),
                  pl.BlockSpec((r, D_GLA), lambda i: (i, C_GG // D_GLA)),
                  pl.BlockSpec((1, GLA_DV), lambda i: (0, 0)), *dep_specs],
        out_specs=[pl.BlockSpec((r, D_GLA), lambda i: (i, 0)),
                   pl.BlockSpec((r, D_GLA), lambda i: (i, 0)),
                   pl.BlockSpec((1, GLA_DV), lambda i: (0, 0))],
        out_shape=[jax.ShapeDtypeStruct((t, D_GLA), F32), jax.ShapeDtypeStruct((t, D_GLA), BF16),
                   jax.ShapeDtypeStruct((1, GLA_DV), F32)],
        compiler_params=_cparams(("arbitrary",)),
    )(dcat, o_raw, proj, norm_w, *deps)


def _gla_bwd(proj, bloc, do_raw, states):
    t = proj.shape[0]
    r = min(256, t)
    ncb = r // GC
    nb = t // r
    scale = GLA_DK ** -0.5

    def body(q_ref, k_ref, v_ref, b_ref, do_ref, st_ref, dq_ref, dk_ref, dv_ref, db_ref, ds_scr):
        @pl.when(pl.program_id(1) == 0)
        def _():
            ds_scr[...] = jnp.zeros_like(ds_scr)

        rows = lax.broadcasted_iota(jnp.int32, (GC, 1), 0)

        def chunk(cc, carry):
            c = ncb - 1 - cc
            r0 = pl.multiple_of(c * GC, GC)
            q = q_ref[pl.ds(r0, GC), :] * scale
            k = k_ref[pl.ds(r0, GC), :]
            v = v_ref[pl.ds(r0, GC), :]
            b = b_ref[pl.ds(r0, GC), :]
            do = do_ref[pl.ds(r0, GC), :]
            st = st_ref[c]
            dsn = ds_scr[...]
            bl = b[GC - 1:GC, :]
            eb = jnp.exp(b)
            ekl = jnp.exp(bl - b)
            ebl = jnp.exp(bl)
            qh = q * eb
            kd = k * ekl
            dob = do.astype(BF16)
            dsb = dsn.astype(BF16)
            dqh = _dot(dob, st.astype(BF16), NN)
            dkd = _dot(v.astype(BF16), dsb, NN)
            dv = _dot(kd.astype(BF16), dsb, NT)
            dq_i = jnp.zeros((GC, GLA_DK), F32)
            dk_i = jnp.zeros((GC, GLA_DK), F32)
            dk_x = jnp.zeros((GC, GLA_DK), F32)
            kr = _bf16_round(k)
            vr = _bf16_round(v)
            dor = _bf16_round(do)
            for i in range(GC):
                w = jnp.exp(jnp.where(rows <= i, b[i:i + 1, :] - b, NEG))
                qw = q[i:i + 1, :] * w
                qwr = _bf16_round(qw)
                a = _bf16_round(jnp.sum(qwr * kr, axis=-1, keepdims=True))
                da = jnp.sum(vr * dor[i:i + 1, :], axis=-1, keepdims=True)
                dv = dv + a * dor[i:i + 1, :]
                dk_x = dk_x + da * qw
                dk_i = dk_i + _bf16_round(da) * qwr
                dq_i = jnp.where(rows == i, jnp.sum(da * (w * k), axis=0, keepdims=True), dq_i)
            dqs = dqh * eb + dq_i
            dk = dkd * ekl + dk_i
            db_last = jnp.sum(dkd * kd, axis=0, keepdims=True) + ebl * jnp.sum(dsn * st, axis=0, keepdims=True)
            db = q * dqs - k * (dkd * ekl + dk_x) + jnp.where(rows == GC - 1, db_last, 0.0)
            dq_ref[pl.ds(r0, GC), :] = (dqs * scale).astype(BF16)
            dk_ref[pl.ds(r0, GC), :] = dk.astype(BF16)
            dv_ref[pl.ds(r0, GC), :] = dv.astype(BF16)
            db_ref[pl.ds(r0, GC), :] = db
            ds_scr[...] = dsn * ebl + _dot(dob, qh.astype(BF16), TN)
            return carry

        lax.fori_loop(0, ncb, chunk, 0)

    rev = lambda i: nb - 1 - i
    return pl.pallas_call(
        body, name="gla_bwd", grid=(GLA_HEADS, nb),
        in_specs=[pl.BlockSpec((r, GLA_DK), lambda h, i: (rev(i), C_QG // GLA_DK + h)),
                  pl.BlockSpec((r, GLA_DK), lambda h, i: (rev(i), C_KG // GLA_DK + h)),
                  pl.BlockSpec((r, GLA_DV), lambda h, i: (rev(i), C_VG // GLA_DV + h)),
                  pl.BlockSpec((r, GLA_DK), lambda h, i: (rev(i), h)),
                  pl.BlockSpec((r, GLA_DV), lambda h, i: (rev(i), h)),
                  pl.BlockSpec((ncb, None, GLA_DV, GLA_DK), lambda h, i: (rev(i), h, 0, 0))],
        out_specs=[pl.BlockSpec((r, GLA_DK), lambda h, i: (rev(i), h)),
                   pl.BlockSpec((r, GLA_DK), lambda h, i: (rev(i), h)),
                   pl.BlockSpec((r, GLA_DV), lambda h, i: (rev(i), h)),
                   pl.BlockSpec((r, GLA_DK), lambda h, i: (rev(i), h))],
        out_shape=[jax.ShapeDtypeStruct((t, GLA_KW), BF16), jax.ShapeDtypeStruct((t, GLA_KW), BF16),
                   jax.ShapeDtypeStruct((t, D_GLA), BF16), jax.ShapeDtypeStruct((t, GLA_KW), F32)],
        scratch_shapes=[pltpu.VMEM((GLA_DV, GLA_DK), F32)],
        compiler_params=_cparams(("parallel", "arbitrary")),
    )(proj, proj, proj, bloc, do_raw, states)


def _gate_bwd(db, dgdz, proj, w2p, triu):
    t = db.shape[0]
    r = triu.shape[0]

    def body(db_ref, s_ref, lo_ref, w_ref, u_ref, dlo_ref, gw_ref, gb_ref):
        @pl.when(pl.program_id(0) == 0)
        def _():
            gw_ref[...] = jnp.zeros_like(gw_ref)
            gb_ref[...] = jnp.zeros_like(gb_ref)

        dz = _dot(u_ref[...], db_ref[...], NN, precision=lax.Precision.HIGHEST) * s_ref[...]
        dzb = dz.astype(BF16)
        gb_ref[...] += jnp.sum(dz, axis=0, keepdims=True)
        gw_ref[...] += _dot(lo_ref[...].astype(BF16), dzb, TN)
        dlo_ref[...] = _dot(dzb, w_ref[...], NT).astype(BF16)

    return pl.pallas_call(
        body, name="gate_bwd", grid=(t // r,),
        in_specs=[pl.BlockSpec((r, GLA_KW), lambda i: (i, 0)),
                  pl.BlockSpec((r, GLA_KW), lambda i: (i, 0)),
                  pl.BlockSpec((r, LANES), lambda i: (i, C_LO // LANES)),
                  pl.BlockSpec((LANES, GLA_KW), lambda i: (0, 0)),
                  pl.BlockSpec((r, r), lambda i: (0, 0))],
        out_specs=[pl.BlockSpec((r, LANES), lambda i: (i, 0)),
                   pl.BlockSpec((LANES, GLA_KW), lambda i: (0, 0)),
                   pl.BlockSpec((1, GLA_KW), lambda i: (0, 0))],
        out_shape=[jax.ShapeDtypeStruct((t, LANES), BF16), jax.ShapeDtypeStruct((LANES, GLA_KW), F32),
                   jax.ShapeDtypeStruct((1, GLA_KW), F32)],
        compiler_params=_cparams(("arbitrary",)),
    )(db, dgdz, proj, w2p, triu)


def _swa_bwd(proj, dcat, sinks, dep=None):
    t = proj.shape[0]

    deps, dep_specs = _dep_operand(dep)

    def body(sink_ref, q_ref, kp_ref, kc_ref, vp_ref, vc_ref, d_ref, *rest):
        dq_ref, dk_ref, dv_ref, dsink_ref = rest[len(deps):]
        i = pl.program_id(0)

        @pl.when(i == 0)
        def _():
            dk_ref[...] = jnp.zeros_like(dk_ref)
            dv_ref[...] = jnp.zeros_like(dv_ref)
            dsink_ref[...] = jnp.zeros_like(dsink_ref)

        valid, lo = _swa_masks(i, SWA_GROUP)
        lane = lax.broadcasted_iota(jnp.int32, (1, LANES), 1)
        kb = jnp.concatenate([kp_ref[...], kc_ref[...]], axis=0)
        vb = jnp.concatenate([vp_ref[...], vc_ref[...]], axis=0)
        dsink = jnp.zeros((1, LANES), F32)
        folded_k, folded_v = [], []
        for kv in range(2):
            kdup = _dup_half(kb, lo, kv).astype(BF16)
            vdup = _dup_half(vb, lo, kv).astype(BF16)
            qm = _swa_stack(q_ref, 0, lo, kv)
            dom = _swa_stack(d_ref, 0, lo, kv)
            pn, psink = _swa_probs(qm, kdup, valid, _swa_sinks(sink_ref, kv))
            dpr = _dot(dom, vdup, NT)
            drow = jnp.sum(dpr * pn, axis=-1, keepdims=True)
            ds_col = psink * drow
            for h in range(SWA_GROUP):
                dsink = dsink + jnp.where(lane == SWA_GROUP * kv + h,
                                          -jnp.sum(ds_col[h * WINDOW:(h + 1) * WINDOW, :]), 0.0)
            dsb = (pn * (dpr - drow) * (SWA_DH ** -0.5)).astype(BF16)
            dq_all = _dot(dsb, kdup, NN)
            for p in range(4):
                c0 = LANES * (4 * kv + p)
                dq_ref[:, c0:c0 + LANES] = _swa_unstack(dq_all, lo, p).astype(BF16)
            dkd = _dot(dsb, qm, TN)
            dvd = _dot(pn.astype(BF16), dom, TN)
            folded_k.append(dkd + pltpu.roll(dkd, SWA_DH, axis=1))
            folded_v.append(dvd + pltpu.roll(dvd, SWA_DH, axis=1))
        dkb = jnp.where(lo, folded_k[0], folded_k[1])
        dvb = jnp.where(lo, folded_v[0], folded_v[1])
        dsink_ref[...] += dsink
        cur = pl.ds(pl.multiple_of(i * WINDOW, WINDOW), WINDOW)
        dk_ref[cur, :] += dkb[WINDOW:, :]
        dv_ref[cur, :] += dvb[WINDOW:, :]

        @pl.when(i > 0)
        def _():
            prev = pl.ds(pl.multiple_of((i - 1) * WINDOW, WINDOW), WINDOW)
            dk_ref[prev, :] += dkb[:WINDOW, :]
            dv_ref[prev, :] += dvb[:WINDOW, :]

    kvspec = lambda col, prev: pl.BlockSpec(
        (WINDOW, LANES), (lambda i: (jnp.maximum(i - 1, 0), col)) if prev else (lambda i: (i, col)))
    full = pl.BlockSpec((t, LANES), lambda i: (0, 0))
    return pl.pallas_call(
        body, name="swa_bwd", grid=(t // WINDOW,),
        in_specs=[pl.BlockSpec(memory_space=pltpu.SMEM),
                  pl.BlockSpec((WINDOW, D_SWA), lambda i: (i, C_QS // D_SWA)),
                  kvspec(C_KS // LANES, True), kvspec(C_KS // LANES, False),
                  kvspec(C_VS // LANES, True), kvspec(C_VS // LANES, False),
                  pl.BlockSpec((WINDOW, D_SWA), lambda i: (i, 1)), *dep_specs],
        out_specs=[pl.BlockSpec((WINDOW, D_SWA), lambda i: (i, 0)), full, full,
                   pl.BlockSpec((1, LANES), lambda i: (0, 0))],
        out_shape=[jax.ShapeDtypeStruct((t, D_SWA), BF16), jax.ShapeDtypeStruct((t, LANES), F32),
                   jax.ShapeDtypeStruct((t, LANES), F32), jax.ShapeDtypeStruct((1, LANES), F32)],
        compiler_params=_cparams(("arbitrary",)),
    )(sinks, proj, proj, proj, proj, proj, dcat, *deps)


def _grad_w_in(xb, dproj):
    t = xb.shape[0]
    tm, tn = 896, 1024
    return _matmul(
        "grad_w_in", dproj, xb, TN, (D_INP // tm, D_MODEL // tn, 1),
        pl.BlockSpec((t, tm), lambda i, j, k: (0, i)),
        pl.BlockSpec((t, tn), lambda i, j, k: (0, j)),
        jax.ShapeDtypeStruct((D_IN, D_MODEL), F32),
        pl.BlockSpec((tm, tn), lambda i, j, k: (i, j)), (tm, tn))


def _grad_x(dproj, w_in_t, dr1, dep=None):
    t = dproj.shape[0]
    tm, tn = min(512, t), 1024

    def epilogue(acc_ref, extra, outs):
        for r0, n in _row_chunks(tm):
            rs = slice(r0, r0 + n)
            outs[0][rs, :] = ALPHA * extra[0][rs, :] + acc_ref[rs, :]

    blk = pl.BlockSpec((tm, tn), lambda j, i, k: (i, j))
    return _matmul(
        "grad_x", dproj, w_in_t, NN, (D_MODEL // tn, t // tm, 1),
        pl.BlockSpec((tm, D_INP), lambda j, i, k: (i, 0)),
        pl.BlockSpec((D_INP, tn), lambda j, i, k: (0, j)),
        jax.ShapeDtypeStruct((t, D_MODEL), F32), blk, (tm, tn),
        extra=(dr1,), extra_specs=(blk,), epilogue=epilogue, dep=dep)


def _place():
    x, y, c = lax.axis_index("x"), lax.axis_index("y"), lax.axis_index("c")
    chips = [(1 - x, y), (x, 1 - y), (1 - x, 1 - y)]
    return x, y, c, chips


def _plan_gather_out(src, land, x, y, c, chips):
    me = 4 * x + 2 * y + c
    return [(src, land.at[me], to) for to in [(x, y, 1 - c)] + [(px, py, c) for px, py in chips]]


def _plan_gather_forward(src, land, x, y, c, chips):
    return [(land.at[4 * px + 2 * py + c], land.at[4 * px + 2 * py + c], (x, y, 1 - c)) for px, py in chips]


def _plan_sibling(src, land, x, y, c, chips):
    return [(src.at[2 * q + (1 - c)], land.at[q], (x, y, 1 - c)) for q in range(4)]


def _plan_chips(src, land, x, y, c, chips):
    return [(src.at[j], land.at[j], (px, py, c)) for j, (px, py) in enumerate(chips)]


_PLAN_COPIES = {_plan_gather_out: 4, _plan_gather_forward: 3, _plan_sibling: 4, _plan_chips: 3}
_HBM = pl.BlockSpec(memory_space=pltpu.HBM)
_SEM = pl.BlockSpec(memory_space=pltpu.SEMAPHORE)
_EFFECT = pltpu.SideEffectType.DATAFLOW_SIDE_EFFECTING


def _hbm(a):
    return pltpu.with_memory_space_constraint(a, pltpu.HBM)


def _plan_descriptors(plans, srcs, lands, send, recv):
    x, y, c, chips = _place()
    cps = []
    for plan, src, land in zip(plans, srcs, lands):
        for s_ref, d_ref, to in plan(src, land, x, y, c, chips):
            k = len(cps)
            cps.append(pltpu.make_async_remote_copy(src_ref=s_ref, dst_ref=d_ref, send_sem=send.at[k],
                                                    recv_sem=recv.at[k], device_id=to, device_id_type=MESH))
    return cps


def _copies_start(name, plans, srcs, lands, after=None):
    has_src = [s is not None for s in srcs]
    arrays = [s for s in srcs if s is not None] + list(lands)
    n_src = sum(has_src)
    n_cp = sum(_PLAN_COPIES[p] for p in plans)
    afters = [] if after is None else [after]

    def body(*refs):
        ins = refs[:len(arrays)]
        send, recv = refs[len(arrays) + len(afters)], refs[len(arrays) + len(afters) + 1]
        token = refs[-1]
        it = iter(ins[:n_src])
        src_refs = [next(it) if h else None for h in has_src]
        for cp in _plan_descriptors(plans, src_refs, ins[n_src:], send, recv):
            cp.start()
        token[...] = jnp.zeros_like(token)

    outs = pl.pallas_call(
        body, name=name,
        in_specs=[_HBM] * len(arrays) + [pl.BlockSpec(memory_space=pl.ANY)] * len(afters),
        out_specs=(_SEM, _SEM, *[_HBM] * len(arrays), pl.BlockSpec(memory_space=pltpu.VMEM)),
        out_shape=(pltpu.SemaphoreType.DMA((n_cp,)), pltpu.SemaphoreType.DMA((n_cp,)),
                   *[pltpu.HBM(a.shape, a.dtype) for a in arrays], jax.ShapeDtypeStruct((8, LANES), F32)),
        input_output_aliases={i: 2 + i for i in range(len(arrays))},
        compiler_params=pltpu.CompilerParams(has_side_effects=_EFFECT),
    )(*[_hbm(a) for a in arrays], *afters)
    send, recv = outs[0], outs[1]
    thru = list(outs[2:-1])
    it = iter(thru[:n_src])
    return send, recv, [next(it) if h else None for h in has_src], thru[n_src:], outs[-1]


def _copies_wait(name, plans, started, after):
    send, recv, srcs, lands, _ = started
    has_src = [s is not None for s in srcs]
    arrays = [s for s in srcs if s is not None] + list(lands)
    n_src = sum(has_src)

    def body(*refs):
        ins = refs[:len(arrays)]
        send_ref, recv_ref = refs[len(arrays)], refs[len(arrays) + 1]
        it = iter(ins[:n_src])
        src_refs = [next(it) if h else None for h in has_src]
        for cp in _plan_descriptors(plans, src_refs, ins[n_src:], send_ref, recv_ref):
            cp.wait_send()
            cp.wait_recv()

    outs = pl.pallas_call(
        body, name=name,
        in_specs=[_HBM] * len(arrays) + [_SEM, _SEM, pl.BlockSpec(memory_space=pl.ANY)],
        out_specs=tuple([_HBM] * len(arrays)),
        out_shape=tuple(pltpu.HBM(a.shape, a.dtype) for a in arrays),
        input_output_aliases={i: i for i in range(len(arrays))},
        compiler_params=pltpu.CompilerParams(has_side_effects=_EFFECT),
    )(*arrays, send, recv, after)
    return list(outs[:n_src]), list(outs[n_src:])


def _shard_tiles(rows, cols, tr):
    if rows % tr == 0:
        return (tr, cols), rows // tr, lambda r: (r, 0)
    tc = 2 * LANES
    return (rows, tc), cols // tc, lambda r: (0, r)


def _pair_sum(name, grad, from_sibling, blocks):
    _, rows, cols = grad.shape
    (br_, bc), steps, at = _shard_tiles(rows, cols, 256)

    def body(blk_ref, g_ref, s_ref, o_ref):
        o_ref[...] = (g_ref[...] + s_ref[...]).astype(BF16)

    return pl.pallas_call(
        body, name=name,
        grid_spec=pltpu.PrefetchScalarGridSpec(
            num_scalar_prefetch=1, grid=(3, steps),
            in_specs=[pl.BlockSpec((None, br_, bc), lambda j, r, br: (br[j], *at(r))),
                      pl.BlockSpec((None, br_, bc), lambda j, r, br: (br[3 + j], *at(r)))],
            out_specs=pl.BlockSpec((None, br_, bc), lambda j, r, br: (j, *at(r)))),
        out_shape=jax.ShapeDtypeStruct((3, rows, cols), BF16),
        compiler_params=_cparams(("parallel", "parallel")),
    )(blocks, grad, from_sibling)


def _adam_math(g, w, m, v):
    m2 = ADAM_B1 * m + (1.0 - ADAM_B1) * g
    v2 = ADAM_B2 * v + (1.0 - ADAM_B2) * (g * g)
    m_hat = m2 / (1.0 - ADAM_B1 ** ADAM_STEP)
    v_hat = v2 / (1.0 - ADAM_B2 ** ADAM_STEP)
    delta = -ADAM_LR * (m_hat / (jnp.sqrt(v_hat) + ADAM_EPS) + ADAM_WD * w)
    return delta, m2, v2


def _sum_adam(name, grad, from_sibling, from_chips, own, w, m, v, dep=None):
    rows, cols = w.shape
    (br_, bc), steps, at = _shard_tiles(rows, cols, 128)

    deps, dep_specs = _dep_operand(dep)

    def body(own_ref, p_ref, s_ref, r_ref, w_ref, m_ref, v_ref, *rest):
        g_out, d_out, m_out, v_out = rest[len(deps):]
        g = p_ref[...] + s_ref[...]
        for j in range(3):
            g = g + r_ref[j].astype(F32)
        d, m2, v2 = _adam_math(g, w_ref[...], m_ref[...], v_ref[...])
        g_out[...] = g
        d_out[...] = d
        m_out[...] = m2
        v_out[...] = v2

    blk = pl.BlockSpec((br_, bc), lambda r, cr: at(r))
    shp = jax.ShapeDtypeStruct((rows, cols), F32)
    return pl.pallas_call(
        body, name=name,
        grid_spec=pltpu.PrefetchScalarGridSpec(
            num_scalar_prefetch=1, grid=(steps,),
            in_specs=[pl.BlockSpec((None, br_, bc), lambda r, cr: (cr[0], *at(r))),
                      pl.BlockSpec((None, br_, bc), lambda r, cr: (cr[1], *at(r))),
                      pl.BlockSpec((3, br_, bc), lambda r, cr: (0, *at(r))),
                      blk, blk, blk, *dep_specs],
            out_specs=[blk, blk, blk, blk]),
        out_shape=[shp, shp, shp, shp],
        compiler_params=_cparams(("parallel",)),
    )(own, grad, from_sibling, from_chips, w, m, v, *deps)


def _adam_small(name, g, w, m, v):
    def body(g_ref, w_ref, m_ref, v_ref, d_out, m_out, v_out):
        d, m2, v2 = _adam_math(g_ref[...], w_ref[...], m_ref[...], v_ref[...])
        d_out[...] = d
        m_out[...] = m2
        v_out[...] = v2

    shp = jax.ShapeDtypeStruct(w.shape, F32)
    return pl.pallas_call(body, name=name, out_shape=[shp, shp, shp])(g, w, m, v)


def _all_reduce_small(pack, dep=None):
    rows = pack.shape[0]
    deps = () if dep is None else (dep,)

    def body(in_ref, *rest):
        out_ref, slots, send, recv = rest[len(deps):]
        x, y, c, _ = _place()
        me = 4 * x + 2 * y + c
        slots[me] = in_ref[...]
        cps = []
        for k in range(1, N_DEV):
            dx, dy, dc = (k >> 2) & 1, (k >> 1) & 1, k & 1
            to = (jnp.bitwise_xor(x, dx), jnp.bitwise_xor(y, dy), jnp.bitwise_xor(c, dc))
            cps.append(pltpu.make_async_remote_copy(
                src_ref=in_ref, dst_ref=slots.at[me], send_sem=send.at[k - 1], recv_sem=recv.at[k - 1],
                device_id=to, device_id_type=MESH))
        for cp in cps:
            cp.start()
        for cp in cps:
            cp.wait()
        acc = slots[0]
        for d in range(1, N_DEV):
            acc = acc + slots[d]
        out_ref[...] = acc

    return pl.pallas_call(
        body, name="all_reduce_small",
        in_specs=[pl.BlockSpec(memory_space=pltpu.VMEM)] * (1 + len(deps)),
        out_specs=pl.BlockSpec(memory_space=pltpu.VMEM),
        out_shape=jax.ShapeDtypeStruct((rows, LANES), F32),
        scratch_shapes=[pltpu.VMEM((N_DEV, rows, LANES), F32),
                        pltpu.SemaphoreType.DMA((N_DEV - 1,)), pltpu.SemaphoreType.DMA((N_DEV - 1,))],
    )(pack, *deps)


def _rows128(a):
    flat = a.reshape(-1)
    padn = (-flat.shape[0]) % (8 * LANES)
    if padn:
        flat = jnp.concatenate([flat, jnp.zeros((padn,), flat.dtype)])
    return flat.reshape(-1, LANES)


def kernel(x, w_in, w_gk2, b_gk, gla_norm_w, swa_sinks, w_out, ln1_g, ln1_b, w_up, w_down, ln2_g, ln2_b, loss_target, m_w_in, m_w_gk2, m_b_gk, m_gla_norm_w, m_swa_sinks, m_w_out, m_ln1_g, m_ln1_b, m_w_up, m_w_down, m_ln2_g, m_ln2_b, v_w_in, v_w_gk2, v_b_gk, v_gla_norm_w, v_swa_sinks, v_w_out, v_ln1_g, v_ln1_b, v_w_up, v_w_down, v_ln2_g, v_ln2_b):
    xc, yc, cc = lax.axis_index("x"), lax.axis_index("y"), lax.axis_index("c")
    me = 4 * xc + 2 * yc + cc

    x2 = x[0]
    t = x2.shape[0]
    target = loss_target[0]

    def land_of(sh):
        return lax.dynamic_update_index_in_dim(lax.empty((N_DEV,) + sh.shape, BF16), sh, me, 0)

    out_plans = [_plan_gather_out] * 3
    fwd_plans = [_plan_gather_forward] * 3
    in_shards = [w_in[0].T.astype(BF16), w_gk2[0].astype(BF16)]
    ag0 = _copies_start("gather_in_start", out_plans[:2], in_shards, [land_of(sh) for sh in in_shards])
    zero = ag0[4][0, 0]
    xb = (x2 + zero).astype(BF16)
    shards = [(w[0] + zero).astype(BF16) for w in (w_out, w_up, w_down)]
    lands = [land_of(sh) for sh in shards]
    _, in_lands = _copies_wait("gather_in_wait", out_plans[:2], ag0, lands[2])
    ag0f = _copies_start("gather_in_forward_start", fwd_plans[:2], [None] * 2, in_lands)
    _, first = _copies_wait("gather_in_forward_wait", fwd_plans[:2], ag0f, ag0f[4])
    w_in_rows = first[0].reshape(D_IN, D_MODEL)
    w_in_t = _to_padded_rows(w_in_rows)
    w_in_o = jnp.concatenate([w_in_rows, jnp.zeros((D_INP - D_IN, D_MODEL), BF16)], axis=0)
    w2 = first[1].transpose(1, 0, 2).reshape(GATE_RANK, GLA_KW)
    w2p = jnp.concatenate([w2, jnp.zeros((LANES - GATE_RANK, GLA_KW), BF16)], axis=0)
    ag1 = _copies_start("gather_out_start", out_plans, shards, lands, first[0])

    proj = _proj(xb, w_in_t, dep=ag1[4])
    rt = min(256, t)
    ii = jnp.arange(rt)
    tri = ((ii[:, None] // GC == ii[None, :] // GC) & (ii[None, :] <= ii[:, None])).astype(F32)
    bloc, dgdz = _gate_fwd(proj, w2p, b_gk, tri)
    o_raw, states = _gla_fwd(proj, bloc)
    gla_out = _gla_norm_fwd(o_raw, proj, gla_norm_w)
    cat = _swa_fwd(proj, swa_sinks[0], gla_out)
    _, lands = _copies_wait("gather_out_wait", out_plans, ag1, cat)
    ag2 = _copies_start("gather_forward_out_start", fwd_plans[:1], [None], lands[:1])
    _, (w_out_g,) = _copies_wait("gather_forward_out_wait", fwd_plans[:1], ag2, ag2[4])
    ag3 = _copies_start("gather_forward_mlp_start", fwd_plans[:2], [None] * 2, lands[1:], w_out_g)
    w_out_f = w_out_g.reshape(D_MODEL, D_MODEL)
    r1, h1, h1b = _mix_ln1(cat, w_out_f, x2, ln1_g, ln1_b, dep=ag3[4])
    _, (w_up_f, w_down_g) = _copies_wait("gather_forward_mlp_wait", fwd_plans[:2], ag3, h1b)
    w_down_f = w_down_g.reshape(D_FF, D_MODEL)
    a_act, hdn = _mlp_up(h1b, w_up_f)
    dr2, dr2b, g_ln2_g, g_ln2_b, loss_part = _mlp_down_loss(hdn, w_down_f, h1, target, ln2_g, ln2_b)

    others = [2 * (1 - xc) + yc, 2 * xc + (1 - yc), 2 * (1 - xc) + (1 - yc)]
    blocks = jnp.stack([2 * q + cc for q in others] + others).astype(jnp.int32)
    own = jnp.stack([me, 2 * xc + yc]).astype(jnp.int32)
    wmv = dict(w_in=[a[0].T for a in (w_in, m_w_in, v_w_in)], w_out=[a[0] for a in (w_out, m_w_out, v_w_out)],
               w_up=[a[0] for a in (w_up, m_w_up, v_w_up)], w_down=[a[0] for a in (w_down, m_w_down, v_w_down)])
    big = {}

    def sib_land(g):
        return lax.empty((4,) + g.shape[1:], F32)

    def chip_land(g):
        return lax.empty((3,) + g.shape[1:], BF16)

    def finish(nm, g, from_sib, from_chips, dep=None):
        outs = _sum_adam("sum_adam_" + nm, g, from_sib, from_chips, own, *wmv[nm], dep=dep)
        big[nm] = [(o.T if nm == "w_in" else o)[None] for o in outs]

    du = _mlp_down_bwd(dr2b, w_down_f, a_act)
    g_down = _grad_w_down(hdn, dr2b)
    sa_down = _copies_start("sibling_down_start", [_plan_sibling], [g_down], [sib_land(g_down)])
    g_up = _grad_w_up(h1b, du, dep=sa_down[4])
    (g_down,), (fs_down,) = _copies_wait("sibling_down_wait", [_plan_sibling], sa_down, g_up)
    p_down = _pair_sum("pair_sum_w_down", g_down, fs_down, blocks)
    sb_down = _copies_start("chips_down_sibling_up_start", [_plan_chips, _plan_sibling], [p_down, g_up],
                            [chip_land(g_down), sib_land(g_up)])
    dr1, dr1b, g_ln1_g, g_ln1_b = _mlp_up_bwd_ln1(du, w_up_f, dr2, r1, ln1_g, dep=sb_down[4])
    dcat = _dcat(dr1b, w_out_f)
    (_, g_up), (fc_down, fs_up) = _copies_wait("chips_down_sibling_up_wait", [_plan_chips, _plan_sibling], sb_down,
                                               dcat)
    p_up = _pair_sum("pair_sum_w_up", g_up, fs_up, blocks)
    sb_up = _copies_start("chips_up_start", [_plan_chips], [p_up], [chip_land(g_up)])
    do_raw, dg_g, g_norm_w = _gla_norm_bwd(dcat, o_raw, proj, gla_norm_w, dep=sb_up[4])
    dq_g, dk_g, dv_g, dgk = _gla_bwd(proj, bloc, do_raw, states)
    _, (fc_up,) = _copies_wait("chips_up_wait", [_plan_chips], sb_up, dgk)
    dlo, gw2_p, g_b_gk = _gate_bwd(dgk, dgdz, proj, w2p, tri.T)
    dq_s, dk_s, dv_s, g_sinks = _swa_bwd(proj, dcat, swa_sinks[0])
    dproj = jnp.concatenate([dq_g, dk_g, dv_g, dg_g, dlo[:, :GATE_RANK], dq_s, dk_s.astype(BF16), dv_s.astype(BF16),
                             jnp.zeros((t, D_INP - D_IN), BF16)], axis=-1)
    gw_in_t = _grad_w_in(xb, dproj)
    g_in = gw_in_t.reshape(N_DEV, D_IN // N_DEV, D_MODEL)
    sa_in = _copies_start("sibling_in_start", [_plan_sibling], [g_in], [sib_land(g_in)])
    g_out = _grad_w_out(cat, dr1b, dep=sa_in[4])
    (g_in,), (fs_in,) = _copies_wait("sibling_in_wait", [_plan_sibling], sa_in, g_out)
    p_in = _pair_sum("pair_sum_w_in", g_in, fs_in, blocks)
    sb_in = _copies_start("chips_in_sibling_out_start", [_plan_chips, _plan_sibling], [p_in, g_out],
                          [chip_land(g_in), sib_land(g_out)])
    grad_x = _grad_x(dproj, w_in_o, dr1, dep=sb_in[4])
    (_, g_out), (fc_in, fs_out) = _copies_wait("chips_in_sibling_out_wait", [_plan_chips, _plan_sibling], sb_in,
                                               grad_x)
    p_out = _pair_sum("pair_sum_w_out", g_out, fs_out, blocks)
    sb_out = _copies_start("chips_out_start", [_plan_chips], [p_out], [chip_land(g_out)])
    def done(nm):
        return big[nm][0][0, :8, :LANES]

    finish("w_in", g_in, fs_in, fc_in, dep=sb_out[4])
    finish("w_down", g_down, fs_down, fc_down, dep=done("w_in"))
    finish("w_up", g_up, fs_up, fc_up, dep=done("w_down"))

    pieces = [loss_part, g_b_gk, g_norm_w, g_sinks[:, :SWA_HEADS], g_ln1_g, g_ln1_b, g_ln2_g, g_ln2_b,
              gw2_p[:GATE_RANK]]
    pack = jnp.concatenate([_rows128(p) for p in pieces], axis=0)
    tot = _all_reduce_small(pack, dep=done("w_up"))
    sizes = [p.size for p in pieces]
    offs = [0]
    for p in pieces:
        offs.append(offs[-1] + _rows128(p).shape[0])
    unpack = lambda i, shape: tot[offs[i]:offs[i + 1]].reshape(-1)[:sizes[i]].reshape(shape)
    loss = tot[0, 0]
    small_names = ["b_gk", "gla_norm_w", "swa_sinks", "ln1_g", "ln1_b", "ln2_g", "ln2_b"]
    small_w = dict(b_gk=(b_gk, m_b_gk, v_b_gk), gla_norm_w=(gla_norm_w, m_gla_norm_w, v_gla_norm_w),
                   swa_sinks=(swa_sinks, m_swa_sinks, v_swa_sinks), ln1_g=(ln1_g, m_ln1_g, v_ln1_g),
                   ln1_b=(ln1_b, m_ln1_b, v_ln1_b), ln2_g=(ln2_g, m_ln2_g, v_ln2_g),
                   ln2_b=(ln2_b, m_ln2_b, v_ln2_b))
    small_g = {nm: unpack(1 + i, small_w[nm][0].shape) for i, nm in enumerate(small_names)}
    g_pack = jnp.concatenate([_rows128(small_g[nm]) for nm in small_names], axis=0)
    small_wmv = [jnp.concatenate([_rows128(small_w[nm][k]) for nm in small_names], axis=0) for k in range(3)]
    small_out = _adam_small("adam_replicated", g_pack, *small_wmv)
    srow = [0]
    for nm in small_names:
        srow.append(srow[-1] + _rows128(small_w[nm][0]).shape[0])
    small = {}
    for i, nm in enumerate(small_names):
        shape = small_w[nm][0].shape
        n = small_w[nm][0].size
        small[nm] = [small_g[nm]] + [o[srow[i]:srow[i + 1]].reshape(-1)[:n].reshape(shape) for o in small_out]

    gw2_full = unpack(8, (GATE_RANK, GLA_KW))
    gw2_loc = lax.dynamic_slice_in_dim(gw2_full, me * (GLA_KW // N_DEV), GLA_KW // N_DEV, axis=1)
    gk2_out = _adam_small("adam_w_gk2", gw2_loc, w_gk2[0], m_w_gk2[0], v_w_gk2[0])
    big["w_gk2"] = [gw2_loc[None]] + [o[None] for o in gk2_out]

    _, (fc_out,) = _copies_wait("chips_out_wait", [_plan_chips], sb_out, tot)
    finish("w_out", g_out, fs_out, fc_out)

    order = ["w_in", "w_gk2", "b_gk", "gla_norm_w", "swa_sinks", "w_out", "ln1_g", "ln1_b", "w_up", "w_down",
             "ln2_g", "ln2_b"]
    res = {**big, **small}
    outs = [loss, grad_x[None]]
    for k in range(4):
        outs += [res[nm][k] for nm in order]
    return tuple(outs)
```

```python
import functools

import jax
import jax.numpy as jnp
from jax import lax
from jax.experimental import pallas as pl
from jax.experimental.pallas import tpu as pltpu

F32 = jnp.float32
BF16 = jnp.bfloat16

N_DEV = 8
D_MODEL = 2048
D_FF = 8192
GLA_HEADS = 4
GLA_DK = 128
GLA_DV = 256
GLA_KW = 512
D_GLA = 1024
GATE_RANK = 16
GATE_TAU = 16.0
SWA_HEADS = 16
SWA_DH = 64
SWA_GROUP = 8
SWA_STACKS = ((0, 1, 2, 3),)
WINDOW = 128
D_SWA = 1024
D_IN = 4368
ALPHA = 2.0 ** 0.25
LN_EPS = 1e-5
RMS_EPS = 1e-5
ADAM_LR = 0.001
ADAM_B1 = 0.9
ADAM_B2 = 0.999
ADAM_EPS = 1e-08
ADAM_WD = 0.01
ADAM_STEP = 10

C_QG, C_KG, C_VG, C_GG, C_QS, C_KS, C_VS, C_LO = 0, 512, 1024, 2048, 3072, 4096, 4224, 4352
D_INP = 4480
LANES = 128
GC = 16
NEG = -1e30

NN = ((1,), (0,))
NT = ((1,), (1,))
TN = ((0,), (0,))

VMEM_LIMIT = 52 * 1024 * 1024
MESH = pl.DeviceIdType.MESH


def _dot(a, b, dn, precision=None):
    return lax.dot_general(a, b, (dn, ((), ())), preferred_element_type=F32, precision=precision)


def _bf16_round(v):
    return v.astype(BF16).astype(F32)


def _cparams(dims):
    return pltpu.CompilerParams(dimension_semantics=dims, vmem_limit_bytes=VMEM_LIMIT)


def _dep_operand(dep):
    if dep is None:
        return (), ()
    return (dep,), (pl.BlockSpec(dep.shape, lambda *_: (0,) * dep.ndim),)


def _to_padded_rows(w):
    pad = jnp.zeros((D_INP - D_IN,) + w.shape[1:], w.dtype)
    return jnp.concatenate([w[:3072], w[3088:], w[3072:3088], pad], axis=0)


def _matmul(name, a, b, dn, grid, a_spec, b_spec, out_shape, out_specs, acc_shape, *,
            extra=(), extra_specs=(), epilogue=None, dims=("parallel", "parallel", "arbitrary"), dep=None):
    nk = grid[2]
    n_extra = len(extra)
    deps, dep_specs = _dep_operand(dep)
    direct = epilogue is None and (nk == 1 or (not isinstance(out_shape, (list, tuple)) and out_shape.dtype == F32))

    scratch = [] if direct or nk == 1 else [pltpu.VMEM(acc_shape, F32)]

    def body(a_ref, b_ref, *rest):
        extra_refs = rest[:n_extra]
        out_refs = rest[n_extra + len(deps):len(rest) - len(scratch)]
        acc_ref = rest[-1] if scratch else out_refs[0]
        part = _dot(a_ref[...].astype(BF16), b_ref[...].astype(BF16), dn)

        def finish():
            if epilogue is None:
                out_refs[0][...] = acc_ref[...].astype(out_refs[0].dtype)
            else:
                epilogue(acc_ref, extra_refs, out_refs)

        if direct and nk == 1:
            out_refs[0][...] = part.reshape(out_refs[0].shape).astype(out_refs[0].dtype)
        elif nk == 1:
            epilogue(part, extra_refs, out_refs)
        else:
            k = pl.program_id(2)

            @pl.when(k == 0)
            def _():
                acc_ref[...] = part

            @pl.when(k > 0)
            def _():
                acc_ref[...] += part

            if not direct:
                @pl.when(k == nk - 1)
                def _():
                    finish()

    return pl.pallas_call(
        body, name=name, grid=grid,
        in_specs=[a_spec, b_spec, *extra_specs, *dep_specs],
        out_specs=out_specs, out_shape=out_shape,
        scratch_shapes=scratch,
        compiler_params=_cparams(dims),
    )(a, b, *extra, *deps)


def _row_chunks(rows, step=128):
    step = min(step, rows)
    return [(r, step) for r in range(0, rows, step)]


def _ln_stats(r):
    mu = jnp.mean(r, axis=-1, keepdims=True)
    xc = r - mu
    var = jnp.mean(xc * xc, axis=-1, keepdims=True)
    rstd = lax.rsqrt(var + LN_EPS)
    return xc * rstd, rstd


def _ln_bwd(dy_g, xhat, rstd):
    m1 = jnp.mean(dy_g, axis=-1, keepdims=True)
    m2 = jnp.mean(dy_g * xhat, axis=-1, keepdims=True)
    return rstd * (dy_g - m1 - xhat * m2)


def _proj(xb, w_in_t, dep=None):
    t = xb.shape[0]
    tm, tn = min(1024, t), 896
    return _matmul(
        "proj", xb, w_in_t, NT, (t // tm, D_INP // tn, 1),
        pl.BlockSpec((tm, D_MODEL), lambda i, j, k: (i, 0)),
        pl.BlockSpec((tn, D_MODEL), lambda i, j, k: (j, 0)),
        jax.ShapeDtypeStruct((t, D_INP), F32),
        pl.BlockSpec((tm, tn), lambda i, j, k: (i, j)),
        (tm, tn), dep=dep)


def _gate_fwd(proj, w2p, b_gk, tri):
    t = proj.shape[0]
    r = tri.shape[0]

    def body(lo_ref, w_ref, b_ref, tri_ref, bloc_ref, dgdz_ref):
        z = _dot(lo_ref[...].astype(BF16), w_ref[...], NN) + b_ref[...]
        e = jnp.exp(-jnp.abs(z))
        gk = (jnp.minimum(z, 0.0) - jnp.log1p(e)) * (1.0 / GATE_TAU)
        inv = 1.0 / (1.0 + e)
        dgdz_ref[...] = jnp.where(z >= 0.0, e * inv, inv) * (1.0 / GATE_TAU)
        bloc_ref[...] = _dot(tri_ref[...], gk, NN, precision=lax.Precision.HIGHEST)

    return pl.pallas_call(
        body, name="gate_fwd", grid=(t // r,),
        in_specs=[pl.BlockSpec((r, LANES), lambda i: (i, C_LO // LANES)),
                  pl.BlockSpec((LANES, GLA_KW), lambda i: (0, 0)),
                  pl.BlockSpec((1, GLA_KW), lambda i: (0, 0)),
                  pl.BlockSpec((r, r), lambda i: (0, 0))],
        out_specs=[pl.BlockSpec((r, GLA_KW), lambda i: (i, 0)),
                   pl.BlockSpec((r, GLA_KW), lambda i: (i, 0))],
        out_shape=[jax.ShapeDtypeStruct((t, GLA_KW), F32), jax.ShapeDtypeStruct((t, GLA_KW), F32)],
        compiler_params=_cparams(("parallel",)),
    )(proj, w2p, b_gk, tri)


def _gla_fwd(proj, bloc):
    t = proj.shape[0]
    r = min(256, t)
    ncb = r // GC
    scale = GLA_DK ** -0.5

    def body(q_ref, k_ref, v_ref, b_ref, o_ref, st_ref, s_scr, m_scr):
        @pl.when(pl.program_id(1) == 0)
        def _():
            s_scr[...] = jnp.zeros_like(s_scr)

        rows = lax.broadcasted_iota(jnp.int32, (GC, 1), 0)
        cols = lax.broadcasted_iota(jnp.int32, (1, GC), 1)

        def increment(c, carry):
            r0 = pl.multiple_of(c * GC, GC)
            k = k_ref[pl.ds(r0, GC), :]
            b = b_ref[pl.ds(r0, GC), :]
            kd = k * jnp.exp(b[GC - 1:GC, :] - b)
            m_scr[c] = _dot(v_ref[pl.ds(r0, GC), :].astype(BF16), kd.astype(BF16), TN)
            return carry

        lax.fori_loop(0, ncb, increment, 0, unroll=8)

        def recur(c, st):
            st_ref[c] = st
            r0 = pl.multiple_of(c * GC, GC)
            bl = b_ref[pl.ds(r0, GC), :][GC - 1:GC, :]
            return st * jnp.exp(bl) + m_scr[c]

        s_scr[...] = lax.fori_loop(0, ncb, recur, s_scr[...])

        def output(c, carry):
            r0 = pl.multiple_of(c * GC, GC)
            q = q_ref[pl.ds(r0, GC), :] * scale
            b = b_ref[pl.ds(r0, GC), :]
            kr = _bf16_round(k_ref[pl.ds(r0, GC), :])
            att = jnp.zeros((GC, GC), F32)
            for j in range(GC):
                w = jnp.exp(jnp.where(rows >= j, b - b[j:j + 1, :], NEG))
                a = jnp.sum(_bf16_round(q * w) * kr[j:j + 1, :], axis=-1, keepdims=True)
                att = jnp.where(cols == j, a, att)
            o_ref[pl.ds(r0, GC), :] = (_dot((q * jnp.exp(b)).astype(BF16), st_ref[c].astype(BF16), NT)
                                       + _dot(att.astype(BF16), v_ref[pl.ds(r0, GC), :].astype(BF16), NN))
            return carry

        lax.fori_loop(0, ncb, output, 0, unroll=8)

    return pl.pallas_call(
        body, name="gla_fwd", grid=(GLA_HEADS, t // r),
        in_specs=[pl.BlockSpec((r, GLA_DK), lambda h, i: (i, C_QG // GLA_DK + h)),
                  pl.BlockSpec((r, GLA_DK), lambda h, i: (i, C_KG // GLA_DK + h)),
                  pl.BlockSpec((r, GLA_DV), lambda h, i: (i, C_VG // GLA_DV + h)),
                  pl.BlockSpec((r, GLA_DK), lambda h, i: (i, h))],
        out_specs=[pl.BlockSpec((r, GLA_DV), lambda h, i: (i, h)),
                   pl.BlockSpec((ncb, None, GLA_DV, GLA_DK), lambda h, i: (i, h, 0, 0))],
        out_shape=[jax.ShapeDtypeStruct((t, D_GLA), F32),
                   jax.ShapeDtypeStruct((t // GC, GLA_HEADS, GLA_DV, GLA_DK), F32)],
        scratch_shapes=[pltpu.VMEM((GLA_DV, GLA_DK), F32), pltpu.VMEM((ncb, GLA_DV, GLA_DK), F32)],
        compiler_params=_cparams(("parallel", "arbitrary")),
    )(proj, proj, proj, bloc)


def _gla_norm_fwd(o_raw, proj, norm_w):
    t = o_raw.shape[0]
    r = min(512, t)

    def body(o_ref, g_ref, w_ref, out_ref):
        w = w_ref[...]
        for h in range(GLA_HEADS):
            sl = slice(h * GLA_DV, (h + 1) * GLA_DV)
            o = o_ref[:, sl]
            g = g_ref[:, sl]
            on = o * lax.rsqrt(jnp.mean(o * o, axis=-1, keepdims=True) + RMS_EPS)
            out_ref[:, sl] = (on * w * (g * jax.nn.sigmoid(g))).astype(BF16)

    return pl.pallas_call(
        body, name="gla_norm_fwd", grid=(t // r,),
        in_specs=[pl.BlockSpec((r, D_GLA), lambda i: (i, 0)),
                  pl.BlockSpec((r, D_GLA), lambda i: (i, C_GG // D_GLA)),
                  pl.BlockSpec((1, GLA_DV), lambda i: (0, 0))],
        out_specs=pl.BlockSpec((r, D_GLA), lambda i: (i, 0)),
        out_shape=jax.ShapeDtypeStruct((t, D_GLA), BF16),
        compiler_params=_cparams(("parallel",)),
    )(o_raw, proj, norm_w)


def _swa_masks(i, heads):
    qi = lax.broadcasted_iota(jnp.int32, (heads * WINDOW, 1), 0) & (WINDOW - 1)
    kj = lax.broadcasted_iota(jnp.int32, (1, 2 * WINDOW), 1)
    valid = (kj > qi) & (kj <= qi + WINDOW) & ((i > 0) | (kj >= WINDOW))
    lo = lax.broadcasted_iota(jnp.int32, (1, LANES), 1) < SWA_DH
    return valid, lo


def _dup_half(x, lo, kv):
    xr = pltpu.roll(x, SWA_DH, axis=1)
    return jnp.where(lo, x, xr) if kv == 0 else jnp.where(lo, xr, x)


def _swa_stack(ref, col0, lo, kv, pairs):
    parts = []
    for p in pairs:
        c0 = col0 + LANES * (4 * kv + p)
        xp = ref[:, c0:c0 + LANES]
        parts += [jnp.where(lo, xp, 0.0).astype(BF16), jnp.where(lo, 0.0, xp).astype(BF16)]
    return jnp.concatenate(parts, axis=0)


def _swa_unstack(x_all, lo, p):
    r0 = 2 * p * WINDOW
    return jnp.where(lo, x_all[r0:r0 + WINDOW, :], x_all[r0 + WINDOW:r0 + 2 * WINDOW, :])


def _swa_sinks(sink_ref, kv, pairs):
    heads = [SWA_GROUP * kv + 2 * p + e for p in pairs for e in range(2)]
    return jnp.concatenate([jnp.full((WINDOW, 1), sink_ref[h], F32) for h in heads], axis=0)


def _swa_probs(qm, kdup, valid, sink):
    s = _dot(qm, kdup, NT) * (SWA_DH ** -0.5)
    s = jnp.where(valid, s, NEG)
    m = jnp.maximum(jnp.max(s, axis=-1, keepdims=True), sink)
    p = jnp.exp(s - m)
    es = jnp.exp(sink - m)
    inv = 1.0 / (jnp.sum(p, axis=-1, keepdims=True) + es)
    return p * inv, es * inv


def _swa_fwd(proj, sinks, gla_out, dep=None):
    t = proj.shape[0]

    deps, dep_specs = _dep_operand(dep)

    def body(sink_ref, q_ref, kp_ref, kc_ref, vp_ref, vc_ref, gla_ref, *rest):
        o_ref = rest[-1]
        i = pl.program_id(0)
        o_ref[:, :D_GLA] = gla_ref[...]
        valid, lo = _swa_masks(i, 1)
        kb = jnp.concatenate([kp_ref[...], kc_ref[...]], axis=0)
        vb = jnp.concatenate([vp_ref[...], vc_ref[...]], axis=0)
        for kv in range(2):
            kdup = _dup_half(kb, lo, kv).astype(BF16)
            vdup = _dup_half(vb, lo, kv).astype(BF16)
            for p in range(4):
                c0 = LANES * (4 * kv + p)
                qp = q_ref[:, c0:c0 + LANES]
                halves = []
                for e in range(2):
                    qm = jnp.where(lo if e == 0 else ~lo, qp, 0.0).astype(BF16)
                    pn, _ = _swa_probs(qm, kdup, valid, sink_ref[SWA_GROUP * kv + 2 * p + e])
                    halves.append(_dot(pn.astype(BF16), vdup, NN))
                o_ref[:, D_GLA + c0:D_GLA + c0 + LANES] = jnp.where(lo, halves[0], halves[1]).astype(BF16)

    kvspec = lambda col, prev: pl.BlockSpec(
        (WINDOW, LANES), (lambda i: (jnp.maximum(i - 1, 0), col)) if prev else (lambda i: (i, col)))
    return pl.pallas_call(
        body, name="swa_fwd", grid=(t // WINDOW,),
        in_specs=[pl.BlockSpec(memory_space=pltpu.SMEM),
                  pl.BlockSpec((WINDOW, D_SWA), lambda i: (i, C_QS // D_SWA)),
                  kvspec(C_KS // LANES, True), kvspec(C_KS // LANES, False),
                  kvspec(C_VS // LANES, True), kvspec(C_VS // LANES, False),
                  pl.BlockSpec((WINDOW, D_GLA), lambda i: (i, 0)), *dep_specs],
        out_specs=pl.BlockSpec((WINDOW, D_MODEL), lambda i: (i, 0)),
        out_shape=jax.ShapeDtypeStruct((t, D_MODEL), BF16),
        compiler_params=_cparams(("parallel",)),
    )(sinks, proj, proj, proj, proj, proj, gla_out, *deps)


def _mix_ln1(cat, w_out, x, ln_g, ln_b, dep=None):
    t = cat.shape[0]
    tm, tk = min(256, t), D_MODEL

    def epilogue(acc_ref, extra, outs):
        x_ref, g_ref, b_ref = extra
        r1_ref, h1_ref, h1b_ref = outs
        for r0, n in _row_chunks(tm):
            rs = slice(r0, r0 + n)
            r1 = ALPHA * x_ref[rs, :] + acc_ref[rs, :]
            xhat, _ = _ln_stats(r1)
            h = xhat * g_ref[...] + b_ref[...]
            r1_ref[rs, :] = r1
            h1_ref[rs, :] = h
            h1b_ref[rs, :] = h.astype(BF16)

    row = pl.BlockSpec((tm, D_MODEL), lambda i, j, k: (i, 0))
    vec = pl.BlockSpec((1, D_MODEL), lambda i, j, k: (0, 0))
    return _matmul(
        "mix_ln1", cat, w_out, NN, (t // tm, 1, D_MODEL // tk),
        pl.BlockSpec((tm, tk), lambda i, j, k: (i, k)),
        pl.BlockSpec((tk, D_MODEL), lambda i, j, k: (k, 0)),
        [jax.ShapeDtypeStruct((t, D_MODEL), F32), jax.ShapeDtypeStruct((t, D_MODEL), F32),
         jax.ShapeDtypeStruct((t, D_MODEL), BF16)],
        [row, row, row], (tm, D_MODEL),
        extra=(x, ln_g, ln_b), extra_specs=(row, vec, vec), epilogue=epilogue, dep=dep)


def _mlp_up(h1b, w_up):
    t = h1b.shape[0]
    tm, tn = min(1024, t), 1024

    def epilogue(acc_ref, extra, outs):
        a_ref, hdn_ref = outs
        for r0, n in _row_chunks(tm, 256):
            rs = slice(r0, r0 + n)
            a = jnp.maximum(acc_ref[rs, :], 0.0)
            a_ref[rs, :] = a.astype(BF16)
            hdn_ref[rs, :] = (a * a).astype(BF16)

    out = pl.BlockSpec((tm, tn), lambda i, j, k: (i, j))
    return _matmul(
        "mlp_up", h1b, w_up, NN, (t // tm, D_FF // tn, 1),
        pl.BlockSpec((tm, D_MODEL), lambda i, j, k: (i, 0)),
        pl.BlockSpec((None, D_MODEL, tn), lambda i, j, k: (j, 0, 0)),
        [jax.ShapeDtypeStruct((t, D_FF), BF16), jax.ShapeDtypeStruct((t, D_FF), BF16)],
        [out, out], (tm, tn), epilogue=epilogue)


def _mlp_down_loss(hdn, w_down, h1, target, ln_g, ln_b):
    t = hdn.shape[0]
    tm, tn, tk = min(1024, t), 1024, 4096
    ff = _matmul(
        "mlp_down", hdn, w_down, NN, (t // tm, D_MODEL // tn, D_FF // tk),
        pl.BlockSpec((tm, tk), lambda i, j, k: (i, k)),
        pl.BlockSpec((tk, tn), lambda i, j, k: (k, j)),
        jax.ShapeDtypeStruct((t, D_MODEL), F32),
        pl.BlockSpec((tm, tn), lambda i, j, k: (i, j)), (tm, tn))
    r = min(256, t)

    def body(ff_ref, h1_ref, t_ref, g_ref, b_ref, dr2_ref, dr2b_ref, gg_ref, gb_ref, loss_ref):
        @pl.when(pl.program_id(0) == 0)
        def _():
            gg_ref[...] = jnp.zeros_like(gg_ref)
            gb_ref[...] = jnp.zeros_like(gb_ref)
            loss_ref[...] = jnp.zeros_like(loss_ref)

        for r0, n in _row_chunks(r, 64):
            rs = slice(r0, r0 + n)
            xhat, rstd = _ln_stats(ALPHA * h1_ref[rs, :] + ff_ref[rs, :])
            err = xhat * g_ref[...] + b_ref[...] - t_ref[rs, :]
            loss_ref[...] += 0.5 * jnp.sum(jnp.mean(err * err, axis=-1, keepdims=True))
            dy = err * (1.0 / D_MODEL)
            gg_ref[...] += jnp.sum(dy * xhat, axis=0, keepdims=True)
            gb_ref[...] += jnp.sum(dy, axis=0, keepdims=True)
            dr2 = _ln_bwd(dy * g_ref[...], xhat, rstd)
            dr2_ref[rs, :] = dr2
            dr2b_ref[rs, :] = dr2.astype(BF16)

    row = pl.BlockSpec((r, D_MODEL), lambda i: (i, 0))
    vec = pl.BlockSpec((1, D_MODEL), lambda i: (0, 0))
    return pl.pallas_call(
        body, name="ln2_loss", grid=(t // r,),
        in_specs=[row, row, row, vec, vec],
        out_specs=[row, row, vec, vec, pl.BlockSpec((1, LANES), lambda i: (0, 0))],
        out_shape=[jax.ShapeDtypeStruct((t, D_MODEL), F32), jax.ShapeDtypeStruct((t, D_MODEL), BF16),
                   jax.ShapeDtypeStruct((1, D_MODEL), F32), jax.ShapeDtypeStruct((1, D_MODEL), F32),
                   jax.ShapeDtypeStruct((1, LANES), F32)],
        compiler_params=_cparams(("arbitrary",)),
    )(ff, h1, target, ln_g, ln_b)


def _mlp_down_bwd(dr2b, w_down, a_act):
    t = dr2b.shape[0]
    tm, tn = min(1024, t), 1024

    def epilogue(acc_ref, extra, outs):
        (a_ref,) = extra
        for r0, n in _row_chunks(tm, 256):
            rs = slice(r0, r0 + n)
            outs[0][rs, :] = (acc_ref[rs, :] * (2.0 * a_ref[rs, :].astype(F32))).astype(BF16)

    blk = pl.BlockSpec((tm, tn), lambda i, j, k: (i, j))
    return _matmul(
        "mlp_down_bwd", dr2b, w_down, NT, (t // tm, D_FF // tn, 1),
        pl.BlockSpec((tm, D_MODEL), lambda i, j, k: (i, 0)),
        pl.BlockSpec((tn, D_MODEL), lambda i, j, k: (j, 0)),
        jax.ShapeDtypeStruct((t, D_FF), BF16), blk, (tm, tn),
        extra=(a_act,), extra_specs=(blk,), epilogue=epilogue)


def _grad_w_down(hdn, dr2b):
    t = hdn.shape[0]
    tm, tn = 1024, 1024
    return _matmul(
        "grad_w_down", hdn, dr2b, TN, (D_MODEL // tn, D_FF // tm, 1),
        pl.BlockSpec((t, tm), lambda j, i, k: (0, i)),
        pl.BlockSpec((t, tn), lambda j, i, k: (0, j)),
        jax.ShapeDtypeStruct((N_DEV, D_FF // N_DEV, D_MODEL), F32),
        pl.BlockSpec((None, tm, tn), lambda j, i, k: (i, 0, j)), (tm, tn))


def _grad_w_up(h1b, du, dep=None):
    t = h1b.shape[0]
    tm, tn = 1024, 1024
    return _matmul(
        "grad_w_up", h1b, du, TN, (N_DEV, D_MODEL // tm, 1),
        pl.BlockSpec((t, tm), lambda i, j, k: (0, j)),
        pl.BlockSpec((t, tn), lambda i, j, k: (0, i)),
        jax.ShapeDtypeStruct((N_DEV, D_MODEL, D_FF // N_DEV), F32),
        pl.BlockSpec((None, tm, tn), lambda i, j, k: (i, j, 0)), (tm, tn), dep=dep)


def _mlp_up_bwd_ln1(du, w_up, dr2, r1, ln_g, dep=None):
    t = du.shape[0]
    tm, tn, tk, nb = min(1024, t), 1024, D_FF // N_DEV, 4
    deps, dep_specs = _dep_operand(dep)

    def mm_body(a_ref, b_ref, *rest):
        o_ref = rest[-1]
        k = pl.program_id(2)
        part = _dot(a_ref[:, :tk], b_ref[0], NT)
        for d in range(1, nb):
            part = part + _dot(a_ref[:, d * tk:(d + 1) * tk], b_ref[d], NT)

        @pl.when(k == 0)
        def _():
            o_ref[...] = part

        @pl.when(k > 0)
        def _():
            o_ref[...] += part

    dff = pl.pallas_call(
        mm_body, name="mlp_up_bwd", grid=(t // tm, D_MODEL // tn, N_DEV // nb),
        in_specs=[pl.BlockSpec((tm, nb * tk), lambda i, j, k: (i, k)),
                  pl.BlockSpec((nb, tn, tk), lambda i, j, k: (k, j, 0)), *dep_specs],
        out_specs=pl.BlockSpec((tm, tn), lambda i, j, k: (i, j)),
        out_shape=jax.ShapeDtypeStruct((t, D_MODEL), F32),
        compiler_params=_cparams(("parallel", "parallel", "arbitrary")),
    )(du, w_up, *deps)
    r = min(256, t)

    def body(acc_ref, dr2_ref, r1_ref, g_ref, dr1_ref, dr1b_ref, gg_ref, gb_ref):
        @pl.when(pl.program_id(0) == 0)
        def _():
            gg_ref[...] = jnp.zeros_like(gg_ref)
            gb_ref[...] = jnp.zeros_like(gb_ref)

        for r0, n in _row_chunks(r, 64):
            rs = slice(r0, r0 + n)
            dh1 = ALPHA * dr2_ref[rs, :] + acc_ref[rs, :]
            xhat, rstd = _ln_stats(r1_ref[rs, :])
            gg_ref[...] += jnp.sum(dh1 * xhat, axis=0, keepdims=True)
            gb_ref[...] += jnp.sum(dh1, axis=0, keepdims=True)
            dr1 = _ln_bwd(dh1 * g_ref[...], xhat, rstd)
            dr1_ref[rs, :] = dr1
            dr1b_ref[rs, :] = dr1.astype(BF16)

    row = pl.BlockSpec((r, D_MODEL), lambda i: (i, 0))
    vec = pl.BlockSpec((1, D_MODEL), lambda i: (0, 0))
    return pl.pallas_call(
        body, name="ln1_bwd", grid=(t // r,),
        in_specs=[row, row, row, vec],
        out_specs=[row, row, vec, vec],
        out_shape=[jax.ShapeDtypeStruct((t, D_MODEL), F32), jax.ShapeDtypeStruct((t, D_MODEL), BF16),
                   jax.ShapeDtypeStruct((1, D_MODEL), F32), jax.ShapeDtypeStruct((1, D_MODEL), F32)],
        compiler_params=_cparams(("arbitrary",)),
    )(dff, dr2, r1, ln_g)


def _dcat(dr1b, w_out):
    t = dr1b.shape[0]
    tm, tn = min(1024, t), 1024
    return _matmul(
        "dcat", dr1b, w_out, NT, (t // tm, D_MODEL // tn, 1),
        pl.BlockSpec((tm, D_MODEL), lambda i, j, k: (i, 0)),
        pl.BlockSpec((tn, D_MODEL), lambda i, j, k: (j, 0)),
        jax.ShapeDtypeStruct((t, D_MODEL), F32),
        pl.BlockSpec((tm, tn), lambda i, j, k: (i, j)), (tm, tn))


def _grad_w_out(cat, dr1b, dep=None):
    t = cat.shape[0]
    tm, tn = 1024, 1024
    return _matmul(
        "grad_w_out", cat, dr1b, TN, (D_MODEL // tm, D_MODEL // tn, 1),
        pl.BlockSpec((t, tm), lambda i, j, k: (0, i)),
        pl.BlockSpec((t, tn), lambda i, j, k: (0, j)),
        jax.ShapeDtypeStruct((N_DEV, D_MODEL // N_DEV, D_MODEL), F32),
        pl.BlockSpec((tm // (D_MODEL // N_DEV), D_MODEL // N_DEV, tn), lambda i, j, k: (i, 0, j)), (tm, tn), dep=dep)


def _gla_norm_bwd(dcat, o_raw, proj, norm_w, dep=None):
    t = o_raw.shape[0]
    r = min(512, t)

    deps, dep_specs = _dep_operand(dep)

    def body(d_ref, o_ref, g_ref, w_ref, *rest):
        do_ref, dg_ref, dw_ref = rest[len(deps):]

        @pl.when(pl.program_id(0) == 0)
        def _():
            dw_ref[...] = jnp.zeros_like(dw_ref)

        w = w_ref[...]
        dw = jnp.zeros((1, GLA_DV), F32)
        for h in range(GLA_HEADS):
            sl = slice(h * GLA_DV, (h + 1) * GLA_DV)
            o = o_ref[:, sl]
            g = g_ref[:, sl]
            d = d_ref[:, sl]
            rr = lax.rsqrt(jnp.mean(o * o, axis=-1, keepdims=True) + RMS_EPS)
            on = o * rr
            sg = jax.nn.sigmoid(g)
            sil = g * sg
            dg_ref[:, sl] = (d * on * w * (sg * (1.0 + g * (1.0 - sg)))).astype(BF16)
            dw = dw + jnp.sum(d * on * sil, axis=0, keepdims=True)
            don = d * w * sil
            do_ref[:, sl] = rr * (don - on * jnp.mean(don * on, axis=-1, keepdims=True))
        dw_ref[...] += dw

    return pl.pallas_call(
        body, name="gla_norm_bwd", grid=(t // r,),
        in_specs=[pl.BlockSpec((r, D_GLA), lambda i: (i, 0)),
                  pl.BlockSpec((r, D_GLA), lambda i: (i, 0)),
                  pl.BlockSpec((r, D_GLA), lambda i: (i, C_GG // D_GLA)),
                  pl.BlockSpec((1, GLA_DV), lambda i: (0, 0)), *dep_specs],
        out_specs=[pl.BlockSpec((r, D_GLA), lambda i: (i, 0)),
                   pl.BlockSpec((r, D_GLA), lambda i: (i, 0)),
                   pl.BlockSpec((1, GLA_DV), lambda i: (0, 0))],
        out_shape=[jax.ShapeDtypeStruct((t, D_GLA), F32), jax.ShapeDtypeStruct((t, D_GLA), BF16),
                   jax.ShapeDtypeStruct((1, GLA_DV), F32)],
        compiler_params=_cparams(("arbitrary",)),
    )(dcat, o_raw, proj, norm_w, *deps)


def _gla_bwd(proj, bloc, do_raw, states):
    t = proj.shape[0]
    r = min(256, t)
    ncb = r // GC
    nb = t // r
    scale = GLA_DK ** -0.5

    def body(q_ref, k_ref, v_ref, b_ref, do_ref, st_ref, dq_ref, dk_ref, dv_ref, db_ref, ds_scr):
        @pl.when(pl.program_id(1) == 0)
        def _():
            ds_scr[...] = jnp.zeros_like(ds_scr)

        rows = lax.broadcasted_iota(jnp.int32, (GC, 1), 0)

        def chunk(cc, carry):
            c = ncb - 1 - cc
            r0 = pl.multiple_of(c * GC, GC)
            q = q_ref[pl.ds(r0, GC), :] * scale
            k = k_ref[pl.ds(r0, GC), :]
            v = v_ref[pl.ds(r0, GC), :]
            b = b_ref[pl.ds(r0, GC), :]
            do = do_ref[pl.ds(r0, GC), :]
            st = st_ref[c]
            dsn = ds_scr[...]
            bl = b[GC - 1:GC, :]
            eb = jnp.exp(b)
            ekl = jnp.exp(bl - b)
            ebl = jnp.exp(bl)
            qh = q * eb
            kd = k * ekl
            dob = do.astype(BF16)
            dsb = dsn.astype(BF16)
            dqh = _dot(dob, st.astype(BF16), NN)
            dkd = _dot(v.astype(BF16), dsb, NN)
            dv = _dot(kd.astype(BF16), dsb, NT)
            dq_i = jnp.zeros((GC, GLA_DK), F32)
            dk_i = jnp.zeros((GC, GLA_DK), F32)
            dk_x = jnp.zeros((GC, GLA_DK), F32)
            kr = _bf16_round(k)
            vr = _bf16_round(v)
            dor = _bf16_round(do)
            for i in range(GC):
                w = jnp.exp(jnp.where(rows <= i, b[i:i + 1, :] - b, NEG))
                qw = q[i:i + 1, :] * w
                qwr = _bf16_round(qw)
                a = _bf16_round(jnp.sum(qwr * kr, axis=-1, keepdims=True))
                da = jnp.sum(vr * dor[i:i + 1, :], axis=-1, keepdims=True)
                dv = dv + a * dor[i:i + 1, :]
                dk_x = dk_x + da * qw
                dk_i = dk_i + _bf16_round(da) * qwr
                dq_i = jnp.where(rows == i, jnp.sum(da * (w * k), axis=0, keepdims=True), dq_i)
            dqs = dqh * eb + dq_i
            dk = dkd * ekl + dk_i
            db_last = jnp.sum(dkd * kd, axis=0, keepdims=True) + ebl * jnp.sum(dsn * st, axis=0, keepdims=True)
            db = q * dqs - k * (dkd * ekl + dk_x) + jnp.where(rows == GC - 1, db_last, 0.0)
            dq_ref[pl.ds(r0, GC), :] = (dqs * scale).astype(BF16)
            dk_ref[pl.ds(r0, GC), :] = dk.astype(BF16)
            dv_ref[pl.ds(r0, GC), :] = dv.astype(BF16)
            db_ref[pl.ds(r0, GC), :] = db
            ds_scr[...] = dsn * ebl + _dot(dob, qh.astype(BF16), TN)
            return carry

        lax.fori_loop(0, ncb, chunk, 0)

    rev = lambda i: nb - 1 - i
    return pl.pallas_call(
        body, name="gla_bwd", grid=(GLA_HEADS, nb),
        in_specs=[pl.BlockSpec((r, GLA_DK), lambda h, i: (rev(i), C_QG // GLA_DK + h)),
                  pl.BlockSpec((r, GLA_DK), lambda h, i: (rev(i), C_KG // GLA_DK + h)),
                  pl.BlockSpec((r, GLA_DV), lambda h, i: (rev(i), C_VG // GLA_DV + h)),
                  pl.BlockSpec((r, GLA_DK), lambda h, i: (rev(i), h)),
                  pl.BlockSpec((r, GLA_DV), lambda h, i: (rev(i), h)),
                  pl.BlockSpec((ncb, None, GLA_DV, GLA_DK), lambda h, i: (rev(i), h, 0, 0))],
        out_specs=[pl.BlockSpec((r, GLA_DK), lambda h, i: (rev(i), h)),
                   pl.BlockSpec((r, GLA_DK), lambda h, i: (rev(i), h)),
                   pl.BlockSpec((r, GLA_DV), lambda h, i: (rev(i), h)),
                   pl.BlockSpec((r, GLA_DK), lambda h, i: (rev(i), h))],
        out_shape=[jax.ShapeDtypeStruct((t, GLA_KW), BF16), jax.ShapeDtypeStruct((t, GLA_KW), BF16),
                   jax.ShapeDtypeStruct((t, D_GLA), BF16), jax.ShapeDtypeStruct((t, GLA_KW), F32)],
        scratch_shapes=[pltpu.VMEM((GLA_DV, GLA_DK), F32)],
        compiler_params=_cparams(("parallel", "arbitrary")),
    )(proj, proj, proj, bloc, do_raw, states)


def _gate_bwd(db, dgdz, proj, w2p, triu):
    t = db.shape[0]
    r = triu.shape[0]

    def body(db_ref, s_ref, lo_ref, w_ref, u_ref, dlo_ref, gw_ref, gb_ref):
        @pl.when(pl.program_id(0) == 0)
        def _():
            gw_ref[...] = jnp.zeros_like(gw_ref)
            gb_ref[...] = jnp.zeros_like(gb_ref)

        dz = _dot(u_ref[...], db_ref[...], NN, precision=lax.Precision.HIGHEST) * s_ref[...]
        dzb = dz.astype(BF16)
        gb_ref[...] += jnp.sum(dz, axis=0, keepdims=True)
        gw_ref[...] += _dot(lo_ref[...].astype(BF16), dzb, TN)
        dlo_ref[...] = _dot(dzb, w_ref[...], NT).astype(BF16)

    return pl.pallas_call(
        body, name="gate_bwd", grid=(t // r,),
        in_specs=[pl.BlockSpec((r, GLA_KW), lambda i: (i, 0)),
                  pl.BlockSpec((r, GLA_KW), lambda i: (i, 0)),
                  pl.BlockSpec((r, LANES), lambda i: (i, C_LO // LANES)),
                  pl.BlockSpec((LANES, GLA_KW), lambda i: (0, 0)),
                  pl.BlockSpec((r, r), lambda i: (0, 0))],
        out_specs=[pl.BlockSpec((r, LANES), lambda i: (i, 0)),
                   pl.BlockSpec((LANES, GLA_KW), lambda i: (0, 0)),
                   pl.BlockSpec((1, GLA_KW), lambda i: (0, 0))],
        out_shape=[jax.ShapeDtypeStruct((t, LANES), BF16), jax.ShapeDtypeStruct((LANES, GLA_KW), F32),
                   jax.ShapeDtypeStruct((1, GLA_KW), F32)],
        compiler_params=_cparams(("arbitrary",)),
    )(db, dgdz, proj, w2p, triu)


def _swa_bwd(proj, dcat, sinks, dep=None):
    t = proj.shape[0]

    deps, dep_specs = _dep_operand(dep)

    def body(sink_ref, q_ref, kp_ref, kc_ref, vp_ref, vc_ref, d_ref, *rest):
        dq_ref, dk_ref, dv_ref, dsink_ref = rest[len(deps):]
        i = pl.program_id(0)

        @pl.when(i == 0)
        def _():
            dk_ref[...] = jnp.zeros_like(dk_ref)
            dv_ref[...] = jnp.zeros_like(dv_ref)
            dsink_ref[...] = jnp.zeros_like(dsink_ref)

        valid, lo = _swa_masks(i, 2 * len(SWA_STACKS[0]))
        lane = lax.broadcasted_iota(jnp.int32, (1, LANES), 1)
        kb = jnp.concatenate([kp_ref[...], kc_ref[...]], axis=0)
        vb = jnp.concatenate([vp_ref[...], vc_ref[...]], axis=0)
        dsink = jnp.zeros((1, LANES), F32)
        folded_k, folded_v = [], []
        for kv in range(2):
            kdup = _dup_half(kb, lo, kv).astype(BF16)
            vdup = _dup_half(vb, lo, kv).astype(BF16)
            dkd = jnp.zeros((2 * WINDOW, LANES), F32)
            dvd = jnp.zeros((2 * WINDOW, LANES), F32)
            for pairs in SWA_STACKS:
                qm = _swa_stack(q_ref, 0, lo, kv, pairs)
                dom = _swa_stack(d_ref, 0, lo, kv, pairs)
                pn, psink = _swa_probs(qm, kdup, valid, _swa_sinks(sink_ref, kv, pairs))
                dpr = _dot(dom, vdup, NT)
                drow = jnp.sum(dpr * pn, axis=-1, keepdims=True)
                ds_col = psink * drow
                for n, p in enumerate(pairs):
                    for e in range(2):
                        r0 = (2 * n + e) * WINDOW
                        dsink = dsink + jnp.where(lane == SWA_GROUP * kv + 2 * p + e,
                                                  -jnp.sum(ds_col[r0:r0 + WINDOW, :]), 0.0)
                dsb = (pn * (dpr - drow) * (SWA_DH ** -0.5)).astype(BF16)
                dq_all = _dot(dsb, kdup, NN)
                for n, p in enumerate(pairs):
                    c0 = LANES * (4 * kv + p)
                    dq_ref[:, c0:c0 + LANES] = _swa_unstack(dq_all, lo, n).astype(BF16)
                dkd = dkd + _dot(dsb, qm, TN)
                dvd = dvd + _dot(pn.astype(BF16), dom, TN)
            folded_k.append(dkd + pltpu.roll(dkd, SWA_DH, axis=1))
            folded_v.append(dvd + pltpu.roll(dvd, SWA_DH, axis=1))
        dkb = jnp.where(lo, folded_k[0], folded_k[1])
        dvb = jnp.where(lo, folded_v[0], folded_v[1])
        dsink_ref[...] += dsink
        cur = pl.ds(pl.multiple_of(i * WINDOW, WINDOW), WINDOW)
        dk_ref[cur, :] += dkb[WINDOW:, :]
        dv_ref[cur, :] += dvb[WINDOW:, :]

        @pl.when(i > 0)
        def _():
            prev = pl.ds(pl.multiple_of((i - 1) * WINDOW, WINDOW), WINDOW)
            dk_ref[prev, :] += dkb[:WINDOW, :]
            dv_ref[prev, :] += dvb[:WINDOW, :]

    kvspec = lambda col, prev: pl.BlockSpec(
        (WINDOW, LANES), (lambda i: (jnp.maximum(i - 1, 0), col)) if prev else (lambda i: (i, col)))
    full = pl.BlockSpec((t, LANES), lambda i: (0, 0))
    return pl.pallas_call(
        body, name="swa_bwd", grid=(t // WINDOW,),
        in_specs=[pl.BlockSpec(memory_space=pltpu.SMEM),
                  pl.BlockSpec((WINDOW, D_SWA), lambda i: (i, C_QS // D_SWA)),
                  kvspec(C_KS // LANES, True), kvspec(C_KS // LANES, False),
                  kvspec(C_VS // LANES, True), kvspec(C_VS // LANES, False),
                  pl.BlockSpec((WINDOW, D_SWA), lambda i: (i, 1)), *dep_specs],
        out_specs=[pl.BlockSpec((WINDOW, D_SWA), lambda i: (i, 0)), full, full,
                   pl.BlockSpec((1, LANES), lambda i: (0, 0))],
        out_shape=[jax.ShapeDtypeStruct((t, D_SWA), BF16), jax.ShapeDtypeStruct((t, LANES), F32),
                   jax.ShapeDtypeStruct((t, LANES), F32), jax.ShapeDtypeStruct((1, LANES), F32)],
        compiler_params=_cparams(("arbitrary",)),
    )(sinks, proj, proj, proj, proj, proj, dcat, *deps)


def _grad_w_in(xb, dproj):
    t = xb.shape[0]
    tm, tn = 896, 1024
    return _matmul(
        "grad_w_in", dproj, xb, TN, (D_INP // tm, D_MODEL // tn, 1),
        pl.BlockSpec((t, tm), lambda i, j, k: (0, i)),
        pl.BlockSpec((t, tn), lambda i, j, k: (0, j)),
        jax.ShapeDtypeStruct((D_IN, D_MODEL), F32),
        pl.BlockSpec((tm, tn), lambda i, j, k: (i, j)), (tm, tn))


def _grad_x(dproj, w_in_t, dr1, dep=None):
    t = dproj.shape[0]
    tm, tn = min(512, t), 1024

    def epilogue(acc_ref, extra, outs):
        for r0, n in _row_chunks(tm):
            rs = slice(r0, r0 + n)
            outs[0][rs, :] = ALPHA * extra[0][rs, :] + acc_ref[rs, :]

    blk = pl.BlockSpec((tm, tn), lambda j, i, k: (i, j))
    return _matmul(
        "grad_x", dproj, w_in_t, NN, (D_MODEL // tn, t // tm, 1),
        pl.BlockSpec((tm, D_INP), lambda j, i, k: (i, 0)),
        pl.BlockSpec((D_INP, tn), lambda j, i, k: (0, j)),
        jax.ShapeDtypeStruct((t, D_MODEL), F32), blk, (tm, tn),
        extra=(dr1,), extra_specs=(blk,), epilogue=epilogue, dep=dep)


def _place():
    x, y, c = lax.axis_index("x"), lax.axis_index("y"), lax.axis_index("c")
    chips = [(1 - x, y), (x, 1 - y), (1 - x, 1 - y)]
    return x, y, c, chips


def _plan_gather_out(src, land, x, y, c, chips):
    me = 4 * x + 2 * y + c
    return [(src, land.at[me], to) for to in [(x, y, 1 - c)] + [(px, py, c) for px, py in chips]]


def _plan_gather_forward(src, land, x, y, c, chips):
    return [(land.at[4 * px + 2 * py + c], land.at[4 * px + 2 * py + c], (x, y, 1 - c)) for px, py in chips]


def _plan_sibling(src, land, x, y, c, chips):
    return [(src.at[2 * q + (1 - c)], land.at[q], (x, y, 1 - c)) for q in range(4)]


def _plan_chips(src, land, x, y, c, chips):
    return [(src.at[j], land.at[j], (px, py, c)) for j, (px, py) in enumerate(chips)]


_PLAN_COPIES = {_plan_gather_out: 4, _plan_gather_forward: 3, _plan_sibling: 4, _plan_chips: 3}
_HBM = pl.BlockSpec(memory_space=pltpu.HBM)
_SEM = pl.BlockSpec(memory_space=pltpu.SEMAPHORE)
_EFFECT = pltpu.SideEffectType.DATAFLOW_SIDE_EFFECTING


def _hbm(a):
    return pltpu.with_memory_space_constraint(a, pltpu.HBM)


def _plan_descriptors(plans, srcs, lands, send, recv):
    x, y, c, chips = _place()
    cps = []
    for plan, src, land in zip(plans, srcs, lands):
        for s_ref, d_ref, to in plan(src, land, x, y, c, chips):
            k = len(cps)
            cps.append(pltpu.make_async_remote_copy(src_ref=s_ref, dst_ref=d_ref, send_sem=send.at[k],
                                                    recv_sem=recv.at[k], device_id=to, device_id_type=MESH))
    return cps


def _copies_start(name, plans, srcs, lands, after=None):
    has_src = [s is not None for s in srcs]
    arrays = [s for s in srcs if s is not None] + list(lands)
    n_src = sum(has_src)
    n_cp = sum(_PLAN_COPIES[p] for p in plans)
    afters = [] if after is None else [after]

    def body(*refs):
        ins = refs[:len(arrays)]
        send, recv = refs[len(arrays) + len(afters)], refs[len(arrays) + len(afters) + 1]
        token = refs[-1]
        it = iter(ins[:n_src])
        src_refs = [next(it) if h else None for h in has_src]
        for cp in _plan_descriptors(plans, src_refs, ins[n_src:], send, recv):
            cp.start()
        token[...] = jnp.zeros_like(token)

    outs = pl.pallas_call(
        body, name=name,
        in_specs=[_HBM] * len(arrays) + [pl.BlockSpec(memory_space=pl.ANY)] * len(afters),
        out_specs=(_SEM, _SEM, *[_HBM] * len(arrays), pl.BlockSpec(memory_space=pltpu.VMEM)),
        out_shape=(pltpu.SemaphoreType.DMA((n_cp,)), pltpu.SemaphoreType.DMA((n_cp,)),
                   *[pltpu.HBM(a.shape, a.dtype) for a in arrays], jax.ShapeDtypeStruct((8, LANES), F32)),
        input_output_aliases={i: 2 + i for i in range(len(arrays))},
        compiler_params=pltpu.CompilerParams(has_side_effects=_EFFECT),
    )(*[_hbm(a) for a in arrays], *afters)
    send, recv = outs[0], outs[1]
    thru = list(outs[2:-1])
    it = iter(thru[:n_src])
    return send, recv, [next(it) if h else None for h in has_src], thru[n_src:], outs[-1]


def _copies_wait(name, plans, started, after):
    send, recv, srcs, lands, _ = started
    has_src = [s is not None for s in srcs]
    arrays = [s for s in srcs if s is not None] + list(lands)
    n_src = sum(has_src)

    def body(*refs):
        ins = refs[:len(arrays)]
        send_ref, recv_ref = refs[len(arrays)], refs[len(arrays) + 1]
        it = iter(ins[:n_src])
        src_refs = [next(it) if h else None for h in has_src]
        for cp in _plan_descriptors(plans, src_refs, ins[n_src:], send_ref, recv_ref):
            cp.wait_send()
            cp.wait_recv()

    outs = pl.pallas_call(
        body, name=name,
        in_specs=[_HBM] * len(arrays) + [_SEM, _SEM, pl.BlockSpec(memory_space=pl.ANY)],
        out_specs=tuple([_HBM] * len(arrays)),
        out_shape=tuple(pltpu.HBM(a.shape, a.dtype) for a in arrays),
        input_output_aliases={i: i for i in range(len(arrays))},
        compiler_params=pltpu.CompilerParams(has_side_effects=_EFFECT),
    )(*arrays, send, recv, after)
    return list(outs[:n_src]), list(outs[n_src:])


def _shard_tiles(rows, cols, tr):
    if rows % tr == 0:
        return (tr, cols), rows // tr, lambda r: (r, 0)
    tc = 2 * LANES
    return (rows, tc), cols // tc, lambda r: (0, r)


def _pair_sum(name, grad, from_sibling, blocks):
    _, rows, cols = grad.shape
    (br_, bc), steps, at = _shard_tiles(rows, cols, 256)

    def body(blk_ref, g_ref, s_ref, o_ref):
        o_ref[...] = (g_ref[...] + s_ref[...]).astype(BF16)

    return pl.pallas_call(
        body, name=name,
        grid_spec=pltpu.PrefetchScalarGridSpec(
            num_scalar_prefetch=1, grid=(3, steps),
            in_specs=[pl.BlockSpec((None, br_, bc), lambda j, r, br: (br[j], *at(r))),
                      pl.BlockSpec((None, br_, bc), lambda j, r, br: (br[3 + j], *at(r)))],
            out_specs=pl.BlockSpec((None, br_, bc), lambda j, r, br: (j, *at(r)))),
        out_shape=jax.ShapeDtypeStruct((3, rows, cols), BF16),
        compiler_params=_cparams(("parallel", "parallel")),
    )(blocks, grad, from_sibling)


def _adam_math(g, w, m, v):
    m2 = ADAM_B1 * m + (1.0 - ADAM_B1) * g
    v2 = ADAM_B2 * v + (1.0 - ADAM_B2) * (g * g)
    m_hat = m2 / (1.0 - ADAM_B1 ** ADAM_STEP)
    v_hat = v2 / (1.0 - ADAM_B2 ** ADAM_STEP)
    delta = -ADAM_LR * (m_hat / (jnp.sqrt(v_hat) + ADAM_EPS) + ADAM_WD * w)
    return delta, m2, v2


def _sum_adam(name, grad, from_sibling, from_chips, own, w, m, v, dep=None):
    rows, cols = w.shape
    (br_, bc), steps, at = _shard_tiles(rows, cols, 128)

    deps, dep_specs = _dep_operand(dep)

    def body(own_ref, p_ref, s_ref, r_ref, w_ref, m_ref, v_ref, *rest):
        g_out, d_out, m_out, v_out = rest[len(deps):]
        g = p_ref[...] + s_ref[...]
        for j in range(3):
            g = g + r_ref[j].astype(F32)
        d, m2, v2 = _adam_math(g, w_ref[...], m_ref[...], v_ref[...])
        g_out[...] = g
        d_out[...] = d
        m_out[...] = m2
        v_out[...] = v2

    blk = pl.BlockSpec((br_, bc), lambda r, cr: at(r))
    shp = jax.ShapeDtypeStruct((rows, cols), F32)
    return pl.pallas_call(
        body, name=name,
        grid_spec=pltpu.PrefetchScalarGridSpec(
            num_scalar_prefetch=1, grid=(steps,),
            in_specs=[pl.BlockSpec((None, br_, bc), lambda r, cr: (cr[0], *at(r))),
                      pl.BlockSpec((None, br_, bc), lambda r, cr: (cr[1], *at(r))),
                      pl.BlockSpec((3, br_, bc), lambda r, cr: (0, *at(r))),
                      blk, blk, blk, *dep_specs],
            out_specs=[blk, blk, blk, blk]),
        out_shape=[shp, shp, shp, shp],
        compiler_params=_cparams(("parallel",)),
    )(own, grad, from_sibling, from_chips, w, m, v, *deps)


def _adam_small(name, g, w, m, v):
    def body(g_ref, w_ref, m_ref, v_ref, d_out, m_out, v_out):
        d, m2, v2 = _adam_math(g_ref[...], w_ref[...], m_ref[...], v_ref[...])
        d_out[...] = d
        m_out[...] = m2
        v_out[...] = v2

    shp = jax.ShapeDtypeStruct(w.shape, F32)
    return pl.pallas_call(body, name=name, out_shape=[shp, shp, shp])(g, w, m, v)


def _all_reduce_small(pack, dep=None):
    rows = pack.shape[0]
    deps = () if dep is None else (dep,)

    def body(in_ref, *rest):
        out_ref, slots, send, recv = rest[len(deps):]
        x, y, c, _ = _place()
        me = 4 * x + 2 * y + c
        slots[me] = in_ref[...]
        cps = []
        for k in range(1, N_DEV):
            dx, dy, dc = (k >> 2) & 1, (k >> 1) & 1, k & 1
            to = (jnp.bitwise_xor(x, dx), jnp.bitwise_xor(y, dy), jnp.bitwise_xor(c, dc))
            cps.append(pltpu.make_async_remote_copy(
                src_ref=in_ref, dst_ref=slots.at[me], send_sem=send.at[k - 1], recv_sem=recv.at[k - 1],
                device_id=to, device_id_type=MESH))
        for cp in cps:
            cp.start()
        for cp in cps:
            cp.wait()
        acc = slots[0]
        for d in range(1, N_DEV):
            acc = acc + slots[d]
        out_ref[...] = acc

    return pl.pallas_call(
        body, name="all_reduce_small",
        in_specs=[pl.BlockSpec(memory_space=pltpu.VMEM)] * (1 + len(deps)),
        out_specs=pl.BlockSpec(memory_space=pltpu.VMEM),
        out_shape=jax.ShapeDtypeStruct((rows, LANES), F32),
        scratch_shapes=[pltpu.VMEM((N_DEV, rows, LANES), F32),
                        pltpu.SemaphoreType.DMA((N_DEV - 1,)), pltpu.SemaphoreType.DMA((N_DEV - 1,))],
    )(pack, *deps)


def _rows128(a):
    flat = a.reshape(-1)
    padn = (-flat.shape[0]) % (8 * LANES)
    if padn:
        flat = jnp.concatenate([flat, jnp.zeros((padn,), flat.dtype)])
    return flat.reshape(-1, LANES)


def kernel(x, w_in, w_gk2, b_gk, gla_norm_w, swa_sinks, w_out, ln1_g, ln1_b, w_up, w_down, ln2_g, ln2_b, loss_target, m_w_in, m_w_gk2, m_b_gk, m_gla_norm_w, m_swa_sinks, m_w_out, m_ln1_g, m_ln1_b, m_w_up, m_w_down, m_ln2_g, m_ln2_b, v_w_in, v_w_gk2, v_b_gk, v_gla_norm_w, v_swa_sinks, v_w_out, v_ln1_g, v_ln1_b, v_w_up, v_w_down, v_ln2_g, v_ln2_b):
    xc, yc, cc = lax.axis_index("x"), lax.axis_index("y"), lax.axis_index("c")
    me = 4 * xc + 2 * yc + cc

    x2 = x[0]
    t = x2.shape[0]
    target = loss_target[0]

    def land_of(sh):
        return lax.dynamic_update_index_in_dim(lax.empty((N_DEV,) + sh.shape, BF16), sh, me, 0)

    out_plans = [_plan_gather_out] * 3
    fwd_plans = [_plan_gather_forward] * 3
    in_shards = [w_in[0].T.astype(BF16), w_gk2[0].astype(BF16)]
    ag0 = _copies_start("gather_in_start", out_plans[:2], in_shards, [land_of(sh) for sh in in_shards])
    zero = ag0[4][0, 0]
    xb = (x2 + zero).astype(BF16)
    shards = [(w[0] + zero).astype(BF16) for w in (w_out, w_up, w_down)]
    lands = [land_of(sh) for sh in shards]
    _, in_lands = _copies_wait("gather_in_wait", out_plans[:2], ag0, lands[2])
    ag0f = _copies_start("gather_in_forward_start", fwd_plans[:2], [None] * 2, in_lands)
    _, first = _copies_wait("gather_in_forward_wait", fwd_plans[:2], ag0f, ag0f[4])
    w_in_rows = first[0].reshape(D_IN, D_MODEL)
    w_in_t = _to_padded_rows(w_in_rows)
    w_in_o = jnp.concatenate([w_in_rows, jnp.zeros((D_INP - D_IN, D_MODEL), BF16)], axis=0)
    w2 = first[1].transpose(1, 0, 2).reshape(GATE_RANK, GLA_KW)
    w2p = jnp.concatenate([w2, jnp.zeros((LANES - GATE_RANK, GLA_KW), BF16)], axis=0)
    ag1 = _copies_start("gather_out_start", out_plans, shards, lands, first[0])

    proj = _proj(xb, w_in_t, dep=ag1[4])
    rt = min(256, t)
    ii = jnp.arange(rt)
    tri = ((ii[:, None] // GC == ii[None, :] // GC) & (ii[None, :] <= ii[:, None])).astype(F32)
    bloc, dgdz = _gate_fwd(proj, w2p, b_gk, tri)
    o_raw, states = _gla_fwd(proj, bloc)
    gla_out = _gla_norm_fwd(o_raw, proj, gla_norm_w)
    cat = _swa_fwd(proj, swa_sinks[0], gla_out)
    _, lands = _copies_wait("gather_out_wait", out_plans, ag1, cat)
    ag2 = _copies_start("gather_forward_out_start", fwd_plans[:1], [None], lands[:1])
    _, (w_out_g,) = _copies_wait("gather_forward_out_wait", fwd_plans[:1], ag2, ag2[4])
    ag3 = _copies_start("gather_forward_mlp_start", fwd_plans[:2], [None] * 2, lands[1:], w_out_g)
    w_out_f = w_out_g.reshape(D_MODEL, D_MODEL)
    r1, h1, h1b = _mix_ln1(cat, w_out_f, x2, ln1_g, ln1_b, dep=ag3[4])
    _, (w_up_f, w_down_g) = _copies_wait("gather_forward_mlp_wait", fwd_plans[:2], ag3, h1b)
    w_down_f = w_down_g.reshape(D_FF, D_MODEL)
    a_act, hdn = _mlp_up(h1b, w_up_f)
    dr2, dr2b, g_ln2_g, g_ln2_b, loss_part = _mlp_down_loss(hdn, w_down_f, h1, target, ln2_g, ln2_b)

    others = [2 * (1 - xc) + yc, 2 * xc + (1 - yc), 2 * (1 - xc) + (1 - yc)]
    blocks = jnp.stack([2 * q + cc for q in others] + others).astype(jnp.int32)
    own = jnp.stack([me, 2 * xc + yc]).astype(jnp.int32)
    wmv = dict(w_in=[a[0].T for a in (w_in, m_w_in, v_w_in)], w_out=[a[0] for a in (w_out, m_w_out, v_w_out)],
               w_up=[a[0] for a in (w_up, m_w_up, v_w_up)], w_down=[a[0] for a in (w_down, m_w_down, v_w_down)])
    big = {}

    def sib_land(g):
        return lax.empty((4,) + g.shape[1:], F32)

    def chip_land(g):
        return lax.empty((3,) + g.shape[1:], BF16)

    def finish(nm, g, from_sib, from_chips, dep=None):
        outs = _sum_adam("sum_adam_" + nm, g, from_sib, from_chips, own, *wmv[nm], dep=dep)
        big[nm] = [(o.T if nm == "w_in" else o)[None] for o in outs]

    du = _mlp_down_bwd(dr2b, w_down_f, a_act)
    g_down = _grad_w_down(hdn, dr2b)
    sa_down = _copies_start("sibling_down_start", [_plan_sibling], [g_down], [sib_land(g_down)])
    g_up = _grad_w_up(h1b, du, dep=sa_down[4])
    (g_down,), (fs_down,) = _copies_wait("sibling_down_wait", [_plan_sibling], sa_down, g_up)
    p_down = _pair_sum("pair_sum_w_down", g_down, fs_down, blocks)
    sb_down = _copies_start("chips_down_sibling_up_start", [_plan_chips, _plan_sibling], [p_down, g_up],
                            [chip_land(g_down), sib_land(g_up)])
    dr1, dr1b, g_ln1_g, g_ln1_b = _mlp_up_bwd_ln1(du, w_up_f, dr2, r1, ln1_g, dep=sb_down[4])
    dcat = _dcat(dr1b, w_out_f)
    (_, g_up), (fc_down, fs_up) = _copies_wait("chips_down_sibling_up_wait", [_plan_chips, _plan_sibling], sb_down,
                                               dcat)
    p_up = _pair_sum("pair_sum_w_up", g_up, fs_up, blocks)
    sb_up = _copies_start("chips_up_start", [_plan_chips], [p_up], [chip_land(g_up)])
    do_raw, dg_g, g_norm_w = _gla_norm_bwd(dcat, o_raw, proj, gla_norm_w, dep=sb_up[4])
    dq_g, dk_g, dv_g, dgk = _gla_bwd(proj, bloc, do_raw, states)
    _, (fc_up,) = _copies_wait("chips_up_wait", [_plan_chips], sb_up, dgk)
    dlo, gw2_p, g_b_gk = _gate_bwd(dgk, dgdz, proj, w2p, tri.T)
    dq_s, dk_s, dv_s, g_sinks = _swa_bwd(proj, dcat, swa_sinks[0])
    dproj = jnp.concatenate([dq_g, dk_g, dv_g, dg_g, dlo[:, :GATE_RANK], dq_s, dk_s.astype(BF16), dv_s.astype(BF16),
                             jnp.zeros((t, D_INP - D_IN), BF16)], axis=-1)
    gw_in_t = _grad_w_in(xb, dproj)
    g_in = gw_in_t.reshape(N_DEV, D_IN // N_DEV, D_MODEL)
    sa_in = _copies_start("sibling_in_start", [_plan_sibling], [g_in], [sib_land(g_in)])
    g_out = _grad_w_out(cat, dr1b, dep=sa_in[4])
    (g_in,), (fs_in,) = _copies_wait("sibling_in_wait", [_plan_sibling], sa_in, g_out)
    p_in = _pair_sum("pair_sum_w_in", g_in, fs_in, blocks)
    sb_in = _copies_start("chips_in_sibling_out_start", [_plan_chips, _plan_sibling], [p_in, g_out],
                          [chip_land(g_in), sib_land(g_out)])
    grad_x = _grad_x(dproj, w_in_o, dr1, dep=sb_in[4])
    (_, g_out), (fc_in, fs_out) = _copies_wait("chips_in_sibling_out_wait", [_plan_chips, _plan_sibling], sb_in,
                                               grad_x)
    p_out = _pair_sum("pair_sum_w_out", g_out, fs_out, blocks)
    sb_out = _copies_start("chips_out_start", [_plan_chips], [p_out], [chip_land(g_out)])
    def done(nm):
        return big[nm][0][0, :8, :LANES]

    finish("w_in", g_in, fs_in, fc_in, dep=sb_out[4])
    finish("w_down", g_down, fs_down, fc_down, dep=done("w_in"))
    finish("w_up", g_up, fs_up, fc_up, dep=done("w_down"))

    pieces = [loss_part, g_b_gk, g_norm_w, g_sinks[:, :SWA_HEADS], g_ln1_g, g_ln1_b, g_ln2_g, g_ln2_b,
              gw2_p[:GATE_RANK]]
    pack = jnp.concatenate([_rows128(p) for p in pieces], axis=0)
    tot = _all_reduce_small(pack, dep=done("w_up"))
    sizes = [p.size for p in pieces]
    offs = [0]
    for p in pieces:
        offs.append(offs[-1] + _rows128(p).shape[0])
    unpack = lambda i, shape: tot[offs[i]:offs[i + 1]].reshape(-1)[:sizes[i]].reshape(shape)
    loss = tot[0, 0]
    small_names = ["b_gk", "gla_norm_w", "swa_sinks", "ln1_g", "ln1_b", "ln2_g", "ln2_b"]
    small_w = dict(b_gk=(b_gk, m_b_gk, v_b_gk), gla_norm_w=(gla_norm_w, m_gla_norm_w, v_gla_norm_w),
                   swa_sinks=(swa_sinks, m_swa_sinks, v_swa_sinks), ln1_g=(ln1_g, m_ln1_g, v_ln1_g),
                   ln1_b=(ln1_b, m_ln1_b, v_ln1_b), ln2_g=(ln2_g, m_ln2_g, v_ln2_g),
                   ln2_b=(ln2_b, m_ln2_b, v_ln2_b))
    small_g = {nm: unpack(1 + i, small_w[nm][0].shape) for i, nm in enumerate(small_names)}
    g_pack = jnp.concatenate([_rows128(small_g[nm]) for nm in small_names], axis=0)
    small_wmv = [jnp.concatenate([_rows128(small_w[nm][k]) for nm in small_names], axis=0) for k in range(3)]
    small_out = _adam_small("adam_replicated", g_pack, *small_wmv)
    srow = [0]
    for nm in small_names:
        srow.append(srow[-1] + _rows128(small_w[nm][0]).shape[0])
    small = {}
    for i, nm in enumerate(small_names):
        shape = small_w[nm][0].shape
        n = small_w[nm][0].size
        small[nm] = [small_g[nm]] + [o[srow[i]:srow[i + 1]].reshape(-1)[:n].reshape(shape) for o in small_out]

    gw2_full = unpack(8, (GATE_RANK, GLA_KW))
    gw2_loc = lax.dynamic_slice_in_dim(gw2_full, me * (GLA_KW // N_DEV), GLA_KW // N_DEV, axis=1)
    gk2_out = _adam_small("adam_w_gk2", gw2_loc, w_gk2[0], m_w_gk2[0], v_w_gk2[0])
    big["w_gk2"] = [gw2_loc[None]] + [o[None] for o in gk2_out]

    _, (fc_out,) = _copies_wait("chips_out_wait", [_plan_chips], sb_out, tot)
    finish("w_out", g_out, fs_out, fc_out)

    order = ["w_in", "w_gk2", "b_gk", "gla_norm_w", "swa_sinks", "w_out", "ln1_g", "ln1_b", "w_up", "w_down",
             "ln2_g", "ln2_b"]
    res = {**big, **small}
    outs = [loss, grad_x[None]]
    for k in range(4):
        outs += [res[nm][k] for nm in order]
    return tuple(outs)
```

```python
import jax
import jax.numpy as jnp
from jax import lax
from jax.experimental import pallas as pl
from jax.experimental.pallas import tpu as pltpu

F32 = jnp.float32
BF16 = jnp.bfloat16

N_DEV = 8
D_MODEL = 2048
D_FF = 8192
GLA_HEADS = 4
GLA_DK = 128
GLA_DV = 256
GLA_KW = 512
D_GLA = 1024
GATE_RANK = 16
GATE_TAU = 16.0
SWA_HEADS = 16
SWA_DH = 64
SWA_GROUP = 8
WINDOW = 128
D_SWA = 1024
D_IN = 4368
ALPHA = 2.0 ** 0.25
LN_EPS = 1e-5
RMS_EPS = 1e-5
ADAM_LR = 0.001
ADAM_B1 = 0.9
ADAM_B2 = 0.999
ADAM_EPS = 1e-08
ADAM_WD = 0.01
ADAM_STEP = 10

C_QG, C_KG, C_VG, C_GG, C_QS, C_KS, C_VS, C_LO = 0, 512, 1024, 2048, 3072, 4096, 4224, 4352
D_INP = 4480
LANES = 128
GC = 16
NEG = -1e30

NN = ((1,), (0,))
NT = ((1,), (1,))
TN = ((0,), (0,))

VMEM_LIMIT = 52 * 1024 * 1024
MESH = pl.DeviceIdType.MESH


def _dot(a, b, dn, precision=None):
    return lax.dot_general(a, b, (dn, ((), ())), preferred_element_type=F32, precision=precision)


def _bf16_round(v):
    return v.astype(BF16).astype(F32)


def _cparams(dims):
    return pltpu.CompilerParams(dimension_semantics=dims, vmem_limit_bytes=VMEM_LIMIT)


def _dep_operand(dep):
    if dep is None:
        return (), ()
    return (dep,), (pl.BlockSpec(dep.shape, lambda *_: (0,) * dep.ndim),)


def _to_padded_rows(w):
    pad = jnp.zeros((D_INP - D_IN,) + w.shape[1:], w.dtype)
    return jnp.concatenate([w[:3072], w[3088:], w[3072:3088], pad], axis=0)


def _matmul(name, a, b, dn, grid, a_spec, b_spec, out_shape, out_specs, acc_shape, *,
            extra=(), extra_specs=(), epilogue=None, dims=("parallel", "parallel", "arbitrary"), dep=None):
    nk = grid[2]
    n_extra = len(extra)
    deps, dep_specs = _dep_operand(dep)
    direct = epilogue is None and (nk == 1 or (not isinstance(out_shape, (list, tuple)) and out_shape.dtype == F32))

    scratch = [] if direct or nk == 1 else [pltpu.VMEM(acc_shape, F32)]

    def body(a_ref, b_ref, *rest):
        extra_refs = rest[:n_extra]
        out_refs = rest[n_extra + len(deps):len(rest) - len(scratch)]
        acc_ref = rest[-1] if scratch else out_refs[0]
        part = _dot(a_ref[...].astype(BF16), b_ref[...].astype(BF16), dn)

        def finish():
            if epilogue is None:
                out_refs[0][...] = acc_ref[...].astype(out_refs[0].dtype)
            else:
                epilogue(acc_ref, extra_refs, out_refs)

        if direct and nk == 1:
            out_refs[0][...] = part.reshape(out_refs[0].shape).astype(out_refs[0].dtype)
        elif nk == 1:
            epilogue(part, extra_refs, out_refs)
        else:
            k = pl.program_id(2)

            @pl.when(k == 0)
            def _():
                acc_ref[...] = part

            @pl.when(k > 0)
            def _():
                acc_ref[...] += part

            if not direct:
                @pl.when(k == nk - 1)
                def _():
                    finish()

    return pl.pallas_call(
        body, name=name, grid=grid,
        in_specs=[a_spec, b_spec, *extra_specs, *dep_specs],
        out_specs=out_specs, out_shape=out_shape,
        scratch_shapes=scratch,
        compiler_params=_cparams(dims),
    )(a, b, *extra, *deps)


def _row_chunks(rows, step=128):
    step = min(step, rows)
    return [(r, step) for r in range(0, rows, step)]


def _ln_stats(r):
    mu = jnp.mean(r, axis=-1, keepdims=True)
    xc = r - mu
    var = jnp.mean(xc * xc, axis=-1, keepdims=True)
    rstd = lax.rsqrt(var + LN_EPS)
    return xc * rstd, rstd


def _ln_bwd(dy_g, xhat, rstd):
    m1 = jnp.mean(dy_g, axis=-1, keepdims=True)
    m2 = jnp.mean(dy_g * xhat, axis=-1, keepdims=True)
    return rstd * (dy_g - m1 - xhat * m2)


def _proj(xb, w_in_t, dep=None):
    t = xb.shape[0]
    tm, tn = min(2048, t), 896
    return _matmul(
        "proj", xb, w_in_t, NT, (t // tm, D_INP // tn, 1),
        pl.BlockSpec((tm, D_MODEL), lambda i, j, k: (i, 0)),
        pl.BlockSpec((tn, D_MODEL), lambda i, j, k: (j, 0)),
        jax.ShapeDtypeStruct((t, D_INP), F32),
        pl.BlockSpec((tm, tn), lambda i, j, k: (i, j)),
        (tm, tn), dep=dep)


def _gate_fwd(proj, w2p, b_gk, tri):
    t = proj.shape[0]
    r = tri.shape[0]

    def body(lo_ref, w_ref, b_ref, tri_ref, bloc_ref, dgdz_ref):
        z = _dot(lo_ref[...].astype(BF16), w_ref[...], NN) + b_ref[...]
        e = jnp.exp(-jnp.abs(z))
        gk = (jnp.minimum(z, 0.0) - jnp.log1p(e)) * (1.0 / GATE_TAU)
        inv = 1.0 / (1.0 + e)
        dgdz_ref[...] = jnp.where(z >= 0.0, e * inv, inv) * (1.0 / GATE_TAU)
        bloc_ref[...] = _dot(tri_ref[...], gk, NN, precision=lax.Precision.HIGHEST)

    return pl.pallas_call(
        body, name="gate_fwd", grid=(t // r,),
        in_specs=[pl.BlockSpec((r, LANES), lambda i: (i, C_LO // LANES)),
                  pl.BlockSpec((LANES, GLA_KW), lambda i: (0, 0)),
                  pl.BlockSpec((1, GLA_KW), lambda i: (0, 0)),
                  pl.BlockSpec((r, r), lambda i: (0, 0))],
        out_specs=[pl.BlockSpec((r, GLA_KW), lambda i: (i, 0)),
                   pl.BlockSpec((r, GLA_KW), lambda i: (i, 0))],
        out_shape=[jax.ShapeDtypeStruct((t, GLA_KW), F32), jax.ShapeDtypeStruct((t, GLA_KW), F32)],
        compiler_params=_cparams(("parallel",)),
    )(proj, w2p, b_gk, tri)


def _gla_fwd(proj, bloc):
    t = proj.shape[0]
    r = min(256, t)
    ncb = r // GC
    scale = GLA_DK ** -0.5

    def body(q_ref, k_ref, v_ref, b_ref, o_ref, st_ref, s_scr, m_scr):
        @pl.when(pl.program_id(1) == 0)
        def _():
            s_scr[...] = jnp.zeros_like(s_scr)

        rows = lax.broadcasted_iota(jnp.int32, (GC, 1), 0)
        cols = lax.broadcasted_iota(jnp.int32, (1, GC), 1)

        def increment(c, carry):
            r0 = pl.multiple_of(c * GC, GC)
            k = k_ref[pl.ds(r0, GC), :]
            b = b_ref[pl.ds(r0, GC), :]
            kd = k * jnp.exp(b[GC - 1:GC, :] - b)
            m_scr[c] = _dot(v_ref[pl.ds(r0, GC), :].astype(BF16), kd.astype(BF16), TN)
            return carry

        lax.fori_loop(0, ncb, increment, 0, unroll=8)

        def recur(c, st):
            st_ref[c] = st
            r0 = pl.multiple_of(c * GC, GC)
            bl = b_ref[pl.ds(r0, GC), :][GC - 1:GC, :]
            return st * jnp.exp(bl) + m_scr[c]

        s_scr[...] = lax.fori_loop(0, ncb, recur, s_scr[...])

        def output(c, carry):
            r0 = pl.multiple_of(c * GC, GC)
            q = q_ref[pl.ds(r0, GC), :] * scale
            b = b_ref[pl.ds(r0, GC), :]
            kr = _bf16_round(k_ref[pl.ds(r0, GC), :])
            att = jnp.zeros((GC, GC), F32)
            for j in range(GC):
                w = jnp.exp(jnp.where(rows >= j, b - b[j:j + 1, :], NEG))
                a = jnp.sum(_bf16_round(q * w) * kr[j:j + 1, :], axis=-1, keepdims=True)
                att = jnp.where(cols == j, a, att)
            o_ref[pl.ds(r0, GC), :] = (_dot((q * jnp.exp(b)).astype(BF16), st_ref[c].astype(BF16), NT)
                                       + _dot(att.astype(BF16), v_ref[pl.ds(r0, GC), :].astype(BF16), NN))
            return carry

        lax.fori_loop(0, ncb, output, 0, unroll=16)

    return pl.pallas_call(
        body, name="gla_fwd", grid=(GLA_HEADS, t // r),
        in_specs=[pl.BlockSpec((r, GLA_DK), lambda h, i: (i, C_QG // GLA_DK + h)),
                  pl.BlockSpec((r, GLA_DK), lambda h, i: (i, C_KG // GLA_DK + h)),
                  pl.BlockSpec((r, GLA_DV), lambda h, i: (i, C_VG // GLA_DV + h)),
                  pl.BlockSpec((r, GLA_DK), lambda h, i: (i, h))],
        out_specs=[pl.BlockSpec((r, GLA_DV), lambda h, i: (i, h)),
                   pl.BlockSpec((ncb, None, GLA_DV, GLA_DK), lambda h, i: (i, h, 0, 0))],
        out_shape=[jax.ShapeDtypeStruct((t, D_GLA), F32),
                   jax.ShapeDtypeStruct((t // GC, GLA_HEADS, GLA_DV, GLA_DK), F32)],
        scratch_shapes=[pltpu.VMEM((GLA_DV, GLA_DK), F32), pltpu.VMEM((ncb, GLA_DV, GLA_DK), F32)],
        compiler_params=_cparams(("parallel", "arbitrary")),
    )(proj, proj, proj, bloc)


def _gla_norm_fwd(o_raw, proj, norm_w):
    t = o_raw.shape[0]
    r = min(512, t)

    def body(o_ref, g_ref, w_ref, out_ref):
        w = w_ref[...]
        for h in range(GLA_HEADS):
            sl = slice(h * GLA_DV, (h + 1) * GLA_DV)
            o = o_ref[:, sl]
            g = g_ref[:, sl]
            on = o * lax.rsqrt(jnp.mean(o * o, axis=-1, keepdims=True) + RMS_EPS)
            out_ref[:, sl] = (on * w * (g * jax.nn.sigmoid(g))).astype(BF16)

    return pl.pallas_call(
        body, name="gla_norm_fwd", grid=(t // r,),
        in_specs=[pl.BlockSpec((r, D_GLA), lambda i: (i, 0)),
                  pl.BlockSpec((r, D_GLA), lambda i: (i, C_GG // D_GLA)),
                  pl.BlockSpec((1, GLA_DV), lambda i: (0, 0))],
        out_specs=pl.BlockSpec((r, D_GLA), lambda i: (i, 0)),
        out_shape=jax.ShapeDtypeStruct((t, D_GLA), BF16),
        compiler_params=_cparams(("parallel",)),
    )(o_raw, proj, norm_w)


def _swa_masks(i, heads):
    qi = lax.broadcasted_iota(jnp.int32, (heads * WINDOW, 1), 0) & (WINDOW - 1)
    kj = lax.broadcasted_iota(jnp.int32, (1, 2 * WINDOW), 1)
    valid = (kj > qi) & (kj <= qi + WINDOW) & ((i > 0) | (kj >= WINDOW))
    lo = lax.broadcasted_iota(jnp.int32, (1, LANES), 1) < SWA_DH
    return valid, lo


def _dup_half(x, lo, kv):
    xr = pltpu.roll(x, SWA_DH, axis=1)
    return jnp.where(lo, x, xr) if kv == 0 else jnp.where(lo, xr, x)


def _swa_stack(ref, col0, lo, kv):
    parts = []
    for p in range(4):
        c0 = col0 + LANES * (4 * kv + p)
        xp = ref[:, c0:c0 + LANES]
        parts += [jnp.where(lo, xp, 0.0).astype(BF16), jnp.where(lo, 0.0, xp).astype(BF16)]
    return jnp.concatenate(parts, axis=0)


def _swa_unstack(x_all, lo, p):
    r0 = 2 * p * WINDOW
    return jnp.where(lo, x_all[r0:r0 + WINDOW, :], x_all[r0 + WINDOW:r0 + 2 * WINDOW, :])


def _swa_sinks(sink_ref, kv):
    return jnp.concatenate([jnp.full((WINDOW, 1), sink_ref[SWA_GROUP * kv + h], F32) for h in range(SWA_GROUP)],
                           axis=0)


def _swa_probs(qm, kdup, valid, sink):
    s = _dot(qm, kdup, NT) * (SWA_DH ** -0.5)
    s = jnp.where(valid, s, NEG)
    m = jnp.maximum(jnp.max(s, axis=-1, keepdims=True), sink)
    p = jnp.exp(s - m)
    es = jnp.exp(sink - m)
    inv = 1.0 / (jnp.sum(p, axis=-1, keepdims=True) + es)
    return p * inv, es * inv


def _swa_fwd(proj, sinks, gla_out, dep=None):
    t = proj.shape[0]

    deps, dep_specs = _dep_operand(dep)

    def body(sink_ref, q_ref, kp_ref, kc_ref, vp_ref, vc_ref, gla_ref, *rest):
        o_ref = rest[-1]
        i = pl.program_id(0)
        o_ref[:, :D_GLA] = gla_ref[...]
        valid, lo = _swa_masks(i, 1)
        kb = jnp.concatenate([kp_ref[...], kc_ref[...]], axis=0)
        vb = jnp.concatenate([vp_ref[...], vc_ref[...]], axis=0)
        for kv in range(2):
            kdup = _dup_half(kb, lo, kv).astype(BF16)
            vdup = _dup_half(vb, lo, kv).astype(BF16)
            for p in range(4):
                c0 = LANES * (4 * kv + p)
                qp = q_ref[:, c0:c0 + LANES]
                halves = []
                for e in range(2):
                    qm = jnp.where(lo if e == 0 else ~lo, qp, 0.0).astype(BF16)
                    pn, _ = _swa_probs(qm, kdup, valid, sink_ref[SWA_GROUP * kv + 2 * p + e])
                    halves.append(_dot(pn.astype(BF16), vdup, NN))
                o_ref[:, D_GLA + c0:D_GLA + c0 + LANES] = jnp.where(lo, halves[0], halves[1]).astype(BF16)

    kvspec = lambda col, prev: pl.BlockSpec(
        (WINDOW, LANES), (lambda i: (jnp.maximum(i - 1, 0), col)) if prev else (lambda i: (i, col)))
    return pl.pallas_call(
        body, name="swa_fwd", grid=(t // WINDOW,),
        in_specs=[pl.BlockSpec(memory_space=pltpu.SMEM),
                  pl.BlockSpec((WINDOW, D_SWA), lambda i: (i, C_QS // D_SWA)),
                  kvspec(C_KS // LANES, True), kvspec(C_KS // LANES, False),
                  kvspec(C_VS // LANES, True), kvspec(C_VS // LANES, False),
                  pl.BlockSpec((WINDOW, D_GLA), lambda i: (i, 0)), *dep_specs],
        out_specs=pl.BlockSpec((WINDOW, D_MODEL), lambda i: (i, 0)),
        out_shape=jax.ShapeDtypeStruct((t, D_MODEL), BF16),
        compiler_params=_cparams(("parallel",)),
    )(sinks, proj, proj, proj, proj, proj, gla_out, *deps)


def _mix_ln1(cat, w_out, x, ln_g, ln_b, dep=None):
    t = cat.shape[0]
    tm, tk = min(256, t), D_MODEL

    def epilogue(acc_ref, extra, outs):
        x_ref, g_ref, b_ref = extra
        r1_ref, h1_ref, h1b_ref = outs
        for r0, n in _row_chunks(tm):
            rs = slice(r0, r0 + n)
            r1 = ALPHA * x_ref[rs, :] + acc_ref[rs, :]
            xhat, _ = _ln_stats(r1)
            h = xhat * g_ref[...] + b_ref[...]
            r1_ref[rs, :] = r1
            h1_ref[rs, :] = h
            h1b_ref[rs, :] = h.astype(BF16)

    row = pl.BlockSpec((tm, D_MODEL), lambda i, j, k: (i, 0))
    vec = pl.BlockSpec((1, D_MODEL), lambda i, j, k: (0, 0))
    return _matmul(
        "mix_ln1", cat, w_out, NN, (t // tm, 1, D_MODEL // tk),
        pl.BlockSpec((tm, tk), lambda i, j, k: (i, k)),
        pl.BlockSpec((tk, D_MODEL), lambda i, j, k: (k, 0)),
        [jax.ShapeDtypeStruct((t, D_MODEL), F32), jax.ShapeDtypeStruct((t, D_MODEL), F32),
         jax.ShapeDtypeStruct((t, D_MODEL), BF16)],
        [row, row, row], (tm, D_MODEL),
        extra=(x, ln_g, ln_b), extra_specs=(row, vec, vec), epilogue=epilogue, dep=dep)


def _mlp_up(h1b, w_up):
    t = h1b.shape[0]
    tm, tn = min(1024, t), 1024

    def epilogue(acc_ref, extra, outs):
        a_ref, hdn_ref = outs
        for r0, n in _row_chunks(tm, 256):
            rs = slice(r0, r0 + n)
            a = jnp.maximum(acc_ref[rs, :], 0.0)
            a_ref[rs, :] = a.astype(BF16)
            hdn_ref[rs, :] = (a * a).astype(BF16)

    out = pl.BlockSpec((tm, tn), lambda i, j, k: (i, j))
    return _matmul(
        "mlp_up", h1b, w_up, NN, (t // tm, D_FF // tn, 1),
        pl.BlockSpec((tm, D_MODEL), lambda i, j, k: (i, 0)),
        pl.BlockSpec((None, D_MODEL, tn), lambda i, j, k: (j, 0, 0)),
        [jax.ShapeDtypeStruct((t, D_FF), BF16), jax.ShapeDtypeStruct((t, D_FF), BF16)],
        [out, out], (tm, tn), epilogue=epilogue)


def _mlp_down_loss(hdn, w_down, h1, target, ln_g, ln_b):
    t = hdn.shape[0]
    tm, tn, tk = min(1024, t), 1024, 4096
    ff = _matmul(
        "mlp_down", hdn, w_down, NN, (t // tm, D_MODEL // tn, D_FF // tk),
        pl.BlockSpec((tm, tk), lambda i, j, k: (i, k)),
        pl.BlockSpec((tk, tn), lambda i, j, k: (k, j)),
        jax.ShapeDtypeStruct((t, D_MODEL), F32),
        pl.BlockSpec((tm, tn), lambda i, j, k: (i, j)), (tm, tn))
    r = min(256, t)

    def body(ff_ref, h1_ref, t_ref, g_ref, b_ref, dr2_ref, dr2b_ref, gg_ref, gb_ref, loss_ref):
        @pl.when(pl.program_id(0) == 0)
        def _():
            gg_ref[...] = jnp.zeros_like(gg_ref)
            gb_ref[...] = jnp.zeros_like(gb_ref)
            loss_ref[...] = jnp.zeros_like(loss_ref)

        for r0, n in _row_chunks(r, 64):
            rs = slice(r0, r0 + n)
            xhat, rstd = _ln_stats(ALPHA * h1_ref[rs, :] + ff_ref[rs, :])
            err = xhat * g_ref[...] + b_ref[...] - t_ref[rs, :]
            loss_ref[...] += 0.5 * jnp.sum(jnp.mean(err * err, axis=-1, keepdims=True))
            dy = err * (1.0 / D_MODEL)
            gg_ref[...] += jnp.sum(dy * xhat, axis=0, keepdims=True)
            gb_ref[...] += jnp.sum(dy, axis=0, keepdims=True)
            dr2 = _ln_bwd(dy * g_ref[...], xhat, rstd)
            dr2_ref[rs, :] = dr2
            dr2b_ref[rs, :] = dr2.astype(BF16)

    row = pl.BlockSpec((r, D_MODEL), lambda i: (i, 0))
    vec = pl.BlockSpec((1, D_MODEL), lambda i: (0, 0))
    return pl.pallas_call(
        body, name="ln2_loss", grid=(t // r,),
        in_specs=[row, row, row, vec, vec],
        out_specs=[row, row, vec, vec, pl.BlockSpec((1, LANES), lambda i: (0, 0))],
        out_shape=[jax.ShapeDtypeStruct((t, D_MODEL), F32), jax.ShapeDtypeStruct((t, D_MODEL), BF16),
                   jax.ShapeDtypeStruct((1, D_MODEL), F32), jax.ShapeDtypeStruct((1, D_MODEL), F32),
                   jax.ShapeDtypeStruct((1, LANES), F32)],
        compiler_params=_cparams(("arbitrary",)),
    )(ff, h1, target, ln_g, ln_b)


def _mlp_down_bwd(dr2b, w_down, a_act):
    t = dr2b.shape[0]
    tm, tn = min(1024, t), 1024

    def epilogue(acc_ref, extra, outs):
        (a_ref,) = extra
        for r0, n in _row_chunks(tm, 256):
            rs = slice(r0, r0 + n)
            outs[0][rs, :] = (acc_ref[rs, :] * (2.0 * a_ref[rs, :].astype(F32))).astype(BF16)

    blk = pl.BlockSpec((tm, tn), lambda i, j, k: (i, j))
    return _matmul(
        "mlp_down_bwd", dr2b, w_down, NT, (t // tm, D_FF // tn, 1),
        pl.BlockSpec((tm, D_MODEL), lambda i, j, k: (i, 0)),
        pl.BlockSpec((tn, D_MODEL), lambda i, j, k: (j, 0)),
        jax.ShapeDtypeStruct((t, D_FF), BF16), blk, (tm, tn),
        extra=(a_act,), extra_specs=(blk,), epilogue=epilogue)


def _grad_w_down(hdn, dr2b):
    t = hdn.shape[0]
    tm, tn = 1024, 1024
    return _matmul(
        "grad_w_down", hdn, dr2b, TN, (D_MODEL // tn, D_FF // tm, 1),
        pl.BlockSpec((t, tm), lambda j, i, k: (0, i)),
        pl.BlockSpec((t, tn), lambda j, i, k: (0, j)),
        jax.ShapeDtypeStruct((N_DEV, D_FF // N_DEV, D_MODEL), F32),
        pl.BlockSpec((None, tm, tn), lambda j, i, k: (i, 0, j)), (tm, tn))


def _grad_w_up(h1b, du, dep=None):
    t = h1b.shape[0]
    tm, tn = 1024, 1024
    return _matmul(
        "grad_w_up", h1b, du, TN, (N_DEV, D_MODEL // tm, 1),
        pl.BlockSpec((t, tm), lambda i, j, k: (0, j)),
        pl.BlockSpec((t, tn), lambda i, j, k: (0, i)),
        jax.ShapeDtypeStruct((N_DEV, D_MODEL, D_FF // N_DEV), F32),
        pl.BlockSpec((None, tm, tn), lambda i, j, k: (i, j, 0)), (tm, tn), dep=dep)


def _mlp_up_bwd_ln1(du, w_up, dr2, r1, ln_g, dep=None):
    t = du.shape[0]
    tm, tn, tk, nb = min(1024, t), 1024, D_FF // N_DEV, 4
    deps, dep_specs = _dep_operand(dep)

    def mm_body(a_ref, b_ref, *rest):
        o_ref = rest[-1]
        k = pl.program_id(2)
        part = _dot(a_ref[:, :tk], b_ref[0], NT)
        for d in range(1, nb):
            part = part + _dot(a_ref[:, d * tk:(d + 1) * tk], b_ref[d], NT)

        @pl.when(k == 0)
        def _():
            o_ref[...] = part

        @pl.when(k > 0)
        def _():
            o_ref[...] += part

    dff = pl.pallas_call(
        mm_body, name="mlp_up_bwd", grid=(t // tm, D_MODEL // tn, N_DEV // nb),
        in_specs=[pl.BlockSpec((tm, nb * tk), lambda i, j, k: (i, k)),
                  pl.BlockSpec((nb, tn, tk), lambda i, j, k: (k, j, 0)), *dep_specs],
        out_specs=pl.BlockSpec((tm, tn), lambda i, j, k: (i, j)),
        out_shape=jax.ShapeDtypeStruct((t, D_MODEL), F32),
        compiler_params=_cparams(("parallel", "parallel", "arbitrary")),
    )(du, w_up, *deps)
    r = min(256, t)

    def body(acc_ref, dr2_ref, r1_ref, g_ref, dr1_ref, dr1b_ref, gg_ref, gb_ref):
        @pl.when(pl.program_id(0) == 0)
        def _():
            gg_ref[...] = jnp.zeros_like(gg_ref)
            gb_ref[...] = jnp.zeros_like(gb_ref)

        for r0, n in _row_chunks(r, 64):
            rs = slice(r0, r0 + n)
            dh1 = ALPHA * dr2_ref[rs, :] + acc_ref[rs, :]
            xhat, rstd = _ln_stats(r1_ref[rs, :])
            gg_ref[...] += jnp.sum(dh1 * xhat, axis=0, keepdims=True)
            gb_ref[...] += jnp.sum(dh1, axis=0, keepdims=True)
            dr1 = _ln_bwd(dh1 * g_ref[...], xhat, rstd)
            dr1_ref[rs, :] = dr1
            dr1b_ref[rs, :] = dr1.astype(BF16)

    row = pl.BlockSpec((r, D_MODEL), lambda i: (i, 0))
    vec = pl.BlockSpec((1, D_MODEL), lambda i: (0, 0))
    return pl.pallas_call(
        body, name="ln1_bwd", grid=(t // r,),
        in_specs=[row, row, row, vec],
        out_specs=[row, row, vec, vec],
        out_shape=[jax.ShapeDtypeStruct((t, D_MODEL), F32), jax.ShapeDtypeStruct((t, D_MODEL), BF16),
                   jax.ShapeDtypeStruct((1, D_MODEL), F32), jax.ShapeDtypeStruct((1, D_MODEL), F32)],
        compiler_params=_cparams(("arbitrary",)),
    )(dff, dr2, r1, ln_g)


def _dcat(dr1b, w_out):
    t = dr1b.shape[0]
    tm, tn = min(1024, t), 1024
    return _matmul(
        "dcat", dr1b, w_out, NT, (t // tm, D_MODEL // tn, 1),
        pl.BlockSpec((tm, D_MODEL), lambda i, j, k: (i, 0)),
        pl.BlockSpec((tn, D_MODEL), lambda i, j, k: (j, 0)),
        jax.ShapeDtypeStruct((t, D_MODEL), F32),
        pl.BlockSpec((tm, tn), lambda i, j, k: (i, j)), (tm, tn))


def _grad_w_out(cat, dr1b, dep=None):
    t = cat.shape[0]
    tm, tn = 1024, 1024
    return _matmul(
        "grad_w_out", cat, dr1b, TN, (D_MODEL // tm, D_MODEL // tn, 1),
        pl.BlockSpec((t, tm), lambda i, j, k: (0, i)),
        pl.BlockSpec((t, tn), lambda i, j, k: (0, j)),
        jax.ShapeDtypeStruct((N_DEV, D_MODEL // N_DEV, D_MODEL), F32),
        pl.BlockSpec((tm // (D_MODEL // N_DEV), D_MODEL // N_DEV, tn), lambda i, j, k: (i, 0, j)), (tm, tn), dep=dep)


def _gla_norm_bwd(dcat, o_raw, proj, norm_w, dep=None):
    t = o_raw.shape[0]
    r = min(512, t)

    deps, dep_specs = _dep_operand(dep)

    def body(d_ref, o_ref, g_ref, w_ref, *rest):
        do_ref, dg_ref, dw_ref = rest[len(deps):]

        @pl.when(pl.program_id(0) == 0)
        def _():
            dw_ref[...] = jnp.zeros_like(dw_ref)

        w = w_ref[...]
        dw = jnp.zeros((1, GLA_DV), F32)
        for h in range(GLA_HEADS):
            sl = slice(h * GLA_DV, (h + 1) * GLA_DV)
            o = o_ref[:, sl]
            g = g_ref[:, sl]
            d = d_ref[:, sl]
            rr = lax.rsqrt(jnp.mean(o * o, axis=-1, keepdims=True) + RMS_EPS)
            on = o * rr
            sg = jax.nn.sigmoid(g)
            sil = g * sg
            dg_ref[:, sl] = (d * on * w * (sg * (1.0 + g * (1.0 - sg)))).astype(BF16)
            dw = dw + jnp.sum(d * on * sil, axis=0, keepdims=True)
            don = d * w * sil
            do_ref[:, sl] = rr * (don - on * jnp.mean(don * on, axis=-1, keepdims=True))
        dw_ref[...] += dw

    return pl.pallas_call(
        body, name="gla_norm_bwd", grid=(t // r,),
        in_specs=[pl.BlockSpec((r, D_GLA), lambda i: (i, 0)),
                  pl.BlockSpec((r, D_GLA), lambda i: (i, 0)),
                  pl.BlockSpec((r, D_GLA), lambda i: (i, C_GG // D_GLA)),
                  pl.BlockSpec((1, GLA_DV), lambda i: (0, 0)), *dep_specs],
        out_specs=[pl.BlockSpec((r, D_GLA), lambda i: (i, 0)),
                   pl.BlockSpec((r, D_GLA), lambda i: (i, 0)),
                   pl.BlockSpec((1, GLA_DV), lambda i: (0, 0))],
        out_shape=[jax.ShapeDtypeStruct((t, D_GLA), F32), jax.ShapeDtypeStruct((t, D_GLA), BF16),
                   jax.ShapeDtypeStruct((1, GLA_DV), F32)],
        compiler_params=_cparams(("arbitrary",)),
    )(dcat, o_raw, proj, norm_w, *deps)


def _gla_bwd(proj, bloc, do_raw, states):
    t = proj.shape[0]
    r = min(256, t)
    ncb = r // GC
    nb = t // r
    scale = GLA_DK ** -0.5

    def body(q_ref, k_ref, v_ref, b_ref, do_ref, st_ref, dq_ref, dk_ref, dv_ref, db_ref, ds_scr):
        @pl.when(pl.program_id(1) == 0)
        def _():
            ds_scr[...] = jnp.zeros_like(ds_scr)

        rows = lax.broadcasted_iota(jnp.int32, (GC, 1), 0)

        def chunk(cc, carry):
            c = ncb - 1 - cc
            r0 = pl.multiple_of(c * GC, GC)
            q = q_ref[pl.ds(r0, GC), :] * scale
            k = k_ref[pl.ds(r0, GC), :]
            v = v_ref[pl.ds(r0, GC), :]
            b = b_ref[pl.ds(r0, GC), :]
            do = do_ref[pl.ds(r0, GC), :]
            st = st_ref[c]
            dsn = ds_scr[...]
            bl = b[GC - 1:GC, :]
            eb = jnp.exp(b)
            ekl = jnp.exp(bl - b)
            ebl = jnp.exp(bl)
            qh = q * eb
            kd = k * ekl
            dob = do.astype(BF16)
            dsb = dsn.astype(BF16)
            dqh = _dot(dob, st.astype(BF16), NN)
            dkd = _dot(v.astype(BF16), dsb, NN)
            dv = _dot(kd.astype(BF16), dsb, NT)
            dq_i = jnp.zeros((GC, GLA_DK), F32)
            dk_i = jnp.zeros((GC, GLA_DK), F32)
            dk_x = jnp.zeros((GC, GLA_DK), F32)
            kr = _bf16_round(k)
            vr = _bf16_round(v)
            dor = _bf16_round(do)
            for i in range(GC):
                w = jnp.exp(jnp.where(rows <= i, b[i:i + 1, :] - b, NEG))
                qw = q[i:i + 1, :] * w
                qwr = _bf16_round(qw)
                a = _bf16_round(jnp.sum(qwr * kr, axis=-1, keepdims=True))
                da = jnp.sum(vr * dor[i:i + 1, :], axis=-1, keepdims=True)
                dv = dv + a * dor[i:i + 1, :]
                dk_x = dk_x + da * qw
                dk_i = dk_i + _bf16_round(da) * qwr
                dq_i = jnp.where(rows == i, jnp.sum(da * (w * k), axis=0, keepdims=True), dq_i)
            dqs = dqh * eb + dq_i
            dk = dkd * ekl + dk_i
            db_last = jnp.sum(dkd * kd, axis=0, keepdims=True) + ebl * jnp.sum(dsn * st, axis=0, keepdims=True)
            db = q * dqs - k * (dkd * ekl + dk_x) + jnp.where(rows == GC - 1, db_last, 0.0)
            dq_ref[pl.ds(r0, GC), :] = (dqs * scale).astype(BF16)
            dk_ref[pl.ds(r0, GC), :] = dk.astype(BF16)
            dv_ref[pl.ds(r0, GC), :] = dv.astype(BF16)
            db_ref[pl.ds(r0, GC), :] = db
            ds_scr[...] = dsn * ebl + _dot(dob, qh.astype(BF16), TN)
            return carry

        lax.fori_loop(0, ncb, chunk, 0)

    rev = lambda i: nb - 1 - i
    return pl.pallas_call(
        body, name="gla_bwd", grid=(GLA_HEADS, nb),
        in_specs=[pl.BlockSpec((r, GLA_DK), lambda h, i: (rev(i), C_QG // GLA_DK + h)),
                  pl.BlockSpec((r, GLA_DK), lambda h, i: (rev(i), C_KG // GLA_DK + h)),
                  pl.BlockSpec((r, GLA_DV), lambda h, i: (rev(i), C_VG // GLA_DV + h)),
                  pl.BlockSpec((r, GLA_DK), lambda h, i: (rev(i), h)),
                  pl.BlockSpec((r, GLA_DV), lambda h, i: (rev(i), h)),
                  pl.BlockSpec((ncb, None, GLA_DV, GLA_DK), lambda h, i: (rev(i), h, 0, 0))],
        out_specs=[pl.BlockSpec((r, GLA_DK), lambda h, i: (rev(i), h)),
                   pl.BlockSpec((r, GLA_DK), lambda h, i: (rev(i), h)),
                   pl.BlockSpec((r, GLA_DV), lambda h, i: (rev(i), h)),
                   pl.BlockSpec((r, GLA_DK), lambda h, i: (rev(i), h))],
        out_shape=[jax.ShapeDtypeStruct((t, GLA_KW), BF16), jax.ShapeDtypeStruct((t, GLA_KW), BF16),
                   jax.ShapeDtypeStruct((t, D_GLA), BF16), jax.ShapeDtypeStruct((t, GLA_KW), F32)],
        scratch_shapes=[pltpu.VMEM((GLA_DV, GLA_DK), F32)],
        compiler_params=_cparams(("parallel", "arbitrary")),
    )(proj, proj, proj, bloc, do_raw, states)


def _gate_bwd(db, dgdz, proj, w2p, triu):
    t = db.shape[0]
    r = triu.shape[0]

    def body(db_ref, s_ref, lo_ref, w_ref, u_ref, dlo_ref, gw_ref, gb_ref):
        @pl.when(pl.program_id(0) == 0)
        def _():
            gw_ref[...] = jnp.zeros_like(gw_ref)
            gb_ref[...] = jnp.zeros_like(gb_ref)

        dz = _dot(u_ref[...], db_ref[...], NN, precision=lax.Precision.HIGHEST) * s_ref[...]
        dzb = dz.astype(BF16)
        gb_ref[...] += jnp.sum(dz, axis=0, keepdims=True)
        gw_ref[...] += _dot(lo_ref[...].astype(BF16), dzb, TN)
        dlo_ref[...] = _dot(dzb, w_ref[...], NT).astype(BF16)

    return pl.pallas_call(
        body, name="gate_bwd", grid=(t // r,),
        in_specs=[pl.BlockSpec((r, GLA_KW), lambda i: (i, 0)),
                  pl.BlockSpec((r, GLA_KW), lambda i: (i, 0)),
                  pl.BlockSpec((r, LANES), lambda i: (i, C_LO // LANES)),
                  pl.BlockSpec((LANES, GLA_KW), lambda i: (0, 0)),
                  pl.BlockSpec((r, r), lambda i: (0, 0))],
        out_specs=[pl.BlockSpec((r, LANES), lambda i: (i, 0)),
                   pl.BlockSpec((LANES, GLA_KW), lambda i: (0, 0)),
                   pl.BlockSpec((1, GLA_KW), lambda i: (0, 0))],
        out_shape=[jax.ShapeDtypeStruct((t, LANES), BF16), jax.ShapeDtypeStruct((LANES, GLA_KW), F32),
                   jax.ShapeDtypeStruct((1, GLA_KW), F32)],
        compiler_params=_cparams(("arbitrary",)),
    )(db, dgdz, proj, w2p, triu)


def _swa_bwd(proj, dcat, sinks, dep=None):
    t = proj.shape[0]

    deps, dep_specs = _dep_operand(dep)

    def body(sink_ref, q_ref, kp_ref, kc_ref, vp_ref, vc_ref, d_ref, *rest):
        dq_ref, dk_ref, dv_ref, dsink_ref = rest[len(deps):]
        i = pl.program_id(0)

        @pl.when(i == 0)
        def _():
            dk_ref[...] = jnp.zeros_like(dk_ref)
            dv_ref[...] = jnp.zeros_like(dv_ref)
            dsink_ref[...] = jnp.zeros_like(dsink_ref)

        valid, lo = _swa_masks(i, SWA_GROUP)
        lane = lax.broadcasted_iota(jnp.int32, (1, LANES), 1)
        kb = jnp.concatenate([kp_ref[...], kc_ref[...]], axis=0)
        vb = jnp.concatenate([vp_ref[...], vc_ref[...]], axis=0)
        dsink = jnp.zeros((1, LANES), F32)
        folded_k, folded_v = [], []
        for kv in range(2):
            kdup = _dup_half(kb, lo, kv).astype(BF16)
            vdup = _dup_half(vb, lo, kv).astype(BF16)
            qm = _swa_stack(q_ref, 0, lo, kv)
            dom = _swa_stack(d_ref, 0, lo, kv)
            pn, psink = _swa_probs(qm, kdup, valid, _swa_sinks(sink_ref, kv))
            dpr = _dot(dom, vdup, NT)
            drow = jnp.sum(dpr * pn, axis=-1, keepdims=True)
            ds_col = psink * drow
            for h in range(SWA_GROUP):
                dsink = dsink + jnp.where(lane == SWA_GROUP * kv + h,
                                          -jnp.sum(ds_col[h * WINDOW:(h + 1) * WINDOW, :]), 0.0)
            dsb = (pn * (dpr - drow) * (SWA_DH ** -0.5)).astype(BF16)
            dq_all = _dot(dsb, kdup, NN)
            for p in range(4):
                c0 = LANES * (4 * kv + p)
                dq_ref[:, c0:c0 + LANES] = _swa_unstack(dq_all, lo, p).astype(BF16)
            dkd = _dot(dsb, qm, TN)
            dvd = _dot(pn.astype(BF16), dom, TN)
            folded_k.append(dkd + pltpu.roll(dkd, SWA_DH, axis=1))
            folded_v.append(dvd + pltpu.roll(dvd, SWA_DH, axis=1))
        dkb = jnp.where(lo, folded_k[0], folded_k[1])
        dvb = jnp.where(lo, folded_v[0], folded_v[1])
        dsink_ref[...] += dsink
        cur = pl.ds(pl.multiple_of(i * WINDOW, WINDOW), WINDOW)
        dk_ref[cur, :] += dkb[WINDOW:, :]
        dv_ref[cur, :] += dvb[WINDOW:, :]

        @pl.when(i > 0)
        def _():
            prev = pl.ds(pl.multiple_of((i - 1) * WINDOW, WINDOW), WINDOW)
            dk_ref[prev, :] += dkb[:WINDOW, :]
            dv_ref[prev, :] += dvb[:WINDOW, :]

    kvspec = lambda col, prev: pl.BlockSpec(
        (WINDOW, LANES), (lambda i: (jnp.maximum(i - 1, 0), col)) if prev else (lambda i: (i, col)))
    full = pl.BlockSpec((t, LANES), lambda i: (0, 0))
    return pl.pallas_call(
        body, name="swa_bwd", grid=(t // WINDOW,),
        in_specs=[pl.BlockSpec(memory_space=pltpu.SMEM),
                  pl.BlockSpec((WINDOW, D_SWA), lambda i: (i, C_QS // D_SWA)),
                  kvspec(C_KS // LANES, True), kvspec(C_KS // LANES, False),
                  kvspec(C_VS // LANES, True), kvspec(C_VS // LANES, False),
                  pl.BlockSpec((WINDOW, D_SWA), lambda i: (i, 1)), *dep_specs],
        out_specs=[pl.BlockSpec((WINDOW, D_SWA), lambda i: (i, 0)), full, full,
                   pl.BlockSpec((1, LANES), lambda i: (0, 0))],
        out_shape=[jax.ShapeDtypeStruct((t, D_SWA), BF16), jax.ShapeDtypeStruct((t, LANES), F32),
                   jax.ShapeDtypeStruct((t, LANES), F32), jax.ShapeDtypeStruct((1, LANES), F32)],
        compiler_params=_cparams(("arbitrary",)),
    )(sinks, proj, proj, proj, proj, proj, dcat, *deps)


def _grad_w_in(xb, dproj):
    t = xb.shape[0]
    tm, tn = 896, 1024
    return _matmul(
        "grad_w_in", dproj, xb, TN, (D_INP // tm, D_MODEL // tn, 1),
        pl.BlockSpec((t, tm), lambda i, j, k: (0, i)),
        pl.BlockSpec((t, tn), lambda i, j, k: (0, j)),
        jax.ShapeDtypeStruct((D_IN, D_MODEL), F32),
        pl.BlockSpec((tm, tn), lambda i, j, k: (i, j)), (tm, tn))


def _grad_x(dproj, w_in_t, dr1, dep=None):
    t = dproj.shape[0]
    tm, tn = min(512, t), 1024

    def epilogue(acc_ref, extra, outs):
        for r0, n in _row_chunks(tm):
            rs = slice(r0, r0 + n)
            outs[0][rs, :] = ALPHA * extra[0][rs, :] + acc_ref[rs, :]

    blk = pl.BlockSpec((tm, tn), lambda j, i, k: (i, j))
    return _matmul(
        "grad_x", dproj, w_in_t, NN, (D_MODEL // tn, t // tm, 1),
        pl.BlockSpec((tm, D_INP), lambda j, i, k: (i, 0)),
        pl.BlockSpec((D_INP, tn), lambda j, i, k: (0, j)),
        jax.ShapeDtypeStruct((t, D_MODEL), F32), blk, (tm, tn),
        extra=(dr1,), extra_specs=(blk,), epilogue=epilogue, dep=dep)


def _place():
    x, y, c = lax.axis_index("x"), lax.axis_index("y"), lax.axis_index("c")
    chips = [(1 - x, y), (x, 1 - y), (1 - x, 1 - y)]
    return x, y, c, chips


def _plan_gather_out(src, land, x, y, c, chips):
    me = 4 * x + 2 * y + c
    return [(src, land.at[me], to) for to in [(x, y, 1 - c)] + [(px, py, c) for px, py in chips]]


def _plan_gather_forward(src, land, x, y, c, chips):
    return [(land.at[4 * px + 2 * py + c], land.at[4 * px + 2 * py + c], (x, y, 1 - c)) for px, py in chips]


def _plan_sibling(src, land, x, y, c, chips):
    return [(src.at[2 * q + (1 - c)], land.at[q], (x, y, 1 - c)) for q in range(4)]


def _plan_chips(src, land, x, y, c, chips):
    return [(src.at[j], land.at[j], (px, py, c)) for j, (px, py) in enumerate(chips)]


_PLAN_COPIES = {_plan_gather_out: 4, _plan_gather_forward: 3, _plan_sibling: 4, _plan_chips: 3}
_HBM = pl.BlockSpec(memory_space=pltpu.HBM)
_SEM = pl.BlockSpec(memory_space=pltpu.SEMAPHORE)
_EFFECT = pltpu.SideEffectType.DATAFLOW_SIDE_EFFECTING


def _hbm(a):
    return pltpu.with_memory_space_constraint(a, pltpu.HBM)


def _plan_descriptors(plans, srcs, lands, send, recv):
    x, y, c, chips = _place()
    cps = []
    for plan, src, land in zip(plans, srcs, lands):
        for s_ref, d_ref, to in plan(src, land, x, y, c, chips):
            k = len(cps)
            cps.append(pltpu.make_async_remote_copy(src_ref=s_ref, dst_ref=d_ref, send_sem=send.at[k],
                                                    recv_sem=recv.at[k], device_id=to, device_id_type=MESH))
    return cps


def _copies_start(name, plans, srcs, lands, after=None):
    has_src = [s is not None for s in srcs]
    arrays = [s for s in srcs if s is not None] + list(lands)
    n_src = sum(has_src)
    n_cp = sum(_PLAN_COPIES[p] for p in plans)
    afters = [] if after is None else [after]

    def body(*refs):
        ins = refs[:len(arrays)]
        send, recv = refs[len(arrays) + len(afters)], refs[len(arrays) + len(afters) + 1]
        token = refs[-1]
        it = iter(ins[:n_src])
        src_refs = [next(it) if h else None for h in has_src]
        for cp in _plan_descriptors(plans, src_refs, ins[n_src:], send, recv):
            cp.start()
        token[...] = jnp.zeros_like(token)

    outs = pl.pallas_call(
        body, name=name,
        in_specs=[_HBM] * len(arrays) + [pl.BlockSpec(memory_space=pl.ANY)] * len(afters),
        out_specs=(_SEM, _SEM, *[_HBM] * len(arrays), pl.BlockSpec(memory_space=pltpu.VMEM)),
        out_shape=(pltpu.SemaphoreType.DMA((n_cp,)), pltpu.SemaphoreType.DMA((n_cp,)),
                   *[pltpu.HBM(a.shape, a.dtype) for a in arrays], jax.ShapeDtypeStruct((8, LANES), F32)),
        input_output_aliases={i: 2 + i for i in range(len(arrays))},
        compiler_params=pltpu.CompilerParams(has_side_effects=_EFFECT),
    )(*[_hbm(a) for a in arrays], *afters)
    send, recv = outs[0], outs[1]
    thru = list(outs[2:-1])
    it = iter(thru[:n_src])
    return send, recv, [next(it) if h else None for h in has_src], thru[n_src:], outs[-1]


def _copies_wait(name, plans, started, after):
    send, recv, srcs, lands, _ = started
    has_src = [s is not None for s in srcs]
    arrays = [s for s in srcs if s is not None] + list(lands)
    n_src = sum(has_src)

    def body(*refs):
        ins = refs[:len(arrays)]
        send_ref, recv_ref = refs[len(arrays)], refs[len(arrays) + 1]
        it = iter(ins[:n_src])
        src_refs = [next(it) if h else None for h in has_src]
        for cp in _plan_descriptors(plans, src_refs, ins[n_src:], send_ref, recv_ref):
            cp.wait_send()
            cp.wait_recv()

    outs = pl.pallas_call(
        body, name=name,
        in_specs=[_HBM] * len(arrays) + [_SEM, _SEM, pl.BlockSpec(memory_space=pl.ANY)],
        out_specs=tuple([_HBM] * len(arrays)),
        out_shape=tuple(pltpu.HBM(a.shape, a.dtype) for a in arrays),
        input_output_aliases={i: i for i in range(len(arrays))},
        compiler_params=pltpu.CompilerParams(has_side_effects=_EFFECT),
    )(*arrays, send, recv, after)
    return list(outs[:n_src]), list(outs[n_src:])


def _shard_tiles(rows, cols, tr):
    if rows % tr == 0:
        return (tr, cols), rows // tr, lambda r: (r, 0)
    tc = 2 * LANES
    return (rows, tc), cols // tc, lambda r: (0, r)


def _pair_sum(name, grad, from_sibling, blocks):
    _, rows, cols = grad.shape
    (br_, bc), steps, at = _shard_tiles(rows, cols, 256)

    def body(blk_ref, g_ref, s_ref, o_ref):
        o_ref[...] = (g_ref[...] + s_ref[...]).astype(BF16)

    return pl.pallas_call(
        body, name=name,
        grid_spec=pltpu.PrefetchScalarGridSpec(
            num_scalar_prefetch=1, grid=(3, steps),
            in_specs=[pl.BlockSpec((None, br_, bc), lambda j, r, br: (br[j], *at(r))),
                      pl.BlockSpec((None, br_, bc), lambda j, r, br: (br[3 + j], *at(r)))],
            out_specs=pl.BlockSpec((None, br_, bc), lambda j, r, br: (j, *at(r)))),
        out_shape=jax.ShapeDtypeStruct((3, rows, cols), BF16),
        compiler_params=_cparams(("parallel", "parallel")),
    )(blocks, grad, from_sibling)


def _adam_math(g, w, m, v):
    m2 = ADAM_B1 * m + (1.0 - ADAM_B1) * g
    v2 = ADAM_B2 * v + (1.0 - ADAM_B2) * (g * g)
    m_hat = m2 / (1.0 - ADAM_B1 ** ADAM_STEP)
    v_hat = v2 / (1.0 - ADAM_B2 ** ADAM_STEP)
    delta = -ADAM_LR * (m_hat / (jnp.sqrt(v_hat) + ADAM_EPS) + ADAM_WD * w)
    return delta, m2, v2


def _sum_adam(name, grad, from_sibling, from_chips, own, w, m, v, dep=None):
    rows, cols = w.shape
    (br_, bc), steps, at = _shard_tiles(rows, cols, 128)

    deps, dep_specs = _dep_operand(dep)

    def body(own_ref, p_ref, s_ref, r_ref, w_ref, m_ref, v_ref, *rest):
        g_out, d_out, m_out, v_out = rest[len(deps):]
        g = p_ref[...] + s_ref[...]
        for j in range(3):
            g = g + r_ref[j].astype(F32)
        d, m2, v2 = _adam_math(g, w_ref[...], m_ref[...], v_ref[...])
        g_out[...] = g
        d_out[...] = d
        m_out[...] = m2
        v_out[...] = v2

    blk = pl.BlockSpec((br_, bc), lambda r, cr: at(r))
    shp = jax.ShapeDtypeStruct((rows, cols), F32)
    return pl.pallas_call(
        body, name=name,
        grid_spec=pltpu.PrefetchScalarGridSpec(
            num_scalar_prefetch=1, grid=(steps,),
            in_specs=[pl.BlockSpec((None, br_, bc), lambda r, cr: (cr[0], *at(r))),
                      pl.BlockSpec((None, br_, bc), lambda r, cr: (cr[1], *at(r))),
                      pl.BlockSpec((3, br_, bc), lambda r, cr: (0, *at(r))),
                      blk, blk, blk, *dep_specs],
            out_specs=[blk, blk, blk, blk]),
        out_shape=[shp, shp, shp, shp],
        compiler_params=_cparams(("parallel",)),
    )(own, grad, from_sibling, from_chips, w, m, v, *deps)


def _adam_small(name, g, w, m, v):
    def body(g_ref, w_ref, m_ref, v_ref, d_out, m_out, v_out):
        d, m2, v2 = _adam_math(g_ref[...], w_ref[...], m_ref[...], v_ref[...])
        d_out[...] = d
        m_out[...] = m2
        v_out[...] = v2

    shp = jax.ShapeDtypeStruct(w.shape, F32)
    return pl.pallas_call(body, name=name, out_shape=[shp, shp, shp])(g, w, m, v)


def _all_reduce_small(pack, dep=None):
    rows = pack.shape[0]
    deps = () if dep is None else (dep,)

    def body(in_ref, *rest):
        out_ref, slots, send, recv = rest[len(deps):]
        x, y, c, _ = _place()
        me = 4 * x + 2 * y + c
        slots[me] = in_ref[...]
        cps = []
        for k in range(1, N_DEV):
            dx, dy, dc = (k >> 2) & 1, (k >> 1) & 1, k & 1
            to = (jnp.bitwise_xor(x, dx), jnp.bitwise_xor(y, dy), jnp.bitwise_xor(c, dc))
            cps.append(pltpu.make_async_remote_copy(
                src_ref=in_ref, dst_ref=slots.at[me], send_sem=send.at[k - 1], recv_sem=recv.at[k - 1],
                device_id=to, device_id_type=MESH))
        for cp in cps:
            cp.start()
        for cp in cps:
            cp.wait()
        acc = slots[0]
        for d in range(1, N_DEV):
            acc = acc + slots[d]
        out_ref[...] = acc

    return pl.pallas_call(
        body, name="all_reduce_small",
        in_specs=[pl.BlockSpec(memory_space=pltpu.VMEM)] * (1 + len(deps)),
        out_specs=pl.BlockSpec(memory_space=pltpu.VMEM),
        out_shape=jax.ShapeDtypeStruct((rows, LANES), F32),
        scratch_shapes=[pltpu.VMEM((N_DEV, rows, LANES), F32),
                        pltpu.SemaphoreType.DMA((N_DEV - 1,)), pltpu.SemaphoreType.DMA((N_DEV - 1,))],
    )(pack, *deps)


def _rows128(a):
    flat = a.reshape(-1)
    padn = (-flat.shape[0]) % (8 * LANES)
    if padn:
        flat = jnp.concatenate([flat, jnp.zeros((padn,), flat.dtype)])
    return flat.reshape(-1, LANES)


def kernel(x, w_in, w_gk2, b_gk, gla_norm_w, swa_sinks, w_out, ln1_g, ln1_b, w_up, w_down, ln2_g, ln2_b, loss_target, m_w_in, m_w_gk2, m_b_gk, m_gla_norm_w, m_swa_sinks, m_w_out, m_ln1_g, m_ln1_b, m_w_up, m_w_down, m_ln2_g, m_ln2_b, v_w_in, v_w_gk2, v_b_gk, v_gla_norm_w, v_swa_sinks, v_w_out, v_ln1_g, v_ln1_b, v_w_up, v_w_down, v_ln2_g, v_ln2_b):
    xc, yc, cc = lax.axis_index("x"), lax.axis_index("y"), lax.axis_index("c")
    me = 4 * xc + 2 * yc + cc

    x2 = x[0]
    t = x2.shape[0]
    target = loss_target[0]

    def land_of(sh):
        return lax.dynamic_update_index_in_dim(lax.empty((N_DEV,) + sh.shape, BF16), sh, me, 0)

    out_plans = [_plan_gather_out] * 3
    fwd_plans = [_plan_gather_forward] * 3
    in_shards = [w_in[0].T.astype(BF16), w_gk2[0].astype(BF16)]
    ag0 = _copies_start("gather_in_start", out_plans[:2], in_shards, [land_of(sh) for sh in in_shards])
    zero = ag0[4][0, 0]
    xb = (x2 + zero).astype(BF16)
    shards = [(w[0] + zero).astype(BF16) for w in (w_out, w_up, w_down)]
    lands = [land_of(sh) for sh in shards]
    _, in_lands = _copies_wait("gather_in_wait", out_plans[:2], ag0, lands[2])
    ag0f = _copies_start("gather_in_forward_start", fwd_plans[:2], [None] * 2, in_lands)
    _, first = _copies_wait("gather_in_forward_wait", fwd_plans[:2], ag0f, ag0f[4])
    w_in_rows = first[0].reshape(D_IN, D_MODEL)
    w_in_t = _to_padded_rows(w_in_rows)
    w_in_o = jnp.concatenate([w_in_rows, jnp.zeros((D_INP - D_IN, D_MODEL), BF16)], axis=0)
    w2 = first[1].transpose(1, 0, 2).reshape(GATE_RANK, GLA_KW)
    w2p = jnp.concatenate([w2, jnp.zeros((LANES - GATE_RANK, GLA_KW), BF16)], axis=0)
    ag1 = _copies_start("gather_out_start", out_plans, shards, lands, first[0])

    proj = _proj(xb, w_in_t, dep=ag1[4])
    rt = min(256, t)
    ii = jnp.arange(rt)
    tri = ((ii[:, None] // GC == ii[None, :] // GC) & (ii[None, :] <= ii[:, None])).astype(F32)
    bloc, dgdz = _gate_fwd(proj, w2p, b_gk, tri)
    o_raw, states = _gla_fwd(proj, bloc)
    gla_out = _gla_norm_fwd(o_raw, proj, gla_norm_w)
    cat = _swa_fwd(proj, swa_sinks[0], gla_out)
    _, lands = _copies_wait("gather_out_wait", out_plans, ag1, cat)
    ag2 = _copies_start("gather_forward_out_start", fwd_plans[:1], [None], lands[:1])
    _, (w_out_g,) = _copies_wait("gather_forward_out_wait", fwd_plans[:1], ag2, ag2[4])
    ag3 = _copies_start("gather_forward_mlp_start", fwd_plans[:2], [None] * 2, lands[1:], w_out_g)
    w_out_f = w_out_g.reshape(D_MODEL, D_MODEL)
    r1, h1, h1b = _mix_ln1(cat, w_out_f, x2, ln1_g, ln1_b, dep=ag3[4])
    _, (w_up_f, w_down_g) = _copies_wait("gather_forward_mlp_wait", fwd_plans[:2], ag3, h1b)
    w_down_f = w_down_g.reshape(D_FF, D_MODEL)
    a_act, hdn = _mlp_up(h1b, w_up_f)
    dr2, dr2b, g_ln2_g, g_ln2_b, loss_part = _mlp_down_loss(hdn, w_down_f, h1, target, ln2_g, ln2_b)

    others = [2 * (1 - xc) + yc, 2 * xc + (1 - yc), 2 * (1 - xc) + (1 - yc)]
    blocks = jnp.stack([2 * q + cc for q in others] + others).astype(jnp.int32)
    own = jnp.stack([me, 2 * xc + yc]).astype(jnp.int32)
    wmv = dict(w_in=[a[0].T for a in (w_in, m_w_in, v_w_in)], w_out=[a[0] for a in (w_out, m_w_out, v_w_out)],
               w_up=[a[0] for a in (w_up, m_w_up, v_w_up)], w_down=[a[0] for a in (w_down, m_w_down, v_w_down)])
    big = {}

    def sib_land(g):
        return lax.empty((4,) + g.shape[1:], F32)

    def chip_land(g):
        return lax.empty((3,) + g.shape[1:], BF16)

    def finish(nm, g, from_sib, from_chips, dep=None):
        outs = _sum_adam("sum_adam_" + nm, g, from_sib, from_chips, own, *wmv[nm], dep=dep)
        big[nm] = [(o.T if nm == "w_in" else o)[None] for o in outs]

    du = _mlp_down_bwd(dr2b, w_down_f, a_act)
    g_down = _grad_w_down(hdn, dr2b)
    sa_down = _copies_start("sibling_down_start", [_plan_sibling], [g_down], [sib_land(g_down)])
    g_up = _grad_w_up(h1b, du, dep=sa_down[4])
    (g_down,), (fs_down,) = _copies_wait("sibling_down_wait", [_plan_sibling], sa_down, g_up)
    p_down = _pair_sum("pair_sum_w_down", g_down, fs_down, blocks)
    sb_down = _copies_start("chips_down_sibling_up_start", [_plan_chips, _plan_sibling], [p_down, g_up],
                            [chip_land(g_down), sib_land(g_up)])
    dr1, dr1b, g_ln1_g, g_ln1_b = _mlp_up_bwd_ln1(du, w_up_f, dr2, r1, ln1_g, dep=sb_down[4])
    dcat = _dcat(dr1b, w_out_f)
    (_, g_up), (fc_down, fs_up) = _copies_wait("chips_down_sibling_up_wait", [_plan_chips, _plan_sibling], sb_down,
                                               dcat)
    p_up = _pair_sum("pair_sum_w_up", g_up, fs_up, blocks)
    sb_up = _copies_start("chips_up_start", [_plan_chips], [p_up], [chip_land(g_up)])
    do_raw, dg_g, g_norm_w = _gla_norm_bwd(dcat, o_raw, proj, gla_norm_w, dep=sb_up[4])
    dq_g, dk_g, dv_g, dgk = _gla_bwd(proj, bloc, do_raw, states)
    _, (fc_up,) = _copies_wait("chips_up_wait", [_plan_chips], sb_up, dgk)
    dlo, gw2_p, g_b_gk = _gate_bwd(dgk, dgdz, proj, w2p, tri.T)
    dq_s, dk_s, dv_s, g_sinks = _swa_bwd(proj, dcat, swa_sinks[0])
    dproj = jnp.concatenate([dq_g, dk_g, dv_g, dg_g, dlo[:, :GATE_RANK], dq_s, dk_s.astype(BF16), dv_s.astype(BF16),
                             jnp.zeros((t, D_INP - D_IN), BF16)], axis=-1)
    gw_in_t = _grad_w_in(xb, dproj)
    g_in = gw_in_t.reshape(N_DEV, D_IN // N_DEV, D_MODEL)
    sa_in = _copies_start("sibling_in_start", [_plan_sibling], [g_in], [sib_land(g_in)])
    g_out = _grad_w_out(cat, dr1b, dep=sa_in[4])
    (g_in,), (fs_in,) = _copies_wait("sibling_in_wait", [_plan_sibling], sa_in, g_out)
    p_in = _pair_sum("pair_sum_w_in", g_in, fs_in, blocks)
    sb_in = _copies_start("chips_in_sibling_out_start", [_plan_chips, _plan_sibling], [p_in, g_out],
                          [chip_land(g_in), sib_land(g_out)])
    grad_x = _grad_x(dproj, w_in_o, dr1, dep=sb_in[4])
    (_, g_out), (fc_in, fs_out) = _copies_wait("chips_in_sibling_out_wait", [_plan_chips, _plan_sibling], sb_in,
                                               grad_x)
    p_out = _pair_sum("pair_sum_w_out", g_out, fs_out, blocks)
    sb_out = _copies_start("chips_out_start", [_plan_chips], [p_out], [chip_land(g_out)])
    def done(nm):
        return big[nm][0][0, :8, :LANES]

    finish("w_in", g_in, fs_in, fc_in, dep=sb_out[4])
    finish("w_down", g_down, fs_down, fc_down, dep=done("w_in"))
    finish("w_up", g_up, fs_up, fc_up, dep=done("w_down"))

    pieces = [loss_part, g_b_gk, g_norm_w, g_sinks[:, :SWA_HEADS], g_ln1_g, g_ln1_b, g_ln2_g, g_ln2_b,
              gw2_p[:GATE_RANK]]
    pack = jnp.concatenate([_rows128(p) for p in pieces], axis=0)
    tot = _all_reduce_small(pack, dep=done("w_up"))
    sizes = [p.size for p in pieces]
    offs = [0]
    for p in pieces:
        offs.append(offs[-1] + _rows128(p).shape[0])
    unpack = lambda i, shape: tot[offs[i]:offs[i + 1]].reshape(-1)[:sizes[i]].reshape(shape)
    loss = tot[0, 0]
    small_names = ["b_gk", "gla_norm_w", "swa_sinks", "ln1_g", "ln1_b", "ln2_g", "ln2_b"]
    small_w = dict(b_gk=(b_gk, m_b_gk, v_b_gk), gla_norm_w=(gla_norm_w, m_gla_norm_w, v_gla_norm_w),
                   swa_sinks=(swa_sinks, m_swa_sinks, v_swa_sinks), ln1_g=(ln1_g, m_ln1_g, v_ln1_g),
                   ln1_b=(ln1_b, m_ln1_b, v_ln1_b), ln2_g=(ln2_g, m_ln2_g, v_ln2_g),
                   ln2_b=(ln2_b, m_ln2_b, v_ln2_b))
    small_g = {nm: unpack(1 + i, small_w[nm][0].shape) for i, nm in enumerate(small_names)}
    g_pack = jnp.concatenate([_rows128(small_g[nm]) for nm in small_names], axis=0)
    small_wmv = [jnp.concatenate([_rows128(small_w[nm][k]) for nm in small_names], axis=0) for k in range(3)]
    small_out = _adam_small("adam_replicated", g_pack, *small_wmv)
    srow = [0]
    for nm in small_names:
        srow.append(srow[-1] + _rows128(small_w[nm][0]).shape[0])
    small = {}
    for i, nm in enumerate(small_names):
        shape = small_w[nm][0].shape
        n = small_w[nm][0].size
        small[nm] = [small_g[nm]] + [o[srow[i]:srow[i + 1]].reshape(-1)[:n].reshape(shape) for o in small_out]

    gw2_full = unpack(8, (GATE_RANK, GLA_KW))
    gw2_loc = lax.dynamic_slice_in_dim(gw2_full, me * (GLA_KW // N_DEV), GLA_KW // N_DEV, axis=1)
    gk2_out = _adam_small("adam_w_gk2", gw2_loc, w_gk2[0], m_w_gk2[0], v_w_gk2[0])
    big["w_gk2"] = [gw2_loc[None]] + [o[None] for o in gk2_out]

    _, (fc_out,) = _copies_wait("chips_out_wait", [_plan_chips], sb_out, tot)
    finish("w_out", g_out, fs_out, fc_out)

    order = ["w_in", "w_gk2", "b_gk", "gla_norm_w", "swa_sinks", "w_out", "ln1_g", "ln1_b", "w_up", "w_down",
             "ln2_g", "ln2_b"]
    res = {**big, **small}
    outs = [loss, grad_x[None]]
    for k in range(4):
        outs += [res[nm][k] for nm in order]
    return tuple(outs)
```
